```python
import math
import jax, jax.numpy as jnp
from jax import lax
import numpy as np

D_MODEL = 1024
BATCH = 8
SEQ = 2048
DEPTH = 4

D_MIX = D_MODEL
ATTN_WIDTH = D_MIX // 2
HEAD_DIM = 64
N_HEADS = ATTN_WIDTH // HEAD_DIM
CONV_WIDTH = D_MIX - ATTN_WIDTH
CONV_GROUPS = CONV_WIDTH // HEAD_DIM
CONV_K = 31
D_IN = 3 * ATTN_WIDTH + N_HEADS + 2 * CONV_WIDTH
D_FF = int(math.ceil((8 * D_MODEL / 3) / 256) * 256)
Q_BLOCK = 128
N_MOD = 6
EPS = 1e-6

kernel_name = "fox_conformer_hymba_adaln_trunk"


def rms_norm(x, g):
    xf = x.astype(jnp.float32)
    y = xf * lax.rsqrt(jnp.mean(xf * xf, axis=-1, keepdims=True) + EPS)
    return (y * g.astype(jnp.float32)).astype(x.dtype)


def layer_norm(x, g, b):
    xf = x.astype(jnp.float32)
    mu = jnp.mean(xf, axis=-1, keepdims=True)
    var = jnp.mean(jnp.square(xf - mu), axis=-1, keepdims=True)
    y = (xf - mu) * lax.rsqrt(var + EPS)
    return (y * g.astype(jnp.float32) + b.astype(jnp.float32)).astype(x.dtype)


def fox_attention(q, k, v, log_f):
    B, S, H, Dh = q.shape
    nb = S // Q_BLOCK
    scale = Dh ** -0.5
    cum = jnp.cumsum(log_f, axis=1)
    cum_k = jnp.transpose(cum, (0, 2, 1))[:, :, None, :]
    q_blocks = jnp.transpose(q.reshape(B, nb, Q_BLOCK, H, Dh), (1, 0, 2, 3, 4))
    c_blocks = jnp.transpose(cum.reshape(B, nb, Q_BLOCK, H), (1, 0, 3, 2))
    k_pos = jnp.arange(S, dtype=jnp.int32)

    def one_block(args):
        i, q_blk, c_blk = args
        s = jnp.einsum('bqhd,bkhd->bhqk', q_blk, k).astype(jnp.float32) * scale
        s = s + c_blk[..., None] - cum_k
        q_pos = i * Q_BLOCK + jnp.arange(Q_BLOCK, dtype=jnp.int32)
        mask = k_pos[None, :] <= q_pos[:, None]
        s = jnp.where(mask[None, None], s, -jnp.inf)
        p = jax.nn.softmax(s, axis=-1)
        return jnp.einsum('bhqk,bkhd->bqhd', p.astype(v.dtype), v)

    outs = lax.map(one_block, (jnp.arange(nb, dtype=jnp.int32), q_blocks, c_blocks))
    return jnp.transpose(outs, (1, 0, 2, 3, 4)).reshape(B, S, H * Dh)


def causal_depthwise_conv(u, w, b):
    C = u.shape[-1]
    out = lax.conv_general_dilated(
        u, w[:, None, :].astype(u.dtype), window_strides=(1,), padding=((CONV_K - 1, 0),),
        dimension_numbers=('NWC', 'WIO', 'NWC'), feature_group_count=C)
    return out + b.astype(u.dtype)


def hybrid_mixer(h, w_in, b_f, conv_w, conv_b, conv_ln_g, conv_ln_b, w_o):
    B, S, _ = h.shape
    proj = h @ w_in
    splits = [ATTN_WIDTH, 2 * ATTN_WIDTH, 3 * ATTN_WIDTH, 3 * ATTN_WIDTH + N_HEADS,
              3 * ATTN_WIDTH + N_HEADS + CONV_WIDTH]
    q, k, v, f_logit, conv_val, conv_gate = jnp.split(proj, splits, axis=-1)
    q = q.reshape(B, S, N_HEADS, HEAD_DIM)
    k = k.reshape(B, S, N_HEADS, HEAD_DIM)
    v = v.reshape(B, S, N_HEADS, HEAD_DIM)
    log_f = jax.nn.log_sigmoid((f_logit + b_f).astype(jnp.float32))
    attn = fox_attention(q, k, v, log_f)
    u = conv_val * jax.nn.sigmoid(conv_gate)
    u = causal_depthwise_conv(u, conv_w, conv_b)
    u = jax.nn.silu(layer_norm(u, conv_ln_g, conv_ln_b))
    return jnp.concatenate([attn, u], axis=-1) @ w_o


def swiglu_ffn(h, w_ffn_in, w_ffn_out):
    g, u = jnp.split(h @ w_ffn_in, 2, axis=-1)
    return (jax.nn.silu(g) * u) @ w_ffn_out


def _fwd_setup_inputs(seed: int = 0) -> dict:
    key = jax.random.key(seed)
    ks = jax.random.split(key, 24)
    f32 = jnp.float32
    L, D = DEPTH, D_MODEL

    def nrm(k, shape, s):
        return jax.random.normal(k, shape, f32) * s

    x = jax.random.normal(ks[0], (BATCH, SEQ, D), f32)
    c = jax.random.normal(ks[1], (BATCH, D), f32)
    w_in = jnp.concatenate([
        nrm(ks[2], (L, D, 3 * ATTN_WIDTH), D ** -0.5),
        nrm(ks[3], (L, D, N_HEADS), 0.5 * D ** -0.5),
        nrm(ks[4], (L, D, 2 * CONV_WIDTH), D ** -0.5),
    ], axis=-1)
    b_f = 2.5 + 0.5 * jax.random.normal(ks[5], (L, N_HEADS), f32)
    conv_w = nrm(ks[6], (L, CONV_K, CONV_WIDTH), CONV_K ** -0.5)
    conv_b = nrm(ks[7], (L, CONV_WIDTH), 0.01)
    conv_ln_g = 1.0 + nrm(ks[8], (L, CONV_WIDTH), 0.05)
    conv_ln_b = nrm(ks[9], (L, CONV_WIDTH), 0.01)
    w_o = nrm(ks[10], (L, D_MIX, D), D_MIX ** -0.5)
    w_ffn_in = nrm(ks[11], (L, D, 2 * D_FF), D ** -0.5)
    w_ffn_out = nrm(ks[12], (L, D_FF, D), D_FF ** -0.5)
    mix_pre_g = 1.0 + nrm(ks[13], (L, D), 0.05)
    mix_post_g = 1.0 + nrm(ks[14], (L, D), 0.05)
    ffn_pre_g = 1.0 + nrm(ks[15], (L, D), 0.05)
    ffn_post_g = 1.0 + nrm(ks[16], (L, D), 0.05)
    ada_w = nrm(ks[17], (L, D, N_MOD * D), 0.5 * D ** -0.5)
    ada_b = nrm(ks[18], (L, N_MOD * D), 0.01)
    return {"x": x, "c": c, "w_in": w_in, "b_f": b_f, "conv_w": conv_w, "conv_b": conv_b,
            "conv_ln_g": conv_ln_g, "conv_ln_b": conv_ln_b, "w_o": w_o,
            "w_ffn_in": w_ffn_in, "w_ffn_out": w_ffn_out,
            "mix_pre_g": mix_pre_g, "mix_post_g": mix_post_g,
            "ffn_pre_g": ffn_pre_g, "ffn_post_g": ffn_post_g,
            "ada_w": ada_w, "ada_b": ada_b}


def _fwd_reference(x, c, w_in, b_f, conv_w, conv_b, conv_ln_g, conv_ln_b, w_o,
              w_ffn_in, w_ffn_out, mix_pre_g, mix_post_g, ffn_pre_g, ffn_post_g,
              ada_w, ada_b):
    c_act = jax.nn.silu(c)
    for i in range(DEPTH):
        mod = c_act @ ada_w[i] + ada_b[i]
        sh1, sc1, g1, sh2, sc2, g2 = [m[:, None, :] for m in jnp.split(mod, N_MOD, axis=-1)]
        h = rms_norm(x, mix_pre_g[i]) * (1.0 + sc1) + sh1
        y = hybrid_mixer(h, w_in[i], b_f[i], conv_w[i], conv_b[i], conv_ln_g[i], conv_ln_b[i], w_o[i])
        x = x + g1 * rms_norm(y, mix_post_g[i])
        h = rms_norm(x, ffn_pre_g[i]) * (1.0 + sc2) + sh2
        y = swiglu_ffn(h, w_ffn_in[i], w_ffn_out[i])
        x = x + g2 * rms_norm(y, ffn_post_g[i])
    return x


import jax as _jax
import jax.numpy as _jnp

TWIN_FORMAT = 'train_step'
FWD_PARAMS = ['x', 'c', 'w_in', 'b_f', 'conv_w', 'conv_b', 'conv_ln_g', 'conv_ln_b', 'w_o', 'w_ffn_in', 'w_ffn_out', 'mix_pre_g', 'mix_post_g', 'ffn_pre_g', 'ffn_post_g', 'ada_w', 'ada_b']
TWIN_WEIGHTS = ['w_in', 'b_f', 'conv_w', 'conv_b', 'conv_ln_g', 'conv_ln_b', 'w_o', 'w_ffn_in', 'w_ffn_out', 'mix_pre_g', 'mix_post_g', 'ffn_pre_g', 'ffn_post_g', 'ada_w', 'ada_b']
TWIN_DIFF_INPUT = 'x'
TWIN_INPUTS = ['x', 'c', 'w_in', 'b_f', 'conv_w', 'conv_b', 'conv_ln_g', 'conv_ln_b', 'w_o', 'w_ffn_in', 'w_ffn_out', 'mix_pre_g', 'mix_post_g', 'ffn_pre_g', 'ffn_post_g', 'ada_w', 'ada_b', 'loss_target', 'm_w_in', 'm_b_f', 'm_conv_w', 'm_conv_b', 'm_conv_ln_g', 'm_conv_ln_b', 'm_w_o', 'm_w_ffn_in', 'm_w_ffn_out', 'm_mix_pre_g', 'm_mix_post_g', 'm_ffn_pre_g', 'm_ffn_post_g', 'm_ada_w', 'm_ada_b', 'v_w_in', 'v_b_f', 'v_conv_w', 'v_conv_b', 'v_conv_ln_g', 'v_conv_ln_b', 'v_w_o', 'v_w_ffn_in', 'v_w_ffn_out', 'v_mix_pre_g', 'v_mix_post_g', 'v_ffn_pre_g', 'v_ffn_post_g', 'v_ada_w', 'v_ada_b']
TWIN_OUTPUTS = ['loss', 'grad_x', 'grad_w_in', 'grad_b_f', 'grad_conv_w', 'grad_conv_b', 'grad_conv_ln_g', 'grad_conv_ln_b', 'grad_w_o', 'grad_w_ffn_in', 'grad_w_ffn_out', 'grad_mix_pre_g', 'grad_mix_post_g', 'grad_ffn_pre_g', 'grad_ffn_post_g', 'grad_ada_w', 'grad_ada_b', 'delta_w_in', 'delta_b_f', 'delta_conv_w', 'delta_conv_b', 'delta_conv_ln_g', 'delta_conv_ln_b', 'delta_w_o', 'delta_w_ffn_in', 'delta_w_ffn_out', 'delta_mix_pre_g', 'delta_mix_post_g', 'delta_ffn_pre_g', 'delta_ffn_post_g', 'delta_ada_w', 'delta_ada_b', 'new_m_w_in', 'new_m_b_f', 'new_m_conv_w', 'new_m_conv_b', 'new_m_conv_ln_g', 'new_m_conv_ln_b', 'new_m_w_o', 'new_m_w_ffn_in', 'new_m_w_ffn_out', 'new_m_mix_pre_g', 'new_m_mix_post_g', 'new_m_ffn_pre_g', 'new_m_ffn_post_g', 'new_m_ada_w', 'new_m_ada_b', 'new_v_w_in', 'new_v_b_f', 'new_v_conv_w', 'new_v_conv_b', 'new_v_conv_ln_g', 'new_v_conv_ln_b', 'new_v_w_o', 'new_v_w_ffn_in', 'new_v_w_ffn_out', 'new_v_mix_pre_g', 'new_v_mix_post_g', 'new_v_ffn_pre_g', 'new_v_ffn_post_g', 'new_v_ada_w', 'new_v_ada_b']
TWIN_LEAF_KINDS = {'loss': 'loss', 'grad_x': 'grad_x', 'grad_w_in': 'grad_w', 'grad_b_f': 'grad_w', 'grad_conv_w': 'grad_w', 'grad_conv_b': 'grad_w', 'grad_conv_ln_g': 'grad_w', 'grad_conv_ln_b': 'grad_w', 'grad_w_o': 'grad_w', 'grad_w_ffn_in': 'grad_w', 'grad_w_ffn_out': 'grad_w', 'grad_mix_pre_g': 'grad_w', 'grad_mix_post_g': 'grad_w', 'grad_ffn_pre_g': 'grad_w', 'grad_ffn_post_g': 'grad_w', 'grad_ada_w': 'grad_w', 'grad_ada_b': 'grad_w', 'delta_w_in': 'delta_w', 'delta_b_f': 'delta_w', 'delta_conv_w': 'delta_w', 'delta_conv_b': 'delta_w', 'delta_conv_ln_g': 'delta_w', 'delta_conv_ln_b': 'delta_w', 'delta_w_o': 'delta_w', 'delta_w_ffn_in': 'delta_w', 'delta_w_ffn_out': 'delta_w', 'delta_mix_pre_g': 'delta_w', 'delta_mix_post_g': 'delta_w', 'delta_ffn_pre_g': 'delta_w', 'delta_ffn_post_g': 'delta_w', 'delta_ada_w': 'delta_w', 'delta_ada_b': 'delta_w', 'new_m_w_in': 'new_m', 'new_m_b_f': 'new_m', 'new_m_conv_w': 'new_m', 'new_m_conv_b': 'new_m', 'new_m_conv_ln_g': 'new_m', 'new_m_conv_ln_b': 'new_m', 'new_m_w_o': 'new_m', 'new_m_w_ffn_in': 'new_m', 'new_m_w_ffn_out': 'new_m', 'new_m_mix_pre_g': 'new_m', 'new_m_mix_post_g': 'new_m', 'new_m_ffn_pre_g': 'new_m', 'new_m_ffn_post_g': 'new_m', 'new_m_ada_w': 'new_m', 'new_m_ada_b': 'new_m', 'new_v_w_in': 'new_v', 'new_v_b_f': 'new_v', 'new_v_conv_w': 'new_v', 'new_v_conv_b': 'new_v', 'new_v_conv_ln_g': 'new_v', 'new_v_conv_ln_b': 'new_v', 'new_v_w_o': 'new_v', 'new_v_w_ffn_in': 'new_v', 'new_v_w_ffn_out': 'new_v', 'new_v_mix_pre_g': 'new_v', 'new_v_mix_post_g': 'new_v', 'new_v_ffn_pre_g': 'new_v', 'new_v_ffn_post_g': 'new_v', 'new_v_ada_w': 'new_v', 'new_v_ada_b': 'new_v'}


def _forward(args):
    return _fwd_reference(*[args[k] for k in FWD_PARAMS])


def _output_shape():
    out = _jax.eval_shape(lambda: _forward(_fwd_setup_inputs(0)))
    return out.shape, out.dtype

N_MICROBATCH = 1
ADAM_LR = 0.001
ADAM_B1 = 0.9
ADAM_B2 = 0.999
ADAM_EPS = 1e-08
ADAM_WD = 0.01
ADAM_STEP = 10
PER_EXAMPLE_BATCH_AXIS = {'x': 0, 'c': 0, 'loss_target': 0}
SHARED_INPUTS = []
_WEIGHT_DTYPES = {'w_in': _jnp.float32, 'b_f': _jnp.float32, 'conv_w': _jnp.float32, 'conv_b': _jnp.float32, 'conv_ln_g': _jnp.float32, 'conv_ln_b': _jnp.float32, 'w_o': _jnp.float32, 'w_ffn_in': _jnp.float32, 'w_ffn_out': _jnp.float32, 'mix_pre_g': _jnp.float32, 'mix_post_g': _jnp.float32, 'ffn_pre_g': _jnp.float32, 'ffn_post_g': _jnp.float32, 'ada_w': _jnp.float32, 'ada_b': _jnp.float32}
MOMENT_SCALE = {'w_in': 3.191320e-01, 'b_f': 3.170447e-01, 'conv_w': 3.007518e-01, 'conv_b': 1.581290e+00, 'conv_ln_g': 7.256742e-01, 'conv_ln_b': 9.770697e-01, 'w_o': 5.905601e-01, 'w_ffn_in': 8.753980e-02, 'w_ffn_out': 1.670707e-01, 'mix_pre_g': 2.325927e-01, 'mix_post_g': 2.014463e+00, 'ffn_pre_g': 1.492380e-01, 'ffn_post_g': 1.916232e+00, 'ada_w': 1.111751e+00, 'ada_b': 1.996910e+00}


def _to_microbatches(a, axis):
    t = _jnp.moveaxis(a, axis, 0)
    t = t.reshape((N_MICROBATCH, t.shape[0] // N_MICROBATCH) + t.shape[1:])
    return _jnp.moveaxis(t, 1, axis + 1)


def setup_inputs(seed: int = 0) -> dict:
    inp = _fwd_setup_inputs(seed)
    key = _jax.random.fold_in(_jax.random.key(seed), 7919)
    shape, _ = _output_shape()
    out = dict(inp)
    out["loss_target"] = _jax.random.normal(_jax.random.fold_in(key, 0), shape, _jnp.float32)
    for i, name in enumerate(TWIN_WEIGHTS):
        w = inp[name].astype(_jnp.float32)
        if MOMENT_SCALE is None:
            s = _jnp.sqrt(_jnp.mean(_jnp.square(w)) + 1e-30)
        else:
            s = MOMENT_SCALE[name]
        km, kv = _jax.random.split(_jax.random.fold_in(key, i + 1))
        out[name] = w
        out["m_" + name] = s * _jax.random.normal(km, w.shape, _jnp.float32)
        out["v_" + name] = (s * s) * _jax.random.uniform(kv, w.shape, _jnp.float32, 0.5, 1.5)
    if N_MICROBATCH > 1:
        for name, axis in PER_EXAMPLE_BATCH_AXIS.items():
            out[name] = _to_microbatches(out[name], axis)
    return {'x': out['x'], 'c': out['c'], 'w_in': out['w_in'], 'b_f': out['b_f'], 'conv_w': out['conv_w'], 'conv_b': out['conv_b'], 'conv_ln_g': out['conv_ln_g'], 'conv_ln_b': out['conv_ln_b'], 'w_o': out['w_o'], 'w_ffn_in': out['w_ffn_in'], 'w_ffn_out': out['w_ffn_out'], 'mix_pre_g': out['mix_pre_g'], 'mix_post_g': out['mix_post_g'], 'ffn_pre_g': out['ffn_pre_g'], 'ffn_post_g': out['ffn_post_g'], 'ada_w': out['ada_w'], 'ada_b': out['ada_b'], 'loss_target': out['loss_target'], 'm_w_in': out['m_w_in'], 'm_b_f': out['m_b_f'], 'm_conv_w': out['m_conv_w'], 'm_conv_b': out['m_conv_b'], 'm_conv_ln_g': out['m_conv_ln_g'], 'm_conv_ln_b': out['m_conv_ln_b'], 'm_w_o': out['m_w_o'], 'm_w_ffn_in': out['m_w_ffn_in'], 'm_w_ffn_out': out['m_w_ffn_out'], 'm_mix_pre_g': out['m_mix_pre_g'], 'm_mix_post_g': out['m_mix_post_g'], 'm_ffn_pre_g': out['m_ffn_pre_g'], 'm_ffn_post_g': out['m_ffn_post_g'], 'm_ada_w': out['m_ada_w'], 'm_ada_b': out['m_ada_b'], 'v_w_in': out['v_w_in'], 'v_b_f': out['v_b_f'], 'v_conv_w': out['v_conv_w'], 'v_conv_b': out['v_conv_b'], 'v_conv_ln_g': out['v_conv_ln_g'], 'v_conv_ln_b': out['v_conv_ln_b'], 'v_w_o': out['v_w_o'], 'v_w_ffn_in': out['v_w_ffn_in'], 'v_w_ffn_out': out['v_w_ffn_out'], 'v_mix_pre_g': out['v_mix_pre_g'], 'v_mix_post_g': out['v_mix_post_g'], 'v_ffn_pre_g': out['v_ffn_pre_g'], 'v_ffn_post_g': out['v_ffn_post_g'], 'v_ada_w': out['v_ada_w'], 'v_ada_b': out['v_ada_b']}


def _loss(weights, diff, rest, loss_target):
    with _jax.named_scope("forward"):
        args = {**rest, TWIN_DIFF_INPUT: diff, **{k: w.astype(_WEIGHT_DTYPES[k]) for k, w in weights.items()}}
        y = _forward(args)
    with _jax.named_scope("loss_head"):
        err = _jnp.square(y.astype(_jnp.float32) - loss_target)
        return 0.5 * _jnp.sum(_jnp.mean(err, axis=-1)) if err.ndim else 0.5 * err


def _adamw(w, g, m, v):
    m = ADAM_B1 * m + (1.0 - ADAM_B1) * g
    v = ADAM_B2 * v + (1.0 - ADAM_B2) * _jnp.square(g)
    m_hat = m / (1.0 - ADAM_B1 ** ADAM_STEP)
    v_hat = v / (1.0 - ADAM_B2 ** ADAM_STEP)
    delta = -ADAM_LR * (m_hat / (_jnp.sqrt(v_hat) + ADAM_EPS) + ADAM_WD * w)
    return delta, m, v


def reference(x, c, w_in, b_f, conv_w, conv_b, conv_ln_g, conv_ln_b, w_o, w_ffn_in, w_ffn_out, mix_pre_g, mix_post_g, ffn_pre_g, ffn_post_g, ada_w, ada_b, loss_target, m_w_in, m_b_f, m_conv_w, m_conv_b, m_conv_ln_g, m_conv_ln_b, m_w_o, m_w_ffn_in, m_w_ffn_out, m_mix_pre_g, m_mix_post_g, m_ffn_pre_g, m_ffn_post_g, m_ada_w, m_ada_b, v_w_in, v_b_f, v_conv_w, v_conv_b, v_conv_ln_g, v_conv_ln_b, v_w_o, v_w_ffn_in, v_w_ffn_out, v_mix_pre_g, v_mix_post_g, v_ffn_pre_g, v_ffn_post_g, v_ada_w, v_ada_b):
    given = dict(x=x, c=c, w_in=w_in, b_f=b_f, conv_w=conv_w, conv_b=conv_b, conv_ln_g=conv_ln_g, conv_ln_b=conv_ln_b, w_o=w_o, w_ffn_in=w_ffn_in, w_ffn_out=w_ffn_out, mix_pre_g=mix_pre_g, mix_post_g=mix_post_g, ffn_pre_g=ffn_pre_g, ffn_post_g=ffn_post_g, ada_w=ada_w, ada_b=ada_b, loss_target=loss_target, m_w_in=m_w_in, m_b_f=m_b_f, m_conv_w=m_conv_w, m_conv_b=m_conv_b, m_conv_ln_g=m_conv_ln_g, m_conv_ln_b=m_conv_ln_b, m_w_o=m_w_o, m_w_ffn_in=m_w_ffn_in, m_w_ffn_out=m_w_ffn_out, m_mix_pre_g=m_mix_pre_g, m_mix_post_g=m_mix_post_g, m_ffn_pre_g=m_ffn_pre_g, m_ffn_post_g=m_ffn_post_g, m_ada_w=m_ada_w, m_ada_b=m_ada_b, v_w_in=v_w_in, v_b_f=v_b_f, v_conv_w=v_conv_w, v_conv_b=v_conv_b, v_conv_ln_g=v_conv_ln_g, v_conv_ln_b=v_conv_ln_b, v_w_o=v_w_o, v_w_ffn_in=v_w_ffn_in, v_w_ffn_out=v_w_ffn_out, v_mix_pre_g=v_mix_pre_g, v_mix_post_g=v_mix_post_g, v_ffn_pre_g=v_ffn_pre_g, v_ffn_post_g=v_ffn_post_g, v_ada_w=v_ada_w, v_ada_b=v_ada_b)
    weights = {n: given[n] for n in TWIN_WEIGHTS}
    shared = {n: given[n] for n in SHARED_INPUTS}
    per_example = {n: given[n] for n in ['x', 'c']}
    grad_fn = _jax.value_and_grad(_loss, argnums=(0, 1))

    def one_microbatch(ex, loss_target):
        ex = dict(ex)
        diff = ex.pop(TWIN_DIFF_INPUT)
        return grad_fn(weights, diff, {**shared, **ex}, loss_target)

    if N_MICROBATCH == 1:
        loss, (grad_w, grad_x) = one_microbatch(per_example, given["loss_target"])
    else:
        def body(carry, xs):
            loss_sum, grad_sum = carry
            l_k, (gw_k, gx_k) = one_microbatch(xs[0], xs[1])
            with _jax.named_scope("update"):
                return (loss_sum + l_k, _jax.tree.map(_jnp.add, grad_sum, gw_k)), gx_k

        init = (_jnp.zeros((), _jnp.float32), _jax.tree.map(_jnp.zeros_like, weights))
        (loss, grad_w), grad_x = _jax.lax.scan(body, init, (per_example, given["loss_target"]))
    with _jax.named_scope("update"):
        delta_w, new_m, new_v = {}, {}, {}
        for n in TWIN_WEIGHTS:
            delta_w[n], new_m[n], new_v[n] = _adamw(weights[n], grad_w[n], given["m_" + n], given["v_" + n])
    return (loss, grad_x, *[grad_w[n] for n in TWIN_WEIGHTS], *[delta_w[n] for n in TWIN_WEIGHTS],
            *[new_m[n] for n in TWIN_WEIGHTS], *[new_v[n] for n in TWIN_WEIGHTS])
```

```python
import functools
import math

import jax
import jax.numpy as jnp
from jax import lax
from jax.experimental import pallas as pl
from jax.experimental.pallas import tpu as pltpu

F32 = jnp.float32
BF16 = jnp.bfloat16
MESH = pl.DeviceIdType.MESH
N_DEV = 8
EPS = 1e-6
CONV_K = 31
CONV_PAD = 32
CONV_CHUNK = 128
N_MOD = 6
NEG = -1e30
LANES = 128
VMEM_LIMIT = 56 * 2**20
ADAM_LR, ADAM_B1, ADAM_B2, ADAM_EPS, ADAM_WD, ADAM_STEP = 0.001, 0.9, 0.999, 1e-08, 0.01, 10

NN = (((1,), (0,)), ((), ()))
NT = (((1,), (1,)), ((), ()))
TN = (((0,), (0,)), ((), ()))


def _dot(a, b, dims):
    return lax.dot_general(a, b, dims, preferred_element_type=F32)


def _tile(n, pref, align=LANES):
    if n <= pref:
        return n
    t = (pref // align) * align
    while t >= align:
        if n % t == 0:
            return t
        t -= align
    return n


def _params(sem=None):
    return pltpu.CompilerParams(dimension_semantics=sem, vmem_limit_bytes=VMEM_LIMIT)


def _sigmoid(x):
    return 1.0 / (1.0 + jnp.exp(-x))


def _my_index():
    return 4 * lax.axis_index("x") + 2 * lax.axis_index("y") + lax.axis_index("c")


def _exchange(x, *, gather, name):
    blk = x.shape if gather else x.shape[1:]

    def body(x_ref, y_ref, send_sems, recv_sems, local_sem):
        mx, my, mc = lax.axis_index("x"), lax.axis_index("y"), lax.axis_index("c")
        me = 4 * mx + 2 * my + mc

        def src(p):
            return x_ref if gather else x_ref.at[p]

        mine = pltpu.make_async_copy(src(me), y_ref.at[me], local_sem)
        mine.start()
        copies = []
        for k in range(1, N_DEV):
            px = (1 - mx) if (k >> 2) & 1 else mx
            py = (1 - my) if (k >> 1) & 1 else my
            pc = (1 - mc) if k & 1 else mc
            cp = pltpu.make_async_remote_copy(
                src_ref=src(4 * px + 2 * py + pc), dst_ref=y_ref.at[me],
                send_sem=send_sems.at[k - 1], recv_sem=recv_sems.at[k - 1],
                device_id=(px, py, pc), device_id_type=MESH)
            cp.start()
            copies.append(cp)
        for cp in copies:
            cp.wait()
        mine.wait()

    return pl.pallas_call(
        body, name=name,
        out_shape=jax.ShapeDtypeStruct((N_DEV,) + tuple(blk), x.dtype),
        in_specs=[pl.BlockSpec(memory_space=pl.ANY)],
        out_specs=pl.BlockSpec(memory_space=pl.ANY),
        scratch_shapes=[pltpu.SemaphoreType.DMA((N_DEV - 1,)), pltpu.SemaphoreType.DMA((N_DEV - 1,)),
                        pltpu.SemaphoreType.DMA(())],
    )(x)


def _mm(pairs, mode, out_dtype, name, tm=512, tn=512):
    dims = {"nn": NN, "nt": NT, "tn": TN}[mode]
    a0, b0 = pairs[0][0], pairs[0][1]
    M = a0.shape[1] if mode == "tn" else a0.shape[0]
    N = b0.shape[0] if mode == "nt" else b0.shape[1]
    tm, tn = _tile(M, tm), _tile(N, tn)
    in_specs, args = [], []
    for pr in pairs:
        a, b = pr[0], pr[1]
        if mode == "tn":
            K = a.shape[0]
            in_specs.append(pl.BlockSpec((K, tm), lambda i, j: (0, i)))
            in_specs.append(pl.BlockSpec((K, tn), lambda i, j: (0, j)))
        elif mode == "nn":
            K = a.shape[1]
            in_specs.append(pl.BlockSpec((tm, K), lambda i, j: (i, 0)))
            in_specs.append(pl.BlockSpec((K, tn), lambda i, j: (0, j)))
        else:
            K = a.shape[1]
            cb = pr[2] if len(pr) > 2 else 0
            in_specs.append(pl.BlockSpec((tm, K), lambda i, j: (i, 0)))
            in_specs.append(pl.BlockSpec((tn, K), functools.partial(lambda i, j, cb: (j, cb), cb=cb)))
        args += [a, b]
    n_pairs = len(pairs)

    def body(*refs):
        o_ref = refs[-1]
        acc = None
        for k in range(n_pairs):
            d = _dot(refs[2 * k][...], refs[2 * k + 1][...], dims)
            acc = d if acc is None else acc + d
        o_ref[...] = acc.astype(o_ref.dtype)

    return pl.pallas_call(
        body, name=name, grid=(M // tm, N // tn), in_specs=in_specs,
        out_specs=pl.BlockSpec((tm, tn), lambda i, j: (i, j)),
        out_shape=jax.ShapeDtypeStruct((M, N), out_dtype),
        compiler_params=_params(("parallel", "arbitrary")),
    )(*args)


def _prenorm(x, g, sc, sh, name):
    T, D = x.shape
    tm = _tile(T, 512, 8)

    def body(x_ref, g_ref, sc_ref, sh_ref, h_ref):
        xv = x_ref[...]
        r = lax.rsqrt(jnp.mean(xv * xv, axis=-1, keepdims=True) + EPS)
        h_ref[...] = (((xv * r) * g_ref[...]) * (1.0 + sc_ref[...]) + sh_ref[...]).astype(BF16)

    row = pl.BlockSpec((1, D), lambda i: (0, 0))
    return pl.pallas_call(
        body, name=name, grid=(T // tm,),
        in_specs=[pl.BlockSpec((tm, D), lambda i: (i, 0)), row, row, row],
        out_specs=pl.BlockSpec((tm, D), lambda i: (i, 0)),
        out_shape=jax.ShapeDtypeStruct((T, D), BF16),
        compiler_params=_params(("parallel",)),
    )(x, g, sc, sh)


def _prenorm_bwd(dh, x, dres, g, sc, name):
    T, D = x.shape
    tm = _tile(T, 256, 8)

    def body(dh_ref, x_ref, dres_ref, g_ref, sc_ref, dx_ref, sm_ref):
        @pl.when(pl.program_id(0) == 0)
        def _():
            sm_ref[...] = jnp.zeros_like(sm_ref)

        xv, dhv = x_ref[...], dh_ref[...]
        r = lax.rsqrt(jnp.mean(xv * xv, axis=-1, keepdims=True) + EPS)
        xh = xv * r
        one_sc = 1.0 + sc_ref[...]
        sm_ref[0:1, :] += jnp.sum(dhv, axis=0, keepdims=True)
        sm_ref[1:2, :] += jnp.sum(dhv * (xh * g_ref[...]), axis=0, keepdims=True)
        sm_ref[2:3, :] += jnp.sum(dhv * one_sc * xh, axis=0, keepdims=True)
        dxh = dhv * one_sc * g_ref[...]
        dx_ref[...] = dres_ref[...] + r * (dxh - xh * jnp.mean(dxh * xh, axis=-1, keepdims=True))

    row = pl.BlockSpec((1, D), lambda i: (0, 0))
    big = pl.BlockSpec((tm, D), lambda i: (i, 0))
    return pl.pallas_call(
        body, name=name, grid=(T // tm,),
        in_specs=[big, big, big, row, row],
        out_specs=[big, pl.BlockSpec((8, D), lambda i: (0, 0))],
        out_shape=(jax.ShapeDtypeStruct((T, D), F32), jax.ShapeDtypeStruct((8, D), F32)),
        compiler_params=_params(("arbitrary",)),
    )(dh, x, dres, g, sc)


def _mm_postnorm(a, w, x, gate, gpost, name):
    T, K = a.shape
    D = w.shape[1]
    tm = _tile(T, 256, 8)

    def body(a_ref, w_ref, x_ref, gate_ref, gp_ref, y_ref, xn_ref):
        y = _dot(a_ref[...], w_ref[...], NN)
        r = lax.rsqrt(jnp.mean(y * y, axis=-1, keepdims=True) + EPS)
        y_ref[...] = y
        xn_ref[...] = x_ref[...] + gate_ref[...] * ((y * r) * gp_ref[...])

    row = pl.BlockSpec((1, D), lambda i: (0, 0))
    big = pl.BlockSpec((tm, D), lambda i: (i, 0))
    return pl.pallas_call(
        body, name=name, grid=(T // tm,),
        in_specs=[pl.BlockSpec((tm, K), lambda i: (i, 0)), pl.BlockSpec((K, D), lambda i: (0, 0)), big, row, row],
        out_specs=[big, big],
        out_shape=(jax.ShapeDtypeStruct((T, D), F32), jax.ShapeDtypeStruct((T, D), F32)),
        compiler_params=_params(("parallel",)),
    )(a, w, x, gate, gpost)


def _postnorm_bwd(dx, y, gate, gpost, name):
    T, D = y.shape
    tm = _tile(T, 256, 8)

    def body(dx_ref, y_ref, gate_ref, gp_ref, dy_ref, sm_ref):
        @pl.when(pl.program_id(0) == 0)
        def _():
            sm_ref[...] = jnp.zeros_like(sm_ref)

        yv, dxv = y_ref[...], dx_ref[...]
        r = lax.rsqrt(jnp.mean(yv * yv, axis=-1, keepdims=True) + EPS)
        yh = yv * r
        dn = dxv * gate_ref[...]
        sm_ref[0:1, :] += jnp.sum(dxv * (yh * gp_ref[...]), axis=0, keepdims=True)
        sm_ref[1:2, :] += jnp.sum(dn * yh, axis=0, keepdims=True)
        dyh = dn * gp_ref[...]
        dy_ref[...] = (r * (dyh - yh * jnp.mean(dyh * yh, axis=-1, keepdims=True))).astype(BF16)

    row = pl.BlockSpec((1, D), lambda i: (0, 0))
    big = pl.BlockSpec((tm, D), lambda i: (i, 0))
    return pl.pallas_call(
        body, name=name, grid=(T // tm,),
        in_specs=[big, big, row, row],
        out_specs=[big, pl.BlockSpec((8, D), lambda i: (0, 0))],
        out_shape=(jax.ShapeDtypeStruct((T, D), BF16), jax.ShapeDtypeStruct((8, D), F32)),
        compiler_params=_params(("arbitrary",)),
    )(dx, y, gate, gpost)


def _ffn_in_fwd(h, w, name):
    T, D = h.shape
    F = w.shape[1] // 2
    tm, tn = _tile(T, 1024), _tile(F, 256)
    nj = F // tn

    def body(h_ref, wg_ref, wu_ref, g_ref, u_ref, act_ref):
        hv = h_ref[...]
        g = _dot(hv, wg_ref[...], NN)
        u = _dot(hv, wu_ref[...], NN)
        g_ref[...] = g.astype(BF16)
        u_ref[...] = u.astype(BF16)
        act_ref[...] = ((g * _sigmoid(g)) * u).astype(BF16)

    out = pl.BlockSpec((tm, tn), lambda i, j: (i, j))
    return pl.pallas_call(
        body, name=name, grid=(T // tm, nj),
        in_specs=[pl.BlockSpec((tm, D), lambda i, j: (i, 0)),
                  pl.BlockSpec((D, tn), lambda i, j: (0, j)),
                  pl.BlockSpec((D, tn), lambda i, j: (0, j + nj))],
        out_specs=[out, out, out],
        out_shape=tuple(jax.ShapeDtypeStruct((T, F), BF16) for _ in range(3)),
        compiler_params=_params(("parallel", "arbitrary")),
    )(h, w, w)


def _ffn_act_bwd(dy, w_out, g, u, name):
    T, D = dy.shape
    F = w_out.shape[0]
    tm, tn = _tile(T, 1024), _tile(F, 256)

    def body(dy_ref, w_ref, g_ref, u_ref, dg_ref, du_ref):
        dact = _dot(dy_ref[...], w_ref[...], NT)
        gv, uv = g_ref[...].astype(F32), u_ref[...].astype(F32)
        sg = _sigmoid(gv)
        dg_ref[...] = (dact * uv * (sg * (1.0 + gv * (1.0 - sg)))).astype(BF16)
        du_ref[...] = (dact * (gv * sg)).astype(BF16)

    tile = pl.BlockSpec((tm, tn), lambda i, j: (i, j))
    return pl.pallas_call(
        body, name=name, grid=(T // tm, F // tn),
        in_specs=[pl.BlockSpec((tm, D), lambda i, j: (i, 0)), pl.BlockSpec((tn, D), lambda i, j: (j, 0)), tile, tile],
        out_specs=[tile, tile],
        out_shape=(jax.ShapeDtypeStruct((T, F), BF16), jax.ShapeDtypeStruct((T, F), BF16)),
        compiler_params=_params(("parallel", "arbitrary")),
    )(dy, w_out, g, u)


def _lane_scan(v, reverse):
    T = v.shape[-1]
    lane = lax.broadcasted_iota(jnp.int32, v.shape, 1)
    d = 1
    while d < T:
        if reverse:
            v = v + jnp.where(lane < T - d, pltpu.roll(v, T - d, axis=1), 0.0)
        else:
            v = v + jnp.where(lane >= d, pltpu.roll(v, d, axis=1), 0.0)
        d *= 2
    return v


def _fgate_fwd(h, wf_t, bf, name):
    T, D = h.shape
    R = wf_t.shape[0]

    def body(h_ref, w_ref, b_ref, fl_ref, cum_ref):
        fl = _dot(w_ref[...], h_ref[...], NT) + b_ref[...]
        fl_ref[...] = fl
        logf = jnp.minimum(fl, 0.0) - jnp.log(1.0 + jnp.exp(-jnp.abs(fl)))
        cum_ref[...] = _lane_scan(logf, reverse=False)

    return pl.pallas_call(
        body, name=name,
        out_shape=(jax.ShapeDtypeStruct((R, T), F32), jax.ShapeDtypeStruct((R, T), F32)),
        compiler_params=_params(),
    )(h, wf_t, bf)


def _fgate_bwd(dcum, fl, h, name):
    R, T = fl.shape
    D = h.shape[1]

    def body(dc_ref, fl_ref, h_ref, dfl_ref, dw_ref, db_ref):
        dlogf = _lane_scan(dc_ref[...], reverse=True)
        dfl = dlogf * _sigmoid(-fl_ref[...])
        dfl_ref[...] = dfl
        dw_ref[...] = _dot(dfl.astype(BF16), h_ref[...], NN)
        db_ref[...] = jnp.broadcast_to(jnp.sum(dfl, axis=-1, keepdims=True), (R, LANES))

    return pl.pallas_call(
        body, name=name,
        out_shape=(jax.ShapeDtypeStruct((R, T), F32), jax.ShapeDtypeStruct((R, D), F32),
                   jax.ShapeDtypeStruct((R, LANES), F32)),
        compiler_params=_params(),
    )(dcum, fl, h)


def _attn_fwd(qkv, cum_rows, n_heads, name, tq=256):
    T = qkv.shape[0]
    A = qkv.shape[1] // 3
    dh = A // n_heads
    hpb = LANES // dh
    nb = A // LANES
    tq = _tile(T, tq)
    nq = T // tq
    scale = dh ** -0.5

    def body(q_ref, k_ref, v_ref, c_ref, o_ref, l_ref):
        hp, i = pl.program_id(0), pl.program_id(1)
        rows = lax.broadcasted_iota(jnp.int32, (tq, tq), 0) + i * tq
        cols = lax.broadcasted_iota(jnp.int32, (tq, tq), 1)
        for hh in range(hpb):
            sl = slice(hh * dh, (hh + 1) * dh)
            qh = q_ref[:, sl]
            crow0 = (hp * hpb + hh) * nq

            def step(j, carry, sl=sl, qh=qh, crow0=crow0):
                m, l, acc = carry
                ks = pl.ds(pl.multiple_of(j * tq, tq), tq)
                kh, vh = k_ref[ks, sl], v_ref[ks, sl]
                s = _dot(qh, kh, NT) * scale - c_ref[pl.ds(crow0 + j, 1), :]
                s = jnp.where(cols + j * tq <= rows, s, NEG)
                m_new = jnp.maximum(m, jnp.max(s, axis=-1, keepdims=True))
                p = jnp.exp(s - m_new)
                alpha = jnp.exp(m - m_new)
                l = alpha * l + jnp.sum(p, axis=-1, keepdims=True)
                acc = alpha * acc + _dot(p.astype(BF16), vh, NN)
                return m_new, l, acc

            init = (jnp.full((tq, 1), NEG, F32), jnp.zeros((tq, 1), F32), jnp.zeros((tq, dh), F32))
            m, l, acc = lax.fori_loop(0, i + 1, step, init)
            o_ref[:, sl] = (acc / l).astype(BF16)
            l_ref[:, hh:hh + 1] = m + jnp.log(l)

    return pl.pallas_call(
        body, name=name, grid=(nb, nq),
        in_specs=[pl.BlockSpec((tq, LANES), lambda h, i: (i, h)),
                  pl.BlockSpec((T, LANES), lambda h, i: (0, nb + h)),
                  pl.BlockSpec((T, LANES), lambda h, i: (0, 2 * nb + h)),
                  pl.BlockSpec(cum_rows.shape, lambda h, i: (0, 0))],
        out_specs=[pl.BlockSpec((tq, LANES), lambda h, i: (i, h)),
                   pl.BlockSpec((None, tq, hpb), lambda h, i: (h, i, 0))],
        out_shape=(jax.ShapeDtypeStruct((T, A), BF16), jax.ShapeDtypeStruct((nb, T, hpb), F32)),
        compiler_params=_params(("parallel", "arbitrary")),
    )(qkv, qkv, qkv, cum_rows)


def _attn_bwd(qkv, dcat, o, lse, cum_rows, n_heads, name, tq=256):
    T = qkv.shape[0]
    A = qkv.shape[1] // 3
    dh = A // n_heads
    hpb = LANES // dh
    nb = A // LANES
    tq = _tile(T, tq)
    nq = T // tq
    scale = dh ** -0.5

    def body(q_ref, k_ref, v_ref, do_ref, o_ref, l_ref, c_ref, dqkv_ref, dc_ref, dr_ref, dq_acc, delta, drow):
        hp = pl.program_id(0)
        rows = lax.broadcasted_iota(jnp.int32, (tq, tq), 0)
        cols = lax.broadcasted_iota(jnp.int32, (tq, tq), 1)
        for hh in range(hpb):
            sl = slice(hh * dh, (hh + 1) * dh)
            crow0 = (hp * hpb + hh) * nq

            def dstep(i, _, sl=sl):
                rs = pl.ds(pl.multiple_of(i * tq, tq), tq)
                delta[rs, :] = jnp.sum(do_ref[rs, sl] * o_ref[rs, sl].astype(F32), axis=-1, keepdims=True)
                dq_acc[rs, :] = jnp.zeros((tq, dh), F32)
                drow[rs, :] = jnp.zeros((tq, 1), F32)
                return 0

            lax.fori_loop(0, nq, dstep, 0)

            def kv_step(j, _, sl=sl, hh=hh, crow0=crow0):
                ks = pl.ds(pl.multiple_of(j * tq, tq), tq)
                kh, vh = k_ref[ks, sl], v_ref[ks, sl]
                crow = c_ref[pl.ds(crow0 + j, 1), :]

                def q_step(i, carry):
                    dk, dv, dcs = carry
                    rs = pl.ds(pl.multiple_of(i * tq, tq), tq)
                    qh = q_ref[rs, sl]
                    doh = do_ref[rs, sl].astype(BF16)
                    s = _dot(qh, kh, NT) * scale - crow
                    s = jnp.where(cols + j * tq <= rows + i * tq, s, NEG)
                    p = jnp.exp(s - l_ref[rs, hh:hh + 1])
                    dv = dv + _dot(p.astype(BF16), doh, TN)
                    dp = _dot(doh, vh, NT)
                    ds = p * (dp - delta[rs, :])
                    dcs = dcs - jnp.sum(ds, axis=0, keepdims=True)
                    drow[rs, :] += jnp.sum(ds, axis=-1, keepdims=True)
                    dsb = (ds * scale).astype(BF16)
                    dq_acc[rs, :] += _dot(dsb, kh, NN)
                    dk = dk + _dot(dsb, qh, TN)
                    return dk, dv, dcs

                init = (jnp.zeros((tq, dh), F32), jnp.zeros((tq, dh), F32), jnp.zeros((1, tq), F32))
                dk, dv, dcs = lax.fori_loop(j, nq, q_step, init)
                dqkv_ref[1, ks, sl] = dk.astype(BF16)
                dqkv_ref[2, ks, sl] = dv.astype(BF16)
                dc_ref[pl.ds(crow0 + j, 1), :] = dcs
                return 0

            lax.fori_loop(0, nq, kv_step, 0)
            dqkv_ref[0, :, sl] = dq_acc[...].astype(BF16)
            dr_ref[:, hh:hh + 1] = drow[...]

    col = lambda off: pl.BlockSpec((T, LANES), functools.partial(lambda h, off: (0, off + h), off=off))
    return pl.pallas_call(
        body, name=name, grid=(nb,),
        in_specs=[col(0), col(nb), col(2 * nb), col(0), col(0),
                  pl.BlockSpec((None, T, hpb), lambda h: (h, 0, 0)),
                  pl.BlockSpec(cum_rows.shape, lambda h: (0, 0))],
        out_specs=[pl.BlockSpec((3, T, LANES), lambda h: (0, 0, h)),
                   pl.BlockSpec(cum_rows.shape, lambda h: (0, 0)),
                   pl.BlockSpec((None, T, hpb), lambda h: (h, 0, 0))],
        out_shape=(jax.ShapeDtypeStruct((3, T, A), BF16), jax.ShapeDtypeStruct(cum_rows.shape, F32),
                   jax.ShapeDtypeStruct((nb, T, hpb), F32)),
        scratch_shapes=[pltpu.VMEM((T, dh), F32), pltpu.VMEM((T, 1), F32), pltpu.VMEM((T, 1), F32)],
        compiler_params=_params(("arbitrary",)),
    )(qkv, qkv, qkv, dcat, o, lse, cum_rows)


def _glu_into(upad, cv_ref, cg_ref, T):
    upad[0:CONV_PAD, :] = jnp.zeros((CONV_PAD, upad.shape[1]), F32)

    def fill(c, _):
        rs = pl.ds(pl.multiple_of(c * CONV_CHUNK, CONV_CHUNK), CONV_CHUNK)
        upad[pl.ds(pl.multiple_of(CONV_PAD + c * CONV_CHUNK, 8), CONV_CHUNK), :] = cv_ref[rs, :] * _sigmoid(cg_ref[rs, :])
        return 0

    lax.fori_loop(0, T // CONV_CHUNK, fill, 0)


def _conv_taps(win, w_ref, first, step):
    acc = None
    for k in range(CONV_K):
        o = first + step * k
        t = w_ref[k:k + 1, :] * win[o:o + CONV_CHUNK, :]
        acc = t if acc is None else acc + t
    return acc


def _conv_fwd(cproj, w, b, lg, lb, name):
    T = cproj.shape[0]
    C = cproj.shape[1] // 2
    off = CONV_PAD - (CONV_K - 1)

    def body(cv_ref, cg_ref, w_ref, b_ref, lg_ref, lb_ref, out_ref, upad, win):
        _glu_into(upad, cv_ref, cg_ref, T)

        def chunk(c, _):
            r0 = pl.multiple_of(c * CONV_CHUNK, CONV_CHUNK)
            win[...] = upad[pl.ds(r0, CONV_CHUNK + CONV_PAD), :]
            u1 = _conv_taps(win, w_ref, off, 1) + b_ref[...]
            mu = jnp.mean(u1, axis=-1, keepdims=True)
            var = jnp.mean(jnp.square(u1 - mu), axis=-1, keepdims=True)
            u2 = ((u1 - mu) * lax.rsqrt(var + EPS)) * lg_ref[...] + lb_ref[...]
            out_ref[pl.ds(r0, CONV_CHUNK), :] = (u2 * _sigmoid(u2)).astype(BF16)
            return 0

        lax.fori_loop(0, T // CONV_CHUNK, chunk, 0)

    row = pl.BlockSpec((1, C), lambda i: (0, 0))
    return pl.pallas_call(
        body, name=name, grid=(1,),
        in_specs=[pl.BlockSpec((T, C), lambda i: (0, 0)), pl.BlockSpec((T, C), lambda i: (0, 1)),
                  pl.BlockSpec(w.shape, lambda i: (0, 0)), row, row, row],
        out_specs=pl.BlockSpec((T, C), lambda i: (0, 0)),
        out_shape=jax.ShapeDtypeStruct((T, C), BF16),
        scratch_shapes=[pltpu.VMEM((T + CONV_PAD, C), F32), pltpu.VMEM((CONV_CHUNK + CONV_PAD, C), F32)],
        compiler_params=_params(("arbitrary",)),
    )(cproj, cproj, w, b, lg, lb)


def _conv_bwd(cproj, dcat, w, b, lg, lb, name):
    T = cproj.shape[0]
    C = cproj.shape[1] // 2
    off = CONV_PAD - (CONV_K - 1)
    n_chunks = T // CONV_CHUNK

    def fold(v):
        return jnp.sum(v.reshape(CONV_CHUNK // 8, 8, C), axis=0)

    def body(cv_ref, cg_ref, du_ref, w_ref, b_ref, lg_ref, lb_ref, dc_ref, dw_ref, sm_ref, upad, dpad, win, dwacc, smacc):
        _glu_into(upad, cv_ref, cg_ref, T)
        dpad[pl.ds(T, CONV_PAD), :] = jnp.zeros((CONV_PAD, C), F32)
        dwacc[...] = jnp.zeros_like(dwacc)
        smacc[...] = jnp.zeros_like(smacc)

        def chunk_a(c, _):
            r0 = pl.multiple_of(c * CONV_CHUNK, CONV_CHUNK)
            win[...] = upad[pl.ds(r0, CONV_CHUNK + CONV_PAD), :]
            u1 = _conv_taps(win, w_ref, off, 1) + b_ref[...]
            mu = jnp.mean(u1, axis=-1, keepdims=True)
            var = jnp.mean(jnp.square(u1 - mu), axis=-1, keepdims=True)
            rstd = lax.rsqrt(var + EPS)
            u1h = (u1 - mu) * rstd
            u2 = u1h * lg_ref[...] + lb_ref[...]
            sg = _sigmoid(u2)
            du2 = du_ref[pl.ds(r0, CONV_CHUNK), :] * (sg * (1.0 + u2 * (1.0 - sg)))
            smacc[8:16, :] += fold(du2 * u1h)
            smacc[16:24, :] += fold(du2)
            du1h = du2 * lg_ref[...]
            du1 = rstd * (du1h - jnp.mean(du1h, axis=-1, keepdims=True)
                          - u1h * jnp.mean(du1h * u1h, axis=-1, keepdims=True))
            smacc[0:8, :] += fold(du1)
            dpad[pl.ds(r0, CONV_CHUNK), :] = du1
            for k in range(CONV_K):
                dwacc[8 * k:8 * k + 8, :] += fold(du1 * win[off + k:off + k + CONV_CHUNK, :])
            return 0

        lax.fori_loop(0, n_chunks, chunk_a, 0)

        def chunk_b(c, _):
            r0 = pl.multiple_of(c * CONV_CHUNK, CONV_CHUNK)
            rs = pl.ds(r0, CONV_CHUNK)
            win[...] = dpad[pl.ds(r0, CONV_CHUNK + CONV_PAD), :]
            du0 = _conv_taps(win, w_ref, CONV_K - 1, -1)
            cv, sg = cv_ref[rs, :], _sigmoid(cg_ref[rs, :])
            dc_ref[rs, 0:C] = (du0 * sg).astype(BF16)
            dc_ref[rs, C:2 * C] = (du0 * cv * (sg * (1.0 - sg))).astype(BF16)
            return 0

        lax.fori_loop(0, n_chunks, chunk_b, 0)
        for k in range(CONV_K):
            dw_ref[k:k + 1, :] = jnp.sum(dwacc[8 * k:8 * k + 8, :], axis=0, keepdims=True)
        dw_ref[CONV_K:CONV_PAD, :] = jnp.zeros((CONV_PAD - CONV_K, C), F32)
        for r in range(3):
            sm_ref[r:r + 1, :] = jnp.sum(smacc[8 * r:8 * r + 8, :], axis=0, keepdims=True)
        sm_ref[3:8, :] = jnp.zeros((5, C), F32)

    row = pl.BlockSpec((1, C), lambda i: (0, 0))
    return pl.pallas_call(
        body, name=name, grid=(1,),
        in_specs=[pl.BlockSpec((T, C), lambda i: (0, 0)), pl.BlockSpec((T, C), lambda i: (0, 1)),
                  pl.BlockSpec((T, C), lambda i: (0, 1)),
                  pl.BlockSpec(w.shape, lambda i: (0, 0)), row, row, row],
        out_specs=[pl.BlockSpec((T, 2 * C), lambda i: (0, 0)), pl.BlockSpec((CONV_PAD, C), lambda i: (0, 0)),
                   pl.BlockSpec((8, C), lambda i: (0, 0))],
        out_shape=(jax.ShapeDtypeStruct((T, 2 * C), BF16), jax.ShapeDtypeStruct((CONV_PAD, C), F32),
                   jax.ShapeDtypeStruct((8, C), F32)),
        scratch_shapes=[pltpu.VMEM((T + CONV_PAD, C), F32), pltpu.VMEM((T + CONV_PAD, C), F32),
                        pltpu.VMEM((CONV_CHUNK + CONV_PAD, C), F32), pltpu.VMEM((8 * CONV_PAD, C), F32),
                        pltpu.VMEM((24, C), F32)],
        compiler_params=_params(("arbitrary",)),
    )(cproj, cproj, dcat, w, b, lg, lb)


def _loss_head(x, target, name):
    T, D = x.shape
    tm = _tile(T, 512, 8)

    def body(x_ref, t_ref, loss_ref, dx_ref):
        @pl.when(pl.program_id(0) == 0)
        def _():
            loss_ref[...] = jnp.zeros_like(loss_ref)

        err = x_ref[...] - t_ref[...]
        part = jnp.sum(jnp.mean(err * err, axis=-1, keepdims=True), axis=0, keepdims=True)
        loss_ref[...] += jnp.broadcast_to(0.5 * part, loss_ref.shape)
        dx_ref[...] = err * (1.0 / D)

    big = pl.BlockSpec((tm, D), lambda i: (i, 0))
    return pl.pallas_call(
        body, name=name, grid=(T // tm,),
        in_specs=[big, big],
        out_specs=[pl.BlockSpec((8, LANES), lambda i: (0, 0)), big],
        out_shape=(jax.ShapeDtypeStruct((8, LANES), F32), jax.ShapeDtypeStruct((T, D), F32)),
        compiler_params=_params(("arbitrary",)),
    )(x, target)


def _ada_fwd(c_all, ada_w, ada_b_loc, name):
    L, D, S = ada_w.shape
    B = c_all.shape[0]

    def body(c_ref, w_ref, b_ref, o_ref):
        c = c_ref[...]
        ca = (c * _sigmoid(c)).astype(BF16)
        o_ref[...] = _dot(ca, w_ref[...].astype(BF16), NN) + b_ref[...]

    return pl.pallas_call(
        body, name=name, grid=(L,),
        in_specs=[pl.BlockSpec((B, D), lambda l: (0, 0)), pl.BlockSpec((None, D, S), lambda l: (l, 0, 0)),
                  pl.BlockSpec((None, 1, S), lambda l: (l, 0, 0))],
        out_specs=pl.BlockSpec((None, B, S), lambda l: (l, 0, 0)),
        out_shape=jax.ShapeDtypeStruct((L, B, S), F32),
        compiler_params=_params(("parallel",)),
    )(c_all, ada_w, ada_b_loc)


def _ada_bwd(c_all_t, dmod_loc, name):
    D, B = c_all_t.shape
    L, _, S = dmod_loc.shape

    def body(c_ref, dm_ref, o_ref):
        c = c_ref[...]
        ca = c * _sigmoid(c)
        acc = None
        for bb in range(B):
            t = ca[:, bb:bb + 1] * dm_ref[bb:bb + 1, :]
            acc = t if acc is None else acc + t
        o_ref[...] = acc

    return pl.pallas_call(
        body, name=name, grid=(L,),
        in_specs=[pl.BlockSpec((D, B), lambda l: (0, 0)), pl.BlockSpec((None, B, S), lambda l: (l, 0, 0))],
        out_specs=pl.BlockSpec((None, D, S), lambda l: (l, 0, 0)),
        out_shape=jax.ShapeDtypeStruct((L, D, S), F32),
        compiler_params=_params(("parallel",)),
    )(c_all_t, dmod_loc)


def _sum_devices(parts, name):
    _, R, C = parts.shape
    tm = _tile(R, 256, 8)

    def body(p_ref, o_ref):
        acc = p_ref[0].astype(F32)
        for d in range(1, N_DEV):
            acc = acc + p_ref[d].astype(F32)
        o_ref[...] = acc

    return pl.pallas_call(
        body, name=name, grid=(R // tm,),
        in_specs=[pl.BlockSpec((N_DEV, tm, C), lambda i: (0, i, 0))],
        out_specs=pl.BlockSpec((tm, C), lambda i: (i, 0)),
        out_shape=jax.ShapeDtypeStruct((R, C), F32),
        compiler_params=_params(("parallel",)),
    )(parts)


def _adamw_math(w, g, m, v):
    m = ADAM_B1 * m + (1.0 - ADAM_B1) * g
    v = ADAM_B2 * v + (1.0 - ADAM_B2) * (g * g)
    m_hat = m / (1.0 - ADAM_B1 ** ADAM_STEP)
    v_hat = v / (1.0 - ADAM_B2 ** ADAM_STEP)
    delta = -ADAM_LR * (m_hat / (jnp.sqrt(v_hat) + ADAM_EPS) + ADAM_WD * w)
    return delta, m, v


def _adamw(w, g, m, v, name, summed):
    R, C = w.shape
    tm = _tile(R, 256, 8)

    def body(w_ref, g_ref, m_ref, v_ref, go_ref, d_ref, mo_ref, vo_ref):
        if summed:
            g = g_ref[0].astype(F32)
            for d in range(1, N_DEV):
                g = g + g_ref[d].astype(F32)
        else:
            g = g_ref[...]
        delta, mn, vn = _adamw_math(w_ref[...], g, m_ref[...], v_ref[...])
        go_ref[...] = g
        d_ref[...] = delta
        mo_ref[...] = mn
        vo_ref[...] = vn

    big = pl.BlockSpec((tm, C), lambda i: (i, 0))
    gspec = pl.BlockSpec((N_DEV, tm, C), lambda i: (0, i, 0)) if summed else big
    return pl.pallas_call(
        body, name=name, grid=(R // tm,),
        in_specs=[big, gspec, big, big],
        out_specs=[big, big, big, big],
        out_shape=tuple(jax.ShapeDtypeStruct((R, C), F32) for _ in range(4)),
        compiler_params=_params(("parallel",)),
    )(w, g, m, v)


def _pack(arrs, D):
    L = arrs[0].shape[0]
    cols = []
    for a in arrs:
        f = a.reshape(L, -1)
        n = f.shape[1]
        rows = -(-n // D)
        cols.append(jnp.pad(f, ((0, 0), (0, rows * D - n))).reshape(L, rows, D))
    return jnp.concatenate(cols, axis=1)


def _unpack(p, shapes, D):
    L = p.shape[0]
    out, r = [], 0
    for s in shapes:
        n = math.prod(s[1:])
        rows = -(-n // D)
        out.append(p[:, r:r + rows].reshape(L, rows * D)[:, :n].reshape(s))
        r += rows
    return out


def kernel(x, c, w_in, b_f, conv_w, conv_b, conv_ln_g, conv_ln_b, w_o, w_ffn_in, w_ffn_out, mix_pre_g, mix_post_g, ffn_pre_g, ffn_post_g, ada_w, ada_b, loss_target, m_w_in, m_b_f, m_conv_w, m_conv_b, m_conv_ln_g, m_conv_ln_b, m_w_o, m_w_ffn_in, m_w_ffn_out, m_mix_pre_g, m_mix_post_g, m_ffn_pre_g, m_ffn_post_g, m_ada_w, m_ada_b, v_w_in, v_b_f, v_conv_w, v_conv_b, v_conv_ln_g, v_conv_ln_b, v_w_o, v_w_ffn_in, v_w_ffn_out, v_mix_pre_g, v_mix_post_g, v_ffn_pre_g, v_ffn_post_g, v_ada_w, v_ada_b):
    L, D, s_in = w_in.shape
    T = x.shape[1]
    H = b_f.shape[1]
    A = D // 2
    C = D - A
    cs = conv_w.shape[2]
    F = w_ffn_out.shape[1] * N_DEV
    s_ff = w_ffn_in.shape[2]
    s_ada = ada_w.shape[2]
    R = 16
    me = _my_index()
    x0 = x[0]
    target = loss_target[0]
    tq = _tile(T, 256)
    nq = T // tq

    def gathered_cols(w, name):
        g = _exchange(w.astype(BF16), gather=True, name=name)
        return jnp.transpose(g, (1, 2, 0, 3)).reshape(w.shape[0], w.shape[1], N_DEV * w.shape[2])

    def gathered_rows(w, name):
        g = _exchange(w.astype(BF16), gather=True, name=name)
        return jnp.transpose(g, (1, 0, 2, 3)).reshape(w.shape[0], N_DEV * w.shape[1], w.shape[2])

    W_in = gathered_cols(w_in, "gather_w_in")
    W_ffn_in = gathered_cols(w_ffn_in, "gather_w_ffn_in")
    W_o = gathered_rows(w_o, "gather_w_o")
    W_ffn_out = gathered_rows(w_ffn_out, "gather_w_ffn_out")
    cw_g = _exchange(conv_w, gather=True, name="gather_conv_w")
    conv_w_full = jnp.transpose(cw_g, (1, 2, 0, 3)).reshape(L, CONV_K, C)
    conv_w_pad = jnp.pad(conv_w_full, ((0, 0), (0, CONV_PAD - CONV_K), (0, 0)))
    W_qkv = W_in[:, :, :3 * A]
    W_f = W_in[:, :, 3 * A:3 * A + H]
    W_c = W_in[:, :, 3 * A + H:]
    W_f_t = jnp.pad(jnp.transpose(W_f, (0, 2, 1)), ((0, 0), (0, R - H), (0, 0)))
    W_f_pad = jnp.pad(W_f, ((0, 0), (0, 0), (0, R - H)))
    b_f_col = jnp.pad(b_f, ((0, 0), (0, R - H)))[:, :, None]

    c_all = _exchange(c, gather=True, name="gather_c").reshape(N_DEV, D)
    ada_b_loc = lax.dynamic_slice_in_dim(ada_b, me * s_ada, s_ada, axis=1)[:, None, :]
    mod_loc = _ada_fwd(c_all, ada_w, ada_b_loc, "ada_fwd")
    mod_g = _exchange(mod_loc, gather=True, name="gather_mod")
    mod = lax.dynamic_index_in_dim(mod_g, me, axis=2, keepdims=False)
    mod = jnp.transpose(mod, (1, 0, 2)).reshape(L, N_MOD, 1, D)

    saved = []
    xc = x0
    for l in range(L):
        sh1, sc1, g1, sh2, sc2, g2 = (mod[l, k] for k in range(N_MOD))
        gpre1, gpost1, gpre2, gpost2 = (p[l][None, :] for p in (mix_pre_g, mix_post_g, ffn_pre_g, ffn_post_g))
        h1 = _prenorm(xc, gpre1, sc1, sh1, f"prenorm1_{l}")
        qkv = _mm([(h1, W_qkv[l])], "nn", BF16, f"proj_qkv_{l}")
        cproj = _mm([(h1, W_c[l])], "nn", F32, f"proj_conv_{l}")
        fl, cum = _fgate_fwd(h1, W_f_t[l], b_f_col[l], f"fgate_{l}")
        cum_rows = cum[:H].reshape(H * nq, tq)
        o, lse = _attn_fwd(qkv, cum_rows, H, f"attn_{l}", tq)
        u3 = _conv_fwd(cproj, conv_w_pad[l], conv_b[l][None, :], conv_ln_g[l][None, :], conv_ln_b[l][None, :], f"conv_{l}")
        cat = jnp.concatenate([o, u3], axis=-1)
        y1, x_mid = _mm_postnorm(cat, W_o[l], xc, g1, gpost1, f"out_proj_{l}")
        h2 = _prenorm(x_mid, gpre2, sc2, sh2, f"prenorm2_{l}")
        g, u, act = _ffn_in_fwd(h2, W_ffn_in[l], f"ffn_in_{l}")
        y2, x_out = _mm_postnorm(act, W_ffn_out[l], x_mid, g2, gpost2, f"ffn_out_{l}")
        saved.append((xc, h1, qkv, cproj, fl, cum_rows, o, lse, cat, y1, x_mid, h2, g, u, act, y2))
        xc = x_out

    loss_tile, dx = _loss_head(xc, target, "loss_head")
    loss = lax.psum(loss_tile[0, 0], ("x", "y", "c"))

    dW_in, dW_o, dW_ffn_in, dW_ffn_out, small = [None] * L, [None] * L, [None] * L, [None] * L, [None] * L
    for l in reversed(range(L)):
        xin, h1, qkv, cproj, fl, cum_rows, o, lse, cat, y1, x_mid, h2, g, u, act, y2 = saved[l]
        sh1, sc1, g1, sh2, sc2, g2 = (mod[l, k] for k in range(N_MOD))
        gpre1, gpost1, gpre2, gpost2 = (p[l][None, :] for p in (mix_pre_g, mix_post_g, ffn_pre_g, ffn_post_g))
        dy2, sm_post2 = _postnorm_bwd(dx, y2, g2, gpost2, f"postnorm2_bwd_{l}")
        dgate, dup = _ffn_act_bwd(dy2, W_ffn_out[l], g, u, f"ffn_act_bwd_{l}")
        dW_ffn_out[l] = _mm([(act, dy2)], "tn", BF16, f"dw_ffn_out_{l}")
        dh2 = _mm([(dgate, W_ffn_in[l], 0), (dup, W_ffn_in[l], 1)], "nt", F32, f"dh2_{l}")
        dWg = _mm([(h2, dgate)], "tn", BF16, f"dw_ffn_gate_{l}")
        dWu = _mm([(h2, dup)], "tn", BF16, f"dw_ffn_up_{l}")
        dW_ffn_in[l] = jnp.concatenate([dWg, dWu], axis=1)
        dx_mid, sm_pre2 = _prenorm_bwd(dh2, x_mid, dx, gpre2, sc2, f"prenorm2_bwd_{l}")
        dy1, sm_post1 = _postnorm_bwd(dx_mid, y1, g1, gpost1, f"postnorm1_bwd_{l}")
        dcat = _mm([(dy1, W_o[l])], "nt", F32, f"dcat_{l}")
        dW_o[l] = _mm([(cat, dy1)], "tn", BF16, f"dw_o_{l}")
        dqkv, dcum_rows, dcum_q = _attn_bwd(qkv, dcat, o, lse, cum_rows, H, f"attn_bwd_{l}", tq)
        dcproj, dconv_w, sm_conv = _conv_bwd(cproj, dcat, conv_w_pad[l], conv_b[l][None, :], conv_ln_g[l][None, :],
                                             conv_ln_b[l][None, :], f"conv_bwd_{l}")
        dcum = dcum_rows.reshape(H, T) + jnp.transpose(dcum_q, (0, 2, 1)).reshape(H, T)
        dcum = jnp.pad(dcum, ((0, R - H), (0, 0)))
        dfl_t, dwf_t, dbf = _fgate_bwd(dcum, fl, h1, f"fgate_bwd_{l}")
        dfl = jnp.transpose(dfl_t).astype(BF16)
        dh1 = _mm([(dqkv[0], W_qkv[l], 0), (dqkv[1], W_qkv[l], 1), (dqkv[2], W_qkv[l], 2),
                   (dfl, W_f_pad[l]), (dcproj, W_c[l])], "nt", F32, f"dh1_{l}")
        dWq = [_mm([(h1, dqkv[k])], "tn", BF16, f"dw_qkv{k}_{l}") for k in range(3)]
        dWc = _mm([(h1, dcproj)], "tn", BF16, f"dw_conv_{l}")
        dWf = jnp.transpose(dwf_t[:H]).astype(BF16)
        dW_in[l] = jnp.concatenate(dWq + [dWf, dWc], axis=1)
        dx, sm_pre1 = _prenorm_bwd(dh1, xin, dx_mid, gpre1, sc1, f"prenorm1_bwd_{l}")
        dmod = jnp.stack([sm_pre1[0], sm_pre1[1], sm_post1[0], sm_pre2[0], sm_pre2[1], sm_post2[0]])
        small[l] = (dmod, sm_pre1[2], sm_post1[1], sm_pre2[2], sm_post2[1], sm_conv[0], sm_conv[1], sm_conv[2],
                    dbf[:H, 0], dconv_w[:CONV_K])
    grad_x = dx[None]

    small_names = 10
    small_l = [jnp.stack([small[l][k] for l in range(L)]) for k in range(small_names)]
    small_shapes = [a.shape for a in small_l]
    packed = _pack(small_l, D)
    rows = packed.shape[1]
    rows_pad = -(-L * rows // 8) * 8
    packed2 = jnp.pad(packed.reshape(L * rows, D), ((0, rows_pad - L * rows), (0, 0)))
    small_g = _exchange(packed2, gather=True, name="gather_small")
    small_sum = _sum_devices(small_g, "sum_small")[:L * rows].reshape(L, rows, D)
    (g_ada_b6, g_mix_pre, g_mix_post, g_ffn_pre, g_ffn_post, g_conv_b, g_ln_g, g_ln_b, g_b_f,
     g_conv_w_full) = _unpack(small_sum, small_shapes, D)
    g_ada_b = g_ada_b6.reshape(L, N_MOD * D)
    g_conv_w = lax.dynamic_slice_in_dim(g_conv_w_full, me * cs, cs, axis=2)
    dmod_all = small_g[:, :L * rows].reshape(N_DEV, L, rows, D)[:, :, :N_MOD].reshape(N_DEV, L, N_MOD * D)
    dmod_loc = jnp.transpose(lax.dynamic_slice_in_dim(dmod_all, me * s_ada, s_ada, axis=2), (1, 0, 2))
    g_ada_w = _ada_bwd(jnp.transpose(c_all), dmod_loc, "ada_bwd")

    def step_cols(dws, w, m, v, name):
        rws, s = w.shape[1], w.shape[2]
        part = jnp.stack([jnp.transpose(dw.reshape(rws, N_DEV, s), (1, 0, 2)) for dw in dws], axis=1)
        recv = _exchange(part.reshape(N_DEV, L * rws, s), gather=False, name="scatter_" + name)
        outs = _adamw(w.reshape(L * rws, s), recv, m.reshape(L * rws, s), v.reshape(L * rws, s), "adamw_" + name, True)
        return [t.reshape(w.shape) for t in outs]

    def step_rows(dws, w, m, v, name):
        s, cl = w.shape[1], w.shape[2]
        part = jnp.stack([dw.reshape(N_DEV, s, cl) for dw in dws], axis=1)
        recv = _exchange(part.reshape(N_DEV, L * s, cl), gather=False, name="scatter_" + name)
        outs = _adamw(w.reshape(L * s, cl), recv, m.reshape(L * s, cl), v.reshape(L * s, cl), "adamw_" + name, True)
        return [t.reshape(w.shape) for t in outs]

    r_w_in = step_cols(dW_in, w_in, m_w_in, v_w_in, "w_in")
    r_w_ffn_in = step_cols(dW_ffn_in, w_ffn_in, m_w_ffn_in, v_w_ffn_in, "w_ffn_in")
    r_w_o = step_rows(dW_o, w_o, m_w_o, v_w_o, "w_o")
    r_w_ffn_out = step_rows(dW_ffn_out, w_ffn_out, m_w_ffn_out, v_w_ffn_out, "w_ffn_out")
    r_ada_w = [t.reshape(ada_w.shape) for t in _adamw(
        ada_w.reshape(L * D, s_ada), g_ada_w.reshape(L * D, s_ada), m_ada_w.reshape(L * D, s_ada),
        v_ada_w.reshape(L * D, s_ada), "adamw_ada_w", False)]

    sw = [b_f, conv_w, conv_b, conv_ln_g, conv_ln_b, mix_pre_g, mix_post_g, ffn_pre_g, ffn_post_g, ada_b]
    sg = [g_b_f, g_conv_w, g_conv_b, g_ln_g, g_ln_b, g_mix_pre, g_mix_post, g_ffn_pre, g_ffn_post, g_ada_b]
    sm = [m_b_f, m_conv_w, m_conv_b, m_conv_ln_g, m_conv_ln_b, m_mix_pre_g, m_mix_post_g, m_ffn_pre_g, m_ffn_post_g, m_ada_b]
    sv = [v_b_f, v_conv_w, v_conv_b, v_conv_ln_g, v_conv_ln_b, v_mix_pre_g, v_mix_post_g, v_ffn_pre_g, v_ffn_post_g, v_ada_b]
    shapes = [a.shape for a in sw]

    def flat(arrs):
        p = _pack(arrs, D)
        n = p.shape[0] * p.shape[1]
        return jnp.pad(p.reshape(n, D), ((0, -(-n // 8) * 8 - n), (0, 0))), p.shape

    pw, pshape = flat(sw)
    pg, pm, pv = flat(sg)[0], flat(sm)[0], flat(sv)[0]
    s_outs = _adamw(pw, pg, pm, pv, "adamw_small", False)
    n_small = pshape[0] * pshape[1]
    s_g, s_d, s_m, s_v = (_unpack(t[:n_small].reshape(pshape), shapes, D) for t in s_outs)

    big = {"w_in": r_w_in, "w_o": r_w_o, "w_ffn_in": r_w_ffn_in, "w_ffn_out": r_w_ffn_out, "ada_w": r_ada_w}
    order = ["w_in", "b_f", "conv_w", "conv_b", "conv_ln_g", "conv_ln_b", "w_o", "w_ffn_in", "w_ffn_out",
             "mix_pre_g", "mix_post_g", "ffn_pre_g", "ffn_post_g", "ada_w", "ada_b"]
    small_pos = {n: i for i, n in enumerate(["b_f", "conv_w", "conv_b", "conv_ln_g", "conv_ln_b", "mix_pre_g",
                                             "mix_post_g", "ffn_pre_g", "ffn_post_g", "ada_b"])}

    def pick(n, k):
        if n in big:
            return big[n][k]
        return (s_g, s_d, s_m, s_v)[k][small_pos[n]]

    return (loss, grad_x, *[pick(n, 0) for n in order], *[pick(n, 1) for n in order],
            *[pick(n, 2) for n in order], *[pick(n, 3) for n in order])
```

```python
import functools
import math

import jax
import jax.numpy as jnp
from jax import lax
from jax.experimental import pallas as pl
from jax.experimental.pallas import tpu as pltpu

F32 = jnp.float32
BF16 = jnp.bfloat16
MESH = pl.DeviceIdType.MESH
N_DEV = 8
EPS = 1e-6
CONV_K = 31
CONV_PAD = 32
CONV_CHUNK = 128
N_MOD = 6
NEG = -1e30
LANES = 128
VMEM_LIMIT = 56 * 2**20
ADAM_LR, ADAM_B1, ADAM_B2, ADAM_EPS, ADAM_WD, ADAM_STEP = 0.001, 0.9, 0.999, 1e-08, 0.01, 10

NN = (((1,), (0,)), ((), ()))
NT = (((1,), (1,)), ((), ()))
TN = (((0,), (0,)), ((), ()))


def _dot(a, b, dims):
    return lax.dot_general(a, b, dims, preferred_element_type=F32)


def _tile(n, pref, align=LANES):
    if n <= pref:
        return n
    t = (pref // align) * align
    while t >= align:
        if n % t == 0:
            return t
        t -= align
    return n


def _params(sem=None):
    return pltpu.CompilerParams(dimension_semantics=sem, vmem_limit_bytes=VMEM_LIMIT)


def _sigmoid(x):
    return 1.0 / (1.0 + jnp.exp(-x))


def _my_index():
    return 4 * lax.axis_index("x") + 2 * lax.axis_index("y") + lax.axis_index("c")


def _exchange(x, *, gather, name):
    blk = x.shape if gather else x.shape[1:]

    def body(x_ref, y_ref, send_sems, recv_sems, local_sem):
        mx, my, mc = lax.axis_index("x"), lax.axis_index("y"), lax.axis_index("c")
        me = 4 * mx + 2 * my + mc

        def src(p):
            return x_ref if gather else x_ref.at[p]

        mine = pltpu.make_async_copy(src(me), y_ref.at[me], local_sem)
        mine.start()
        copies = []
        for k in range(1, N_DEV):
            px = (1 - mx) if (k >> 2) & 1 else mx
            py = (1 - my) if (k >> 1) & 1 else my
            pc = (1 - mc) if k & 1 else mc
            cp = pltpu.make_async_remote_copy(
                src_ref=src(4 * px + 2 * py + pc), dst_ref=y_ref.at[me],
                send_sem=send_sems.at[k - 1], recv_sem=recv_sems.at[k - 1],
                device_id=(px, py, pc), device_id_type=MESH)
            cp.start()
            copies.append(cp)
        for cp in copies:
            cp.wait()
        mine.wait()

    return pl.pallas_call(
        body, name=name,
        out_shape=jax.ShapeDtypeStruct((N_DEV,) + tuple(blk), x.dtype),
        in_specs=[pl.BlockSpec(memory_space=pl.ANY)],
        out_specs=pl.BlockSpec(memory_space=pl.ANY),
        scratch_shapes=[pltpu.SemaphoreType.DMA((N_DEV - 1,)), pltpu.SemaphoreType.DMA((N_DEV - 1,)),
                        pltpu.SemaphoreType.DMA(())],
    )(x)


_HBM = pl.BlockSpec(memory_space=pl.ANY)


def _gather2(x, name):
    def body(x_ref, y_ref, send_sems, recv_sems, local_sem):
        mx, my, mc = lax.axis_index("x"), lax.axis_index("y"), lax.axis_index("c")
        me, sib = (mx, my, mc), (mx, my, 1 - mc)
        chips = [(1 - mx, my), (mx, 1 - my), (1 - mx, 1 - my)]

        def slot(px, py, pc):
            return y_ref.at[4 * px + 2 * py + pc]

        def copy(k, block, to, src=None):
            return pltpu.make_async_remote_copy(
                src_ref=slot(*block) if src is None else src, dst_ref=slot(*block),
                send_sem=send_sems.at[k], recv_sem=recv_sems.at[k], device_id=to, device_id_type=MESH)

        mine = pltpu.make_async_copy(x_ref, slot(*me), local_sem)
        mine.start()
        first = [copy(0, me, sib, src=x_ref)] + [copy(1 + j, me, (*chip, mc), src=x_ref) for j, chip in enumerate(chips)]
        for cp in first:
            cp.start()
        passed = [copy(4 + j, (*chip, mc), sib) for j, chip in enumerate(chips)]
        for j, chip in enumerate(chips):
            copy(1 + j, (*chip, mc), me).wait_recv()
            passed[j].start()
        copy(0, sib, me).wait_recv()
        for j, chip in enumerate(chips):
            copy(4 + j, (*chip, 1 - mc), me).wait_recv()
        for cp in first + passed:
            cp.wait_send()
        mine.wait()

    return pl.pallas_call(
        body, name=name,
        out_shape=jax.ShapeDtypeStruct((N_DEV,) + tuple(x.shape), x.dtype),
        in_specs=[_HBM], out_specs=_HBM,
        scratch_shapes=[pltpu.SemaphoreType.DMA((7,)), pltpu.SemaphoreType.DMA((7,)), pltpu.SemaphoreType.DMA(())],
    )(x)


def _swap_sibling(x, name):
    def body(x_ref, y_ref, send_sem, recv_sem):
        sib = (lax.axis_index("x"), lax.axis_index("y"), 1 - lax.axis_index("c"))
        cp = pltpu.make_async_remote_copy(src_ref=x_ref, dst_ref=y_ref, send_sem=send_sem, recv_sem=recv_sem,
                                          device_id=sib, device_id_type=MESH)
        cp.start()
        cp.wait()

    return pl.pallas_call(
        body, name=name, out_shape=jax.ShapeDtypeStruct(x.shape, x.dtype), in_specs=[_HBM], out_specs=_HBM,
        scratch_shapes=[pltpu.SemaphoreType.DMA(()), pltpu.SemaphoreType.DMA(())],
    )(x)


def _exchange_chips(x, name):
    def body(x_ref, r_ref, send_sems, recv_sems, local_sem):
        mx, my, mc = lax.axis_index("x"), lax.axis_index("y"), lax.axis_index("c")
        here = 2 * mx + my
        mine = pltpu.make_async_copy(x_ref.at[here], r_ref.at[here], local_sem)
        mine.start()
        copies = []
        for j, (px, py) in enumerate([(1 - mx, my), (mx, 1 - my), (1 - mx, 1 - my)]):
            cp = pltpu.make_async_remote_copy(
                src_ref=x_ref.at[2 * px + py], dst_ref=r_ref.at[here], send_sem=send_sems.at[j], recv_sem=recv_sems.at[j],
                device_id=(px, py, mc), device_id_type=MESH)
            cp.start()
            copies.append(cp)
        for cp in copies:
            cp.wait()
        mine.wait()

    return pl.pallas_call(
        body, name=name, out_shape=jax.ShapeDtypeStruct(x.shape, x.dtype), in_specs=[_HBM], out_specs=_HBM,
        scratch_shapes=[pltpu.SemaphoreType.DMA((3,)), pltpu.SemaphoreType.DMA((3,)), pltpu.SemaphoreType.DMA(())],
    )(x)


def _pair_sum(a, b, name):
    R, C = a.shape
    tm = _tile(R, 512, 16)

    def body(a_ref, b_ref, o_ref):
        o_ref[...] = (a_ref[...].astype(F32) + b_ref[...].astype(F32)).astype(BF16)

    big = pl.BlockSpec((tm, C), lambda i: (i, 0))
    return pl.pallas_call(
        body, name=name, grid=(R // tm,), in_specs=[big, big], out_specs=big,
        out_shape=jax.ShapeDtypeStruct((R, C), BF16), compiler_params=_params(("parallel",)),
    )(a, b)


def _mm(pairs, mode, out_dtype, name, tm=512, tn=512):
    dims = {"nn": NN, "nt": NT, "tn": TN}[mode]
    a0, b0 = pairs[0][0], pairs[0][1]
    M = a0.shape[1] if mode == "tn" else a0.shape[0]
    N = b0.shape[0] if mode == "nt" else b0.shape[1]
    tm, tn = _tile(M, tm), _tile(N, tn)
    in_specs, args = [], []
    for pr in pairs:
        a, b = pr[0], pr[1]
        if mode == "tn":
            K = a.shape[0]
            in_specs.append(pl.BlockSpec((K, tm), lambda i, j: (0, i)))
            in_specs.append(pl.BlockSpec((K, tn), lambda i, j: (0, j)))
        elif mode == "nn":
            K = a.shape[1]
            in_specs.append(pl.BlockSpec((tm, K), lambda i, j: (i, 0)))
            in_specs.append(pl.BlockSpec((K, tn), lambda i, j: (0, j)))
        else:
            K = a.shape[1]
            cb = pr[2] if len(pr) > 2 else 0
            in_specs.append(pl.BlockSpec((tm, K), lambda i, j: (i, 0)))
            in_specs.append(pl.BlockSpec((tn, K), functools.partial(lambda i, j, cb: (j, cb), cb=cb)))
        args += [a, b]
    n_pairs = len(pairs)

    def body(*refs):
        o_ref = refs[-1]
        acc = None
        for k in range(n_pairs):
            d = _dot(refs[2 * k][...], refs[2 * k + 1][...], dims)
            acc = d if acc is None else acc + d
        o_ref[...] = acc.astype(o_ref.dtype)

    return pl.pallas_call(
        body, name=name, grid=(M // tm, N // tn), in_specs=in_specs,
        out_specs=pl.BlockSpec((tm, tn), lambda i, j: (i, j)),
        out_shape=jax.ShapeDtypeStruct((M, N), out_dtype),
        compiler_params=_params(("parallel", "arbitrary")),
    )(*args)


def _prenorm(x, g, sc, sh, name):
    T, D = x.shape
    tm = _tile(T, 512, 8)

    def body(x_ref, g_ref, sc_ref, sh_ref, h_ref):
        xv = x_ref[...]
        r = lax.rsqrt(jnp.mean(xv * xv, axis=-1, keepdims=True) + EPS)
        h_ref[...] = (((xv * r) * g_ref[...]) * (1.0 + sc_ref[...]) + sh_ref[...]).astype(BF16)

    row = pl.BlockSpec((1, D), lambda i: (0, 0))
    return pl.pallas_call(
        body, name=name, grid=(T // tm,),
        in_specs=[pl.BlockSpec((tm, D), lambda i: (i, 0)), row, row, row],
        out_specs=pl.BlockSpec((tm, D), lambda i: (i, 0)),
        out_shape=jax.ShapeDtypeStruct((T, D), BF16),
        compiler_params=_params(("parallel",)),
    )(x, g, sc, sh)


def _prenorm_bwd(dh, x, dres, g, sc, name):
    T, D = x.shape
    tm = _tile(T, 256, 8)

    def body(dh_ref, x_ref, dres_ref, g_ref, sc_ref, dx_ref, sm_ref):
        @pl.when(pl.program_id(0) == 0)
        def _():
            sm_ref[...] = jnp.zeros_like(sm_ref)

        xv, dhv = x_ref[...], dh_ref[...]
        r = lax.rsqrt(jnp.mean(xv * xv, axis=-1, keepdims=True) + EPS)
        xh = xv * r
        one_sc = 1.0 + sc_ref[...]
        sm_ref[0:1, :] += jnp.sum(dhv, axis=0, keepdims=True)
        sm_ref[1:2, :] += jnp.sum(dhv * (xh * g_ref[...]), axis=0, keepdims=True)
        sm_ref[2:3, :] += jnp.sum(dhv * one_sc * xh, axis=0, keepdims=True)
        dxh = dhv * one_sc * g_ref[...]
        dx_ref[...] = dres_ref[...] + r * (dxh - xh * jnp.mean(dxh * xh, axis=-1, keepdims=True))

    row = pl.BlockSpec((1, D), lambda i: (0, 0))
    big = pl.BlockSpec((tm, D), lambda i: (i, 0))
    return pl.pallas_call(
        body, name=name, grid=(T // tm,),
        in_specs=[big, big, big, row, row],
        out_specs=[big, pl.BlockSpec((8, D), lambda i: (0, 0))],
        out_shape=(jax.ShapeDtypeStruct((T, D), F32), jax.ShapeDtypeStruct((8, D), F32)),
        compiler_params=_params(("arbitrary",)),
    )(dh, x, dres, g, sc)


def _mm_postnorm(a, w, x, gate, gpost, name):
    T, K = a.shape
    D = w.shape[1]
    tm = _tile(T, 256, 8)

    def body(a_ref, w_ref, x_ref, gate_ref, gp_ref, y_ref, xn_ref):
        y = _dot(a_ref[...], w_ref[...], NN)
        r = lax.rsqrt(jnp.mean(y * y, axis=-1, keepdims=True) + EPS)
        y_ref[...] = y
        xn_ref[...] = x_ref[...] + gate_ref[...] * ((y * r) * gp_ref[...])

    row = pl.BlockSpec((1, D), lambda i: (0, 0))
    big = pl.BlockSpec((tm, D), lambda i: (i, 0))
    return pl.pallas_call(
        body, name=name, grid=(T // tm,),
        in_specs=[pl.BlockSpec((tm, K), lambda i: (i, 0)), pl.BlockSpec((K, D), lambda i: (0, 0)), big, row, row],
        out_specs=[big, big],
        out_shape=(jax.ShapeDtypeStruct((T, D), F32), jax.ShapeDtypeStruct((T, D), F32)),
        compiler_params=_params(("parallel",)),
    )(a, w, x, gate, gpost)


def _postnorm_bwd(dx, y, gate, gpost, name):
    T, D = y.shape
    tm = _tile(T, 256, 8)

    def body(dx_ref, y_ref, gate_ref, gp_ref, dy_ref, sm_ref):
        @pl.when(pl.program_id(0) == 0)
        def _():
            sm_ref[...] = jnp.zeros_like(sm_ref)

        yv, dxv = y_ref[...], dx_ref[...]
        r = lax.rsqrt(jnp.mean(yv * yv, axis=-1, keepdims=True) + EPS)
        yh = yv * r
        dn = dxv * gate_ref[...]
        sm_ref[0:1, :] += jnp.sum(dxv * (yh * gp_ref[...]), axis=0, keepdims=True)
        sm_ref[1:2, :] += jnp.sum(dn * yh, axis=0, keepdims=True)
        dyh = dn * gp_ref[...]
        dy_ref[...] = (r * (dyh - yh * jnp.mean(dyh * yh, axis=-1, keepdims=True))).astype(BF16)

    row = pl.BlockSpec((1, D), lambda i: (0, 0))
    big = pl.BlockSpec((tm, D), lambda i: (i, 0))
    return pl.pallas_call(
        body, name=name, grid=(T // tm,),
        in_specs=[big, big, row, row],
        out_specs=[big, pl.BlockSpec((8, D), lambda i: (0, 0))],
        out_shape=(jax.ShapeDtypeStruct((T, D), BF16), jax.ShapeDtypeStruct((8, D), F32)),
        compiler_params=_params(("arbitrary",)),
    )(dx, y, gate, gpost)


def _ffn_in_fwd(h, w, name):
    T, D = h.shape
    F = w.shape[1] // 2
    tm, tn = _tile(T, 1024), _tile(F, 256)
    nj = F // tn

    def body(h_ref, wg_ref, wu_ref, g_ref, u_ref, act_ref):
        hv = h_ref[...]
        g = _dot(hv, wg_ref[...], NN)
        u = _dot(hv, wu_ref[...], NN)
        g_ref[...] = g.astype(BF16)
        u_ref[...] = u.astype(BF16)
        act_ref[...] = ((g * _sigmoid(g)) * u).astype(BF16)

    out = pl.BlockSpec((tm, tn), lambda i, j: (i, j))
    return pl.pallas_call(
        body, name=name, grid=(T // tm, nj),
        in_specs=[pl.BlockSpec((tm, D), lambda i, j: (i, 0)),
                  pl.BlockSpec((D, tn), lambda i, j: (0, j)),
                  pl.BlockSpec((D, tn), lambda i, j: (0, j + nj))],
        out_specs=[out, out, out],
        out_shape=tuple(jax.ShapeDtypeStruct((T, F), BF16) for _ in range(3)),
        compiler_params=_params(("parallel", "arbitrary")),
    )(h, w, w)


def _ffn_act_bwd(dy, w_out, g, u, name):
    T, D = dy.shape
    F = w_out.shape[0]
    tm, tn = _tile(T, 1024), _tile(F, 256)

    def body(dy_ref, w_ref, g_ref, u_ref, dg_ref, du_ref):
        dact = _dot(dy_ref[...], w_ref[...], NT)
        gv, uv = g_ref[...].astype(F32), u_ref[...].astype(F32)
        sg = _sigmoid(gv)
        dg_ref[...] = (dact * uv * (sg * (1.0 + gv * (1.0 - sg)))).astype(BF16)
        du_ref[...] = (dact * (gv * sg)).astype(BF16)

    tile = pl.BlockSpec((tm, tn), lambda i, j: (i, j))
    return pl.pallas_call(
        body, name=name, grid=(T // tm, F // tn),
        in_specs=[pl.BlockSpec((tm, D), lambda i, j: (i, 0)), pl.BlockSpec((tn, D), lambda i, j: (j, 0)), tile, tile],
        out_specs=[tile, tile],
        out_shape=(jax.ShapeDtypeStruct((T, F), BF16), jax.ShapeDtypeStruct((T, F), BF16)),
        compiler_params=_params(("parallel", "arbitrary")),
    )(dy, w_out, g, u)


def _lane_scan(v, reverse):
    T = v.shape[-1]
    lane = lax.broadcasted_iota(jnp.int32, v.shape, 1)
    d = 1
    while d < T:
        if reverse:
            v = v + jnp.where(lane < T - d, pltpu.roll(v, T - d, axis=1), 0.0)
        else:
            v = v + jnp.where(lane >= d, pltpu.roll(v, d, axis=1), 0.0)
        d *= 2
    return v


def _fgate_fwd(h, wf_t, bf, name):
    T, D = h.shape
    R = wf_t.shape[0]

    def body(h_ref, w_ref, b_ref, fl_ref, cum_ref):
        fl = _dot(w_ref[...], h_ref[...], NT) + b_ref[...]
        fl_ref[...] = fl
        logf = jnp.minimum(fl, 0.0) - jnp.log(1.0 + jnp.exp(-jnp.abs(fl)))
        cum_ref[...] = _lane_scan(logf, reverse=False)

    return pl.pallas_call(
        body, name=name,
        out_shape=(jax.ShapeDtypeStruct((R, T), F32), jax.ShapeDtypeStruct((R, T), F32)),
        compiler_params=_params(),
    )(h, wf_t, bf)


def _fgate_bwd(dcum, fl, h, name):
    R, T = fl.shape
    D = h.shape[1]

    def body(dc_ref, fl_ref, h_ref, dfl_ref, dw_ref, db_ref):
        dlogf = _lane_scan(dc_ref[...], reverse=True)
        dfl = dlogf * _sigmoid(-fl_ref[...])
        dfl_ref[...] = dfl
        dw_ref[...] = _dot(dfl.astype(BF16), h_ref[...], NN)
        db_ref[...] = jnp.broadcast_to(jnp.sum(dfl, axis=-1, keepdims=True), (R, LANES))

    return pl.pallas_call(
        body, name=name,
        out_shape=(jax.ShapeDtypeStruct((R, T), F32), jax.ShapeDtypeStruct((R, D), F32),
                   jax.ShapeDtypeStruct((R, LANES), F32)),
        compiler_params=_params(),
    )(dcum, fl, h)


def _head_masks(hpb, dh, rows):
    lane = lax.broadcasted_iota(jnp.int32, (rows, LANES), 1)
    return [(lane >= h * dh) & (lane < (h + 1) * dh) for h in range(hpb)]


def _stack_heads(v, masks):
    return jnp.concatenate([jnp.where(mk, v, jnp.zeros_like(v)) for mk in masks], axis=0)


def _heads_to_lanes(col, masks, tq):
    out = jnp.broadcast_to(col[0:tq], (tq, LANES))
    for h in range(1, len(masks)):
        out = jnp.where(masks[h], col[h * tq:(h + 1) * tq], out)
    return out


def _causal_stack(hpb, tq):
    r = lax.broadcasted_iota(jnp.int32, (tq, tq), 0)
    c = lax.broadcasted_iota(jnp.int32, (tq, tq), 1)
    return jnp.concatenate([c] * hpb, axis=0) <= jnp.concatenate([r] * hpb, axis=0)


def _attn_fwd(qkv, cum_rows, n_heads, name, tq):
    T = qkv.shape[0]
    A = qkv.shape[1] // 3
    dh = A // n_heads
    hpb = LANES // dh
    nb = A // LANES
    nq = T // tq
    scale = dh ** -0.5

    def body(q_ref, k_ref, v_ref, c_ref, o_ref, l_ref, vbd):
        hp, i = pl.program_id(0), pl.program_id(1)
        masks = _head_masks(hpb, dh, tq)

        @pl.when(i == 0)
        def _():
            def fill(j, _):
                vbd[j] = _stack_heads(v_ref[pl.ds(pl.multiple_of(j * tq, tq), tq), :], masks)
                return 0

            lax.fori_loop(0, nq, fill, 0)

        qs = _stack_heads(q_ref[...], masks)
        crow0 = hp * hpb * nq

        def tile(j, carry, diag):
            m, l, acc = carry
            kt = k_ref[pl.ds(pl.multiple_of(j * tq, tq), tq), :]
            bias = jnp.concatenate(
                [jnp.broadcast_to(c_ref[pl.ds(crow0 + h * nq + j, 1), :], (tq, tq)) for h in range(hpb)], axis=0)
            s = _dot(qs, kt, NT) * scale - bias
            if diag:
                s = jnp.where(_causal_stack(hpb, tq), s, NEG)
            m_new = jnp.maximum(m, jnp.max(s, axis=-1, keepdims=True))
            p = jnp.exp(s - m_new)
            alpha = jnp.exp(m - m_new)
            l = alpha * l + jnp.sum(p, axis=-1, keepdims=True)
            pcat = jnp.concatenate([p[h * tq:(h + 1) * tq] for h in range(hpb)], axis=1).astype(BF16)
            acc = _heads_to_lanes(alpha, masks, tq) * acc + _dot(pcat, vbd[j], NN)
            return m_new, l, acc

        init = (jnp.full((hpb * tq, 1), NEG, F32), jnp.zeros((hpb * tq, 1), F32), jnp.zeros((tq, LANES), F32))
        carry = lax.fori_loop(0, i, lambda j, c: tile(j, c, False), init)
        m, l, acc = tile(i, carry, True)
        o_ref[...] = (acc / _heads_to_lanes(l, masks, tq)).astype(BF16)
        lse = m + jnp.log(l)
        for h in range(hpb):
            l_ref[:, h:h + 1] = lse[h * tq:(h + 1) * tq]

    return pl.pallas_call(
        body, name=name, grid=(nb, nq),
        in_specs=[pl.BlockSpec((tq, LANES), lambda h, i: (i, h)),
                  pl.BlockSpec((T, LANES), lambda h, i: (0, nb + h)),
                  pl.BlockSpec((T, LANES), lambda h, i: (0, 2 * nb + h)),
                  pl.BlockSpec(cum_rows.shape, lambda h, i: (0, 0))],
        out_specs=[pl.BlockSpec((tq, LANES), lambda h, i: (i, h)),
                   pl.BlockSpec((None, tq, hpb), lambda h, i: (h, i, 0))],
        out_shape=(jax.ShapeDtypeStruct((T, A), BF16), jax.ShapeDtypeStruct((nb, T, hpb), F32)),
        scratch_shapes=[pltpu.VMEM((nq, hpb * tq, LANES), BF16)],
        compiler_params=_params(("arbitrary", "arbitrary")),
    )(qkv, qkv, qkv, cum_rows)


def _attn_bwd(qkv, dcat, o, lse, cum_rows, n_heads, name, tq):
    T = qkv.shape[0]
    A = qkv.shape[1] // 3
    dh = A // n_heads
    hpb = LANES // dh
    nb = A // LANES
    nq = T // tq
    scale = dh ** -0.5

    def body(q_ref, k_ref, v_ref, do_ref, o_ref, l_ref, c_ref, dqkv_ref, dc_ref, dr_ref,
             dq_acc, delta, drow, qs_scr, dos_scr, kbd_scr):
        hp = pl.program_id(0)
        masks = _head_masks(hpb, dh, tq)
        crow0 = hp * hpb * nq

        def prologue(i, _):
            rs = pl.ds(pl.multiple_of(i * tq, tq), tq)
            do = do_ref[rs, :]
            prod = do * o_ref[rs, :].astype(F32)
            for h in range(hpb):
                delta[rs, h:h + 1] = jnp.sum(jnp.where(masks[h], prod, 0.0), axis=-1, keepdims=True)
            qs_scr[i] = _stack_heads(q_ref[rs, :], masks)
            dos_scr[i] = _stack_heads(do, masks).astype(BF16)
            kbd_scr[i] = _stack_heads(k_ref[rs, :], masks)
            dq_acc[rs, :] = jnp.zeros((tq, LANES), F32)
            drow[rs, :] = jnp.zeros((tq, hpb), F32)
            return 0

        lax.fori_loop(0, nq, prologue, 0)

        def kv_step(j, _):
            ks = pl.ds(pl.multiple_of(j * tq, tq), tq)
            kt, vt = k_ref[ks, :], v_ref[ks, :]
            kbd = kbd_scr[j]
            bias = jnp.concatenate(
                [jnp.broadcast_to(c_ref[pl.ds(crow0 + h * nq + j, 1), :], (tq, tq)) for h in range(hpb)], axis=0)

            def q_step(i, carry, diag):
                dk, dv, dcs = carry
                rs = pl.ds(pl.multiple_of(i * tq, tq), tq)
                qs, dos = qs_scr[i], dos_scr[i]
                s = _dot(qs, kt, NT) * scale - bias
                if diag:
                    s = jnp.where(_causal_stack(hpb, tq), s, NEG)
                lse = jnp.concatenate([l_ref[rs, h:h + 1] for h in range(hpb)], axis=0)
                p = jnp.exp(s - lse)
                dv = dv + _dot(p.astype(BF16), dos, TN)
                dp = _dot(dos, vt, NT)
                ds = p * (dp - jnp.concatenate([delta[rs, h:h + 1] for h in range(hpb)], axis=0))
                dcs = tuple(dcs[h] - jnp.sum(ds[h * tq:(h + 1) * tq], axis=0, keepdims=True) for h in range(hpb))
                rsum = jnp.sum(ds, axis=-1, keepdims=True)
                for h in range(hpb):
                    drow[rs, h:h + 1] += rsum[h * tq:(h + 1) * tq]
                dsb = (ds * scale).astype(BF16)
                dk = dk + _dot(dsb, qs, TN)
                dscat = jnp.concatenate([dsb[h * tq:(h + 1) * tq] for h in range(hpb)], axis=1)
                dq_acc[rs, :] += _dot(dscat, kbd, NN)
                return dk, dv, dcs

            init = (jnp.zeros((tq, LANES), F32), jnp.zeros((tq, LANES), F32),
                    tuple(jnp.zeros((1, tq), F32) for _ in range(hpb)))
            carry = q_step(j, init, True)
            dk, dv, dcs = lax.fori_loop(j + 1, nq, lambda i, c: q_step(i, c, False), carry)
            dqkv_ref[1, ks, :] = dk.astype(BF16)
            dqkv_ref[2, ks, :] = dv.astype(BF16)
            for h in range(hpb):
                dc_ref[pl.ds(crow0 + h * nq + j, 1), :] = dcs[h]
            return 0

        lax.fori_loop(0, nq, kv_step, 0)
        dqkv_ref[0] = dq_acc[...].astype(BF16)
        dr_ref[...] = drow[...]

    col = lambda off: pl.BlockSpec((T, LANES), functools.partial(lambda h, off: (0, off + h), off=off))
    return pl.pallas_call(
        body, name=name, grid=(nb,),
        in_specs=[col(0), col(nb), col(2 * nb), col(0), col(0),
                  pl.BlockSpec((None, T, hpb), lambda h: (h, 0, 0)),
                  pl.BlockSpec(cum_rows.shape, lambda h: (0, 0))],
        out_specs=[pl.BlockSpec((3, T, LANES), lambda h: (0, 0, h)),
                   pl.BlockSpec(cum_rows.shape, lambda h: (0, 0)),
                   pl.BlockSpec((None, T, hpb), lambda h: (h, 0, 0))],
        out_shape=(jax.ShapeDtypeStruct((3, T, A), BF16), jax.ShapeDtypeStruct(cum_rows.shape, F32),
                   jax.ShapeDtypeStruct((nb, T, hpb), F32)),
        scratch_shapes=[pltpu.VMEM((T, LANES), F32), pltpu.VMEM((T, hpb), F32), pltpu.VMEM((T, hpb), F32),
                        pltpu.VMEM((nq, hpb * tq, LANES), BF16), pltpu.VMEM((nq, hpb * tq, LANES), BF16),
                        pltpu.VMEM((nq, hpb * tq, LANES), BF16)],
        compiler_params=_params(("arbitrary",)),
    )(qkv, qkv, qkv, dcat, o, lse, cum_rows)


def _glu_into(upad, cv_ref, cg_ref, T):
    upad[0:CONV_PAD, :] = jnp.zeros((CONV_PAD, upad.shape[1]), F32)

    def fill(c, _):
        rs = pl.ds(pl.multiple_of(c * CONV_CHUNK, CONV_CHUNK), CONV_CHUNK)
        upad[pl.ds(pl.multiple_of(CONV_PAD + c * CONV_CHUNK, 8), CONV_CHUNK), :] = cv_ref[rs, :] * _sigmoid(cg_ref[rs, :])
        return 0

    lax.fori_loop(0, T // CONV_CHUNK, fill, 0)


def _conv_taps(win, w_ref, first, step):
    acc = None
    for k in range(CONV_K):
        o = first + step * k
        t = w_ref[k:k + 1, :] * win[o:o + CONV_CHUNK, :]
        acc = t if acc is None else acc + t
    return acc


def _conv_fwd(cproj, w, b, lg, lb, name):
    T = cproj.shape[0]
    C = cproj.shape[1] // 2
    off = CONV_PAD - (CONV_K - 1)

    def body(cv_ref, cg_ref, w_ref, b_ref, lg_ref, lb_ref, out_ref, upad, win):
        _glu_into(upad, cv_ref, cg_ref, T)

        def chunk(c, _):
            r0 = pl.multiple_of(c * CONV_CHUNK, CONV_CHUNK)
            win[...] = upad[pl.ds(r0, CONV_CHUNK + CONV_PAD), :]
            u1 = _conv_taps(win, w_ref, off, 1) + b_ref[...]
            mu = jnp.mean(u1, axis=-1, keepdims=True)
            var = jnp.mean(jnp.square(u1 - mu), axis=-1, keepdims=True)
            u2 = ((u1 - mu) * lax.rsqrt(var + EPS)) * lg_ref[...] + lb_ref[...]
            out_ref[pl.ds(r0, CONV_CHUNK), :] = (u2 * _sigmoid(u2)).astype(BF16)
            return 0

        lax.fori_loop(0, T // CONV_CHUNK, chunk, 0)

    row = pl.BlockSpec((1, C), lambda i: (0, 0))
    return pl.pallas_call(
        body, name=name, grid=(1,),
        in_specs=[pl.BlockSpec((T, C), lambda i: (0, 0)), pl.BlockSpec((T, C), lambda i: (0, 1)),
                  pl.BlockSpec(w.shape, lambda i: (0, 0)), row, row, row],
        out_specs=pl.BlockSpec((T, C), lambda i: (0, 0)),
        out_shape=jax.ShapeDtypeStruct((T, C), BF16),
        scratch_shapes=[pltpu.VMEM((T + CONV_PAD, C), F32), pltpu.VMEM((CONV_CHUNK + CONV_PAD, C), F32)],
        compiler_params=_params(("arbitrary",)),
    )(cproj, cproj, w, b, lg, lb)


def _conv_bwd(cproj, dcat, w, b, lg, lb, name):
    T = cproj.shape[0]
    C = cproj.shape[1] // 2
    off = CONV_PAD - (CONV_K - 1)
    n_chunks = T // CONV_CHUNK

    def fold(v):
        return jnp.sum(v.reshape(CONV_CHUNK // 8, 8, C), axis=0)

    def body(cv_ref, cg_ref, du_ref, w_ref, b_ref, lg_ref, lb_ref, dc_ref, dw_ref, sm_ref, upad, dpad, win, dwacc, smacc):
        _glu_into(upad, cv_ref, cg_ref, T)
        dpad[pl.ds(T, CONV_PAD), :] = jnp.zeros((CONV_PAD, C), F32)
        dwacc[...] = jnp.zeros_like(dwacc)
        smacc[...] = jnp.zeros_like(smacc)

        def chunk_a(c, _):
            r0 = pl.multiple_of(c * CONV_CHUNK, CONV_CHUNK)
            win[...] = upad[pl.ds(r0, CONV_CHUNK + CONV_PAD), :]
            u1 = _conv_taps(win, w_ref, off, 1) + b_ref[...]
            mu = jnp.mean(u1, axis=-1, keepdims=True)
            var = jnp.mean(jnp.square(u1 - mu), axis=-1, keepdims=True)
            rstd = lax.rsqrt(var + EPS)
            u1h = (u1 - mu) * rstd
            u2 = u1h * lg_ref[...] + lb_ref[...]
            sg = _sigmoid(u2)
            du2 = du_ref[pl.ds(r0, CONV_CHUNK), :] * (sg * (1.0 + u2 * (1.0 - sg)))
            smacc[8:16, :] += fold(du2 * u1h)
            smacc[16:24, :] += fold(du2)
            du1h = du2 * lg_ref[...]
            du1 = rstd * (du1h - jnp.mean(du1h, axis=-1, keepdims=True)
                          - u1h * jnp.mean(du1h * u1h, axis=-1, keepdims=True))
            smacc[0:8, :] += fold(du1)
            dpad[pl.ds(r0, CONV_CHUNK), :] = du1
            for k in range(CONV_K):
                dwacc[8 * k:8 * k + 8, :] += fold(du1 * win[off + k:off + k + CONV_CHUNK, :])
            return 0

        lax.fori_loop(0, n_chunks, chunk_a, 0)

        def chunk_b(c, _):
            r0 = pl.multiple_of(c * CONV_CHUNK, CONV_CHUNK)
            rs = pl.ds(r0, CONV_CHUNK)
            win[...] = dpad[pl.ds(r0, CONV_CHUNK + CONV_PAD), :]
            du0 = _conv_taps(win, w_ref, CONV_K - 1, -1)
            cv, sg = cv_ref[rs, :], _sigmoid(cg_ref[rs, :])
            dc_ref[rs, 0:C] = (du0 * sg).astype(BF16)
            dc_ref[rs, C:2 * C] = (du0 * cv * (sg * (1.0 - sg))).astype(BF16)
            return 0

        lax.fori_loop(0, n_chunks, chunk_b, 0)
        for k in range(CONV_K):
            dw_ref[k:k + 1, :] = jnp.sum(dwacc[8 * k:8 * k + 8, :], axis=0, keepdims=True)
        dw_ref[CONV_K:CONV_PAD, :] = jnp.zeros((CONV_PAD - CONV_K, C), F32)
        for r in range(3):
            sm_ref[r:r + 1, :] = jnp.sum(smacc[8 * r:8 * r + 8, :], axis=0, keepdims=True)
        sm_ref[3:8, :] = jnp.zeros((5, C), F32)

    row = pl.BlockSpec((1, C), lambda i: (0, 0))
    return pl.pallas_call(
        body, name=name, grid=(1,),
        in_specs=[pl.BlockSpec((T, C), lambda i: (0, 0)), pl.BlockSpec((T, C), lambda i: (0, 1)),
                  pl.BlockSpec((T, C), lambda i: (0, 1)),
                  pl.BlockSpec(w.shape, lambda i: (0, 0)), row, row, row],
        out_specs=[pl.BlockSpec((T, 2 * C), lambda i: (0, 0)), pl.BlockSpec((CONV_PAD, C), lambda i: (0, 0)),
                   pl.BlockSpec((8, C), lambda i: (0, 0))],
        out_shape=(jax.ShapeDtypeStruct((T, 2 * C), BF16), jax.ShapeDtypeStruct((CONV_PAD, C), F32),
                   jax.ShapeDtypeStruct((8, C), F32)),
        scratch_shapes=[pltpu.VMEM((T + CONV_PAD, C), F32), pltpu.VMEM((T + CONV_PAD, C), F32),
                        pltpu.VMEM((CONV_CHUNK + CONV_PAD, C), F32), pltpu.VMEM((8 * CONV_PAD, C), F32),
                        pltpu.VMEM((24, C), F32)],
        compiler_params=_params(("arbitrary",)),
    )(cproj, cproj, dcat, w, b, lg, lb)


def _loss_head(x, target, name):
    T, D = x.shape
    tm = _tile(T, 512, 8)

    def body(x_ref, t_ref, loss_ref, dx_ref):
        @pl.when(pl.program_id(0) == 0)
        def _():
            loss_ref[...] = jnp.zeros_like(loss_ref)

        err = x_ref[...] - t_ref[...]
        part = jnp.sum(jnp.mean(err * err, axis=-1, keepdims=True), axis=0, keepdims=True)
        loss_ref[...] += jnp.broadcast_to(0.5 * part, loss_ref.shape)
        dx_ref[...] = err * (1.0 / D)

    big = pl.BlockSpec((tm, D), lambda i: (i, 0))
    return pl.pallas_call(
        body, name=name, grid=(T // tm,),
        in_specs=[big, big],
        out_specs=[pl.BlockSpec((8, LANES), lambda i: (0, 0)), big],
        out_shape=(jax.ShapeDtypeStruct((8, LANES), F32), jax.ShapeDtypeStruct((T, D), F32)),
        compiler_params=_params(("arbitrary",)),
    )(x, target)


def _ada_fwd(c_all, ada_w, ada_b_loc, name):
    L, D, S = ada_w.shape
    B = c_all.shape[0]

    def body(c_ref, w_ref, b_ref, o_ref):
        c = c_ref[...]
        ca = (c * _sigmoid(c)).astype(BF16)
        o_ref[...] = _dot(ca, w_ref[...].astype(BF16), NN) + b_ref[...]

    return pl.pallas_call(
        body, name=name, grid=(L,),
        in_specs=[pl.BlockSpec((B, D), lambda l: (0, 0)), pl.BlockSpec((None, D, S), lambda l: (l, 0, 0)),
                  pl.BlockSpec((None, 1, S), lambda l: (l, 0, 0))],
        out_specs=pl.BlockSpec((None, B, S), lambda l: (l, 0, 0)),
        out_shape=jax.ShapeDtypeStruct((L, B, S), F32),
        compiler_params=_params(("parallel",)),
    )(c_all, ada_w, ada_b_loc)


def _ada_bwd(c_all_t, dmod_loc, name):
    D, B = c_all_t.shape
    L, _, S = dmod_loc.shape

    def body(c_ref, dm_ref, o_ref):
        c = c_ref[...]
        ca = c * _sigmoid(c)
        acc = None
        for bb in range(B):
            t = ca[:, bb:bb + 1] * dm_ref[bb:bb + 1, :]
            acc = t if acc is None else acc + t
        o_ref[...] = acc

    return pl.pallas_call(
        body, name=name, grid=(L,),
        in_specs=[pl.BlockSpec((D, B), lambda l: (0, 0)), pl.BlockSpec((None, B, S), lambda l: (l, 0, 0))],
        out_specs=pl.BlockSpec((None, D, S), lambda l: (l, 0, 0)),
        out_shape=jax.ShapeDtypeStruct((L, D, S), F32),
        compiler_params=_params(("parallel",)),
    )(c_all_t, dmod_loc)


def _sum_devices(parts, name):
    _, R, C = parts.shape
    tm = _tile(R, 256, 8)

    def body(p_ref, o_ref):
        acc = p_ref[0].astype(F32)
        for d in range(1, N_DEV):
            acc = acc + p_ref[d].astype(F32)
        o_ref[...] = acc

    return pl.pallas_call(
        body, name=name, grid=(R // tm,),
        in_specs=[pl.BlockSpec((N_DEV, tm, C), lambda i: (0, i, 0))],
        out_specs=pl.BlockSpec((tm, C), lambda i: (i, 0)),
        out_shape=jax.ShapeDtypeStruct((R, C), F32),
        compiler_params=_params(("parallel",)),
    )(parts)


def _adamw_math(w, g, m, v):
    m = ADAM_B1 * m + (1.0 - ADAM_B1) * g
    v = ADAM_B2 * v + (1.0 - ADAM_B2) * (g * g)
    m_hat = m / (1.0 - ADAM_B1 ** ADAM_STEP)
    v_hat = v / (1.0 - ADAM_B2 ** ADAM_STEP)
    delta = -ADAM_LR * (m_hat / (jnp.sqrt(v_hat) + ADAM_EPS) + ADAM_WD * w)
    return delta, m, v


def _adamw(w, g, m, v, name, summed):
    R, C = w.shape
    tm = _tile(R, 256, 16)
    n_parts = g.shape[0] if summed else 0

    def body(w_ref, g_ref, m_ref, v_ref, go_ref, d_ref, mo_ref, vo_ref):
        if summed:
            g = g_ref[0].astype(F32)
            for d in range(1, n_parts):
                g = g + g_ref[d].astype(F32)
        else:
            g = g_ref[...]
        delta, mn, vn = _adamw_math(w_ref[...], g, m_ref[...], v_ref[...])
        go_ref[...] = g
        d_ref[...] = delta
        mo_ref[...] = mn
        vo_ref[...] = vn

    big = pl.BlockSpec((tm, C), lambda i: (i, 0))
    gspec = pl.BlockSpec((n_parts, tm, C), lambda i: (0, i, 0)) if summed else big
    return pl.pallas_call(
        body, name=name, grid=(R // tm,),
        in_specs=[big, gspec, big, big],
        out_specs=[big, big, big, big],
        out_shape=tuple(jax.ShapeDtypeStruct((R, C), F32) for _ in range(4)),
        compiler_params=_params(("parallel",)),
    )(w, g, m, v)


def _pack(arrs, D):
    L = arrs[0].shape[0]
    cols = []
    for a in arrs:
        f = a.reshape(L, -1)
        n = f.shape[1]
        cols.append(jnp.pad(f, ((0, 0), (0, -(-n // D) * D - n))))
    flat = jnp.concatenate(cols, axis=1)
    return flat.reshape(L, flat.shape[1] // D, D)


def _unpack(p, shapes, D):
    L = p.shape[0]
    out, r = [], 0
    for s in shapes:
        n = math.prod(s[1:])
        rows = -(-n // D)
        out.append(p[:, r:r + rows].reshape(L, rows * D)[:, :n].reshape(s))
        r += rows
    return out


def kernel(x, c, w_in, b_f, conv_w, conv_b, conv_ln_g, conv_ln_b, w_o, w_ffn_in, w_ffn_out, mix_pre_g, mix_post_g, ffn_pre_g, ffn_post_g, ada_w, ada_b, loss_target, m_w_in, m_b_f, m_conv_w, m_conv_b, m_conv_ln_g, m_conv_ln_b, m_w_o, m_w_ffn_in, m_w_ffn_out, m_mix_pre_g, m_mix_post_g, m_ffn_pre_g, m_ffn_post_g, m_ada_w, m_ada_b, v_w_in, v_b_f, v_conv_w, v_conv_b, v_conv_ln_g, v_conv_ln_b, v_w_o, v_w_ffn_in, v_w_ffn_out, v_mix_pre_g, v_mix_post_g, v_ffn_pre_g, v_ffn_post_g, v_ada_w, v_ada_b):
    L, D, s_in = w_in.shape
    T = x.shape[1]
    H = b_f.shape[1]
    A = D // 2
    C = D - A
    cs = conv_w.shape[2]
    F = w_ffn_out.shape[1] * N_DEV
    s_ff = w_ffn_in.shape[2]
    s_ada = ada_w.shape[2]
    R = 16
    me = _my_index()
    x0 = x[0]
    target = loss_target[0]
    tq = _tile(T, LANES)
    nq = T // tq

    def gathered_cols(w, name):
        g = _gather2(w.astype(BF16), name)
        return jnp.transpose(g, (1, 2, 0, 3)).reshape(w.shape[0], w.shape[1], N_DEV * w.shape[2])

    def gathered_rows(w, name):
        g = _gather2(w.astype(BF16), name)
        return jnp.transpose(g, (1, 0, 2, 3)).reshape(w.shape[0], N_DEV * w.shape[1], w.shape[2])

    W_in = gathered_cols(w_in, "gather_w_in")
    W_ffn_in = gathered_cols(w_ffn_in, "gather_w_ffn_in")
    W_o = gathered_rows(w_o, "gather_w_o")
    W_ffn_out = gathered_rows(w_ffn_out, "gather_w_ffn_out")
    cw_g = _exchange(conv_w, gather=True, name="gather_conv_w")
    conv_w_full = jnp.transpose(cw_g, (1, 2, 0, 3)).reshape(L, CONV_K, C)
    conv_w_pad = jnp.pad(conv_w_full, ((0, 0), (0, CONV_PAD - CONV_K), (0, 0)))
    W_qkv = W_in[:, :, :3 * A]
    W_f = W_in[:, :, 3 * A:3 * A + H]
    W_c = W_in[:, :, 3 * A + H:]
    W_f_t = jnp.pad(jnp.transpose(W_f, (0, 2, 1)), ((0, 0), (0, R - H), (0, 0)))
    W_f_pad = jnp.pad(W_f, ((0, 0), (0, 0), (0, R - H)))
    b_f_col = jnp.pad(b_f, ((0, 0), (0, R - H)))[:, :, None]

    c_all = _exchange(c, gather=True, name="gather_c").reshape(N_DEV, D)
    ada_b_loc = lax.dynamic_slice_in_dim(ada_b, me * s_ada, s_ada, axis=1)[:, None, :]
    mod_loc = _ada_fwd(c_all, ada_w, ada_b_loc, "ada_fwd")
    mod_g = _exchange(mod_loc, gather=True, name="gather_mod")
    mod = lax.dynamic_index_in_dim(mod_g, me, axis=2, keepdims=False)
    mod = jnp.transpose(mod, (1, 0, 2)).reshape(L, N_MOD, 1, D)

    saved = []
    xc = x0
    for l in range(L):
        sh1, sc1, g1, sh2, sc2, g2 = (mod[l, k] for k in range(N_MOD))
        gpre1, gpost1, gpre2, gpost2 = (p[l][None, :] for p in (mix_pre_g, mix_post_g, ffn_pre_g, ffn_post_g))
        h1 = _prenorm(xc, gpre1, sc1, sh1, f"prenorm1_{l}")
        qkv = _mm([(h1, W_qkv[l])], "nn", BF16, f"proj_qkv_{l}")
        cproj = _mm([(h1, W_c[l])], "nn", F32, f"proj_conv_{l}")
        fl, cum = _fgate_fwd(h1, W_f_t[l], b_f_col[l], f"fgate_{l}")
        cum_rows = cum[:H].reshape(H * nq, tq)
        o, lse = _attn_fwd(qkv, cum_rows, H, f"attn_{l}", tq)
        u3 = _conv_fwd(cproj, conv_w_pad[l], conv_b[l][None, :], conv_ln_g[l][None, :], conv_ln_b[l][None, :], f"conv_{l}")
        cat = jnp.concatenate([o, u3], axis=-1)
        y1, x_mid = _mm_postnorm(cat, W_o[l], xc, g1, gpost1, f"out_proj_{l}")
        h2 = _prenorm(x_mid, gpre2, sc2, sh2, f"prenorm2_{l}")
        g, u, act = _ffn_in_fwd(h2, W_ffn_in[l], f"ffn_in_{l}")
        y2, x_out = _mm_postnorm(act, W_ffn_out[l], x_mid, g2, gpost2, f"ffn_out_{l}")
        saved.append((xc, h1, qkv, cproj, fl, cum_rows, o, lse, cat, y1, x_mid, h2, g, u, act, y2))
        xc = x_out

    loss_tile, dx = _loss_head(xc, target, "loss_head")
    loss = lax.psum(loss_tile[0, 0], ("x", "y", "c"))

    dW_in, dW_o, dW_ffn_in, dW_ffn_out, small = [None] * L, [None] * L, [None] * L, [None] * L, [None] * L
    for l in reversed(range(L)):
        xin, h1, qkv, cproj, fl, cum_rows, o, lse, cat, y1, x_mid, h2, g, u, act, y2 = saved[l]
        sh1, sc1, g1, sh2, sc2, g2 = (mod[l, k] for k in range(N_MOD))
        gpre1, gpost1, gpre2, gpost2 = (p[l][None, :] for p in (mix_pre_g, mix_post_g, ffn_pre_g, ffn_post_g))
        dy2, sm_post2 = _postnorm_bwd(dx, y2, g2, gpost2, f"postnorm2_bwd_{l}")
        dgate, dup = _ffn_act_bwd(dy2, W_ffn_out[l], g, u, f"ffn_act_bwd_{l}")
        dW_ffn_out[l] = _mm([(act, dy2)], "tn", BF16, f"dw_ffn_out_{l}")
        dh2 = _mm([(dgate, W_ffn_in[l], 0), (dup, W_ffn_in[l], 1)], "nt", F32, f"dh2_{l}")
        dWg = _mm([(h2, dgate)], "tn", BF16, f"dw_ffn_gate_{l}")
        dWu = _mm([(h2, dup)], "tn", BF16, f"dw_ffn_up_{l}")
        dW_ffn_in[l] = jnp.concatenate([dWg, dWu], axis=1)
        dx_mid, sm_pre2 = _prenorm_bwd(dh2, x_mid, dx, gpre2, sc2, f"prenorm2_bwd_{l}")
        dy1, sm_post1 = _postnorm_bwd(dx_mid, y1, g1, gpost1, f"postnorm1_bwd_{l}")
        dcat = _mm([(dy1, W_o[l])], "nt", F32, f"dcat_{l}")
        dW_o[l] = _mm([(cat, dy1)], "tn", BF16, f"dw_o_{l}")
        dqkv, dcum_rows, dcum_q = _attn_bwd(qkv, dcat, o, lse, cum_rows, H, f"attn_bwd_{l}", tq)
        dcproj, dconv_w, sm_conv = _conv_bwd(cproj, dcat, conv_w_pad[l], conv_b[l][None, :], conv_ln_g[l][None, :],
                                             conv_ln_b[l][None, :], f"conv_bwd_{l}")
        dcum = dcum_rows.reshape(H, T) + jnp.transpose(dcum_q, (0, 2, 1)).reshape(H, T)
        dcum = jnp.pad(dcum, ((0, R - H), (0, 0)))
        dfl_t, dwf_t, dbf = _fgate_bwd(dcum, fl, h1, f"fgate_bwd_{l}")
        dfl = jnp.transpose(dfl_t).astype(BF16)
        dh1 = _mm([(dqkv[0], W_qkv[l], 0), (dqkv[1], W_qkv[l], 1), (dqkv[2], W_qkv[l], 2),
                   (dfl, W_f_pad[l]), (dcproj, W_c[l])], "nt", F32, f"dh1_{l}")
        dWq = [_mm([(h1, dqkv[k])], "tn", BF16, f"dw_qkv{k}_{l}") for k in range(3)]
        dWc = _mm([(h1, dcproj)], "tn", BF16, f"dw_conv_{l}")
        dWf = jnp.transpose(dwf_t[:H]).astype(BF16)
        dW_in[l] = jnp.concatenate(dWq + [dWf, dWc], axis=1)
        dx, sm_pre1 = _prenorm_bwd(dh1, xin, dx_mid, gpre1, sc1, f"prenorm1_bwd_{l}")
        dmod = jnp.stack([sm_pre1[0], sm_pre1[1], sm_post1[0], sm_pre2[0], sm_pre2[1], sm_post2[0]])
        small[l] = (dmod, sm_pre1[2], sm_post1[1], sm_pre2[2], sm_post2[1], sm_conv[0], sm_conv[1], sm_conv[2],
                    dbf[:H, 0], dconv_w[:CONV_K])
    grad_x = dx[None]

    small_names = 10
    small_l = [jnp.stack([small[l][k] for l in range(L)]) for k in range(small_names)]
    small_shapes = [a.shape for a in small_l]
    packed = _pack(small_l, D)
    rows = packed.shape[1]
    rows_pad = -(-L * rows // 8) * 8
    packed2 = jnp.pad(packed.reshape(L * rows, D), ((0, rows_pad - L * rows), (0, 0)))
    small_g = _exchange(packed2, gather=True, name="gather_small")
    small_sum = _sum_devices(small_g, "sum_small")[:L * rows].reshape(L, rows, D)
    (g_ada_b6, g_mix_pre, g_mix_post, g_ffn_pre, g_ffn_post, g_conv_b, g_ln_g, g_ln_b, g_b_f,
     g_conv_w_full) = _unpack(small_sum, small_shapes, D)
    g_ada_b = g_ada_b6.reshape(L, N_MOD * D)
    g_conv_w = lax.dynamic_slice_in_dim(g_conv_w_full, me * cs, cs, axis=2)
    dmod_all = small_g[:, :L * rows].reshape(N_DEV, L, rows, D)[:, :, :N_MOD].reshape(N_DEV, L, N_MOD * D)
    dmod_loc = jnp.transpose(lax.dynamic_slice_in_dim(dmod_all, me * s_ada, s_ada, axis=2), (1, 0, 2))
    g_ada_w = _ada_bwd(jnp.transpose(c_all), dmod_loc, "ada_bwd")

    mc = lax.axis_index("c")

    def reduce_scatter(part, name):
        _, Rr, Cc = part.shape
        part4 = part.reshape(4, 2, Rr, Cc)
        keep = lax.dynamic_index_in_dim(part4, mc, axis=1, keepdims=False)
        send = lax.dynamic_index_in_dim(part4, 1 - mc, axis=1, keepdims=False)
        got = _swap_sibling(send, "swap_" + name)
        both = _pair_sum(keep.reshape(4 * Rr, Cc), got.reshape(4 * Rr, Cc), "pairsum_" + name).reshape(4, Rr, Cc)
        return _exchange_chips(both, "scatter_" + name)

    def step_cols(dws, w, m, v, name):
        rws, s = w.shape[1], w.shape[2]
        part = jnp.stack([jnp.transpose(dw.reshape(rws, N_DEV, s), (1, 0, 2)) for dw in dws], axis=1)
        recv = reduce_scatter(part.reshape(N_DEV, L * rws, s), name)
        outs = _adamw(w.reshape(L * rws, s), recv, m.reshape(L * rws, s), v.reshape(L * rws, s), "adamw_" + name, True)
        return [t.reshape(w.shape) for t in outs]

    def step_rows(dws, w, m, v, name):
        s, cl = w.shape[1], w.shape[2]
        part = jnp.stack([dw.reshape(N_DEV, s, cl) for dw in dws], axis=1)
        recv = reduce_scatter(part.reshape(N_DEV, L * s, cl), name)
        outs = _adamw(w.reshape(L * s, cl), recv, m.reshape(L * s, cl), v.reshape(L * s, cl), "adamw_" + name, True)
        return [t.reshape(w.shape) for t in outs]

    r_w_in = step_cols(dW_in, w_in, m_w_in, v_w_in, "w_in")
    r_w_ffn_in = step_cols(dW_ffn_in, w_ffn_in, m_w_ffn_in, v_w_ffn_in, "w_ffn_in")
    r_w_o = step_rows(dW_o, w_o, m_w_o, v_w_o, "w_o")
    r_w_ffn_out = step_rows(dW_ffn_out, w_ffn_out, m_w_ffn_out, v_w_ffn_out, "w_ffn_out")
    r_ada_w = [t.reshape(ada_w.shape) for t in _adamw(
        ada_w.reshape(L * D, s_ada), g_ada_w.reshape(L * D, s_ada), m_ada_w.reshape(L * D, s_ada),
        v_ada_w.reshape(L * D, s_ada), "adamw_ada_w", False)]

    sw = [b_f, conv_w, conv_b, conv_ln_g, conv_ln_b, mix_pre_g, mix_post_g, ffn_pre_g, ffn_post_g, ada_b]
    sg = [g_b_f, g_conv_w, g_conv_b, g_ln_g, g_ln_b, g_mix_pre, g_mix_post, g_ffn_pre, g_ffn_post, g_ada_b]
    sm = [m_b_f, m_conv_w, m_conv_b, m_conv_ln_g, m_conv_ln_b, m_mix_pre_g, m_mix_post_g, m_ffn_pre_g, m_ffn_post_g, m_ada_b]
    sv = [v_b_f, v_conv_w, v_conv_b, v_conv_ln_g, v_conv_ln_b, v_mix_pre_g, v_mix_post_g, v_ffn_pre_g, v_ffn_post_g, v_ada_b]
    shapes = [a.shape for a in sw]

    def flat(arrs):
        p = _pack(arrs, D)
        n = p.shape[0] * p.shape[1]
        return jnp.pad(p.reshape(n, D), ((0, -(-n // 8) * 8 - n), (0, 0))), p.shape

    pw, pshape = flat(sw)
    pg, pm, pv = flat(sg)[0], flat(sm)[0], flat(sv)[0]
    s_outs = _adamw(pw, pg, pm, pv, "adamw_small", False)
    n_small = pshape[0] * pshape[1]
    s_g, s_d, s_m, s_v = (_unpack(t[:n_small].reshape(pshape), shapes, D) for t in s_outs)

    big = {"w_in": r_w_in, "w_o": r_w_o, "w_ffn_in": r_w_ffn_in, "w_ffn_out": r_w_ffn_out, "ada_w": r_ada_w}
    order = ["w_in", "b_f", "conv_w", "conv_b", "conv_ln_g", "conv_ln_b", "w_o", "w_ffn_in", "w_ffn_out",
             "mix_pre_g", "mix_post_g", "ffn_pre_g", "ffn_post_g", "ada_w", "ada_b"]
    small_pos = {n: i for i, n in enumerate(["b_f", "conv_w", "conv_b", "conv_ln_g", "conv_ln_b", "mix_pre_g",
                                             "mix_post_g", "ffn_pre_g", "ffn_post_g", "ada_b"])}

    def pick(n, k):
        if n in big:
            return big[n][k]
        return (s_g, s_d, s_m, s_v)[k][small_pos[n]]

    return (loss, grad_x, *[pick(n, 0) for n in order], *[pick(n, 1) for n in order],
            *[pick(n, 2) for n in order], *[pick(n, 3) for n in order])
```

```python
import functools
import math

import jax
import jax.numpy as jnp
from jax import lax
from jax.experimental import pallas as pl
from jax.experimental.pallas import tpu as pltpu

F32 = jnp.float32
BF16 = jnp.bfloat16
MESH = pl.DeviceIdType.MESH
N_DEV = 8
EPS = 1e-6
CONV_K = 31
CONV_PAD = 32
CONV_CHUNK = 128
N_MOD = 6
NEG = -1e30
LANES = 128
VMEM_LIMIT = 56 * 2**20
ADAM_LR, ADAM_B1, ADAM_B2, ADAM_EPS, ADAM_WD, ADAM_STEP = 0.001, 0.9, 0.999, 1e-08, 0.01, 10

NN = (((1,), (0,)), ((), ()))
NT = (((1,), (1,)), ((), ()))
TN = (((0,), (0,)), ((), ()))


def _dot(a, b, dims):
    return lax.dot_general(a, b, dims, preferred_element_type=F32)


def _tile(n, pref, align=LANES):
    if n <= pref:
        return n
    t = (pref // align) * align
    while t >= align:
        if n % t == 0:
            return t
        t -= align
    return n


def _params(sem=None):
    return pltpu.CompilerParams(dimension_semantics=sem, vmem_limit_bytes=VMEM_LIMIT)


def _sigmoid(x):
    return 1.0 / (1.0 + jnp.exp(-x))


def _my_index():
    return 4 * lax.axis_index("x") + 2 * lax.axis_index("y") + lax.axis_index("c")


def _exchange(x, *, gather, name):
    blk = x.shape if gather else x.shape[1:]

    def body(x_ref, y_ref, send_sems, recv_sems, local_sem):
        mx, my, mc = lax.axis_index("x"), lax.axis_index("y"), lax.axis_index("c")
        me = 4 * mx + 2 * my + mc

        def src(p):
            return x_ref if gather else x_ref.at[p]

        mine = pltpu.make_async_copy(src(me), y_ref.at[me], local_sem)
        mine.start()
        copies = []
        for k in range(1, N_DEV):
            px = (1 - mx) if (k >> 2) & 1 else mx
            py = (1 - my) if (k >> 1) & 1 else my
            pc = (1 - mc) if k & 1 else mc
            cp = pltpu.make_async_remote_copy(
                src_ref=src(4 * px + 2 * py + pc), dst_ref=y_ref.at[me],
                send_sem=send_sems.at[k - 1], recv_sem=recv_sems.at[k - 1],
                device_id=(px, py, pc), device_id_type=MESH)
            cp.start()
            copies.append(cp)
        for cp in copies:
            cp.wait()
        mine.wait()

    return pl.pallas_call(
        body, name=name,
        out_shape=jax.ShapeDtypeStruct((N_DEV,) + tuple(blk), x.dtype),
        in_specs=[pl.BlockSpec(memory_space=pl.ANY)],
        out_specs=pl.BlockSpec(memory_space=pl.ANY),
        scratch_shapes=[pltpu.SemaphoreType.DMA((N_DEV - 1,)), pltpu.SemaphoreType.DMA((N_DEV - 1,)),
                        pltpu.SemaphoreType.DMA(())],
    )(x)


_HBM = pl.BlockSpec(memory_space=pl.ANY)


def _gather2(x, name):
    def body(x_ref, y_ref, send_sems, recv_sems, local_sem):
        mx, my, mc = lax.axis_index("x"), lax.axis_index("y"), lax.axis_index("c")
        me, sib = (mx, my, mc), (mx, my, 1 - mc)
        chips = [(1 - mx, my), (mx, 1 - my), (1 - mx, 1 - my)]

        def slot(px, py, pc):
            return y_ref.at[4 * px + 2 * py + pc]

        def copy(k, block, to, src=None):
            return pltpu.make_async_remote_copy(
                src_ref=slot(*block) if src is None else src, dst_ref=slot(*block),
                send_sem=send_sems.at[k], recv_sem=recv_sems.at[k], device_id=to, device_id_type=MESH)

        mine = pltpu.make_async_copy(x_ref, slot(*me), local_sem)
        mine.start()
        first = [copy(0, me, sib, src=x_ref)] + [copy(1 + j, me, (*chip, mc), src=x_ref) for j, chip in enumerate(chips)]
        for cp in first:
            cp.start()
        passed = [copy(4 + j, (*chip, mc), sib) for j, chip in enumerate(chips)]
        for j, chip in enumerate(chips):
            copy(1 + j, (*chip, mc), me).wait_recv()
            passed[j].start()
        copy(0, sib, me).wait_recv()
        for j, chip in enumerate(chips):
            copy(4 + j, (*chip, 1 - mc), me).wait_recv()
        for cp in first + passed:
            cp.wait_send()
        mine.wait()

    return pl.pallas_call(
        body, name=name,
        out_shape=jax.ShapeDtypeStruct((N_DEV,) + tuple(x.shape), x.dtype),
        in_specs=[_HBM], out_specs=_HBM,
        scratch_shapes=[pltpu.SemaphoreType.DMA((7,)), pltpu.SemaphoreType.DMA((7,)), pltpu.SemaphoreType.DMA(())],
    )(x)


def _swap_sibling(x, name):
    def body(x_ref, y_ref, send_sem, recv_sem):
        sib = (lax.axis_index("x"), lax.axis_index("y"), 1 - lax.axis_index("c"))
        cp = pltpu.make_async_remote_copy(src_ref=x_ref, dst_ref=y_ref, send_sem=send_sem, recv_sem=recv_sem,
                                          device_id=sib, device_id_type=MESH)
        cp.start()
        cp.wait()

    return pl.pallas_call(
        body, name=name, out_shape=jax.ShapeDtypeStruct(x.shape, x.dtype), in_specs=[_HBM], out_specs=_HBM,
        scratch_shapes=[pltpu.SemaphoreType.DMA(()), pltpu.SemaphoreType.DMA(())],
    )(x)


def _exchange_chips(x, name):
    def body(x_ref, r_ref, send_sems, recv_sems, local_sem):
        mx, my, mc = lax.axis_index("x"), lax.axis_index("y"), lax.axis_index("c")
        here = 2 * mx + my
        mine = pltpu.make_async_copy(x_ref.at[here], r_ref.at[here], local_sem)
        mine.start()
        copies = []
        for j, (px, py) in enumerate([(1 - mx, my), (mx, 1 - my), (1 - mx, 1 - my)]):
            cp = pltpu.make_async_remote_copy(
                src_ref=x_ref.at[2 * px + py], dst_ref=r_ref.at[here], send_sem=send_sems.at[j], recv_sem=recv_sems.at[j],
                device_id=(px, py, mc), device_id_type=MESH)
            cp.start()
            copies.append(cp)
        for cp in copies:
            cp.wait()
        mine.wait()

    return pl.pallas_call(
        body, name=name, out_shape=jax.ShapeDtypeStruct(x.shape, x.dtype), in_specs=[_HBM], out_specs=_HBM,
        scratch_shapes=[pltpu.SemaphoreType.DMA((3,)), pltpu.SemaphoreType.DMA((3,)), pltpu.SemaphoreType.DMA(())],
    )(x)


def _pair_sum(a, b, name):
    R, C = a.shape
    tm = _tile(R, 512, 16)

    def body(a_ref, b_ref, o_ref):
        o_ref[...] = (a_ref[...].astype(F32) + b_ref[...].astype(F32)).astype(BF16)

    big = pl.BlockSpec((tm, C), lambda i: (i, 0))
    return pl.pallas_call(
        body, name=name, grid=(R // tm,), in_specs=[big, big], out_specs=big,
        out_shape=jax.ShapeDtypeStruct((R, C), BF16), compiler_params=_params(("parallel",)),
    )(a, b)


def _mm(pairs, mode, out_dtype, name, tm=512, tn=512):
    dims = {"nn": NN, "nt": NT, "tn": TN}[mode]
    a0, b0 = pairs[0][0], pairs[0][1]
    M = a0.shape[1] if mode == "tn" else a0.shape[0]
    N = b0.shape[0] if mode == "nt" else b0.shape[1]
    tm, tn = _tile(M, tm), _tile(N, tn)
    in_specs, args = [], []
    for pr in pairs:
        a, b = pr[0], pr[1]
        if mode == "tn":
            K = a.shape[0]
            in_specs.append(pl.BlockSpec((K, tm), lambda i, j: (0, i)))
            in_specs.append(pl.BlockSpec((K, tn), lambda i, j: (0, j)))
        elif mode == "nn":
            K = a.shape[1]
            in_specs.append(pl.BlockSpec((tm, K), lambda i, j: (i, 0)))
            in_specs.append(pl.BlockSpec((K, tn), lambda i, j: (0, j)))
        else:
            K = a.shape[1]
            cb = pr[2] if len(pr) > 2 else 0
            in_specs.append(pl.BlockSpec((tm, K), lambda i, j: (i, 0)))
            in_specs.append(pl.BlockSpec((tn, K), functools.partial(lambda i, j, cb: (j, cb), cb=cb)))
        args += [a, b]
    n_pairs = len(pairs)

    def body(*refs):
        o_ref = refs[-1]
        acc = None
        for k in range(n_pairs):
            d = _dot(refs[2 * k][...], refs[2 * k + 1][...], dims)
            acc = d if acc is None else acc + d
        o_ref[...] = acc.astype(o_ref.dtype)

    return pl.pallas_call(
        body, name=name, grid=(M // tm, N // tn), in_specs=in_specs,
        out_specs=pl.BlockSpec((tm, tn), lambda i, j: (i, j)),
        out_shape=jax.ShapeDtypeStruct((M, N), out_dtype),
        compiler_params=_params(("parallel", "arbitrary")),
    )(*args)


def _prenorm(x, g, sc, sh, name):
    T, D = x.shape
    tm = _tile(T, 512, 8)

    def body(x_ref, g_ref, sc_ref, sh_ref, h_ref):
        xv = x_ref[...]
        r = lax.rsqrt(jnp.mean(xv * xv, axis=-1, keepdims=True) + EPS)
        h_ref[...] = (((xv * r) * g_ref[...]) * (1.0 + sc_ref[...]) + sh_ref[...]).astype(BF16)

    row = pl.BlockSpec((1, D), lambda i: (0, 0))
    return pl.pallas_call(
        body, name=name, grid=(T // tm,),
        in_specs=[pl.BlockSpec((tm, D), lambda i: (i, 0)), row, row, row],
        out_specs=pl.BlockSpec((tm, D), lambda i: (i, 0)),
        out_shape=jax.ShapeDtypeStruct((T, D), BF16),
        compiler_params=_params(("parallel",)),
    )(x, g, sc, sh)


def _prenorm_bwd(dh, x, dres, g, sc, name):
    T, D = x.shape
    tm = _tile(T, 256, 8)

    def body(dh_ref, x_ref, dres_ref, g_ref, sc_ref, dx_ref, sm_ref):
        @pl.when(pl.program_id(0) == 0)
        def _():
            sm_ref[...] = jnp.zeros_like(sm_ref)

        xv, dhv = x_ref[...], dh_ref[...]
        r = lax.rsqrt(jnp.mean(xv * xv, axis=-1, keepdims=True) + EPS)
        xh = xv * r
        one_sc = 1.0 + sc_ref[...]
        sm_ref[0:1, :] += jnp.sum(dhv, axis=0, keepdims=True)
        sm_ref[1:2, :] += jnp.sum(dhv * (xh * g_ref[...]), axis=0, keepdims=True)
        sm_ref[2:3, :] += jnp.sum(dhv * one_sc * xh, axis=0, keepdims=True)
        dxh = dhv * one_sc * g_ref[...]
        dx_ref[...] = dres_ref[...] + r * (dxh - xh * jnp.mean(dxh * xh, axis=-1, keepdims=True))

    row = pl.BlockSpec((1, D), lambda i: (0, 0))
    big = pl.BlockSpec((tm, D), lambda i: (i, 0))
    return pl.pallas_call(
        body, name=name, grid=(T // tm,),
        in_specs=[big, big, big, row, row],
        out_specs=[big, pl.BlockSpec((8, D), lambda i: (0, 0))],
        out_shape=(jax.ShapeDtypeStruct((T, D), F32), jax.ShapeDtypeStruct((8, D), F32)),
        compiler_params=_params(("arbitrary",)),
    )(dh, x, dres, g, sc)


def _mm_postnorm(a, w, x, gate, gpost, name):
    T, K = a.shape
    D = w.shape[1]
    tm = _tile(T, 256, 8)

    def body(a_ref, w_ref, x_ref, gate_ref, gp_ref, y_ref, xn_ref):
        y = _dot(a_ref[...], w_ref[...], NN)
        r = lax.rsqrt(jnp.mean(y * y, axis=-1, keepdims=True) + EPS)
        y_ref[...] = y
        xn_ref[...] = x_ref[...] + gate_ref[...] * ((y * r) * gp_ref[...])

    row = pl.BlockSpec((1, D), lambda i: (0, 0))
    big = pl.BlockSpec((tm, D), lambda i: (i, 0))
    return pl.pallas_call(
        body, name=name, grid=(T // tm,),
        in_specs=[pl.BlockSpec((tm, K), lambda i: (i, 0)), pl.BlockSpec((K, D), lambda i: (0, 0)), big, row, row],
        out_specs=[big, big],
        out_shape=(jax.ShapeDtypeStruct((T, D), F32), jax.ShapeDtypeStruct((T, D), F32)),
        compiler_params=_params(("parallel",)),
    )(a, w, x, gate, gpost)


def _postnorm_bwd(dx, y, gate, gpost, name):
    T, D = y.shape
    tm = _tile(T, 256, 8)

    def body(dx_ref, y_ref, gate_ref, gp_ref, dy_ref, sm_ref):
        @pl.when(pl.program_id(0) == 0)
        def _():
            sm_ref[...] = jnp.zeros_like(sm_ref)

        yv, dxv = y_ref[...], dx_ref[...]
        r = lax.rsqrt(jnp.mean(yv * yv, axis=-1, keepdims=True) + EPS)
        yh = yv * r
        dn = dxv * gate_ref[...]
        sm_ref[0:1, :] += jnp.sum(dxv * (yh * gp_ref[...]), axis=0, keepdims=True)
        sm_ref[1:2, :] += jnp.sum(dn * yh, axis=0, keepdims=True)
        dyh = dn * gp_ref[...]
        dy_ref[...] = (r * (dyh - yh * jnp.mean(dyh * yh, axis=-1, keepdims=True))).astype(BF16)

    row = pl.BlockSpec((1, D), lambda i: (0, 0))
    big = pl.BlockSpec((tm, D), lambda i: (i, 0))
    return pl.pallas_call(
        body, name=name, grid=(T // tm,),
        in_specs=[big, big, row, row],
        out_specs=[big, pl.BlockSpec((8, D), lambda i: (0, 0))],
        out_shape=(jax.ShapeDtypeStruct((T, D), BF16), jax.ShapeDtypeStruct((8, D), F32)),
        compiler_params=_params(("arbitrary",)),
    )(dx, y, gate, gpost)


def _ffn_in_fwd(h, w, name):
    T, D = h.shape
    F = w.shape[1] // 2
    tm, tn = _tile(T, 1024), _tile(F, 256)
    nj = F // tn

    def body(h_ref, wg_ref, wu_ref, g_ref, u_ref, act_ref):
        hv = h_ref[...]
        g = _dot(hv, wg_ref[...], NN)
        u = _dot(hv, wu_ref[...], NN)
        g_ref[...] = g.astype(BF16)
        u_ref[...] = u.astype(BF16)
        act_ref[...] = ((g * _sigmoid(g)) * u).astype(BF16)

    out = pl.BlockSpec((tm, tn), lambda i, j: (i, j))
    return pl.pallas_call(
        body, name=name, grid=(T // tm, nj),
        in_specs=[pl.BlockSpec((tm, D), lambda i, j: (i, 0)),
                  pl.BlockSpec((D, tn), lambda i, j: (0, j)),
                  pl.BlockSpec((D, tn), lambda i, j: (0, j + nj))],
        out_specs=[out, out, out],
        out_shape=tuple(jax.ShapeDtypeStruct((T, F), BF16) for _ in range(3)),
        compiler_params=_params(("parallel", "arbitrary")),
    )(h, w, w)


def _ffn_act_bwd(dy, w_out, g, u, name):
    T, D = dy.shape
    F = w_out.shape[0]
    tm, tn = _tile(T, 1024), _tile(F, 256)

    def body(dy_ref, w_ref, g_ref, u_ref, dg_ref, du_ref):
        dact = _dot(dy_ref[...], w_ref[...], NT)
        gv, uv = g_ref[...].astype(F32), u_ref[...].astype(F32)
        sg = _sigmoid(gv)
        dg_ref[...] = (dact * uv * (sg * (1.0 + gv * (1.0 - sg)))).astype(BF16)
        du_ref[...] = (dact * (gv * sg)).astype(BF16)

    tile = pl.BlockSpec((tm, tn), lambda i, j: (i, j))
    return pl.pallas_call(
        body, name=name, grid=(T // tm, F // tn),
        in_specs=[pl.BlockSpec((tm, D), lambda i, j: (i, 0)), pl.BlockSpec((tn, D), lambda i, j: (j, 0)), tile, tile],
        out_specs=[tile, tile],
        out_shape=(jax.ShapeDtypeStruct((T, F), BF16), jax.ShapeDtypeStruct((T, F), BF16)),
        compiler_params=_params(("parallel", "arbitrary")),
    )(dy, w_out, g, u)


def _lane_scan(v, reverse):
    T = v.shape[-1]
    lane = lax.broadcasted_iota(jnp.int32, v.shape, 1)
    d = 1
    while d < T:
        if reverse:
            v = v + jnp.where(lane < T - d, pltpu.roll(v, T - d, axis=1), 0.0)
        else:
            v = v + jnp.where(lane >= d, pltpu.roll(v, d, axis=1), 0.0)
        d *= 2
    return v


def _fgate_fwd(h, wf_t, bf, name):
    T, D = h.shape
    R = wf_t.shape[0]

    def body(h_ref, w_ref, b_ref, fl_ref, cum_ref):
        fl = _dot(w_ref[...], h_ref[...], NT) + b_ref[...]
        fl_ref[...] = fl
        logf = jnp.minimum(fl, 0.0) - jnp.log(1.0 + jnp.exp(-jnp.abs(fl)))
        cum_ref[...] = _lane_scan(logf, reverse=False)

    return pl.pallas_call(
        body, name=name,
        out_shape=(jax.ShapeDtypeStruct((R, T), F32), jax.ShapeDtypeStruct((R, T), F32)),
        compiler_params=_params(),
    )(h, wf_t, bf)


def _fgate_bwd(dcum, fl, h, name):
    R, T = fl.shape
    D = h.shape[1]

    def body(dc_ref, fl_ref, h_ref, dfl_ref, dw_ref, db_ref):
        dlogf = _lane_scan(dc_ref[...], reverse=True)
        dfl = dlogf * _sigmoid(-fl_ref[...])
        dfl_ref[...] = dfl
        dw_ref[...] = _dot(dfl.astype(BF16), h_ref[...], NN)
        db_ref[...] = jnp.broadcast_to(jnp.sum(dfl, axis=-1, keepdims=True), (R, LANES))

    return pl.pallas_call(
        body, name=name,
        out_shape=(jax.ShapeDtypeStruct((R, T), F32), jax.ShapeDtypeStruct((R, D), F32),
                   jax.ShapeDtypeStruct((R, LANES), F32)),
        compiler_params=_params(),
    )(dcum, fl, h)


def _head_masks(hpb, dh, rows):
    lane = lax.broadcasted_iota(jnp.int32, (rows, LANES), 1)
    return [(lane >= h * dh) & (lane < (h + 1) * dh) for h in range(hpb)]


def _stack_heads(v, masks):
    return jnp.concatenate([jnp.where(mk, v, jnp.zeros_like(v)) for mk in masks], axis=0)


def _heads_to_lanes(col, masks, tq):
    out = jnp.broadcast_to(col[0:tq], (tq, LANES))
    for h in range(1, len(masks)):
        out = jnp.where(masks[h], col[h * tq:(h + 1) * tq], out)
    return out


def _causal_stack(hpb, tq):
    r = lax.broadcasted_iota(jnp.int32, (tq, tq), 0)
    c = lax.broadcasted_iota(jnp.int32, (tq, tq), 1)
    return jnp.concatenate([c] * hpb, axis=0) <= jnp.concatenate([r] * hpb, axis=0)


def _attn_fwd(qkv, cum_rows, n_heads, name, tq):
    T = qkv.shape[0]
    A = qkv.shape[1] // 3
    dh = A // n_heads
    hpb = LANES // dh
    nb = A // LANES
    nq = T // tq
    scale = dh ** -0.5

    def body(q_ref, k_ref, v_ref, c_ref, o_ref, l_ref, vbd):
        hp, i = pl.program_id(0), pl.program_id(1)
        masks = _head_masks(hpb, dh, tq)

        @pl.when(i == 0)
        def _():
            def fill(j, _):
                vbd[j] = _stack_heads(v_ref[pl.ds(pl.multiple_of(j * tq, tq), tq), :], masks)
                return 0

            lax.fori_loop(0, nq, fill, 0)

        qs = _stack_heads(q_ref[...], masks)
        crow0 = hp * hpb * nq

        def tile(j, carry, diag):
            m, l, acc = carry
            kt = k_ref[pl.ds(pl.multiple_of(j * tq, tq), tq), :]
            bias = jnp.concatenate(
                [jnp.broadcast_to(c_ref[pl.ds(crow0 + h * nq + j, 1), :], (tq, tq)) for h in range(hpb)], axis=0)
            s = _dot(qs, kt, NT) * scale - bias
            if diag:
                s = jnp.where(_causal_stack(hpb, tq), s, NEG)
            m_new = jnp.maximum(m, jnp.max(s, axis=-1, keepdims=True))
            p = jnp.exp(s - m_new)
            alpha = jnp.exp(m - m_new)
            l = alpha * l + jnp.sum(p, axis=-1, keepdims=True)
            pcat = jnp.concatenate([p[h * tq:(h + 1) * tq] for h in range(hpb)], axis=1).astype(BF16)
            acc = _heads_to_lanes(alpha, masks, tq) * acc + _dot(pcat, vbd[j], NN)
            return m_new, l, acc

        init = (jnp.full((hpb * tq, 1), NEG, F32), jnp.zeros((hpb * tq, 1), F32), jnp.zeros((tq, LANES), F32))
        carry = lax.fori_loop(0, i, lambda j, c: tile(j, c, False), init)
        m, l, acc = tile(i, carry, True)
        o_ref[...] = (acc / _heads_to_lanes(l, masks, tq)).astype(BF16)
        lse = m + jnp.log(l)
        for h in range(hpb):
            l_ref[:, h:h + 1] = lse[h * tq:(h + 1) * tq]

    return pl.pallas_call(
        body, name=name, grid=(nb, nq),
        in_specs=[pl.BlockSpec((tq, LANES), lambda h, i: (i, h)),
                  pl.BlockSpec((T, LANES), lambda h, i: (0, nb + h)),
                  pl.BlockSpec((T, LANES), lambda h, i: (0, 2 * nb + h)),
                  pl.BlockSpec(cum_rows.shape, lambda h, i: (0, 0))],
        out_specs=[pl.BlockSpec((tq, LANES), lambda h, i: (i, h)),
                   pl.BlockSpec((None, tq, hpb), lambda h, i: (h, i, 0))],
        out_shape=(jax.ShapeDtypeStruct((T, A), BF16), jax.ShapeDtypeStruct((nb, T, hpb), F32)),
        scratch_shapes=[pltpu.VMEM((nq, hpb * tq, LANES), BF16)],
        compiler_params=_params(("arbitrary", "arbitrary")),
    )(qkv, qkv, qkv, cum_rows)


def _attn_bwd(qkv, dcat, o, lse, cum_rows, n_heads, name, tq):
    T = qkv.shape[0]
    A = qkv.shape[1] // 3
    dh = A // n_heads
    hpb = LANES // dh
    nb = A // LANES
    nq = T // tq
    scale = dh ** -0.5

    def body(q_ref, k_ref, v_ref, do_ref, o_ref, l_ref, c_ref, dqkv_ref, dc_ref, dr_ref,
             dq_acc, delta, drow, qs_scr, dos_scr, kbd_scr):
        hp = pl.program_id(0)
        masks = _head_masks(hpb, dh, tq)
        crow0 = hp * hpb * nq

        def prologue(i, _):
            rs = pl.ds(pl.multiple_of(i * tq, tq), tq)
            do = do_ref[rs, :]
            prod = do * o_ref[rs, :].astype(F32)
            for h in range(hpb):
                delta[rs, h:h + 1] = jnp.sum(jnp.where(masks[h], prod, 0.0), axis=-1, keepdims=True)
            qs_scr[i] = _stack_heads(q_ref[rs, :], masks)
            dos_scr[i] = _stack_heads(do, masks).astype(BF16)
            kbd_scr[i] = _stack_heads(k_ref[rs, :], masks)
            dq_acc[rs, :] = jnp.zeros((tq, LANES), F32)
            drow[rs, :] = jnp.zeros((tq, hpb), F32)
            return 0

        lax.fori_loop(0, nq, prologue, 0)

        def kv_step(j, _):
            ks = pl.ds(pl.multiple_of(j * tq, tq), tq)
            kt, vt = k_ref[ks, :], v_ref[ks, :]
            kbd = kbd_scr[j]
            bias = jnp.concatenate(
                [jnp.broadcast_to(c_ref[pl.ds(crow0 + h * nq + j, 1), :], (tq, tq)) for h in range(hpb)], axis=0)

            def q_step(i, carry, diag):
                dk, dv, dcs = carry
                rs = pl.ds(pl.multiple_of(i * tq, tq), tq)
                qs, dos = qs_scr[i], dos_scr[i]
                s = _dot(qs, kt, NT) * scale - bias
                if diag:
                    s = jnp.where(_causal_stack(hpb, tq), s, NEG)
                lse = jnp.concatenate([l_ref[rs, h:h + 1] for h in range(hpb)], axis=0)
                p = jnp.exp(s - lse)
                dv = dv + _dot(p.astype(BF16), dos, TN)
                dp = _dot(dos, vt, NT)
                ds = p * (dp - jnp.concatenate([delta[rs, h:h + 1] for h in range(hpb)], axis=0))
                dcs = tuple(dcs[h] - jnp.sum(ds[h * tq:(h + 1) * tq], axis=0, keepdims=True) for h in range(hpb))
                rsum = jnp.sum(ds, axis=-1, keepdims=True)
                for h in range(hpb):
                    drow[rs, h:h + 1] += rsum[h * tq:(h + 1) * tq]
                dsb = (ds * scale).astype(BF16)
                dk = dk + _dot(dsb, qs, TN)
                dscat = jnp.concatenate([dsb[h * tq:(h + 1) * tq] for h in range(hpb)], axis=1)
                dq_acc[rs, :] += _dot(dscat, kbd, NN)
                return dk, dv, dcs

            init = (jnp.zeros((tq, LANES), F32), jnp.zeros((tq, LANES), F32),
                    tuple(jnp.zeros((1, tq), F32) for _ in range(hpb)))
            carry = q_step(j, init, True)
            dk, dv, dcs = lax.fori_loop(j + 1, nq, lambda i, c: q_step(i, c, False), carry)
            dqkv_ref[1, ks, :] = dk.astype(BF16)
            dqkv_ref[2, ks, :] = dv.astype(BF16)
            for h in range(hpb):
                dc_ref[pl.ds(crow0 + h * nq + j, 1), :] = dcs[h]
            return 0

        lax.fori_loop(0, nq, kv_step, 0)
        dqkv_ref[0] = dq_acc[...].astype(BF16)
        dr_ref[...] = drow[...]

    col = lambda off: pl.BlockSpec((T, LANES), functools.partial(lambda h, off: (0, off + h), off=off))
    return pl.pallas_call(
        body, name=name, grid=(nb,),
        in_specs=[col(0), col(nb), col(2 * nb), col(0), col(0),
                  pl.BlockSpec((None, T, hpb), lambda h: (h, 0, 0)),
                  pl.BlockSpec(cum_rows.shape, lambda h: (0, 0))],
        out_specs=[pl.BlockSpec((3, T, LANES), lambda h: (0, 0, h)),
                   pl.BlockSpec(cum_rows.shape, lambda h: (0, 0)),
                   pl.BlockSpec((None, T, hpb), lambda h: (h, 0, 0))],
        out_shape=(jax.ShapeDtypeStruct((3, T, A), BF16), jax.ShapeDtypeStruct(cum_rows.shape, F32),
                   jax.ShapeDtypeStruct((nb, T, hpb), F32)),
        scratch_shapes=[pltpu.VMEM((T, LANES), F32), pltpu.VMEM((T, hpb), F32), pltpu.VMEM((T, hpb), F32),
                        pltpu.VMEM((nq, hpb * tq, LANES), BF16), pltpu.VMEM((nq, hpb * tq, LANES), BF16),
                        pltpu.VMEM((nq, hpb * tq, LANES), BF16)],
        compiler_params=_params(("arbitrary",)),
    )(qkv, qkv, qkv, dcat, o, lse, cum_rows)


def _glu_into(upad, cv_ref, cg_ref, T):
    upad[0:CONV_PAD, :] = jnp.zeros((CONV_PAD, upad.shape[1]), F32)

    def fill(c, _):
        rs = pl.ds(pl.multiple_of(c * CONV_CHUNK, CONV_CHUNK), CONV_CHUNK)
        upad[pl.ds(pl.multiple_of(CONV_PAD + c * CONV_CHUNK, 8), CONV_CHUNK), :] = cv_ref[rs, :] * _sigmoid(cg_ref[rs, :])
        return 0

    lax.fori_loop(0, T // CONV_CHUNK, fill, 0)


def _conv_taps(win, w_ref, first, step):
    acc = None
    for k in range(CONV_K):
        o = first + step * k
        t = w_ref[k:k + 1, :] * win[o:o + CONV_CHUNK, :]
        acc = t if acc is None else acc + t
    return acc


def _conv_fwd(cproj, w, b, lg, lb, name):
    T = cproj.shape[0]
    C = cproj.shape[1] // 2
    off = CONV_PAD - (CONV_K - 1)

    def body(cv_ref, cg_ref, w_ref, b_ref, lg_ref, lb_ref, out_ref, upad, win):
        _glu_into(upad, cv_ref, cg_ref, T)

        def chunk(c, _):
            r0 = pl.multiple_of(c * CONV_CHUNK, CONV_CHUNK)
            win[...] = upad[pl.ds(r0, CONV_CHUNK + CONV_PAD), :]
            u1 = _conv_taps(win, w_ref, off, 1) + b_ref[...]
            mu = jnp.mean(u1, axis=-1, keepdims=True)
            var = jnp.mean(jnp.square(u1 - mu), axis=-1, keepdims=True)
            u2 = ((u1 - mu) * lax.rsqrt(var + EPS)) * lg_ref[...] + lb_ref[...]
            out_ref[pl.ds(r0, CONV_CHUNK), :] = (u2 * _sigmoid(u2)).astype(BF16)
            return 0

        lax.fori_loop(0, T // CONV_CHUNK, chunk, 0)

    row = pl.BlockSpec((1, C), lambda i: (0, 0))
    return pl.pallas_call(
        body, name=name, grid=(1,),
        in_specs=[pl.BlockSpec((T, C), lambda i: (0, 0)), pl.BlockSpec((T, C), lambda i: (0, 1)),
                  pl.BlockSpec(w.shape, lambda i: (0, 0)), row, row, row],
        out_specs=pl.BlockSpec((T, C), lambda i: (0, 0)),
        out_shape=jax.ShapeDtypeStruct((T, C), BF16),
        scratch_shapes=[pltpu.VMEM((T + CONV_PAD, C), F32), pltpu.VMEM((CONV_CHUNK + CONV_PAD, C), F32)],
        compiler_params=_params(("arbitrary",)),
    )(cproj, cproj, w, b, lg, lb)


def _conv_bwd(cproj, dcat, w, b, lg, lb, name):
    T = cproj.shape[0]
    C = cproj.shape[1] // 2
    off = CONV_PAD - (CONV_K - 1)
    n_chunks = T // CONV_CHUNK

    def fold(v):
        return jnp.sum(v.reshape(CONV_CHUNK // 8, 8, C), axis=0)

    def body(cv_ref, cg_ref, du_ref, w_ref, b_ref, lg_ref, lb_ref, dc_ref, dw_ref, sm_ref, upad, dpad, win, dwacc, smacc):
        _glu_into(upad, cv_ref, cg_ref, T)
        dpad[pl.ds(T, CONV_PAD), :] = jnp.zeros((CONV_PAD, C), F32)
        dwacc[...] = jnp.zeros_like(dwacc)
        smacc[...] = jnp.zeros_like(smacc)

        def chunk_a(c, _):
            r0 = pl.multiple_of(c * CONV_CHUNK, CONV_CHUNK)
            win[...] = upad[pl.ds(r0, CONV_CHUNK + CONV_PAD), :]
            u1 = _conv_taps(win, w_ref, off, 1) + b_ref[...]
            mu = jnp.mean(u1, axis=-1, keepdims=True)
            var = jnp.mean(jnp.square(u1 - mu), axis=-1, keepdims=True)
            rstd = lax.rsqrt(var + EPS)
            u1h = (u1 - mu) * rstd
            u2 = u1h * lg_ref[...] + lb_ref[...]
            sg = _sigmoid(u2)
            du2 = du_ref[pl.ds(r0, CONV_CHUNK), :] * (sg * (1.0 + u2 * (1.0 - sg)))
            smacc[8:16, :] += fold(du2 * u1h)
            smacc[16:24, :] += fold(du2)
            du1h = du2 * lg_ref[...]
            du1 = rstd * (du1h - jnp.mean(du1h, axis=-1, keepdims=True)
                          - u1h * jnp.mean(du1h * u1h, axis=-1, keepdims=True))
            smacc[0:8, :] += fold(du1)
            dpad[pl.ds(r0, CONV_CHUNK), :] = du1
            for k in range(CONV_K):
                dwacc[8 * k:8 * k + 8, :] += fold(du1 * win[off + k:off + k + CONV_CHUNK, :])
            return 0

        lax.fori_loop(0, n_chunks, chunk_a, 0)

        def chunk_b(c, _):
            r0 = pl.multiple_of(c * CONV_CHUNK, CONV_CHUNK)
            rs = pl.ds(r0, CONV_CHUNK)
            win[...] = dpad[pl.ds(r0, CONV_CHUNK + CONV_PAD), :]
            du0 = _conv_taps(win, w_ref, CONV_K - 1, -1)
            cv, sg = cv_ref[rs, :], _sigmoid(cg_ref[rs, :])
            dc_ref[rs, 0:C] = (du0 * sg).astype(BF16)
            dc_ref[rs, C:2 * C] = (du0 * cv * (sg * (1.0 - sg))).astype(BF16)
            return 0

        lax.fori_loop(0, n_chunks, chunk_b, 0)
        for k in range(CONV_K):
            dw_ref[k:k + 1, :] = jnp.sum(dwacc[8 * k:8 * k + 8, :], axis=0, keepdims=True)
        dw_ref[CONV_K:CONV_PAD, :] = jnp.zeros((CONV_PAD - CONV_K, C), F32)
        for r in range(3):
            sm_ref[r:r + 1, :] = jnp.sum(smacc[8 * r:8 * r + 8, :], axis=0, keepdims=True)
        sm_ref[3:8, :] = jnp.zeros((5, C), F32)

    row = pl.BlockSpec((1, C), lambda i: (0, 0))
    return pl.pallas_call(
        body, name=name, grid=(1,),
        in_specs=[pl.BlockSpec((T, C), lambda i: (0, 0)), pl.BlockSpec((T, C), lambda i: (0, 1)),
                  pl.BlockSpec((T, C), lambda i: (0, 1)),
                  pl.BlockSpec(w.shape, lambda i: (0, 0)), row, row, row],
        out_specs=[pl.BlockSpec((T, 2 * C), lambda i: (0, 0)), pl.BlockSpec((CONV_PAD, C), lambda i: (0, 0)),
                   pl.BlockSpec((8, C), lambda i: (0, 0))],
        out_shape=(jax.ShapeDtypeStruct((T, 2 * C), BF16), jax.ShapeDtypeStruct((CONV_PAD, C), F32),
                   jax.ShapeDtypeStruct((8, C), F32)),
        scratch_shapes=[pltpu.VMEM((T + CONV_PAD, C), F32), pltpu.VMEM((T + CONV_PAD, C), F32),
                        pltpu.VMEM((CONV_CHUNK + CONV_PAD, C), F32), pltpu.VMEM((8 * CONV_PAD, C), F32),
                        pltpu.VMEM((24, C), F32)],
        compiler_params=_params(("arbitrary",)),
    )(cproj, cproj, dcat, w, b, lg, lb)


def _loss_head(x, target, name):
    T, D = x.shape
    tm = _tile(T, 512, 8)

    def body(x_ref, t_ref, loss_ref, dx_ref):
        @pl.when(pl.program_id(0) == 0)
        def _():
            loss_ref[...] = jnp.zeros_like(loss_ref)

        err = x_ref[...] - t_ref[...]
        part = jnp.sum(jnp.mean(err * err, axis=-1, keepdims=True), axis=0, keepdims=True)
        loss_ref[...] += jnp.broadcast_to(0.5 * part, loss_ref.shape)
        dx_ref[...] = err * (1.0 / D)

    big = pl.BlockSpec((tm, D), lambda i: (i, 0))
    return pl.pallas_call(
        body, name=name, grid=(T // tm,),
        in_specs=[big, big],
        out_specs=[pl.BlockSpec((8, LANES), lambda i: (0, 0)), big],
        out_shape=(jax.ShapeDtypeStruct((8, LANES), F32), jax.ShapeDtypeStruct((T, D), F32)),
        compiler_params=_params(("arbitrary",)),
    )(x, target)


def _ada_fwd(c_all, ada_w, ada_b_loc, name):
    L, D, S = ada_w.shape
    B = c_all.shape[0]

    def body(c_ref, w_ref, b_ref, o_ref):
        c = c_ref[...]
        ca = (c * _sigmoid(c)).astype(BF16)
        o_ref[...] = _dot(ca, w_ref[...].astype(BF16), NN) + b_ref[...]

    return pl.pallas_call(
        body, name=name, grid=(L,),
        in_specs=[pl.BlockSpec((B, D), lambda l: (0, 0)), pl.BlockSpec((None, D, S), lambda l: (l, 0, 0)),
                  pl.BlockSpec((None, 1, S), lambda l: (l, 0, 0))],
        out_specs=pl.BlockSpec((None, B, S), lambda l: (l, 0, 0)),
        out_shape=jax.ShapeDtypeStruct((L, B, S), F32),
        compiler_params=_params(("parallel",)),
    )(c_all, ada_w, ada_b_loc)


def _ada_bwd(c_all_t, dmod_loc, name):
    D, B = c_all_t.shape
    L, _, S = dmod_loc.shape

    def body(c_ref, dm_ref, o_ref):
        c = c_ref[...]
        ca = c * _sigmoid(c)
        acc = None
        for bb in range(B):
            t = ca[:, bb:bb + 1] * dm_ref[bb:bb + 1, :]
            acc = t if acc is None else acc + t
        o_ref[...] = acc

    return pl.pallas_call(
        body, name=name, grid=(L,),
        in_specs=[pl.BlockSpec((D, B), lambda l: (0, 0)), pl.BlockSpec((None, B, S), lambda l: (l, 0, 0))],
        out_specs=pl.BlockSpec((None, D, S), lambda l: (l, 0, 0)),
        out_shape=jax.ShapeDtypeStruct((L, D, S), F32),
        compiler_params=_params(("parallel",)),
    )(c_all_t, dmod_loc)


def _sum_devices(parts, name):
    _, R, C = parts.shape
    tm = _tile(R, 256, 8)

    def body(p_ref, o_ref):
        acc = p_ref[0].astype(F32)
        for d in range(1, N_DEV):
            acc = acc + p_ref[d].astype(F32)
        o_ref[...] = acc

    return pl.pallas_call(
        body, name=name, grid=(R // tm,),
        in_specs=[pl.BlockSpec((N_DEV, tm, C), lambda i: (0, i, 0))],
        out_specs=pl.BlockSpec((tm, C), lambda i: (i, 0)),
        out_shape=jax.ShapeDtypeStruct((R, C), F32),
        compiler_params=_params(("parallel",)),
    )(parts)


def _adamw_math(w, g, m, v):
    m = ADAM_B1 * m + (1.0 - ADAM_B1) * g
    v = ADAM_B2 * v + (1.0 - ADAM_B2) * (g * g)
    m_hat = m / (1.0 - ADAM_B1 ** ADAM_STEP)
    v_hat = v / (1.0 - ADAM_B2 ** ADAM_STEP)
    delta = -ADAM_LR * (m_hat / (jnp.sqrt(v_hat) + ADAM_EPS) + ADAM_WD * w)
    return delta, m, v


def _adamw(w, g, m, v, name, summed):
    R, C = w.shape
    tm = _tile(R, 256, 16)
    n_parts = g.shape[0] if summed else 0

    def body(w_ref, g_ref, m_ref, v_ref, go_ref, d_ref, mo_ref, vo_ref):
        if summed:
            g = g_ref[0].astype(F32)
            for d in range(1, n_parts):
                g = g + g_ref[d].astype(F32)
        else:
            g = g_ref[...]
        delta, mn, vn = _adamw_math(w_ref[...], g, m_ref[...], v_ref[...])
        go_ref[...] = g
        d_ref[...] = delta
        mo_ref[...] = mn
        vo_ref[...] = vn

    big = pl.BlockSpec((tm, C), lambda i: (i, 0))
    gspec = pl.BlockSpec((n_parts, tm, C), lambda i: (0, i, 0)) if summed else big
    return pl.pallas_call(
        body, name=name, grid=(R // tm,),
        in_specs=[big, gspec, big, big],
        out_specs=[big, big, big, big],
        out_shape=tuple(jax.ShapeDtypeStruct((R, C), F32) for _ in range(4)),
        compiler_params=_params(("parallel",)),
    )(w, g, m, v)


def _pack(arrs, D):
    L = arrs[0].shape[0]
    cols = []
    for a in arrs:
        f = a.reshape(L, -1)
        n = f.shape[1]
        cols.append(jnp.pad(f, ((0, 0), (0, -(-n // D) * D - n))))
    flat = jnp.concatenate(cols, axis=1)
    return flat.reshape(L, flat.shape[1] // D, D)


def _unpack(p, shapes, D):
    L = p.shape[0]
    out, r = [], 0
    for s in shapes:
        n = math.prod(s[1:])
        rows = -(-n // D)
        out.append(p[:, r:r + rows].reshape(L, rows * D)[:, :n].reshape(s))
        r += rows
    return out


def kernel(x, c, w_in, b_f, conv_w, conv_b, conv_ln_g, conv_ln_b, w_o, w_ffn_in, w_ffn_out, mix_pre_g, mix_post_g, ffn_pre_g, ffn_post_g, ada_w, ada_b, loss_target, m_w_in, m_b_f, m_conv_w, m_conv_b, m_conv_ln_g, m_conv_ln_b, m_w_o, m_w_ffn_in, m_w_ffn_out, m_mix_pre_g, m_mix_post_g, m_ffn_pre_g, m_ffn_post_g, m_ada_w, m_ada_b, v_w_in, v_b_f, v_conv_w, v_conv_b, v_conv_ln_g, v_conv_ln_b, v_w_o, v_w_ffn_in, v_w_ffn_out, v_mix_pre_g, v_mix_post_g, v_ffn_pre_g, v_ffn_post_g, v_ada_w, v_ada_b):
    L, D, s_in = w_in.shape
    T = x.shape[1]
    H = b_f.shape[1]
    A = D // 2
    C = D - A
    cs = conv_w.shape[2]
    F = w_ffn_out.shape[1] * N_DEV
    s_ff = w_ffn_in.shape[2]
    s_ada = ada_w.shape[2]
    R = 16
    me = _my_index()
    x0 = x[0]
    target = loss_target[0]
    tq = _tile(T, 512)
    nq = T // tq

    def gathered_cols(w, name):
        g = _gather2(w.astype(BF16), name)
        return jnp.transpose(g, (1, 2, 0, 3)).reshape(w.shape[0], w.shape[1], N_DEV * w.shape[2])

    def gathered_rows(w, name):
        g = _gather2(w.astype(BF16), name)
        return jnp.transpose(g, (1, 0, 2, 3)).reshape(w.shape[0], N_DEV * w.shape[1], w.shape[2])

    W_in = gathered_cols(w_in, "gather_w_in")
    W_ffn_in = gathered_cols(w_ffn_in, "gather_w_ffn_in")
    W_o = gathered_rows(w_o, "gather_w_o")
    W_ffn_out = gathered_rows(w_ffn_out, "gather_w_ffn_out")
    cw_g = _exchange(conv_w, gather=True, name="gather_conv_w")
    conv_w_full = jnp.transpose(cw_g, (1, 2, 0, 3)).reshape(L, CONV_K, C)
    conv_w_pad = jnp.pad(conv_w_full, ((0, 0), (0, CONV_PAD - CONV_K), (0, 0)))
    W_qkv = W_in[:, :, :3 * A]
    W_f = W_in[:, :, 3 * A:3 * A + H]
    W_c = W_in[:, :, 3 * A + H:]
    W_f_t = jnp.pad(jnp.transpose(W_f, (0, 2, 1)), ((0, 0), (0, R - H), (0, 0)))
    W_f_pad = jnp.pad(W_f, ((0, 0), (0, 0), (0, R - H)))
    b_f_col = jnp.pad(b_f, ((0, 0), (0, R - H)))[:, :, None]

    c_all = _exchange(c, gather=True, name="gather_c").reshape(N_DEV, D)
    ada_b_loc = lax.dynamic_slice_in_dim(ada_b, me * s_ada, s_ada, axis=1)[:, None, :]
    mod_loc = _ada_fwd(c_all, ada_w, ada_b_loc, "ada_fwd")
    mod_g = _exchange(mod_loc, gather=True, name="gather_mod")
    mod = lax.dynamic_index_in_dim(mod_g, me, axis=2, keepdims=False)
    mod = jnp.transpose(mod, (1, 0, 2)).reshape(L, N_MOD, 1, D)

    saved = []
    xc = x0
    for l in range(L):
        sh1, sc1, g1, sh2, sc2, g2 = (mod[l, k] for k in range(N_MOD))
        gpre1, gpost1, gpre2, gpost2 = (p[l][None, :] for p in (mix_pre_g, mix_post_g, ffn_pre_g, ffn_post_g))
        h1 = _prenorm(xc, gpre1, sc1, sh1, f"prenorm1_{l}")
        qkv = _mm([(h1, W_qkv[l])], "nn", BF16, f"proj_qkv_{l}")
        cproj = _mm([(h1, W_c[l])], "nn", F32, f"proj_conv_{l}")
        fl, cum = _fgate_fwd(h1, W_f_t[l], b_f_col[l], f"fgate_{l}")
        cum_rows = cum[:H].reshape(H * nq, tq)
        o, lse = _attn_fwd(qkv, cum_rows, H, f"attn_{l}", tq)
        u3 = _conv_fwd(cproj, conv_w_pad[l], conv_b[l][None, :], conv_ln_g[l][None, :], conv_ln_b[l][None, :], f"conv_{l}")
        cat = jnp.concatenate([o, u3], axis=-1)
        y1, x_mid = _mm_postnorm(cat, W_o[l], xc, g1, gpost1, f"out_proj_{l}")
        h2 = _prenorm(x_mid, gpre2, sc2, sh2, f"prenorm2_{l}")
        g, u, act = _ffn_in_fwd(h2, W_ffn_in[l], f"ffn_in_{l}")
        y2, x_out = _mm_postnorm(act, W_ffn_out[l], x_mid, g2, gpost2, f"ffn_out_{l}")
        saved.append((xc, h1, qkv, cproj, fl, cum_rows, o, lse, cat, y1, x_mid, h2, g, u, act, y2))
        xc = x_out

    loss_tile, dx = _loss_head(xc, target, "loss_head")
    loss = lax.psum(loss_tile[0, 0], ("x", "y", "c"))

    dW_in, dW_o, dW_ffn_in, dW_ffn_out, small = [None] * L, [None] * L, [None] * L, [None] * L, [None] * L
    for l in reversed(range(L)):
        xin, h1, qkv, cproj, fl, cum_rows, o, lse, cat, y1, x_mid, h2, g, u, act, y2 = saved[l]
        sh1, sc1, g1, sh2, sc2, g2 = (mod[l, k] for k in range(N_MOD))
        gpre1, gpost1, gpre2, gpost2 = (p[l][None, :] for p in (mix_pre_g, mix_post_g, ffn_pre_g, ffn_post_g))
        dy2, sm_post2 = _postnorm_bwd(dx, y2, g2, gpost2, f"postnorm2_bwd_{l}")
        dgate, dup = _ffn_act_bwd(dy2, W_ffn_out[l], g, u, f"ffn_act_bwd_{l}")
        dW_ffn_out[l] = _mm([(act, dy2)], "tn", BF16, f"dw_ffn_out_{l}")
        dh2 = _mm([(dgate, W_ffn_in[l], 0), (dup, W_ffn_in[l], 1)], "nt", F32, f"dh2_{l}")
        dWg = _mm([(h2, dgate)], "tn", BF16, f"dw_ffn_gate_{l}")
        dWu = _mm([(h2, dup)], "tn", BF16, f"dw_ffn_up_{l}")
        dW_ffn_in[l] = jnp.concatenate([dWg, dWu], axis=1)
        dx_mid, sm_pre2 = _prenorm_bwd(dh2, x_mid, dx, gpre2, sc2, f"prenorm2_bwd_{l}")
        dy1, sm_post1 = _postnorm_bwd(dx_mid, y1, g1, gpost1, f"postnorm1_bwd_{l}")
        dcat = _mm([(dy1, W_o[l])], "nt", F32, f"dcat_{l}")
        dW_o[l] = _mm([(cat, dy1)], "tn", BF16, f"dw_o_{l}")
        dqkv, dcum_rows, dcum_q = _attn_bwd(qkv, dcat, o, lse, cum_rows, H, f"attn_bwd_{l}", tq)
        dcproj, dconv_w, sm_conv = _conv_bwd(cproj, dcat, conv_w_pad[l], conv_b[l][None, :], conv_ln_g[l][None, :],
                                             conv_ln_b[l][None, :], f"conv_bwd_{l}")
        dcum = dcum_rows.reshape(H, T) + jnp.transpose(dcum_q, (0, 2, 1)).reshape(H, T)
        dcum = jnp.pad(dcum, ((0, R - H), (0, 0)))
        dfl_t, dwf_t, dbf = _fgate_bwd(dcum, fl, h1, f"fgate_bwd_{l}")
        dfl = jnp.transpose(dfl_t).astype(BF16)
        dh1 = _mm([(dqkv[0], W_qkv[l], 0), (dqkv[1], W_qkv[l], 1), (dqkv[2], W_qkv[l], 2),
                   (dfl, W_f_pad[l]), (dcproj, W_c[l])], "nt", F32, f"dh1_{l}")
        dWq = [_mm([(h1, dqkv[k])], "tn", BF16, f"dw_qkv{k}_{l}") for k in range(3)]
        dWc = _mm([(h1, dcproj)], "tn", BF16, f"dw_conv_{l}")
        dWf = jnp.transpose(dwf_t[:H]).astype(BF16)
        dW_in[l] = jnp.concatenate(dWq + [dWf, dWc], axis=1)
        dx, sm_pre1 = _prenorm_bwd(dh1, xin, dx_mid, gpre1, sc1, f"prenorm1_bwd_{l}")
        dmod = jnp.stack([sm_pre1[0], sm_pre1[1], sm_post1[0], sm_pre2[0], sm_pre2[1], sm_post2[0]])
        small[l] = (dmod, sm_pre1[2], sm_post1[1], sm_pre2[2], sm_post2[1], sm_conv[0], sm_conv[1], sm_conv[2],
                    dbf[:H, 0], dconv_w[:CONV_K])
    grad_x = dx[None]

    small_names = 10
    small_l = [jnp.stack([small[l][k] for l in range(L)]) for k in range(small_names)]
    small_shapes = [a.shape for a in small_l]
    packed = _pack(small_l, D)
    rows = packed.shape[1]
    rows_pad = -(-L * rows // 8) * 8
    packed2 = jnp.pad(packed.reshape(L * rows, D), ((0, rows_pad - L * rows), (0, 0)))
    small_g = _exchange(packed2, gather=True, name="gather_small")
    small_sum = _sum_devices(small_g, "sum_small")[:L * rows].reshape(L, rows, D)
    (g_ada_b6, g_mix_pre, g_mix_post, g_ffn_pre, g_ffn_post, g_conv_b, g_ln_g, g_ln_b, g_b_f,
     g_conv_w_full) = _unpack(small_sum, small_shapes, D)
    g_ada_b = g_ada_b6.reshape(L, N_MOD * D)
    g_conv_w = lax.dynamic_slice_in_dim(g_conv_w_full, me * cs, cs, axis=2)
    dmod_all = small_g[:, :L * rows].reshape(N_DEV, L, rows, D)[:, :, :N_MOD].reshape(N_DEV, L, N_MOD * D)
    dmod_loc = jnp.transpose(lax.dynamic_slice_in_dim(dmod_all, me * s_ada, s_ada, axis=2), (1, 0, 2))
    g_ada_w = _ada_bwd(jnp.transpose(c_all), dmod_loc, "ada_bwd")

    mc = lax.axis_index("c")

    def reduce_scatter(part, name):
        _, Rr, Cc = part.shape
        part4 = part.reshape(4, 2, Rr, Cc)
        keep = lax.dynamic_index_in_dim(part4, mc, axis=1, keepdims=False)
        send = lax.dynamic_index_in_dim(part4, 1 - mc, axis=1, keepdims=False)
        got = _swap_sibling(send, "swap_" + name)
        both = _pair_sum(keep.reshape(4 * Rr, Cc), got.reshape(4 * Rr, Cc), "pairsum_" + name).reshape(4, Rr, Cc)
        return _exchange_chips(both, "scatter_" + name)

    def step_cols(dws, w, m, v, name):
        rws, s = w.shape[1], w.shape[2]
        part = jnp.stack([jnp.transpose(dw.reshape(rws, N_DEV, s), (1, 0, 2)) for dw in dws], axis=1)
        recv = reduce_scatter(part.reshape(N_DEV, L * rws, s), name)
        outs = _adamw(w.reshape(L * rws, s), recv, m.reshape(L * rws, s), v.reshape(L * rws, s), "adamw_" + name, True)
        return [t.reshape(w.shape) for t in outs]

    def step_rows(dws, w, m, v, name):
        s, cl = w.shape[1], w.shape[2]
        part = jnp.stack([dw.reshape(N_DEV, s, cl) for dw in dws], axis=1)
        recv = reduce_scatter(part.reshape(N_DEV, L * s, cl), name)
        outs = _adamw(w.reshape(L * s, cl), recv, m.reshape(L * s, cl), v.reshape(L * s, cl), "adamw_" + name, True)
        return [t.reshape(w.shape) for t in outs]

    r_w_in = step_cols(dW_in, w_in, m_w_in, v_w_in, "w_in")
    r_w_ffn_in = step_cols(dW_ffn_in, w_ffn_in, m_w_ffn_in, v_w_ffn_in, "w_ffn_in")
    r_w_o = step_rows(dW_o, w_o, m_w_o, v_w_o, "w_o")
    r_w_ffn_out = step_rows(dW_ffn_out, w_ffn_out, m_w_ffn_out, v_w_ffn_out, "w_ffn_out")
    r_ada_w = [t.reshape(ada_w.shape) for t in _adamw(
        ada_w.reshape(L * D, s_ada), g_ada_w.reshape(L * D, s_ada), m_ada_w.reshape(L * D, s_ada),
        v_ada_w.reshape(L * D, s_ada), "adamw_ada_w", False)]

    sw = [b_f, conv_w, conv_b, conv_ln_g, conv_ln_b, mix_pre_g, mix_post_g, ffn_pre_g, ffn_post_g, ada_b]
    sg = [g_b_f, g_conv_w, g_conv_b, g_ln_g, g_ln_b, g_mix_pre, g_mix_post, g_ffn_pre, g_ffn_post, g_ada_b]
    sm = [m_b_f, m_conv_w, m_conv_b, m_conv_ln_g, m_conv_ln_b, m_mix_pre_g, m_mix_post_g, m_ffn_pre_g, m_ffn_post_g, m_ada_b]
    sv = [v_b_f, v_conv_w, v_conv_b, v_conv_ln_g, v_conv_ln_b, v_mix_pre_g, v_mix_post_g, v_ffn_pre_g, v_ffn_post_g, v_ada_b]
    shapes = [a.shape for a in sw]

    def flat(arrs):
        p = _pack(arrs, D)
        n = p.shape[0] * p.shape[1]
        return jnp.pad(p.reshape(n, D), ((0, -(-n // 8) * 8 - n), (0, 0))), p.shape

    pw, pshape = flat(sw)
    pg, pm, pv = flat(sg)[0], flat(sm)[0], flat(sv)[0]
    s_outs = _adamw(pw, pg, pm, pv, "adamw_small", False)
    n_small = pshape[0] * pshape[1]
    s_g, s_d, s_m, s_v = (_unpack(t[:n_small].reshape(pshape), shapes, D) for t in s_outs)

    big = {"w_in": r_w_in, "w_o": r_w_o, "w_ffn_in": r_w_ffn_in, "w_ffn_out": r_w_ffn_out, "ada_w": r_ada_w}
    order = ["w_in", "b_f", "conv_w", "conv_b", "conv_ln_g", "conv_ln_b", "w_o", "w_ffn_in", "w_ffn_out",
             "mix_pre_g", "mix_post_g", "ffn_pre_g", "ffn_post_g", "ada_w", "ada_b"]
    small_pos = {n: i for i, n in enumerate(["b_f", "conv_w", "conv_b", "conv_ln_g", "conv_ln_b", "mix_pre_g",
                                             "mix_post_g", "ffn_pre_g", "ffn_post_g", "ada_b"])}

    def pick(n, k):
        if n in big:
            return big[n][k]
        return (s_g, s_d, s_m, s_v)[k][small_pos[n]]

    return (loss, grad_x, *[pick(n, 0) for n in order], *[pick(n, 1) for n in order],
            *[pick(n, 2) for n in order], *[pick(n, 3) for n in order])
```

```python
import functools
import math

import jax
import jax.numpy as jnp
from jax import lax
from jax.experimental import pallas as pl
from jax.experimental.pallas import tpu as pltpu

F32 = jnp.float32
BF16 = jnp.bfloat16
MESH = pl.DeviceIdType.MESH
N_DEV = 8
EPS = 1e-6
CONV_K = 31
CONV_PAD = 32
CONV_CHUNK = 128
N_MOD = 6
NEG = -1e30
LANES = 128
VMEM_LIMIT = 56 * 2**20
ADAM_LR, ADAM_B1, ADAM_B2, ADAM_EPS, ADAM_WD, ADAM_STEP = 0.001, 0.9, 0.999, 1e-08, 0.01, 10

NN = (((1,), (0,)), ((), ()))
NT = (((1,), (1,)), ((), ()))
TN = (((0,), (0,)), ((), ()))


def _dot(a, b, dims):
    return lax.dot_general(a, b, dims, preferred_element_type=F32)


def _tile(n, pref, align=LANES):
    if n <= pref:
        return n
    t = (pref // align) * align
    while t >= align:
        if n % t == 0:
            return t
        t -= align
    return n


def _params(sem=None):
    return pltpu.CompilerParams(dimension_semantics=sem, vmem_limit_bytes=VMEM_LIMIT)


def _sigmoid(x):
    return 1.0 / (1.0 + jnp.exp(-x))


def _my_index():
    return 4 * lax.axis_index("x") + 2 * lax.axis_index("y") + lax.axis_index("c")


def _exchange(x, *, gather, name):
    blk = x.shape if gather else x.shape[1:]

    def body(x_ref, y_ref, send_sems, recv_sems, local_sem):
        mx, my, mc = lax.axis_index("x"), lax.axis_index("y"), lax.axis_index("c")
        me = 4 * mx + 2 * my + mc

        def src(p):
            return x_ref if gather else x_ref.at[p]

        mine = pltpu.make_async_copy(src(me), y_ref.at[me], local_sem)
        mine.start()
        copies = []
        for k in range(1, N_DEV):
            px = (1 - mx) if (k >> 2) & 1 else mx
            py = (1 - my) if (k >> 1) & 1 else my
            pc = (1 - mc) if k & 1 else mc
            cp = pltpu.make_async_remote_copy(
                src_ref=src(4 * px + 2 * py + pc), dst_ref=y_ref.at[me],
                send_sem=send_sems.at[k - 1], recv_sem=recv_sems.at[k - 1],
                device_id=(px, py, pc), device_id_type=MESH)
            cp.start()
            copies.append(cp)
        for cp in copies:
            cp.wait()
        mine.wait()

    return pl.pallas_call(
        body, name=name,
        out_shape=jax.ShapeDtypeStruct((N_DEV,) + tuple(blk), x.dtype),
        in_specs=[pl.BlockSpec(memory_space=pl.ANY)],
        out_specs=pl.BlockSpec(memory_space=pl.ANY),
        scratch_shapes=[pltpu.SemaphoreType.DMA((N_DEV - 1,)), pltpu.SemaphoreType.DMA((N_DEV - 1,)),
                        pltpu.SemaphoreType.DMA(())],
    )(x)


_HBM = pl.BlockSpec(memory_space=pl.ANY)


def _gather2(x, name):
    def body(x_ref, y_ref, send_sems, recv_sems, local_sem):
        mx, my, mc = lax.axis_index("x"), lax.axis_index("y"), lax.axis_index("c")
        me, sib = (mx, my, mc), (mx, my, 1 - mc)
        chips = [(1 - mx, my), (mx, 1 - my), (1 - mx, 1 - my)]

        def slot(px, py, pc):
            return y_ref.at[4 * px + 2 * py + pc]

        def copy(k, block, to, src=None):
            return pltpu.make_async_remote_copy(
                src_ref=slot(*block) if src is None else src, dst_ref=slot(*block),
                send_sem=send_sems.at[k], recv_sem=recv_sems.at[k], device_id=to, device_id_type=MESH)

        mine = pltpu.make_async_copy(x_ref, slot(*me), local_sem)
        mine.start()
        first = [copy(0, me, sib, src=x_ref)] + [copy(1 + j, me, (*chip, mc), src=x_ref) for j, chip in enumerate(chips)]
        for cp in first:
            cp.start()
        passed = [copy(4 + j, (*chip, mc), sib) for j, chip in enumerate(chips)]
        for j, chip in enumerate(chips):
            copy(1 + j, (*chip, mc), me).wait_recv()
            passed[j].start()
        copy(0, sib, me).wait_recv()
        for j, chip in enumerate(chips):
            copy(4 + j, (*chip, 1 - mc), me).wait_recv()
        for cp in first + passed:
            cp.wait_send()
        mine.wait()

    return pl.pallas_call(
        body, name=name,
        out_shape=jax.ShapeDtypeStruct((N_DEV,) + tuple(x.shape), x.dtype),
        in_specs=[_HBM], out_specs=_HBM,
        scratch_shapes=[pltpu.SemaphoreType.DMA((7,)), pltpu.SemaphoreType.DMA((7,)), pltpu.SemaphoreType.DMA(())],
    )(x)


def _swap_sibling(xs, name):
    n = len(xs)

    def body(*refs):
        send_sems, recv_sems = refs[2 * n], refs[2 * n + 1]
        sib = (lax.axis_index("x"), lax.axis_index("y"), 1 - lax.axis_index("c"))
        copies = [pltpu.make_async_remote_copy(src_ref=refs[a], dst_ref=refs[n + a], send_sem=send_sems.at[a],
                                               recv_sem=recv_sems.at[a], device_id=sib, device_id_type=MESH)
                  for a in range(n)]
        for cp in copies:
            cp.start()
        for cp in copies:
            cp.wait()

    return pl.pallas_call(
        body, name=name, out_shape=[jax.ShapeDtypeStruct(x.shape, x.dtype) for x in xs],
        in_specs=[_HBM] * n, out_specs=[_HBM] * n,
        scratch_shapes=[pltpu.SemaphoreType.DMA((n,)), pltpu.SemaphoreType.DMA((n,))],
    )(*xs)


def _exchange_chips(x, name):
    def body(x_ref, r_ref, send_sems, recv_sems, local_sem):
        mx, my, mc = lax.axis_index("x"), lax.axis_index("y"), lax.axis_index("c")
        here = 2 * mx + my
        mine = pltpu.make_async_copy(x_ref.at[here], r_ref.at[here], local_sem)
        mine.start()
        copies = []
        for j, (px, py) in enumerate([(1 - mx, my), (mx, 1 - my), (1 - mx, 1 - my)]):
            cp = pltpu.make_async_remote_copy(
                src_ref=x_ref.at[2 * px + py], dst_ref=r_ref.at[here], send_sem=send_sems.at[j], recv_sem=recv_sems.at[j],
                device_id=(px, py, mc), device_id_type=MESH)
            cp.start()
            copies.append(cp)
        for cp in copies:
            cp.wait()
        mine.wait()

    return pl.pallas_call(
        body, name=name, out_shape=jax.ShapeDtypeStruct(x.shape, x.dtype), in_specs=[_HBM], out_specs=_HBM,
        scratch_shapes=[pltpu.SemaphoreType.DMA((3,)), pltpu.SemaphoreType.DMA((3,)), pltpu.SemaphoreType.DMA(())],
    )(x)


_SEM = pl.BlockSpec(memory_space=pltpu.SEMAPHORE)
_HBM_SPEC = pl.BlockSpec(memory_space=pltpu.HBM)
_EFFECT = pltpu.CompilerParams(has_side_effects=pltpu.SideEffectType.DATAFLOW_SIDE_EFFECTING)


def _in_hbm(a):
    return pltpu.with_memory_space_constraint(a, pltpu.HBM)


def _chip_copies(x_refs, land_refs, send_sems, recv_sems, loc_sems):
    mx, my, mc = lax.axis_index("x"), lax.axis_index("y"), lax.axis_index("c")
    here = 2 * mx + my
    local, remote = [], []
    for a, (x_ref, land_ref) in enumerate(zip(x_refs, land_refs)):
        local.append(pltpu.make_async_copy(x_ref.at[here], land_ref.at[here], loc_sems.at[a]))
        for j, (px, py) in enumerate([(1 - mx, my), (mx, 1 - my), (1 - mx, 1 - my)]):
            remote.append(pltpu.make_async_remote_copy(
                src_ref=x_ref.at[2 * px + py], dst_ref=land_ref.at[here], send_sem=send_sems.at[3 * a + j],
                recv_sem=recv_sems.at[3 * a + j], device_id=(px, py, mc), device_id_type=MESH))
    return local, remote


def _chips_start(xs, name):
    n = len(xs)

    def body(*refs):
        local, remote = _chip_copies(refs[:n], refs[n:2 * n], *refs[2 * n:2 * n + 3])
        for cp in local + remote:
            cp.start()
        refs[-1][...] = jnp.zeros_like(refs[-1])

    lands = [lax.empty(x.shape, x.dtype) for x in xs]
    outs = pl.pallas_call(
        body, name=name,
        out_shape=(pltpu.SemaphoreType.DMA((3 * n,)), pltpu.SemaphoreType.DMA((3 * n,)), pltpu.SemaphoreType.DMA((n,)),
                   *[pltpu.HBM(x.shape, x.dtype) for x in xs], *[pltpu.HBM(x.shape, x.dtype) for x in xs],
                   jax.ShapeDtypeStruct((8, LANES), F32)),
        in_specs=[_HBM_SPEC] * (2 * n),
        out_specs=(_SEM, _SEM, _SEM, *[_HBM_SPEC] * (2 * n), pl.BlockSpec(memory_space=pltpu.VMEM)),
        input_output_aliases={i: 3 + i for i in range(2 * n)},
        compiler_params=_EFFECT,
    )(*[_in_hbm(x) for x in xs], *[_in_hbm(t) for t in lands])
    return (outs[0], outs[1], outs[2], list(outs[3:3 + n]), list(outs[3 + n:3 + 2 * n])), outs[-1]


def _chips_wait(handle, after, name):
    send_sems, recv_sems, loc_sems, x_thru, land_thru = handle
    n = len(x_thru)

    def body(*refs):
        local, remote = _chip_copies(refs[:n], refs[n:2 * n], *refs[2 * n:2 * n + 3])
        for cp in local:
            cp.wait()
        for cp in remote:
            cp.wait_send()
            cp.wait_recv()

    outs = pl.pallas_call(
        body, name=name,
        out_shape=[pltpu.HBM(x.shape, x.dtype) for x in x_thru + land_thru],
        in_specs=[_HBM_SPEC] * (2 * n) + [_SEM, _SEM, _SEM, _HBM],
        out_specs=[_HBM_SPEC] * (2 * n),
        input_output_aliases={i: i for i in range(2 * n)},
        compiler_params=_EFFECT,
    )(*x_thru, *land_thru, send_sems, recv_sems, loc_sems, after)
    return list(outs[n:])


def _gather_first_copies(x_refs, y_refs, send_sems, sib_recv, ici_recv, loc_sems):
    mx, my, mc = lax.axis_index("x"), lax.axis_index("y"), lax.axis_index("c")
    me = 4 * mx + 2 * my + mc
    local, remote = [], []
    for a, (x_ref, y_ref) in enumerate(zip(x_refs, y_refs)):
        local.append(pltpu.make_async_copy(x_ref, y_ref.at[me], loc_sems.at[a]))
        remote.append(pltpu.make_async_remote_copy(
            src_ref=x_ref, dst_ref=y_ref.at[me], send_sem=send_sems.at[4 * a], recv_sem=sib_recv.at[a],
            device_id=(mx, my, 1 - mc), device_id_type=MESH))
        for j, (px, py) in enumerate([(1 - mx, my), (mx, 1 - my), (1 - mx, 1 - my)]):
            remote.append(pltpu.make_async_remote_copy(
                src_ref=x_ref, dst_ref=y_ref.at[me], send_sem=send_sems.at[4 * a + 1 + j], recv_sem=ici_recv.at[3 * a + j],
                device_id=(px, py, mc), device_id_type=MESH))
    return local, remote


def _gather_forward_copies(y_refs, ici_recv, fwd_send, fwd_recv):
    mx, my, mc = lax.axis_index("x"), lax.axis_index("y"), lax.axis_index("c")
    pairs = []
    for a, y_ref in enumerate(y_refs):
        for j, (px, py) in enumerate([(1 - mx, my), (mx, 1 - my), (1 - mx, 1 - my)]):
            slot = y_ref.at[4 * px + 2 * py + mc]
            arrival = pltpu.make_async_remote_copy(
                src_ref=slot, dst_ref=slot, send_sem=fwd_send.at[3 * a + j], recv_sem=ici_recv.at[3 * a + j],
                device_id=(px, py, mc), device_id_type=MESH)
            onward = pltpu.make_async_remote_copy(
                src_ref=slot, dst_ref=slot, send_sem=fwd_send.at[3 * a + j], recv_sem=fwd_recv.at[3 * a + j],
                device_id=(mx, my, 1 - mc), device_id_type=MESH)
            pairs.append((arrival, onward))
    return pairs


def _gather_start(xs, name):
    n = len(xs)

    def body(*refs):
        local, remote = _gather_first_copies(refs[:n], refs[n:2 * n], *refs[2 * n:2 * n + 4])
        for cp in local + remote:
            cp.start()

    ys = [lax.empty((N_DEV,) + tuple(x.shape), x.dtype) for x in xs]
    dma = pltpu.SemaphoreType.DMA
    outs = pl.pallas_call(
        body, name=name,
        out_shape=(dma((4 * n,)), dma((n,)), dma((3 * n,)), dma((n,)),
                   *[pltpu.HBM(x.shape, x.dtype) for x in xs], *[pltpu.HBM(y.shape, y.dtype) for y in ys]),
        in_specs=[_HBM_SPEC] * (2 * n),
        out_specs=(_SEM, _SEM, _SEM, _SEM, *[_HBM_SPEC] * (2 * n)),
        input_output_aliases={i: 4 + i for i in range(2 * n)},
        compiler_params=_EFFECT,
    )(*[_in_hbm(x) for x in xs], *[_in_hbm(y) for y in ys])
    return dict(send=outs[0], sib_recv=outs[1], ici_recv=outs[2], loc=outs[3], x=list(outs[4:4 + n]), y=list(outs[4 + n:]))


def _gather_mid(h, after, name):
    n = len(h["y"])

    def body(*refs):
        for arrival, onward in _gather_forward_copies(refs[:n], refs[n], refs[n + 2 + n], refs[n + 3 + n]):
            arrival.wait_recv()
            onward.start()

    dma = pltpu.SemaphoreType.DMA
    outs = pl.pallas_call(
        body, name=name,
        out_shape=(*[pltpu.HBM(y.shape, y.dtype) for y in h["y"]], dma((3 * n,)), dma((3 * n,))),
        in_specs=[_HBM_SPEC] * n + [_SEM, _HBM],
        out_specs=(*[_HBM_SPEC] * n, _SEM, _SEM),
        input_output_aliases={i: i for i in range(n)},
        compiler_params=_EFFECT,
    )(*h["y"], h["ici_recv"], after)
    return dict(h, y=list(outs[:n]), fwd_send=outs[n], fwd_recv=outs[n + 1])


def _gather_wait(h, after, name):
    n = len(h["y"])

    def body(*refs):
        x_refs, y_refs = refs[:n], refs[n:2 * n]
        send, sib_recv, loc, fwd_send, fwd_recv = refs[2 * n:2 * n + 5]
        local, remote = _gather_first_copies(x_refs, y_refs, send, sib_recv, fwd_recv, loc)
        for cp in local:
            cp.wait()
        for k, cp in enumerate(remote):
            cp.wait_send()
            if k % 4 == 0:
                cp.wait_recv()
        for _, onward in _gather_forward_copies(y_refs, fwd_recv, fwd_send, fwd_recv):
            onward.wait_send()
            onward.wait_recv()

    outs = pl.pallas_call(
        body, name=name,
        out_shape=[pltpu.HBM(t.shape, t.dtype) for t in h["x"] + h["y"]],
        in_specs=[_HBM_SPEC] * (2 * n) + [_SEM] * 5 + [_HBM],
        out_specs=[_HBM_SPEC] * (2 * n),
        input_output_aliases={i: i for i in range(2 * n)},
        compiler_params=_EFFECT,
    )(*h["x"], *h["y"], h["send"], h["sib_recv"], h["loc"], h["fwd_send"], h["fwd_recv"], after)
    return list(outs[n:])


def _pair_sum(a, b, name):
    R, C = a.shape
    tm = _tile(R, 512, 16)

    def body(a_ref, b_ref, o_ref):
        o_ref[...] = (a_ref[...].astype(F32) + b_ref[...].astype(F32)).astype(BF16)

    big = pl.BlockSpec((tm, C), lambda i: (i, 0))
    return pl.pallas_call(
        body, name=name, grid=(R // tm,), in_specs=[big, big], out_specs=big,
        out_shape=jax.ShapeDtypeStruct((R, C), BF16), compiler_params=_params(("parallel",)),
    )(a, b)


def _mm(pairs, mode, out_dtype, name, tm=512, tn=512):
    dims = {"nn": NN, "nt": NT, "tn": TN}[mode]
    a0, b0 = pairs[0][0], pairs[0][1]
    M = a0.shape[1] if mode == "tn" else a0.shape[0]
    N = b0.shape[0] if mode == "nt" else b0.shape[1]
    tm, tn = _tile(M, tm), _tile(N, tn)
    in_specs, args = [], []
    for pr in pairs:
        a, b = pr[0], pr[1]
        if mode == "tn":
            K = a.shape[0]
            in_specs.append(pl.BlockSpec((K, tm), lambda i, j: (0, i)))
            in_specs.append(pl.BlockSpec((K, tn), lambda i, j: (0, j)))
        elif mode == "nn":
            K = a.shape[1]
            in_specs.append(pl.BlockSpec((tm, K), lambda i, j: (i, 0)))
            in_specs.append(pl.BlockSpec((K, tn), lambda i, j: (0, j)))
        else:
            K = a.shape[1]
            cb = pr[2] if len(pr) > 2 else 0
            in_specs.append(pl.BlockSpec((tm, K), lambda i, j: (i, 0)))
            in_specs.append(pl.BlockSpec((tn, K), functools.partial(lambda i, j, cb: (j, cb), cb=cb)))
        args += [a, b]
    n_pairs = len(pairs)

    def body(*refs):
        o_ref = refs[-1]
        acc = None
        for k in range(n_pairs):
            d = _dot(refs[2 * k][...], refs[2 * k + 1][...], dims)
            acc = d if acc is None else acc + d
        o_ref[...] = acc.astype(o_ref.dtype)

    return pl.pallas_call(
        body, name=name, grid=(M // tm, N // tn), in_specs=in_specs,
        out_specs=pl.BlockSpec((tm, tn), lambda i, j: (i, j)),
        out_shape=jax.ShapeDtypeStruct((M, N), out_dtype),
        compiler_params=_params(("parallel", "arbitrary")),
    )(*args)


def _prenorm(x, g, sc, sh, name):
    T, D = x.shape
    tm = _tile(T, 512, 8)

    def body(x_ref, g_ref, sc_ref, sh_ref, h_ref):
        xv = x_ref[...]
        r = lax.rsqrt(jnp.mean(xv * xv, axis=-1, keepdims=True) + EPS)
        h_ref[...] = (((xv * r) * g_ref[...]) * (1.0 + sc_ref[...]) + sh_ref[...]).astype(BF16)

    row = pl.BlockSpec((1, D), lambda i: (0, 0))
    return pl.pallas_call(
        body, name=name, grid=(T // tm,),
        in_specs=[pl.BlockSpec((tm, D), lambda i: (i, 0)), row, row, row],
        out_specs=pl.BlockSpec((tm, D), lambda i: (i, 0)),
        out_shape=jax.ShapeDtypeStruct((T, D), BF16),
        compiler_params=_params(("parallel",)),
    )(x, g, sc, sh)


def _prenorm_bwd(dh, x, dres, g, sc, name):
    T, D = x.shape
    tm = _tile(T, 256, 8)

    def body(dh_ref, x_ref, dres_ref, g_ref, sc_ref, dx_ref, sm_ref):
        @pl.when(pl.program_id(0) == 0)
        def _():
            sm_ref[...] = jnp.zeros_like(sm_ref)

        xv, dhv = x_ref[...], dh_ref[...]
        r = lax.rsqrt(jnp.mean(xv * xv, axis=-1, keepdims=True) + EPS)
        xh = xv * r
        one_sc = 1.0 + sc_ref[...]
        sm_ref[0:1, :] += jnp.sum(dhv, axis=0, keepdims=True)
        sm_ref[1:2, :] += jnp.sum(dhv * (xh * g_ref[...]), axis=0, keepdims=True)
        sm_ref[2:3, :] += jnp.sum(dhv * one_sc * xh, axis=0, keepdims=True)
        dxh = dhv * one_sc * g_ref[...]
        dx_ref[...] = dres_ref[...] + r * (dxh - xh * jnp.mean(dxh * xh, axis=-1, keepdims=True))

    row = pl.BlockSpec((1, D), lambda i: (0, 0))
    big = pl.BlockSpec((tm, D), lambda i: (i, 0))
    return pl.pallas_call(
        body, name=name, grid=(T // tm,),
        in_specs=[big, big, big, row, row],
        out_specs=[big, pl.BlockSpec((8, D), lambda i: (0, 0))],
        out_shape=(jax.ShapeDtypeStruct((T, D), F32), jax.ShapeDtypeStruct((8, D), F32)),
        compiler_params=_params(("arbitrary",)),
    )(dh, x, dres, g, sc)


def _mm_postnorm(a, w, x, gate, gpost, name):
    T, K = a.shape
    D = w.shape[1]
    tm = _tile(T, 256, 8)

    def body(a_ref, w_ref, x_ref, gate_ref, gp_ref, y_ref, xn_ref):
        y = _dot(a_ref[...], w_ref[...], NN)
        r = lax.rsqrt(jnp.mean(y * y, axis=-1, keepdims=True) + EPS)
        y_ref[...] = y
        xn_ref[...] = x_ref[...] + gate_ref[...] * ((y * r) * gp_ref[...])

    row = pl.BlockSpec((1, D), lambda i: (0, 0))
    big = pl.BlockSpec((tm, D), lambda i: (i, 0))
    return pl.pallas_call(
        body, name=name, grid=(T // tm,),
        in_specs=[pl.BlockSpec((tm, K), lambda i: (i, 0)), pl.BlockSpec((K, D), lambda i: (0, 0)), big, row, row],
        out_specs=[big, big],
        out_shape=(jax.ShapeDtypeStruct((T, D), F32), jax.ShapeDtypeStruct((T, D), F32)),
        compiler_params=_params(("parallel",)),
    )(a, w, x, gate, gpost)


def _postnorm_bwd(dx, y, gate, gpost, name):
    T, D = y.shape
    tm = _tile(T, 256, 8)

    def body(dx_ref, y_ref, gate_ref, gp_ref, dy_ref, sm_ref):
        @pl.when(pl.program_id(0) == 0)
        def _():
            sm_ref[...] = jnp.zeros_like(sm_ref)

        yv, dxv = y_ref[...], dx_ref[...]
        r = lax.rsqrt(jnp.mean(yv * yv, axis=-1, keepdims=True) + EPS)
        yh = yv * r
        dn = dxv * gate_ref[...]
        sm_ref[0:1, :] += jnp.sum(dxv * (yh * gp_ref[...]), axis=0, keepdims=True)
        sm_ref[1:2, :] += jnp.sum(dn * yh, axis=0, keepdims=True)
        dyh = dn * gp_ref[...]
        dy_ref[...] = (r * (dyh - yh * jnp.mean(dyh * yh, axis=-1, keepdims=True))).astype(BF16)

    row = pl.BlockSpec((1, D), lambda i: (0, 0))
    big = pl.BlockSpec((tm, D), lambda i: (i, 0))
    return pl.pallas_call(
        body, name=name, grid=(T // tm,),
        in_specs=[big, big, row, row],
        out_specs=[big, pl.BlockSpec((8, D), lambda i: (0, 0))],
        out_shape=(jax.ShapeDtypeStruct((T, D), BF16), jax.ShapeDtypeStruct((8, D), F32)),
        compiler_params=_params(("arbitrary",)),
    )(dx, y, gate, gpost)


def _ffn_in_fwd(h, w, name):
    T, D = h.shape
    F = w.shape[1] // 2
    tm, tn = _tile(T, 1024), _tile(F, 256)
    nj = F // tn

    def body(h_ref, wg_ref, wu_ref, g_ref, u_ref, act_ref):
        hv = h_ref[...]
        g = _dot(hv, wg_ref[...], NN)
        u = _dot(hv, wu_ref[...], NN)
        g_ref[...] = g.astype(BF16)
        u_ref[...] = u.astype(BF16)
        act_ref[...] = ((g * _sigmoid(g)) * u).astype(BF16)

    out = pl.BlockSpec((tm, tn), lambda i, j: (i, j))
    return pl.pallas_call(
        body, name=name, grid=(T // tm, nj),
        in_specs=[pl.BlockSpec((tm, D), lambda i, j: (i, 0)),
                  pl.BlockSpec((D, tn), lambda i, j: (0, j)),
                  pl.BlockSpec((D, tn), lambda i, j: (0, j + nj))],
        out_specs=[out, out, out],
        out_shape=tuple(jax.ShapeDtypeStruct((T, F), BF16) for _ in range(3)),
        compiler_params=_params(("parallel", "arbitrary")),
    )(h, w, w)


def _ffn_act_bwd(dy, w_out, g, u, name):
    T, D = dy.shape
    F = w_out.shape[0]
    tm, tn = _tile(T, 1024), _tile(F, 256)

    def body(dy_ref, w_ref, g_ref, u_ref, dg_ref, du_ref):
        dact = _dot(dy_ref[...], w_ref[...], NT)
        gv, uv = g_ref[...].astype(F32), u_ref[...].astype(F32)
        sg = _sigmoid(gv)
        dg_ref[...] = (dact * uv * (sg * (1.0 + gv * (1.0 - sg)))).astype(BF16)
        du_ref[...] = (dact * (gv * sg)).astype(BF16)

    tile = pl.BlockSpec((tm, tn), lambda i, j: (i, j))
    return pl.pallas_call(
        body, name=name, grid=(T // tm, F // tn),
        in_specs=[pl.BlockSpec((tm, D), lambda i, j: (i, 0)), pl.BlockSpec((tn, D), lambda i, j: (j, 0)), tile, tile],
        out_specs=[tile, tile],
        out_shape=(jax.ShapeDtypeStruct((T, F), BF16), jax.ShapeDtypeStruct((T, F), BF16)),
        compiler_params=_params(("parallel", "arbitrary")),
    )(dy, w_out, g, u)


def _lane_scan(v, reverse):
    T = v.shape[-1]
    lane = lax.broadcasted_iota(jnp.int32, v.shape, 1)
    d = 1
    while d < T:
        if reverse:
            v = v + jnp.where(lane < T - d, pltpu.roll(v, T - d, axis=1), 0.0)
        else:
            v = v + jnp.where(lane >= d, pltpu.roll(v, d, axis=1), 0.0)
        d *= 2
    return v


def _fgate_fwd(h, wf_t, bf, name):
    T, D = h.shape
    R = wf_t.shape[0]

    def body(h_ref, w_ref, b_ref, fl_ref, cum_ref):
        fl = _dot(w_ref[...], h_ref[...], NT) + b_ref[...]
        fl_ref[...] = fl
        logf = jnp.minimum(fl, 0.0) - jnp.log(1.0 + jnp.exp(-jnp.abs(fl)))
        cum_ref[...] = _lane_scan(logf, reverse=False)

    return pl.pallas_call(
        body, name=name,
        out_shape=(jax.ShapeDtypeStruct((R, T), F32), jax.ShapeDtypeStruct((R, T), F32)),
        compiler_params=_params(),
    )(h, wf_t, bf)


def _fgate_bwd(dcum, fl, h, name):
    R, T = fl.shape
    D = h.shape[1]

    def body(dc_ref, fl_ref, h_ref, dfl_ref, dw_ref, db_ref):
        dlogf = _lane_scan(dc_ref[...], reverse=True)
        dfl = dlogf * _sigmoid(-fl_ref[...])
        dfl_ref[...] = dfl
        dw_ref[...] = _dot(dfl.astype(BF16), h_ref[...], NN)
        db_ref[...] = jnp.broadcast_to(jnp.sum(dfl, axis=-1, keepdims=True), (R, LANES))

    return pl.pallas_call(
        body, name=name,
        out_shape=(jax.ShapeDtypeStruct((R, T), F32), jax.ShapeDtypeStruct((R, D), F32),
                   jax.ShapeDtypeStruct((R, LANES), F32)),
        compiler_params=_params(),
    )(dcum, fl, h)


def _head_masks(hpb, dh, rows):
    lane = lax.broadcasted_iota(jnp.int32, (rows, LANES), 1)
    return [(lane >= h * dh) & (lane < (h + 1) * dh) for h in range(hpb)]


def _stack_heads(v, masks):
    return jnp.concatenate([jnp.where(mk, v, jnp.zeros_like(v)) for mk in masks], axis=0)


def _heads_to_lanes(col, masks, tq):
    out = jnp.broadcast_to(col[0:tq], (tq, LANES))
    for h in range(1, len(masks)):
        out = jnp.where(masks[h], col[h * tq:(h + 1) * tq], out)
    return out


def _causal_stack(hpb, tq):
    r = lax.broadcasted_iota(jnp.int32, (tq, tq), 0)
    c = lax.broadcasted_iota(jnp.int32, (tq, tq), 1)
    return jnp.concatenate([c] * hpb, axis=0) <= jnp.concatenate([r] * hpb, axis=0)


def _attn_fwd(qkv, cum_rows, n_heads, name, tq):
    T = qkv.shape[0]
    A = qkv.shape[1] // 3
    dh = A // n_heads
    hpb = LANES // dh
    nb = A // LANES
    nq = T // tq
    scale = dh ** -0.5

    def body(q_ref, k_ref, v_ref, c_ref, o_ref, l_ref, vbd):
        hp, i = pl.program_id(0), pl.program_id(1)
        masks = _head_masks(hpb, dh, tq)

        @pl.when(i == 0)
        def _():
            def fill(j, _):
                vbd[j] = _stack_heads(v_ref[pl.ds(pl.multiple_of(j * tq, tq), tq), :], masks)
                return 0

            lax.fori_loop(0, nq, fill, 0)

        qs = _stack_heads(q_ref[...], masks)
        crow0 = hp * hpb * nq

        def tile(j, carry, diag):
            m, l, acc = carry
            kt = k_ref[pl.ds(pl.multiple_of(j * tq, tq), tq), :]
            bias = jnp.concatenate(
                [jnp.broadcast_to(c_ref[pl.ds(crow0 + h * nq + j, 1), :], (tq, tq)) for h in range(hpb)], axis=0)
            s = _dot(qs, kt, NT) * scale - bias
            if diag:
                s = jnp.where(_causal_stack(hpb, tq), s, NEG)
            m_new = jnp.maximum(m, jnp.max(s, axis=-1, keepdims=True))
            p = jnp.exp(s - m_new)
            alpha = jnp.exp(m - m_new)
            l = alpha * l + jnp.sum(p, axis=-1, keepdims=True)
            pcat = jnp.concatenate([p[h * tq:(h + 1) * tq] for h in range(hpb)], axis=1).astype(BF16)
            acc = _heads_to_lanes(alpha, masks, tq) * acc + _dot(pcat, vbd[j], NN)
            return m_new, l, acc

        init = (jnp.full((hpb * tq, 1), NEG, F32), jnp.zeros((hpb * tq, 1), F32), jnp.zeros((tq, LANES), F32))
        carry = lax.fori_loop(0, i, lambda j, c: tile(j, c, False), init)
        m, l, acc = tile(i, carry, True)
        o_ref[...] = (acc / _heads_to_lanes(l, masks, tq)).astype(BF16)
        lse = m + jnp.log(l)
        for h in range(hpb):
            l_ref[:, h:h + 1] = lse[h * tq:(h + 1) * tq]

    return pl.pallas_call(
        body, name=name, grid=(nb, nq),
        in_specs=[pl.BlockSpec((tq, LANES), lambda h, i: (i, h)),
                  pl.BlockSpec((T, LANES), lambda h, i: (0, nb + h)),
                  pl.BlockSpec((T, LANES), lambda h, i: (0, 2 * nb + h)),
                  pl.BlockSpec(cum_rows.shape, lambda h, i: (0, 0))],
        out_specs=[pl.BlockSpec((tq, LANES), lambda h, i: (i, h)),
                   pl.BlockSpec((None, tq, hpb), lambda h, i: (h, i, 0))],
        out_shape=(jax.ShapeDtypeStruct((T, A), BF16), jax.ShapeDtypeStruct((nb, T, hpb), F32)),
        scratch_shapes=[pltpu.VMEM((nq, hpb * tq, LANES), BF16)],
        compiler_params=_params(("arbitrary", "arbitrary")),
    )(qkv, qkv, qkv, cum_rows)


def _attn_bwd(qkv, dcat, o, lse, cum_rows, n_heads, name, tq):
    T = qkv.shape[0]
    A = qkv.shape[1] // 3
    dh = A // n_heads
    hpb = LANES // dh
    nb = A // LANES
    nq = T // tq
    scale = dh ** -0.5

    def body(q_ref, k_ref, v_ref, do_ref, o_ref, l_ref, c_ref, dqkv_ref, dc_ref, dr_ref,
             dq_acc, delta, drow, qs_scr, dos_scr, kbd_scr):
        hp = pl.program_id(0)
        masks = _head_masks(hpb, dh, tq)
        crow0 = hp * hpb * nq

        def prologue(i, _):
            rs = pl.ds(pl.multiple_of(i * tq, tq), tq)
            do = do_ref[rs, :]
            prod = do * o_ref[rs, :].astype(F32)
            for h in range(hpb):
                delta[rs, h:h + 1] = jnp.sum(jnp.where(masks[h], prod, 0.0), axis=-1, keepdims=True)
            qs_scr[i] = _stack_heads(q_ref[rs, :], masks)
            dos_scr[i] = _stack_heads(do, masks).astype(BF16)
            kbd_scr[i] = _stack_heads(k_ref[rs, :], masks)
            dq_acc[rs, :] = jnp.zeros((tq, LANES), F32)
            drow[rs, :] = jnp.zeros((tq, hpb), F32)
            return 0

        lax.fori_loop(0, nq, prologue, 0)

        def kv_step(j, _):
            ks = pl.ds(pl.multiple_of(j * tq, tq), tq)
            kt, vt = k_ref[ks, :], v_ref[ks, :]
            kbd = kbd_scr[j]
            bias = jnp.concatenate(
                [jnp.broadcast_to(c_ref[pl.ds(crow0 + h * nq + j, 1), :], (tq, tq)) for h in range(hpb)], axis=0)

            def q_step(i, carry, diag):
                dk, dv, dcs = carry
                rs = pl.ds(pl.multiple_of(i * tq, tq), tq)
                qs, dos = qs_scr[i], dos_scr[i]
                s = _dot(qs, kt, NT) * scale - bias
                if diag:
                    s = jnp.where(_causal_stack(hpb, tq), s, NEG)
                lse = jnp.concatenate([l_ref[rs, h:h + 1] for h in range(hpb)], axis=0)
                p = jnp.exp(s - lse)
                dv = dv + _dot(p.astype(BF16), dos, TN)
                dp = _dot(dos, vt, NT)
                ds = p * (dp - jnp.concatenate([delta[rs, h:h + 1] for h in range(hpb)], axis=0))
                dcs = tuple(dcs[h] - jnp.sum(ds[h * tq:(h + 1) * tq], axis=0, keepdims=True) for h in range(hpb))
                rsum = jnp.sum(ds, axis=-1, keepdims=True)
                for h in range(hpb):
                    drow[rs, h:h + 1] += rsum[h * tq:(h + 1) * tq]
                dsb = (ds * scale).astype(BF16)
                dk = dk + _dot(dsb, qs, TN)
                dscat = jnp.concatenate([dsb[h * tq:(h + 1) * tq] for h in range(hpb)], axis=1)
                dq_acc[rs, :] += _dot(dscat, kbd, NN)
                return dk, dv, dcs

            init = (jnp.zeros((tq, LANES), F32), jnp.zeros((tq, LANES), F32),
                    tuple(jnp.zeros((1, tq), F32) for _ in range(hpb)))
            carry = q_step(j, init, True)
            dk, dv, dcs = lax.fori_loop(j + 1, nq, lambda i, c: q_step(i, c, False), carry)
            dqkv_ref[1, ks, :] = dk.astype(BF16)
            dqkv_ref[2, ks, :] = dv.astype(BF16)
            for h in range(hpb):
                dc_ref[pl.ds(crow0 + h * nq + j, 1), :] = dcs[h]
            return 0

        lax.fori_loop(0, nq, kv_step, 0)
        dqkv_ref[0] = dq_acc[...].astype(BF16)
        dr_ref[...] = drow[...]

    col = lambda off: pl.BlockSpec((T, LANES), functools.partial(lambda h, off: (0, off + h), off=off))
    return pl.pallas_call(
        body, name=name, grid=(nb,),
        in_specs=[col(0), col(nb), col(2 * nb), col(0), col(0),
                  pl.BlockSpec((None, T, hpb), lambda h: (h, 0, 0)),
                  pl.BlockSpec(cum_rows.shape, lambda h: (0, 0))],
        out_specs=[pl.BlockSpec((3, T, LANES), lambda h: (0, 0, h)),
                   pl.BlockSpec(cum_rows.shape, lambda h: (0, 0)),
                   pl.BlockSpec((None, T, hpb), lambda h: (h, 0, 0))],
        out_shape=(jax.ShapeDtypeStruct((3, T, A), BF16), jax.ShapeDtypeStruct(cum_rows.shape, F32),
                   jax.ShapeDtypeStruct((nb, T, hpb), F32)),
        scratch_shapes=[pltpu.VMEM((T, LANES), F32), pltpu.VMEM((T, hpb), F32), pltpu.VMEM((T, hpb), F32),
                        pltpu.VMEM((nq, hpb * tq, LANES), BF16), pltpu.VMEM((nq, hpb * tq, LANES), BF16),
                        pltpu.VMEM((nq, hpb * tq, LANES), BF16)],
        compiler_params=_params(("arbitrary",)),
    )(qkv, qkv, qkv, dcat, o, lse, cum_rows)


def _glu_into(upad, cv_ref, cg_ref, T):
    upad[0:CONV_PAD, :] = jnp.zeros((CONV_PAD, upad.shape[1]), F32)

    def fill(c, _):
        rs = pl.ds(pl.multiple_of(c * CONV_CHUNK, CONV_CHUNK), CONV_CHUNK)
        upad[pl.ds(pl.multiple_of(CONV_PAD + c * CONV_CHUNK, 8), CONV_CHUNK), :] = cv_ref[rs, :] * _sigmoid(cg_ref[rs, :])
        return 0

    lax.fori_loop(0, T // CONV_CHUNK, fill, 0)


def _conv_taps(win, w_ref, first, step):
    acc = None
    for k in range(CONV_K):
        o = first + step * k
        t = w_ref[k:k + 1, :] * win[o:o + CONV_CHUNK, :]
        acc = t if acc is None else acc + t
    return acc


def _conv_fwd(cproj, w, b, lg, lb, name):
    T = cproj.shape[0]
    C = cproj.shape[1] // 2
    off = CONV_PAD - (CONV_K - 1)

    def body(cv_ref, cg_ref, w_ref, b_ref, lg_ref, lb_ref, out_ref, upad, win):
        _glu_into(upad, cv_ref, cg_ref, T)

        def chunk(c, _):
            r0 = pl.multiple_of(c * CONV_CHUNK, CONV_CHUNK)
            win[...] = upad[pl.ds(r0, CONV_CHUNK + CONV_PAD), :]
            u1 = _conv_taps(win, w_ref, off, 1) + b_ref[...]
            mu = jnp.mean(u1, axis=-1, keepdims=True)
            var = jnp.mean(jnp.square(u1 - mu), axis=-1, keepdims=True)
            u2 = ((u1 - mu) * lax.rsqrt(var + EPS)) * lg_ref[...] + lb_ref[...]
            out_ref[pl.ds(r0, CONV_CHUNK), :] = (u2 * _sigmoid(u2)).astype(BF16)
            return 0

        lax.fori_loop(0, T // CONV_CHUNK, chunk, 0)

    row = pl.BlockSpec((1, C), lambda i: (0, 0))
    return pl.pallas_call(
        body, name=name, grid=(1,),
        in_specs=[pl.BlockSpec((T, C), lambda i: (0, 0)), pl.BlockSpec((T, C), lambda i: (0, 1)),
                  pl.BlockSpec(w.shape, lambda i: (0, 0)), row, row, row],
        out_specs=pl.BlockSpec((T, C), lambda i: (0, 0)),
        out_shape=jax.ShapeDtypeStruct((T, C), BF16),
        scratch_shapes=[pltpu.VMEM((T + CONV_PAD, C), F32), pltpu.VMEM((CONV_CHUNK + CONV_PAD, C), F32)],
        compiler_params=_params(("arbitrary",)),
    )(cproj, cproj, w, b, lg, lb)


def _conv_bwd(cproj, dcat, w, b, lg, lb, name):
    T = cproj.shape[0]
    C = cproj.shape[1] // 2
    off = CONV_PAD - (CONV_K - 1)
    n_chunks = T // CONV_CHUNK

    def fold(v):
        return jnp.sum(v.reshape(CONV_CHUNK // 8, 8, C), axis=0)

    def body(cv_ref, cg_ref, du_ref, w_ref, b_ref, lg_ref, lb_ref, dc_ref, dw_ref, sm_ref, upad, dpad, win, dwacc, smacc):
        _glu_into(upad, cv_ref, cg_ref, T)
        dpad[pl.ds(T, CONV_PAD), :] = jnp.zeros((CONV_PAD, C), F32)
        dwacc[...] = jnp.zeros_like(dwacc)
        smacc[...] = jnp.zeros_like(smacc)

        def chunk_a(c, _):
            r0 = pl.multiple_of(c * CONV_CHUNK, CONV_CHUNK)
            win[...] = upad[pl.ds(r0, CONV_CHUNK + CONV_PAD), :]
            u1 = _conv_taps(win, w_ref, off, 1) + b_ref[...]
            mu = jnp.mean(u1, axis=-1, keepdims=True)
            var = jnp.mean(jnp.square(u1 - mu), axis=-1, keepdims=True)
            rstd = lax.rsqrt(var + EPS)
            u1h = (u1 - mu) * rstd
            u2 = u1h * lg_ref[...] + lb_ref[...]
            sg = _sigmoid(u2)
            du2 = du_ref[pl.ds(r0, CONV_CHUNK), :] * (sg * (1.0 + u2 * (1.0 - sg)))
            smacc[8:16, :] += fold(du2 * u1h)
            smacc[16:24, :] += fold(du2)
            du1h = du2 * lg_ref[...]
            du1 = rstd * (du1h - jnp.mean(du1h, axis=-1, keepdims=True)
                          - u1h * jnp.mean(du1h * u1h, axis=-1, keepdims=True))
            smacc[0:8, :] += fold(du1)
            dpad[pl.ds(r0, CONV_CHUNK), :] = du1
            for k in range(CONV_K):
                dwacc[8 * k:8 * k + 8, :] += fold(du1 * win[off + k:off + k + CONV_CHUNK, :])
            return 0

        lax.fori_loop(0, n_chunks, chunk_a, 0)

        def chunk_b(c, _):
            r0 = pl.multiple_of(c * CONV_CHUNK, CONV_CHUNK)
            rs = pl.ds(r0, CONV_CHUNK)
            win[...] = dpad[pl.ds(r0, CONV_CHUNK + CONV_PAD), :]
            du0 = _conv_taps(win, w_ref, CONV_K - 1, -1)
            cv, sg = cv_ref[rs, :], _sigmoid(cg_ref[rs, :])
            dc_ref[rs, 0:C] = (du0 * sg).astype(BF16)
            dc_ref[rs, C:2 * C] = (du0 * cv * (sg * (1.0 - sg))).astype(BF16)
            return 0

        lax.fori_loop(0, n_chunks, chunk_b, 0)
        for k in range(CONV_K):
            dw_ref[k:k + 1, :] = jnp.sum(dwacc[8 * k:8 * k + 8, :], axis=0, keepdims=True)
        dw_ref[CONV_K:CONV_PAD, :] = jnp.zeros((CONV_PAD - CONV_K, C), F32)
        for r in range(3):
            sm_ref[r:r + 1, :] = jnp.sum(smacc[8 * r:8 * r + 8, :], axis=0, keepdims=True)
        sm_ref[3:8, :] = jnp.zeros((5, C), F32)

    row = pl.BlockSpec((1, C), lambda i: (0, 0))
    return pl.pallas_call(
        body, name=name, grid=(1,),
        in_specs=[pl.BlockSpec((T, C), lambda i: (0, 0)), pl.BlockSpec((T, C), lambda i: (0, 1)),
                  pl.BlockSpec((T, C), lambda i: (0, 1)),
                  pl.BlockSpec(w.shape, lambda i: (0, 0)), row, row, row],
        out_specs=[pl.BlockSpec((T, 2 * C), lambda i: (0, 0)), pl.BlockSpec((CONV_PAD, C), lambda i: (0, 0)),
                   pl.BlockSpec((8, C), lambda i: (0, 0))],
        out_shape=(jax.ShapeDtypeStruct((T, 2 * C), BF16), jax.ShapeDtypeStruct((CONV_PAD, C), F32),
                   jax.ShapeDtypeStruct((8, C), F32)),
        scratch_shapes=[pltpu.VMEM((T + CONV_PAD, C), F32), pltpu.VMEM((T + CONV_PAD, C), F32),
                        pltpu.VMEM((CONV_CHUNK + CONV_PAD, C), F32), pltpu.VMEM((8 * CONV_PAD, C), F32),
                        pltpu.VMEM((24, C), F32)],
        compiler_params=_params(("arbitrary",)),
    )(cproj, cproj, dcat, w, b, lg, lb)


def _loss_head(x, target, name):
    T, D = x.shape
    tm = _tile(T, 512, 8)

    def body(x_ref, t_ref, loss_ref, dx_ref):
        @pl.when(pl.program_id(0) == 0)
        def _():
            loss_ref[...] = jnp.zeros_like(loss_ref)

        err = x_ref[...] - t_ref[...]
        part = jnp.sum(jnp.mean(err * err, axis=-1, keepdims=True), axis=0, keepdims=True)
        loss_ref[...] += jnp.broadcast_to(0.5 * part, loss_ref.shape)
        dx_ref[...] = err * (1.0 / D)

    big = pl.BlockSpec((tm, D), lambda i: (i, 0))
    return pl.pallas_call(
        body, name=name, grid=(T // tm,),
        in_specs=[big, big],
        out_specs=[pl.BlockSpec((8, LANES), lambda i: (0, 0)), big],
        out_shape=(jax.ShapeDtypeStruct((8, LANES), F32), jax.ShapeDtypeStruct((T, D), F32)),
        compiler_params=_params(("arbitrary",)),
    )(x, target)


def _ada_fwd(c_all, ada_w, ada_b_loc, name):
    L, D, S = ada_w.shape
    B = c_all.shape[0]

    def body(c_ref, w_ref, b_ref, o_ref):
        c = c_ref[...]
        ca = (c * _sigmoid(c)).astype(BF16)
        o_ref[...] = _dot(ca, w_ref[...].astype(BF16), NN) + b_ref[...]

    return pl.pallas_call(
        body, name=name, grid=(L,),
        in_specs=[pl.BlockSpec((B, D), lambda l: (0, 0)), pl.BlockSpec((None, D, S), lambda l: (l, 0, 0)),
                  pl.BlockSpec((None, 1, S), lambda l: (l, 0, 0))],
        out_specs=pl.BlockSpec((None, B, S), lambda l: (l, 0, 0)),
        out_shape=jax.ShapeDtypeStruct((L, B, S), F32),
        compiler_params=_params(("parallel",)),
    )(c_all, ada_w, ada_b_loc)


def _ada_bwd(c_all_t, dmod_loc, name):
    D, B = c_all_t.shape
    L, _, S = dmod_loc.shape

    def body(c_ref, dm_ref, o_ref):
        c = c_ref[...]
        ca = c * _sigmoid(c)
        acc = None
        for bb in range(B):
            t = ca[:, bb:bb + 1] * dm_ref[bb:bb + 1, :]
            acc = t if acc is None else acc + t
        o_ref[...] = acc

    return pl.pallas_call(
        body, name=name, grid=(L,),
        in_specs=[pl.BlockSpec((D, B), lambda l: (0, 0)), pl.BlockSpec((None, B, S), lambda l: (l, 0, 0))],
        out_specs=pl.BlockSpec((None, D, S), lambda l: (l, 0, 0)),
        out_shape=jax.ShapeDtypeStruct((L, D, S), F32),
        compiler_params=_params(("parallel",)),
    )(c_all_t, dmod_loc)


def _sum_devices(parts, name):
    _, R, C = parts.shape
    tm = _tile(R, 256, 8)

    def body(p_ref, o_ref):
        acc = p_ref[0].astype(F32)
        for d in range(1, N_DEV):
            acc = acc + p_ref[d].astype(F32)
        o_ref[...] = acc

    return pl.pallas_call(
        body, name=name, grid=(R // tm,),
        in_specs=[pl.BlockSpec((N_DEV, tm, C), lambda i: (0, i, 0))],
        out_specs=pl.BlockSpec((tm, C), lambda i: (i, 0)),
        out_shape=jax.ShapeDtypeStruct((R, C), F32),
        compiler_params=_params(("parallel",)),
    )(parts)


def _adamw_math(w, g, m, v):
    m = ADAM_B1 * m + (1.0 - ADAM_B1) * g
    v = ADAM_B2 * v + (1.0 - ADAM_B2) * (g * g)
    m_hat = m / (1.0 - ADAM_B1 ** ADAM_STEP)
    v_hat = v / (1.0 - ADAM_B2 ** ADAM_STEP)
    delta = -ADAM_LR * (m_hat / (jnp.sqrt(v_hat) + ADAM_EPS) + ADAM_WD * w)
    return delta, m, v


def _adamw(w, g, m, v, name, summed):
    R, C = w.shape
    tm = _tile(R, 256, 16)
    n_parts = g.shape[0] if summed else 0

    def body(w_ref, g_ref, m_ref, v_ref, go_ref, d_ref, mo_ref, vo_ref):
        if summed:
            g = g_ref[0].astype(F32)
            for d in range(1, n_parts):
                g = g + g_ref[d].astype(F32)
        else:
            g = g_ref[...]
        delta, mn, vn = _adamw_math(w_ref[...], g, m_ref[...], v_ref[...])
        go_ref[...] = g
        d_ref[...] = delta
        mo_ref[...] = mn
        vo_ref[...] = vn

    big = pl.BlockSpec((tm, C), lambda i: (i, 0))
    gspec = pl.BlockSpec((n_parts, tm, C), lambda i: (0, i, 0)) if summed else big
    return pl.pallas_call(
        body, name=name, grid=(R // tm,),
        in_specs=[big, gspec, big, big],
        out_specs=[big, big, big, big],
        out_shape=tuple(jax.ShapeDtypeStruct((R, C), F32) for _ in range(4)),
        compiler_params=_params(("parallel",)),
    )(w, g, m, v)


def _adamw_summed(w, parts, m, v, name):
    L = len(parts)
    n_parts, R, C = parts[0].shape
    tm = _tile(R, 256, 16)
    nr = R // tm

    def body(*refs):
        w_ref, g_refs = refs[0], refs[1:1 + L]
        m_ref, v_ref, go_ref, d_ref, mo_ref, vo_ref = refs[1 + L:]
        for ll in range(L):
            @pl.when(pl.program_id(0) == ll)
            def _(ll=ll):
                g = g_refs[ll][0].astype(F32)
                for d in range(1, n_parts):
                    g = g + g_refs[ll][d].astype(F32)
                delta, mn, vn = _adamw_math(w_ref[...], g, m_ref[...], v_ref[...])
                go_ref[...] = g
                d_ref[...] = delta
                mo_ref[...] = mn
                vo_ref[...] = vn

    big = pl.BlockSpec((tm, C), lambda l, i: (l * nr + i, 0))
    gspecs = [pl.BlockSpec((n_parts, tm, C), functools.partial(lambda l, i, ll: (0, jnp.where(l == ll, i, 0), 0), ll=ll))
              for ll in range(L)]
    return pl.pallas_call(
        body, name=name, grid=(L, nr),
        in_specs=[big, *gspecs, big, big],
        out_specs=[big, big, big, big],
        out_shape=tuple(jax.ShapeDtypeStruct((L * R, C), F32) for _ in range(4)),
        compiler_params=_params(("arbitrary", "arbitrary")),
    )(w, *parts, m, v)


def _pack(arrs, D):
    L = arrs[0].shape[0]
    cols = []
    for a in arrs:
        f = a.reshape(L, -1)
        n = f.shape[1]
        cols.append(jnp.pad(f, ((0, 0), (0, -(-n // D) * D - n))))
    flat = jnp.concatenate(cols, axis=1)
    return flat.reshape(L, flat.shape[1] // D, D)


def _unpack(p, shapes, D):
    L = p.shape[0]
    out, r = [], 0
    for s in shapes:
        n = math.prod(s[1:])
        rows = -(-n // D)
        out.append(p[:, r:r + rows].reshape(L, rows * D)[:, :n].reshape(s))
        r += rows
    return out


def kernel(x, c, w_in, b_f, conv_w, conv_b, conv_ln_g, conv_ln_b, w_o, w_ffn_in, w_ffn_out, mix_pre_g, mix_post_g, ffn_pre_g, ffn_post_g, ada_w, ada_b, loss_target, m_w_in, m_b_f, m_conv_w, m_conv_b, m_conv_ln_g, m_conv_ln_b, m_w_o, m_w_ffn_in, m_w_ffn_out, m_mix_pre_g, m_mix_post_g, m_ffn_pre_g, m_ffn_post_g, m_ada_w, m_ada_b, v_w_in, v_b_f, v_conv_w, v_conv_b, v_conv_ln_g, v_conv_ln_b, v_w_o, v_w_ffn_in, v_w_ffn_out, v_mix_pre_g, v_mix_post_g, v_ffn_pre_g, v_ffn_post_g, v_ada_w, v_ada_b):
    L, D, s_in = w_in.shape
    T = x.shape[1]
    H = b_f.shape[1]
    A = D // 2
    C = D - A
    cs = conv_w.shape[2]
    F = w_ffn_out.shape[1] * N_DEV
    s_ff = w_ffn_in.shape[2]
    s_ada = ada_w.shape[2]
    R = 16
    me = _my_index()
    x0 = x[0]
    target = loss_target[0]
    tq = _tile(T, 512)
    nq = T // tq

    h_gmix = [_gather_start([w_in[l].astype(BF16), w_o[l].astype(BF16)], f"gather_mix_{l}") for l in range(L)]
    h_gffn = [_gather_start([w_ffn_in[l].astype(BF16), w_ffn_out[l].astype(BF16)], f"gather_ffn_{l}") for l in range(L)]
    cw_g = _exchange(conv_w, gather=True, name="gather_conv_w")
    conv_w_full = jnp.transpose(cw_g, (1, 2, 0, 3)).reshape(L, CONV_K, C)
    conv_w_pad = jnp.pad(conv_w_full, ((0, 0), (0, CONV_PAD - CONV_K), (0, 0)))
    b_f_col = jnp.pad(b_f, ((0, 0), (0, R - H)))[:, :, None]

    def by_cols(g):
        return jnp.transpose(g, (1, 0, 2)).reshape(g.shape[1], N_DEV * g.shape[2])

    def by_rows(g):
        return g.reshape(N_DEV * g.shape[1], g.shape[2])

    W_qkv, W_f_t, W_f_pad, W_c, W_o, W_ffn_in, W_ffn_out = ([None] * L for _ in range(7))

    c_all = _exchange(c, gather=True, name="gather_c").reshape(N_DEV, D)
    ada_b_loc = lax.dynamic_slice_in_dim(ada_b, me * s_ada, s_ada, axis=1)[:, None, :]
    mod_loc = _ada_fwd(c_all, ada_w, ada_b_loc, "ada_fwd")
    mod_g = _exchange(mod_loc, gather=True, name="gather_mod")
    mod = lax.dynamic_index_in_dim(mod_g, me, axis=2, keepdims=False)
    mod = jnp.transpose(mod, (1, 0, 2)).reshape(L, N_MOD, 1, D)

    saved = []
    xc = x0
    for l in range(L):
        sh1, sc1, g1, sh2, sc2, g2 = (mod[l, k] for k in range(N_MOD))
        gpre1, gpost1, gpre2, gpost2 = (p[l][None, :] for p in (mix_pre_g, mix_post_g, ffn_pre_g, ffn_post_g))
        if l == 0:
            h_gmix[0] = _gather_mid(h_gmix[0], c, "gather_mid_mix_0")
            h_gffn[0] = _gather_mid(h_gffn[0], c, "gather_mid_ffn_0")
        g_in, g_o = _gather_wait(h_gmix[l], xc, f"gather_wait_mix_{l}")
        W_in_l = by_cols(g_in)
        W_qkv[l], W_f, W_c[l], W_o[l] = W_in_l[:, :3 * A], W_in_l[:, 3 * A:3 * A + H], W_in_l[:, 3 * A + H:], by_rows(g_o)
        W_f_t[l] = jnp.pad(jnp.transpose(W_f), ((0, R - H), (0, 0)))
        W_f_pad[l] = jnp.pad(W_f, ((0, 0), (0, R - H)))
        h1 = _prenorm(xc, gpre1, sc1, sh1, f"prenorm1_{l}")
        qkv = _mm([(h1, W_qkv[l])], "nn", BF16, f"proj_qkv_{l}")
        cproj = _mm([(h1, W_c[l])], "nn", F32, f"proj_conv_{l}")
        fl, cum = _fgate_fwd(h1, W_f_t[l], b_f_col[l], f"fgate_{l}")
        cum_rows = cum[:H].reshape(H * nq, tq)
        o, lse = _attn_fwd(qkv, cum_rows, H, f"attn_{l}", tq)
        if l + 1 < L:
            h_gmix[l + 1] = _gather_mid(h_gmix[l + 1], o, f"gather_mid_mix_{l + 1}")
            h_gffn[l + 1] = _gather_mid(h_gffn[l + 1], o, f"gather_mid_ffn_{l + 1}")
        u3 = _conv_fwd(cproj, conv_w_pad[l], conv_b[l][None, :], conv_ln_g[l][None, :], conv_ln_b[l][None, :], f"conv_{l}")
        cat = jnp.concatenate([o, u3], axis=-1)
        y1, x_mid = _mm_postnorm(cat, W_o[l], xc, g1, gpost1, f"out_proj_{l}")
        g_fi, g_fo = _gather_wait(h_gffn[l], x_mid, f"gather_wait_ffn_{l}")
        W_ffn_in[l], W_ffn_out[l] = by_cols(g_fi), by_rows(g_fo)
        h2 = _prenorm(x_mid, gpre2, sc2, sh2, f"prenorm2_{l}")
        g, u, act = _ffn_in_fwd(h2, W_ffn_in[l], f"ffn_in_{l}")
        y2, x_out = _mm_postnorm(act, W_ffn_out[l], x_mid, g2, gpost2, f"ffn_out_{l}")
        saved.append((xc, h1, qkv, cproj, fl, cum_rows, o, lse, cat, y1, x_mid, h2, g, u, act, y2))
        xc = x_out

    loss_tile, dx = _loss_head(xc, target, "loss_head")
    loss = lax.psum(loss_tile[0, 0], ("x", "y", "c"))

    mc = lax.axis_index("c")

    def reduce_start(parts, name):
        keeps, sends = [], []
        for part in parts:
            part4 = part.reshape(4, 2, *part.shape[1:])
            keeps.append(lax.dynamic_index_in_dim(part4, mc, axis=1, keepdims=False))
            sends.append(lax.dynamic_index_in_dim(part4, 1 - mc, axis=1, keepdims=False))
        gots = _swap_sibling(sends, "swap_" + name)
        boths = [_pair_sum(k.reshape(-1, k.shape[-1]), t.reshape(-1, t.shape[-1]), f"pairsum{a}_{name}").reshape(k.shape)
                 for a, (k, t) in enumerate(zip(keeps, gots))]
        return _chips_start(boths, "scatter_" + name)

    small, h_ffn, h_mix = [None] * L, [None] * L, [None] * L
    for l in reversed(range(L)):
        xin, h1, qkv, cproj, fl, cum_rows, o, lse, cat, y1, x_mid, h2, g, u, act, y2 = saved[l]
        sh1, sc1, g1, sh2, sc2, g2 = (mod[l, k] for k in range(N_MOD))
        gpre1, gpost1, gpre2, gpost2 = (p[l][None, :] for p in (mix_pre_g, mix_post_g, ffn_pre_g, ffn_post_g))
        dy2, sm_post2 = _postnorm_bwd(dx, y2, g2, gpost2, f"postnorm2_bwd_{l}")
        dgate, dup = _ffn_act_bwd(dy2, W_ffn_out[l], g, u, f"ffn_act_bwd_{l}")
        dW_ffn_out = _mm([(act, dy2)], "tn", BF16, f"dw_ffn_out_{l}")
        dh2 = _mm([(dgate, W_ffn_in[l], 0), (dup, W_ffn_in[l], 1)], "nt", F32, f"dh2_{l}")
        dWg = _mm([(h2, dgate)], "tn", BF16, f"dw_ffn_gate_{l}")
        dWu = _mm([(h2, dup)], "tn", BF16, f"dw_ffn_up_{l}")
        dW_ffn_in = jnp.concatenate([dWg, dWu], axis=1)
        h_ffn[l], token = reduce_start([jnp.transpose(dW_ffn_in.reshape(D, N_DEV, s_ff), (1, 0, 2)),
                                        dW_ffn_out.reshape(N_DEV, F // N_DEV, D)], f"ffn_{l}")
        token, sc2 = lax.optimization_barrier((token, sc2))
        dx_mid, sm_pre2 = _prenorm_bwd(dh2, x_mid, dx, gpre2, sc2, f"prenorm2_bwd_{l}")
        dy1, sm_post1 = _postnorm_bwd(dx_mid, y1, g1, gpost1, f"postnorm1_bwd_{l}")
        dcat = _mm([(dy1, W_o[l])], "nt", F32, f"dcat_{l}")
        dW_o = _mm([(cat, dy1)], "tn", BF16, f"dw_o_{l}")
        dqkv, dcum_rows, dcum_q = _attn_bwd(qkv, dcat, o, lse, cum_rows, H, f"attn_bwd_{l}", tq)
        dcproj, dconv_w, sm_conv = _conv_bwd(cproj, dcat, conv_w_pad[l], conv_b[l][None, :], conv_ln_g[l][None, :],
                                             conv_ln_b[l][None, :], f"conv_bwd_{l}")
        dcum = dcum_rows.reshape(H, T) + jnp.transpose(dcum_q, (0, 2, 1)).reshape(H, T)
        dcum = jnp.pad(dcum, ((0, R - H), (0, 0)))
        dfl_t, dwf_t, dbf = _fgate_bwd(dcum, fl, h1, f"fgate_bwd_{l}")
        dfl = jnp.transpose(dfl_t).astype(BF16)
        dh1 = _mm([(dqkv[0], W_qkv[l], 0), (dqkv[1], W_qkv[l], 1), (dqkv[2], W_qkv[l], 2),
                   (dfl, W_f_pad[l]), (dcproj, W_c[l])], "nt", F32, f"dh1_{l}")
        dWq = [_mm([(h1, dqkv[k])], "tn", BF16, f"dw_qkv{k}_{l}") for k in range(3)]
        dWc = _mm([(h1, dcproj)], "tn", BF16, f"dw_conv_{l}")
        dWf = jnp.transpose(dwf_t[:H]).astype(BF16)
        dW_in = jnp.concatenate(dWq + [dWf, dWc], axis=1)
        h_mix[l], token = reduce_start([jnp.transpose(dW_in.reshape(D, N_DEV, s_in), (1, 0, 2)),
                                        dW_o.reshape(N_DEV, D // N_DEV, D)], f"mix_{l}")
        token, sc1 = lax.optimization_barrier((token, sc1))
        dx, sm_pre1 = _prenorm_bwd(dh1, xin, dx_mid, gpre1, sc1, f"prenorm1_bwd_{l}")
        dmod = jnp.stack([sm_pre1[0], sm_pre1[1], sm_post1[0], sm_pre2[0], sm_pre2[1], sm_post2[0]])
        small[l] = (dmod, sm_pre1[2], sm_post1[1], sm_pre2[2], sm_post2[1], sm_conv[0], sm_conv[1], sm_conv[2],
                    dbf[:H, 0], dconv_w[:CONV_K])
    grad_x = dx[None]

    small_names = 10
    small_l = [jnp.stack([small[l][k] for l in range(L)]) for k in range(small_names)]
    small_shapes = [a.shape for a in small_l]
    packed = _pack(small_l, D)
    rows = packed.shape[1]
    rows_pad = -(-L * rows // 8) * 8
    packed2 = jnp.pad(packed.reshape(L * rows, D), ((0, rows_pad - L * rows), (0, 0)))
    small_g = _exchange(packed2, gather=True, name="gather_small")
    small_sum = _sum_devices(small_g, "sum_small")[:L * rows].reshape(L, rows, D)
    (g_ada_b6, g_mix_pre, g_mix_post, g_ffn_pre, g_ffn_post, g_conv_b, g_ln_g, g_ln_b, g_b_f,
     g_conv_w_full) = _unpack(small_sum, small_shapes, D)
    g_ada_b = g_ada_b6.reshape(L, N_MOD * D)
    g_conv_w = lax.dynamic_slice_in_dim(g_conv_w_full, me * cs, cs, axis=2)
    dmod_all = small_g[:, :L * rows].reshape(N_DEV, L, rows, D)[:, :, :N_MOD].reshape(N_DEV, L, N_MOD * D)
    dmod_loc = jnp.transpose(lax.dynamic_slice_in_dim(dmod_all, me * s_ada, s_ada, axis=2), (1, 0, 2))
    g_ada_w = _ada_bwd(jnp.transpose(c_all), dmod_loc, "ada_bwd")

    got_ffn = [_chips_wait(h_ffn[l], dx, f"scatter_wait_ffn_{l}") for l in range(L)]
    got_mix = [_chips_wait(h_mix[l], dx, f"scatter_wait_mix_{l}") for l in range(L)]

    def step(w, m, v, parts, name):
        two_d = lambda t: t.reshape(-1, t.shape[-1])
        outs = _adamw_summed(two_d(w), parts, two_d(m), two_d(v), "adamw_" + name)
        return [t.reshape(w.shape) for t in outs]

    r_w_ffn_in = step(w_ffn_in, m_w_ffn_in, v_w_ffn_in, [got_ffn[l][0] for l in range(L)], "w_ffn_in")
    r_w_ffn_out = step(w_ffn_out, m_w_ffn_out, v_w_ffn_out, [got_ffn[l][1] for l in range(L)], "w_ffn_out")
    r_w_in = step(w_in, m_w_in, v_w_in, [got_mix[l][0] for l in range(L)], "w_in")
    r_w_o = step(w_o, m_w_o, v_w_o, [got_mix[l][1] for l in range(L)], "w_o")
    r_ada_w = [t.reshape(ada_w.shape) for t in _adamw(
        ada_w.reshape(L * D, s_ada), g_ada_w.reshape(L * D, s_ada), m_ada_w.reshape(L * D, s_ada),
        v_ada_w.reshape(L * D, s_ada), "adamw_ada_w", False)]

    sw = [b_f, conv_w, conv_b, conv_ln_g, conv_ln_b, mix_pre_g, mix_post_g, ffn_pre_g, ffn_post_g, ada_b]
    sg = [g_b_f, g_conv_w, g_conv_b, g_ln_g, g_ln_b, g_mix_pre, g_mix_post, g_ffn_pre, g_ffn_post, g_ada_b]
    sm = [m_b_f, m_conv_w, m_conv_b, m_conv_ln_g, m_conv_ln_b, m_mix_pre_g, m_mix_post_g, m_ffn_pre_g, m_ffn_post_g, m_ada_b]
    sv = [v_b_f, v_conv_w, v_conv_b, v_conv_ln_g, v_conv_ln_b, v_mix_pre_g, v_mix_post_g, v_ffn_pre_g, v_ffn_post_g, v_ada_b]
    shapes = [a.shape for a in sw]

    def flat(arrs):
        p = _pack(arrs, D)
        n = p.shape[0] * p.shape[1]
        return jnp.pad(p.reshape(n, D), ((0, -(-n // 8) * 8 - n), (0, 0))), p.shape

    pw, pshape = flat(sw)
    pg, pm, pv = flat(sg)[0], flat(sm)[0], flat(sv)[0]
    s_outs = _adamw(pw, pg, pm, pv, "adamw_small", False)
    n_small = pshape[0] * pshape[1]
    s_g, s_d, s_m, s_v = (_unpack(t[:n_small].reshape(pshape), shapes, D) for t in s_outs)

    big = {"w_in": r_w_in, "w_o": r_w_o, "w_ffn_in": r_w_ffn_in, "w_ffn_out": r_w_ffn_out, "ada_w": r_ada_w}
    order = ["w_in", "b_f", "conv_w", "conv_b", "conv_ln_g", "conv_ln_b", "w_o", "w_ffn_in", "w_ffn_out",
             "mix_pre_g", "mix_post_g", "ffn_pre_g", "ffn_post_g", "ada_w", "ada_b"]
    small_pos = {n: i for i, n in enumerate(["b_f", "conv_w", "conv_b", "conv_ln_g", "conv_ln_b", "mix_pre_g",
                                             "mix_post_g", "ffn_pre_g", "ffn_post_g", "ada_b"])}

    def pick(n, k):
        if n in big:
            return big[n][k]
        return (s_g, s_d, s_m, s_v)[k][small_pos[n]]

    return (loss, grad_x, *[pick(n, 0) for n in order], *[pick(n, 1) for n in order],
            *[pick(n, 2) for n in order], *[pick(n, 3) for n in order])
```

```python
import functools
import math

import jax
import jax.numpy as jnp
from jax import lax
from jax.experimental import pallas as pl
from jax.experimental.pallas import tpu as pltpu

F32 = jnp.float32
BF16 = jnp.bfloat16
MESH = pl.DeviceIdType.MESH
N_DEV = 8
EPS = 1e-6
CONV_K = 31
CONV_PAD = 32
CONV_CHUNK = 128
N_MOD = 6
NEG = -1e30
LANES = 128
VMEM_LIMIT = 56 * 2**20
ADAM_LR, ADAM_B1, ADAM_B2, ADAM_EPS, ADAM_WD, ADAM_STEP = 0.001, 0.9, 0.999, 1e-08, 0.01, 10

NN = (((1,), (0,)), ((), ()))
NT = (((1,), (1,)), ((), ()))
TN = (((0,), (0,)), ((), ()))


def _dot(a, b, dims):
    return lax.dot_general(a, b, dims, preferred_element_type=F32)


def _tile(n, pref, align=LANES):
    if n <= pref:
        return n
    t = (pref // align) * align
    while t >= align:
        if n % t == 0:
            return t
        t -= align
    return n


def _params(sem=None):
    return pltpu.CompilerParams(dimension_semantics=sem, vmem_limit_bytes=VMEM_LIMIT)


def _sigmoid(x):
    return 1.0 / (1.0 + jnp.exp(-x))


def _my_index():
    return 4 * lax.axis_index("x") + 2 * lax.axis_index("y") + lax.axis_index("c")


def _exchange(x, *, gather, name):
    blk = x.shape if gather else x.shape[1:]

    def body(x_ref, y_ref, send_sems, recv_sems, local_sem):
        mx, my, mc = lax.axis_index("x"), lax.axis_index("y"), lax.axis_index("c")
        me = 4 * mx + 2 * my + mc

        def src(p):
            return x_ref if gather else x_ref.at[p]

        mine = pltpu.make_async_copy(src(me), y_ref.at[me], local_sem)
        mine.start()
        copies = []
        for k in range(1, N_DEV):
            px = (1 - mx) if (k >> 2) & 1 else mx
            py = (1 - my) if (k >> 1) & 1 else my
            pc = (1 - mc) if k & 1 else mc
            cp = pltpu.make_async_remote_copy(
                src_ref=src(4 * px + 2 * py + pc), dst_ref=y_ref.at[me],
                send_sem=send_sems.at[k - 1], recv_sem=recv_sems.at[k - 1],
                device_id=(px, py, pc), device_id_type=MESH)
            cp.start()
            copies.append(cp)
        for cp in copies:
            cp.wait()
        mine.wait()

    return pl.pallas_call(
        body, name=name,
        out_shape=jax.ShapeDtypeStruct((N_DEV,) + tuple(blk), x.dtype),
        in_specs=[pl.BlockSpec(memory_space=pl.ANY)],
        out_specs=pl.BlockSpec(memory_space=pl.ANY),
        scratch_shapes=[pltpu.SemaphoreType.DMA((N_DEV - 1,)), pltpu.SemaphoreType.DMA((N_DEV - 1,)),
                        pltpu.SemaphoreType.DMA(())],
    )(x)


_HBM = pl.BlockSpec(memory_space=pl.ANY)


def _gather2(x, name):
    def body(x_ref, y_ref, send_sems, recv_sems, local_sem):
        mx, my, mc = lax.axis_index("x"), lax.axis_index("y"), lax.axis_index("c")
        me, sib = (mx, my, mc), (mx, my, 1 - mc)
        chips = [(1 - mx, my), (mx, 1 - my), (1 - mx, 1 - my)]

        def slot(px, py, pc):
            return y_ref.at[4 * px + 2 * py + pc]

        def copy(k, block, to, src=None):
            return pltpu.make_async_remote_copy(
                src_ref=slot(*block) if src is None else src, dst_ref=slot(*block),
                send_sem=send_sems.at[k], recv_sem=recv_sems.at[k], device_id=to, device_id_type=MESH)

        mine = pltpu.make_async_copy(x_ref, slot(*me), local_sem)
        mine.start()
        first = [copy(0, me, sib, src=x_ref)] + [copy(1 + j, me, (*chip, mc), src=x_ref) for j, chip in enumerate(chips)]
        for cp in first:
            cp.start()
        passed = [copy(4 + j, (*chip, mc), sib) for j, chip in enumerate(chips)]
        for j, chip in enumerate(chips):
            copy(1 + j, (*chip, mc), me).wait_recv()
            passed[j].start()
        copy(0, sib, me).wait_recv()
        for j, chip in enumerate(chips):
            copy(4 + j, (*chip, 1 - mc), me).wait_recv()
        for cp in first + passed:
            cp.wait_send()
        mine.wait()

    return pl.pallas_call(
        body, name=name,
        out_shape=jax.ShapeDtypeStruct((N_DEV,) + tuple(x.shape), x.dtype),
        in_specs=[_HBM], out_specs=_HBM,
        scratch_shapes=[pltpu.SemaphoreType.DMA((7,)), pltpu.SemaphoreType.DMA((7,)), pltpu.SemaphoreType.DMA(())],
    )(x)


def _swap_sibling(xs, name):
    n = len(xs)

    def body(*refs):
        send_sems, recv_sems = refs[2 * n], refs[2 * n + 1]
        sib = (lax.axis_index("x"), lax.axis_index("y"), 1 - lax.axis_index("c"))
        copies = [pltpu.make_async_remote_copy(src_ref=refs[a], dst_ref=refs[n + a], send_sem=send_sems.at[a],
                                               recv_sem=recv_sems.at[a], device_id=sib, device_id_type=MESH)
                  for a in range(n)]
        for cp in copies:
            cp.start()
        for cp in copies:
            cp.wait()

    return pl.pallas_call(
        body, name=name, out_shape=[jax.ShapeDtypeStruct(x.shape, x.dtype) for x in xs],
        in_specs=[_HBM] * n, out_specs=[_HBM] * n,
        scratch_shapes=[pltpu.SemaphoreType.DMA((n,)), pltpu.SemaphoreType.DMA((n,))],
    )(*xs)


def _exchange_chips(x, name):
    def body(x_ref, r_ref, send_sems, recv_sems, local_sem):
        mx, my, mc = lax.axis_index("x"), lax.axis_index("y"), lax.axis_index("c")
        here = 2 * mx + my
        mine = pltpu.make_async_copy(x_ref.at[here], r_ref.at[here], local_sem)
        mine.start()
        copies = []
        for j, (px, py) in enumerate([(1 - mx, my), (mx, 1 - my), (1 - mx, 1 - my)]):
            cp = pltpu.make_async_remote_copy(
                src_ref=x_ref.at[2 * px + py], dst_ref=r_ref.at[here], send_sem=send_sems.at[j], recv_sem=recv_sems.at[j],
                device_id=(px, py, mc), device_id_type=MESH)
            cp.start()
            copies.append(cp)
        for cp in copies:
            cp.wait()
        mine.wait()

    return pl.pallas_call(
        body, name=name, out_shape=jax.ShapeDtypeStruct(x.shape, x.dtype), in_specs=[_HBM], out_specs=_HBM,
        scratch_shapes=[pltpu.SemaphoreType.DMA((3,)), pltpu.SemaphoreType.DMA((3,)), pltpu.SemaphoreType.DMA(())],
    )(x)


_SEM = pl.BlockSpec(memory_space=pltpu.SEMAPHORE)
_HBM_SPEC = pl.BlockSpec(memory_space=pltpu.HBM)
_EFFECT = pltpu.CompilerParams(has_side_effects=pltpu.SideEffectType.DATAFLOW_SIDE_EFFECTING)


def _in_hbm(a):
    return pltpu.with_memory_space_constraint(a, pltpu.HBM)


def _chip_copies(x_refs, land_refs, send_sems, recv_sems, loc_sems):
    mx, my, mc = lax.axis_index("x"), lax.axis_index("y"), lax.axis_index("c")
    here = 2 * mx + my
    local, remote = [], []
    for a, (x_ref, land_ref) in enumerate(zip(x_refs, land_refs)):
        local.append(pltpu.make_async_copy(x_ref.at[here], land_ref.at[here], loc_sems.at[a]))
        for j, (px, py) in enumerate([(1 - mx, my), (mx, 1 - my), (1 - mx, 1 - my)]):
            remote.append(pltpu.make_async_remote_copy(
                src_ref=x_ref.at[2 * px + py], dst_ref=land_ref.at[here], send_sem=send_sems.at[3 * a + j],
                recv_sem=recv_sems.at[3 * a + j], device_id=(px, py, mc), device_id_type=MESH))
    return local, remote


def _chips_start(xs, name):
    n = len(xs)

    def body(*refs):
        local, remote = _chip_copies(refs[:n], refs[n:2 * n], *refs[2 * n:2 * n + 3])
        for cp in local + remote:
            cp.start()
        refs[-1][...] = jnp.zeros_like(refs[-1])

    lands = [lax.empty(x.shape, x.dtype) for x in xs]
    outs = pl.pallas_call(
        body, name=name,
        out_shape=(pltpu.SemaphoreType.DMA((3 * n,)), pltpu.SemaphoreType.DMA((3 * n,)), pltpu.SemaphoreType.DMA((n,)),
                   *[pltpu.HBM(x.shape, x.dtype) for x in xs], *[pltpu.HBM(x.shape, x.dtype) for x in xs],
                   jax.ShapeDtypeStruct((8, LANES), F32)),
        in_specs=[_HBM_SPEC] * (2 * n),
        out_specs=(_SEM, _SEM, _SEM, *[_HBM_SPEC] * (2 * n), pl.BlockSpec(memory_space=pltpu.VMEM)),
        input_output_aliases={i: 3 + i for i in range(2 * n)},
        compiler_params=_EFFECT,
    )(*[_in_hbm(x) for x in xs], *[_in_hbm(t) for t in lands])
    return (outs[0], outs[1], outs[2], list(outs[3:3 + n]), list(outs[3 + n:3 + 2 * n])), outs[-1]


def _chips_wait(handle, after, name):
    send_sems, recv_sems, loc_sems, x_thru, land_thru = handle
    n = len(x_thru)

    def body(*refs):
        local, remote = _chip_copies(refs[:n], refs[n:2 * n], *refs[2 * n:2 * n + 3])
        for cp in local:
            cp.wait()
        for cp in remote:
            cp.wait_send()
            cp.wait_recv()

    outs = pl.pallas_call(
        body, name=name,
        out_shape=[pltpu.HBM(x.shape, x.dtype) for x in x_thru + land_thru],
        in_specs=[_HBM_SPEC] * (2 * n) + [_SEM, _SEM, _SEM, _HBM],
        out_specs=[_HBM_SPEC] * (2 * n),
        input_output_aliases={i: i for i in range(2 * n)},
        compiler_params=_EFFECT,
    )(*x_thru, *land_thru, send_sems, recv_sems, loc_sems, after)
    return list(outs[n:])


def _gather_first_copies(x_refs, y_refs, send_sems, sib_recv, ici_recv, loc_sems):
    mx, my, mc = lax.axis_index("x"), lax.axis_index("y"), lax.axis_index("c")
    me = 4 * mx + 2 * my + mc
    local, remote = [], []
    for a, (x_ref, y_ref) in enumerate(zip(x_refs, y_refs)):
        local.append(pltpu.make_async_copy(x_ref, y_ref.at[me], loc_sems.at[a]))
        remote.append(pltpu.make_async_remote_copy(
            src_ref=x_ref, dst_ref=y_ref.at[me], send_sem=send_sems.at[4 * a], recv_sem=sib_recv.at[a],
            device_id=(mx, my, 1 - mc), device_id_type=MESH))
        for j, (px, py) in enumerate([(1 - mx, my), (mx, 1 - my), (1 - mx, 1 - my)]):
            remote.append(pltpu.make_async_remote_copy(
                src_ref=x_ref, dst_ref=y_ref.at[me], send_sem=send_sems.at[4 * a + 1 + j], recv_sem=ici_recv.at[3 * a + j],
                device_id=(px, py, mc), device_id_type=MESH))
    return local, remote


def _gather_forward_copies(y_refs, ici_recv, fwd_send, fwd_recv):
    mx, my, mc = lax.axis_index("x"), lax.axis_index("y"), lax.axis_index("c")
    pairs = []
    for a, y_ref in enumerate(y_refs):
        for j, (px, py) in enumerate([(1 - mx, my), (mx, 1 - my), (1 - mx, 1 - my)]):
            slot = y_ref.at[4 * px + 2 * py + mc]
            arrival = pltpu.make_async_remote_copy(
                src_ref=slot, dst_ref=slot, send_sem=fwd_send.at[3 * a + j], recv_sem=ici_recv.at[3 * a + j],
                device_id=(px, py, mc), device_id_type=MESH)
            onward = pltpu.make_async_remote_copy(
                src_ref=slot, dst_ref=slot, send_sem=fwd_send.at[3 * a + j], recv_sem=fwd_recv.at[3 * a + j],
                device_id=(mx, my, 1 - mc), device_id_type=MESH)
            pairs.append((arrival, onward))
    return pairs


def _gather_start(xs, name, after=()):
    n = len(xs)
    ni = 2 * n + len(after)

    def body(*refs):
        local, remote = _gather_first_copies(refs[:n], refs[n:2 * n], *refs[ni:ni + 4])
        for cp in local + remote:
            cp.start()
        refs[-1][...] = jnp.zeros_like(refs[-1])

    ys = [lax.empty((N_DEV,) + tuple(x.shape), x.dtype) for x in xs]
    dma = pltpu.SemaphoreType.DMA
    outs = pl.pallas_call(
        body, name=name,
        out_shape=(dma((4 * n,)), dma((n,)), dma((3 * n,)), dma((n,)),
                   *[pltpu.HBM(x.shape, x.dtype) for x in xs], *[pltpu.HBM(y.shape, y.dtype) for y in ys],
                   jax.ShapeDtypeStruct((8, LANES), F32)),
        in_specs=[_HBM_SPEC] * (2 * n) + [_HBM] * len(after),
        out_specs=(_SEM, _SEM, _SEM, _SEM, *[_HBM_SPEC] * (2 * n), pl.BlockSpec(memory_space=pltpu.VMEM)),
        input_output_aliases={i: 4 + i for i in range(2 * n)},
        compiler_params=_EFFECT,
    )(*[_in_hbm(x) for x in xs], *[_in_hbm(y) for y in ys], *after)
    return dict(send=outs[0], sib_recv=outs[1], ici_recv=outs[2], loc=outs[3], x=list(outs[4:4 + n]),
                y=list(outs[4 + n:4 + 2 * n]), token=outs[-1])


def _gather_mid(h, after, name):
    n = len(h["y"])

    def body(*refs):
        for arrival, onward in _gather_forward_copies(refs[:n], refs[n], refs[n + 2 + n], refs[n + 3 + n]):
            arrival.wait_recv()
            onward.start()

    dma = pltpu.SemaphoreType.DMA
    outs = pl.pallas_call(
        body, name=name,
        out_shape=(*[pltpu.HBM(y.shape, y.dtype) for y in h["y"]], dma((3 * n,)), dma((3 * n,))),
        in_specs=[_HBM_SPEC] * n + [_SEM, _HBM],
        out_specs=(*[_HBM_SPEC] * n, _SEM, _SEM),
        input_output_aliases={i: i for i in range(n)},
        compiler_params=_EFFECT,
    )(*h["y"], h["ici_recv"], after)
    return dict(h, y=list(outs[:n]), fwd_send=outs[n], fwd_recv=outs[n + 1])


def _gather_wait(h, after, name):
    n = len(h["y"])

    def body(*refs):
        x_refs, y_refs = refs[:n], refs[n:2 * n]
        send, sib_recv, loc, fwd_send, fwd_recv = refs[2 * n:2 * n + 5]
        local, remote = _gather_first_copies(x_refs, y_refs, send, sib_recv, fwd_recv, loc)
        for cp in local:
            cp.wait()
        for k, cp in enumerate(remote):
            cp.wait_send()
            if k % 4 == 0:
                cp.wait_recv()
        for _, onward in _gather_forward_copies(y_refs, fwd_recv, fwd_send, fwd_recv):
            onward.wait_send()
            onward.wait_recv()

    outs = pl.pallas_call(
        body, name=name,
        out_shape=[pltpu.HBM(t.shape, t.dtype) for t in h["x"] + h["y"]],
        in_specs=[_HBM_SPEC] * (2 * n) + [_SEM] * 5 + [_HBM],
        out_specs=[_HBM_SPEC] * (2 * n),
        input_output_aliases={i: i for i in range(2 * n)},
        compiler_params=_EFFECT,
    )(*h["x"], *h["y"], h["send"], h["sib_recv"], h["loc"], h["fwd_send"], h["fwd_recv"], after)
    return list(outs[n:])


def _pair_sum(a, b, name):
    R, C = a.shape
    tm = _tile(R, 512, 16)

    def body(a_ref, b_ref, o_ref):
        o_ref[...] = (a_ref[...].astype(F32) + b_ref[...].astype(F32)).astype(BF16)

    big = pl.BlockSpec((tm, C), lambda i: (i, 0))
    return pl.pallas_call(
        body, name=name, grid=(R // tm,), in_specs=[big, big], out_specs=big,
        out_shape=jax.ShapeDtypeStruct((R, C), BF16), compiler_params=_params(("parallel",)),
    )(a, b)


def _mm(pairs, mode, out_dtype, name, tm=512, tn=512):
    dims = {"nn": NN, "nt": NT, "tn": TN}[mode]
    a0, b0 = pairs[0][0], pairs[0][1]
    M = a0.shape[1] if mode == "tn" else a0.shape[0]
    N = b0.shape[0] if mode == "nt" else b0.shape[1]
    tm, tn = _tile(M, tm), _tile(N, tn)
    in_specs, args = [], []
    for pr in pairs:
        a, b = pr[0], pr[1]
        if mode == "tn":
            K = a.shape[0]
            in_specs.append(pl.BlockSpec((K, tm), lambda i, j: (0, i)))
            in_specs.append(pl.BlockSpec((K, tn), lambda i, j: (0, j)))
        elif mode == "nn":
            K = a.shape[1]
            in_specs.append(pl.BlockSpec((tm, K), lambda i, j: (i, 0)))
            in_specs.append(pl.BlockSpec((K, tn), lambda i, j: (0, j)))
        else:
            K = a.shape[1]
            cb = pr[2] if len(pr) > 2 else 0
            in_specs.append(pl.BlockSpec((tm, K), lambda i, j: (i, 0)))
            in_specs.append(pl.BlockSpec((tn, K), functools.partial(lambda i, j, cb: (j, cb), cb=cb)))
        args += [a, b]
    n_pairs = len(pairs)

    def body(*refs):
        o_ref = refs[-1]
        acc = None
        for k in range(n_pairs):
            d = _dot(refs[2 * k][...], refs[2 * k + 1][...], dims)
            acc = d if acc is None else acc + d
        o_ref[...] = acc.astype(o_ref.dtype)

    return pl.pallas_call(
        body, name=name, grid=(M // tm, N // tn), in_specs=in_specs,
        out_specs=pl.BlockSpec((tm, tn), lambda i, j: (i, j)),
        out_shape=jax.ShapeDtypeStruct((M, N), out_dtype),
        compiler_params=_params(("parallel", "arbitrary")),
    )(*args)


def _prenorm(x, g, sc, sh, name, first=()):
    T, D = x.shape
    tm = _tile(T, 512, 8)

    def body(x_ref, g_ref, sc_ref, sh_ref, *rest):
        h_ref = rest[-1]
        xv = x_ref[...]
        r = lax.rsqrt(jnp.mean(xv * xv, axis=-1, keepdims=True) + EPS)
        h_ref[...] = (((xv * r) * g_ref[...]) * (1.0 + sc_ref[...]) + sh_ref[...]).astype(BF16)

    row = pl.BlockSpec((1, D), lambda i: (0, 0))
    return pl.pallas_call(
        body, name=name, grid=(T // tm,),
        in_specs=[pl.BlockSpec((tm, D), lambda i: (i, 0)), row, row, row] + [_HBM] * len(first),
        out_specs=pl.BlockSpec((tm, D), lambda i: (i, 0)),
        out_shape=jax.ShapeDtypeStruct((T, D), BF16),
        compiler_params=_params(("parallel",)),
    )(x, g, sc, sh, *first)


def _prenorm_bwd(dh, x, dres, g, sc, name, first=()):
    T, D = x.shape
    tm = _tile(T, 256, 8)

    def body(dh_ref, x_ref, dres_ref, g_ref, sc_ref, *rest):
        dx_ref, sm_ref = rest[-2:]
        @pl.when(pl.program_id(0) == 0)
        def _():
            sm_ref[...] = jnp.zeros_like(sm_ref)

        xv, dhv = x_ref[...], dh_ref[...]
        r = lax.rsqrt(jnp.mean(xv * xv, axis=-1, keepdims=True) + EPS)
        xh = xv * r
        one_sc = 1.0 + sc_ref[...]
        sm_ref[0:1, :] += jnp.sum(dhv, axis=0, keepdims=True)
        sm_ref[1:2, :] += jnp.sum(dhv * (xh * g_ref[...]), axis=0, keepdims=True)
        sm_ref[2:3, :] += jnp.sum(dhv * one_sc * xh, axis=0, keepdims=True)
        dxh = dhv * one_sc * g_ref[...]
        dx_ref[...] = dres_ref[...] + r * (dxh - xh * jnp.mean(dxh * xh, axis=-1, keepdims=True))

    row = pl.BlockSpec((1, D), lambda i: (0, 0))
    big = pl.BlockSpec((tm, D), lambda i: (i, 0))
    return pl.pallas_call(
        body, name=name, grid=(T // tm,),
        in_specs=[big, big, big, row, row] + [_HBM] * len(first),
        out_specs=[big, pl.BlockSpec((8, D), lambda i: (0, 0))],
        out_shape=(jax.ShapeDtypeStruct((T, D), F32), jax.ShapeDtypeStruct((8, D), F32)),
        compiler_params=_params(("arbitrary",)),
    )(dh, x, dres, g, sc, *first)


def _mm_postnorm(a, w, x, gate, gpost, name):
    T, K = a.shape
    D = w.shape[1]
    tm = _tile(T, 256, 8)

    def body(a_ref, w_ref, x_ref, gate_ref, gp_ref, y_ref, xn_ref):
        y = _dot(a_ref[...], w_ref[...], NN)
        r = lax.rsqrt(jnp.mean(y * y, axis=-1, keepdims=True) + EPS)
        y_ref[...] = y
        xn_ref[...] = x_ref[...] + gate_ref[...] * ((y * r) * gp_ref[...])

    row = pl.BlockSpec((1, D), lambda i: (0, 0))
    big = pl.BlockSpec((tm, D), lambda i: (i, 0))
    return pl.pallas_call(
        body, name=name, grid=(T // tm,),
        in_specs=[pl.BlockSpec((tm, K), lambda i: (i, 0)), pl.BlockSpec((K, D), lambda i: (0, 0)), big, row, row],
        out_specs=[big, big],
        out_shape=(jax.ShapeDtypeStruct((T, D), F32), jax.ShapeDtypeStruct((T, D), F32)),
        compiler_params=_params(("parallel",)),
    )(a, w, x, gate, gpost)


def _postnorm_bwd(dx, y, gate, gpost, name):
    T, D = y.shape
    tm = _tile(T, 256, 8)

    def body(dx_ref, y_ref, gate_ref, gp_ref, dy_ref, sm_ref):
        @pl.when(pl.program_id(0) == 0)
        def _():
            sm_ref[...] = jnp.zeros_like(sm_ref)

        yv, dxv = y_ref[...], dx_ref[...]
        r = lax.rsqrt(jnp.mean(yv * yv, axis=-1, keepdims=True) + EPS)
        yh = yv * r
        dn = dxv * gate_ref[...]
        sm_ref[0:1, :] += jnp.sum(dxv * (yh * gp_ref[...]), axis=0, keepdims=True)
        sm_ref[1:2, :] += jnp.sum(dn * yh, axis=0, keepdims=True)
        dyh = dn * gp_ref[...]
        dy_ref[...] = (r * (dyh - yh * jnp.mean(dyh * yh, axis=-1, keepdims=True))).astype(BF16)

    row = pl.BlockSpec((1, D), lambda i: (0, 0))
    big = pl.BlockSpec((tm, D), lambda i: (i, 0))
    return pl.pallas_call(
        body, name=name, grid=(T // tm,),
        in_specs=[big, big, row, row],
        out_specs=[big, pl.BlockSpec((8, D), lambda i: (0, 0))],
        out_shape=(jax.ShapeDtypeStruct((T, D), BF16), jax.ShapeDtypeStruct((8, D), F32)),
        compiler_params=_params(("arbitrary",)),
    )(dx, y, gate, gpost)


def _ffn_in_fwd(h, w, name):
    T, D = h.shape
    F = w.shape[1] // 2
    tm, tn = _tile(T, 1024), _tile(F, 256)
    nj = F // tn

    def body(h_ref, wg_ref, wu_ref, g_ref, u_ref, act_ref):
        hv = h_ref[...]
        g = _dot(hv, wg_ref[...], NN)
        u = _dot(hv, wu_ref[...], NN)
        g_ref[...] = g.astype(BF16)
        u_ref[...] = u.astype(BF16)
        act_ref[...] = ((g * _sigmoid(g)) * u).astype(BF16)

    out = pl.BlockSpec((tm, tn), lambda i, j: (i, j))
    return pl.pallas_call(
        body, name=name, grid=(T // tm, nj),
        in_specs=[pl.BlockSpec((tm, D), lambda i, j: (i, 0)),
                  pl.BlockSpec((D, tn), lambda i, j: (0, j)),
                  pl.BlockSpec((D, tn), lambda i, j: (0, j + nj))],
        out_specs=[out, out, out],
        out_shape=tuple(jax.ShapeDtypeStruct((T, F), BF16) for _ in range(3)),
        compiler_params=_params(("parallel", "arbitrary")),
    )(h, w, w)


def _ffn_act_bwd(dy, w_out, g, u, name):
    T, D = dy.shape
    F = w_out.shape[0]
    tm, tn = _tile(T, 1024), _tile(F, 256)

    def body(dy_ref, w_ref, g_ref, u_ref, dg_ref, du_ref):
        dact = _dot(dy_ref[...], w_ref[...], NT)
        gv, uv = g_ref[...].astype(F32), u_ref[...].astype(F32)
        sg = _sigmoid(gv)
        dg_ref[...] = (dact * uv * (sg * (1.0 + gv * (1.0 - sg)))).astype(BF16)
        du_ref[...] = (dact * (gv * sg)).astype(BF16)

    tile = pl.BlockSpec((tm, tn), lambda i, j: (i, j))
    return pl.pallas_call(
        body, name=name, grid=(T // tm, F // tn),
        in_specs=[pl.BlockSpec((tm, D), lambda i, j: (i, 0)), pl.BlockSpec((tn, D), lambda i, j: (j, 0)), tile, tile],
        out_specs=[tile, tile],
        out_shape=(jax.ShapeDtypeStruct((T, F), BF16), jax.ShapeDtypeStruct((T, F), BF16)),
        compiler_params=_params(("parallel", "arbitrary")),
    )(dy, w_out, g, u)


def _lane_scan(v, reverse):
    T = v.shape[-1]
    lane = lax.broadcasted_iota(jnp.int32, v.shape, 1)
    d = 1
    while d < T:
        if reverse:
            v = v + jnp.where(lane < T - d, pltpu.roll(v, T - d, axis=1), 0.0)
        else:
            v = v + jnp.where(lane >= d, pltpu.roll(v, d, axis=1), 0.0)
        d *= 2
    return v


def _fgate_fwd(h, wf_t, bf, name):
    T, D = h.shape
    R = wf_t.shape[0]

    def body(h_ref, w_ref, b_ref, fl_ref, cum_ref):
        fl = _dot(w_ref[...], h_ref[...], NT) + b_ref[...]
        fl_ref[...] = fl
        logf = jnp.minimum(fl, 0.0) - jnp.log(1.0 + jnp.exp(-jnp.abs(fl)))
        cum_ref[...] = _lane_scan(logf, reverse=False)

    return pl.pallas_call(
        body, name=name,
        out_shape=(jax.ShapeDtypeStruct((R, T), F32), jax.ShapeDtypeStruct((R, T), F32)),
        compiler_params=_params(),
    )(h, wf_t, bf)


def _fgate_bwd(dcum, fl, h, name):
    R, T = fl.shape
    D = h.shape[1]

    def body(dc_ref, fl_ref, h_ref, dfl_ref, dw_ref, db_ref):
        dlogf = _lane_scan(dc_ref[...], reverse=True)
        dfl = dlogf * _sigmoid(-fl_ref[...])
        dfl_ref[...] = dfl
        dw_ref[...] = _dot(dfl.astype(BF16), h_ref[...], NN)
        db_ref[...] = jnp.broadcast_to(jnp.sum(dfl, axis=-1, keepdims=True), (R, LANES))

    return pl.pallas_call(
        body, name=name,
        out_shape=(jax.ShapeDtypeStruct((R, T), F32), jax.ShapeDtypeStruct((R, D), F32),
                   jax.ShapeDtypeStruct((R, LANES), F32)),
        compiler_params=_params(),
    )(dcum, fl, h)


def _head_masks(hpb, dh, rows):
    lane = lax.broadcasted_iota(jnp.int32, (rows, LANES), 1)
    return [(lane >= h * dh) & (lane < (h + 1) * dh) for h in range(hpb)]


def _stack_heads(v, masks):
    return jnp.concatenate([jnp.where(mk, v, jnp.zeros_like(v)) for mk in masks], axis=0)


def _heads_to_lanes(col, masks, tq):
    out = jnp.broadcast_to(col[0:tq], (tq, LANES))
    for h in range(1, len(masks)):
        out = jnp.where(masks[h], col[h * tq:(h + 1) * tq], out)
    return out


def _causal_stack(hpb, tq):
    r = lax.broadcasted_iota(jnp.int32, (tq, tq), 0)
    c = lax.broadcasted_iota(jnp.int32, (tq, tq), 1)
    return jnp.concatenate([c] * hpb, axis=0) <= jnp.concatenate([r] * hpb, axis=0)


def _attn_fwd(qkv, cum_rows, n_heads, name, tq):
    T = qkv.shape[0]
    A = qkv.shape[1] // 3
    dh = A // n_heads
    hpb = LANES // dh
    nb = A // LANES
    nq = T // tq
    scale = dh ** -0.5

    def body(q_ref, k_ref, v_ref, c_ref, o_ref, l_ref, vbd):
        hp, i = pl.program_id(0), pl.program_id(1)
        masks = _head_masks(hpb, dh, tq)

        @pl.when(i == 0)
        def _():
            def fill(j, _):
                vbd[j] = _stack_heads(v_ref[pl.ds(pl.multiple_of(j * tq, tq), tq), :], masks)
                return 0

            lax.fori_loop(0, nq, fill, 0)

        qs = _stack_heads(q_ref[...], masks)
        crow0 = hp * hpb * nq

        def tile(j, carry, diag):
            m, l, acc = carry
            kt = k_ref[pl.ds(pl.multiple_of(j * tq, tq), tq), :]
            bias = jnp.concatenate(
                [jnp.broadcast_to(c_ref[pl.ds(crow0 + h * nq + j, 1), :], (tq, tq)) for h in range(hpb)], axis=0)
            s = _dot(qs, kt, NT) * scale - bias
            if diag:
                s = jnp.where(_causal_stack(hpb, tq), s, NEG)
            m_new = jnp.maximum(m, jnp.max(s, axis=-1, keepdims=True))
            p = jnp.exp(s - m_new)
            alpha = jnp.exp(m - m_new)
            l = alpha * l + jnp.sum(p, axis=-1, keepdims=True)
            pcat = jnp.concatenate([p[h * tq:(h + 1) * tq] for h in range(hpb)], axis=1).astype(BF16)
            acc = _heads_to_lanes(alpha, masks, tq) * acc + _dot(pcat, vbd[j], NN)
            return m_new, l, acc

        init = (jnp.full((hpb * tq, 1), NEG, F32), jnp.zeros((hpb * tq, 1), F32), jnp.zeros((tq, LANES), F32))
        carry = lax.fori_loop(0, i, lambda j, c: tile(j, c, False), init)
        m, l, acc = tile(i, carry, True)
        o_ref[...] = (acc / _heads_to_lanes(l, masks, tq)).astype(BF16)
        lse = m + jnp.log(l)
        for h in range(hpb):
            l_ref[:, h:h + 1] = lse[h * tq:(h + 1) * tq]

    return pl.pallas_call(
        body, name=name, grid=(nb, nq),
        in_specs=[pl.BlockSpec((tq, LANES), lambda h, i: (i, h)),
                  pl.BlockSpec((T, LANES), lambda h, i: (0, nb + h)),
                  pl.BlockSpec((T, LANES), lambda h, i: (0, 2 * nb + h)),
                  pl.BlockSpec(cum_rows.shape, lambda h, i: (0, 0))],
        out_specs=[pl.BlockSpec((tq, LANES), lambda h, i: (i, h)),
                   pl.BlockSpec((None, tq, hpb), lambda h, i: (h, i, 0))],
        out_shape=(jax.ShapeDtypeStruct((T, A), BF16), jax.ShapeDtypeStruct((nb, T, hpb), F32)),
        scratch_shapes=[pltpu.VMEM((nq, hpb * tq, LANES), BF16)],
        compiler_params=_params(("arbitrary", "arbitrary")),
    )(qkv, qkv, qkv, cum_rows)


def _attn_bwd(qkv, dcat, o, lse, cum_rows, n_heads, name, tq):
    T = qkv.shape[0]
    A = qkv.shape[1] // 3
    dh = A // n_heads
    hpb = LANES // dh
    nb = A // LANES
    nq = T // tq
    scale = dh ** -0.5

    def body(q_ref, k_ref, v_ref, do_ref, o_ref, l_ref, c_ref, dqkv_ref, dc_ref, dr_ref,
             dq_acc, delta, drow, qs_scr, dos_scr, kbd_scr):
        hp = pl.program_id(0)
        masks = _head_masks(hpb, dh, tq)
        crow0 = hp * hpb * nq

        def prologue(i, _):
            rs = pl.ds(pl.multiple_of(i * tq, tq), tq)
            do = do_ref[rs, :]
            prod = do * o_ref[rs, :].astype(F32)
            for h in range(hpb):
                delta[rs, h:h + 1] = jnp.sum(jnp.where(masks[h], prod, 0.0), axis=-1, keepdims=True)
            qs_scr[i] = _stack_heads(q_ref[rs, :], masks)
            dos_scr[i] = _stack_heads(do, masks).astype(BF16)
            kbd_scr[i] = _stack_heads(k_ref[rs, :], masks)
            dq_acc[rs, :] = jnp.zeros((tq, LANES), F32)
            drow[rs, :] = jnp.zeros((tq, hpb), F32)
            return 0

        lax.fori_loop(0, nq, prologue, 0)

        def kv_step(j, _):
            ks = pl.ds(pl.multiple_of(j * tq, tq), tq)
            kt, vt = k_ref[ks, :], v_ref[ks, :]
            kbd = kbd_scr[j]
            bias = jnp.concatenate(
                [jnp.broadcast_to(c_ref[pl.ds(crow0 + h * nq + j, 1), :], (tq, tq)) for h in range(hpb)], axis=0)

            def q_step(i, carry, diag):
                dk, dv, dcs = carry
                rs = pl.ds(pl.multiple_of(i * tq, tq), tq)
                qs, dos = qs_scr[i], dos_scr[i]
                s = _dot(qs, kt, NT) * scale - bias
                if diag:
                    s = jnp.where(_causal_stack(hpb, tq), s, NEG)
                lse = jnp.concatenate([l_ref[rs, h:h + 1] for h in range(hpb)], axis=0)
                p = jnp.exp(s - lse)
                dv = dv + _dot(p.astype(BF16), dos, TN)
                dp = _dot(dos, vt, NT)
                ds = p * (dp - jnp.concatenate([delta[rs, h:h + 1] for h in range(hpb)], axis=0))
                dcs = tuple(dcs[h] - jnp.sum(ds[h * tq:(h + 1) * tq], axis=0, keepdims=True) for h in range(hpb))
                rsum = jnp.sum(ds, axis=-1, keepdims=True)
                for h in range(hpb):
                    drow[rs, h:h + 1] += rsum[h * tq:(h + 1) * tq]
                dsb = (ds * scale).astype(BF16)
                dk = dk + _dot(dsb, qs, TN)
                dscat = jnp.concatenate([dsb[h * tq:(h + 1) * tq] for h in range(hpb)], axis=1)
                dq_acc[rs, :] += _dot(dscat, kbd, NN)
                return dk, dv, dcs

            init = (jnp.zeros((tq, LANES), F32), jnp.zeros((tq, LANES), F32),
                    tuple(jnp.zeros((1, tq), F32) for _ in range(hpb)))
            carry = q_step(j, init, True)
            dk, dv, dcs = lax.fori_loop(j + 1, nq, lambda i, c: q_step(i, c, False), carry)
            dqkv_ref[1, ks, :] = dk.astype(BF16)
            dqkv_ref[2, ks, :] = dv.astype(BF16)
            for h in range(hpb):
                dc_ref[pl.ds(crow0 + h * nq + j, 1), :] = dcs[h]
            return 0

        lax.fori_loop(0, nq, kv_step, 0)
        dqkv_ref[0] = dq_acc[...].astype(BF16)
        dr_ref[...] = drow[...]

    col = lambda off: pl.BlockSpec((T, LANES), functools.partial(lambda h, off: (0, off + h), off=off))
    return pl.pallas_call(
        body, name=name, grid=(nb,),
        in_specs=[col(0), col(nb), col(2 * nb), col(0), col(0),
                  pl.BlockSpec((None, T, hpb), lambda h: (h, 0, 0)),
                  pl.BlockSpec(cum_rows.shape, lambda h: (0, 0))],
        out_specs=[pl.BlockSpec((3, T, LANES), lambda h: (0, 0, h)),
                   pl.BlockSpec(cum_rows.shape, lambda h: (0, 0)),
                   pl.BlockSpec((None, T, hpb), lambda h: (h, 0, 0))],
        out_shape=(jax.ShapeDtypeStruct((3, T, A), BF16), jax.ShapeDtypeStruct(cum_rows.shape, F32),
                   jax.ShapeDtypeStruct((nb, T, hpb), F32)),
        scratch_shapes=[pltpu.VMEM((T, LANES), F32), pltpu.VMEM((T, hpb), F32), pltpu.VMEM((T, hpb), F32),
                        pltpu.VMEM((nq, hpb * tq, LANES), BF16), pltpu.VMEM((nq, hpb * tq, LANES), BF16),
                        pltpu.VMEM((nq, hpb * tq, LANES), BF16)],
        compiler_params=_params(("arbitrary",)),
    )(qkv, qkv, qkv, dcat, o, lse, cum_rows)


def _glu_into(upad, cv_ref, cg_ref, T):
    upad[0:CONV_PAD, :] = jnp.zeros((CONV_PAD, upad.shape[1]), F32)

    def fill(c, _):
        rs = pl.ds(pl.multiple_of(c * CONV_CHUNK, CONV_CHUNK), CONV_CHUNK)
        upad[pl.ds(pl.multiple_of(CONV_PAD + c * CONV_CHUNK, 8), CONV_CHUNK), :] = cv_ref[rs, :] * _sigmoid(cg_ref[rs, :])
        return 0

    lax.fori_loop(0, T // CONV_CHUNK, fill, 0)


def _conv_taps(win, w_ref, first, step):
    acc = None
    for k in range(CONV_K):
        o = first + step * k
        t = w_ref[k:k + 1, :] * win[o:o + CONV_CHUNK, :]
        acc = t if acc is None else acc + t
    return acc


def _conv_fwd(cproj, w, b, lg, lb, name):
    T = cproj.shape[0]
    C = cproj.shape[1] // 2
    off = CONV_PAD - (CONV_K - 1)

    def body(cv_ref, cg_ref, w_ref, b_ref, lg_ref, lb_ref, out_ref, upad, win):
        _glu_into(upad, cv_ref, cg_ref, T)

        def chunk(c, _):
            r0 = pl.multiple_of(c * CONV_CHUNK, CONV_CHUNK)
            win[...] = upad[pl.ds(r0, CONV_CHUNK + CONV_PAD), :]
            u1 = _conv_taps(win, w_ref, off, 1) + b_ref[...]
            mu = jnp.mean(u1, axis=-1, keepdims=True)
            var = jnp.mean(jnp.square(u1 - mu), axis=-1, keepdims=True)
            u2 = ((u1 - mu) * lax.rsqrt(var + EPS)) * lg_ref[...] + lb_ref[...]
            out_ref[pl.ds(r0, CONV_CHUNK), :] = (u2 * _sigmoid(u2)).astype(BF16)
            return 0

        lax.fori_loop(0, T // CONV_CHUNK, chunk, 0)

    row = pl.BlockSpec((1, C), lambda i: (0, 0))
    return pl.pallas_call(
        body, name=name, grid=(1,),
        in_specs=[pl.BlockSpec((T, C), lambda i: (0, 0)), pl.BlockSpec((T, C), lambda i: (0, 1)),
                  pl.BlockSpec(w.shape, lambda i: (0, 0)), row, row, row],
        out_specs=pl.BlockSpec((T, C), lambda i: (0, 0)),
        out_shape=jax.ShapeDtypeStruct((T, C), BF16),
        scratch_shapes=[pltpu.VMEM((T + CONV_PAD, C), F32), pltpu.VMEM((CONV_CHUNK + CONV_PAD, C), F32)],
        compiler_params=_params(("arbitrary",)),
    )(cproj, cproj, w, b, lg, lb)


def _conv_bwd(cproj, dcat, w, b, lg, lb, name):
    T = cproj.shape[0]
    C = cproj.shape[1] // 2
    off = CONV_PAD - (CONV_K - 1)
    n_chunks = T // CONV_CHUNK

    def fold(v):
        return jnp.sum(v.reshape(CONV_CHUNK // 8, 8, C), axis=0)

    def body(cv_ref, cg_ref, du_ref, w_ref, b_ref, lg_ref, lb_ref, dc_ref, dw_ref, sm_ref, upad, dpad, win, dwacc, smacc):
        _glu_into(upad, cv_ref, cg_ref, T)
        dpad[pl.ds(T, CONV_PAD), :] = jnp.zeros((CONV_PAD, C), F32)
        dwacc[...] = jnp.zeros_like(dwacc)
        smacc[...] = jnp.zeros_like(smacc)

        def chunk_a(c, _):
            r0 = pl.multiple_of(c * CONV_CHUNK, CONV_CHUNK)
            win[...] = upad[pl.ds(r0, CONV_CHUNK + CONV_PAD), :]
            u1 = _conv_taps(win, w_ref, off, 1) + b_ref[...]
            mu = jnp.mean(u1, axis=-1, keepdims=True)
            var = jnp.mean(jnp.square(u1 - mu), axis=-1, keepdims=True)
            rstd = lax.rsqrt(var + EPS)
            u1h = (u1 - mu) * rstd
            u2 = u1h * lg_ref[...] + lb_ref[...]
            sg = _sigmoid(u2)
            du2 = du_ref[pl.ds(r0, CONV_CHUNK), :] * (sg * (1.0 + u2 * (1.0 - sg)))
            smacc[8:16, :] += fold(du2 * u1h)
            smacc[16:24, :] += fold(du2)
            du1h = du2 * lg_ref[...]
            du1 = rstd * (du1h - jnp.mean(du1h, axis=-1, keepdims=True)
                          - u1h * jnp.mean(du1h * u1h, axis=-1, keepdims=True))
            smacc[0:8, :] += fold(du1)
            dpad[pl.ds(r0, CONV_CHUNK), :] = du1
            for k in range(CONV_K):
                dwacc[8 * k:8 * k + 8, :] += fold(du1 * win[off + k:off + k + CONV_CHUNK, :])
            return 0

        lax.fori_loop(0, n_chunks, chunk_a, 0)

        def chunk_b(c, _):
            r0 = pl.multiple_of(c * CONV_CHUNK, CONV_CHUNK)
            rs = pl.ds(r0, CONV_CHUNK)
            win[...] = dpad[pl.ds(r0, CONV_CHUNK + CONV_PAD), :]
            du0 = _conv_taps(win, w_ref, CONV_K - 1, -1)
            cv, sg = cv_ref[rs, :], _sigmoid(cg_ref[rs, :])
            dc_ref[rs, 0:C] = (du0 * sg).astype(BF16)
            dc_ref[rs, C:2 * C] = (du0 * cv * (sg * (1.0 - sg))).astype(BF16)
            return 0

        lax.fori_loop(0, n_chunks, chunk_b, 0)
        for k in range(CONV_K):
            dw_ref[k:k + 1, :] = jnp.sum(dwacc[8 * k:8 * k + 8, :], axis=0, keepdims=True)
        dw_ref[CONV_K:CONV_PAD, :] = jnp.zeros((CONV_PAD - CONV_K, C), F32)
        for r in range(3):
            sm_ref[r:r + 1, :] = jnp.sum(smacc[8 * r:8 * r + 8, :], axis=0, keepdims=True)
        sm_ref[3:8, :] = jnp.zeros((5, C), F32)

    row = pl.BlockSpec((1, C), lambda i: (0, 0))
    return pl.pallas_call(
        body, name=name, grid=(1,),
        in_specs=[pl.BlockSpec((T, C), lambda i: (0, 0)), pl.BlockSpec((T, C), lambda i: (0, 1)),
                  pl.BlockSpec((T, C), lambda i: (0, 1)),
                  pl.BlockSpec(w.shape, lambda i: (0, 0)), row, row, row],
        out_specs=[pl.BlockSpec((T, 2 * C), lambda i: (0, 0)), pl.BlockSpec((CONV_PAD, C), lambda i: (0, 0)),
                   pl.BlockSpec((8, C), lambda i: (0, 0))],
        out_shape=(jax.ShapeDtypeStruct((T, 2 * C), BF16), jax.ShapeDtypeStruct((CONV_PAD, C), F32),
                   jax.ShapeDtypeStruct((8, C), F32)),
        scratch_shapes=[pltpu.VMEM((T + CONV_PAD, C), F32), pltpu.VMEM((T + CONV_PAD, C), F32),
                        pltpu.VMEM((CONV_CHUNK + CONV_PAD, C), F32), pltpu.VMEM((8 * CONV_PAD, C), F32),
                        pltpu.VMEM((24, C), F32)],
        compiler_params=_params(("arbitrary",)),
    )(cproj, cproj, dcat, w, b, lg, lb)


def _loss_head(x, target, name):
    T, D = x.shape
    tm = _tile(T, 512, 8)

    def body(x_ref, t_ref, loss_ref, dx_ref):
        @pl.when(pl.program_id(0) == 0)
        def _():
            loss_ref[...] = jnp.zeros_like(loss_ref)

        err = x_ref[...] - t_ref[...]
        part = jnp.sum(jnp.mean(err * err, axis=-1, keepdims=True), axis=0, keepdims=True)
        loss_ref[...] += jnp.broadcast_to(0.5 * part, loss_ref.shape)
        dx_ref[...] = err * (1.0 / D)

    big = pl.BlockSpec((tm, D), lambda i: (i, 0))
    return pl.pallas_call(
        body, name=name, grid=(T // tm,),
        in_specs=[big, big],
        out_specs=[pl.BlockSpec((8, LANES), lambda i: (0, 0)), big],
        out_shape=(jax.ShapeDtypeStruct((8, LANES), F32), jax.ShapeDtypeStruct((T, D), F32)),
        compiler_params=_params(("arbitrary",)),
    )(x, target)


def _ada_fwd(c_all, ada_w, ada_b_loc, name):
    L, D, S = ada_w.shape
    B = c_all.shape[0]

    def body(c_ref, w_ref, b_ref, o_ref):
        c = c_ref[...]
        ca = (c * _sigmoid(c)).astype(BF16)
        o_ref[...] = _dot(ca, w_ref[...].astype(BF16), NN) + b_ref[...]

    return pl.pallas_call(
        body, name=name, grid=(L,),
        in_specs=[pl.BlockSpec((B, D), lambda l: (0, 0)), pl.BlockSpec((None, D, S), lambda l: (l, 0, 0)),
                  pl.BlockSpec((None, 1, S), lambda l: (l, 0, 0))],
        out_specs=pl.BlockSpec((None, B, S), lambda l: (l, 0, 0)),
        out_shape=jax.ShapeDtypeStruct((L, B, S), F32),
        compiler_params=_params(("parallel",)),
    )(c_all, ada_w, ada_b_loc)


def _ada_bwd(c_all_t, dmod_loc, name):
    D, B = c_all_t.shape
    L, _, S = dmod_loc.shape

    def body(c_ref, dm_ref, o_ref):
        c = c_ref[...]
        ca = c * _sigmoid(c)
        acc = None
        for bb in range(B):
            t = ca[:, bb:bb + 1] * dm_ref[bb:bb + 1, :]
            acc = t if acc is None else acc + t
        o_ref[...] = acc

    return pl.pallas_call(
        body, name=name, grid=(L,),
        in_specs=[pl.BlockSpec((D, B), lambda l: (0, 0)), pl.BlockSpec((None, B, S), lambda l: (l, 0, 0))],
        out_specs=pl.BlockSpec((None, D, S), lambda l: (l, 0, 0)),
        out_shape=jax.ShapeDtypeStruct((L, D, S), F32),
        compiler_params=_params(("parallel",)),
    )(c_all_t, dmod_loc)


def _sum_devices(parts, name):
    _, R, C = parts.shape
    tm = _tile(R, 256, 8)

    def body(p_ref, o_ref):
        acc = p_ref[0].astype(F32)
        for d in range(1, N_DEV):
            acc = acc + p_ref[d].astype(F32)
        o_ref[...] = acc

    return pl.pallas_call(
        body, name=name, grid=(R // tm,),
        in_specs=[pl.BlockSpec((N_DEV, tm, C), lambda i: (0, i, 0))],
        out_specs=pl.BlockSpec((tm, C), lambda i: (i, 0)),
        out_shape=jax.ShapeDtypeStruct((R, C), F32),
        compiler_params=_params(("parallel",)),
    )(parts)


def _adamw_math(w, g, m, v):
    m = ADAM_B1 * m + (1.0 - ADAM_B1) * g
    v = ADAM_B2 * v + (1.0 - ADAM_B2) * (g * g)
    m_hat = m / (1.0 - ADAM_B1 ** ADAM_STEP)
    v_hat = v / (1.0 - ADAM_B2 ** ADAM_STEP)
    delta = -ADAM_LR * (m_hat / (jnp.sqrt(v_hat) + ADAM_EPS) + ADAM_WD * w)
    return delta, m, v


def _adamw(w, g, m, v, name, summed):
    R, C = w.shape
    tm = _tile(R, 256, 16)
    n_parts = g.shape[0] if summed else 0

    def body(w_ref, g_ref, m_ref, v_ref, go_ref, d_ref, mo_ref, vo_ref):
        if summed:
            g = g_ref[0].astype(F32)
            for d in range(1, n_parts):
                g = g + g_ref[d].astype(F32)
        else:
            g = g_ref[...]
        delta, mn, vn = _adamw_math(w_ref[...], g, m_ref[...], v_ref[...])
        go_ref[...] = g
        d_ref[...] = delta
        mo_ref[...] = mn
        vo_ref[...] = vn

    big = pl.BlockSpec((tm, C), lambda i: (i, 0))
    gspec = pl.BlockSpec((n_parts, tm, C), lambda i: (0, i, 0)) if summed else big
    return pl.pallas_call(
        body, name=name, grid=(R // tm,),
        in_specs=[big, gspec, big, big],
        out_specs=[big, big, big, big],
        out_shape=tuple(jax.ShapeDtypeStruct((R, C), F32) for _ in range(4)),
        compiler_params=_params(("parallel",)),
    )(w, g, m, v)


def _adamw_summed(w, parts, m, v, name):
    L = len(parts)
    n_parts, R, C = parts[0].shape
    tm = _tile(R, 256, 16)
    nr = R // tm

    def body(*refs):
        w_ref, g_refs = refs[0], refs[1:1 + L]
        m_ref, v_ref, go_ref, d_ref, mo_ref, vo_ref = refs[1 + L:]
        for ll in range(L):
            @pl.when(pl.program_id(0) == ll)
            def _(ll=ll):
                g = g_refs[ll][0].astype(F32)
                for d in range(1, n_parts):
                    g = g + g_refs[ll][d].astype(F32)
                delta, mn, vn = _adamw_math(w_ref[...], g, m_ref[...], v_ref[...])
                go_ref[...] = g
                d_ref[...] = delta
                mo_ref[...] = mn
                vo_ref[...] = vn

    big = pl.BlockSpec((tm, C), lambda l, i: (l * nr + i, 0))
    gspecs = [pl.BlockSpec((n_parts, tm, C), functools.partial(lambda l, i, ll: (0, jnp.where(l == ll, i, 0), 0), ll=ll))
              for ll in range(L)]
    return pl.pallas_call(
        body, name=name, grid=(L, nr),
        in_specs=[big, *gspecs, big, big],
        out_specs=[big, big, big, big],
        out_shape=tuple(jax.ShapeDtypeStruct((L * R, C), F32) for _ in range(4)),
        compiler_params=_params(("arbitrary", "arbitrary")),
    )(w, *parts, m, v)


def _pack(arrs, D):
    L = arrs[0].shape[0]
    cols = []
    for a in arrs:
        f = a.reshape(L, -1)
        n = f.shape[1]
        cols.append(jnp.pad(f, ((0, 0), (0, -(-n // D) * D - n))))
    flat = jnp.concatenate(cols, axis=1)
    return flat.reshape(L, flat.shape[1] // D, D)


def _unpack(p, shapes, D):
    L = p.shape[0]
    out, r = [], 0
    for s in shapes:
        n = math.prod(s[1:])
        rows = -(-n // D)
        out.append(p[:, r:r + rows].reshape(L, rows * D)[:, :n].reshape(s))
        r += rows
    return out


def kernel(x, c, w_in, b_f, conv_w, conv_b, conv_ln_g, conv_ln_b, w_o, w_ffn_in, w_ffn_out, mix_pre_g, mix_post_g, ffn_pre_g, ffn_post_g, ada_w, ada_b, loss_target, m_w_in, m_b_f, m_conv_w, m_conv_b, m_conv_ln_g, m_conv_ln_b, m_w_o, m_w_ffn_in, m_w_ffn_out, m_mix_pre_g, m_mix_post_g, m_ffn_pre_g, m_ffn_post_g, m_ada_w, m_ada_b, v_w_in, v_b_f, v_conv_w, v_conv_b, v_conv_ln_g, v_conv_ln_b, v_w_o, v_w_ffn_in, v_w_ffn_out, v_mix_pre_g, v_mix_post_g, v_ffn_pre_g, v_ffn_post_g, v_ada_w, v_ada_b):
    L, D, s_in = w_in.shape
    T = x.shape[1]
    H = b_f.shape[1]
    A = D // 2
    C = D - A
    cs = conv_w.shape[2]
    F = w_ffn_out.shape[1] * N_DEV
    s_ff = w_ffn_in.shape[2]
    s_ada = ada_w.shape[2]
    R = 16
    me = _my_index()
    x0 = x[0]
    target = loss_target[0]
    tq = _tile(T, 512)
    nq = T // tq

    h_gmix, h_gffn, chain = [], [], ()
    for l in range(L):
        h_gmix.append(_gather_start([w_in[l].astype(BF16), w_o[l].astype(BF16)], f"gather_mix_{l}", chain))
        h_gffn.append(_gather_start([w_ffn_in[l].astype(BF16), w_ffn_out[l].astype(BF16)], f"gather_ffn_{l}",
                                    (h_gmix[l]["token"],)))
        chain = (h_gffn[l]["token"],)
    gather_tokens = chain
    cw_g = _exchange(conv_w, gather=True, name="gather_conv_w")
    conv_w_full = jnp.transpose(cw_g, (1, 2, 0, 3)).reshape(L, CONV_K, C)
    conv_w_pad = jnp.pad(conv_w_full, ((0, 0), (0, CONV_PAD - CONV_K), (0, 0)))
    b_f_col = jnp.pad(b_f, ((0, 0), (0, R - H)))[:, :, None]

    def by_cols(g):
        return jnp.transpose(g, (1, 0, 2)).reshape(g.shape[1], N_DEV * g.shape[2])

    def by_rows(g):
        return g.reshape(N_DEV * g.shape[1], g.shape[2])

    W_qkv, W_f_t, W_f_pad, W_c, W_o, W_ffn_in, W_ffn_out = ([None] * L for _ in range(7))

    c_all = _exchange(c, gather=True, name="gather_c").reshape(N_DEV, D)
    ada_b_loc = lax.dynamic_slice_in_dim(ada_b, me * s_ada, s_ada, axis=1)[:, None, :]
    mod_loc = _ada_fwd(c_all, ada_w, ada_b_loc, "ada_fwd")
    mod_g = _exchange(mod_loc, gather=True, name="gather_mod")
    mod = lax.dynamic_index_in_dim(mod_g, me, axis=2, keepdims=False)
    mod = jnp.transpose(mod, (1, 0, 2)).reshape(L, N_MOD, 1, D)

    saved = []
    xc = x0
    for l in range(L):
        sh1, sc1, g1, sh2, sc2, g2 = (mod[l, k] for k in range(N_MOD))
        gpre1, gpost1, gpre2, gpost2 = (p[l][None, :] for p in (mix_pre_g, mix_post_g, ffn_pre_g, ffn_post_g))
        dep = gather_tokens[0] if l == 0 else xc
        g_in, g_o = _gather_wait(_gather_mid(h_gmix[l], dep, f"gather_mid_mix_{l}"), dep, f"gather_wait_mix_{l}")
        W_in_l = by_cols(g_in)
        W_qkv[l], W_f, W_c[l], W_o[l] = W_in_l[:, :3 * A], W_in_l[:, 3 * A:3 * A + H], W_in_l[:, 3 * A + H:], by_rows(g_o)
        W_f_t[l] = jnp.pad(jnp.transpose(W_f), ((0, R - H), (0, 0)))
        W_f_pad[l] = jnp.pad(W_f, ((0, 0), (0, R - H)))
        h1 = _prenorm(xc, gpre1, sc1, sh1, f"prenorm1_{l}", first=gather_tokens if l == 0 else ())
        qkv = _mm([(h1, W_qkv[l])], "nn", BF16, f"proj_qkv_{l}")
        cproj = _mm([(h1, W_c[l])], "nn", F32, f"proj_conv_{l}")
        fl, cum = _fgate_fwd(h1, W_f_t[l], b_f_col[l], f"fgate_{l}")
        cum_rows = cum[:H].reshape(H * nq, tq)
        o, lse = _attn_fwd(qkv, cum_rows, H, f"attn_{l}", tq)
        h_ffn_l = _gather_mid(h_gffn[l], o, f"gather_mid_ffn_{l}")
        u3 = _conv_fwd(cproj, conv_w_pad[l], conv_b[l][None, :], conv_ln_g[l][None, :], conv_ln_b[l][None, :], f"conv_{l}")
        cat = jnp.concatenate([o, u3], axis=-1)
        y1, x_mid = _mm_postnorm(cat, W_o[l], xc, g1, gpost1, f"out_proj_{l}")
        g_fi, g_fo = _gather_wait(h_ffn_l, x_mid, f"gather_wait_ffn_{l}")
        W_ffn_in[l], W_ffn_out[l] = by_cols(g_fi), by_rows(g_fo)
        h2 = _prenorm(x_mid, gpre2, sc2, sh2, f"prenorm2_{l}")
        g, u, act = _ffn_in_fwd(h2, W_ffn_in[l], f"ffn_in_{l}")
        y2, x_out = _mm_postnorm(act, W_ffn_out[l], x_mid, g2, gpost2, f"ffn_out_{l}")
        saved.append((xc, h1, qkv, cproj, fl, cum_rows, o, lse, cat, y1, x_mid, h2, g, u, act, y2))
        xc = x_out

    loss_tile, dx = _loss_head(xc, target, "loss_head")
    loss = lax.psum(loss_tile[0, 0], ("x", "y", "c"))

    mc = lax.axis_index("c")

    def reduce_start(parts, name):
        keeps, sends = [], []
        for part in parts:
            part4 = part.reshape(4, 2, *part.shape[1:])
            keeps.append(lax.dynamic_index_in_dim(part4, mc, axis=1, keepdims=False))
            sends.append(lax.dynamic_index_in_dim(part4, 1 - mc, axis=1, keepdims=False))
        gots = _swap_sibling(sends, "swap_" + name)
        boths = [_pair_sum(k.reshape(-1, k.shape[-1]), t.reshape(-1, t.shape[-1]), f"pairsum{a}_{name}").reshape(k.shape)
                 for a, (k, t) in enumerate(zip(keeps, gots))]
        return _chips_start(boths, "scatter_" + name)

    small, h_ffn, h_mix = [None] * L, [None] * L, [None] * L
    for l in reversed(range(L)):
        xin, h1, qkv, cproj, fl, cum_rows, o, lse, cat, y1, x_mid, h2, g, u, act, y2 = saved[l]
        sh1, sc1, g1, sh2, sc2, g2 = (mod[l, k] for k in range(N_MOD))
        gpre1, gpost1, gpre2, gpost2 = (p[l][None, :] for p in (mix_pre_g, mix_post_g, ffn_pre_g, ffn_post_g))
        dy2, sm_post2 = _postnorm_bwd(dx, y2, g2, gpost2, f"postnorm2_bwd_{l}")
        dgate, dup = _ffn_act_bwd(dy2, W_ffn_out[l], g, u, f"ffn_act_bwd_{l}")
        dW_ffn_out = _mm([(act, dy2)], "tn", BF16, f"dw_ffn_out_{l}")
        dh2 = _mm([(dgate, W_ffn_in[l], 0), (dup, W_ffn_in[l], 1)], "nt", F32, f"dh2_{l}")
        dWg = _mm([(h2, dgate)], "tn", BF16, f"dw_ffn_gate_{l}")
        dWu = _mm([(h2, dup)], "tn", BF16, f"dw_ffn_up_{l}")
        dW_ffn_in = jnp.concatenate([dWg, dWu], axis=1)
        h_ffn[l], token = reduce_start([jnp.transpose(dW_ffn_in.reshape(D, N_DEV, s_ff), (1, 0, 2)),
                                        dW_ffn_out.reshape(N_DEV, F // N_DEV, D)], f"ffn_{l}")
        dx_mid, sm_pre2 = _prenorm_bwd(dh2, x_mid, dx, gpre2, sc2, f"prenorm2_bwd_{l}", first=(token,))
        dy1, sm_post1 = _postnorm_bwd(dx_mid, y1, g1, gpost1, f"postnorm1_bwd_{l}")
        dcat = _mm([(dy1, W_o[l])], "nt", F32, f"dcat_{l}")
        dW_o = _mm([(cat, dy1)], "tn", BF16, f"dw_o_{l}")
        dqkv, dcum_rows, dcum_q = _attn_bwd(qkv, dcat, o, lse, cum_rows, H, f"attn_bwd_{l}", tq)
        dcproj, dconv_w, sm_conv = _conv_bwd(cproj, dcat, conv_w_pad[l], conv_b[l][None, :], conv_ln_g[l][None, :],
                                             conv_ln_b[l][None, :], f"conv_bwd_{l}")
        dcum = dcum_rows.reshape(H, T) + jnp.transpose(dcum_q, (0, 2, 1)).reshape(H, T)
        dcum = jnp.pad(dcum, ((0, R - H), (0, 0)))
        dfl_t, dwf_t, dbf = _fgate_bwd(dcum, fl, h1, f"fgate_bwd_{l}")
        dfl = jnp.transpose(dfl_t).astype(BF16)
        dh1 = _mm([(dqkv[0], W_qkv[l], 0), (dqkv[1], W_qkv[l], 1), (dqkv[2], W_qkv[l], 2),
                   (dfl, W_f_pad[l]), (dcproj, W_c[l])], "nt", F32, f"dh1_{l}")
        dWq = [_mm([(h1, dqkv[k])], "tn", BF16, f"dw_qkv{k}_{l}") for k in range(3)]
        dWc = _mm([(h1, dcproj)], "tn", BF16, f"dw_conv_{l}")
        dWf = jnp.transpose(dwf_t[:H]).astype(BF16)
        dW_in = jnp.concatenate(dWq + [dWf, dWc], axis=1)
        h_mix[l], token = reduce_start([jnp.transpose(dW_in.reshape(D, N_DEV, s_in), (1, 0, 2)),
                                        dW_o.reshape(N_DEV, D // N_DEV, D)], f"mix_{l}")
        dx, sm_pre1 = _prenorm_bwd(dh1, xin, dx_mid, gpre1, sc1, f"prenorm1_bwd_{l}", first=(token,))
        dmod = jnp.stack([sm_pre1[0], sm_pre1[1], sm_post1[0], sm_pre2[0], sm_pre2[1], sm_post2[0]])
        small[l] = (dmod, sm_pre1[2], sm_post1[1], sm_pre2[2], sm_post2[1], sm_conv[0], sm_conv[1], sm_conv[2],
                    dbf[:H, 0], dconv_w[:CONV_K])
    grad_x = dx[None]

    small_names = 10
    small_l = [jnp.stack([small[l][k] for l in range(L)]) for k in range(small_names)]
    small_shapes = [a.shape for a in small_l]
    packed = _pack(small_l, D)
    rows = packed.shape[1]
    rows_pad = -(-L * rows // 8) * 8
    packed2 = jnp.pad(packed.reshape(L * rows, D), ((0, rows_pad - L * rows), (0, 0)))
    small_g = _exchange(packed2, gather=True, name="gather_small")
    small_sum = _sum_devices(small_g, "sum_small")[:L * rows].reshape(L, rows, D)
    (g_ada_b6, g_mix_pre, g_mix_post, g_ffn_pre, g_ffn_post, g_conv_b, g_ln_g, g_ln_b, g_b_f,
     g_conv_w_full) = _unpack(small_sum, small_shapes, D)
    g_ada_b = g_ada_b6.reshape(L, N_MOD * D)
    g_conv_w = lax.dynamic_slice_in_dim(g_conv_w_full, me * cs, cs, axis=2)
    dmod_all = small_g[:, :L * rows].reshape(N_DEV, L, rows, D)[:, :, :N_MOD].reshape(N_DEV, L, N_MOD * D)
    dmod_loc = jnp.transpose(lax.dynamic_slice_in_dim(dmod_all, me * s_ada, s_ada, axis=2), (1, 0, 2))
    g_ada_w = _ada_bwd(jnp.transpose(c_all), dmod_loc, "ada_bwd")

    got_ffn = [_chips_wait(h_ffn[l], dx, f"scatter_wait_ffn_{l}") for l in range(L)]
    got_mix = [_chips_wait(h_mix[l], dx, f"scatter_wait_mix_{l}") for l in range(L)]

    def step(w, m, v, parts, name):
        two_d = lambda t: t.reshape(-1, t.shape[-1])
        outs = _adamw_summed(two_d(w), parts, two_d(m), two_d(v), "adamw_" + name)
        return [t.reshape(w.shape) for t in outs]

    r_w_ffn_in = step(w_ffn_in, m_w_ffn_in, v_w_ffn_in, [got_ffn[l][0] for l in range(L)], "w_ffn_in")
    r_w_ffn_out = step(w_ffn_out, m_w_ffn_out, v_w_ffn_out, [got_ffn[l][1] for l in range(L)], "w_ffn_out")
    r_w_in = step(w_in, m_w_in, v_w_in, [got_mix[l][0] for l in range(L)], "w_in")
    r_w_o = step(w_o, m_w_o, v_w_o, [got_mix[l][1] for l in range(L)], "w_o")
    r_ada_w = [t.reshape(ada_w.shape) for t in _adamw(
        ada_w.reshape(L * D, s_ada), g_ada_w.reshape(L * D, s_ada), m_ada_w.reshape(L * D, s_ada),
        v_ada_w.reshape(L * D, s_ada), "adamw_ada_w", False)]

    sw = [b_f, conv_w, conv_b, conv_ln_g, conv_ln_b, mix_pre_g, mix_post_g, ffn_pre_g, ffn_post_g, ada_b]
    sg = [g_b_f, g_conv_w, g_conv_b, g_ln_g, g_ln_b, g_mix_pre, g_mix_post, g_ffn_pre, g_ffn_post, g_ada_b]
    sm = [m_b_f, m_conv_w, m_conv_b, m_conv_ln_g, m_conv_ln_b, m_mix_pre_g, m_mix_post_g, m_ffn_pre_g, m_ffn_post_g, m_ada_b]
    sv = [v_b_f, v_conv_w, v_conv_b, v_conv_ln_g, v_conv_ln_b, v_mix_pre_g, v_mix_post_g, v_ffn_pre_g, v_ffn_post_g, v_ada_b]
    shapes = [a.shape for a in sw]

    def flat(arrs):
        p = _pack(arrs, D)
        n = p.shape[0] * p.shape[1]
        return jnp.pad(p.reshape(n, D), ((0, -(-n // 8) * 8 - n), (0, 0))), p.shape

    pw, pshape = flat(sw)
    pg, pm, pv = flat(sg)[0], flat(sm)[0], flat(sv)[0]
    s_outs = _adamw(pw, pg, pm, pv, "adamw_small", False)
    n_small = pshape[0] * pshape[1]
    s_g, s_d, s_m, s_v = (_unpack(t[:n_small].reshape(pshape), shapes, D) for t in s_outs)

    big = {"w_in": r_w_in, "w_o": r_w_o, "w_ffn_in": r_w_ffn_in, "w_ffn_out": r_w_ffn_out, "ada_w": r_ada_w}
    order = ["w_in", "b_f", "conv_w", "conv_b", "conv_ln_g", "conv_ln_b", "w_o", "w_ffn_in", "w_ffn_out",
             "mix_pre_g", "mix_post_g", "ffn_pre_g", "ffn_post_g", "ada_w", "ada_b"]
    small_pos = {n: i for i, n in enumerate(["b_f", "conv_w", "conv_b", "conv_ln_g", "conv_ln_b", "mix_pre_g",
                                             "mix_post_g", "ffn_pre_g", "ffn_post_g", "ada_b"])}

    def pick(n, k):
        if n in big:
            return big[n][k]
        return (s_g, s_d, s_m, s_v)[k][small_pos[n]]

    return (loss, grad_x, *[pick(n, 0) for n in order], *[pick(n, 1) for n in order],
            *[pick(n, 2) for n in order], *[pick(n, 3) for n in order])
```

```python
import functools
import math

import jax
import jax.numpy as jnp
from jax import lax
from jax.experimental import pallas as pl
from jax.experimental.pallas import tpu as pltpu

F32 = jnp.float32
BF16 = jnp.bfloat16
MESH = pl.DeviceIdType.MESH
N_DEV = 8
EPS = 1e-6
CONV_K = 31
CONV_PAD = 32
CONV_CHUNK = 128
N_MOD = 6
NEG = -1e30
LANES = 128
VMEM_LIMIT = 56 * 2**20
MM_TILE = 1024
MM_ROWS_SMALL = 512
FFN_TILE = 1408
ADAM_LR, ADAM_B1, ADAM_B2, ADAM_EPS, ADAM_WD, ADAM_STEP = 0.001, 0.9, 0.999, 1e-08, 0.01, 10

NN = (((1,), (0,)), ((), ()))
NT = (((1,), (1,)), ((), ()))
TN = (((0,), (0,)), ((), ()))


def _dot(a, b, dims):
    return lax.dot_general(a, b, dims, preferred_element_type=F32)


def _tile(n, pref, align=LANES):
    if n <= pref:
        return n
    t = (pref // align) * align
    while t >= align:
        if n % t == 0:
            return t
        t -= align
    return n


def _params(sem=None):
    return pltpu.CompilerParams(dimension_semantics=sem, vmem_limit_bytes=VMEM_LIMIT)


def _sigmoid(x):
    return 1.0 / (1.0 + jnp.exp(-x))


def _my_index():
    return 4 * lax.axis_index("x") + 2 * lax.axis_index("y") + lax.axis_index("c")


def _exchange(x, *, gather, name):
    blk = x.shape if gather else x.shape[1:]

    def body(x_ref, y_ref, send_sems, recv_sems, local_sem):
        mx, my, mc = lax.axis_index("x"), lax.axis_index("y"), lax.axis_index("c")
        me = 4 * mx + 2 * my + mc

        def src(p):
            return x_ref if gather else x_ref.at[p]

        mine = pltpu.make_async_copy(src(me), y_ref.at[me], local_sem)
        mine.start()
        copies = []
        for k in range(1, N_DEV):
            px = (1 - mx) if (k >> 2) & 1 else mx
            py = (1 - my) if (k >> 1) & 1 else my
            pc = (1 - mc) if k & 1 else mc
            cp = pltpu.make_async_remote_copy(
                src_ref=src(4 * px + 2 * py + pc), dst_ref=y_ref.at[me],
                send_sem=send_sems.at[k - 1], recv_sem=recv_sems.at[k - 1],
                device_id=(px, py, pc), device_id_type=MESH)
            cp.start()
            copies.append(cp)
        for cp in copies:
            cp.wait()
        mine.wait()

    return pl.pallas_call(
        body, name=name,
        out_shape=jax.ShapeDtypeStruct((N_DEV,) + tuple(blk), x.dtype),
        in_specs=[pl.BlockSpec(memory_space=pl.ANY)],
        out_specs=pl.BlockSpec(memory_space=pl.ANY),
        scratch_shapes=[pltpu.SemaphoreType.DMA((N_DEV - 1,)), pltpu.SemaphoreType.DMA((N_DEV - 1,)),
                        pltpu.SemaphoreType.DMA(())],
    )(x)


_HBM = pl.BlockSpec(memory_space=pl.ANY)


def _swap_sibling(xs, name):
    n = len(xs)

    def body(*refs):
        send_sems, recv_sems = refs[2 * n], refs[2 * n + 1]
        sib = (lax.axis_index("x"), lax.axis_index("y"), 1 - lax.axis_index("c"))
        copies = [pltpu.make_async_remote_copy(src_ref=refs[a], dst_ref=refs[n + a], send_sem=send_sems.at[a],
                                               recv_sem=recv_sems.at[a], device_id=sib, device_id_type=MESH)
                  for a in range(n)]
        for cp in copies:
            cp.start()
        for cp in copies:
            cp.wait()

    return pl.pallas_call(
        body, name=name, out_shape=[jax.ShapeDtypeStruct(x.shape, x.dtype) for x in xs],
        in_specs=[_HBM] * n, out_specs=[_HBM] * n,
        scratch_shapes=[pltpu.SemaphoreType.DMA((n,)), pltpu.SemaphoreType.DMA((n,))],
    )(*xs)


_SEM = pl.BlockSpec(memory_space=pltpu.SEMAPHORE)
_HBM_SPEC = pl.BlockSpec(memory_space=pltpu.HBM)
_EFFECT = pltpu.CompilerParams(has_side_effects=pltpu.SideEffectType.DATAFLOW_SIDE_EFFECTING)


def _in_hbm(a):
    return pltpu.with_memory_space_constraint(a, pltpu.HBM)


def _chip_copies(x_refs, land_refs, send_sems, recv_sems, loc_sems):
    mx, my, mc = lax.axis_index("x"), lax.axis_index("y"), lax.axis_index("c")
    here = 2 * mx + my
    local, remote = [], []
    for a, (x_ref, land_ref) in enumerate(zip(x_refs, land_refs)):
        local.append(pltpu.make_async_copy(x_ref.at[here], land_ref.at[here], loc_sems.at[a]))
        for j, (px, py) in enumerate([(1 - mx, my), (mx, 1 - my), (1 - mx, 1 - my)]):
            remote.append(pltpu.make_async_remote_copy(
                src_ref=x_ref.at[2 * px + py], dst_ref=land_ref.at[here], send_sem=send_sems.at[3 * a + j],
                recv_sem=recv_sems.at[3 * a + j], device_id=(px, py, mc), device_id_type=MESH))
    return local, remote


def _chips_start(xs, name):
    n = len(xs)

    def body(*refs):
        local, remote = _chip_copies(refs[:n], refs[n:2 * n], *refs[2 * n:2 * n + 3])
        for cp in local + remote:
            cp.start()
        refs[-1][...] = jnp.zeros_like(refs[-1])

    lands = [lax.empty(x.shape, x.dtype) for x in xs]
    outs = pl.pallas_call(
        body, name=name,
        out_shape=(pltpu.SemaphoreType.DMA((3 * n,)), pltpu.SemaphoreType.DMA((3 * n,)), pltpu.SemaphoreType.DMA((n,)),
                   *[pltpu.HBM(x.shape, x.dtype) for x in xs], *[pltpu.HBM(x.shape, x.dtype) for x in xs],
                   jax.ShapeDtypeStruct((8, LANES), F32)),
        in_specs=[_HBM_SPEC] * (2 * n),
        out_specs=(_SEM, _SEM, _SEM, *[_HBM_SPEC] * (2 * n), pl.BlockSpec(memory_space=pltpu.VMEM)),
        input_output_aliases={i: 3 + i for i in range(2 * n)},
        compiler_params=_EFFECT,
    )(*[_in_hbm(x) for x in xs], *[_in_hbm(t) for t in lands])
    return (outs[0], outs[1], outs[2], list(outs[3:3 + n]), list(outs[3 + n:3 + 2 * n])), outs[-1]


def _chips_wait(handle, after, name):
    send_sems, recv_sems, loc_sems, x_thru, land_thru = handle
    n = len(x_thru)

    def body(*refs):
        local, remote = _chip_copies(refs[:n], refs[n:2 * n], *refs[2 * n:2 * n + 3])
        for cp in local:
            cp.wait()
        for cp in remote:
            cp.wait_send()
            cp.wait_recv()

    outs = pl.pallas_call(
        body, name=name,
        out_shape=[pltpu.HBM(x.shape, x.dtype) for x in x_thru + land_thru],
        in_specs=[_HBM_SPEC] * (2 * n) + [_SEM, _SEM, _SEM, _HBM],
        out_specs=[_HBM_SPEC] * (2 * n),
        input_output_aliases={i: i for i in range(2 * n)},
        compiler_params=_EFFECT,
    )(*x_thru, *land_thru, send_sems, recv_sems, loc_sems, after)
    return list(outs[n:])


def _gather_first_copies(x_refs, y_refs, send_sems, sib_recv, ici_recv, loc_sems):
    mx, my, mc = lax.axis_index("x"), lax.axis_index("y"), lax.axis_index("c")
    me = 4 * mx + 2 * my + mc
    local, remote = [], []
    for a, (x_ref, y_ref) in enumerate(zip(x_refs, y_refs)):
        local.append(pltpu.make_async_copy(x_ref, y_ref.at[me], loc_sems.at[a]))
        remote.append(pltpu.make_async_remote_copy(
            src_ref=x_ref, dst_ref=y_ref.at[me], send_sem=send_sems.at[4 * a], recv_sem=sib_recv.at[a],
            device_id=(mx, my, 1 - mc), device_id_type=MESH))
        for j, (px, py) in enumerate([(1 - mx, my), (mx, 1 - my), (1 - mx, 1 - my)]):
            remote.append(pltpu.make_async_remote_copy(
                src_ref=x_ref, dst_ref=y_ref.at[me], send_sem=send_sems.at[4 * a + 1 + j], recv_sem=ici_recv.at[3 * a + j],
                device_id=(px, py, mc), device_id_type=MESH))
    return local, remote


def _gather_forward_copies(y_refs, ici_recv, fwd_send, fwd_recv):
    mx, my, mc = lax.axis_index("x"), lax.axis_index("y"), lax.axis_index("c")
    pairs = []
    for a, y_ref in enumerate(y_refs):
        for j, (px, py) in enumerate([(1 - mx, my), (mx, 1 - my), (1 - mx, 1 - my)]):
            slot = y_ref.at[4 * px + 2 * py + mc]
            arrival = pltpu.make_async_remote_copy(
                src_ref=slot, dst_ref=slot, send_sem=fwd_send.at[3 * a + j], recv_sem=ici_recv.at[3 * a + j],
                device_id=(px, py, mc), device_id_type=MESH)
            onward = pltpu.make_async_remote_copy(
                src_ref=slot, dst_ref=slot, send_sem=fwd_send.at[3 * a + j], recv_sem=fwd_recv.at[3 * a + j],
                device_id=(mx, my, 1 - mc), device_id_type=MESH)
            pairs.append((arrival, onward))
    return pairs


def _gather_start(xs, name, after=()):
    n = len(xs)
    ni = 2 * n + len(after)

    def body(*refs):
        local, remote = _gather_first_copies(refs[:n], refs[n:2 * n], *refs[ni:ni + 4])
        for cp in local + remote:
            cp.start()
        refs[-1][...] = jnp.zeros_like(refs[-1])

    ys = [lax.empty((N_DEV,) + tuple(x.shape), x.dtype) for x in xs]
    dma = pltpu.SemaphoreType.DMA
    outs = pl.pallas_call(
        body, name=name,
        out_shape=(dma((4 * n,)), dma((n,)), dma((3 * n,)), dma((n,)),
                   *[pltpu.HBM(x.shape, x.dtype) for x in xs], *[pltpu.HBM(y.shape, y.dtype) for y in ys],
                   jax.ShapeDtypeStruct((8, LANES), F32)),
        in_specs=[_HBM_SPEC] * (2 * n) + [_HBM] * len(after),
        out_specs=(_SEM, _SEM, _SEM, _SEM, *[_HBM_SPEC] * (2 * n), pl.BlockSpec(memory_space=pltpu.VMEM)),
        input_output_aliases={i: 4 + i for i in range(2 * n)},
        compiler_params=_EFFECT,
    )(*[_in_hbm(x) for x in xs], *[_in_hbm(y) for y in ys], *after)
    return dict(send=outs[0], sib_recv=outs[1], ici_recv=outs[2], loc=outs[3], x=list(outs[4:4 + n]),
                y=list(outs[4 + n:4 + 2 * n]), token=outs[-1])


def _gather_mid(h, after, name):
    n = len(h["y"])

    def body(*refs):
        for arrival, onward in _gather_forward_copies(refs[:n], refs[n], refs[n + 2 + n], refs[n + 3 + n]):
            arrival.wait_recv()
            onward.start()

    dma = pltpu.SemaphoreType.DMA
    outs = pl.pallas_call(
        body, name=name,
        out_shape=(*[pltpu.HBM(y.shape, y.dtype) for y in h["y"]], dma((3 * n,)), dma((3 * n,))),
        in_specs=[_HBM_SPEC] * n + [_SEM, _HBM],
        out_specs=(*[_HBM_SPEC] * n, _SEM, _SEM),
        input_output_aliases={i: i for i in range(n)},
        compiler_params=_EFFECT,
    )(*h["y"], h["ici_recv"], after)
    return dict(h, y=list(outs[:n]), fwd_send=outs[n], fwd_recv=outs[n + 1])


def _gather_wait(h, after, name):
    n = len(h["y"])

    def body(*refs):
        x_refs, y_refs = refs[:n], refs[n:2 * n]
        send, sib_recv, loc, fwd_send, fwd_recv = refs[2 * n:2 * n + 5]
        local, remote = _gather_first_copies(x_refs, y_refs, send, sib_recv, fwd_recv, loc)
        for cp in local:
            cp.wait()
        for k, cp in enumerate(remote):
            cp.wait_send()
            if k % 4 == 0:
                cp.wait_recv()
        for _, onward in _gather_forward_copies(y_refs, fwd_recv, fwd_send, fwd_recv):
            onward.wait_send()
            onward.wait_recv()

    outs = pl.pallas_call(
        body, name=name,
        out_shape=[pltpu.HBM(t.shape, t.dtype) for t in h["x"] + h["y"]],
        in_specs=[_HBM_SPEC] * (2 * n) + [_SEM] * 5 + [_HBM],
        out_specs=[_HBM_SPEC] * (2 * n),
        input_output_aliases={i: i for i in range(2 * n)},
        compiler_params=_EFFECT,
    )(*h["x"], *h["y"], h["send"], h["sib_recv"], h["loc"], h["fwd_send"], h["fwd_recv"], after)
    return list(outs[n:])


def _pair_sum(a, b, name):
    R, C = a.shape
    tm = _tile(R, 512, 16)

    def body(a_ref, b_ref, o_ref):
        o_ref[...] = (a_ref[...].astype(F32) + b_ref[...].astype(F32)).astype(BF16)

    big = pl.BlockSpec((tm, C), lambda i: (i, 0))
    return pl.pallas_call(
        body, name=name, grid=(R // tm,), in_specs=[big, big], out_specs=big,
        out_shape=jax.ShapeDtypeStruct((R, C), BF16), compiler_params=_params(("parallel",)),
    )(a, b)


def _mm(pairs, mode, out_dtype, name, tm=MM_TILE, tn=MM_TILE):
    dims = {"nn": NN, "nt": NT, "tn": TN}[mode]
    a0, b0 = pairs[0][0], pairs[0][1]
    M = a0.shape[1] if mode == "tn" else a0.shape[0]
    N = b0.shape[0] if mode == "nt" else b0.shape[1]
    tm, tn = _tile(M, tm), _tile(N, tn)
    in_specs, args = [], []
    for pr in pairs:
        a, b = pr[0], pr[1]
        if mode == "tn":
            K = a.shape[0]
            in_specs.append(pl.BlockSpec((K, tm), lambda i, j: (0, i)))
            in_specs.append(pl.BlockSpec((K, tn), lambda i, j: (0, j)))
        elif mode == "nn":
            K = a.shape[1]
            in_specs.append(pl.BlockSpec((tm, K), lambda i, j: (i, 0)))
            in_specs.append(pl.BlockSpec((K, tn), lambda i, j: (0, j)))
        else:
            K = a.shape[1]
            cb = pr[2] if len(pr) > 2 else 0
            in_specs.append(pl.BlockSpec((tm, K), lambda i, j: (i, 0)))
            in_specs.append(pl.BlockSpec((tn, K), functools.partial(lambda i, j, cb: (j, cb), cb=cb)))
        args += [a, b]
    n_pairs = len(pairs)

    def body(*refs):
        o_ref = refs[-1]
        acc = None
        for k in range(n_pairs):
            d = _dot(refs[2 * k][...], refs[2 * k + 1][...], dims)
            acc = d if acc is None else acc + d
        o_ref[...] = acc.astype(o_ref.dtype)

    return pl.pallas_call(
        body, name=name, grid=(M // tm, N // tn), in_specs=in_specs,
        out_specs=pl.BlockSpec((tm, tn), lambda i, j: (i, j)),
        out_shape=jax.ShapeDtypeStruct((M, N), out_dtype),
        compiler_params=_params(("parallel", "arbitrary")),
    )(*args)


def _prenorm(x, g, sc, sh, name, first=()):
    T, D = x.shape
    tm = _tile(T, 512, 8)

    def body(x_ref, g_ref, sc_ref, sh_ref, *rest):
        h_ref = rest[-1]
        xv = x_ref[...]
        r = lax.rsqrt(jnp.mean(xv * xv, axis=-1, keepdims=True) + EPS)
        h_ref[...] = (((xv * r) * g_ref[...]) * (1.0 + sc_ref[...]) + sh_ref[...]).astype(BF16)

    row = pl.BlockSpec((1, D), lambda i: (0, 0))
    return pl.pallas_call(
        body, name=name, grid=(T // tm,),
        in_specs=[pl.BlockSpec((tm, D), lambda i: (i, 0)), row, row, row] + [_HBM] * len(first),
        out_specs=pl.BlockSpec((tm, D), lambda i: (i, 0)),
        out_shape=jax.ShapeDtypeStruct((T, D), BF16),
        compiler_params=_params(("parallel",)),
    )(x, g, sc, sh, *first)


def _prenorm_bwd(dh, x, dres, g, sc, name, first=()):
    T, D = x.shape
    tm = _tile(T, 256, 8)

    def body(dh_ref, x_ref, dres_ref, g_ref, sc_ref, *rest):
        dx_ref, sm_ref = rest[-2:]
        @pl.when(pl.program_id(0) == 0)
        def _():
            sm_ref[...] = jnp.zeros_like(sm_ref)

        xv, dhv = x_ref[...], dh_ref[...]
        r = lax.rsqrt(jnp.mean(xv * xv, axis=-1, keepdims=True) + EPS)
        xh = xv * r
        one_sc = 1.0 + sc_ref[...]
        sm_ref[0:1, :] += jnp.sum(dhv, axis=0, keepdims=True)
        sm_ref[1:2, :] += jnp.sum(dhv * (xh * g_ref[...]), axis=0, keepdims=True)
        sm_ref[2:3, :] += jnp.sum(dhv * one_sc * xh, axis=0, keepdims=True)
        dxh = dhv * one_sc * g_ref[...]
        dx_ref[...] = dres_ref[...] + r * (dxh - xh * jnp.mean(dxh * xh, axis=-1, keepdims=True))

    row = pl.BlockSpec((1, D), lambda i: (0, 0))
    big = pl.BlockSpec((tm, D), lambda i: (i, 0))
    return pl.pallas_call(
        body, name=name, grid=(T // tm,),
        in_specs=[big, big, big, row, row] + [_HBM] * len(first),
        out_specs=[big, pl.BlockSpec((8, D), lambda i: (0, 0))],
        out_shape=(jax.ShapeDtypeStruct((T, D), F32), jax.ShapeDtypeStruct((8, D), F32)),
        compiler_params=_params(("arbitrary",)),
    )(dh, x, dres, g, sc, *first)


def _mm_postnorm(a, w, x, gate, gpost, name):
    T, K = a.shape
    D = w.shape[1]
    tm = _tile(T, 256, 8)

    def body(a_ref, w_ref, x_ref, gate_ref, gp_ref, y_ref, xn_ref):
        y = _dot(a_ref[...], w_ref[...], NN)
        r = lax.rsqrt(jnp.mean(y * y, axis=-1, keepdims=True) + EPS)
        y_ref[...] = y
        xn_ref[...] = x_ref[...] + gate_ref[...] * ((y * r) * gp_ref[...])

    row = pl.BlockSpec((1, D), lambda i: (0, 0))
    big = pl.BlockSpec((tm, D), lambda i: (i, 0))
    return pl.pallas_call(
        body, name=name, grid=(T // tm,),
        in_specs=[pl.BlockSpec((tm, K), lambda i: (i, 0)), pl.BlockSpec((K, D), lambda i: (0, 0)), big, row, row],
        out_specs=[big, big],
        out_shape=(jax.ShapeDtypeStruct((T, D), F32), jax.ShapeDtypeStruct((T, D), F32)),
        compiler_params=_params(("parallel",)),
    )(a, w, x, gate, gpost)


def _postnorm_bwd(dx, y, gate, gpost, name):
    T, D = y.shape
    tm = _tile(T, 256, 8)

    def body(dx_ref, y_ref, gate_ref, gp_ref, dy_ref, sm_ref):
        @pl.when(pl.program_id(0) == 0)
        def _():
            sm_ref[...] = jnp.zeros_like(sm_ref)

        yv, dxv = y_ref[...], dx_ref[...]
        r = lax.rsqrt(jnp.mean(yv * yv, axis=-1, keepdims=True) + EPS)
        yh = yv * r
        dn = dxv * gate_ref[...]
        sm_ref[0:1, :] += jnp.sum(dxv * (yh * gp_ref[...]), axis=0, keepdims=True)
        sm_ref[1:2, :] += jnp.sum(dn * yh, axis=0, keepdims=True)
        dyh = dn * gp_ref[...]
        dy_ref[...] = (r * (dyh - yh * jnp.mean(dyh * yh, axis=-1, keepdims=True))).astype(BF16)

    row = pl.BlockSpec((1, D), lambda i: (0, 0))
    big = pl.BlockSpec((tm, D), lambda i: (i, 0))
    return pl.pallas_call(
        body, name=name, grid=(T // tm,),
        in_specs=[big, big, row, row],
        out_specs=[big, pl.BlockSpec((8, D), lambda i: (0, 0))],
        out_shape=(jax.ShapeDtypeStruct((T, D), BF16), jax.ShapeDtypeStruct((8, D), F32)),
        compiler_params=_params(("arbitrary",)),
    )(dx, y, gate, gpost)


def _ffn_in_fwd(h, w, name):
    T, D = h.shape
    F = w.shape[1] // 2
    tm, tn = _tile(T, MM_ROWS_SMALL), _tile(F, FFN_TILE)
    nj = F // tn

    def body(h_ref, wg_ref, wu_ref, g_ref, u_ref, act_ref):
        hv = h_ref[...]
        g = _dot(hv, wg_ref[...], NN)
        u = _dot(hv, wu_ref[...], NN)
        g_ref[...] = g.astype(BF16)
        u_ref[...] = u.astype(BF16)
        act_ref[...] = ((g * _sigmoid(g)) * u).astype(BF16)

    out = pl.BlockSpec((tm, tn), lambda j, i: (i, j))
    return pl.pallas_call(
        body, name=name, grid=(nj, T // tm),
        in_specs=[pl.BlockSpec((tm, D), lambda j, i: (i, 0)),
                  pl.BlockSpec((D, tn), lambda j, i: (0, j)),
                  pl.BlockSpec((D, tn), lambda j, i: (0, j + nj))],
        out_specs=[out, out, out],
        out_shape=tuple(jax.ShapeDtypeStruct((T, F), BF16) for _ in range(3)),
        compiler_params=_params(("parallel", "arbitrary")),
    )(h, w, w)


def _ffn_act_bwd(dy, w_out, g, u, name):
    T, D = dy.shape
    F = w_out.shape[0]
    tm, tn = _tile(T, MM_ROWS_SMALL), _tile(F, FFN_TILE)

    def body(dy_ref, w_ref, g_ref, u_ref, dg_ref, du_ref):
        dact = _dot(dy_ref[...], w_ref[...], NT)
        gv, uv = g_ref[...].astype(F32), u_ref[...].astype(F32)
        sg = _sigmoid(gv)
        dg_ref[...] = (dact * uv * (sg * (1.0 + gv * (1.0 - sg)))).astype(BF16)
        du_ref[...] = (dact * (gv * sg)).astype(BF16)

    tile = pl.BlockSpec((tm, tn), lambda j, i: (i, j))
    return pl.pallas_call(
        body, name=name, grid=(F // tn, T // tm),
        in_specs=[pl.BlockSpec((tm, D), lambda j, i: (i, 0)), pl.BlockSpec((tn, D), lambda j, i: (j, 0)), tile, tile],
        out_specs=[tile, tile],
        out_shape=(jax.ShapeDtypeStruct((T, F), BF16), jax.ShapeDtypeStruct((T, F), BF16)),
        compiler_params=_params(("parallel", "arbitrary")),
    )(dy, w_out, g, u)


def _lane_scan(v, reverse):
    T = v.shape[-1]
    lane = lax.broadcasted_iota(jnp.int32, v.shape, 1)
    d = 1
    while d < T:
        if reverse:
            v = v + jnp.where(lane < T - d, pltpu.roll(v, T - d, axis=1), 0.0)
        else:
            v = v + jnp.where(lane >= d, pltpu.roll(v, d, axis=1), 0.0)
        d *= 2
    return v


def _fgate_fwd(h, wf_t, bf, name):
    T, D = h.shape
    R = wf_t.shape[0]

    def body(h_ref, w_ref, b_ref, fl_ref, cum_ref):
        fl = _dot(w_ref[...], h_ref[...], NT) + b_ref[...]
        fl_ref[...] = fl
        logf = jnp.minimum(fl, 0.0) - jnp.log(1.0 + jnp.exp(-jnp.abs(fl)))
        cum_ref[...] = _lane_scan(logf, reverse=False)

    return pl.pallas_call(
        body, name=name,
        out_shape=(jax.ShapeDtypeStruct((R, T), F32), jax.ShapeDtypeStruct((R, T), F32)),
        compiler_params=_params(),
    )(h, wf_t, bf)


def _fgate_bwd(dcum, fl, h, name):
    R, T = fl.shape
    D = h.shape[1]

    def body(dc_ref, fl_ref, h_ref, dfl_ref, dw_ref, db_ref):
        dlogf = _lane_scan(dc_ref[...], reverse=True)
        dfl = dlogf * _sigmoid(-fl_ref[...])
        dfl_ref[...] = dfl
        dw_ref[...] = _dot(dfl.astype(BF16), h_ref[...], NN)
        db_ref[...] = jnp.broadcast_to(jnp.sum(dfl, axis=-1, keepdims=True), (R, LANES))

    return pl.pallas_call(
        body, name=name,
        out_shape=(jax.ShapeDtypeStruct((R, T), F32), jax.ShapeDtypeStruct((R, D), F32),
                   jax.ShapeDtypeStruct((R, LANES), F32)),
        compiler_params=_params(),
    )(dcum, fl, h)


def _head_masks(hpb, dh, rows):
    lane = lax.broadcasted_iota(jnp.int32, (rows, LANES), 1)
    return [(lane >= h * dh) & (lane < (h + 1) * dh) for h in range(hpb)]


def _stack_heads(v, masks):
    return jnp.concatenate([jnp.where(mk, v, jnp.zeros_like(v)) for mk in masks], axis=0)


def _heads_to_lanes(col, masks, tq):
    out = jnp.broadcast_to(col[0:tq], (tq, LANES))
    for h in range(1, len(masks)):
        out = jnp.where(masks[h], col[h * tq:(h + 1) * tq], out)
    return out


def _causal_stack(hpb, tq):
    r = lax.broadcasted_iota(jnp.int32, (tq, tq), 0)
    c = lax.broadcasted_iota(jnp.int32, (tq, tq), 1)
    return jnp.concatenate([c] * hpb, axis=0) <= jnp.concatenate([r] * hpb, axis=0)


def _attn_fwd(qkv, cum_rows, n_heads, name, tq):
    T = qkv.shape[0]
    A = qkv.shape[1] // 3
    dh = A // n_heads
    hpb = LANES // dh
    nb = A // LANES
    nq = T // tq
    scale = dh ** -0.5

    def body(q_ref, k_ref, v_ref, c_ref, o_ref, l_ref, vbd):
        hp, i = pl.program_id(0), pl.program_id(1)
        masks = _head_masks(hpb, dh, tq)

        @pl.when(i == 0)
        def _():
            def fill(j, _):
                vbd[j] = _stack_heads(v_ref[pl.ds(pl.multiple_of(j * tq, tq), tq), :], masks)
                return 0

            lax.fori_loop(0, nq, fill, 0)

        qs = _stack_heads(q_ref[...], masks)
        crow0 = hp * hpb * nq

        def tile(j, carry, diag):
            m, l, acc = carry
            kt = k_ref[pl.ds(pl.multiple_of(j * tq, tq), tq), :]
            bias = jnp.concatenate(
                [jnp.broadcast_to(c_ref[pl.ds(crow0 + h * nq + j, 1), :], (tq, tq)) for h in range(hpb)], axis=0)
            s = _dot(qs, kt, NT) * scale - bias
            if diag:
                s = jnp.where(_causal_stack(hpb, tq), s, NEG)
            m_new = jnp.maximum(m, jnp.max(s, axis=-1, keepdims=True))
            p = jnp.exp(s - m_new)
            alpha = jnp.exp(m - m_new)
            l = alpha * l + jnp.sum(p, axis=-1, keepdims=True)
            pcat = jnp.concatenate([p[h * tq:(h + 1) * tq] for h in range(hpb)], axis=1).astype(BF16)
            acc = _heads_to_lanes(alpha, masks, tq) * acc + _dot(pcat, vbd[j], NN)
            return m_new, l, acc

        init = (jnp.full((hpb * tq, 1), NEG, F32), jnp.zeros((hpb * tq, 1), F32), jnp.zeros((tq, LANES), F32))
        carry = lax.fori_loop(0, i, lambda j, c: tile(j, c, False), init)
        m, l, acc = tile(i, carry, True)
        o_ref[...] = (acc / _heads_to_lanes(l, masks, tq)).astype(BF16)
        lse = m + jnp.log(l)
        for h in range(hpb):
            l_ref[:, h:h + 1] = lse[h * tq:(h + 1) * tq]

    return pl.pallas_call(
        body, name=name, grid=(nb, nq),
        in_specs=[pl.BlockSpec((tq, LANES), lambda h, i: (i, h)),
                  pl.BlockSpec((T, LANES), lambda h, i: (0, nb + h)),
                  pl.BlockSpec((T, LANES), lambda h, i: (0, 2 * nb + h)),
                  pl.BlockSpec(cum_rows.shape, lambda h, i: (0, 0))],
        out_specs=[pl.BlockSpec((tq, LANES), lambda h, i: (i, h)),
                   pl.BlockSpec((None, tq, hpb), lambda h, i: (h, i, 0))],
        out_shape=(jax.ShapeDtypeStruct((T, A), BF16), jax.ShapeDtypeStruct((nb, T, hpb), F32)),
        scratch_shapes=[pltpu.VMEM((nq, hpb * tq, LANES), BF16)],
        compiler_params=_params(("arbitrary", "arbitrary")),
    )(qkv, qkv, qkv, cum_rows)


def _attn_bwd(qkv, dcat, o, lse, cum_rows, n_heads, name, tq):
    T = qkv.shape[0]
    A = qkv.shape[1] // 3
    dh = A // n_heads
    hpb = LANES // dh
    nb = A // LANES
    nq = T // tq
    scale = dh ** -0.5

    def body(q_ref, k_ref, v_ref, do_ref, o_ref, l_ref, c_ref, dqkv_ref, dc_ref, dr_ref,
             dq_acc, delta, drow, qs_scr, dos_scr, kbd_scr):
        hp = pl.program_id(0)
        masks = _head_masks(hpb, dh, tq)
        crow0 = hp * hpb * nq

        def prologue(i, _):
            rs = pl.ds(pl.multiple_of(i * tq, tq), tq)
            do = do_ref[rs, :]
            prod = do * o_ref[rs, :].astype(F32)
            for h in range(hpb):
                delta[rs, h:h + 1] = jnp.sum(jnp.where(masks[h], prod, 0.0), axis=-1, keepdims=True)
            qs_scr[i] = _stack_heads(q_ref[rs, :], masks)
            dos_scr[i] = _stack_heads(do, masks).astype(BF16)
            kbd_scr[i] = _stack_heads(k_ref[rs, :], masks)
            dq_acc[rs, :] = jnp.zeros((tq, LANES), F32)
            drow[rs, :] = jnp.zeros((tq, hpb), F32)
            return 0

        lax.fori_loop(0, nq, prologue, 0)

        def kv_step(j, _):
            ks = pl.ds(pl.multiple_of(j * tq, tq), tq)
            kt, vt = k_ref[ks, :], v_ref[ks, :]
            kbd = kbd_scr[j]
            bias = jnp.concatenate(
                [jnp.broadcast_to(c_ref[pl.ds(crow0 + h * nq + j, 1), :], (tq, tq)) for h in range(hpb)], axis=0)

            def q_step(i, carry, diag):
                dk, dv, dcs = carry
                rs = pl.ds(pl.multiple_of(i * tq, tq), tq)
                qs, dos = qs_scr[i], dos_scr[i]
                s = _dot(qs, kt, NT) * scale - bias
                if diag:
                    s = jnp.where(_causal_stack(hpb, tq), s, NEG)
                lse = jnp.concatenate([l_ref[rs, h:h + 1] for h in range(hpb)], axis=0)
                p = jnp.exp(s - lse)
                dv = dv + _dot(p.astype(BF16), dos, TN)
                dp = _dot(dos, vt, NT)
                ds = p * (dp - jnp.concatenate([delta[rs, h:h + 1] for h in range(hpb)], axis=0))
                dcs = tuple(dcs[h] - jnp.sum(ds[h * tq:(h + 1) * tq], axis=0, keepdims=True) for h in range(hpb))
                rsum = jnp.sum(ds, axis=-1, keepdims=True)
                for h in range(hpb):
                    drow[rs, h:h + 1] += rsum[h * tq:(h + 1) * tq]
                dsb = (ds * scale).astype(BF16)
                dk = dk + _dot(dsb, qs, TN)
                dscat = jnp.concatenate([dsb[h * tq:(h + 1) * tq] for h in range(hpb)], axis=1)
                dq_acc[rs, :] += _dot(dscat, kbd, NN)
                return dk, dv, dcs

            init = (jnp.zeros((tq, LANES), F32), jnp.zeros((tq, LANES), F32),
                    tuple(jnp.zeros((1, tq), F32) for _ in range(hpb)))
            carry = q_step(j, init, True)
            dk, dv, dcs = lax.fori_loop(j + 1, nq, lambda i, c: q_step(i, c, False), carry)
            dqkv_ref[1, ks, :] = dk.astype(BF16)
            dqkv_ref[2, ks, :] = dv.astype(BF16)
            for h in range(hpb):
                dc_ref[pl.ds(crow0 + h * nq + j, 1), :] = dcs[h]
            return 0

        lax.fori_loop(0, nq, kv_step, 0)
        dqkv_ref[0] = dq_acc[...].astype(BF16)
        dr_ref[...] = drow[...]

    col = lambda off: pl.BlockSpec((T, LANES), functools.partial(lambda h, off: (0, off + h), off=off))
    return pl.pallas_call(
        body, name=name, grid=(nb,),
        in_specs=[col(0), col(nb), col(2 * nb), col(0), col(0),
                  pl.BlockSpec((None, T, hpb), lambda h: (h, 0, 0)),
                  pl.BlockSpec(cum_rows.shape, lambda h: (0, 0))],
        out_specs=[pl.BlockSpec((3, T, LANES), lambda h: (0, 0, h)),
                   pl.BlockSpec(cum_rows.shape, lambda h: (0, 0)),
                   pl.BlockSpec((None, T, hpb), lambda h: (h, 0, 0))],
        out_shape=(jax.ShapeDtypeStruct((3, T, A), BF16), jax.ShapeDtypeStruct(cum_rows.shape, F32),
                   jax.ShapeDtypeStruct((nb, T, hpb), F32)),
        scratch_shapes=[pltpu.VMEM((T, LANES), F32), pltpu.VMEM((T, hpb), F32), pltpu.VMEM((T, hpb), F32),
                        pltpu.VMEM((nq, hpb * tq, LANES), BF16), pltpu.VMEM((nq, hpb * tq, LANES), BF16),
                        pltpu.VMEM((nq, hpb * tq, LANES), BF16)],
        compiler_params=_params(("arbitrary",)),
    )(qkv, qkv, qkv, dcat, o, lse, cum_rows)


def _glu_into(upad, cv_ref, cg_ref, T):
    upad[0:CONV_PAD, :] = jnp.zeros((CONV_PAD, upad.shape[1]), F32)

    def fill(c, _):
        rs = pl.ds(pl.multiple_of(c * CONV_CHUNK, CONV_CHUNK), CONV_CHUNK)
        upad[pl.ds(pl.multiple_of(CONV_PAD + c * CONV_CHUNK, 8), CONV_CHUNK), :] = cv_ref[rs, :] * _sigmoid(cg_ref[rs, :])
        return 0

    lax.fori_loop(0, T // CONV_CHUNK, fill, 0)


SUBLANES = 8
CONV_SHIFT_ROWS = CONV_CHUNK + CONV_PAD - SUBLANES


def _load_window(win, sh, src, r0):
    win[...] = src[pl.ds(r0, CONV_CHUNK + CONV_PAD), :]
    for b in range(1, SUBLANES):
        sh[b - 1] = win[b:b + CONV_SHIFT_ROWS, :]


def _tap(win, sh, o):
    b = o % SUBLANES
    if b == 0:
        return win[o:o + CONV_CHUNK, :]
    return sh[b - 1, o - b:o - b + CONV_CHUNK, :]


def _conv_taps(win, sh, w_ref, first, step):
    acc = None
    for k in range(CONV_K):
        t = w_ref[k:k + 1, :] * _tap(win, sh, first + step * k)
        acc = t if acc is None else acc + t
    return acc


def _conv_scratch(C):
    return [pltpu.VMEM((CONV_CHUNK + CONV_PAD, C), F32), pltpu.VMEM((SUBLANES - 1, CONV_SHIFT_ROWS, C), F32)]


def _conv_fwd(cproj, w, b, lg, lb, name):
    T = cproj.shape[0]
    C = cproj.shape[1] // 2
    off = CONV_PAD - (CONV_K - 1)

    def body(cv_ref, cg_ref, w_ref, b_ref, lg_ref, lb_ref, out_ref, upad, win, sh):
        _glu_into(upad, cv_ref, cg_ref, T)

        def chunk(c, _):
            r0 = pl.multiple_of(c * CONV_CHUNK, CONV_CHUNK)
            _load_window(win, sh, upad, r0)
            u1 = _conv_taps(win, sh, w_ref, off, 1) + b_ref[...]
            mu = jnp.mean(u1, axis=-1, keepdims=True)
            var = jnp.mean(jnp.square(u1 - mu), axis=-1, keepdims=True)
            u2 = ((u1 - mu) * lax.rsqrt(var + EPS)) * lg_ref[...] + lb_ref[...]
            out_ref[pl.ds(r0, CONV_CHUNK), :] = (u2 * _sigmoid(u2)).astype(BF16)
            return 0

        lax.fori_loop(0, T // CONV_CHUNK, chunk, 0)

    row = pl.BlockSpec((1, C), lambda i: (0, 0))
    return pl.pallas_call(
        body, name=name, grid=(1,),
        in_specs=[pl.BlockSpec((T, C), lambda i: (0, 0)), pl.BlockSpec((T, C), lambda i: (0, 1)),
                  pl.BlockSpec(w.shape, lambda i: (0, 0)), row, row, row],
        out_specs=pl.BlockSpec((T, C), lambda i: (0, 0)),
        out_shape=jax.ShapeDtypeStruct((T, C), BF16),
        scratch_shapes=[pltpu.VMEM((T + CONV_PAD, C), F32)] + _conv_scratch(C),
        compiler_params=_params(("arbitrary",)),
    )(cproj, cproj, w, b, lg, lb)


def _conv_bwd(cproj, dcat, w, b, lg, lb, name):
    T = cproj.shape[0]
    C = cproj.shape[1] // 2
    off = CONV_PAD - (CONV_K - 1)
    n_chunks = T // CONV_CHUNK

    def fold(v):
        return jnp.sum(v.reshape(CONV_CHUNK // 8, 8, C), axis=0)

    def body(cv_ref, cg_ref, du_ref, w_ref, b_ref, lg_ref, lb_ref, dc_ref, dw_ref, sm_ref,
             upad, dpad, dwacc, smacc, win, sh):
        _glu_into(upad, cv_ref, cg_ref, T)
        dpad[pl.ds(T, CONV_PAD), :] = jnp.zeros((CONV_PAD, C), F32)
        dwacc[...] = jnp.zeros_like(dwacc)
        smacc[...] = jnp.zeros_like(smacc)

        def chunk_a(c, _):
            r0 = pl.multiple_of(c * CONV_CHUNK, CONV_CHUNK)
            _load_window(win, sh, upad, r0)
            u1 = _conv_taps(win, sh, w_ref, off, 1) + b_ref[...]
            mu = jnp.mean(u1, axis=-1, keepdims=True)
            var = jnp.mean(jnp.square(u1 - mu), axis=-1, keepdims=True)
            rstd = lax.rsqrt(var + EPS)
            u1h = (u1 - mu) * rstd
            u2 = u1h * lg_ref[...] + lb_ref[...]
            sg = _sigmoid(u2)
            du2 = du_ref[pl.ds(r0, CONV_CHUNK), :] * (sg * (1.0 + u2 * (1.0 - sg)))
            smacc[8:16, :] += fold(du2 * u1h)
            smacc[16:24, :] += fold(du2)
            du1h = du2 * lg_ref[...]
            du1 = rstd * (du1h - jnp.mean(du1h, axis=-1, keepdims=True)
                          - u1h * jnp.mean(du1h * u1h, axis=-1, keepdims=True))
            smacc[0:8, :] += fold(du1)
            dpad[pl.ds(r0, CONV_CHUNK), :] = du1
            for k in range(CONV_K):
                dwacc[8 * k:8 * k + 8, :] += fold(du1 * _tap(win, sh, off + k))
            return 0

        lax.fori_loop(0, n_chunks, chunk_a, 0)

        def chunk_b(c, _):
            r0 = pl.multiple_of(c * CONV_CHUNK, CONV_CHUNK)
            rs = pl.ds(r0, CONV_CHUNK)
            _load_window(win, sh, dpad, r0)
            du0 = _conv_taps(win, sh, w_ref, CONV_K - 1, -1)
            cv, sg = cv_ref[rs, :], _sigmoid(cg_ref[rs, :])
            dc_ref[rs, 0:C] = (du0 * sg).astype(BF16)
            dc_ref[rs, C:2 * C] = (du0 * cv * (sg * (1.0 - sg))).astype(BF16)
            return 0

        lax.fori_loop(0, n_chunks, chunk_b, 0)
        for k in range(CONV_K):
            dw_ref[k:k + 1, :] = jnp.sum(dwacc[8 * k:8 * k + 8, :], axis=0, keepdims=True)
        dw_ref[CONV_K:CONV_PAD, :] = jnp.zeros((CONV_PAD - CONV_K, C), F32)
        for r in range(3):
            sm_ref[r:r + 1, :] = jnp.sum(smacc[8 * r:8 * r + 8, :], axis=0, keepdims=True)
        sm_ref[3:8, :] = jnp.zeros((5, C), F32)

    row = pl.BlockSpec((1, C), lambda i: (0, 0))
    return pl.pallas_call(
        body, name=name, grid=(1,),
        in_specs=[pl.BlockSpec((T, C), lambda i: (0, 0)), pl.BlockSpec((T, C), lambda i: (0, 1)),
                  pl.BlockSpec((T, C), lambda i: (0, 1)),
                  pl.BlockSpec(w.shape, lambda i: (0, 0)), row, row, row],
        out_specs=[pl.BlockSpec((T, 2 * C), lambda i: (0, 0)), pl.BlockSpec((CONV_PAD, C), lambda i: (0, 0)),
                   pl.BlockSpec((8, C), lambda i: (0, 0))],
        out_shape=(jax.ShapeDtypeStruct((T, 2 * C), BF16), jax.ShapeDtypeStruct((CONV_PAD, C), F32),
                   jax.ShapeDtypeStruct((8, C), F32)),
        scratch_shapes=[pltpu.VMEM((T + CONV_PAD, C), F32), pltpu.VMEM((T + CONV_PAD, C), F32),
                        pltpu.VMEM((8 * CONV_PAD, C), F32), pltpu.VMEM((24, C), F32)] + _conv_scratch(C),
        compiler_params=_params(("arbitrary",)),
    )(cproj, cproj, dcat, w, b, lg, lb)


def _loss_head(x, target, name):
    T, D = x.shape
    tm = _tile(T, 512, 8)

    def body(x_ref, t_ref, loss_ref, dx_ref):
        @pl.when(pl.program_id(0) == 0)
        def _():
            loss_ref[...] = jnp.zeros_like(loss_ref)

        err = x_ref[...] - t_ref[...]
        part = jnp.sum(jnp.mean(err * err, axis=-1, keepdims=True), axis=0, keepdims=True)
        loss_ref[...] += jnp.broadcast_to(0.5 * part, loss_ref.shape)
        dx_ref[...] = err * (1.0 / D)

    big = pl.BlockSpec((tm, D), lambda i: (i, 0))
    return pl.pallas_call(
        body, name=name, grid=(T // tm,),
        in_specs=[big, big],
        out_specs=[pl.BlockSpec((8, LANES), lambda i: (0, 0)), big],
        out_shape=(jax.ShapeDtypeStruct((8, LANES), F32), jax.ShapeDtypeStruct((T, D), F32)),
        compiler_params=_params(("arbitrary",)),
    )(x, target)


def _ada_fwd(c_all, ada_w, ada_b_loc, name):
    L, D, S = ada_w.shape
    B = c_all.shape[0]

    def body(c_ref, w_ref, b_ref, o_ref):
        c = c_ref[...]
        ca = (c * _sigmoid(c)).astype(BF16)
        o_ref[...] = _dot(ca, w_ref[...].astype(BF16), NN) + b_ref[...]

    return pl.pallas_call(
        body, name=name, grid=(L,),
        in_specs=[pl.BlockSpec((B, D), lambda l: (0, 0)), pl.BlockSpec((None, D, S), lambda l: (l, 0, 0)),
                  pl.BlockSpec((None, 1, S), lambda l: (l, 0, 0))],
        out_specs=pl.BlockSpec((None, B, S), lambda l: (l, 0, 0)),
        out_shape=jax.ShapeDtypeStruct((L, B, S), F32),
        compiler_params=_params(("parallel",)),
    )(c_all, ada_w, ada_b_loc)


def _ada_bwd(c_all_t, dmod_loc, name):
    D, B = c_all_t.shape
    L, _, S = dmod_loc.shape

    def body(c_ref, dm_ref, o_ref):
        c = c_ref[...]
        ca = c * _sigmoid(c)
        acc = None
        for bb in range(B):
            t = ca[:, bb:bb + 1] * dm_ref[bb:bb + 1, :]
            acc = t if acc is None else acc + t
        o_ref[...] = acc

    return pl.pallas_call(
        body, name=name, grid=(L,),
        in_specs=[pl.BlockSpec((D, B), lambda l: (0, 0)), pl.BlockSpec((None, B, S), lambda l: (l, 0, 0))],
        out_specs=pl.BlockSpec((None, D, S), lambda l: (l, 0, 0)),
        out_shape=jax.ShapeDtypeStruct((L, D, S), F32),
        compiler_params=_params(("parallel",)),
    )(c_all_t, dmod_loc)


def _sum_devices(parts, name):
    _, R, C = parts.shape
    tm = _tile(R, 256, 8)

    def body(p_ref, o_ref):
        acc = p_ref[0].astype(F32)
        for d in range(1, N_DEV):
            acc = acc + p_ref[d].astype(F32)
        o_ref[...] = acc

    return pl.pallas_call(
        body, name=name, grid=(R // tm,),
        in_specs=[pl.BlockSpec((N_DEV, tm, C), lambda i: (0, i, 0))],
        out_specs=pl.BlockSpec((tm, C), lambda i: (i, 0)),
        out_shape=jax.ShapeDtypeStruct((R, C), F32),
        compiler_params=_params(("parallel",)),
    )(parts)


def _adamw_math(w, g, m, v):
    m = ADAM_B1 * m + (1.0 - ADAM_B1) * g
    v = ADAM_B2 * v + (1.0 - ADAM_B2) * (g * g)
    m_hat = m / (1.0 - ADAM_B1 ** ADAM_STEP)
    v_hat = v / (1.0 - ADAM_B2 ** ADAM_STEP)
    delta = -ADAM_LR * (m_hat / (jnp.sqrt(v_hat) + ADAM_EPS) + ADAM_WD * w)
    return delta, m, v


def _adamw(w, g, m, v, name, summed):
    R, C = w.shape
    tm = _tile(R, 256, 16)
    n_parts = g.shape[0] if summed else 0

    def body(w_ref, g_ref, m_ref, v_ref, go_ref, d_ref, mo_ref, vo_ref):
        if summed:
            g = g_ref[0].astype(F32)
            for d in range(1, n_parts):
                g = g + g_ref[d].astype(F32)
        else:
            g = g_ref[...]
        delta, mn, vn = _adamw_math(w_ref[...], g, m_ref[...], v_ref[...])
        go_ref[...] = g
        d_ref[...] = delta
        mo_ref[...] = mn
        vo_ref[...] = vn

    big = pl.BlockSpec((tm, C), lambda i: (i, 0))
    gspec = pl.BlockSpec((n_parts, tm, C), lambda i: (0, i, 0)) if summed else big
    return pl.pallas_call(
        body, name=name, grid=(R // tm,),
        in_specs=[big, gspec, big, big],
        out_specs=[big, big, big, big],
        out_shape=tuple(jax.ShapeDtypeStruct((R, C), F32) for _ in range(4)),
        compiler_params=_params(("parallel",)),
    )(w, g, m, v)


def _adamw_summed(w, parts, m, v, name):
    L = len(parts)
    n_parts, R, C = parts[0].shape
    tm = _tile(R, 256, 16)
    nr = R // tm

    def body(*refs):
        w_ref, g_refs = refs[0], refs[1:1 + L]
        m_ref, v_ref, go_ref, d_ref, mo_ref, vo_ref = refs[1 + L:]
        for ll in range(L):
            @pl.when(pl.program_id(0) == ll)
            def _(ll=ll):
                g = g_refs[ll][0].astype(F32)
                for d in range(1, n_parts):
                    g = g + g_refs[ll][d].astype(F32)
                delta, mn, vn = _adamw_math(w_ref[...], g, m_ref[...], v_ref[...])
                go_ref[...] = g
                d_ref[...] = delta
                mo_ref[...] = mn
                vo_ref[...] = vn

    big = pl.BlockSpec((tm, C), lambda l, i: (l * nr + i, 0))
    gspecs = [pl.BlockSpec((n_parts, tm, C), functools.partial(lambda l, i, ll: (0, jnp.where(l == ll, i, 0), 0), ll=ll))
              for ll in range(L)]
    return pl.pallas_call(
        body, name=name, grid=(L, nr),
        in_specs=[big, *gspecs, big, big],
        out_specs=[big, big, big, big],
        out_shape=tuple(jax.ShapeDtypeStruct((L * R, C), F32) for _ in range(4)),
        compiler_params=_params(("arbitrary", "arbitrary")),
    )(w, *parts, m, v)


def _pack(arrs, D):
    L = arrs[0].shape[0]
    cols = []
    for a in arrs:
        f = a.reshape(L, -1)
        n = f.shape[1]
        cols.append(jnp.pad(f, ((0, 0), (0, -(-n // D) * D - n))))
    flat = jnp.concatenate(cols, axis=1)
    return flat.reshape(L, flat.shape[1] // D, D)


def _unpack(p, shapes, D):
    L = p.shape[0]
    out, r = [], 0
    for s in shapes:
        n = math.prod(s[1:])
        rows = -(-n // D)
        out.append(p[:, r:r + rows].reshape(L, rows * D)[:, :n].reshape(s))
        r += rows
    return out


def kernel(x, c, w_in, b_f, conv_w, conv_b, conv_ln_g, conv_ln_b, w_o, w_ffn_in, w_ffn_out, mix_pre_g, mix_post_g, ffn_pre_g, ffn_post_g, ada_w, ada_b, loss_target, m_w_in, m_b_f, m_conv_w, m_conv_b, m_conv_ln_g, m_conv_ln_b, m_w_o, m_w_ffn_in, m_w_ffn_out, m_mix_pre_g, m_mix_post_g, m_ffn_pre_g, m_ffn_post_g, m_ada_w, m_ada_b, v_w_in, v_b_f, v_conv_w, v_conv_b, v_conv_ln_g, v_conv_ln_b, v_w_o, v_w_ffn_in, v_w_ffn_out, v_mix_pre_g, v_mix_post_g, v_ffn_pre_g, v_ffn_post_g, v_ada_w, v_ada_b):
    L, D, s_in = w_in.shape
    T = x.shape[1]
    H = b_f.shape[1]
    A = D // 2
    C = D - A
    cs = conv_w.shape[2]
    F = w_ffn_out.shape[1] * N_DEV
    s_ff = w_ffn_in.shape[2]
    s_ada = ada_w.shape[2]
    R = 16
    me = _my_index()
    x0 = x[0]
    target = loss_target[0]
    tq = _tile(T, 512)
    nq = T // tq

    c_all = _exchange(c, gather=True, name="gather_c").reshape(N_DEV, D)
    ada_b_loc = lax.dynamic_slice_in_dim(ada_b, me * s_ada, s_ada, axis=1)[:, None, :]
    mod_loc = _ada_fwd(c_all, ada_w, ada_b_loc, "ada_fwd")
    mod_g = _exchange(mod_loc, gather=True, name="gather_mod")
    mod = lax.dynamic_index_in_dim(mod_g, me, axis=2, keepdims=False)
    mod = jnp.transpose(mod, (1, 0, 2)).reshape(L, N_MOD, 1, D)
    cw_g = _exchange(conv_w, gather=True, name="gather_conv_w")
    conv_w_full = jnp.transpose(cw_g, (1, 2, 0, 3)).reshape(L, CONV_K, C)
    conv_w_pad = jnp.pad(conv_w_full, ((0, 0), (0, CONV_PAD - CONV_K), (0, 0)))
    b_f_col = jnp.pad(b_f, ((0, 0), (0, R - H)))[:, :, None]

    h_gmix, h_gffn, chain = [], [], (mod_g, cw_g)
    for l in range(L):
        h_gmix.append(_gather_start([w_in[l].astype(BF16), w_o[l].astype(BF16)], f"gather_mix_{l}", chain))
        h_gffn.append(_gather_start([w_ffn_in[l].astype(BF16), w_ffn_out[l].astype(BF16)], f"gather_ffn_{l}",
                                    (h_gmix[l]["token"],)))
        chain = (h_gffn[l]["token"],)
    gather_tokens = chain

    def by_cols(g):
        return jnp.transpose(g, (1, 0, 2)).reshape(g.shape[1], N_DEV * g.shape[2])

    def by_rows(g):
        return g.reshape(N_DEV * g.shape[1], g.shape[2])

    W_qkv, W_f_t, W_f_pad, W_c, W_o, W_ffn_in, W_ffn_out = ([None] * L for _ in range(7))

    saved = []
    xc = x0
    for l in range(L):
        sh1, sc1, g1, sh2, sc2, g2 = (mod[l, k] for k in range(N_MOD))
        gpre1, gpost1, gpre2, gpost2 = (p[l][None, :] for p in (mix_pre_g, mix_post_g, ffn_pre_g, ffn_post_g))
        dep = gather_tokens[0] if l == 0 else xc
        g_in, g_o = _gather_wait(_gather_mid(h_gmix[l], dep, f"gather_mid_mix_{l}"), dep, f"gather_wait_mix_{l}")
        W_in_l = by_cols(g_in)
        W_qkv[l], W_f, W_c[l], W_o[l] = W_in_l[:, :3 * A], W_in_l[:, 3 * A:3 * A + H], W_in_l[:, 3 * A + H:], by_rows(g_o)
        W_f_t[l] = jnp.pad(jnp.transpose(W_f), ((0, R - H), (0, 0)))
        W_f_pad[l] = jnp.pad(W_f, ((0, 0), (0, R - H)))
        h1 = _prenorm(xc, gpre1, sc1, sh1, f"prenorm1_{l}", first=gather_tokens if l == 0 else ())
        qkv = _mm([(h1, W_qkv[l])], "nn", BF16, f"proj_qkv_{l}")
        cproj = _mm([(h1, W_c[l])], "nn", F32, f"proj_conv_{l}")
        fl, cum = _fgate_fwd(h1, W_f_t[l], b_f_col[l], f"fgate_{l}")
        cum_rows = cum[:H].reshape(H * nq, tq)
        o, lse = _attn_fwd(qkv, cum_rows, H, f"attn_{l}", tq)
        h_ffn_l = _gather_mid(h_gffn[l], o, f"gather_mid_ffn_{l}")
        u3 = _conv_fwd(cproj, conv_w_pad[l], conv_b[l][None, :], conv_ln_g[l][None, :], conv_ln_b[l][None, :], f"conv_{l}")
        cat = jnp.concatenate([o, u3], axis=-1)
        y1, x_mid = _mm_postnorm(cat, W_o[l], xc, g1, gpost1, f"out_proj_{l}")
        g_fi, g_fo = _gather_wait(h_ffn_l, x_mid, f"gather_wait_ffn_{l}")
        W_ffn_in[l], W_ffn_out[l] = by_cols(g_fi), by_rows(g_fo)
        h2 = _prenorm(x_mid, gpre2, sc2, sh2, f"prenorm2_{l}")
        g, u, act = _ffn_in_fwd(h2, W_ffn_in[l], f"ffn_in_{l}")
        y2, x_out = _mm_postnorm(act, W_ffn_out[l], x_mid, g2, gpost2, f"ffn_out_{l}")
        saved.append((xc, h1, qkv, cproj, fl, cum_rows, o, lse, cat, y1, x_mid, h2, g, u, act, y2))
        xc = x_out

    loss_tile, dx = _loss_head(xc, target, "loss_head")
    loss = lax.psum(loss_tile[0, 0], ("x", "y", "c"))

    mc = lax.axis_index("c")

    def reduce_start(parts, name):
        keeps, sends = [], []
        for part in parts:
            part4 = part.reshape(4, 2, *part.shape[1:])
            keeps.append(lax.dynamic_index_in_dim(part4, mc, axis=1, keepdims=False))
            sends.append(lax.dynamic_index_in_dim(part4, 1 - mc, axis=1, keepdims=False))
        gots = _swap_sibling(sends, "swap_" + name)
        boths = [_pair_sum(k.reshape(-1, k.shape[-1]), t.reshape(-1, t.shape[-1]), f"pairsum{a}_{name}").reshape(k.shape)
                 for a, (k, t) in enumerate(zip(keeps, gots))]
        return _chips_start(boths, "scatter_" + name)

    small, h_ffn, h_mix = [None] * L, [None] * L, [None] * L
    for l in reversed(range(L)):
        xin, h1, qkv, cproj, fl, cum_rows, o, lse, cat, y1, x_mid, h2, g, u, act, y2 = saved[l]
        sh1, sc1, g1, sh2, sc2, g2 = (mod[l, k] for k in range(N_MOD))
        gpre1, gpost1, gpre2, gpost2 = (p[l][None, :] for p in (mix_pre_g, mix_post_g, ffn_pre_g, ffn_post_g))
        dy2, sm_post2 = _postnorm_bwd(dx, y2, g2, gpost2, f"postnorm2_bwd_{l}")
        dgate, dup = _ffn_act_bwd(dy2, W_ffn_out[l], g, u, f"ffn_act_bwd_{l}")
        dW_ffn_out = _mm([(act, dy2)], "tn", BF16, f"dw_ffn_out_{l}", tm=FFN_TILE)
        dh2 = _mm([(dgate, W_ffn_in[l], 0), (dup, W_ffn_in[l], 1)], "nt", F32, f"dh2_{l}", tn=MM_ROWS_SMALL)
        dWg = _mm([(h2, dgate)], "tn", BF16, f"dw_ffn_gate_{l}", tn=FFN_TILE)
        dWu = _mm([(h2, dup)], "tn", BF16, f"dw_ffn_up_{l}", tn=FFN_TILE)
        dW_ffn_in = jnp.concatenate([dWg, dWu], axis=1)
        h_ffn[l], token = reduce_start([jnp.transpose(dW_ffn_in.reshape(D, N_DEV, s_ff), (1, 0, 2)),
                                        dW_ffn_out.reshape(N_DEV, F // N_DEV, D)], f"ffn_{l}")
        dx_mid, sm_pre2 = _prenorm_bwd(dh2, x_mid, dx, gpre2, sc2, f"prenorm2_bwd_{l}", first=(token,))
        dy1, sm_post1 = _postnorm_bwd(dx_mid, y1, g1, gpost1, f"postnorm1_bwd_{l}")
        dcat = _mm([(dy1, W_o[l])], "nt", F32, f"dcat_{l}")
        dW_o = _mm([(cat, dy1)], "tn", BF16, f"dw_o_{l}")
        dqkv, dcum_rows, dcum_q = _attn_bwd(qkv, dcat, o, lse, cum_rows, H, f"attn_bwd_{l}", tq)
        dcproj, dconv_w, sm_conv = _conv_bwd(cproj, dcat, conv_w_pad[l], conv_b[l][None, :], conv_ln_g[l][None, :],
                                             conv_ln_b[l][None, :], f"conv_bwd_{l}")
        dcum = dcum_rows.reshape(H, T) + jnp.transpose(dcum_q, (0, 2, 1)).reshape(H, T)
        dcum = jnp.pad(dcum, ((0, R - H), (0, 0)))
        dfl_t, dwf_t, dbf = _fgate_bwd(dcum, fl, h1, f"fgate_bwd_{l}")
        dfl = jnp.transpose(dfl_t).astype(BF16)
        dh1 = _mm([(dqkv[0], W_qkv[l], 0), (dqkv[1], W_qkv[l], 1), (dqkv[2], W_qkv[l], 2),
                   (dfl, W_f_pad[l]), (dcproj, W_c[l])], "nt", F32, f"dh1_{l}")
        dWq = [_mm([(h1, dqkv[k])], "tn", BF16, f"dw_qkv{k}_{l}") for k in range(3)]
        dWc = _mm([(h1, dcproj)], "tn", BF16, f"dw_conv_{l}")
        dWf = jnp.transpose(dwf_t[:H]).astype(BF16)
        dW_in = jnp.concatenate(dWq + [dWf, dWc], axis=1)
        h_mix[l], token = reduce_start([jnp.transpose(dW_in.reshape(D, N_DEV, s_in), (1, 0, 2)),
                                        dW_o.reshape(N_DEV, D // N_DEV, D)], f"mix_{l}")
        dx, sm_pre1 = _prenorm_bwd(dh1, xin, dx_mid, gpre1, sc1, f"prenorm1_bwd_{l}", first=(token,))
        dmod = jnp.stack([sm_pre1[0], sm_pre1[1], sm_post1[0], sm_pre2[0], sm_pre2[1], sm_post2[0]])
        small[l] = (dmod, sm_pre1[2], sm_post1[1], sm_pre2[2], sm_post2[1], sm_conv[0], sm_conv[1], sm_conv[2],
                    dbf[:H, 0], dconv_w[:CONV_K])
    grad_x = dx[None]

    small_names = 10
    small_l = [jnp.stack([small[l][k] for l in range(L)]) for k in range(small_names)]
    small_shapes = [a.shape for a in small_l]
    packed = _pack(small_l, D)
    rows = packed.shape[1]
    rows_pad = -(-L * rows // 8) * 8
    packed2 = jnp.pad(packed.reshape(L * rows, D), ((0, rows_pad - L * rows), (0, 0)))
    small_g = _exchange(packed2, gather=True, name="gather_small")
    small_sum = _sum_devices(small_g, "sum_small")[:L * rows].reshape(L, rows, D)
    (g_ada_b6, g_mix_pre, g_mix_post, g_ffn_pre, g_ffn_post, g_conv_b, g_ln_g, g_ln_b, g_b_f,
     g_conv_w_full) = _unpack(small_sum, small_shapes, D)
    g_ada_b = g_ada_b6.reshape(L, N_MOD * D)
    g_conv_w = lax.dynamic_slice_in_dim(g_conv_w_full, me * cs, cs, axis=2)
    dmod_all = small_g[:, :L * rows].reshape(N_DEV, L, rows, D)[:, :, :N_MOD].reshape(N_DEV, L, N_MOD * D)
    dmod_loc = jnp.transpose(lax.dynamic_slice_in_dim(dmod_all, me * s_ada, s_ada, axis=2), (1, 0, 2))
    g_ada_w = _ada_bwd(jnp.transpose(c_all), dmod_loc, "ada_bwd")

    got_ffn = [_chips_wait(h_ffn[l], dx, f"scatter_wait_ffn_{l}") for l in range(L)]
    got_mix = [_chips_wait(h_mix[l], dx, f"scatter_wait_mix_{l}") for l in range(L)]

    def step(w, m, v, parts, name):
        two_d = lambda t: t.reshape(-1, t.shape[-1])
        outs = _adamw_summed(two_d(w), parts, two_d(m), two_d(v), "adamw_" + name)
        return [t.reshape(w.shape) for t in outs]

    r_w_ffn_in = step(w_ffn_in, m_w_ffn_in, v_w_ffn_in, [got_ffn[l][0] for l in range(L)], "w_ffn_in")
    r_w_ffn_out = step(w_ffn_out, m_w_ffn_out, v_w_ffn_out, [got_ffn[l][1] for l in range(L)], "w_ffn_out")
    r_w_in = step(w_in, m_w_in, v_w_in, [got_mix[l][0] for l in range(L)], "w_in")
    r_w_o = step(w_o, m_w_o, v_w_o, [got_mix[l][1] for l in range(L)], "w_o")
    r_ada_w = [t.reshape(ada_w.shape) for t in _adamw(
        ada_w.reshape(L * D, s_ada), g_ada_w.reshape(L * D, s_ada), m_ada_w.reshape(L * D, s_ada),
        v_ada_w.reshape(L * D, s_ada), "adamw_ada_w", False)]

    sw = [b_f, conv_w, conv_b, conv_ln_g, conv_ln_b, mix_pre_g, mix_post_g, ffn_pre_g, ffn_post_g, ada_b]
    sg = [g_b_f, g_conv_w, g_conv_b, g_ln_g, g_ln_b, g_mix_pre, g_mix_post, g_ffn_pre, g_ffn_post, g_ada_b]
    sm = [m_b_f, m_conv_w, m_conv_b, m_conv_ln_g, m_conv_ln_b, m_mix_pre_g, m_mix_post_g, m_ffn_pre_g, m_ffn_post_g, m_ada_b]
    sv = [v_b_f, v_conv_w, v_conv_b, v_conv_ln_g, v_conv_ln_b, v_mix_pre_g, v_mix_post_g, v_ffn_pre_g, v_ffn_post_g, v_ada_b]
    shapes = [a.shape for a in sw]

    def flat(arrs):
        p = _pack(arrs, D)
        n = p.shape[0] * p.shape[1]
        return jnp.pad(p.reshape(n, D), ((0, -(-n // 8) * 8 - n), (0, 0))), p.shape

    pw, pshape = flat(sw)
    pg, pm, pv = flat(sg)[0], flat(sm)[0], flat(sv)[0]
    s_outs = _adamw(pw, pg, pm, pv, "adamw_small", False)
    n_small = pshape[0] * pshape[1]
    s_g, s_d, s_m, s_v = (_unpack(t[:n_small].reshape(pshape), shapes, D) for t in s_outs)

    big = {"w_in": r_w_in, "w_o": r_w_o, "w_ffn_in": r_w_ffn_in, "w_ffn_out": r_w_ffn_out, "ada_w": r_ada_w}
    order = ["w_in", "b_f", "conv_w", "conv_b", "conv_ln_g", "conv_ln_b", "w_o", "w_ffn_in", "w_ffn_out",
             "mix_pre_g", "mix_post_g", "ffn_pre_g", "ffn_post_g", "ada_w", "ada_b"]
    small_pos = {n: i for i, n in enumerate(["b_f", "conv_w", "conv_b", "conv_ln_g", "conv_ln_b", "mix_pre_g",
                                             "mix_post_g", "ffn_pre_g", "ffn_post_g", "ada_b"])}

    def pick(n, k):
        if n in big:
            return big[n][k]
        return (s_g, s_d, s_m, s_v)[k][small_pos[n]]

    return (loss, grad_x, *[pick(n, 0) for n in order], *[pick(n, 1) for n in order],
            *[pick(n, 2) for n in order], *[pick(n, 3) for n in order])
```

```python
import functools
import math

import jax
import jax.numpy as jnp
from jax import lax
from jax.experimental import pallas as pl
from jax.experimental.pallas import tpu as pltpu

F32 = jnp.float32
BF16 = jnp.bfloat16
MESH = pl.DeviceIdType.MESH
N_DEV = 8
EPS = 1e-6
CONV_K = 31
CONV_PAD = 32
CONV_CHUNK = 128
N_MOD = 6
NEG = -1e30
LANES = 128
VMEM_LIMIT = 56 * 2**20
MM_TILE = 1024
MM_ROWS_SMALL = 512
FFN_TILE = 1408
ADAM_LR, ADAM_B1, ADAM_B2, ADAM_EPS, ADAM_WD, ADAM_STEP = 0.001, 0.9, 0.999, 1e-08, 0.01, 10

NN = (((1,), (0,)), ((), ()))
NT = (((1,), (1,)), ((), ()))
TN = (((0,), (0,)), ((), ()))


def _dot(a, b, dims):
    return lax.dot_general(a, b, dims, preferred_element_type=F32)


def _tile(n, pref, align=LANES):
    if n <= pref:
        return n
    t = (pref // align) * align
    while t >= align:
        if n % t == 0:
            return t
        t -= align
    return n


def _params(sem=None):
    return pltpu.CompilerParams(dimension_semantics=sem, vmem_limit_bytes=VMEM_LIMIT)


def _sigmoid(x):
    return 1.0 / (1.0 + jnp.exp(-x))


def _my_index():
    return 4 * lax.axis_index("x") + 2 * lax.axis_index("y") + lax.axis_index("c")


def _exchange(x, *, gather, name):
    blk = x.shape if gather else x.shape[1:]

    def body(x_ref, y_ref, send_sems, recv_sems, local_sem):
        mx, my, mc = lax.axis_index("x"), lax.axis_index("y"), lax.axis_index("c")
        me = 4 * mx + 2 * my + mc

        def src(p):
            return x_ref if gather else x_ref.at[p]

        mine = pltpu.make_async_copy(src(me), y_ref.at[me], local_sem)
        mine.start()
        copies = []
        for k in range(1, N_DEV):
            px = (1 - mx) if (k >> 2) & 1 else mx
            py = (1 - my) if (k >> 1) & 1 else my
            pc = (1 - mc) if k & 1 else mc
            cp = pltpu.make_async_remote_copy(
                src_ref=src(4 * px + 2 * py + pc), dst_ref=y_ref.at[me],
                send_sem=send_sems.at[k - 1], recv_sem=recv_sems.at[k - 1],
                device_id=(px, py, pc), device_id_type=MESH)
            cp.start()
            copies.append(cp)
        for cp in copies:
            cp.wait()
        mine.wait()

    return pl.pallas_call(
        body, name=name,
        out_shape=jax.ShapeDtypeStruct((N_DEV,) + tuple(blk), x.dtype),
        in_specs=[pl.BlockSpec(memory_space=pl.ANY)],
        out_specs=pl.BlockSpec(memory_space=pl.ANY),
        scratch_shapes=[pltpu.SemaphoreType.DMA((N_DEV - 1,)), pltpu.SemaphoreType.DMA((N_DEV - 1,)),
                        pltpu.SemaphoreType.DMA(())],
    )(x)


_HBM = pl.BlockSpec(memory_space=pl.ANY)


def _swap_sibling(xs, name):
    n = len(xs)

    def body(*refs):
        send_sems, recv_sems = refs[2 * n], refs[2 * n + 1]
        sib = (lax.axis_index("x"), lax.axis_index("y"), 1 - lax.axis_index("c"))
        copies = [pltpu.make_async_remote_copy(src_ref=refs[a], dst_ref=refs[n + a], send_sem=send_sems.at[a],
                                               recv_sem=recv_sems.at[a], device_id=sib, device_id_type=MESH)
                  for a in range(n)]
        for cp in copies:
            cp.start()
        for cp in copies:
            cp.wait()

    return pl.pallas_call(
        body, name=name, out_shape=[jax.ShapeDtypeStruct(x.shape, x.dtype) for x in xs],
        in_specs=[_HBM] * n, out_specs=[_HBM] * n,
        scratch_shapes=[pltpu.SemaphoreType.DMA((n,)), pltpu.SemaphoreType.DMA((n,))],
    )(*xs)


_SEM = pl.BlockSpec(memory_space=pltpu.SEMAPHORE)
_HBM_SPEC = pl.BlockSpec(memory_space=pltpu.HBM)
_EFFECT = pltpu.CompilerParams(has_side_effects=pltpu.SideEffectType.DATAFLOW_SIDE_EFFECTING)


def _in_hbm(a):
    return pltpu.with_memory_space_constraint(a, pltpu.HBM)


def _chip_copies(x_refs, land_refs, send_sems, recv_sems, loc_sems):
    mx, my, mc = lax.axis_index("x"), lax.axis_index("y"), lax.axis_index("c")
    here = 2 * mx + my
    local, remote = [], []
    for a, (x_ref, land_ref) in enumerate(zip(x_refs, land_refs)):
        local.append(pltpu.make_async_copy(x_ref.at[here], land_ref.at[here], loc_sems.at[a]))
        for j, (px, py) in enumerate([(1 - mx, my), (mx, 1 - my), (1 - mx, 1 - my)]):
            remote.append(pltpu.make_async_remote_copy(
                src_ref=x_ref.at[2 * px + py], dst_ref=land_ref.at[here], send_sem=send_sems.at[3 * a + j],
                recv_sem=recv_sems.at[3 * a + j], device_id=(px, py, mc), device_id_type=MESH))
    return local, remote


def _chips_start(xs, name):
    n = len(xs)

    def body(*refs):
        local, remote = _chip_copies(refs[:n], refs[n:2 * n], *refs[2 * n:2 * n + 3])
        for cp in local + remote:
            cp.start()
        refs[-1][...] = jnp.zeros_like(refs[-1])

    lands = [lax.empty(x.shape, x.dtype) for x in xs]
    outs = pl.pallas_call(
        body, name=name,
        out_shape=(pltpu.SemaphoreType.DMA((3 * n,)), pltpu.SemaphoreType.DMA((3 * n,)), pltpu.SemaphoreType.DMA((n,)),
                   *[pltpu.HBM(x.shape, x.dtype) for x in xs], *[pltpu.HBM(x.shape, x.dtype) for x in xs],
                   jax.ShapeDtypeStruct((8, LANES), F32)),
        in_specs=[_HBM_SPEC] * (2 * n),
        out_specs=(_SEM, _SEM, _SEM, *[_HBM_SPEC] * (2 * n), pl.BlockSpec(memory_space=pltpu.VMEM)),
        input_output_aliases={i: 3 + i for i in range(2 * n)},
        compiler_params=_EFFECT,
    )(*[_in_hbm(x) for x in xs], *[_in_hbm(t) for t in lands])
    return (outs[0], outs[1], outs[2], list(outs[3:3 + n]), list(outs[3 + n:3 + 2 * n])), outs[-1]


def _chips_wait(handle, after, name):
    send_sems, recv_sems, loc_sems, x_thru, land_thru = handle
    n = len(x_thru)

    def body(*refs):
        local, remote = _chip_copies(refs[:n], refs[n:2 * n], *refs[2 * n:2 * n + 3])
        for cp in local:
            cp.wait()
        for cp in remote:
            cp.wait_send()
            cp.wait_recv()

    outs = pl.pallas_call(
        body, name=name,
        out_shape=[pltpu.HBM(x.shape, x.dtype) for x in x_thru + land_thru],
        in_specs=[_HBM_SPEC] * (2 * n) + [_SEM, _SEM, _SEM, _HBM],
        out_specs=[_HBM_SPEC] * (2 * n),
        input_output_aliases={i: i for i in range(2 * n)},
        compiler_params=_EFFECT,
    )(*x_thru, *land_thru, send_sems, recv_sems, loc_sems, after)
    return list(outs[n:])


def _gather_first_copies(x_refs, y_refs, send_sems, sib_recv, ici_recv, loc_sems):
    mx, my, mc = lax.axis_index("x"), lax.axis_index("y"), lax.axis_index("c")
    me = 4 * mx + 2 * my + mc
    local, remote = [], []
    for a, (x_ref, y_ref) in enumerate(zip(x_refs, y_refs)):
        local.append(pltpu.make_async_copy(x_ref, y_ref.at[me], loc_sems.at[a]))
        remote.append(pltpu.make_async_remote_copy(
            src_ref=x_ref, dst_ref=y_ref.at[me], send_sem=send_sems.at[4 * a], recv_sem=sib_recv.at[a],
            device_id=(mx, my, 1 - mc), device_id_type=MESH))
        for j, (px, py) in enumerate([(1 - mx, my), (mx, 1 - my), (1 - mx, 1 - my)]):
            remote.append(pltpu.make_async_remote_copy(
                src_ref=x_ref, dst_ref=y_ref.at[me], send_sem=send_sems.at[4 * a + 1 + j], recv_sem=ici_recv.at[3 * a + j],
                device_id=(px, py, mc), device_id_type=MESH))
    return local, remote


def _gather_forward_copies(y_refs, ici_recv, fwd_send, fwd_recv):
    mx, my, mc = lax.axis_index("x"), lax.axis_index("y"), lax.axis_index("c")
    pairs = []
    for a, y_ref in enumerate(y_refs):
        for j, (px, py) in enumerate([(1 - mx, my), (mx, 1 - my), (1 - mx, 1 - my)]):
            slot = y_ref.at[4 * px + 2 * py + mc]
            arrival = pltpu.make_async_remote_copy(
                src_ref=slot, dst_ref=slot, send_sem=fwd_send.at[3 * a + j], recv_sem=ici_recv.at[3 * a + j],
                device_id=(px, py, mc), device_id_type=MESH)
            onward = pltpu.make_async_remote_copy(
                src_ref=slot, dst_ref=slot, send_sem=fwd_send.at[3 * a + j], recv_sem=fwd_recv.at[3 * a + j],
                device_id=(mx, my, 1 - mc), device_id_type=MESH)
            pairs.append((arrival, onward))
    return pairs


def _gather_start(xs, name, after=()):
    n = len(xs)
    ni = 2 * n + len(after)

    def body(*refs):
        local, remote = _gather_first_copies(refs[:n], refs[n:2 * n], *refs[ni:ni + 4])
        for cp in local + remote:
            cp.start()
        refs[-1][...] = jnp.zeros_like(refs[-1])

    ys = [lax.empty((N_DEV,) + tuple(x.shape), x.dtype) for x in xs]
    dma = pltpu.SemaphoreType.DMA
    outs = pl.pallas_call(
        body, name=name,
        out_shape=(dma((4 * n,)), dma((n,)), dma((3 * n,)), dma((n,)),
                   *[pltpu.HBM(x.shape, x.dtype) for x in xs], *[pltpu.HBM(y.shape, y.dtype) for y in ys],
                   jax.ShapeDtypeStruct((8, LANES), F32)),
        in_specs=[_HBM_SPEC] * (2 * n) + [_HBM] * len(after),
        out_specs=(_SEM, _SEM, _SEM, _SEM, *[_HBM_SPEC] * (2 * n), pl.BlockSpec(memory_space=pltpu.VMEM)),
        input_output_aliases={i: 4 + i for i in range(2 * n)},
        compiler_params=_EFFECT,
    )(*[_in_hbm(x) for x in xs], *[_in_hbm(y) for y in ys], *after)
    return dict(send=outs[0], sib_recv=outs[1], ici_recv=outs[2], loc=outs[3], x=list(outs[4:4 + n]),
                y=list(outs[4 + n:4 + 2 * n]), token=outs[-1])


def _gather_mid(h, after, name):
    n = len(h["y"])

    def body(*refs):
        for arrival, onward in _gather_forward_copies(refs[:n], refs[n], refs[n + 2 + n], refs[n + 3 + n]):
            arrival.wait_recv()
            onward.start()

    dma = pltpu.SemaphoreType.DMA
    outs = pl.pallas_call(
        body, name=name,
        out_shape=(*[pltpu.HBM(y.shape, y.dtype) for y in h["y"]], dma((3 * n,)), dma((3 * n,))),
        in_specs=[_HBM_SPEC] * n + [_SEM, _HBM],
        out_specs=(*[_HBM_SPEC] * n, _SEM, _SEM),
        input_output_aliases={i: i for i in range(n)},
        compiler_params=_EFFECT,
    )(*h["y"], h["ici_recv"], after)
    return dict(h, y=list(outs[:n]), fwd_send=outs[n], fwd_recv=outs[n + 1])


def _gather_wait(h, after, name):
    n = len(h["y"])

    def body(*refs):
        x_refs, y_refs = refs[:n], refs[n:2 * n]
        send, sib_recv, loc, fwd_send, fwd_recv = refs[2 * n:2 * n + 5]
        local, remote = _gather_first_copies(x_refs, y_refs, send, sib_recv, fwd_recv, loc)
        for cp in local:
            cp.wait()
        for k, cp in enumerate(remote):
            cp.wait_send()
            if k % 4 == 0:
                cp.wait_recv()
        for _, onward in _gather_forward_copies(y_refs, fwd_recv, fwd_send, fwd_recv):
            onward.wait_send()
            onward.wait_recv()

    outs = pl.pallas_call(
        body, name=name,
        out_shape=[pltpu.HBM(t.shape, t.dtype) for t in h["x"] + h["y"]],
        in_specs=[_HBM_SPEC] * (2 * n) + [_SEM] * 5 + [_HBM],
        out_specs=[_HBM_SPEC] * (2 * n),
        input_output_aliases={i: i for i in range(2 * n)},
        compiler_params=_EFFECT,
    )(*h["x"], *h["y"], h["send"], h["sib_recv"], h["loc"], h["fwd_send"], h["fwd_recv"], after)
    return list(outs[n:])


def _pair_sum(a, b, name):
    R, C = a.shape
    tm = _tile(R, 512, 16)

    def body(a_ref, b_ref, o_ref):
        o_ref[...] = (a_ref[...].astype(F32) + b_ref[...].astype(F32)).astype(BF16)

    big = pl.BlockSpec((tm, C), lambda i: (i, 0))
    return pl.pallas_call(
        body, name=name, grid=(R // tm,), in_specs=[big, big], out_specs=big,
        out_shape=jax.ShapeDtypeStruct((R, C), BF16), compiler_params=_params(("parallel",)),
    )(a, b)


def _mm(pairs, mode, out_dtype, name, tm=MM_TILE, tn=MM_TILE):
    dims = {"nn": NN, "nt": NT, "tn": TN}[mode]
    a0, b0 = pairs[0][0], pairs[0][1]
    M = a0.shape[1] if mode == "tn" else a0.shape[0]
    N = b0.shape[0] if mode == "nt" else b0.shape[1]
    tm, tn = _tile(M, tm), _tile(N, tn)
    in_specs, args = [], []
    for pr in pairs:
        a, b = pr[0], pr[1]
        if mode == "tn":
            K = a.shape[0]
            in_specs.append(pl.BlockSpec((K, tm), lambda i, j: (0, i)))
            in_specs.append(pl.BlockSpec((K, tn), lambda i, j: (0, j)))
        elif mode == "nn":
            K = a.shape[1]
            rb = pr[2] if len(pr) > 2 else 0
            in_specs.append(pl.BlockSpec((tm, K), lambda i, j: (i, 0)))
            in_specs.append(pl.BlockSpec((K, tn), functools.partial(lambda i, j, rb: (rb, j), rb=rb)))
        else:
            K = a.shape[1]
            cb = pr[2] if len(pr) > 2 else 0
            in_specs.append(pl.BlockSpec((tm, K), lambda i, j: (i, 0)))
            in_specs.append(pl.BlockSpec((tn, K), functools.partial(lambda i, j, cb: (j, cb), cb=cb)))
        args += [a, b]
    n_pairs = len(pairs)

    def body(*refs):
        o_ref = refs[-1]
        acc = None
        for k in range(n_pairs):
            d = _dot(refs[2 * k][...], refs[2 * k + 1][...], dims)
            acc = d if acc is None else acc + d
        o_ref[...] = acc.astype(o_ref.dtype)

    return pl.pallas_call(
        body, name=name, grid=(M // tm, N // tn), in_specs=in_specs,
        out_specs=pl.BlockSpec((tm, tn), lambda i, j: (i, j)),
        out_shape=jax.ShapeDtypeStruct((M, N), out_dtype),
        compiler_params=_params(("parallel", "arbitrary")),
    )(*args)


def _prenorm(x, g, sc, sh, name, first=()):
    T, D = x.shape
    tm = _tile(T, 512, 8)

    def body(x_ref, g_ref, sc_ref, sh_ref, *rest):
        h_ref = rest[-1]
        xv = x_ref[...]
        r = lax.rsqrt(jnp.mean(xv * xv, axis=-1, keepdims=True) + EPS)
        h_ref[...] = (((xv * r) * g_ref[...]) * (1.0 + sc_ref[...]) + sh_ref[...]).astype(BF16)

    row = pl.BlockSpec((1, D), lambda i: (0, 0))
    return pl.pallas_call(
        body, name=name, grid=(T // tm,),
        in_specs=[pl.BlockSpec((tm, D), lambda i: (i, 0)), row, row, row] + [_HBM] * len(first),
        out_specs=pl.BlockSpec((tm, D), lambda i: (i, 0)),
        out_shape=jax.ShapeDtypeStruct((T, D), BF16),
        compiler_params=_params(("parallel",)),
    )(x, g, sc, sh, *first)


def _prenorm_bwd(dh, x, dres, g, sc, name, first=()):
    T, D = x.shape
    tm = _tile(T, 256, 8)

    def body(dh_ref, x_ref, dres_ref, g_ref, sc_ref, *rest):
        dx_ref, sm_ref = rest[-2:]
        @pl.when(pl.program_id(0) == 0)
        def _():
            sm_ref[...] = jnp.zeros_like(sm_ref)

        xv, dhv = x_ref[...], dh_ref[...]
        r = lax.rsqrt(jnp.mean(xv * xv, axis=-1, keepdims=True) + EPS)
        xh = xv * r
        one_sc = 1.0 + sc_ref[...]
        sm_ref[0:1, :] += jnp.sum(dhv, axis=0, keepdims=True)
        sm_ref[1:2, :] += jnp.sum(dhv * (xh * g_ref[...]), axis=0, keepdims=True)
        sm_ref[2:3, :] += jnp.sum(dhv * one_sc * xh, axis=0, keepdims=True)
        dxh = dhv * one_sc * g_ref[...]
        dx_ref[...] = dres_ref[...] + r * (dxh - xh * jnp.mean(dxh * xh, axis=-1, keepdims=True))

    row = pl.BlockSpec((1, D), lambda i: (0, 0))
    big = pl.BlockSpec((tm, D), lambda i: (i, 0))
    return pl.pallas_call(
        body, name=name, grid=(T // tm,),
        in_specs=[big, big, big, row, row] + [_HBM] * len(first),
        out_specs=[big, pl.BlockSpec((8, D), lambda i: (0, 0))],
        out_shape=(jax.ShapeDtypeStruct((T, D), F32), jax.ShapeDtypeStruct((8, D), F32)),
        compiler_params=_params(("arbitrary",)),
    )(dh, x, dres, g, sc, *first)


def _mm_postnorm(a, w, x, gate, gpost, name):
    T, K = a.shape
    D = w.shape[1]
    tm = _tile(T, 256, 8)

    def body(a_ref, w_ref, x_ref, gate_ref, gp_ref, y_ref, xn_ref):
        y = _dot(a_ref[...], w_ref[...], NN)
        r = lax.rsqrt(jnp.mean(y * y, axis=-1, keepdims=True) + EPS)
        y_ref[...] = y
        xn_ref[...] = x_ref[...] + gate_ref[...] * ((y * r) * gp_ref[...])

    row = pl.BlockSpec((1, D), lambda i: (0, 0))
    big = pl.BlockSpec((tm, D), lambda i: (i, 0))
    return pl.pallas_call(
        body, name=name, grid=(T // tm,),
        in_specs=[pl.BlockSpec((tm, K), lambda i: (i, 0)), pl.BlockSpec((K, D), lambda i: (0, 0)), big, row, row],
        out_specs=[big, big],
        out_shape=(jax.ShapeDtypeStruct((T, D), F32), jax.ShapeDtypeStruct((T, D), F32)),
        compiler_params=_params(("parallel",)),
    )(a, w, x, gate, gpost)


def _postnorm_bwd(dx, y, gate, gpost, name):
    T, D = y.shape
    tm = _tile(T, 256, 8)

    def body(dx_ref, y_ref, gate_ref, gp_ref, dy_ref, sm_ref):
        @pl.when(pl.program_id(0) == 0)
        def _():
            sm_ref[...] = jnp.zeros_like(sm_ref)

        yv, dxv = y_ref[...], dx_ref[...]
        r = lax.rsqrt(jnp.mean(yv * yv, axis=-1, keepdims=True) + EPS)
        yh = yv * r
        dn = dxv * gate_ref[...]
        sm_ref[0:1, :] += jnp.sum(dxv * (yh * gp_ref[...]), axis=0, keepdims=True)
        sm_ref[1:2, :] += jnp.sum(dn * yh, axis=0, keepdims=True)
        dyh = dn * gp_ref[...]
        dy_ref[...] = (r * (dyh - yh * jnp.mean(dyh * yh, axis=-1, keepdims=True))).astype(BF16)

    row = pl.BlockSpec((1, D), lambda i: (0, 0))
    big = pl.BlockSpec((tm, D), lambda i: (i, 0))
    return pl.pallas_call(
        body, name=name, grid=(T // tm,),
        in_specs=[big, big, row, row],
        out_specs=[big, pl.BlockSpec((8, D), lambda i: (0, 0))],
        out_shape=(jax.ShapeDtypeStruct((T, D), BF16), jax.ShapeDtypeStruct((8, D), F32)),
        compiler_params=_params(("arbitrary",)),
    )(dx, y, gate, gpost)


def _ffn_in_fwd(h, w_t, name):
    T, D = h.shape
    F = w_t.shape[0] // 2
    tm, tn = _tile(T, MM_ROWS_SMALL), _tile(F, FFN_TILE)
    nj = F // tn

    def body(h_ref, wg_ref, wu_ref, g_ref, u_ref, act_ref):
        hv = h_ref[...]
        g = _dot(hv, wg_ref[...], NT)
        u = _dot(hv, wu_ref[...], NT)
        g_ref[...] = g.astype(BF16)
        u_ref[...] = u.astype(BF16)
        act_ref[...] = ((g * _sigmoid(g)) * u).astype(BF16)

    out = pl.BlockSpec((tm, tn), lambda j, i: (i, j))
    return pl.pallas_call(
        body, name=name, grid=(nj, T // tm),
        in_specs=[pl.BlockSpec((tm, D), lambda j, i: (i, 0)),
                  pl.BlockSpec((tn, D), lambda j, i: (j, 0)),
                  pl.BlockSpec((tn, D), lambda j, i: (j + nj, 0))],
        out_specs=[out, out, out],
        out_shape=tuple(jax.ShapeDtypeStruct((T, F), BF16) for _ in range(3)),
        compiler_params=_params(("parallel", "arbitrary")),
    )(h, w_t, w_t)


def _ffn_act_bwd(dy, w_out, g, u, name):
    T, D = dy.shape
    F = w_out.shape[0]
    tm, tn = _tile(T, MM_ROWS_SMALL), _tile(F, FFN_TILE)

    def body(dy_ref, w_ref, g_ref, u_ref, dg_ref, du_ref):
        dact = _dot(dy_ref[...], w_ref[...], NT)
        gv, uv = g_ref[...].astype(F32), u_ref[...].astype(F32)
        sg = _sigmoid(gv)
        dg_ref[...] = (dact * uv * (sg * (1.0 + gv * (1.0 - sg)))).astype(BF16)
        du_ref[...] = (dact * (gv * sg)).astype(BF16)

    tile = pl.BlockSpec((tm, tn), lambda j, i: (i, j))
    return pl.pallas_call(
        body, name=name, grid=(F // tn, T // tm),
        in_specs=[pl.BlockSpec((tm, D), lambda j, i: (i, 0)), pl.BlockSpec((tn, D), lambda j, i: (j, 0)), tile, tile],
        out_specs=[tile, tile],
        out_shape=(jax.ShapeDtypeStruct((T, F), BF16), jax.ShapeDtypeStruct((T, F), BF16)),
        compiler_params=_params(("parallel", "arbitrary")),
    )(dy, w_out, g, u)


def _lane_scan(v, reverse):
    T = v.shape[-1]
    lane = lax.broadcasted_iota(jnp.int32, v.shape, 1)
    d = 1
    while d < T:
        if reverse:
            v = v + jnp.where(lane < T - d, pltpu.roll(v, T - d, axis=1), 0.0)
        else:
            v = v + jnp.where(lane >= d, pltpu.roll(v, d, axis=1), 0.0)
        d *= 2
    return v


def _fgate_fwd(h, wf_t, bf, name):
    T, D = h.shape
    R = wf_t.shape[0]

    def body(h_ref, w_ref, b_ref, fl_ref, cum_ref):
        fl = _dot(w_ref[...], h_ref[...], NT) + b_ref[...]
        fl_ref[...] = fl
        logf = jnp.minimum(fl, 0.0) - jnp.log(1.0 + jnp.exp(-jnp.abs(fl)))
        cum_ref[...] = _lane_scan(logf, reverse=False)

    return pl.pallas_call(
        body, name=name,
        out_shape=(jax.ShapeDtypeStruct((R, T), F32), jax.ShapeDtypeStruct((R, T), F32)),
        compiler_params=_params(),
    )(h, wf_t, bf)


def _fgate_bwd(dcum, fl, h, name):
    R, T = fl.shape
    D = h.shape[1]

    def body(dc_ref, fl_ref, h_ref, dfl_ref, dw_ref, db_ref):
        dlogf = _lane_scan(dc_ref[...], reverse=True)
        dfl = dlogf * _sigmoid(-fl_ref[...])
        dfl_ref[...] = dfl
        dw_ref[...] = _dot(dfl.astype(BF16), h_ref[...], NN)
        db_ref[...] = jnp.broadcast_to(jnp.sum(dfl, axis=-1, keepdims=True), (R, LANES))

    return pl.pallas_call(
        body, name=name,
        out_shape=(jax.ShapeDtypeStruct((R, T), F32), jax.ShapeDtypeStruct((R, D), F32),
                   jax.ShapeDtypeStruct((R, LANES), F32)),
        compiler_params=_params(),
    )(dcum, fl, h)


def _head_masks(hpb, dh, rows):
    lane = lax.broadcasted_iota(jnp.int32, (rows, LANES), 1)
    return [(lane >= h * dh) & (lane < (h + 1) * dh) for h in range(hpb)]


def _stack_heads(v, masks):
    return jnp.concatenate([jnp.where(mk, v, jnp.zeros_like(v)) for mk in masks], axis=0)


def _heads_to_lanes(col, masks, tq):
    out = jnp.broadcast_to(col[0:tq], (tq, LANES))
    for h in range(1, len(masks)):
        out = jnp.where(masks[h], col[h * tq:(h + 1) * tq], out)
    return out


def _causal_stack(hpb, tq):
    r = lax.broadcasted_iota(jnp.int32, (tq, tq), 0)
    c = lax.broadcasted_iota(jnp.int32, (tq, tq), 1)
    return jnp.concatenate([c] * hpb, axis=0) <= jnp.concatenate([r] * hpb, axis=0)


def _attn_fwd(qkv, cum_rows, n_heads, name, tq):
    T = qkv.shape[0]
    A = qkv.shape[1] // 3
    dh = A // n_heads
    hpb = LANES // dh
    nb = A // LANES
    nq = T // tq
    scale = dh ** -0.5

    def body(q_ref, k_ref, v_ref, c_ref, o_ref, l_ref, vbd):
        hp, i = pl.program_id(0), pl.program_id(1)
        masks = _head_masks(hpb, dh, tq)

        @pl.when(i == 0)
        def _():
            def fill(j, _):
                vbd[j] = _stack_heads(v_ref[pl.ds(pl.multiple_of(j * tq, tq), tq), :], masks)
                return 0

            lax.fori_loop(0, nq, fill, 0)

        qs = _stack_heads(q_ref[...], masks)
        crow0 = hp * hpb * nq

        def tile(j, carry, diag):
            m, l, acc = carry
            kt = k_ref[pl.ds(pl.multiple_of(j * tq, tq), tq), :]
            bias = jnp.concatenate(
                [jnp.broadcast_to(c_ref[pl.ds(crow0 + h * nq + j, 1), :], (tq, tq)) for h in range(hpb)], axis=0)
            s = _dot(qs, kt, NT) * scale - bias
            if diag:
                s = jnp.where(_causal_stack(hpb, tq), s, NEG)
            m_new = jnp.maximum(m, jnp.max(s, axis=-1, keepdims=True))
            p = jnp.exp(s - m_new)
            alpha = jnp.exp(m - m_new)
            l = alpha * l + jnp.sum(p, axis=-1, keepdims=True)
            pcat = jnp.concatenate([p[h * tq:(h + 1) * tq] for h in range(hpb)], axis=1).astype(BF16)
            acc = _heads_to_lanes(alpha, masks, tq) * acc + _dot(pcat, vbd[j], NN)
            return m_new, l, acc

        init = (jnp.full((hpb * tq, 1), NEG, F32), jnp.zeros((hpb * tq, 1), F32), jnp.zeros((tq, LANES), F32))
        carry = lax.fori_loop(0, i, lambda j, c: tile(j, c, False), init)
        m, l, acc = tile(i, carry, True)
        o_ref[...] = (acc / _heads_to_lanes(l, masks, tq)).astype(BF16)
        lse = m + jnp.log(l)
        for h in range(hpb):
            l_ref[:, h:h + 1] = lse[h * tq:(h + 1) * tq]

    return pl.pallas_call(
        body, name=name, grid=(nb, nq),
        in_specs=[pl.BlockSpec((tq, LANES), lambda h, i: (i, h)),
                  pl.BlockSpec((T, LANES), lambda h, i: (0, nb + h)),
                  pl.BlockSpec((T, LANES), lambda h, i: (0, 2 * nb + h)),
                  pl.BlockSpec(cum_rows.shape, lambda h, i: (0, 0))],
        out_specs=[pl.BlockSpec((tq, LANES), lambda h, i: (i, h)),
                   pl.BlockSpec((None, tq, hpb), lambda h, i: (h, i, 0))],
        out_shape=(jax.ShapeDtypeStruct((T, A), BF16), jax.ShapeDtypeStruct((nb, T, hpb), F32)),
        scratch_shapes=[pltpu.VMEM((nq, hpb * tq, LANES), BF16)],
        compiler_params=_params(("arbitrary", "arbitrary")),
    )(qkv, qkv, qkv, cum_rows)


def _attn_bwd(qkv, dcat, o, lse, cum_rows, n_heads, name, tq):
    T = qkv.shape[0]
    A = qkv.shape[1] // 3
    dh = A // n_heads
    hpb = LANES // dh
    nb = A // LANES
    nq = T // tq
    scale = dh ** -0.5

    def body(q_ref, k_ref, v_ref, do_ref, o_ref, l_ref, c_ref, dqkv_ref, dc_ref, dr_ref,
             dq_acc, delta, drow, qs_scr, dos_scr, kbd_scr):
        hp = pl.program_id(0)
        masks = _head_masks(hpb, dh, tq)
        crow0 = hp * hpb * nq

        def prologue(i, _):
            rs = pl.ds(pl.multiple_of(i * tq, tq), tq)
            do = do_ref[rs, :]
            prod = do * o_ref[rs, :].astype(F32)
            for h in range(hpb):
                delta[rs, h:h + 1] = jnp.sum(jnp.where(masks[h], prod, 0.0), axis=-1, keepdims=True)
            qs_scr[i] = _stack_heads(q_ref[rs, :], masks)
            dos_scr[i] = _stack_heads(do, masks).astype(BF16)
            kbd_scr[i] = _stack_heads(k_ref[rs, :], masks)
            dq_acc[rs, :] = jnp.zeros((tq, LANES), F32)
            drow[rs, :] = jnp.zeros((tq, hpb), F32)
            return 0

        lax.fori_loop(0, nq, prologue, 0)

        def kv_step(j, _):
            ks = pl.ds(pl.multiple_of(j * tq, tq), tq)
            kt, vt = k_ref[ks, :], v_ref[ks, :]
            kbd = kbd_scr[j]
            bias = jnp.concatenate(
                [jnp.broadcast_to(c_ref[pl.ds(crow0 + h * nq + j, 1), :], (tq, tq)) for h in range(hpb)], axis=0)

            def q_step(i, carry, diag):
                dk, dv, dcs = carry
                rs = pl.ds(pl.multiple_of(i * tq, tq), tq)
                qs, dos = qs_scr[i], dos_scr[i]
                s = _dot(qs, kt, NT) * scale - bias
                if diag:
                    s = jnp.where(_causal_stack(hpb, tq), s, NEG)
                lse = jnp.concatenate([l_ref[rs, h:h + 1] for h in range(hpb)], axis=0)
                p = jnp.exp(s - lse)
                dv = dv + _dot(p.astype(BF16), dos, TN)
                dp = _dot(dos, vt, NT)
                ds = p * (dp - jnp.concatenate([delta[rs, h:h + 1] for h in range(hpb)], axis=0))
                dcs = tuple(dcs[h] - jnp.sum(ds[h * tq:(h + 1) * tq], axis=0, keepdims=True) for h in range(hpb))
                rsum = jnp.sum(ds, axis=-1, keepdims=True)
                for h in range(hpb):
                    drow[rs, h:h + 1] += rsum[h * tq:(h + 1) * tq]
                dsb = (ds * scale).astype(BF16)
                dk = dk + _dot(dsb, qs, TN)
                dscat = jnp.concatenate([dsb[h * tq:(h + 1) * tq] for h in range(hpb)], axis=1)
                dq_acc[rs, :] += _dot(dscat, kbd, NN)
                return dk, dv, dcs

            init = (jnp.zeros((tq, LANES), F32), jnp.zeros((tq, LANES), F32),
                    tuple(jnp.zeros((1, tq), F32) for _ in range(hpb)))
            carry = q_step(j, init, True)
            dk, dv, dcs = lax.fori_loop(j + 1, nq, lambda i, c: q_step(i, c, False), carry)
            dqkv_ref[1, ks, :] = dk.astype(BF16)
            dqkv_ref[2, ks, :] = dv.astype(BF16)
            for h in range(hpb):
                dc_ref[pl.ds(crow0 + h * nq + j, 1), :] = dcs[h]
            return 0

        lax.fori_loop(0, nq, kv_step, 0)
        dqkv_ref[0] = dq_acc[...].astype(BF16)
        dr_ref[...] = drow[...]

    col = lambda off: pl.BlockSpec((T, LANES), functools.partial(lambda h, off: (0, off + h), off=off))
    return pl.pallas_call(
        body, name=name, grid=(nb,),
        in_specs=[col(0), col(nb), col(2 * nb), col(0), col(0),
                  pl.BlockSpec((None, T, hpb), lambda h: (h, 0, 0)),
                  pl.BlockSpec(cum_rows.shape, lambda h: (0, 0))],
        out_specs=[pl.BlockSpec((3, T, LANES), lambda h: (0, 0, h)),
                   pl.BlockSpec(cum_rows.shape, lambda h: (0, 0)),
                   pl.BlockSpec((None, T, hpb), lambda h: (h, 0, 0))],
        out_shape=(jax.ShapeDtypeStruct((3, T, A), BF16), jax.ShapeDtypeStruct(cum_rows.shape, F32),
                   jax.ShapeDtypeStruct((nb, T, hpb), F32)),
        scratch_shapes=[pltpu.VMEM((T, LANES), F32), pltpu.VMEM((T, hpb), F32), pltpu.VMEM((T, hpb), F32),
                        pltpu.VMEM((nq, hpb * tq, LANES), BF16), pltpu.VMEM((nq, hpb * tq, LANES), BF16),
                        pltpu.VMEM((nq, hpb * tq, LANES), BF16)],
        compiler_params=_params(("arbitrary",)),
    )(qkv, qkv, qkv, dcat, o, lse, cum_rows)


def _glu_into(upad, cv_ref, cg_ref, T):
    upad[0:CONV_PAD, :] = jnp.zeros((CONV_PAD, upad.shape[1]), F32)

    def fill(c, _):
        rs = pl.ds(pl.multiple_of(c * CONV_CHUNK, CONV_CHUNK), CONV_CHUNK)
        upad[pl.ds(pl.multiple_of(CONV_PAD + c * CONV_CHUNK, 8), CONV_CHUNK), :] = cv_ref[rs, :] * _sigmoid(cg_ref[rs, :])
        return 0

    lax.fori_loop(0, T // CONV_CHUNK, fill, 0)


SUBLANES = 8
CONV_SHIFT_ROWS = CONV_CHUNK + CONV_PAD - SUBLANES


def _load_window(win, sh, src, r0):
    win[...] = src[pl.ds(r0, CONV_CHUNK + CONV_PAD), :]
    for b in range(1, SUBLANES):
        sh[b - 1] = win[b:b + CONV_SHIFT_ROWS, :]


def _tap(win, sh, o):
    b = o % SUBLANES
    if b == 0:
        return win[o:o + CONV_CHUNK, :]
    return sh[b - 1, o - b:o - b + CONV_CHUNK, :]


def _conv_taps(win, sh, w_ref, first, step):
    acc = None
    for k in range(CONV_K):
        t = w_ref[k:k + 1, :] * _tap(win, sh, first + step * k)
        acc = t if acc is None else acc + t
    return acc


def _conv_scratch(C):
    return [pltpu.VMEM((CONV_CHUNK + CONV_PAD, C), F32), pltpu.VMEM((SUBLANES - 1, CONV_SHIFT_ROWS, C), F32)]


def _conv_fwd(cproj, w, b, lg, lb, name):
    T = cproj.shape[0]
    C = cproj.shape[1] // 2
    off = CONV_PAD - (CONV_K - 1)

    def body(cv_ref, cg_ref, w_ref, b_ref, lg_ref, lb_ref, out_ref, upad, win, sh):
        _glu_into(upad, cv_ref, cg_ref, T)

        def chunk(c, _):
            r0 = pl.multiple_of(c * CONV_CHUNK, CONV_CHUNK)
            _load_window(win, sh, upad, r0)
            u1 = _conv_taps(win, sh, w_ref, off, 1) + b_ref[...]
            mu = jnp.mean(u1, axis=-1, keepdims=True)
            var = jnp.mean(jnp.square(u1 - mu), axis=-1, keepdims=True)
            u2 = ((u1 - mu) * lax.rsqrt(var + EPS)) * lg_ref[...] + lb_ref[...]
            out_ref[pl.ds(r0, CONV_CHUNK), :] = (u2 * _sigmoid(u2)).astype(BF16)
            return 0

        lax.fori_loop(0, T // CONV_CHUNK, chunk, 0)

    row = pl.BlockSpec((1, C), lambda i: (0, 0))
    return pl.pallas_call(
        body, name=name, grid=(1,),
        in_specs=[pl.BlockSpec((T, C), lambda i: (0, 0)), pl.BlockSpec((T, C), lambda i: (0, 1)),
                  pl.BlockSpec(w.shape, lambda i: (0, 0)), row, row, row],
        out_specs=pl.BlockSpec((T, C), lambda i: (0, 0)),
        out_shape=jax.ShapeDtypeStruct((T, C), BF16),
        scratch_shapes=[pltpu.VMEM((T + CONV_PAD, C), F32)] + _conv_scratch(C),
        compiler_params=_params(("arbitrary",)),
    )(cproj, cproj, w, b, lg, lb)


def _conv_bwd(cproj, dcat, w, b, lg, lb, name):
    T = cproj.shape[0]
    C = cproj.shape[1] // 2
    off = CONV_PAD - (CONV_K - 1)
    n_chunks = T // CONV_CHUNK

    def fold(v):
        return jnp.sum(v.reshape(CONV_CHUNK // 8, 8, C), axis=0)

    def body(cv_ref, cg_ref, du_ref, w_ref, b_ref, lg_ref, lb_ref, dc_ref, dw_ref, sm_ref,
             upad, dpad, dwacc, smacc, win, sh):
        _glu_into(upad, cv_ref, cg_ref, T)
        dpad[pl.ds(T, CONV_PAD), :] = jnp.zeros((CONV_PAD, C), F32)
        dwacc[...] = jnp.zeros_like(dwacc)
        smacc[...] = jnp.zeros_like(smacc)

        def chunk_a(c, _):
            r0 = pl.multiple_of(c * CONV_CHUNK, CONV_CHUNK)
            _load_window(win, sh, upad, r0)
            u1 = _conv_taps(win, sh, w_ref, off, 1) + b_ref[...]
            mu = jnp.mean(u1, axis=-1, keepdims=True)
            var = jnp.mean(jnp.square(u1 - mu), axis=-1, keepdims=True)
            rstd = lax.rsqrt(var + EPS)
            u1h = (u1 - mu) * rstd
            u2 = u1h * lg_ref[...] + lb_ref[...]
            sg = _sigmoid(u2)
            du2 = du_ref[pl.ds(r0, CONV_CHUNK), :] * (sg * (1.0 + u2 * (1.0 - sg)))
            smacc[8:16, :] += fold(du2 * u1h)
            smacc[16:24, :] += fold(du2)
            du1h = du2 * lg_ref[...]
            du1 = rstd * (du1h - jnp.mean(du1h, axis=-1, keepdims=True)
                          - u1h * jnp.mean(du1h * u1h, axis=-1, keepdims=True))
            smacc[0:8, :] += fold(du1)
            dpad[pl.ds(r0, CONV_CHUNK), :] = du1
            for k in range(CONV_K):
                dwacc[8 * k:8 * k + 8, :] += fold(du1 * _tap(win, sh, off + k))
            return 0

        lax.fori_loop(0, n_chunks, chunk_a, 0)

        def chunk_b(c, _):
            r0 = pl.multiple_of(c * CONV_CHUNK, CONV_CHUNK)
            rs = pl.ds(r0, CONV_CHUNK)
            _load_window(win, sh, dpad, r0)
            du0 = _conv_taps(win, sh, w_ref, CONV_K - 1, -1)
            cv, sg = cv_ref[rs, :], _sigmoid(cg_ref[rs, :])
            dc_ref[rs, 0:C] = (du0 * sg).astype(BF16)
            dc_ref[rs, C:2 * C] = (du0 * cv * (sg * (1.0 - sg))).astype(BF16)
            return 0

        lax.fori_loop(0, n_chunks, chunk_b, 0)
        for k in range(CONV_K):
            dw_ref[k:k + 1, :] = jnp.sum(dwacc[8 * k:8 * k + 8, :], axis=0, keepdims=True)
        dw_ref[CONV_K:CONV_PAD, :] = jnp.zeros((CONV_PAD - CONV_K, C), F32)
        for r in range(3):
            sm_ref[r:r + 1, :] = jnp.sum(smacc[8 * r:8 * r + 8, :], axis=0, keepdims=True)
        sm_ref[3:8, :] = jnp.zeros((5, C), F32)

    row = pl.BlockSpec((1, C), lambda i: (0, 0))
    return pl.pallas_call(
        body, name=name, grid=(1,),
        in_specs=[pl.BlockSpec((T, C), lambda i: (0, 0)), pl.BlockSpec((T, C), lambda i: (0, 1)),
                  pl.BlockSpec((T, C), lambda i: (0, 1)),
                  pl.BlockSpec(w.shape, lambda i: (0, 0)), row, row, row],
        out_specs=[pl.BlockSpec((T, 2 * C), lambda i: (0, 0)), pl.BlockSpec((CONV_PAD, C), lambda i: (0, 0)),
                   pl.BlockSpec((8, C), lambda i: (0, 0))],
        out_shape=(jax.ShapeDtypeStruct((T, 2 * C), BF16), jax.ShapeDtypeStruct((CONV_PAD, C), F32),
                   jax.ShapeDtypeStruct((8, C), F32)),
        scratch_shapes=[pltpu.VMEM((T + CONV_PAD, C), F32), pltpu.VMEM((T + CONV_PAD, C), F32),
                        pltpu.VMEM((8 * CONV_PAD, C), F32), pltpu.VMEM((24, C), F32)] + _conv_scratch(C),
        compiler_params=_params(("arbitrary",)),
    )(cproj, cproj, dcat, w, b, lg, lb)


def _loss_head(x, target, name):
    T, D = x.shape
    tm = _tile(T, 512, 8)

    def body(x_ref, t_ref, loss_ref, dx_ref):
        @pl.when(pl.program_id(0) == 0)
        def _():
            loss_ref[...] = jnp.zeros_like(loss_ref)

        err = x_ref[...] - t_ref[...]
        part = jnp.sum(jnp.mean(err * err, axis=-1, keepdims=True), axis=0, keepdims=True)
        loss_ref[...] += jnp.broadcast_to(0.5 * part, loss_ref.shape)
        dx_ref[...] = err * (1.0 / D)

    big = pl.BlockSpec((tm, D), lambda i: (i, 0))
    return pl.pallas_call(
        body, name=name, grid=(T // tm,),
        in_specs=[big, big],
        out_specs=[pl.BlockSpec((8, LANES), lambda i: (0, 0)), big],
        out_shape=(jax.ShapeDtypeStruct((8, LANES), F32), jax.ShapeDtypeStruct((T, D), F32)),
        compiler_params=_params(("arbitrary",)),
    )(x, target)


def _ada_fwd(c_all, ada_w, ada_b_loc, name):
    L, D, S = ada_w.shape
    B = c_all.shape[0]

    def body(c_ref, w_ref, b_ref, o_ref):
        c = c_ref[...]
        ca = (c * _sigmoid(c)).astype(BF16)
        o_ref[...] = _dot(ca, w_ref[...].astype(BF16), NN) + b_ref[...]

    return pl.pallas_call(
        body, name=name, grid=(L,),
        in_specs=[pl.BlockSpec((B, D), lambda l: (0, 0)), pl.BlockSpec((None, D, S), lambda l: (l, 0, 0)),
                  pl.BlockSpec((None, 1, S), lambda l: (l, 0, 0))],
        out_specs=pl.BlockSpec((None, B, S), lambda l: (l, 0, 0)),
        out_shape=jax.ShapeDtypeStruct((L, B, S), F32),
        compiler_params=_params(("parallel",)),
    )(c_all, ada_w, ada_b_loc)


def _ada_bwd(c_all_t, dmod_loc, name):
    D, B = c_all_t.shape
    L, _, S = dmod_loc.shape

    def body(c_ref, dm_ref, o_ref):
        c = c_ref[...]
        ca = c * _sigmoid(c)
        acc = None
        for bb in range(B):
            t = ca[:, bb:bb + 1] * dm_ref[bb:bb + 1, :]
            acc = t if acc is None else acc + t
        o_ref[...] = acc

    return pl.pallas_call(
        body, name=name, grid=(L,),
        in_specs=[pl.BlockSpec((D, B), lambda l: (0, 0)), pl.BlockSpec((None, B, S), lambda l: (l, 0, 0))],
        out_specs=pl.BlockSpec((None, D, S), lambda l: (l, 0, 0)),
        out_shape=jax.ShapeDtypeStruct((L, D, S), F32),
        compiler_params=_params(("parallel",)),
    )(c_all_t, dmod_loc)


def _sum_devices(parts, name):
    _, R, C = parts.shape
    tm = _tile(R, 256, 8)

    def body(p_ref, o_ref):
        acc = p_ref[0].astype(F32)
        for d in range(1, N_DEV):
            acc = acc + p_ref[d].astype(F32)
        o_ref[...] = acc

    return pl.pallas_call(
        body, name=name, grid=(R // tm,),
        in_specs=[pl.BlockSpec((N_DEV, tm, C), lambda i: (0, i, 0))],
        out_specs=pl.BlockSpec((tm, C), lambda i: (i, 0)),
        out_shape=jax.ShapeDtypeStruct((R, C), F32),
        compiler_params=_params(("parallel",)),
    )(parts)


def _adamw_math(w, g, m, v):
    m = ADAM_B1 * m + (1.0 - ADAM_B1) * g
    v = ADAM_B2 * v + (1.0 - ADAM_B2) * (g * g)
    m_hat = m / (1.0 - ADAM_B1 ** ADAM_STEP)
    v_hat = v / (1.0 - ADAM_B2 ** ADAM_STEP)
    delta = -ADAM_LR * (m_hat / (jnp.sqrt(v_hat) + ADAM_EPS) + ADAM_WD * w)
    return delta, m, v


def _adamw(w, g, m, v, name, summed):
    R, C = w.shape
    tm = _tile(R, 256, 16)
    n_parts = g.shape[0] if summed else 0

    def body(w_ref, g_ref, m_ref, v_ref, go_ref, d_ref, mo_ref, vo_ref):
        if summed:
            g = g_ref[0].astype(F32)
            for d in range(1, n_parts):
                g = g + g_ref[d].astype(F32)
        else:
            g = g_ref[...]
        delta, mn, vn = _adamw_math(w_ref[...], g, m_ref[...], v_ref[...])
        go_ref[...] = g
        d_ref[...] = delta
        mo_ref[...] = mn
        vo_ref[...] = vn

    big = pl.BlockSpec((tm, C), lambda i: (i, 0))
    gspec = pl.BlockSpec((n_parts, tm, C), lambda i: (0, i, 0)) if summed else big
    return pl.pallas_call(
        body, name=name, grid=(R // tm,),
        in_specs=[big, gspec, big, big],
        out_specs=[big, big, big, big],
        out_shape=tuple(jax.ShapeDtypeStruct((R, C), F32) for _ in range(4)),
        compiler_params=_params(("parallel",)),
    )(w, g, m, v)


def _adamw_summed(w, parts, m, v, name):
    L = len(parts)
    n_parts, R, C = parts[0].shape
    tm = _tile(R, 256, 16)
    nr = R // tm

    def body(*refs):
        w_ref, g_refs = refs[0], refs[1:1 + L]
        m_ref, v_ref, go_ref, d_ref, mo_ref, vo_ref = refs[1 + L:]
        for ll in range(L):
            @pl.when(pl.program_id(0) == ll)
            def _(ll=ll):
                g = g_refs[ll][0].astype(F32)
                for d in range(1, n_parts):
                    g = g + g_refs[ll][d].astype(F32)
                delta, mn, vn = _adamw_math(w_ref[...], g, m_ref[...], v_ref[...])
                go_ref[...] = g
                d_ref[...] = delta
                mo_ref[...] = mn
                vo_ref[...] = vn

    big = pl.BlockSpec((None, tm, C), lambda l, i: (l, i, 0))
    gspecs = [pl.BlockSpec((n_parts, tm, C), functools.partial(lambda l, i, ll: (0, jnp.where(l == ll, i, 0), 0), ll=ll))
              for ll in range(L)]
    return pl.pallas_call(
        body, name=name, grid=(L, nr),
        in_specs=[big, *gspecs, big, big],
        out_specs=[big, big, big, big],
        out_shape=tuple(jax.ShapeDtypeStruct((L, R, C), F32) for _ in range(4)),
        compiler_params=_params(("arbitrary", "arbitrary")),
    )(w, *parts, m, v)


def _pack(arrs, D):
    L = arrs[0].shape[0]
    cols = []
    for a in arrs:
        f = a.reshape(L, -1)
        n = f.shape[1]
        cols.append(jnp.pad(f, ((0, 0), (0, -(-n // D) * D - n))))
    flat = jnp.concatenate(cols, axis=1)
    return flat.reshape(L, flat.shape[1] // D, D)


def _unpack(p, shapes, D):
    L = p.shape[0]
    out, r = [], 0
    for s in shapes:
        n = math.prod(s[1:])
        rows = -(-n // D)
        out.append(p[:, r:r + rows].reshape(L, rows * D)[:, :n].reshape(s))
        r += rows
    return out


def kernel(x, c, w_in, b_f, conv_w, conv_b, conv_ln_g, conv_ln_b, w_o, w_ffn_in, w_ffn_out, mix_pre_g, mix_post_g, ffn_pre_g, ffn_post_g, ada_w, ada_b, loss_target, m_w_in, m_b_f, m_conv_w, m_conv_b, m_conv_ln_g, m_conv_ln_b, m_w_o, m_w_ffn_in, m_w_ffn_out, m_mix_pre_g, m_mix_post_g, m_ffn_pre_g, m_ffn_post_g, m_ada_w, m_ada_b, v_w_in, v_b_f, v_conv_w, v_conv_b, v_conv_ln_g, v_conv_ln_b, v_w_o, v_w_ffn_in, v_w_ffn_out, v_mix_pre_g, v_mix_post_g, v_ffn_pre_g, v_ffn_post_g, v_ada_w, v_ada_b):
    L, D, s_in = w_in.shape
    T = x.shape[1]
    H = b_f.shape[1]
    A = D // 2
    C = D - A
    cs = conv_w.shape[2]
    F = w_ffn_out.shape[1] * N_DEV
    s_ff = w_ffn_in.shape[2]
    w_ffn_in_t = jnp.transpose(w_ffn_in, (0, 2, 1))
    s_ada = ada_w.shape[2]
    R = 16
    me = _my_index()
    x0 = x[0]
    target = loss_target[0]
    tq = _tile(T, 512)
    nq = T // tq

    c_all = _exchange(c, gather=True, name="gather_c").reshape(N_DEV, D)
    ada_b_loc = lax.dynamic_slice_in_dim(ada_b, me * s_ada, s_ada, axis=1)[:, None, :]
    mod_loc = _ada_fwd(c_all, ada_w, ada_b_loc, "ada_fwd")
    mod_g = _exchange(mod_loc, gather=True, name="gather_mod")
    mod = lax.dynamic_index_in_dim(mod_g, me, axis=2, keepdims=False)
    mod = jnp.transpose(mod, (1, 0, 2)).reshape(L, N_MOD, 1, D)
    cw_g = _exchange(conv_w, gather=True, name="gather_conv_w")
    conv_w_full = jnp.transpose(cw_g, (1, 2, 0, 3)).reshape(L, CONV_K, C)
    conv_w_pad = jnp.pad(conv_w_full, ((0, 0), (0, CONV_PAD - CONV_K), (0, 0)))
    b_f_col = jnp.pad(b_f, ((0, 0), (0, R - H)))[:, :, None]

    h_gmix, h_gffn, chain = [], [], (mod_g, cw_g)
    for l in range(L):
        h_gmix.append(_gather_start([w_in[l].astype(BF16), w_o[l].astype(BF16)], f"gather_mix_{l}", chain))
        h_gffn.append(_gather_start([w_ffn_in_t[l].astype(BF16), w_ffn_out[l].astype(BF16)], f"gather_ffn_{l}",
                                    (h_gmix[l]["token"],)))
        chain = (h_gffn[l]["token"],)
    gather_tokens = chain

    def by_cols(g):
        return jnp.transpose(g, (1, 0, 2)).reshape(g.shape[1], N_DEV * g.shape[2])

    def by_rows(g):
        return g.reshape(N_DEV * g.shape[1], g.shape[2])

    W_qkv, W_f_t, W_f_pad, W_c, W_o, W_ffn_in_t, W_ffn_out = ([None] * L for _ in range(7))

    saved = []
    xc = x0
    for l in range(L):
        sh1, sc1, g1, sh2, sc2, g2 = (mod[l, k] for k in range(N_MOD))
        gpre1, gpost1, gpre2, gpost2 = (p[l][None, :] for p in (mix_pre_g, mix_post_g, ffn_pre_g, ffn_post_g))
        dep = gather_tokens[0] if l == 0 else xc
        g_in, g_o = _gather_wait(_gather_mid(h_gmix[l], dep, f"gather_mid_mix_{l}"), dep, f"gather_wait_mix_{l}")
        W_in_l = by_cols(g_in)
        W_qkv[l], W_f, W_c[l], W_o[l] = W_in_l[:, :3 * A], W_in_l[:, 3 * A:3 * A + H], W_in_l[:, 3 * A + H:], by_rows(g_o)
        W_f_t[l] = jnp.pad(jnp.transpose(W_f), ((0, R - H), (0, 0)))
        W_f_pad[l] = jnp.pad(W_f, ((0, 0), (0, R - H)))
        h1 = _prenorm(xc, gpre1, sc1, sh1, f"prenorm1_{l}", first=gather_tokens if l == 0 else ())
        qkv = _mm([(h1, W_qkv[l])], "nn", BF16, f"proj_qkv_{l}")
        cproj = _mm([(h1, W_c[l])], "nn", F32, f"proj_conv_{l}")
        fl, cum = _fgate_fwd(h1, W_f_t[l], b_f_col[l], f"fgate_{l}")
        cum_rows = cum[:H].reshape(H * nq, tq)
        o, lse = _attn_fwd(qkv, cum_rows, H, f"attn_{l}", tq)
        h_ffn_l = _gather_mid(h_gffn[l], o, f"gather_mid_ffn_{l}")
        u3 = _conv_fwd(cproj, conv_w_pad[l], conv_b[l][None, :], conv_ln_g[l][None, :], conv_ln_b[l][None, :], f"conv_{l}")
        cat = jnp.concatenate([o, u3], axis=-1)
        y1, x_mid = _mm_postnorm(cat, W_o[l], xc, g1, gpost1, f"out_proj_{l}")
        g_fi, g_fo = _gather_wait(h_ffn_l, x_mid, f"gather_wait_ffn_{l}")
        W_ffn_in_t[l], W_ffn_out[l] = by_rows(g_fi), by_rows(g_fo)
        h2 = _prenorm(x_mid, gpre2, sc2, sh2, f"prenorm2_{l}")
        g, u, act = _ffn_in_fwd(h2, W_ffn_in_t[l], f"ffn_in_{l}")
        y2, x_out = _mm_postnorm(act, W_ffn_out[l], x_mid, g2, gpost2, f"ffn_out_{l}")
        saved.append((xc, h1, qkv, cproj, fl, cum_rows, o, lse, cat, y1, x_mid, h2, g, u, act, y2))
        xc = x_out

    loss_tile, dx = _loss_head(xc, target, "loss_head")
    loss = lax.psum(loss_tile[0, 0], ("x", "y", "c"))

    mc = lax.axis_index("c")

    def reduce_start(parts, name):
        keeps, sends = [], []
        for part in parts:
            part4 = part.reshape(4, 2, *part.shape[1:])
            keeps.append(lax.dynamic_index_in_dim(part4, mc, axis=1, keepdims=False))
            sends.append(lax.dynamic_index_in_dim(part4, 1 - mc, axis=1, keepdims=False))
        gots = _swap_sibling(sends, "swap_" + name)
        boths = [_pair_sum(k.reshape(-1, k.shape[-1]), t.reshape(-1, t.shape[-1]), f"pairsum{a}_{name}").reshape(k.shape)
                 for a, (k, t) in enumerate(zip(keeps, gots))]
        return _chips_start(boths, "scatter_" + name)

    small, h_ffn, h_mix = [None] * L, [None] * L, [None] * L
    for l in reversed(range(L)):
        xin, h1, qkv, cproj, fl, cum_rows, o, lse, cat, y1, x_mid, h2, g, u, act, y2 = saved[l]
        sh1, sc1, g1, sh2, sc2, g2 = (mod[l, k] for k in range(N_MOD))
        gpre1, gpost1, gpre2, gpost2 = (p[l][None, :] for p in (mix_pre_g, mix_post_g, ffn_pre_g, ffn_post_g))
        dy2, sm_post2 = _postnorm_bwd(dx, y2, g2, gpost2, f"postnorm2_bwd_{l}")
        dgate, dup = _ffn_act_bwd(dy2, W_ffn_out[l], g, u, f"ffn_act_bwd_{l}")
        dW_ffn_out = _mm([(act, dy2)], "tn", BF16, f"dw_ffn_out_{l}", tm=FFN_TILE)
        dh2 = _mm([(dgate, W_ffn_in_t[l], 0), (dup, W_ffn_in_t[l], 1)], "nn", F32, f"dh2_{l}", tn=MM_ROWS_SMALL)
        dWg_t = _mm([(dgate, h2)], "tn", BF16, f"dw_ffn_gate_{l}", tm=FFN_TILE)
        dWu_t = _mm([(dup, h2)], "tn", BF16, f"dw_ffn_up_{l}", tm=FFN_TILE)
        half = N_DEV // 2
        dW_ffn_in_t = jnp.concatenate([dWg_t.reshape(half, s_ff, D), dWu_t.reshape(half, s_ff, D)], axis=0)
        h_ffn[l], token = reduce_start([dW_ffn_in_t, dW_ffn_out.reshape(N_DEV, F // N_DEV, D)], f"ffn_{l}")
        dx_mid, sm_pre2 = _prenorm_bwd(dh2, x_mid, dx, gpre2, sc2, f"prenorm2_bwd_{l}", first=(token,))
        dy1, sm_post1 = _postnorm_bwd(dx_mid, y1, g1, gpost1, f"postnorm1_bwd_{l}")
        dcat = _mm([(dy1, W_o[l])], "nt", F32, f"dcat_{l}")
        dW_o = _mm([(cat, dy1)], "tn", BF16, f"dw_o_{l}")
        dqkv, dcum_rows, dcum_q = _attn_bwd(qkv, dcat, o, lse, cum_rows, H, f"attn_bwd_{l}", tq)
        dcproj, dconv_w, sm_conv = _conv_bwd(cproj, dcat, conv_w_pad[l], conv_b[l][None, :], conv_ln_g[l][None, :],
                                             conv_ln_b[l][None, :], f"conv_bwd_{l}")
        dcum = dcum_rows.reshape(H, T) + jnp.transpose(dcum_q, (0, 2, 1)).reshape(H, T)
        dcum = jnp.pad(dcum, ((0, R - H), (0, 0)))
        dfl_t, dwf_t, dbf = _fgate_bwd(dcum, fl, h1, f"fgate_bwd_{l}")
        dfl = jnp.transpose(dfl_t).astype(BF16)
        dh1 = _mm([(dqkv[0], W_qkv[l], 0), (dqkv[1], W_qkv[l], 1), (dqkv[2], W_qkv[l], 2),
                   (dfl, W_f_pad[l]), (dcproj, W_c[l])], "nt", F32, f"dh1_{l}")
        dWq = [_mm([(h1, dqkv[k])], "tn", BF16, f"dw_qkv{k}_{l}") for k in range(3)]
        dWc = _mm([(h1, dcproj)], "tn", BF16, f"dw_conv_{l}")
        dWf = jnp.transpose(dwf_t[:H]).astype(BF16)
        dW_in = jnp.concatenate(dWq + [dWf, dWc], axis=1)
        h_mix[l], token = reduce_start([jnp.transpose(dW_in.reshape(D, N_DEV, s_in), (1, 0, 2)),
                                        dW_o.reshape(N_DEV, D // N_DEV, D)], f"mix_{l}")
        dx, sm_pre1 = _prenorm_bwd(dh1, xin, dx_mid, gpre1, sc1, f"prenorm1_bwd_{l}", first=(token,))
        dmod = jnp.stack([sm_pre1[0], sm_pre1[1], sm_post1[0], sm_pre2[0], sm_pre2[1], sm_post2[0]])
        small[l] = (dmod, sm_pre1[2], sm_post1[1], sm_pre2[2], sm_post2[1], sm_conv[0], sm_conv[1], sm_conv[2],
                    dbf[:H, 0], dconv_w[:CONV_K])
    grad_x = dx[None]

    small_names = 10
    small_l = [jnp.stack([small[l][k] for l in range(L)]) for k in range(small_names)]
    small_shapes = [a.shape for a in small_l]
    packed = _pack(small_l, D)
    rows = packed.shape[1]
    rows_pad = -(-L * rows // 8) * 8
    packed2 = jnp.pad(packed.reshape(L * rows, D), ((0, rows_pad - L * rows), (0, 0)))
    small_g = _exchange(packed2, gather=True, name="gather_small")
    small_sum = _sum_devices(small_g, "sum_small")[:L * rows].reshape(L, rows, D)
    (g_ada_b6, g_mix_pre, g_mix_post, g_ffn_pre, g_ffn_post, g_conv_b, g_ln_g, g_ln_b, g_b_f,
     g_conv_w_full) = _unpack(small_sum, small_shapes, D)
    g_ada_b = g_ada_b6.reshape(L, N_MOD * D)
    g_conv_w = lax.dynamic_slice_in_dim(g_conv_w_full, me * cs, cs, axis=2)
    dmod_all = small_g[:, :L * rows].reshape(N_DEV, L, rows, D)[:, :, :N_MOD].reshape(N_DEV, L, N_MOD * D)
    dmod_loc = jnp.transpose(lax.dynamic_slice_in_dim(dmod_all, me * s_ada, s_ada, axis=2), (1, 0, 2))
    g_ada_w = _ada_bwd(jnp.transpose(c_all), dmod_loc, "ada_bwd")

    got_ffn = [_chips_wait(h_ffn[l], dx, f"scatter_wait_ffn_{l}") for l in range(L)]
    got_mix = [_chips_wait(h_mix[l], dx, f"scatter_wait_mix_{l}") for l in range(L)]

    def step(w, m, v, parts, name):
        return list(_adamw_summed(w, parts, m, v, "adamw_" + name))

    swap = lambda t: jnp.transpose(t, (0, 2, 1))
    r_w_ffn_in = [swap(t) for t in step(w_ffn_in_t, swap(m_w_ffn_in), swap(v_w_ffn_in),
                                        [got_ffn[l][0] for l in range(L)], "w_ffn_in")]
    r_w_ffn_out = step(w_ffn_out, m_w_ffn_out, v_w_ffn_out, [got_ffn[l][1] for l in range(L)], "w_ffn_out")
    r_w_in = step(w_in, m_w_in, v_w_in, [got_mix[l][0] for l in range(L)], "w_in")
    r_w_o = step(w_o, m_w_o, v_w_o, [got_mix[l][1] for l in range(L)], "w_o")
    r_ada_w = [t.reshape(ada_w.shape) for t in _adamw(
        ada_w.reshape(L * D, s_ada), g_ada_w.reshape(L * D, s_ada), m_ada_w.reshape(L * D, s_ada),
        v_ada_w.reshape(L * D, s_ada), "adamw_ada_w", False)]

    sw = [b_f, conv_w, conv_b, conv_ln_g, conv_ln_b, mix_pre_g, mix_post_g, ffn_pre_g, ffn_post_g, ada_b]
    sg = [g_b_f, g_conv_w, g_conv_b, g_ln_g, g_ln_b, g_mix_pre, g_mix_post, g_ffn_pre, g_ffn_post, g_ada_b]
    sm = [m_b_f, m_conv_w, m_conv_b, m_conv_ln_g, m_conv_ln_b, m_mix_pre_g, m_mix_post_g, m_ffn_pre_g, m_ffn_post_g, m_ada_b]
    sv = [v_b_f, v_conv_w, v_conv_b, v_conv_ln_g, v_conv_ln_b, v_mix_pre_g, v_mix_post_g, v_ffn_pre_g, v_ffn_post_g, v_ada_b]
    shapes = [a.shape for a in sw]

    def flat(arrs):
        p = _pack(arrs, D)
        n = p.shape[0] * p.shape[1]
        return jnp.pad(p.reshape(n, D), ((0, -(-n // 8) * 8 - n), (0, 0))), p.shape

    pw, pshape = flat(sw)
    pg, pm, pv = flat(sg)[0], flat(sm)[0], flat(sv)[0]
    s_outs = _adamw(pw, pg, pm, pv, "adamw_small", False)
    n_small = pshape[0] * pshape[1]
    s_g, s_d, s_m, s_v = (_unpack(t[:n_small].reshape(pshape), shapes, D) for t in s_outs)

    big = {"w_in": r_w_in, "w_o": r_w_o, "w_ffn_in": r_w_ffn_in, "w_ffn_out": r_w_ffn_out, "ada_w": r_ada_w}
    order = ["w_in", "b_f", "conv_w", "conv_b", "conv_ln_g", "conv_ln_b", "w_o", "w_ffn_in", "w_ffn_out",
             "mix_pre_g", "mix_post_g", "ffn_pre_g", "ffn_post_g", "ada_w", "ada_b"]
    small_pos = {n: i for i, n in enumerate(["b_f", "conv_w", "conv_b", "conv_ln_g", "conv_ln_b", "mix_pre_g",
                                             "mix_post_g", "ffn_pre_g", "ffn_post_g", "ada_b"])}

    def pick(n, k):
        if n in big:
            return big[n][k]
        return (s_g, s_d, s_m, s_v)[k][small_pos[n]]

    return (loss, grad_x, *[pick(n, 0) for n in order], *[pick(n, 1) for n in order],
            *[pick(n, 2) for n in order], *[pick(n, 3) for n in order])
```

```python
import functools
import math

import jax
import jax.numpy as jnp
from jax import lax
from jax.experimental import pallas as pl
from jax.experimental.pallas import tpu as pltpu

F32 = jnp.float32
BF16 = jnp.bfloat16
MESH = pl.DeviceIdType.MESH
N_DEV = 8
EPS = 1e-6
CONV_K = 31
CONV_PAD = 32
CONV_CHUNK = 128
N_MOD = 6
NEG = -1e30
LANES = 128
VMEM_LIMIT = 56 * 2**20
MM_TILE = 1024
MM_ROWS = 2048
MM_COLS = 512
MM_ROWS_SMALL = 512
FFN_TILE = 1408
FFN_COLS = 256
ADAM_LR, ADAM_B1, ADAM_B2, ADAM_EPS, ADAM_WD, ADAM_STEP = 0.001, 0.9, 0.999, 1e-08, 0.01, 10

NN = (((1,), (0,)), ((), ()))
NT = (((1,), (1,)), ((), ()))
TN = (((0,), (0,)), ((), ()))


def _dot(a, b, dims):
    return lax.dot_general(a, b, dims, preferred_element_type=F32)


def _tile(n, pref, align=LANES):
    if n <= pref:
        return n
    t = (pref // align) * align
    while t >= align:
        if n % t == 0:
            return t
        t -= align
    return n


def _params(sem=None):
    return pltpu.CompilerParams(dimension_semantics=sem, vmem_limit_bytes=VMEM_LIMIT)


def _sigmoid(x):
    return 1.0 / (1.0 + jnp.exp(-x))


def _my_index():
    return 4 * lax.axis_index("x") + 2 * lax.axis_index("y") + lax.axis_index("c")


def _exchange(x, *, gather, name):
    blk = x.shape if gather else x.shape[1:]

    def body(x_ref, y_ref, send_sems, recv_sems, local_sem):
        mx, my, mc = lax.axis_index("x"), lax.axis_index("y"), lax.axis_index("c")
        me = 4 * mx + 2 * my + mc

        def src(p):
            return x_ref if gather else x_ref.at[p]

        mine = pltpu.make_async_copy(src(me), y_ref.at[me], local_sem)
        mine.start()
        copies = []
        for k in range(1, N_DEV):
            px = (1 - mx) if (k >> 2) & 1 else mx
            py = (1 - my) if (k >> 1) & 1 else my
            pc = (1 - mc) if k & 1 else mc
            cp = pltpu.make_async_remote_copy(
                src_ref=src(4 * px + 2 * py + pc), dst_ref=y_ref.at[me],
                send_sem=send_sems.at[k - 1], recv_sem=recv_sems.at[k - 1],
                device_id=(px, py, pc), device_id_type=MESH)
            cp.start()
            copies.append(cp)
        for cp in copies:
            cp.wait()
        mine.wait()

    return pl.pallas_call(
        body, name=name,
        out_shape=jax.ShapeDtypeStruct((N_DEV,) + tuple(blk), x.dtype),
        in_specs=[pl.BlockSpec(memory_space=pl.ANY)],
        out_specs=pl.BlockSpec(memory_space=pl.ANY),
        scratch_shapes=[pltpu.SemaphoreType.DMA((N_DEV - 1,)), pltpu.SemaphoreType.DMA((N_DEV - 1,)),
                        pltpu.SemaphoreType.DMA(())],
    )(x)


_HBM = pl.BlockSpec(memory_space=pl.ANY)


def _swap_sibling(xs, name):
    n = len(xs)

    def body(*refs):
        send_sems, recv_sems = refs[2 * n], refs[2 * n + 1]
        sib = (lax.axis_index("x"), lax.axis_index("y"), 1 - lax.axis_index("c"))
        copies = [pltpu.make_async_remote_copy(src_ref=refs[a], dst_ref=refs[n + a], send_sem=send_sems.at[a],
                                               recv_sem=recv_sems.at[a], device_id=sib, device_id_type=MESH)
                  for a in range(n)]
        for cp in copies:
            cp.start()
        for cp in copies:
            cp.wait()

    return pl.pallas_call(
        body, name=name, out_shape=[jax.ShapeDtypeStruct(x.shape, x.dtype) for x in xs],
        in_specs=[_HBM] * n, out_specs=[_HBM] * n,
        scratch_shapes=[pltpu.SemaphoreType.DMA((n,)), pltpu.SemaphoreType.DMA((n,))],
    )(*xs)


_SEM = pl.BlockSpec(memory_space=pltpu.SEMAPHORE)
_HBM_SPEC = pl.BlockSpec(memory_space=pltpu.HBM)
_EFFECT = pltpu.CompilerParams(has_side_effects=pltpu.SideEffectType.DATAFLOW_SIDE_EFFECTING)


def _in_hbm(a):
    return pltpu.with_memory_space_constraint(a, pltpu.HBM)


def _chip_copies(x_refs, land_refs, send_sems, recv_sems, loc_sems):
    mx, my, mc = lax.axis_index("x"), lax.axis_index("y"), lax.axis_index("c")
    here = 2 * mx + my
    local, remote = [], []
    for a, (x_ref, land_ref) in enumerate(zip(x_refs, land_refs)):
        local.append(pltpu.make_async_copy(x_ref.at[here], land_ref.at[here], loc_sems.at[a]))
        for j, (px, py) in enumerate([(1 - mx, my), (mx, 1 - my), (1 - mx, 1 - my)]):
            remote.append(pltpu.make_async_remote_copy(
                src_ref=x_ref.at[2 * px + py], dst_ref=land_ref.at[here], send_sem=send_sems.at[3 * a + j],
                recv_sem=recv_sems.at[3 * a + j], device_id=(px, py, mc), device_id_type=MESH))
    return local, remote


def _chips_start(xs, name):
    n = len(xs)

    def body(*refs):
        local, remote = _chip_copies(refs[:n], refs[n:2 * n], *refs[2 * n:2 * n + 3])
        for cp in local + remote:
            cp.start()
        refs[-1][...] = jnp.zeros_like(refs[-1])

    lands = [lax.empty(x.shape, x.dtype) for x in xs]
    outs = pl.pallas_call(
        body, name=name,
        out_shape=(pltpu.SemaphoreType.DMA((3 * n,)), pltpu.SemaphoreType.DMA((3 * n,)), pltpu.SemaphoreType.DMA((n,)),
                   *[pltpu.HBM(x.shape, x.dtype) for x in xs], *[pltpu.HBM(x.shape, x.dtype) for x in xs],
                   jax.ShapeDtypeStruct((8, LANES), F32)),
        in_specs=[_HBM_SPEC] * (2 * n),
        out_specs=(_SEM, _SEM, _SEM, *[_HBM_SPEC] * (2 * n), pl.BlockSpec(memory_space=pltpu.VMEM)),
        input_output_aliases={i: 3 + i for i in range(2 * n)},
        compiler_params=_EFFECT,
    )(*[_in_hbm(x) for x in xs], *[_in_hbm(t) for t in lands])
    return (outs[0], outs[1], outs[2], list(outs[3:3 + n]), list(outs[3 + n:3 + 2 * n])), outs[-1]


def _chips_wait(handle, after, name):
    send_sems, recv_sems, loc_sems, x_thru, land_thru = handle
    n = len(x_thru)

    def body(*refs):
        local, remote = _chip_copies(refs[:n], refs[n:2 * n], *refs[2 * n:2 * n + 3])
        for cp in local:
            cp.wait()
        for cp in remote:
            cp.wait_send()
            cp.wait_recv()

    outs = pl.pallas_call(
        body, name=name,
        out_shape=[pltpu.HBM(x.shape, x.dtype) for x in x_thru + land_thru],
        in_specs=[_HBM_SPEC] * (2 * n) + [_SEM, _SEM, _SEM, _HBM],
        out_specs=[_HBM_SPEC] * (2 * n),
        input_output_aliases={i: i for i in range(2 * n)},
        compiler_params=_EFFECT,
    )(*x_thru, *land_thru, send_sems, recv_sems, loc_sems, after)
    return list(outs[n:])


def _gather_first_copies(x_refs, y_refs, send_sems, sib_recv, ici_recv, loc_sems):
    mx, my, mc = lax.axis_index("x"), lax.axis_index("y"), lax.axis_index("c")
    me = 4 * mx + 2 * my + mc
    local, remote = [], []
    for a, (x_ref, y_ref) in enumerate(zip(x_refs, y_refs)):
        local.append(pltpu.make_async_copy(x_ref, y_ref.at[me], loc_sems.at[a]))
        remote.append(pltpu.make_async_remote_copy(
            src_ref=x_ref, dst_ref=y_ref.at[me], send_sem=send_sems.at[4 * a], recv_sem=sib_recv.at[a],
            device_id=(mx, my, 1 - mc), device_id_type=MESH))
        for j, (px, py) in enumerate([(1 - mx, my), (mx, 1 - my), (1 - mx, 1 - my)]):
            remote.append(pltpu.make_async_remote_copy(
                src_ref=x_ref, dst_ref=y_ref.at[me], send_sem=send_sems.at[4 * a + 1 + j], recv_sem=ici_recv.at[3 * a + j],
                device_id=(px, py, mc), device_id_type=MESH))
    return local, remote


def _gather_forward_copies(y_refs, ici_recv, fwd_send, fwd_recv):
    mx, my, mc = lax.axis_index("x"), lax.axis_index("y"), lax.axis_index("c")
    pairs = []
    for a, y_ref in enumerate(y_refs):
        for j, (px, py) in enumerate([(1 - mx, my), (mx, 1 - my), (1 - mx, 1 - my)]):
            slot = y_ref.at[4 * px + 2 * py + mc]
            arrival = pltpu.make_async_remote_copy(
                src_ref=slot, dst_ref=slot, send_sem=fwd_send.at[3 * a + j], recv_sem=ici_recv.at[3 * a + j],
                device_id=(px, py, mc), device_id_type=MESH)
            onward = pltpu.make_async_remote_copy(
                src_ref=slot, dst_ref=slot, send_sem=fwd_send.at[3 * a + j], recv_sem=fwd_recv.at[3 * a + j],
                device_id=(mx, my, 1 - mc), device_id_type=MESH)
            pairs.append((arrival, onward))
    return pairs


def _gather_start(xs, name, after=()):
    n = len(xs)
    ni = 2 * n + len(after)

    def body(*refs):
        local, remote = _gather_first_copies(refs[:n], refs[n:2 * n], *refs[ni:ni + 4])
        for cp in local + remote:
            cp.start()
        refs[-1][...] = jnp.zeros_like(refs[-1])

    ys = [lax.empty((N_DEV,) + tuple(x.shape), x.dtype) for x in xs]
    dma = pltpu.SemaphoreType.DMA
    outs = pl.pallas_call(
        body, name=name,
        out_shape=(dma((4 * n,)), dma((n,)), dma((3 * n,)), dma((n,)),
                   *[pltpu.HBM(x.shape, x.dtype) for x in xs], *[pltpu.HBM(y.shape, y.dtype) for y in ys],
                   jax.ShapeDtypeStruct((8, LANES), F32)),
        in_specs=[_HBM_SPEC] * (2 * n) + [_HBM] * len(after),
        out_specs=(_SEM, _SEM, _SEM, _SEM, *[_HBM_SPEC] * (2 * n), pl.BlockSpec(memory_space=pltpu.VMEM)),
        input_output_aliases={i: 4 + i for i in range(2 * n)},
        compiler_params=_EFFECT,
    )(*[_in_hbm(x) for x in xs], *[_in_hbm(y) for y in ys], *after)
    return dict(send=outs[0], sib_recv=outs[1], ici_recv=outs[2], loc=outs[3], x=list(outs[4:4 + n]),
                y=list(outs[4 + n:4 + 2 * n]), token=outs[-1])


def _gather_mid(h, after, name):
    n = len(h["y"])

    def body(*refs):
        for arrival, onward in _gather_forward_copies(refs[:n], refs[n], refs[n + 2 + n], refs[n + 3 + n]):
            arrival.wait_recv()
            onward.start()

    dma = pltpu.SemaphoreType.DMA
    outs = pl.pallas_call(
        body, name=name,
        out_shape=(*[pltpu.HBM(y.shape, y.dtype) for y in h["y"]], dma((3 * n,)), dma((3 * n,))),
        in_specs=[_HBM_SPEC] * n + [_SEM, _HBM],
        out_specs=(*[_HBM_SPEC] * n, _SEM, _SEM),
        input_output_aliases={i: i for i in range(n)},
        compiler_params=_EFFECT,
    )(*h["y"], h["ici_recv"], after)
    return dict(h, y=list(outs[:n]), fwd_send=outs[n], fwd_recv=outs[n + 1])


def _gather_wait(h, after, name):
    n = len(h["y"])

    def body(*refs):
        x_refs, y_refs = refs[:n], refs[n:2 * n]
        send, sib_recv, loc, fwd_send, fwd_recv = refs[2 * n:2 * n + 5]
        local, remote = _gather_first_copies(x_refs, y_refs, send, sib_recv, fwd_recv, loc)
        for cp in local:
            cp.wait()
        for k, cp in enumerate(remote):
            cp.wait_send()
            if k % 4 == 0:
                cp.wait_recv()
        for _, onward in _gather_forward_copies(y_refs, fwd_recv, fwd_send, fwd_recv):
            onward.wait_send()
            onward.wait_recv()

    outs = pl.pallas_call(
        body, name=name,
        out_shape=[pltpu.HBM(t.shape, t.dtype) for t in h["x"] + h["y"]],
        in_specs=[_HBM_SPEC] * (2 * n) + [_SEM] * 5 + [_HBM],
        out_specs=[_HBM_SPEC] * (2 * n),
        input_output_aliases={i: i for i in range(2 * n)},
        compiler_params=_EFFECT,
    )(*h["x"], *h["y"], h["send"], h["sib_recv"], h["loc"], h["fwd_send"], h["fwd_recv"], after)
    return list(outs[n:])


def _pair_sum(a, b, name):
    R, C = a.shape
    tm = _tile(R, 512, 16)

    def body(a_ref, b_ref, o_ref):
        o_ref[...] = (a_ref[...].astype(F32) + b_ref[...].astype(F32)).astype(BF16)

    big = pl.BlockSpec((tm, C), lambda i: (i, 0))
    return pl.pallas_call(
        body, name=name, grid=(R // tm,), in_specs=[big, big], out_specs=big,
        out_shape=jax.ShapeDtypeStruct((R, C), BF16), compiler_params=_params(("parallel",)),
    )(a, b)


def _mm(pairs, mode, out_dtype, name, tm=MM_TILE, tn=MM_TILE):
    dims = {"nn": NN, "nt": NT, "tn": TN}[mode]
    a0, b0 = pairs[0][0], pairs[0][1]
    M = a0.shape[1] if mode == "tn" else a0.shape[0]
    N = b0.shape[0] if mode == "nt" else b0.shape[1]
    tm, tn = _tile(M, tm), _tile(N, tn)
    in_specs, args = [], []
    for pr in pairs:
        a, b = pr[0], pr[1]
        if mode == "tn":
            K = a.shape[0]
            in_specs.append(pl.BlockSpec((K, tm), lambda i, j: (0, i)))
            in_specs.append(pl.BlockSpec((K, tn), lambda i, j: (0, j)))
        elif mode == "nn":
            K = a.shape[1]
            rb = pr[2] if len(pr) > 2 else 0
            in_specs.append(pl.BlockSpec((tm, K), lambda i, j: (i, 0)))
            in_specs.append(pl.BlockSpec((K, tn), functools.partial(lambda i, j, rb: (rb, j), rb=rb)))
        else:
            K = a.shape[1]
            cb = pr[2] if len(pr) > 2 else 0
            in_specs.append(pl.BlockSpec((tm, K), lambda i, j: (i, 0)))
            in_specs.append(pl.BlockSpec((tn, K), functools.partial(lambda i, j, cb: (j, cb), cb=cb)))
        args += [a, b]
    n_pairs = len(pairs)

    def body(*refs):
        o_ref = refs[-1]
        acc = None
        for k in range(n_pairs):
            d = _dot(refs[2 * k][...], refs[2 * k + 1][...], dims)
            acc = d if acc is None else acc + d
        o_ref[...] = acc.astype(o_ref.dtype)

    return pl.pallas_call(
        body, name=name, grid=(M // tm, N // tn), in_specs=in_specs,
        out_specs=pl.BlockSpec((tm, tn), lambda i, j: (i, j)),
        out_shape=jax.ShapeDtypeStruct((M, N), out_dtype),
        compiler_params=_params(("parallel", "arbitrary")),
    )(*args)


def _prenorm(x, g, sc, sh, name, first=()):
    T, D = x.shape
    tm = _tile(T, 512, 8)

    def body(x_ref, g_ref, sc_ref, sh_ref, *rest):
        h_ref = rest[-1]
        xv = x_ref[...]
        r = lax.rsqrt(jnp.mean(xv * xv, axis=-1, keepdims=True) + EPS)
        h_ref[...] = (((xv * r) * g_ref[...]) * (1.0 + sc_ref[...]) + sh_ref[...]).astype(BF16)

    row = pl.BlockSpec((1, D), lambda i: (0, 0))
    return pl.pallas_call(
        body, name=name, grid=(T // tm,),
        in_specs=[pl.BlockSpec((tm, D), lambda i: (i, 0)), row, row, row] + [_HBM] * len(first),
        out_specs=pl.BlockSpec((tm, D), lambda i: (i, 0)),
        out_shape=jax.ShapeDtypeStruct((T, D), BF16),
        compiler_params=_params(("parallel",)),
    )(x, g, sc, sh, *first)


def _prenorm_bwd(dh, x, dres, g, sc, name, first=()):
    T, D = x.shape
    tm = _tile(T, 256, 8)

    def body(dh_ref, x_ref, dres_ref, g_ref, sc_ref, *rest):
        dx_ref, sm_ref = rest[-2:]
        @pl.when(pl.program_id(0) == 0)
        def _():
            sm_ref[...] = jnp.zeros_like(sm_ref)

        xv, dhv = x_ref[...], dh_ref[...]
        r = lax.rsqrt(jnp.mean(xv * xv, axis=-1, keepdims=True) + EPS)
        xh = xv * r
        one_sc = 1.0 + sc_ref[...]
        sm_ref[0:1, :] += jnp.sum(dhv, axis=0, keepdims=True)
        sm_ref[1:2, :] += jnp.sum(dhv * (xh * g_ref[...]), axis=0, keepdims=True)
        sm_ref[2:3, :] += jnp.sum(dhv * one_sc * xh, axis=0, keepdims=True)
        dxh = dhv * one_sc * g_ref[...]
        dx_ref[...] = dres_ref[...] + r * (dxh - xh * jnp.mean(dxh * xh, axis=-1, keepdims=True))

    row = pl.BlockSpec((1, D), lambda i: (0, 0))
    big = pl.BlockSpec((tm, D), lambda i: (i, 0))
    return pl.pallas_call(
        body, name=name, grid=(T // tm,),
        in_specs=[big, big, big, row, row] + [_HBM] * len(first),
        out_specs=[big, pl.BlockSpec((8, D), lambda i: (0, 0))],
        out_shape=(jax.ShapeDtypeStruct((T, D), F32), jax.ShapeDtypeStruct((8, D), F32)),
        compiler_params=_params(("arbitrary",)),
    )(dh, x, dres, g, sc, *first)


def _mm_postnorm(a, w, x, gate, gpost, name):
    T, K = a.shape
    D = w.shape[1]
    tm = _tile(T, MM_ROWS_SMALL, 8)

    def body(a_ref, w_ref, x_ref, gate_ref, gp_ref, y_ref, xn_ref):
        y = _dot(a_ref[...], w_ref[...], NN)
        r = lax.rsqrt(jnp.mean(y * y, axis=-1, keepdims=True) + EPS)
        y_ref[...] = y
        xn_ref[...] = x_ref[...] + gate_ref[...] * ((y * r) * gp_ref[...])

    row = pl.BlockSpec((1, D), lambda i: (0, 0))
    big = pl.BlockSpec((tm, D), lambda i: (i, 0))
    return pl.pallas_call(
        body, name=name, grid=(T // tm,),
        in_specs=[pl.BlockSpec((tm, K), lambda i: (i, 0)), pl.BlockSpec((K, D), lambda i: (0, 0)), big, row, row],
        out_specs=[big, big],
        out_shape=(jax.ShapeDtypeStruct((T, D), F32), jax.ShapeDtypeStruct((T, D), F32)),
        compiler_params=_params(("parallel",)),
    )(a, w, x, gate, gpost)


def _postnorm_bwd(dx, y, gate, gpost, name):
    T, D = y.shape
    tm = _tile(T, 256, 8)

    def body(dx_ref, y_ref, gate_ref, gp_ref, dy_ref, sm_ref):
        @pl.when(pl.program_id(0) == 0)
        def _():
            sm_ref[...] = jnp.zeros_like(sm_ref)

        yv, dxv = y_ref[...], dx_ref[...]
        r = lax.rsqrt(jnp.mean(yv * yv, axis=-1, keepdims=True) + EPS)
        yh = yv * r
        dn = dxv * gate_ref[...]
        sm_ref[0:1, :] += jnp.sum(dxv * (yh * gp_ref[...]), axis=0, keepdims=True)
        sm_ref[1:2, :] += jnp.sum(dn * yh, axis=0, keepdims=True)
        dyh = dn * gp_ref[...]
        dy_ref[...] = (r * (dyh - yh * jnp.mean(dyh * yh, axis=-1, keepdims=True))).astype(BF16)

    row = pl.BlockSpec((1, D), lambda i: (0, 0))
    big = pl.BlockSpec((tm, D), lambda i: (i, 0))
    return pl.pallas_call(
        body, name=name, grid=(T // tm,),
        in_specs=[big, big, row, row],
        out_specs=[big, pl.BlockSpec((8, D), lambda i: (0, 0))],
        out_shape=(jax.ShapeDtypeStruct((T, D), BF16), jax.ShapeDtypeStruct((8, D), F32)),
        compiler_params=_params(("arbitrary",)),
    )(dx, y, gate, gpost)


def _ffn_in_fwd(h, w_t, name):
    T, D = h.shape
    F = w_t.shape[0] // 2
    tm, tn = _tile(T, MM_ROWS), _tile(F, FFN_COLS)
    nj = F // tn

    def body(h_ref, wg_ref, wu_ref, g_ref, u_ref, act_ref):
        hv = h_ref[...]
        g = _dot(hv, wg_ref[...], NT)
        u = _dot(hv, wu_ref[...], NT)
        g_ref[...] = g.astype(BF16)
        u_ref[...] = u.astype(BF16)
        act_ref[...] = ((g * _sigmoid(g)) * u).astype(BF16)

    out = pl.BlockSpec((tm, tn), lambda j, i: (i, j))
    return pl.pallas_call(
        body, name=name, grid=(nj, T // tm),
        in_specs=[pl.BlockSpec((tm, D), lambda j, i: (i, 0)),
                  pl.BlockSpec((tn, D), lambda j, i: (j, 0)),
                  pl.BlockSpec((tn, D), lambda j, i: (j + nj, 0))],
        out_specs=[out, out, out],
        out_shape=tuple(jax.ShapeDtypeStruct((T, F), BF16) for _ in range(3)),
        compiler_params=_params(("parallel", "arbitrary")),
    )(h, w_t, w_t)


def _ffn_act_bwd(dy, w_out, g, u, name):
    T, D = dy.shape
    F = w_out.shape[0]
    tm, tn = _tile(T, MM_ROWS), _tile(F, FFN_COLS)

    def body(dy_ref, w_ref, g_ref, u_ref, dg_ref, du_ref):
        dact = _dot(dy_ref[...], w_ref[...], NT)
        gv, uv = g_ref[...].astype(F32), u_ref[...].astype(F32)
        sg = _sigmoid(gv)
        dg_ref[...] = (dact * uv * (sg * (1.0 + gv * (1.0 - sg)))).astype(BF16)
        du_ref[...] = (dact * (gv * sg)).astype(BF16)

    tile = pl.BlockSpec((tm, tn), lambda j, i: (i, j))
    return pl.pallas_call(
        body, name=name, grid=(F // tn, T // tm),
        in_specs=[pl.BlockSpec((tm, D), lambda j, i: (i, 0)), pl.BlockSpec((tn, D), lambda j, i: (j, 0)), tile, tile],
        out_specs=[tile, tile],
        out_shape=(jax.ShapeDtypeStruct((T, F), BF16), jax.ShapeDtypeStruct((T, F), BF16)),
        compiler_params=_params(("parallel", "arbitrary")),
    )(dy, w_out, g, u)


def _lane_scan(v, reverse):
    T = v.shape[-1]
    lane = lax.broadcasted_iota(jnp.int32, v.shape, 1)
    d = 1
    while d < T:
        if reverse:
            v = v + jnp.where(lane < T - d, pltpu.roll(v, T - d, axis=1), 0.0)
        else:
            v = v + jnp.where(lane >= d, pltpu.roll(v, d, axis=1), 0.0)
        d *= 2
    return v


def _fgate_fwd(h, wf_t, bf, name):
    T, D = h.shape
    R = wf_t.shape[0]

    def body(h_ref, w_ref, b_ref, fl_ref, cum_ref):
        fl = _dot(w_ref[...], h_ref[...], NT) + b_ref[...]
        fl_ref[...] = fl
        logf = jnp.minimum(fl, 0.0) - jnp.log(1.0 + jnp.exp(-jnp.abs(fl)))
        cum_ref[...] = _lane_scan(logf, reverse=False)

    return pl.pallas_call(
        body, name=name,
        out_shape=(jax.ShapeDtypeStruct((R, T), F32), jax.ShapeDtypeStruct((R, T), F32)),
        compiler_params=_params(),
    )(h, wf_t, bf)


def _fgate_bwd(dcum, fl, h, name):
    R, T = fl.shape
    D = h.shape[1]

    def body(dc_ref, fl_ref, h_ref, dfl_ref, dw_ref, db_ref):
        dlogf = _lane_scan(dc_ref[...], reverse=True)
        dfl = dlogf * _sigmoid(-fl_ref[...])
        dfl_ref[...] = dfl
        dw_ref[...] = _dot(dfl.astype(BF16), h_ref[...], NN)
        db_ref[...] = jnp.broadcast_to(jnp.sum(dfl, axis=-1, keepdims=True), (R, LANES))

    return pl.pallas_call(
        body, name=name,
        out_shape=(jax.ShapeDtypeStruct((R, T), F32), jax.ShapeDtypeStruct((R, D), F32),
                   jax.ShapeDtypeStruct((R, LANES), F32)),
        compiler_params=_params(),
    )(dcum, fl, h)


def _head_masks(hpb, dh, rows):
    lane = lax.broadcasted_iota(jnp.int32, (rows, LANES), 1)
    return [(lane >= h * dh) & (lane < (h + 1) * dh) for h in range(hpb)]


def _stack_heads(v, masks):
    return jnp.concatenate([jnp.where(mk, v, jnp.zeros_like(v)) for mk in masks], axis=0)


def _heads_to_lanes(col, masks, tq):
    out = jnp.broadcast_to(col[0:tq], (tq, LANES))
    for h in range(1, len(masks)):
        out = jnp.where(masks[h], col[h * tq:(h + 1) * tq], out)
    return out


def _causal_stack(hpb, tq):
    r = lax.broadcasted_iota(jnp.int32, (tq, tq), 0)
    c = lax.broadcasted_iota(jnp.int32, (tq, tq), 1)
    return jnp.concatenate([c] * hpb, axis=0) <= jnp.concatenate([r] * hpb, axis=0)


def _attn_fwd(qkv, cum_rows, n_heads, name, tq):
    T = qkv.shape[0]
    A = qkv.shape[1] // 3
    dh = A // n_heads
    hpb = LANES // dh
    nb = A // LANES
    nq = T // tq
    scale = dh ** -0.5

    def body(q_ref, k_ref, v_ref, c_ref, o_ref, l_ref, vbd):
        hp, i = pl.program_id(0), pl.program_id(1)
        masks = _head_masks(hpb, dh, tq)

        @pl.when(i == 0)
        def _():
            def fill(j, _):
                vbd[j] = _stack_heads(v_ref[pl.ds(pl.multiple_of(j * tq, tq), tq), :], masks)
                return 0

            lax.fori_loop(0, nq, fill, 0)

        qs = _stack_heads(q_ref[...], masks)
        crow0 = hp * hpb * nq

        def tile(j, carry, diag):
            m, l, acc = carry
            kt = k_ref[pl.ds(pl.multiple_of(j * tq, tq), tq), :]
            bias = jnp.concatenate(
                [jnp.broadcast_to(c_ref[pl.ds(crow0 + h * nq + j, 1), :], (tq, tq)) for h in range(hpb)], axis=0)
            s = _dot(qs, kt, NT) * scale - bias
            if diag:
                s = jnp.where(_causal_stack(hpb, tq), s, NEG)
            m_new = jnp.maximum(m, jnp.max(s, axis=-1, keepdims=True))
            p = jnp.exp(s - m_new)
            alpha = jnp.exp(m - m_new)
            l = alpha * l + jnp.sum(p, axis=-1, keepdims=True)
            pcat = jnp.concatenate([p[h * tq:(h + 1) * tq] for h in range(hpb)], axis=1).astype(BF16)
            acc = _heads_to_lanes(alpha, masks, tq) * acc + _dot(pcat, vbd[j], NN)
            return m_new, l, acc

        init = (jnp.full((hpb * tq, 1), NEG, F32), jnp.zeros((hpb * tq, 1), F32), jnp.zeros((tq, LANES), F32))
        carry = lax.fori_loop(0, i, lambda j, c: tile(j, c, False), init)
        m, l, acc = tile(i, carry, True)
        o_ref[...] = (acc / _heads_to_lanes(l, masks, tq)).astype(BF16)
        lse = m + jnp.log(l)
        for h in range(hpb):
            l_ref[:, h:h + 1] = lse[h * tq:(h + 1) * tq]

    return pl.pallas_call(
        body, name=name, grid=(nb, nq),
        in_specs=[pl.BlockSpec((tq, LANES), lambda h, i: (i, h)),
                  pl.BlockSpec((T, LANES), lambda h, i: (0, nb + h)),
                  pl.BlockSpec((T, LANES), lambda h, i: (0, 2 * nb + h)),
                  pl.BlockSpec(cum_rows.shape, lambda h, i: (0, 0))],
        out_specs=[pl.BlockSpec((tq, LANES), lambda h, i: (i, h)),
                   pl.BlockSpec((None, tq, hpb), lambda h, i: (h, i, 0))],
        out_shape=(jax.ShapeDtypeStruct((T, A), BF16), jax.ShapeDtypeStruct((nb, T, hpb), F32)),
        scratch_shapes=[pltpu.VMEM((nq, hpb * tq, LANES), BF16)],
        compiler_params=_params(("arbitrary", "arbitrary")),
    )(qkv, qkv, qkv, cum_rows)


def _attn_bwd(qkv, dcat, o, lse, cum_rows, n_heads, name, tq):
    T = qkv.shape[0]
    A = qkv.shape[1] // 3
    dh = A // n_heads
    hpb = LANES // dh
    nb = A // LANES
    nq = T // tq
    scale = dh ** -0.5

    def body(q_ref, k_ref, v_ref, do_ref, o_ref, l_ref, c_ref, dqkv_ref, dc_ref, dr_ref,
             dq_acc, delta, drow, qs_scr, dos_scr, kbd_scr):
        hp = pl.program_id(0)
        masks = _head_masks(hpb, dh, tq)
        crow0 = hp * hpb * nq

        def prologue(i, _):
            rs = pl.ds(pl.multiple_of(i * tq, tq), tq)
            do = do_ref[rs, :]
            prod = do * o_ref[rs, :].astype(F32)
            for h in range(hpb):
                delta[rs, h:h + 1] = jnp.sum(jnp.where(masks[h], prod, 0.0), axis=-1, keepdims=True)
            qs_scr[i] = _stack_heads(q_ref[rs, :], masks)
            dos_scr[i] = _stack_heads(do, masks).astype(BF16)
            kbd_scr[i] = _stack_heads(k_ref[rs, :], masks)
            dq_acc[rs, :] = jnp.zeros((tq, LANES), F32)
            drow[rs, :] = jnp.zeros((tq, hpb), F32)
            return 0

        lax.fori_loop(0, nq, prologue, 0)

        def kv_step(j, _):
            ks = pl.ds(pl.multiple_of(j * tq, tq), tq)
            kt, vt = k_ref[ks, :], v_ref[ks, :]
            kbd = kbd_scr[j]
            bias = jnp.concatenate(
                [jnp.broadcast_to(c_ref[pl.ds(crow0 + h * nq + j, 1), :], (tq, tq)) for h in range(hpb)], axis=0)

            def q_step(i, carry, diag):
                dk, dv, dcs = carry
                rs = pl.ds(pl.multiple_of(i * tq, tq), tq)
                qs, dos = qs_scr[i], dos_scr[i]
                s = _dot(qs, kt, NT) * scale - bias
                if diag:
                    s = jnp.where(_causal_stack(hpb, tq), s, NEG)
                lse = jnp.concatenate([l_ref[rs, h:h + 1] for h in range(hpb)], axis=0)
                p = jnp.exp(s - lse)
                dv = dv + _dot(p.astype(BF16), dos, TN)
                dp = _dot(dos, vt, NT)
                ds = p * (dp - jnp.concatenate([delta[rs, h:h + 1] for h in range(hpb)], axis=0))
                dcs = tuple(dcs[h] - jnp.sum(ds[h * tq:(h + 1) * tq], axis=0, keepdims=True) for h in range(hpb))
                rsum = jnp.sum(ds, axis=-1, keepdims=True)
                for h in range(hpb):
                    drow[rs, h:h + 1] += rsum[h * tq:(h + 1) * tq]
                dsb = (ds * scale).astype(BF16)
                dk = dk + _dot(dsb, qs, TN)
                dscat = jnp.concatenate([dsb[h * tq:(h + 1) * tq] for h in range(hpb)], axis=1)
                dq_acc[rs, :] += _dot(dscat, kbd, NN)
                return dk, dv, dcs

            init = (jnp.zeros((tq, LANES), F32), jnp.zeros((tq, LANES), F32),
                    tuple(jnp.zeros((1, tq), F32) for _ in range(hpb)))
            carry = q_step(j, init, True)
            dk, dv, dcs = lax.fori_loop(j + 1, nq, lambda i, c: q_step(i, c, False), carry)
            dqkv_ref[1, ks, :] = dk.astype(BF16)
            dqkv_ref[2, ks, :] = dv.astype(BF16)
            for h in range(hpb):
                dc_ref[pl.ds(crow0 + h * nq + j, 1), :] = dcs[h]
            return 0

        lax.fori_loop(0, nq, kv_step, 0)
        dqkv_ref[0] = dq_acc[...].astype(BF16)
        dr_ref[...] = drow[...]

    col = lambda off: pl.BlockSpec((T, LANES), functools.partial(lambda h, off: (0, off + h), off=off))
    return pl.pallas_call(
        body, name=name, grid=(nb,),
        in_specs=[col(0), col(nb), col(2 * nb), col(0), col(0),
                  pl.BlockSpec((None, T, hpb), lambda h: (h, 0, 0)),
                  pl.BlockSpec(cum_rows.shape, lambda h: (0, 0))],
        out_specs=[pl.BlockSpec((3, T, LANES), lambda h: (0, 0, h)),
                   pl.BlockSpec(cum_rows.shape, lambda h: (0, 0)),
                   pl.BlockSpec((None, T, hpb), lambda h: (h, 0, 0))],
        out_shape=(jax.ShapeDtypeStruct((3, T, A), BF16), jax.ShapeDtypeStruct(cum_rows.shape, F32),
                   jax.ShapeDtypeStruct((nb, T, hpb), F32)),
        scratch_shapes=[pltpu.VMEM((T, LANES), F32), pltpu.VMEM((T, hpb), F32), pltpu.VMEM((T, hpb), F32),
                        pltpu.VMEM((nq, hpb * tq, LANES), BF16), pltpu.VMEM((nq, hpb * tq, LANES), BF16),
                        pltpu.VMEM((nq, hpb * tq, LANES), BF16)],
        compiler_params=_params(("arbitrary",)),
    )(qkv, qkv, qkv, dcat, o, lse, cum_rows)


def _glu_into(upad, cv_ref, cg_ref, T):
    upad[0:CONV_PAD, :] = jnp.zeros((CONV_PAD, upad.shape[1]), F32)

    def fill(c, _):
        rs = pl.ds(pl.multiple_of(c * CONV_CHUNK, CONV_CHUNK), CONV_CHUNK)
        upad[pl.ds(pl.multiple_of(CONV_PAD + c * CONV_CHUNK, 8), CONV_CHUNK), :] = cv_ref[rs, :] * _sigmoid(cg_ref[rs, :])
        return 0

    lax.fori_loop(0, T // CONV_CHUNK, fill, 0)


SUBLANES = 8
CONV_SHIFT_ROWS = CONV_CHUNK + CONV_PAD - SUBLANES


def _load_window(win, sh, src, r0):
    win[...] = src[pl.ds(r0, CONV_CHUNK + CONV_PAD), :]
    for b in range(1, SUBLANES):
        sh[b - 1] = win[b:b + CONV_SHIFT_ROWS, :]


def _tap(win, sh, o):
    b = o % SUBLANES
    if b == 0:
        return win[o:o + CONV_CHUNK, :]
    return sh[b - 1, o - b:o - b + CONV_CHUNK, :]


def _conv_taps(win, sh, w_ref, first, step):
    acc = None
    for k in range(CONV_K):
        t = w_ref[k:k + 1, :] * _tap(win, sh, first + step * k)
        acc = t if acc is None else acc + t
    return acc


def _conv_scratch(C):
    return [pltpu.VMEM((CONV_CHUNK + CONV_PAD, C), F32), pltpu.VMEM((SUBLANES - 1, CONV_SHIFT_ROWS, C), F32)]


def _conv_fwd(cproj, w, b, lg, lb, name):
    T = cproj.shape[0]
    C = cproj.shape[1] // 2
    off = CONV_PAD - (CONV_K - 1)

    def body(cv_ref, cg_ref, w_ref, b_ref, lg_ref, lb_ref, out_ref, upad, win, sh):
        _glu_into(upad, cv_ref, cg_ref, T)

        def chunk(c, _):
            r0 = pl.multiple_of(c * CONV_CHUNK, CONV_CHUNK)
            _load_window(win, sh, upad, r0)
            u1 = _conv_taps(win, sh, w_ref, off, 1) + b_ref[...]
            mu = jnp.mean(u1, axis=-1, keepdims=True)
            var = jnp.mean(jnp.square(u1 - mu), axis=-1, keepdims=True)
            u2 = ((u1 - mu) * lax.rsqrt(var + EPS)) * lg_ref[...] + lb_ref[...]
            out_ref[pl.ds(r0, CONV_CHUNK), :] = (u2 * _sigmoid(u2)).astype(BF16)
            return 0

        lax.fori_loop(0, T // CONV_CHUNK, chunk, 0)

    row = pl.BlockSpec((1, C), lambda i: (0, 0))
    return pl.pallas_call(
        body, name=name, grid=(1,),
        in_specs=[pl.BlockSpec((T, C), lambda i: (0, 0)), pl.BlockSpec((T, C), lambda i: (0, 1)),
                  pl.BlockSpec(w.shape, lambda i: (0, 0)), row, row, row],
        out_specs=pl.BlockSpec((T, C), lambda i: (0, 0)),
        out_shape=jax.ShapeDtypeStruct((T, C), BF16),
        scratch_shapes=[pltpu.VMEM((T + CONV_PAD, C), F32)] + _conv_scratch(C),
        compiler_params=_params(("arbitrary",)),
    )(cproj, cproj, w, b, lg, lb)


def _conv_bwd(cproj, dcat, w, b, lg, lb, name):
    T = cproj.shape[0]
    C = cproj.shape[1] // 2
    off = CONV_PAD - (CONV_K - 1)
    n_chunks = T // CONV_CHUNK

    def fold(v):
        return jnp.sum(v.reshape(CONV_CHUNK // 8, 8, C), axis=0)

    def body(cv_ref, cg_ref, du_ref, w_ref, b_ref, lg_ref, lb_ref, dc_ref, dw_ref, sm_ref,
             upad, dpad, dwacc, smacc, win, sh):
        _glu_into(upad, cv_ref, cg_ref, T)
        dpad[pl.ds(T, CONV_PAD), :] = jnp.zeros((CONV_PAD, C), F32)
        dwacc[...] = jnp.zeros_like(dwacc)
        smacc[...] = jnp.zeros_like(smacc)

        def chunk_a(c, _):
            r0 = pl.multiple_of(c * CONV_CHUNK, CONV_CHUNK)
            _load_window(win, sh, upad, r0)
            u1 = _conv_taps(win, sh, w_ref, off, 1) + b_ref[...]
            mu = jnp.mean(u1, axis=-1, keepdims=True)
            var = jnp.mean(jnp.square(u1 - mu), axis=-1, keepdims=True)
            rstd = lax.rsqrt(var + EPS)
            u1h = (u1 - mu) * rstd
            u2 = u1h * lg_ref[...] + lb_ref[...]
            sg = _sigmoid(u2)
            du2 = du_ref[pl.ds(r0, CONV_CHUNK), :] * (sg * (1.0 + u2 * (1.0 - sg)))
            smacc[8:16, :] += fold(du2 * u1h)
            smacc[16:24, :] += fold(du2)
            du1h = du2 * lg_ref[...]
            du1 = rstd * (du1h - jnp.mean(du1h, axis=-1, keepdims=True)
                          - u1h * jnp.mean(du1h * u1h, axis=-1, keepdims=True))
            smacc[0:8, :] += fold(du1)
            dpad[pl.ds(r0, CONV_CHUNK), :] = du1
            for k in range(CONV_K):
                dwacc[8 * k:8 * k + 8, :] += fold(du1 * _tap(win, sh, off + k))
            return 0

        lax.fori_loop(0, n_chunks, chunk_a, 0)

        def chunk_b(c, _):
            r0 = pl.multiple_of(c * CONV_CHUNK, CONV_CHUNK)
            rs = pl.ds(r0, CONV_CHUNK)
            _load_window(win, sh, dpad, r0)
            du0 = _conv_taps(win, sh, w_ref, CONV_K - 1, -1)
            cv, sg = cv_ref[rs, :], _sigmoid(cg_ref[rs, :])
            dc_ref[rs, 0:C] = (du0 * sg).astype(BF16)
            dc_ref[rs, C:2 * C] = (du0 * cv * (sg * (1.0 - sg))).astype(BF16)
            return 0

        lax.fori_loop(0, n_chunks, chunk_b, 0)
        for k in range(CONV_K):
            dw_ref[k:k + 1, :] = jnp.sum(dwacc[8 * k:8 * k + 8, :], axis=0, keepdims=True)
        dw_ref[CONV_K:CONV_PAD, :] = jnp.zeros((CONV_PAD - CONV_K, C), F32)
        for r in range(3):
            sm_ref[r:r + 1, :] = jnp.sum(smacc[8 * r:8 * r + 8, :], axis=0, keepdims=True)
        sm_ref[3:8, :] = jnp.zeros((5, C), F32)

    row = pl.BlockSpec((1, C), lambda i: (0, 0))
    return pl.pallas_call(
        body, name=name, grid=(1,),
        in_specs=[pl.BlockSpec((T, C), lambda i: (0, 0)), pl.BlockSpec((T, C), lambda i: (0, 1)),
                  pl.BlockSpec((T, C), lambda i: (0, 1)),
                  pl.BlockSpec(w.shape, lambda i: (0, 0)), row, row, row],
        out_specs=[pl.BlockSpec((T, 2 * C), lambda i: (0, 0)), pl.BlockSpec((CONV_PAD, C), lambda i: (0, 0)),
                   pl.BlockSpec((8, C), lambda i: (0, 0))],
        out_shape=(jax.ShapeDtypeStruct((T, 2 * C), BF16), jax.ShapeDtypeStruct((CONV_PAD, C), F32),
                   jax.ShapeDtypeStruct((8, C), F32)),
        scratch_shapes=[pltpu.VMEM((T + CONV_PAD, C), F32), pltpu.VMEM((T + CONV_PAD, C), F32),
                        pltpu.VMEM((8 * CONV_PAD, C), F32), pltpu.VMEM((24, C), F32)] + _conv_scratch(C),
        compiler_params=_params(("arbitrary",)),
    )(cproj, cproj, dcat, w, b, lg, lb)


def _loss_head(x, target, name):
    T, D = x.shape
    tm = _tile(T, 512, 8)

    def body(x_ref, t_ref, loss_ref, dx_ref):
        @pl.when(pl.program_id(0) == 0)
        def _():
            loss_ref[...] = jnp.zeros_like(loss_ref)

        err = x_ref[...] - t_ref[...]
        part = jnp.sum(jnp.mean(err * err, axis=-1, keepdims=True), axis=0, keepdims=True)
        loss_ref[...] += jnp.broadcast_to(0.5 * part, loss_ref.shape)
        dx_ref[...] = err * (1.0 / D)

    big = pl.BlockSpec((tm, D), lambda i: (i, 0))
    return pl.pallas_call(
        body, name=name, grid=(T // tm,),
        in_specs=[big, big],
        out_specs=[pl.BlockSpec((8, LANES), lambda i: (0, 0)), big],
        out_shape=(jax.ShapeDtypeStruct((8, LANES), F32), jax.ShapeDtypeStruct((T, D), F32)),
        compiler_params=_params(("arbitrary",)),
    )(x, target)


def _ada_fwd(c_all, ada_w, ada_b_loc, name):
    L, D, S = ada_w.shape
    B = c_all.shape[0]

    def body(c_ref, w_ref, b_ref, o_ref):
        c = c_ref[...]
        ca = (c * _sigmoid(c)).astype(BF16)
        o_ref[...] = _dot(ca, w_ref[...].astype(BF16), NN) + b_ref[...]

    return pl.pallas_call(
        body, name=name, grid=(L,),
        in_specs=[pl.BlockSpec((B, D), lambda l: (0, 0)), pl.BlockSpec((None, D, S), lambda l: (l, 0, 0)),
                  pl.BlockSpec((None, 1, S), lambda l: (l, 0, 0))],
        out_specs=pl.BlockSpec((None, B, S), lambda l: (l, 0, 0)),
        out_shape=jax.ShapeDtypeStruct((L, B, S), F32),
        compiler_params=_params(("parallel",)),
    )(c_all, ada_w, ada_b_loc)


def _ada_bwd(c_all_t, dmod_loc, name):
    D, B = c_all_t.shape
    L, _, S = dmod_loc.shape

    def body(c_ref, dm_ref, o_ref):
        c = c_ref[...]
        ca = c * _sigmoid(c)
        acc = None
        for bb in range(B):
            t = ca[:, bb:bb + 1] * dm_ref[bb:bb + 1, :]
            acc = t if acc is None else acc + t
        o_ref[...] = acc

    return pl.pallas_call(
        body, name=name, grid=(L,),
        in_specs=[pl.BlockSpec((D, B), lambda l: (0, 0)), pl.BlockSpec((None, B, S), lambda l: (l, 0, 0))],
        out_specs=pl.BlockSpec((None, D, S), lambda l: (l, 0, 0)),
        out_shape=jax.ShapeDtypeStruct((L, D, S), F32),
        compiler_params=_params(("parallel",)),
    )(c_all_t, dmod_loc)


def _sum_devices(parts, name):
    _, R, C = parts.shape
    tm = _tile(R, 256, 8)

    def body(p_ref, o_ref):
        acc = p_ref[0].astype(F32)
        for d in range(1, N_DEV):
            acc = acc + p_ref[d].astype(F32)
        o_ref[...] = acc

    return pl.pallas_call(
        body, name=name, grid=(R // tm,),
        in_specs=[pl.BlockSpec((N_DEV, tm, C), lambda i: (0, i, 0))],
        out_specs=pl.BlockSpec((tm, C), lambda i: (i, 0)),
        out_shape=jax.ShapeDtypeStruct((R, C), F32),
        compiler_params=_params(("parallel",)),
    )(parts)


def _adamw_math(w, g, m, v):
    m = ADAM_B1 * m + (1.0 - ADAM_B1) * g
    v = ADAM_B2 * v + (1.0 - ADAM_B2) * (g * g)
    m_hat = m / (1.0 - ADAM_B1 ** ADAM_STEP)
    v_hat = v / (1.0 - ADAM_B2 ** ADAM_STEP)
    delta = -ADAM_LR * (m_hat / (jnp.sqrt(v_hat) + ADAM_EPS) + ADAM_WD * w)
    return delta, m, v


def _adamw(w, g, m, v, name, summed):
    R, C = w.shape
    tm = _tile(R, 256, 16)
    n_parts = g.shape[0] if summed else 0

    def body(w_ref, g_ref, m_ref, v_ref, go_ref, d_ref, mo_ref, vo_ref):
        if summed:
            g = g_ref[0].astype(F32)
            for d in range(1, n_parts):
                g = g + g_ref[d].astype(F32)
        else:
            g = g_ref[...]
        delta, mn, vn = _adamw_math(w_ref[...], g, m_ref[...], v_ref[...])
        go_ref[...] = g
        d_ref[...] = delta
        mo_ref[...] = mn
        vo_ref[...] = vn

    big = pl.BlockSpec((tm, C), lambda i: (i, 0))
    gspec = pl.BlockSpec((n_parts, tm, C), lambda i: (0, i, 0)) if summed else big
    return pl.pallas_call(
        body, name=name, grid=(R // tm,),
        in_specs=[big, gspec, big, big],
        out_specs=[big, big, big, big],
        out_shape=tuple(jax.ShapeDtypeStruct((R, C), F32) for _ in range(4)),
        compiler_params=_params(("parallel",)),
    )(w, g, m, v)


def _adamw_summed(w, parts, m, v, name):
    L = len(parts)
    n_parts, R, C = parts[0].shape
    tm = _tile(R, 256, 16)
    nr = R // tm

    def body(*refs):
        w_ref, g_refs = refs[0], refs[1:1 + L]
        m_ref, v_ref, go_ref, d_ref, mo_ref, vo_ref = refs[1 + L:]
        for ll in range(L):
            @pl.when(pl.program_id(0) == ll)
            def _(ll=ll):
                g = g_refs[ll][0].astype(F32)
                for d in range(1, n_parts):
                    g = g + g_refs[ll][d].astype(F32)
                delta, mn, vn = _adamw_math(w_ref[...], g, m_ref[...], v_ref[...])
                go_ref[...] = g
                d_ref[...] = delta
                mo_ref[...] = mn
                vo_ref[...] = vn

    big = pl.BlockSpec((None, tm, C), lambda l, i: (l, i, 0))
    gspecs = [pl.BlockSpec((n_parts, tm, C), functools.partial(lambda l, i, ll: (0, jnp.where(l == ll, i, 0), 0), ll=ll))
              for ll in range(L)]
    return pl.pallas_call(
        body, name=name, grid=(L, nr),
        in_specs=[big, *gspecs, big, big],
        out_specs=[big, big, big, big],
        out_shape=tuple(jax.ShapeDtypeStruct((L, R, C), F32) for _ in range(4)),
        compiler_params=_params(("arbitrary", "arbitrary")),
    )(w, *parts, m, v)


def _pack(arrs, D):
    L = arrs[0].shape[0]
    cols = []
    for a in arrs:
        f = a.reshape(L, -1)
        n = f.shape[1]
        cols.append(jnp.pad(f, ((0, 0), (0, -(-n // D) * D - n))))
    flat = jnp.concatenate(cols, axis=1)
    return flat.reshape(L, flat.shape[1] // D, D)


def _unpack(p, shapes, D):
    L = p.shape[0]
    out, r = [], 0
    for s in shapes:
        n = math.prod(s[1:])
        rows = -(-n // D)
        out.append(p[:, r:r + rows].reshape(L, rows * D)[:, :n].reshape(s))
        r += rows
    return out


def kernel(x, c, w_in, b_f, conv_w, conv_b, conv_ln_g, conv_ln_b, w_o, w_ffn_in, w_ffn_out, mix_pre_g, mix_post_g, ffn_pre_g, ffn_post_g, ada_w, ada_b, loss_target, m_w_in, m_b_f, m_conv_w, m_conv_b, m_conv_ln_g, m_conv_ln_b, m_w_o, m_w_ffn_in, m_w_ffn_out, m_mix_pre_g, m_mix_post_g, m_ffn_pre_g, m_ffn_post_g, m_ada_w, m_ada_b, v_w_in, v_b_f, v_conv_w, v_conv_b, v_conv_ln_g, v_conv_ln_b, v_w_o, v_w_ffn_in, v_w_ffn_out, v_mix_pre_g, v_mix_post_g, v_ffn_pre_g, v_ffn_post_g, v_ada_w, v_ada_b):
    L, D, s_in = w_in.shape
    T = x.shape[1]
    H = b_f.shape[1]
    A = D // 2
    C = D - A
    cs = conv_w.shape[2]
    F = w_ffn_out.shape[1] * N_DEV
    s_ff = w_ffn_in.shape[2]
    w_ffn_in_t = jnp.transpose(w_ffn_in, (0, 2, 1))
    s_ada = ada_w.shape[2]
    R = 16
    me = _my_index()
    x0 = x[0]
    target = loss_target[0]
    tq = _tile(T, 512)
    nq = T // tq

    c_all = _exchange(c, gather=True, name="gather_c").reshape(N_DEV, D)
    ada_b_loc = lax.dynamic_slice_in_dim(ada_b, me * s_ada, s_ada, axis=1)[:, None, :]
    mod_loc = _ada_fwd(c_all, ada_w, ada_b_loc, "ada_fwd")
    mod_g = _exchange(mod_loc, gather=True, name="gather_mod")
    mod = lax.dynamic_index_in_dim(mod_g, me, axis=2, keepdims=False)
    mod = jnp.transpose(mod, (1, 0, 2)).reshape(L, N_MOD, 1, D)
    cw_g = _exchange(conv_w, gather=True, name="gather_conv_w")
    conv_w_full = jnp.transpose(cw_g, (1, 2, 0, 3)).reshape(L, CONV_K, C)
    conv_w_pad = jnp.pad(conv_w_full, ((0, 0), (0, CONV_PAD - CONV_K), (0, 0)))
    b_f_col = jnp.pad(b_f, ((0, 0), (0, R - H)))[:, :, None]

    h_gmix, h_gffn, chain = [], [], (mod_g, cw_g)
    for l in range(L):
        h_gmix.append(_gather_start([w_in[l].astype(BF16), w_o[l].astype(BF16)], f"gather_mix_{l}", chain))
        h_gffn.append(_gather_start([w_ffn_in_t[l].astype(BF16), w_ffn_out[l].astype(BF16)], f"gather_ffn_{l}",
                                    (h_gmix[l]["token"],)))
        chain = (h_gffn[l]["token"],)
    gather_tokens = chain

    def by_cols(g):
        return jnp.transpose(g, (1, 0, 2)).reshape(g.shape[1], N_DEV * g.shape[2])

    def by_rows(g):
        return g.reshape(N_DEV * g.shape[1], g.shape[2])

    W_qkv, W_f_t, W_f_pad, W_c, W_o, W_ffn_in_t, W_ffn_out = ([None] * L for _ in range(7))

    saved = []
    xc = x0
    for l in range(L):
        sh1, sc1, g1, sh2, sc2, g2 = (mod[l, k] for k in range(N_MOD))
        gpre1, gpost1, gpre2, gpost2 = (p[l][None, :] for p in (mix_pre_g, mix_post_g, ffn_pre_g, ffn_post_g))
        dep = gather_tokens[0] if l == 0 else xc
        g_in, g_o = _gather_wait(_gather_mid(h_gmix[l], dep, f"gather_mid_mix_{l}"), dep, f"gather_wait_mix_{l}")
        W_in_l = by_cols(g_in)
        W_qkv[l], W_f, W_c[l], W_o[l] = W_in_l[:, :3 * A], W_in_l[:, 3 * A:3 * A + H], W_in_l[:, 3 * A + H:], by_rows(g_o)
        W_f_t[l] = jnp.pad(jnp.transpose(W_f), ((0, R - H), (0, 0)))
        W_f_pad[l] = jnp.pad(W_f, ((0, 0), (0, R - H)))
        h1 = _prenorm(xc, gpre1, sc1, sh1, f"prenorm1_{l}", first=gather_tokens if l == 0 else ())
        qkv = _mm([(h1, W_qkv[l])], "nn", BF16, f"proj_qkv_{l}", tm=MM_ROWS)
        cproj = _mm([(h1, W_c[l])], "nn", F32, f"proj_conv_{l}", tm=MM_ROWS, tn=MM_COLS)
        fl, cum = _fgate_fwd(h1, W_f_t[l], b_f_col[l], f"fgate_{l}")
        cum_rows = cum[:H].reshape(H * nq, tq)
        o, lse = _attn_fwd(qkv, cum_rows, H, f"attn_{l}", tq)
        h_ffn_l = _gather_mid(h_gffn[l], o, f"gather_mid_ffn_{l}")
        u3 = _conv_fwd(cproj, conv_w_pad[l], conv_b[l][None, :], conv_ln_g[l][None, :], conv_ln_b[l][None, :], f"conv_{l}")
        cat = jnp.concatenate([o, u3], axis=-1)
        y1, x_mid = _mm_postnorm(cat, W_o[l], xc, g1, gpost1, f"out_proj_{l}")
        g_fi, g_fo = _gather_wait(h_ffn_l, x_mid, f"gather_wait_ffn_{l}")
        W_ffn_in_t[l], W_ffn_out[l] = by_rows(g_fi), by_rows(g_fo)
        h2 = _prenorm(x_mid, gpre2, sc2, sh2, f"prenorm2_{l}")
        g, u, act = _ffn_in_fwd(h2, W_ffn_in_t[l], f"ffn_in_{l}")
        y2, x_out = _mm_postnorm(act, W_ffn_out[l], x_mid, g2, gpost2, f"ffn_out_{l}")
        saved.append((xc, h1, qkv, cproj, fl, cum_rows, o, lse, cat, y1, x_mid, h2, g, u, act, y2))
        xc = x_out

    loss_tile, dx = _loss_head(xc, target, "loss_head")
    loss = lax.psum(loss_tile[0, 0], ("x", "y", "c"))

    mc = lax.axis_index("c")

    def reduce_start(parts, name):
        keeps, sends = [], []
        for part in parts:
            part4 = part.reshape(4, 2, *part.shape[1:])
            keeps.append(lax.dynamic_index_in_dim(part4, mc, axis=1, keepdims=False))
            sends.append(lax.dynamic_index_in_dim(part4, 1 - mc, axis=1, keepdims=False))
        gots = _swap_sibling(sends, "swap_" + name)
        boths = [_pair_sum(k.reshape(-1, k.shape[-1]), t.reshape(-1, t.shape[-1]), f"pairsum{a}_{name}").reshape(k.shape)
                 for a, (k, t) in enumerate(zip(keeps, gots))]
        return _chips_start(boths, "scatter_" + name)

    small, h_ffn, h_mix = [None] * L, [None] * L, [None] * L
    for l in reversed(range(L)):
        xin, h1, qkv, cproj, fl, cum_rows, o, lse, cat, y1, x_mid, h2, g, u, act, y2 = saved[l]
        sh1, sc1, g1, sh2, sc2, g2 = (mod[l, k] for k in range(N_MOD))
        gpre1, gpost1, gpre2, gpost2 = (p[l][None, :] for p in (mix_pre_g, mix_post_g, ffn_pre_g, ffn_post_g))
        dy2, sm_post2 = _postnorm_bwd(dx, y2, g2, gpost2, f"postnorm2_bwd_{l}")
        dgate, dup = _ffn_act_bwd(dy2, W_ffn_out[l], g, u, f"ffn_act_bwd_{l}")
        dW_ffn_out = _mm([(act, dy2)], "tn", BF16, f"dw_ffn_out_{l}", tm=FFN_TILE)
        dh2 = _mm([(dgate, W_ffn_in_t[l], 0), (dup, W_ffn_in_t[l], 1)], "nn", F32, f"dh2_{l}", tn=MM_ROWS_SMALL)
        dWg_t = _mm([(dgate, h2)], "tn", BF16, f"dw_ffn_gate_{l}", tm=FFN_TILE)
        dWu_t = _mm([(dup, h2)], "tn", BF16, f"dw_ffn_up_{l}", tm=FFN_TILE)
        half = N_DEV // 2
        dW_ffn_in_t = jnp.concatenate([dWg_t.reshape(half, s_ff, D), dWu_t.reshape(half, s_ff, D)], axis=0)
        h_ffn[l], token = reduce_start([dW_ffn_in_t, dW_ffn_out.reshape(N_DEV, F // N_DEV, D)], f"ffn_{l}")
        dx_mid, sm_pre2 = _prenorm_bwd(dh2, x_mid, dx, gpre2, sc2, f"prenorm2_bwd_{l}", first=(token,))
        dy1, sm_post1 = _postnorm_bwd(dx_mid, y1, g1, gpost1, f"postnorm1_bwd_{l}")
        dcat = _mm([(dy1, W_o[l])], "nt", F32, f"dcat_{l}", tm=MM_ROWS, tn=MM_COLS)
        dW_o = _mm([(cat, dy1)], "tn", BF16, f"dw_o_{l}")
        dqkv, dcum_rows, dcum_q = _attn_bwd(qkv, dcat, o, lse, cum_rows, H, f"attn_bwd_{l}", tq)
        dcproj, dconv_w, sm_conv = _conv_bwd(cproj, dcat, conv_w_pad[l], conv_b[l][None, :], conv_ln_g[l][None, :],
                                             conv_ln_b[l][None, :], f"conv_bwd_{l}")
        dcum = dcum_rows.reshape(H, T) + jnp.transpose(dcum_q, (0, 2, 1)).reshape(H, T)
        dcum = jnp.pad(dcum, ((0, R - H), (0, 0)))
        dfl_t, dwf_t, dbf = _fgate_bwd(dcum, fl, h1, f"fgate_bwd_{l}")
        dfl = jnp.transpose(dfl_t).astype(BF16)
        dh1 = _mm([(dqkv[0], W_qkv[l], 0), (dqkv[1], W_qkv[l], 1), (dqkv[2], W_qkv[l], 2),
                   (dfl, W_f_pad[l]), (dcproj, W_c[l])], "nt", F32, f"dh1_{l}", tm=MM_ROWS, tn=MM_COLS)
        dWq = [_mm([(h1, dqkv[k])], "tn", BF16, f"dw_qkv{k}_{l}") for k in range(3)]
        dWc = _mm([(h1, dcproj)], "tn", BF16, f"dw_conv_{l}")
        dWf = jnp.transpose(dwf_t[:H]).astype(BF16)
        dW_in = jnp.concatenate(dWq + [dWf, dWc], axis=1)
        h_mix[l], token = reduce_start([jnp.transpose(dW_in.reshape(D, N_DEV, s_in), (1, 0, 2)),
                                        dW_o.reshape(N_DEV, D // N_DEV, D)], f"mix_{l}")
        dx, sm_pre1 = _prenorm_bwd(dh1, xin, dx_mid, gpre1, sc1, f"prenorm1_bwd_{l}", first=(token,))
        dmod = jnp.stack([sm_pre1[0], sm_pre1[1], sm_post1[0], sm_pre2[0], sm_pre2[1], sm_post2[0]])
        small[l] = (dmod, sm_pre1[2], sm_post1[1], sm_pre2[2], sm_post2[1], sm_conv[0], sm_conv[1], sm_conv[2],
                    dbf[:H, 0], dconv_w[:CONV_K])
    grad_x = dx[None]

    small_names = 10
    small_l = [jnp.stack([small[l][k] for l in range(L)]) for k in range(small_names)]
    small_shapes = [a.shape for a in small_l]
    packed = _pack(small_l, D)
    rows = packed.shape[1]
    rows_pad = -(-L * rows // 8) * 8
    packed2 = jnp.pad(packed.reshape(L * rows, D), ((0, rows_pad - L * rows), (0, 0)))
    h_small = _gather_start([packed2], "gather_small")

    got_ffn = [_chips_wait(h_ffn[l], h_small["token"], f"scatter_wait_ffn_{l}") for l in range(L)]
    got_mix = [_chips_wait(h_mix[l], h_small["token"], f"scatter_wait_mix_{l}") for l in range(L)]

    def step(w, m, v, parts, name):
        return list(_adamw_summed(w, parts, m, v, "adamw_" + name))

    swap = lambda t: jnp.transpose(t, (0, 2, 1))
    r_w_ffn_in = [swap(t) for t in step(w_ffn_in_t, swap(m_w_ffn_in), swap(v_w_ffn_in),
                                        [got_ffn[l][0] for l in range(L)], "w_ffn_in")]
    r_w_ffn_out = step(w_ffn_out, m_w_ffn_out, v_w_ffn_out, [got_ffn[l][1] for l in range(L)], "w_ffn_out")
    r_w_in = step(w_in, m_w_in, v_w_in, [got_mix[l][0] for l in range(L)], "w_in")
    r_w_o = step(w_o, m_w_o, v_w_o, [got_mix[l][1] for l in range(L)], "w_o")

    small_g = _gather_wait(_gather_mid(h_small, r_w_o[1], "gather_small_mid"), r_w_o[1], "gather_small_wait")[0]
    small_sum = _sum_devices(small_g, "sum_small")[:L * rows].reshape(L, rows, D)
    (g_ada_b6, g_mix_pre, g_mix_post, g_ffn_pre, g_ffn_post, g_conv_b, g_ln_g, g_ln_b, g_b_f,
     g_conv_w_full) = _unpack(small_sum, small_shapes, D)
    g_ada_b = g_ada_b6.reshape(L, N_MOD * D)
    g_conv_w = lax.dynamic_slice_in_dim(g_conv_w_full, me * cs, cs, axis=2)
    dmod_all = small_g[:, :L * rows].reshape(N_DEV, L, rows, D)[:, :, :N_MOD].reshape(N_DEV, L, N_MOD * D)
    dmod_loc = jnp.transpose(lax.dynamic_slice_in_dim(dmod_all, me * s_ada, s_ada, axis=2), (1, 0, 2))
    g_ada_w = _ada_bwd(jnp.transpose(c_all), dmod_loc, "ada_bwd")
    r_ada_w = [t.reshape(ada_w.shape) for t in _adamw(
        ada_w.reshape(L * D, s_ada), g_ada_w.reshape(L * D, s_ada), m_ada_w.reshape(L * D, s_ada),
        v_ada_w.reshape(L * D, s_ada), "adamw_ada_w", False)]

    sw = [b_f, conv_w, conv_b, conv_ln_g, conv_ln_b, mix_pre_g, mix_post_g, ffn_pre_g, ffn_post_g, ada_b]
    sg = [g_b_f, g_conv_w, g_conv_b, g_ln_g, g_ln_b, g_mix_pre, g_mix_post, g_ffn_pre, g_ffn_post, g_ada_b]
    sm = [m_b_f, m_conv_w, m_conv_b, m_conv_ln_g, m_conv_ln_b, m_mix_pre_g, m_mix_post_g, m_ffn_pre_g, m_ffn_post_g, m_ada_b]
    sv = [v_b_f, v_conv_w, v_conv_b, v_conv_ln_g, v_conv_ln_b, v_mix_pre_g, v_mix_post_g, v_ffn_pre_g, v_ffn_post_g, v_ada_b]
    shapes = [a.shape for a in sw]

    def flat(arrs):
        p = _pack(arrs, D)
        n = p.shape[0] * p.shape[1]
        return jnp.pad(p.reshape(n, D), ((0, -(-n // 8) * 8 - n), (0, 0))), p.shape

    pw, pshape = flat(sw)
    pg, pm, pv = flat(sg)[0], flat(sm)[0], flat(sv)[0]
    s_outs = _adamw(pw, pg, pm, pv, "adamw_small", False)
    n_small = pshape[0] * pshape[1]
    s_g, s_d, s_m, s_v = (_unpack(t[:n_small].reshape(pshape), shapes, D) for t in s_outs)

    big = {"w_in": r_w_in, "w_o": r_w_o, "w_ffn_in": r_w_ffn_in, "w_ffn_out": r_w_ffn_out, "ada_w": r_ada_w}
    order = ["w_in", "b_f", "conv_w", "conv_b", "conv_ln_g", "conv_ln_b", "w_o", "w_ffn_in", "w_ffn_out",
             "mix_pre_g", "mix_post_g", "ffn_pre_g", "ffn_post_g", "ada_w", "ada_b"]
    small_pos = {n: i for i, n in enumerate(["b_f", "conv_w", "conv_b", "conv_ln_g", "conv_ln_b", "mix_pre_g",
                                             "mix_post_g", "ffn_pre_g", "ffn_post_g", "ada_b"])}

    def pick(n, k):
        if n in big:
            return big[n][k]
        return (s_g, s_d, s_m, s_v)[k][small_pos[n]]

    return (loss, grad_x, *[pick(n, 0) for n in order], *[pick(n, 1) for n in order],
            *[pick(n, 2) for n in order], *[pick(n, 3) for n in order])
```

```python
import functools
import math

import jax
import jax.numpy as jnp
from jax import lax
from jax.experimental import pallas as pl
from jax.experimental.pallas import tpu as pltpu

F32 = jnp.float32
BF16 = jnp.bfloat16
MESH = pl.DeviceIdType.MESH
N_DEV = 8
EPS = 1e-6
CONV_K = 31
CONV_PAD = 32
CONV_CHUNK = 128
N_MOD = 6
NEG = -1e30
LANES = 128
VMEM_LIMIT = 56 * 2**20
MM_TILE = 1024
MM_ROWS = 2048
MM_COLS = 512
MM_ROWS_SMALL = 512
FFN_COLS = 256
DW_TILE = 256
ADAM_LR, ADAM_B1, ADAM_B2, ADAM_EPS, ADAM_WD, ADAM_STEP = 0.001, 0.9, 0.999, 1e-08, 0.01, 10

NN = (((1,), (0,)), ((), ()))
NT = (((1,), (1,)), ((), ()))
TN = (((0,), (0,)), ((), ()))


def _dot(a, b, dims):
    return lax.dot_general(a, b, dims, preferred_element_type=F32)


def _tile(n, pref, align=LANES):
    if n <= pref:
        return n
    t = (pref // align) * align
    while t >= align:
        if n % t == 0:
            return t
        t -= align
    return n


def _params(sem=None):
    return pltpu.CompilerParams(dimension_semantics=sem, vmem_limit_bytes=VMEM_LIMIT)


def _sigmoid(x):
    return 1.0 / (1.0 + jnp.exp(-x))


def _my_index():
    return 4 * lax.axis_index("x") + 2 * lax.axis_index("y") + lax.axis_index("c")


def _exchange(x, *, gather, name):
    blk = x.shape if gather else x.shape[1:]

    def body(x_ref, y_ref, send_sems, recv_sems, local_sem):
        mx, my, mc = lax.axis_index("x"), lax.axis_index("y"), lax.axis_index("c")
        me = 4 * mx + 2 * my + mc

        def src(p):
            return x_ref if gather else x_ref.at[p]

        mine = pltpu.make_async_copy(src(me), y_ref.at[me], local_sem)
        mine.start()
        copies = []
        for k in range(1, N_DEV):
            px = (1 - mx) if (k >> 2) & 1 else mx
            py = (1 - my) if (k >> 1) & 1 else my
            pc = (1 - mc) if k & 1 else mc
            cp = pltpu.make_async_remote_copy(
                src_ref=src(4 * px + 2 * py + pc), dst_ref=y_ref.at[me],
                send_sem=send_sems.at[k - 1], recv_sem=recv_sems.at[k - 1],
                device_id=(px, py, pc), device_id_type=MESH)
            cp.start()
            copies.append(cp)
        for cp in copies:
            cp.wait()
        mine.wait()

    return pl.pallas_call(
        body, name=name,
        out_shape=jax.ShapeDtypeStruct((N_DEV,) + tuple(blk), x.dtype),
        in_specs=[pl.BlockSpec(memory_space=pl.ANY)],
        out_specs=pl.BlockSpec(memory_space=pl.ANY),
        scratch_shapes=[pltpu.SemaphoreType.DMA((N_DEV - 1,)), pltpu.SemaphoreType.DMA((N_DEV - 1,)),
                        pltpu.SemaphoreType.DMA(())],
    )(x)


_HBM = pl.BlockSpec(memory_space=pl.ANY)


def _swap_sibling(xs, name):
    n = len(xs)

    def body(*refs):
        send_sems, recv_sems = refs[2 * n], refs[2 * n + 1]
        sib = (lax.axis_index("x"), lax.axis_index("y"), 1 - lax.axis_index("c"))
        copies = [pltpu.make_async_remote_copy(src_ref=refs[a], dst_ref=refs[n + a], send_sem=send_sems.at[a],
                                               recv_sem=recv_sems.at[a], device_id=sib, device_id_type=MESH)
                  for a in range(n)]
        for cp in copies:
            cp.start()
        for cp in copies:
            cp.wait()

    return pl.pallas_call(
        body, name=name, out_shape=[jax.ShapeDtypeStruct(x.shape, x.dtype) for x in xs],
        in_specs=[_HBM] * n, out_specs=[_HBM] * n,
        scratch_shapes=[pltpu.SemaphoreType.DMA((n,)), pltpu.SemaphoreType.DMA((n,))],
    )(*xs)


_SEM = pl.BlockSpec(memory_space=pltpu.SEMAPHORE)
_HBM_SPEC = pl.BlockSpec(memory_space=pltpu.HBM)
_EFFECT = pltpu.CompilerParams(has_side_effects=pltpu.SideEffectType.DATAFLOW_SIDE_EFFECTING)


def _in_hbm(a):
    return pltpu.with_memory_space_constraint(a, pltpu.HBM)


def _chip_copies(x_refs, land_refs, send_sems, recv_sems, loc_sems):
    mx, my, mc = lax.axis_index("x"), lax.axis_index("y"), lax.axis_index("c")
    here = 2 * mx + my
    local, remote = [], []
    for a, (x_ref, land_ref) in enumerate(zip(x_refs, land_refs)):
        local.append(pltpu.make_async_copy(x_ref.at[here], land_ref.at[here], loc_sems.at[a]))
        for j, (px, py) in enumerate([(1 - mx, my), (mx, 1 - my), (1 - mx, 1 - my)]):
            remote.append(pltpu.make_async_remote_copy(
                src_ref=x_ref.at[2 * px + py], dst_ref=land_ref.at[here], send_sem=send_sems.at[3 * a + j],
                recv_sem=recv_sems.at[3 * a + j], device_id=(px, py, mc), device_id_type=MESH))
    return local, remote


def _chips_start(xs, name):
    n = len(xs)

    def body(*refs):
        local, remote = _chip_copies(refs[:n], refs[n:2 * n], *refs[2 * n:2 * n + 3])
        for cp in local + remote:
            cp.start()
        refs[-1][...] = jnp.zeros_like(refs[-1])

    lands = [lax.empty(x.shape, x.dtype) for x in xs]
    outs = pl.pallas_call(
        body, name=name,
        out_shape=(pltpu.SemaphoreType.DMA((3 * n,)), pltpu.SemaphoreType.DMA((3 * n,)), pltpu.SemaphoreType.DMA((n,)),
                   *[pltpu.HBM(x.shape, x.dtype) for x in xs], *[pltpu.HBM(x.shape, x.dtype) for x in xs],
                   jax.ShapeDtypeStruct((8, LANES), F32)),
        in_specs=[_HBM_SPEC] * (2 * n),
        out_specs=(_SEM, _SEM, _SEM, *[_HBM_SPEC] * (2 * n), pl.BlockSpec(memory_space=pltpu.VMEM)),
        input_output_aliases={i: 3 + i for i in range(2 * n)},
        compiler_params=_EFFECT,
    )(*[_in_hbm(x) for x in xs], *[_in_hbm(t) for t in lands])
    return (outs[0], outs[1], outs[2], list(outs[3:3 + n]), list(outs[3 + n:3 + 2 * n])), outs[-1]


def _chips_wait(handle, after, name):
    send_sems, recv_sems, loc_sems, x_thru, land_thru = handle
    n = len(x_thru)

    def body(*refs):
        local, remote = _chip_copies(refs[:n], refs[n:2 * n], *refs[2 * n:2 * n + 3])
        for cp in local:
            cp.wait()
        for cp in remote:
            cp.wait_send()
            cp.wait_recv()

    outs = pl.pallas_call(
        body, name=name,
        out_shape=[pltpu.HBM(x.shape, x.dtype) for x in x_thru + land_thru],
        in_specs=[_HBM_SPEC] * (2 * n) + [_SEM, _SEM, _SEM, _HBM],
        out_specs=[_HBM_SPEC] * (2 * n),
        input_output_aliases={i: i for i in range(2 * n)},
        compiler_params=_EFFECT,
    )(*x_thru, *land_thru, send_sems, recv_sems, loc_sems, after)
    return list(outs[n:])


def _gather_first_copies(x_refs, y_refs, send_sems, sib_recv, ici_recv, loc_sems):
    mx, my, mc = lax.axis_index("x"), lax.axis_index("y"), lax.axis_index("c")
    me = 4 * mx + 2 * my + mc
    local, remote = [], []
    for a, (x_ref, y_ref) in enumerate(zip(x_refs, y_refs)):
        local.append(pltpu.make_async_copy(x_ref, y_ref.at[me], loc_sems.at[a]))
        remote.append(pltpu.make_async_remote_copy(
            src_ref=x_ref, dst_ref=y_ref.at[me], send_sem=send_sems.at[4 * a], recv_sem=sib_recv.at[a],
            device_id=(mx, my, 1 - mc), device_id_type=MESH))
        for j, (px, py) in enumerate([(1 - mx, my), (mx, 1 - my), (1 - mx, 1 - my)]):
            remote.append(pltpu.make_async_remote_copy(
                src_ref=x_ref, dst_ref=y_ref.at[me], send_sem=send_sems.at[4 * a + 1 + j], recv_sem=ici_recv.at[3 * a + j],
                device_id=(px, py, mc), device_id_type=MESH))
    return local, remote


def _gather_forward_copies(y_refs, ici_recv, fwd_send, fwd_recv):
    mx, my, mc = lax.axis_index("x"), lax.axis_index("y"), lax.axis_index("c")
    pairs = []
    for a, y_ref in enumerate(y_refs):
        for j, (px, py) in enumerate([(1 - mx, my), (mx, 1 - my), (1 - mx, 1 - my)]):
            slot = y_ref.at[4 * px + 2 * py + mc]
            arrival = pltpu.make_async_remote_copy(
                src_ref=slot, dst_ref=slot, send_sem=fwd_send.at[3 * a + j], recv_sem=ici_recv.at[3 * a + j],
                device_id=(px, py, mc), device_id_type=MESH)
            onward = pltpu.make_async_remote_copy(
                src_ref=slot, dst_ref=slot, send_sem=fwd_send.at[3 * a + j], recv_sem=fwd_recv.at[3 * a + j],
                device_id=(mx, my, 1 - mc), device_id_type=MESH)
            pairs.append((arrival, onward))
    return pairs


def _gather_start(xs, name, after=()):
    n = len(xs)
    ni = 2 * n + len(after)

    def body(*refs):
        local, remote = _gather_first_copies(refs[:n], refs[n:2 * n], *refs[ni:ni + 4])
        for cp in local + remote:
            cp.start()
        refs[-1][...] = jnp.zeros_like(refs[-1])

    ys = [lax.empty((N_DEV,) + tuple(x.shape), x.dtype) for x in xs]
    dma = pltpu.SemaphoreType.DMA
    outs = pl.pallas_call(
        body, name=name,
        out_shape=(dma((4 * n,)), dma((n,)), dma((3 * n,)), dma((n,)),
                   *[pltpu.HBM(x.shape, x.dtype) for x in xs], *[pltpu.HBM(y.shape, y.dtype) for y in ys],
                   jax.ShapeDtypeStruct((8, LANES), F32)),
        in_specs=[_HBM_SPEC] * (2 * n) + [_HBM] * len(after),
        out_specs=(_SEM, _SEM, _SEM, _SEM, *[_HBM_SPEC] * (2 * n), pl.BlockSpec(memory_space=pltpu.VMEM)),
        input_output_aliases={i: 4 + i for i in range(2 * n)},
        compiler_params=_EFFECT,
    )(*[_in_hbm(x) for x in xs], *[_in_hbm(y) for y in ys], *after)
    return dict(send=outs[0], sib_recv=outs[1], ici_recv=outs[2], loc=outs[3], x=list(outs[4:4 + n]),
                y=list(outs[4 + n:4 + 2 * n]), token=outs[-1])


def _gather_mid(h, after, name):
    n = len(h["y"])

    def body(*refs):
        for arrival, onward in _gather_forward_copies(refs[:n], refs[n], refs[n + 2 + n], refs[n + 3 + n]):
            arrival.wait_recv()
            onward.start()

    dma = pltpu.SemaphoreType.DMA
    outs = pl.pallas_call(
        body, name=name,
        out_shape=(*[pltpu.HBM(y.shape, y.dtype) for y in h["y"]], dma((3 * n,)), dma((3 * n,))),
        in_specs=[_HBM_SPEC] * n + [_SEM, _HBM],
        out_specs=(*[_HBM_SPEC] * n, _SEM, _SEM),
        input_output_aliases={i: i for i in range(n)},
        compiler_params=_EFFECT,
    )(*h["y"], h["ici_recv"], after)
    return dict(h, y=list(outs[:n]), fwd_send=outs[n], fwd_recv=outs[n + 1])


def _gather_wait(h, after, name):
    n = len(h["y"])

    def body(*refs):
        x_refs, y_refs = refs[:n], refs[n:2 * n]
        send, sib_recv, loc, fwd_send, fwd_recv = refs[2 * n:2 * n + 5]
        local, remote = _gather_first_copies(x_refs, y_refs, send, sib_recv, fwd_recv, loc)
        for cp in local:
            cp.wait()
        for k, cp in enumerate(remote):
            cp.wait_send()
            if k % 4 == 0:
                cp.wait_recv()
        for _, onward in _gather_forward_copies(y_refs, fwd_recv, fwd_send, fwd_recv):
            onward.wait_send()
            onward.wait_recv()

    outs = pl.pallas_call(
        body, name=name,
        out_shape=[pltpu.HBM(t.shape, t.dtype) for t in h["x"] + h["y"]],
        in_specs=[_HBM_SPEC] * (2 * n) + [_SEM] * 5 + [_HBM],
        out_specs=[_HBM_SPEC] * (2 * n),
        input_output_aliases={i: i for i in range(2 * n)},
        compiler_params=_EFFECT,
    )(*h["x"], *h["y"], h["send"], h["sib_recv"], h["loc"], h["fwd_send"], h["fwd_recv"], after)
    return list(outs[n:])


def _pair_sum(a, b, name):
    R, C = a.shape
    tm = _tile(R, 512, 16)

    def body(a_ref, b_ref, o_ref):
        o_ref[...] = (a_ref[...].astype(F32) + b_ref[...].astype(F32)).astype(BF16)

    big = pl.BlockSpec((tm, C), lambda i: (i, 0))
    return pl.pallas_call(
        body, name=name, grid=(R // tm,), in_specs=[big, big], out_specs=big,
        out_shape=jax.ShapeDtypeStruct((R, C), BF16), compiler_params=_params(("parallel",)),
    )(a, b)


def _mm(pairs, mode, out_dtype, name, tm=MM_TILE, tn=MM_TILE, n_cols=None):
    dims = {"nn": NN, "nt": NT, "tn": TN}[mode]
    a0, b0 = pairs[0][0], pairs[0][1]
    M = a0.shape[1] if mode == "tn" else a0.shape[0]
    N = n_cols or (b0.shape[0] if mode == "nt" else b0.shape[1])
    tm, tn = _tile(M, tm), _tile(N, tn)
    in_specs, args = [], []
    for pr in pairs:
        a, b = pr[0], pr[1]
        if mode == "tn":
            K = a.shape[0]
            in_specs.append(pl.BlockSpec((K, tm), lambda i, j: (0, i)))
            in_specs.append(pl.BlockSpec((K, tn), lambda i, j: (0, j)))
        elif mode == "nn":
            K = a.shape[1]
            rb = pr[2] if len(pr) > 2 else 0
            in_specs.append(pl.BlockSpec((tm, K), lambda i, j: (i, 0)))
            in_specs.append(pl.BlockSpec((K, tn), functools.partial(lambda i, j, rb: (rb, j), rb=rb)))
        else:
            K = a.shape[1]
            cb = pr[2] if len(pr) > 2 else 0
            in_specs.append(pl.BlockSpec((tm, K), lambda i, j: (i, 0)))
            in_specs.append(pl.BlockSpec((tn, K), functools.partial(lambda i, j, cb: (j, cb), cb=cb)))
        args += [a, b]
    n_pairs = len(pairs)

    def body(*refs):
        o_ref = refs[-1]
        acc = None
        for k in range(n_pairs):
            d = _dot(refs[2 * k][...], refs[2 * k + 1][...], dims)
            acc = d if acc is None else acc + d
        o_ref[...] = acc.astype(o_ref.dtype)

    return pl.pallas_call(
        body, name=name, grid=(M // tm, N // tn), in_specs=in_specs,
        out_specs=pl.BlockSpec((tm, tn), lambda i, j: (i, j)),
        out_shape=jax.ShapeDtypeStruct((M, N), out_dtype),
        compiler_params=_params(("parallel", "arbitrary")),
    )(*args)


def _prenorm(x, g, sc, sh, name, first=()):
    T, D = x.shape
    tm = _tile(T, 512, 8)

    def body(x_ref, g_ref, sc_ref, sh_ref, *rest):
        h_ref = rest[-1]
        xv = x_ref[...]
        r = lax.rsqrt(jnp.mean(xv * xv, axis=-1, keepdims=True) + EPS)
        h_ref[...] = (((xv * r) * g_ref[...]) * (1.0 + sc_ref[...]) + sh_ref[...]).astype(BF16)

    row = pl.BlockSpec((1, D), lambda i: (0, 0))
    return pl.pallas_call(
        body, name=name, grid=(T // tm,),
        in_specs=[pl.BlockSpec((tm, D), lambda i: (i, 0)), row, row, row] + [_HBM] * len(first),
        out_specs=pl.BlockSpec((tm, D), lambda i: (i, 0)),
        out_shape=jax.ShapeDtypeStruct((T, D), BF16),
        compiler_params=_params(("parallel",)),
    )(x, g, sc, sh, *first)


def _prenorm_bwd(dh, x, dres, g, sc, name, first=()):
    T, D = x.shape
    tm = _tile(T, 256, 8)

    def body(dh_ref, x_ref, dres_ref, g_ref, sc_ref, *rest):
        dx_ref, sm_ref = rest[-2:]
        @pl.when(pl.program_id(0) == 0)
        def _():
            sm_ref[...] = jnp.zeros_like(sm_ref)

        xv, dhv = x_ref[...], dh_ref[...]
        r = lax.rsqrt(jnp.mean(xv * xv, axis=-1, keepdims=True) + EPS)
        xh = xv * r
        one_sc = 1.0 + sc_ref[...]
        sm_ref[0:1, :] += jnp.sum(dhv, axis=0, keepdims=True)
        sm_ref[1:2, :] += jnp.sum(dhv * (xh * g_ref[...]), axis=0, keepdims=True)
        sm_ref[2:3, :] += jnp.sum(dhv * one_sc * xh, axis=0, keepdims=True)
        dxh = dhv * one_sc * g_ref[...]
        dx_ref[...] = dres_ref[...] + r * (dxh - xh * jnp.mean(dxh * xh, axis=-1, keepdims=True))

    row = pl.BlockSpec((1, D), lambda i: (0, 0))
    big = pl.BlockSpec((tm, D), lambda i: (i, 0))
    return pl.pallas_call(
        body, name=name, grid=(T // tm,),
        in_specs=[big, big, big, row, row] + [_HBM] * len(first),
        out_specs=[big, pl.BlockSpec((8, D), lambda i: (0, 0))],
        out_shape=(jax.ShapeDtypeStruct((T, D), F32), jax.ShapeDtypeStruct((8, D), F32)),
        compiler_params=_params(("arbitrary",)),
    )(dh, x, dres, g, sc, *first)


def _mm_postnorm(a, w, x, gate, gpost, name):
    T, K = a.shape
    D = w.shape[1]
    tm = _tile(T, MM_ROWS_SMALL, 8)

    def body(a_ref, w_ref, x_ref, gate_ref, gp_ref, y_ref, xn_ref):
        y = _dot(a_ref[...], w_ref[...], NN)
        r = lax.rsqrt(jnp.mean(y * y, axis=-1, keepdims=True) + EPS)
        y_ref[...] = y
        xn_ref[...] = x_ref[...] + gate_ref[...] * ((y * r) * gp_ref[...])

    row = pl.BlockSpec((1, D), lambda i: (0, 0))
    big = pl.BlockSpec((tm, D), lambda i: (i, 0))
    return pl.pallas_call(
        body, name=name, grid=(T // tm,),
        in_specs=[pl.BlockSpec((tm, K), lambda i: (i, 0)), pl.BlockSpec((K, D), lambda i: (0, 0)), big, row, row],
        out_specs=[big, big],
        out_shape=(jax.ShapeDtypeStruct((T, D), F32), jax.ShapeDtypeStruct((T, D), F32)),
        compiler_params=_params(("parallel",)),
    )(a, w, x, gate, gpost)


def _postnorm_bwd(dx, y, gate, gpost, name):
    T, D = y.shape
    tm = _tile(T, 256, 8)

    def body(dx_ref, y_ref, gate_ref, gp_ref, dy_ref, sm_ref):
        @pl.when(pl.program_id(0) == 0)
        def _():
            sm_ref[...] = jnp.zeros_like(sm_ref)

        yv, dxv = y_ref[...], dx_ref[...]
        r = lax.rsqrt(jnp.mean(yv * yv, axis=-1, keepdims=True) + EPS)
        yh = yv * r
        dn = dxv * gate_ref[...]
        sm_ref[0:1, :] += jnp.sum(dxv * (yh * gp_ref[...]), axis=0, keepdims=True)
        sm_ref[1:2, :] += jnp.sum(dn * yh, axis=0, keepdims=True)
        dyh = dn * gp_ref[...]
        dy_ref[...] = (r * (dyh - yh * jnp.mean(dyh * yh, axis=-1, keepdims=True))).astype(BF16)

    row = pl.BlockSpec((1, D), lambda i: (0, 0))
    big = pl.BlockSpec((tm, D), lambda i: (i, 0))
    return pl.pallas_call(
        body, name=name, grid=(T // tm,),
        in_specs=[big, big, row, row],
        out_specs=[big, pl.BlockSpec((8, D), lambda i: (0, 0))],
        out_shape=(jax.ShapeDtypeStruct((T, D), BF16), jax.ShapeDtypeStruct((8, D), F32)),
        compiler_params=_params(("arbitrary",)),
    )(dx, y, gate, gpost)


def _ffn_in_fwd(h, w_t, name):
    T, D = h.shape
    F = w_t.shape[0] // 2
    tm, tn = _tile(T, MM_ROWS), _tile(F, FFN_COLS)
    nj = F // tn

    def body(h_ref, wg_ref, wu_ref, g_ref, u_ref, act_ref):
        hv = h_ref[...]
        g = _dot(hv, wg_ref[...], NT)
        u = _dot(hv, wu_ref[...], NT)
        g_ref[...] = g.astype(BF16)
        u_ref[...] = u.astype(BF16)
        act_ref[...] = ((g * _sigmoid(g)) * u).astype(BF16)

    out = pl.BlockSpec((tm, tn), lambda j, i: (i, j))
    return pl.pallas_call(
        body, name=name, grid=(nj, T // tm),
        in_specs=[pl.BlockSpec((tm, D), lambda j, i: (i, 0)),
                  pl.BlockSpec((tn, D), lambda j, i: (j, 0)),
                  pl.BlockSpec((tn, D), lambda j, i: (j + nj, 0))],
        out_specs=[out, out, out],
        out_shape=tuple(jax.ShapeDtypeStruct((T, F), BF16) for _ in range(3)),
        compiler_params=_params(("parallel", "arbitrary")),
    )(h, w_t, w_t)


def _ffn_act_bwd(dy, w_out, g, u, name):
    T, D = dy.shape
    F = w_out.shape[0]
    tm, tn = _tile(T, MM_ROWS), _tile(F, FFN_COLS)

    def body(dy_ref, w_ref, g_ref, u_ref, dg_ref, du_ref):
        dact = _dot(dy_ref[...], w_ref[...], NT)
        gv, uv = g_ref[...].astype(F32), u_ref[...].astype(F32)
        sg = _sigmoid(gv)
        dg_ref[...] = (dact * uv * (sg * (1.0 + gv * (1.0 - sg)))).astype(BF16)
        du_ref[...] = (dact * (gv * sg)).astype(BF16)

    tile = pl.BlockSpec((tm, tn), lambda j, i: (i, j))
    return pl.pallas_call(
        body, name=name, grid=(F // tn, T // tm),
        in_specs=[pl.BlockSpec((tm, D), lambda j, i: (i, 0)), pl.BlockSpec((tn, D), lambda j, i: (j, 0)), tile, tile],
        out_specs=[tile, tile],
        out_shape=(jax.ShapeDtypeStruct((T, F), BF16), jax.ShapeDtypeStruct((T, F), BF16)),
        compiler_params=_params(("parallel", "arbitrary")),
    )(dy, w_out, g, u)


def _lane_scan(v, reverse):
    T = v.shape[-1]
    lane = lax.broadcasted_iota(jnp.int32, v.shape, 1)
    d = 1
    while d < T:
        if reverse:
            v = v + jnp.where(lane < T - d, pltpu.roll(v, T - d, axis=1), 0.0)
        else:
            v = v + jnp.where(lane >= d, pltpu.roll(v, d, axis=1), 0.0)
        d *= 2
    return v


def _fgate_fwd(h, wf_t, bf, name):
    T, D = h.shape
    R = wf_t.shape[0]

    def body(h_ref, w_ref, b_ref, fl_ref, cum_ref):
        fl = _dot(w_ref[...], h_ref[...], NT) + b_ref[...]
        fl_ref[...] = fl
        logf = jnp.minimum(fl, 0.0) - jnp.log(1.0 + jnp.exp(-jnp.abs(fl)))
        cum_ref[...] = _lane_scan(logf, reverse=False)

    return pl.pallas_call(
        body, name=name,
        out_shape=(jax.ShapeDtypeStruct((R, T), F32), jax.ShapeDtypeStruct((R, T), F32)),
        compiler_params=_params(),
    )(h, wf_t, bf)


def _fgate_bwd(dcum, fl, h, name):
    R, T = fl.shape
    D = h.shape[1]

    def body(dc_ref, fl_ref, h_ref, dfl_ref, dw_ref, db_ref):
        dlogf = _lane_scan(dc_ref[...], reverse=True)
        dfl = dlogf * _sigmoid(-fl_ref[...])
        dfl_ref[...] = dfl
        dw_ref[...] = _dot(dfl.astype(BF16), h_ref[...], NN)
        db_ref[...] = jnp.broadcast_to(jnp.sum(dfl, axis=-1, keepdims=True), (R, LANES))

    return pl.pallas_call(
        body, name=name,
        out_shape=(jax.ShapeDtypeStruct((R, T), F32), jax.ShapeDtypeStruct((R, D), F32),
                   jax.ShapeDtypeStruct((R, LANES), F32)),
        compiler_params=_params(),
    )(dcum, fl, h)


def _head_masks(hpb, dh, rows):
    lane = lax.broadcasted_iota(jnp.int32, (rows, LANES), 1)
    return [(lane >= h * dh) & (lane < (h + 1) * dh) for h in range(hpb)]


def _stack_heads(v, masks):
    return jnp.concatenate([jnp.where(mk, v, jnp.zeros_like(v)) for mk in masks], axis=0)


def _heads_to_lanes(col, masks, tq):
    out = jnp.broadcast_to(col[0:tq], (tq, LANES))
    for h in range(1, len(masks)):
        out = jnp.where(masks[h], col[h * tq:(h + 1) * tq], out)
    return out


def _causal_stack(hpb, tq):
    r = lax.broadcasted_iota(jnp.int32, (tq, tq), 0)
    c = lax.broadcasted_iota(jnp.int32, (tq, tq), 1)
    return jnp.concatenate([c] * hpb, axis=0) <= jnp.concatenate([r] * hpb, axis=0)


def _attn_fwd(qkv, cum_rows, n_heads, name, tq):
    T = qkv.shape[0]
    A = qkv.shape[1] // 3
    dh = A // n_heads
    hpb = LANES // dh
    nb = A // LANES
    nq = T // tq
    scale = dh ** -0.5

    def body(q_ref, k_ref, v_ref, c_ref, o_ref, l_ref, vbd):
        hp, i = pl.program_id(0), pl.program_id(1)
        masks = _head_masks(hpb, dh, tq)

        @pl.when(i == 0)
        def _():
            def fill(j, _):
                vbd[j] = _stack_heads(v_ref[pl.ds(pl.multiple_of(j * tq, tq), tq), :], masks)
                return 0

            lax.fori_loop(0, nq, fill, 0)

        qs = _stack_heads(q_ref[...], masks)
        crow0 = hp * hpb * nq

        def tile(j, carry, diag):
            m, l, acc = carry
            kt = k_ref[pl.ds(pl.multiple_of(j * tq, tq), tq), :]
            bias = jnp.concatenate(
                [jnp.broadcast_to(c_ref[pl.ds(crow0 + h * nq + j, 1), :], (tq, tq)) for h in range(hpb)], axis=0)
            s = _dot(qs, kt, NT) * scale - bias
            if diag:
                s = jnp.where(_causal_stack(hpb, tq), s, NEG)
            m_new = jnp.maximum(m, jnp.max(s, axis=-1, keepdims=True))
            p = jnp.exp(s - m_new)
            alpha = jnp.exp(m - m_new)
            l = alpha * l + jnp.sum(p, axis=-1, keepdims=True)
            pcat = jnp.concatenate([p[h * tq:(h + 1) * tq] for h in range(hpb)], axis=1).astype(BF16)
            acc = _heads_to_lanes(alpha, masks, tq) * acc + _dot(pcat, vbd[j], NN)
            return m_new, l, acc

        init = (jnp.full((hpb * tq, 1), NEG, F32), jnp.zeros((hpb * tq, 1), F32), jnp.zeros((tq, LANES), F32))
        carry = lax.fori_loop(0, i, lambda j, c: tile(j, c, False), init)
        m, l, acc = tile(i, carry, True)
        o_ref[...] = (acc / _heads_to_lanes(l, masks, tq)).astype(BF16)
        lse = m + jnp.log(l)
        for h in range(hpb):
            l_ref[:, h:h + 1] = lse[h * tq:(h + 1) * tq]

    return pl.pallas_call(
        body, name=name, grid=(nb, nq),
        in_specs=[pl.BlockSpec((tq, LANES), lambda h, i: (i, h)),
                  pl.BlockSpec((T, LANES), lambda h, i: (0, nb + h)),
                  pl.BlockSpec((T, LANES), lambda h, i: (0, 2 * nb + h)),
                  pl.BlockSpec(cum_rows.shape, lambda h, i: (0, 0))],
        out_specs=[pl.BlockSpec((tq, LANES), lambda h, i: (i, h)),
                   pl.BlockSpec((None, tq, hpb), lambda h, i: (h, i, 0))],
        out_shape=(jax.ShapeDtypeStruct((T, A), BF16), jax.ShapeDtypeStruct((nb, T, hpb), F32)),
        scratch_shapes=[pltpu.VMEM((nq, hpb * tq, LANES), BF16)],
        compiler_params=_params(("arbitrary", "arbitrary")),
    )(qkv, qkv, qkv, cum_rows)


def _attn_bwd(qkv, dcat, o, lse, cum_rows, n_heads, name, tq):
    T = qkv.shape[0]
    A = qkv.shape[1] // 3
    dh = A // n_heads
    hpb = LANES // dh
    nb = A // LANES
    nq = T // tq
    scale = dh ** -0.5

    def body(q_ref, k_ref, v_ref, do_ref, o_ref, l_ref, c_ref, dqkv_ref, dc_ref, dr_ref,
             dq_acc, delta, drow, qs_scr, dos_scr, kbd_scr):
        hp = pl.program_id(0)
        masks = _head_masks(hpb, dh, tq)
        crow0 = hp * hpb * nq

        def prologue(i, _):
            rs = pl.ds(pl.multiple_of(i * tq, tq), tq)
            do = do_ref[rs, :]
            prod = do * o_ref[rs, :].astype(F32)
            for h in range(hpb):
                delta[rs, h:h + 1] = jnp.sum(jnp.where(masks[h], prod, 0.0), axis=-1, keepdims=True)
            qs_scr[i] = _stack_heads(q_ref[rs, :], masks)
            dos_scr[i] = _stack_heads(do, masks).astype(BF16)
            kbd_scr[i] = _stack_heads(k_ref[rs, :], masks)
            dq_acc[rs, :] = jnp.zeros((tq, LANES), F32)
            drow[rs, :] = jnp.zeros((tq, hpb), F32)
            return 0

        lax.fori_loop(0, nq, prologue, 0)

        def kv_step(j, _):
            ks = pl.ds(pl.multiple_of(j * tq, tq), tq)
            kt, vt = k_ref[ks, :], v_ref[ks, :]
            kbd = kbd_scr[j]
            bias = jnp.concatenate(
                [jnp.broadcast_to(c_ref[pl.ds(crow0 + h * nq + j, 1), :], (tq, tq)) for h in range(hpb)], axis=0)

            def q_step(i, carry, diag):
                dk, dv, dcs = carry
                rs = pl.ds(pl.multiple_of(i * tq, tq), tq)
                qs, dos = qs_scr[i], dos_scr[i]
                s = _dot(qs, kt, NT) * scale - bias
                if diag:
                    s = jnp.where(_causal_stack(hpb, tq), s, NEG)
                lse = jnp.concatenate([l_ref[rs, h:h + 1] for h in range(hpb)], axis=0)
                p = jnp.exp(s - lse)
                dv = dv + _dot(p.astype(BF16), dos, TN)
                dp = _dot(dos, vt, NT)
                ds = p * (dp - jnp.concatenate([delta[rs, h:h + 1] for h in range(hpb)], axis=0))
                dcs = tuple(dcs[h] - jnp.sum(ds[h * tq:(h + 1) * tq], axis=0, keepdims=True) for h in range(hpb))
                rsum = jnp.sum(ds, axis=-1, keepdims=True)
                for h in range(hpb):
                    drow[rs, h:h + 1] += rsum[h * tq:(h + 1) * tq]
                dsb = (ds * scale).astype(BF16)
                dk = dk + _dot(dsb, qs, TN)
                dscat = jnp.concatenate([dsb[h * tq:(h + 1) * tq] for h in range(hpb)], axis=1)
                dq_acc[rs, :] += _dot(dscat, kbd, NN)
                return dk, dv, dcs

            init = (jnp.zeros((tq, LANES), F32), jnp.zeros((tq, LANES), F32),
                    tuple(jnp.zeros((1, tq), F32) for _ in range(hpb)))
            carry = q_step(j, init, True)
            dk, dv, dcs = lax.fori_loop(j + 1, nq, lambda i, c: q_step(i, c, False), carry)
            dqkv_ref[1, ks, :] = dk.astype(BF16)
            dqkv_ref[2, ks, :] = dv.astype(BF16)
            for h in range(hpb):
                dc_ref[pl.ds(crow0 + h * nq + j, 1), :] = dcs[h]
            return 0

        lax.fori_loop(0, nq, kv_step, 0)
        dqkv_ref[0] = dq_acc[...].astype(BF16)
        dr_ref[...] = drow[...]

    col = lambda off: pl.BlockSpec((T, LANES), functools.partial(lambda h, off: (0, off + h), off=off))
    return pl.pallas_call(
        body, name=name, grid=(nb,),
        in_specs=[col(0), col(nb), col(2 * nb), col(0), col(0),
                  pl.BlockSpec((None, T, hpb), lambda h: (h, 0, 0)),
                  pl.BlockSpec(cum_rows.shape, lambda h: (0, 0))],
        out_specs=[pl.BlockSpec((3, T, LANES), lambda h: (0, 0, h)),
                   pl.BlockSpec(cum_rows.shape, lambda h: (0, 0)),
                   pl.BlockSpec((None, T, hpb), lambda h: (h, 0, 0))],
        out_shape=(jax.ShapeDtypeStruct((3, T, A), BF16), jax.ShapeDtypeStruct(cum_rows.shape, F32),
                   jax.ShapeDtypeStruct((nb, T, hpb), F32)),
        scratch_shapes=[pltpu.VMEM((T, LANES), F32), pltpu.VMEM((T, hpb), F32), pltpu.VMEM((T, hpb), F32),
                        pltpu.VMEM((nq, hpb * tq, LANES), BF16), pltpu.VMEM((nq, hpb * tq, LANES), BF16),
                        pltpu.VMEM((nq, hpb * tq, LANES), BF16)],
        compiler_params=_params(("arbitrary",)),
    )(qkv, qkv, qkv, dcat, o, lse, cum_rows)


def _glu_into(upad, cv_ref, cg_ref, T):
    upad[0:CONV_PAD, :] = jnp.zeros((CONV_PAD, upad.shape[1]), F32)

    def fill(c, _):
        rs = pl.ds(pl.multiple_of(c * CONV_CHUNK, CONV_CHUNK), CONV_CHUNK)
        upad[pl.ds(pl.multiple_of(CONV_PAD + c * CONV_CHUNK, 8), CONV_CHUNK), :] = cv_ref[rs, :] * _sigmoid(cg_ref[rs, :])
        return 0

    lax.fori_loop(0, T // CONV_CHUNK, fill, 0)


SUBLANES = 8
CONV_SHIFT_ROWS = CONV_CHUNK + CONV_PAD - SUBLANES


def _load_window(win, sh, src, r0):
    win[...] = src[pl.ds(r0, CONV_CHUNK + CONV_PAD), :]
    for b in range(1, SUBLANES):
        sh[b - 1] = win[b:b + CONV_SHIFT_ROWS, :]


def _tap(win, sh, o):
    b = o % SUBLANES
    if b == 0:
        return win[o:o + CONV_CHUNK, :]
    return sh[b - 1, o - b:o - b + CONV_CHUNK, :]


def _conv_taps(win, sh, w_ref, first, step):
    acc = None
    for k in range(CONV_K):
        t = w_ref[k:k + 1, :] * _tap(win, sh, first + step * k)
        acc = t if acc is None else acc + t
    return acc


def _conv_scratch(C):
    return [pltpu.VMEM((CONV_CHUNK + CONV_PAD, C), F32), pltpu.VMEM((SUBLANES - 1, CONV_SHIFT_ROWS, C), F32)]


def _conv_fwd(cproj, w, b, lg, lb, name):
    T = cproj.shape[0]
    C = cproj.shape[1] // 2
    off = CONV_PAD - (CONV_K - 1)

    def body(cv_ref, cg_ref, w_ref, b_ref, lg_ref, lb_ref, out_ref, upad, win, sh):
        _glu_into(upad, cv_ref, cg_ref, T)

        def chunk(c, _):
            r0 = pl.multiple_of(c * CONV_CHUNK, CONV_CHUNK)
            _load_window(win, sh, upad, r0)
            u1 = _conv_taps(win, sh, w_ref, off, 1) + b_ref[...]
            mu = jnp.mean(u1, axis=-1, keepdims=True)
            var = jnp.mean(jnp.square(u1 - mu), axis=-1, keepdims=True)
            u2 = ((u1 - mu) * lax.rsqrt(var + EPS)) * lg_ref[...] + lb_ref[...]
            out_ref[pl.ds(r0, CONV_CHUNK), :] = (u2 * _sigmoid(u2)).astype(BF16)
            return 0

        lax.fori_loop(0, T // CONV_CHUNK, chunk, 0)

    row = pl.BlockSpec((1, C), lambda i: (0, 0))
    return pl.pallas_call(
        body, name=name, grid=(1,),
        in_specs=[pl.BlockSpec((T, C), lambda i: (0, 0)), pl.BlockSpec((T, C), lambda i: (0, 1)),
                  pl.BlockSpec(w.shape, lambda i: (0, 0)), row, row, row],
        out_specs=pl.BlockSpec((T, C), lambda i: (0, 0)),
        out_shape=jax.ShapeDtypeStruct((T, C), BF16),
        scratch_shapes=[pltpu.VMEM((T + CONV_PAD, C), F32)] + _conv_scratch(C),
        compiler_params=_params(("arbitrary",)),
    )(cproj, cproj, w, b, lg, lb)


def _conv_bwd(cproj, dcat, w, b, lg, lb, name):
    T = cproj.shape[0]
    C = cproj.shape[1] // 2
    off = CONV_PAD - (CONV_K - 1)
    n_chunks = T // CONV_CHUNK

    def fold(v):
        return jnp.sum(v.reshape(CONV_CHUNK // 8, 8, C), axis=0)

    def body(cv_ref, cg_ref, du_ref, w_ref, b_ref, lg_ref, lb_ref, dc_ref, dw_ref, sm_ref,
             upad, dpad, dwacc, smacc, win, sh):
        _glu_into(upad, cv_ref, cg_ref, T)
        dpad[pl.ds(T, CONV_PAD), :] = jnp.zeros((CONV_PAD, C), F32)
        dwacc[...] = jnp.zeros_like(dwacc)
        smacc[...] = jnp.zeros_like(smacc)

        def chunk_a(c, _):
            r0 = pl.multiple_of(c * CONV_CHUNK, CONV_CHUNK)
            _load_window(win, sh, upad, r0)
            u1 = _conv_taps(win, sh, w_ref, off, 1) + b_ref[...]
            mu = jnp.mean(u1, axis=-1, keepdims=True)
            var = jnp.mean(jnp.square(u1 - mu), axis=-1, keepdims=True)
            rstd = lax.rsqrt(var + EPS)
            u1h = (u1 - mu) * rstd
            u2 = u1h * lg_ref[...] + lb_ref[...]
            sg = _sigmoid(u2)
            du2 = du_ref[pl.ds(r0, CONV_CHUNK), :] * (sg * (1.0 + u2 * (1.0 - sg)))
            smacc[8:16, :] += fold(du2 * u1h)
            smacc[16:24, :] += fold(du2)
            du1h = du2 * lg_ref[...]
            du1 = rstd * (du1h - jnp.mean(du1h, axis=-1, keepdims=True)
                          - u1h * jnp.mean(du1h * u1h, axis=-1, keepdims=True))
            smacc[0:8, :] += fold(du1)
            dpad[pl.ds(r0, CONV_CHUNK), :] = du1
            for k in range(CONV_K):
                dwacc[8 * k:8 * k + 8, :] += fold(du1 * _tap(win, sh, off + k))
            return 0

        lax.fori_loop(0, n_chunks, chunk_a, 0)

        def chunk_b(c, _):
            r0 = pl.multiple_of(c * CONV_CHUNK, CONV_CHUNK)
            rs = pl.ds(r0, CONV_CHUNK)
            _load_window(win, sh, dpad, r0)
            du0 = _conv_taps(win, sh, w_ref, CONV_K - 1, -1)
            cv, sg = cv_ref[rs, :], _sigmoid(cg_ref[rs, :])
            dc_ref[rs, 0:C] = (du0 * sg).astype(BF16)
            dc_ref[rs, C:2 * C] = (du0 * cv * (sg * (1.0 - sg))).astype(BF16)
            return 0

        lax.fori_loop(0, n_chunks, chunk_b, 0)
        for k in range(CONV_K):
            dw_ref[k:k + 1, :] = jnp.sum(dwacc[8 * k:8 * k + 8, :], axis=0, keepdims=True)
        dw_ref[CONV_K:CONV_PAD, :] = jnp.zeros((CONV_PAD - CONV_K, C), F32)
        for r in range(3):
            sm_ref[r:r + 1, :] = jnp.sum(smacc[8 * r:8 * r + 8, :], axis=0, keepdims=True)
        sm_ref[3:8, :] = jnp.zeros((5, C), F32)

    row = pl.BlockSpec((1, C), lambda i: (0, 0))
    return pl.pallas_call(
        body, name=name, grid=(1,),
        in_specs=[pl.BlockSpec((T, C), lambda i: (0, 0)), pl.BlockSpec((T, C), lambda i: (0, 1)),
                  pl.BlockSpec((T, C), lambda i: (0, 1)),
                  pl.BlockSpec(w.shape, lambda i: (0, 0)), row, row, row],
        out_specs=[pl.BlockSpec((T, 2 * C), lambda i: (0, 0)), pl.BlockSpec((CONV_PAD, C), lambda i: (0, 0)),
                   pl.BlockSpec((8, C), lambda i: (0, 0))],
        out_shape=(jax.ShapeDtypeStruct((T, 2 * C), BF16), jax.ShapeDtypeStruct((CONV_PAD, C), F32),
                   jax.ShapeDtypeStruct((8, C), F32)),
        scratch_shapes=[pltpu.VMEM((T + CONV_PAD, C), F32), pltpu.VMEM((T + CONV_PAD, C), F32),
                        pltpu.VMEM((8 * CONV_PAD, C), F32), pltpu.VMEM((24, C), F32)] + _conv_scratch(C),
        compiler_params=_params(("arbitrary",)),
    )(cproj, cproj, dcat, w, b, lg, lb)


def _loss_head(x, target, name):
    T, D = x.shape
    tm = _tile(T, 512, 8)

    def body(x_ref, t_ref, loss_ref, dx_ref):
        @pl.when(pl.program_id(0) == 0)
        def _():
            loss_ref[...] = jnp.zeros_like(loss_ref)

        err = x_ref[...] - t_ref[...]
        part = jnp.sum(jnp.mean(err * err, axis=-1, keepdims=True), axis=0, keepdims=True)
        loss_ref[...] += jnp.broadcast_to(0.5 * part, loss_ref.shape)
        dx_ref[...] = err * (1.0 / D)

    big = pl.BlockSpec((tm, D), lambda i: (i, 0))
    return pl.pallas_call(
        body, name=name, grid=(T // tm,),
        in_specs=[big, big],
        out_specs=[pl.BlockSpec((8, LANES), lambda i: (0, 0)), big],
        out_shape=(jax.ShapeDtypeStruct((8, LANES), F32), jax.ShapeDtypeStruct((T, D), F32)),
        compiler_params=_params(("arbitrary",)),
    )(x, target)


def _ada_fwd(c_all, ada_w, ada_b_loc, name):
    L, D, S = ada_w.shape
    B = c_all.shape[0]

    def body(c_ref, w_ref, b_ref, o_ref):
        c = c_ref[...]
        ca = (c * _sigmoid(c)).astype(BF16)
        o_ref[...] = _dot(ca, w_ref[...].astype(BF16), NN) + b_ref[...]

    return pl.pallas_call(
        body, name=name, grid=(L,),
        in_specs=[pl.BlockSpec((B, D), lambda l: (0, 0)), pl.BlockSpec((None, D, S), lambda l: (l, 0, 0)),
                  pl.BlockSpec((None, 1, S), lambda l: (l, 0, 0))],
        out_specs=pl.BlockSpec((None, B, S), lambda l: (l, 0, 0)),
        out_shape=jax.ShapeDtypeStruct((L, B, S), F32),
        compiler_params=_params(("parallel",)),
    )(c_all, ada_w, ada_b_loc)


def _ada_bwd(c_all_t, dmod_loc, name):
    D, B = c_all_t.shape
    L, _, S = dmod_loc.shape

    def body(c_ref, dm_ref, o_ref):
        c = c_ref[...]
        ca = c * _sigmoid(c)
        acc = None
        for bb in range(B):
            t = ca[:, bb:bb + 1] * dm_ref[bb:bb + 1, :]
            acc = t if acc is None else acc + t
        o_ref[...] = acc

    return pl.pallas_call(
        body, name=name, grid=(L,),
        in_specs=[pl.BlockSpec((D, B), lambda l: (0, 0)), pl.BlockSpec((None, B, S), lambda l: (l, 0, 0))],
        out_specs=pl.BlockSpec((None, D, S), lambda l: (l, 0, 0)),
        out_shape=jax.ShapeDtypeStruct((L, D, S), F32),
        compiler_params=_params(("parallel",)),
    )(c_all_t, dmod_loc)


def _sum_devices(parts, name):
    _, R, C = parts.shape
    tm = _tile(R, 256, 8)

    def body(p_ref, o_ref):
        acc = p_ref[0].astype(F32)
        for d in range(1, N_DEV):
            acc = acc + p_ref[d].astype(F32)
        o_ref[...] = acc

    return pl.pallas_call(
        body, name=name, grid=(R // tm,),
        in_specs=[pl.BlockSpec((N_DEV, tm, C), lambda i: (0, i, 0))],
        out_specs=pl.BlockSpec((tm, C), lambda i: (i, 0)),
        out_shape=jax.ShapeDtypeStruct((R, C), F32),
        compiler_params=_params(("parallel",)),
    )(parts)


def _adamw_math(w, g, m, v):
    m = ADAM_B1 * m + (1.0 - ADAM_B1) * g
    v = ADAM_B2 * v + (1.0 - ADAM_B2) * (g * g)
    m_hat = m / (1.0 - ADAM_B1 ** ADAM_STEP)
    v_hat = v / (1.0 - ADAM_B2 ** ADAM_STEP)
    delta = -ADAM_LR * (m_hat / (jnp.sqrt(v_hat) + ADAM_EPS) + ADAM_WD * w)
    return delta, m, v


def _adamw(w, g, m, v, name, summed):
    R, C = w.shape
    tm = _tile(R, 256, 16)
    n_parts = g.shape[0] if summed else 0

    def body(w_ref, g_ref, m_ref, v_ref, go_ref, d_ref, mo_ref, vo_ref):
        if summed:
            g = g_ref[0].astype(F32)
            for d in range(1, n_parts):
                g = g + g_ref[d].astype(F32)
        else:
            g = g_ref[...]
        delta, mn, vn = _adamw_math(w_ref[...], g, m_ref[...], v_ref[...])
        go_ref[...] = g
        d_ref[...] = delta
        mo_ref[...] = mn
        vo_ref[...] = vn

    big = pl.BlockSpec((tm, C), lambda i: (i, 0))
    gspec = pl.BlockSpec((n_parts, tm, C), lambda i: (0, i, 0)) if summed else big
    return pl.pallas_call(
        body, name=name, grid=(R // tm,),
        in_specs=[big, gspec, big, big],
        out_specs=[big, big, big, big],
        out_shape=tuple(jax.ShapeDtypeStruct((R, C), F32) for _ in range(4)),
        compiler_params=_params(("parallel",)),
    )(w, g, m, v)


def _adamw_summed(w, parts, m, v, name):
    L = len(parts)
    n_parts, R, C = parts[0].shape
    tm = _tile(R, 256, 16)
    nr = R // tm

    def body(*refs):
        w_ref, g_refs = refs[0], refs[1:1 + L]
        m_ref, v_ref, go_ref, d_ref, mo_ref, vo_ref = refs[1 + L:]
        for ll in range(L):
            @pl.when(pl.program_id(0) == ll)
            def _(ll=ll):
                g = g_refs[ll][0].astype(F32)
                for d in range(1, n_parts):
                    g = g + g_refs[ll][d].astype(F32)
                delta, mn, vn = _adamw_math(w_ref[...], g, m_ref[...], v_ref[...])
                go_ref[...] = g
                d_ref[...] = delta
                mo_ref[...] = mn
                vo_ref[...] = vn

    big = pl.BlockSpec((None, tm, C), lambda l, i: (l, i, 0))
    gspecs = [pl.BlockSpec((n_parts, tm, C), functools.partial(lambda l, i, ll: (0, jnp.where(l == ll, i, 0), 0), ll=ll))
              for ll in range(L)]
    return pl.pallas_call(
        body, name=name, grid=(L, nr),
        in_specs=[big, *gspecs, big, big],
        out_specs=[big, big, big, big],
        out_shape=tuple(jax.ShapeDtypeStruct((L, R, C), F32) for _ in range(4)),
        compiler_params=_params(("arbitrary", "arbitrary")),
    )(w, *parts, m, v)


def _pack(arrs, D):
    L = arrs[0].shape[0]
    cols = []
    for a in arrs:
        f = a.reshape(L, -1)
        n = f.shape[1]
        cols.append(jnp.pad(f, ((0, 0), (0, -(-n // D) * D - n))))
    flat = jnp.concatenate(cols, axis=1)
    return flat.reshape(L, flat.shape[1] // D, D)


def _unpack(p, shapes, D):
    L = p.shape[0]
    out, r = [], 0
    for s in shapes:
        n = math.prod(s[1:])
        rows = -(-n // D)
        out.append(p[:, r:r + rows].reshape(L, rows * D)[:, :n].reshape(s))
        r += rows
    return out


def kernel(x, c, w_in, b_f, conv_w, conv_b, conv_ln_g, conv_ln_b, w_o, w_ffn_in, w_ffn_out, mix_pre_g, mix_post_g, ffn_pre_g, ffn_post_g, ada_w, ada_b, loss_target, m_w_in, m_b_f, m_conv_w, m_conv_b, m_conv_ln_g, m_conv_ln_b, m_w_o, m_w_ffn_in, m_w_ffn_out, m_mix_pre_g, m_mix_post_g, m_ffn_pre_g, m_ffn_post_g, m_ada_w, m_ada_b, v_w_in, v_b_f, v_conv_w, v_conv_b, v_conv_ln_g, v_conv_ln_b, v_w_o, v_w_ffn_in, v_w_ffn_out, v_mix_pre_g, v_mix_post_g, v_ffn_pre_g, v_ffn_post_g, v_ada_w, v_ada_b):
    L, D, s_in = w_in.shape
    T = x.shape[1]
    H = b_f.shape[1]
    A = D // 2
    C = D - A
    cs = conv_w.shape[2]
    F = w_ffn_out.shape[1] * N_DEV
    s_ff = w_ffn_in.shape[2]
    w_ffn_in_t = jnp.transpose(w_ffn_in, (0, 2, 1))
    w_in_t = jnp.transpose(w_in, (0, 2, 1))
    s_ada = ada_w.shape[2]
    R = 16
    me = _my_index()
    x0 = x[0]
    target = loss_target[0]
    tq = _tile(T, 512)
    nq = T // tq

    c_all = _exchange(c, gather=True, name="gather_c").reshape(N_DEV, D)
    ada_b_loc = lax.dynamic_slice_in_dim(ada_b, me * s_ada, s_ada, axis=1)[:, None, :]
    mod_loc = _ada_fwd(c_all, ada_w, ada_b_loc, "ada_fwd")
    mod_g = _exchange(mod_loc, gather=True, name="gather_mod")
    mod = lax.dynamic_index_in_dim(mod_g, me, axis=2, keepdims=False)
    mod = jnp.transpose(mod, (1, 0, 2)).reshape(L, N_MOD, 1, D)
    cw_g = _exchange(conv_w, gather=True, name="gather_conv_w")
    conv_w_full = jnp.transpose(cw_g, (1, 2, 0, 3)).reshape(L, CONV_K, C)
    conv_w_pad = jnp.pad(conv_w_full, ((0, 0), (0, CONV_PAD - CONV_K), (0, 0)))
    b_f_col = jnp.pad(b_f, ((0, 0), (0, R - H)))[:, :, None]

    h_gmix, h_gffn, chain = [], [], (mod_g, cw_g)
    for l in range(L):
        h_gmix.append(_gather_start([w_in_t[l].astype(BF16), w_o[l].astype(BF16)], f"gather_mix_{l}", chain))
        h_gffn.append(_gather_start([w_ffn_in_t[l].astype(BF16), w_ffn_out[l].astype(BF16)], f"gather_ffn_{l}",
                                    (h_gmix[l]["token"],)))
        chain = (h_gffn[l]["token"],)
    gather_tokens = chain

    def by_rows(g):
        return g.reshape(N_DEV * g.shape[1], g.shape[2])

    W_in_t, W_f_t, W_c_t, W_o, W_ffn_in_t, W_ffn_out = ([None] * L for _ in range(6))

    saved = []
    xc = x0
    for l in range(L):
        sh1, sc1, g1, sh2, sc2, g2 = (mod[l, k] for k in range(N_MOD))
        gpre1, gpost1, gpre2, gpost2 = (p[l][None, :] for p in (mix_pre_g, mix_post_g, ffn_pre_g, ffn_post_g))
        dep = gather_tokens[0] if l == 0 else xc
        g_in, g_o = _gather_wait(_gather_mid(h_gmix[l], dep, f"gather_mid_mix_{l}"), dep, f"gather_wait_mix_{l}")
        W_in_t[l], W_o[l] = by_rows(g_in), by_rows(g_o)
        W_f_t[l] = jnp.pad(W_in_t[l][3 * A:3 * A + H], ((0, R - H), (0, 0)))
        W_c_t[l] = W_in_t[l][3 * A + H:]
        h1 = _prenorm(xc, gpre1, sc1, sh1, f"prenorm1_{l}", first=gather_tokens if l == 0 else ())
        qkv = _mm([(h1, W_in_t[l])], "nt", BF16, f"proj_qkv_{l}", tm=MM_ROWS, n_cols=3 * A)
        cproj = _mm([(h1, W_c_t[l])], "nt", F32, f"proj_conv_{l}", tm=MM_ROWS, tn=MM_COLS)
        fl, cum = _fgate_fwd(h1, W_f_t[l], b_f_col[l], f"fgate_{l}")
        cum_rows = cum[:H].reshape(H * nq, tq)
        o, lse = _attn_fwd(qkv, cum_rows, H, f"attn_{l}", tq)
        h_ffn_l = _gather_mid(h_gffn[l], o, f"gather_mid_ffn_{l}")
        u3 = _conv_fwd(cproj, conv_w_pad[l], conv_b[l][None, :], conv_ln_g[l][None, :], conv_ln_b[l][None, :], f"conv_{l}")
        cat = jnp.concatenate([o, u3], axis=-1)
        y1, x_mid = _mm_postnorm(cat, W_o[l], xc, g1, gpost1, f"out_proj_{l}")
        g_fi, g_fo = _gather_wait(h_ffn_l, x_mid, f"gather_wait_ffn_{l}")
        W_ffn_in_t[l], W_ffn_out[l] = by_rows(g_fi), by_rows(g_fo)
        h2 = _prenorm(x_mid, gpre2, sc2, sh2, f"prenorm2_{l}")
        g, u, act = _ffn_in_fwd(h2, W_ffn_in_t[l], f"ffn_in_{l}")
        y2, x_out = _mm_postnorm(act, W_ffn_out[l], x_mid, g2, gpost2, f"ffn_out_{l}")
        saved.append((xc, h1, qkv, cproj, fl, cum_rows, o, lse, cat, y1, x_mid, h2, g, u, act, y2))
        xc = x_out

    loss_tile, dx = _loss_head(xc, target, "loss_head")
    loss = lax.psum(loss_tile[0, 0], ("x", "y", "c"))

    mc = lax.axis_index("c")

    def reduce_start(parts, name):
        keeps, sends = [], []
        for part in parts:
            part4 = part.reshape(4, 2, *part.shape[1:])
            keeps.append(lax.dynamic_index_in_dim(part4, mc, axis=1, keepdims=False))
            sends.append(lax.dynamic_index_in_dim(part4, 1 - mc, axis=1, keepdims=False))
        gots = _swap_sibling(sends, "swap_" + name)
        boths = [_pair_sum(k.reshape(-1, k.shape[-1]), t.reshape(-1, t.shape[-1]), f"pairsum{a}_{name}").reshape(k.shape)
                 for a, (k, t) in enumerate(zip(keeps, gots))]
        return _chips_start(boths, "scatter_" + name)

    small, h_ffn, h_mix = [None] * L, [None] * L, [None] * L
    for l in reversed(range(L)):
        xin, h1, qkv, cproj, fl, cum_rows, o, lse, cat, y1, x_mid, h2, g, u, act, y2 = saved[l]
        sh1, sc1, g1, sh2, sc2, g2 = (mod[l, k] for k in range(N_MOD))
        gpre1, gpost1, gpre2, gpost2 = (p[l][None, :] for p in (mix_pre_g, mix_post_g, ffn_pre_g, ffn_post_g))
        dy2, sm_post2 = _postnorm_bwd(dx, y2, g2, gpost2, f"postnorm2_bwd_{l}")
        dgate, dup = _ffn_act_bwd(dy2, W_ffn_out[l], g, u, f"ffn_act_bwd_{l}")
        dW_ffn_out = _mm([(act, dy2)], "tn", BF16, f"dw_ffn_out_{l}", tm=DW_TILE)
        dh2 = _mm([(dgate, W_ffn_in_t[l], 0), (dup, W_ffn_in_t[l], 1)], "nn", F32, f"dh2_{l}", tn=FFN_COLS)
        dWg_t = _mm([(dgate, h2)], "tn", BF16, f"dw_ffn_gate_{l}", tm=DW_TILE)
        dWu_t = _mm([(dup, h2)], "tn", BF16, f"dw_ffn_up_{l}", tm=DW_TILE)
        half = N_DEV // 2
        dW_ffn_in_t = jnp.concatenate([dWg_t.reshape(half, s_ff, D), dWu_t.reshape(half, s_ff, D)], axis=0)
        h_ffn[l], token = reduce_start([dW_ffn_in_t, dW_ffn_out.reshape(N_DEV, F // N_DEV, D)], f"ffn_{l}")
        dx_mid, sm_pre2 = _prenorm_bwd(dh2, x_mid, dx, gpre2, sc2, f"prenorm2_bwd_{l}", first=(token,))
        dy1, sm_post1 = _postnorm_bwd(dx_mid, y1, g1, gpost1, f"postnorm1_bwd_{l}")
        dcat = _mm([(dy1, W_o[l])], "nt", F32, f"dcat_{l}", tm=MM_ROWS, tn=MM_COLS)
        dW_o = _mm([(cat, dy1)], "tn", BF16, f"dw_o_{l}", tm=DW_TILE)
        dqkv, dcum_rows, dcum_q = _attn_bwd(qkv, dcat, o, lse, cum_rows, H, f"attn_bwd_{l}", tq)
        dcproj, dconv_w, sm_conv = _conv_bwd(cproj, dcat, conv_w_pad[l], conv_b[l][None, :], conv_ln_g[l][None, :],
                                             conv_ln_b[l][None, :], f"conv_bwd_{l}")
        dcum = dcum_rows.reshape(H, T) + jnp.transpose(dcum_q, (0, 2, 1)).reshape(H, T)
        dcum = jnp.pad(dcum, ((0, R - H), (0, 0)))
        dfl_t, dwf_t, dbf = _fgate_bwd(dcum, fl, h1, f"fgate_bwd_{l}")
        dfl = jnp.transpose(dfl_t).astype(BF16)
        dh1 = _mm([(dqkv[0], W_in_t[l], 0), (dqkv[1], W_in_t[l], 1), (dqkv[2], W_in_t[l], 2),
                   (dfl, W_f_t[l]), (dcproj, W_c_t[l])], "nn", F32, f"dh1_{l}", tm=MM_ROWS, tn=FFN_COLS)
        dWq_t = [_mm([(dqkv[k], h1)], "tn", F32, f"dw_qkv{k}_{l}", tm=DW_TILE) for k in range(3)]
        dWc_t = _mm([(dcproj, h1)], "tn", F32, f"dw_conv_{l}", tm=DW_TILE)
        dW_in_t = jnp.concatenate(dWq_t + [dwf_t[:H], dWc_t], axis=0).astype(BF16)
        h_mix[l], token = reduce_start([dW_in_t.reshape(N_DEV, s_in, D), dW_o.reshape(N_DEV, D // N_DEV, D)], f"mix_{l}")
        dx, sm_pre1 = _prenorm_bwd(dh1, xin, dx_mid, gpre1, sc1, f"prenorm1_bwd_{l}", first=(token,))
        dmod = jnp.stack([sm_pre1[0], sm_pre1[1], sm_post1[0], sm_pre2[0], sm_pre2[1], sm_post2[0]])
        small[l] = (dmod, sm_pre1[2], sm_post1[1], sm_pre2[2], sm_post2[1], sm_conv[0], sm_conv[1], sm_conv[2],
                    dbf[:H, 0], dconv_w[:CONV_K])
    grad_x = dx[None]

    small_names = 10
    small_l = [jnp.stack([small[l][k] for l in range(L)]) for k in range(small_names)]
    small_shapes = [a.shape for a in small_l]
    packed = _pack(small_l, D)
    rows = packed.shape[1]
    rows_pad = -(-L * rows // 8) * 8
    packed2 = jnp.pad(packed.reshape(L * rows, D), ((0, rows_pad - L * rows), (0, 0)))
    h_small = _gather_start([packed2], "gather_small")

    got_ffn = [_chips_wait(h_ffn[l], h_small["token"], f"scatter_wait_ffn_{l}") for l in range(L)]
    got_mix = [_chips_wait(h_mix[l], h_small["token"], f"scatter_wait_mix_{l}") for l in range(L)]

    def step(w, m, v, parts, name):
        return list(_adamw_summed(w, parts, m, v, "adamw_" + name))

    swap = lambda t: jnp.transpose(t, (0, 2, 1))
    r_w_ffn_in = [swap(t) for t in step(w_ffn_in_t, swap(m_w_ffn_in), swap(v_w_ffn_in),
                                        [got_ffn[l][0] for l in range(L)], "w_ffn_in")]
    r_w_ffn_out = step(w_ffn_out, m_w_ffn_out, v_w_ffn_out, [got_ffn[l][1] for l in range(L)], "w_ffn_out")
    r_w_in = [swap(t) for t in step(w_in_t, swap(m_w_in), swap(v_w_in), [got_mix[l][0] for l in range(L)], "w_in")]
    r_w_o = step(w_o, m_w_o, v_w_o, [got_mix[l][1] for l in range(L)], "w_o")

    small_g = _gather_wait(_gather_mid(h_small, r_w_o[1], "gather_small_mid"), r_w_o[1], "gather_small_wait")[0]
    small_sum = _sum_devices(small_g, "sum_small")[:L * rows].reshape(L, rows, D)
    (g_ada_b6, g_mix_pre, g_mix_post, g_ffn_pre, g_ffn_post, g_conv_b, g_ln_g, g_ln_b, g_b_f,
     g_conv_w_full) = _unpack(small_sum, small_shapes, D)
    g_ada_b = g_ada_b6.reshape(L, N_MOD * D)
    g_conv_w = lax.dynamic_slice_in_dim(g_conv_w_full, me * cs, cs, axis=2)
    dmod_all = small_g[:, :L * rows].reshape(N_DEV, L, rows, D)[:, :, :N_MOD].reshape(N_DEV, L, N_MOD * D)
    dmod_loc = jnp.transpose(lax.dynamic_slice_in_dim(dmod_all, me * s_ada, s_ada, axis=2), (1, 0, 2))
    g_ada_w = _ada_bwd(jnp.transpose(c_all), dmod_loc, "ada_bwd")
    r_ada_w = [t.reshape(ada_w.shape) for t in _adamw(
        ada_w.reshape(L * D, s_ada), g_ada_w.reshape(L * D, s_ada), m_ada_w.reshape(L * D, s_ada),
        v_ada_w.reshape(L * D, s_ada), "adamw_ada_w", False)]

    sw = [b_f, conv_w, conv_b, conv_ln_g, conv_ln_b, mix_pre_g, mix_post_g, ffn_pre_g, ffn_post_g, ada_b]
    sg = [g_b_f, g_conv_w, g_conv_b, g_ln_g, g_ln_b, g_mix_pre, g_mix_post, g_ffn_pre, g_ffn_post, g_ada_b]
    sm = [m_b_f, m_conv_w, m_conv_b, m_conv_ln_g, m_conv_ln_b, m_mix_pre_g, m_mix_post_g, m_ffn_pre_g, m_ffn_post_g, m_ada_b]
    sv = [v_b_f, v_conv_w, v_conv_b, v_conv_ln_g, v_conv_ln_b, v_mix_pre_g, v_mix_post_g, v_ffn_pre_g, v_ffn_post_g, v_ada_b]
    shapes = [a.shape for a in sw]

    def flat(arrs):
        p = _pack(arrs, D)
        n = p.shape[0] * p.shape[1]
        return jnp.pad(p.reshape(n, D), ((0, -(-n // 8) * 8 - n), (0, 0))), p.shape

    pw, pshape = flat(sw)
    pg, pm, pv = flat(sg)[0], flat(sm)[0], flat(sv)[0]
    s_outs = _adamw(pw, pg, pm, pv, "adamw_small", False)
    n_small = pshape[0] * pshape[1]
    s_g, s_d, s_m, s_v = (_unpack(t[:n_small].reshape(pshape), shapes, D) for t in s_outs)

    big = {"w_in": r_w_in, "w_o": r_w_o, "w_ffn_in": r_w_ffn_in, "w_ffn_out": r_w_ffn_out, "ada_w": r_ada_w}
    order = ["w_in", "b_f", "conv_w", "conv_b", "conv_ln_g", "conv_ln_b", "w_o", "w_ffn_in", "w_ffn_out",
             "mix_pre_g", "mix_post_g", "ffn_pre_g", "ffn_post_g", "ada_w", "ada_b"]
    small_pos = {n: i for i, n in enumerate(["b_f", "conv_w", "conv_b", "conv_ln_g", "conv_ln_b", "mix_pre_g",
                                             "mix_post_g", "ffn_pre_g", "ffn_post_g", "ada_b"])}

    def pick(n, k):
        if n in big:
            return big[n][k]
        return (s_g, s_d, s_m, s_v)[k][small_pos[n]]

    return (loss, grad_x, *[pick(n, 0) for n in order], *[pick(n, 1) for n in order],
            *[pick(n, 2) for n in order], *[pick(n, 3) for n in order])
```

```python
import functools
import math

import jax
import jax.numpy as jnp
from jax import lax
from jax.experimental import pallas as pl
from jax.experimental.pallas import tpu as pltpu

F32 = jnp.float32
BF16 = jnp.bfloat16
MESH = pl.DeviceIdType.MESH
N_DEV = 8
EPS = 1e-6
CONV_K = 31
CONV_PAD = 32
CONV_CHUNK = 128
N_MOD = 6
NEG = -1e30
LANES = 128
VMEM_LIMIT = 56 * 2**20
MM_TILE = 1024
MM_ROWS = 2048
MM_COLS = 512
MM_ROWS_SMALL = 512
FFN_COLS = 256
DW_TILE = 1408
ADAM_LR, ADAM_B1, ADAM_B2, ADAM_EPS, ADAM_WD, ADAM_STEP = 0.001, 0.9, 0.999, 1e-08, 0.01, 10

NN = (((1,), (0,)), ((), ()))
NT = (((1,), (1,)), ((), ()))
TN = (((0,), (0,)), ((), ()))


def _dot(a, b, dims):
    return lax.dot_general(a, b, dims, preferred_element_type=F32)


def _tile(n, pref, align=LANES):
    if n <= pref:
        return n
    t = (pref // align) * align
    while t >= align:
        if n % t == 0:
            return t
        t -= align
    return n


def _params(sem=None):
    return pltpu.CompilerParams(dimension_semantics=sem, vmem_limit_bytes=VMEM_LIMIT)


def _sigmoid(x):
    return 1.0 / (1.0 + jnp.exp(-x))


def _my_index():
    return 4 * lax.axis_index("x") + 2 * lax.axis_index("y") + lax.axis_index("c")


def _exchange(x, *, gather, name):
    blk = x.shape if gather else x.shape[1:]

    def body(x_ref, y_ref, send_sems, recv_sems, local_sem):
        mx, my, mc = lax.axis_index("x"), lax.axis_index("y"), lax.axis_index("c")
        me = 4 * mx + 2 * my + mc

        def src(p):
            return x_ref if gather else x_ref.at[p]

        mine = pltpu.make_async_copy(src(me), y_ref.at[me], local_sem)
        mine.start()
        copies = []
        for k in range(1, N_DEV):
            px = (1 - mx) if (k >> 2) & 1 else mx
            py = (1 - my) if (k >> 1) & 1 else my
            pc = (1 - mc) if k & 1 else mc
            cp = pltpu.make_async_remote_copy(
                src_ref=src(4 * px + 2 * py + pc), dst_ref=y_ref.at[me],
                send_sem=send_sems.at[k - 1], recv_sem=recv_sems.at[k - 1],
                device_id=(px, py, pc), device_id_type=MESH)
            cp.start()
            copies.append(cp)
        for cp in copies:
            cp.wait()
        mine.wait()

    return pl.pallas_call(
        body, name=name,
        out_shape=jax.ShapeDtypeStruct((N_DEV,) + tuple(blk), x.dtype),
        in_specs=[pl.BlockSpec(memory_space=pl.ANY)],
        out_specs=pl.BlockSpec(memory_space=pl.ANY),
        scratch_shapes=[pltpu.SemaphoreType.DMA((N_DEV - 1,)), pltpu.SemaphoreType.DMA((N_DEV - 1,)),
                        pltpu.SemaphoreType.DMA(())],
    )(x)


_HBM = pl.BlockSpec(memory_space=pl.ANY)


_SEM = pl.BlockSpec(memory_space=pltpu.SEMAPHORE)
_HBM_SPEC = pl.BlockSpec(memory_space=pltpu.HBM)
_EFFECT = pltpu.CompilerParams(has_side_effects=pltpu.SideEffectType.DATAFLOW_SIDE_EFFECTING)


def _in_hbm(a):
    return pltpu.with_memory_space_constraint(a, pltpu.HBM)


def _scatter_copies(x_refs, land_refs, send_sems, recv_sems, loc_sems):
    mx, my, mc = lax.axis_index("x"), lax.axis_index("y"), lax.axis_index("c")
    me = 4 * mx + 2 * my + mc
    local, remote = [], []
    for a, (x_ref, land_ref) in enumerate(zip(x_refs, land_refs)):
        local.append(pltpu.make_async_copy(x_ref.at[me], land_ref.at[me], loc_sems.at[a]))
        for k in range(1, N_DEV):
            px = (1 - mx) if (k >> 2) & 1 else mx
            py = (1 - my) if (k >> 1) & 1 else my
            pc = (1 - mc) if k & 1 else mc
            remote.append(pltpu.make_async_remote_copy(
                src_ref=x_ref.at[4 * px + 2 * py + pc], dst_ref=land_ref.at[me],
                send_sem=send_sems.at[(N_DEV - 1) * a + k - 1], recv_sem=recv_sems.at[(N_DEV - 1) * a + k - 1],
                device_id=(px, py, pc), device_id_type=MESH))
    return local, remote


def _scatter_start(xs, name):
    n = len(xs)

    def body(*refs):
        local, remote = _scatter_copies(refs[:n], refs[n:2 * n], *refs[2 * n:2 * n + 3])
        for cp in local + remote:
            cp.start()
        refs[-1][...] = jnp.zeros_like(refs[-1])

    lands = [lax.empty(x.shape, x.dtype) for x in xs]
    n_sems = (N_DEV - 1) * n
    outs = pl.pallas_call(
        body, name=name,
        out_shape=(pltpu.SemaphoreType.DMA((n_sems,)), pltpu.SemaphoreType.DMA((n_sems,)), pltpu.SemaphoreType.DMA((n,)),
                   *[pltpu.HBM(x.shape, x.dtype) for x in xs], *[pltpu.HBM(x.shape, x.dtype) for x in xs],
                   jax.ShapeDtypeStruct((8, LANES), F32)),
        in_specs=[_HBM_SPEC] * (2 * n),
        out_specs=(_SEM, _SEM, _SEM, *[_HBM_SPEC] * (2 * n), pl.BlockSpec(memory_space=pltpu.VMEM)),
        input_output_aliases={i: 3 + i for i in range(2 * n)},
        compiler_params=_EFFECT,
    )(*[_in_hbm(x) for x in xs], *[_in_hbm(t) for t in lands])
    return (outs[0], outs[1], outs[2], list(outs[3:3 + n]), list(outs[3 + n:3 + 2 * n])), outs[-1]


def _scatter_wait(handle, after, name):
    send_sems, recv_sems, loc_sems, x_thru, land_thru = handle
    n = len(x_thru)

    def body(*refs):
        local, remote = _scatter_copies(refs[:n], refs[n:2 * n], *refs[2 * n:2 * n + 3])
        for cp in local:
            cp.wait()
        for cp in remote:
            cp.wait_send()
            cp.wait_recv()

    outs = pl.pallas_call(
        body, name=name,
        out_shape=[pltpu.HBM(x.shape, x.dtype) for x in x_thru + land_thru],
        in_specs=[_HBM_SPEC] * (2 * n) + [_SEM, _SEM, _SEM, _HBM],
        out_specs=[_HBM_SPEC] * (2 * n),
        input_output_aliases={i: i for i in range(2 * n)},
        compiler_params=_EFFECT,
    )(*x_thru, *land_thru, send_sems, recv_sems, loc_sems, after)
    return list(outs[n:])


def _gather_first_copies(x_refs, y_refs, send_sems, sib_recv, ici_recv, loc_sems):
    mx, my, mc = lax.axis_index("x"), lax.axis_index("y"), lax.axis_index("c")
    me = 4 * mx + 2 * my + mc
    local, remote = [], []
    for a, (x_ref, y_ref) in enumerate(zip(x_refs, y_refs)):
        local.append(pltpu.make_async_copy(x_ref, y_ref.at[me], loc_sems.at[a]))
        remote.append(pltpu.make_async_remote_copy(
            src_ref=x_ref, dst_ref=y_ref.at[me], send_sem=send_sems.at[4 * a], recv_sem=sib_recv.at[a],
            device_id=(mx, my, 1 - mc), device_id_type=MESH))
        for j, (px, py) in enumerate([(1 - mx, my), (mx, 1 - my), (1 - mx, 1 - my)]):
            remote.append(pltpu.make_async_remote_copy(
                src_ref=x_ref, dst_ref=y_ref.at[me], send_sem=send_sems.at[4 * a + 1 + j], recv_sem=ici_recv.at[3 * a + j],
                device_id=(px, py, mc), device_id_type=MESH))
    return local, remote


def _gather_forward_copies(y_refs, ici_recv, fwd_send, fwd_recv):
    mx, my, mc = lax.axis_index("x"), lax.axis_index("y"), lax.axis_index("c")
    pairs = []
    for a, y_ref in enumerate(y_refs):
        for j, (px, py) in enumerate([(1 - mx, my), (mx, 1 - my), (1 - mx, 1 - my)]):
            slot = y_ref.at[4 * px + 2 * py + mc]
            arrival = pltpu.make_async_remote_copy(
                src_ref=slot, dst_ref=slot, send_sem=fwd_send.at[3 * a + j], recv_sem=ici_recv.at[3 * a + j],
                device_id=(px, py, mc), device_id_type=MESH)
            onward = pltpu.make_async_remote_copy(
                src_ref=slot, dst_ref=slot, send_sem=fwd_send.at[3 * a + j], recv_sem=fwd_recv.at[3 * a + j],
                device_id=(mx, my, 1 - mc), device_id_type=MESH)
            pairs.append((arrival, onward))
    return pairs


def _gather_start(xs, name, after=()):
    n = len(xs)
    ni = 2 * n + len(after)

    def body(*refs):
        local, remote = _gather_first_copies(refs[:n], refs[n:2 * n], *refs[ni:ni + 4])
        for cp in local + remote:
            cp.start()
        refs[-1][...] = jnp.zeros_like(refs[-1])

    ys = [lax.empty((N_DEV,) + tuple(x.shape), x.dtype) for x in xs]
    dma = pltpu.SemaphoreType.DMA
    outs = pl.pallas_call(
        body, name=name,
        out_shape=(dma((4 * n,)), dma((n,)), dma((3 * n,)), dma((n,)),
                   *[pltpu.HBM(x.shape, x.dtype) for x in xs], *[pltpu.HBM(y.shape, y.dtype) for y in ys],
                   jax.ShapeDtypeStruct((8, LANES), F32)),
        in_specs=[_HBM_SPEC] * (2 * n) + [_HBM] * len(after),
        out_specs=(_SEM, _SEM, _SEM, _SEM, *[_HBM_SPEC] * (2 * n), pl.BlockSpec(memory_space=pltpu.VMEM)),
        input_output_aliases={i: 4 + i for i in range(2 * n)},
        compiler_params=_EFFECT,
    )(*[_in_hbm(x) for x in xs], *[_in_hbm(y) for y in ys], *after)
    return dict(send=outs[0], sib_recv=outs[1], ici_recv=outs[2], loc=outs[3], x=list(outs[4:4 + n]),
                y=list(outs[4 + n:4 + 2 * n]), token=outs[-1])


def _gather_mid(h, after, name):
    n = len(h["y"])

    def body(*refs):
        for arrival, onward in _gather_forward_copies(refs[:n], refs[n], refs[n + 2 + n], refs[n + 3 + n]):
            arrival.wait_recv()
            onward.start()

    dma = pltpu.SemaphoreType.DMA
    outs = pl.pallas_call(
        body, name=name,
        out_shape=(*[pltpu.HBM(y.shape, y.dtype) for y in h["y"]], dma((3 * n,)), dma((3 * n,))),
        in_specs=[_HBM_SPEC] * n + [_SEM, _HBM],
        out_specs=(*[_HBM_SPEC] * n, _SEM, _SEM),
        input_output_aliases={i: i for i in range(n)},
        compiler_params=_EFFECT,
    )(*h["y"], h["ici_recv"], after)
    return dict(h, y=list(outs[:n]), fwd_send=outs[n], fwd_recv=outs[n + 1])


def _gather_wait(h, after, name):
    n = len(h["y"])

    def body(*refs):
        x_refs, y_refs = refs[:n], refs[n:2 * n]
        send, sib_recv, loc, fwd_send, fwd_recv = refs[2 * n:2 * n + 5]
        local, remote = _gather_first_copies(x_refs, y_refs, send, sib_recv, fwd_recv, loc)
        for cp in local:
            cp.wait()
        for k, cp in enumerate(remote):
            cp.wait_send()
            if k % 4 == 0:
                cp.wait_recv()
        for _, onward in _gather_forward_copies(y_refs, fwd_recv, fwd_send, fwd_recv):
            onward.wait_send()
            onward.wait_recv()

    outs = pl.pallas_call(
        body, name=name,
        out_shape=[pltpu.HBM(t.shape, t.dtype) for t in h["x"] + h["y"]],
        in_specs=[_HBM_SPEC] * (2 * n) + [_SEM] * 5 + [_HBM],
        out_specs=[_HBM_SPEC] * (2 * n),
        input_output_aliases={i: i for i in range(2 * n)},
        compiler_params=_EFFECT,
    )(*h["x"], *h["y"], h["send"], h["sib_recv"], h["loc"], h["fwd_send"], h["fwd_recv"], after)
    return list(outs[n:])


def _mm(pairs, mode, out_dtype, name, tm=MM_TILE, tn=MM_TILE, n_cols=None):
    dims = {"nn": NN, "nt": NT, "tn": TN}[mode]
    a0, b0 = pairs[0][0], pairs[0][1]
    M = a0.shape[1] if mode == "tn" else a0.shape[0]
    N = n_cols or (b0.shape[0] if mode == "nt" else b0.shape[1])
    tm, tn = _tile(M, tm), _tile(N, tn)
    in_specs, args = [], []
    for pr in pairs:
        a, b = pr[0], pr[1]
        if mode == "tn":
            K = a.shape[0]
            in_specs.append(pl.BlockSpec((K, tm), lambda i, j: (0, i)))
            in_specs.append(pl.BlockSpec((K, tn), lambda i, j: (0, j)))
        elif mode == "nn":
            K = a.shape[1]
            rb = pr[2] if len(pr) > 2 else 0
            in_specs.append(pl.BlockSpec((tm, K), lambda i, j: (i, 0)))
            in_specs.append(pl.BlockSpec((K, tn), functools.partial(lambda i, j, rb: (rb, j), rb=rb)))
        else:
            K = a.shape[1]
            cb = pr[2] if len(pr) > 2 else 0
            in_specs.append(pl.BlockSpec((tm, K), lambda i, j: (i, 0)))
            in_specs.append(pl.BlockSpec((tn, K), functools.partial(lambda i, j, cb: (j, cb), cb=cb)))
        args += [a, b]
    n_pairs = len(pairs)

    def body(*refs):
        o_ref = refs[-1]
        acc = None
        for k in range(n_pairs):
            d = _dot(refs[2 * k][...], refs[2 * k + 1][...], dims)
            acc = d if acc is None else acc + d
        o_ref[...] = acc.astype(o_ref.dtype)

    return pl.pallas_call(
        body, name=name, grid=(M // tm, N // tn), in_specs=in_specs,
        out_specs=pl.BlockSpec((tm, tn), lambda i, j: (i, j)),
        out_shape=jax.ShapeDtypeStruct((M, N), out_dtype),
        compiler_params=_params(("parallel", "arbitrary")),
    )(*args)


def _prenorm(x, g, sc, sh, name, first=()):
    T, D = x.shape
    tm = _tile(T, 512, 8)

    def body(x_ref, g_ref, sc_ref, sh_ref, *rest):
        h_ref = rest[-1]
        xv = x_ref[...]
        r = lax.rsqrt(jnp.mean(xv * xv, axis=-1, keepdims=True) + EPS)
        h_ref[...] = (((xv * r) * g_ref[...]) * (1.0 + sc_ref[...]) + sh_ref[...]).astype(BF16)

    row = pl.BlockSpec((1, D), lambda i: (0, 0))
    return pl.pallas_call(
        body, name=name, grid=(T // tm,),
        in_specs=[pl.BlockSpec((tm, D), lambda i: (i, 0)), row, row, row] + [_HBM] * len(first),
        out_specs=pl.BlockSpec((tm, D), lambda i: (i, 0)),
        out_shape=jax.ShapeDtypeStruct((T, D), BF16),
        compiler_params=_params(("parallel",)),
    )(x, g, sc, sh, *first)


def _prenorm_bwd(dh, x, dres, g, sc, name, first=()):
    T, D = x.shape
    tm = _tile(T, 256, 8)

    def body(dh_ref, x_ref, dres_ref, g_ref, sc_ref, *rest):
        dx_ref, sm_ref = rest[-2:]
        @pl.when(pl.program_id(0) == 0)
        def _():
            sm_ref[...] = jnp.zeros_like(sm_ref)

        xv, dhv = x_ref[...], dh_ref[...]
        r = lax.rsqrt(jnp.mean(xv * xv, axis=-1, keepdims=True) + EPS)
        xh = xv * r
        one_sc = 1.0 + sc_ref[...]
        sm_ref[0:1, :] += jnp.sum(dhv, axis=0, keepdims=True)
        sm_ref[1:2, :] += jnp.sum(dhv * (xh * g_ref[...]), axis=0, keepdims=True)
        sm_ref[2:3, :] += jnp.sum(dhv * one_sc * xh, axis=0, keepdims=True)
        dxh = dhv * one_sc * g_ref[...]
        dx_ref[...] = dres_ref[...] + r * (dxh - xh * jnp.mean(dxh * xh, axis=-1, keepdims=True))

    row = pl.BlockSpec((1, D), lambda i: (0, 0))
    big = pl.BlockSpec((tm, D), lambda i: (i, 0))
    return pl.pallas_call(
        body, name=name, grid=(T // tm,),
        in_specs=[big, big, big, row, row] + [_HBM] * len(first),
        out_specs=[big, pl.BlockSpec((8, D), lambda i: (0, 0))],
        out_shape=(jax.ShapeDtypeStruct((T, D), F32), jax.ShapeDtypeStruct((8, D), F32)),
        compiler_params=_params(("arbitrary",)),
    )(dh, x, dres, g, sc, *first)


def _mm_postnorm(a, w, x, gate, gpost, name):
    T, K = a.shape
    D = w.shape[1]
    tm = _tile(T, MM_ROWS_SMALL, 8)

    def body(a_ref, w_ref, x_ref, gate_ref, gp_ref, y_ref, xn_ref):
        y = _dot(a_ref[...], w_ref[...], NN)
        r = lax.rsqrt(jnp.mean(y * y, axis=-1, keepdims=True) + EPS)
        y_ref[...] = y
        xn_ref[...] = x_ref[...] + gate_ref[...] * ((y * r) * gp_ref[...])

    row = pl.BlockSpec((1, D), lambda i: (0, 0))
    big = pl.BlockSpec((tm, D), lambda i: (i, 0))
    return pl.pallas_call(
        body, name=name, grid=(T // tm,),
        in_specs=[pl.BlockSpec((tm, K), lambda i: (i, 0)), pl.BlockSpec((K, D), lambda i: (0, 0)), big, row, row],
        out_specs=[big, big],
        out_shape=(jax.ShapeDtypeStruct((T, D), F32), jax.ShapeDtypeStruct((T, D), F32)),
        compiler_params=_params(("parallel",)),
    )(a, w, x, gate, gpost)


def _postnorm_bwd(dx, y, gate, gpost, name):
    T, D = y.shape
    tm = _tile(T, 256, 8)

    def body(dx_ref, y_ref, gate_ref, gp_ref, dy_ref, sm_ref):
        @pl.when(pl.program_id(0) == 0)
        def _():
            sm_ref[...] = jnp.zeros_like(sm_ref)

        yv, dxv = y_ref[...], dx_ref[...]
        r = lax.rsqrt(jnp.mean(yv * yv, axis=-1, keepdims=True) + EPS)
        yh = yv * r
        dn = dxv * gate_ref[...]
        sm_ref[0:1, :] += jnp.sum(dxv * (yh * gp_ref[...]), axis=0, keepdims=True)
        sm_ref[1:2, :] += jnp.sum(dn * yh, axis=0, keepdims=True)
        dyh = dn * gp_ref[...]
        dy_ref[...] = (r * (dyh - yh * jnp.mean(dyh * yh, axis=-1, keepdims=True))).astype(BF16)

    row = pl.BlockSpec((1, D), lambda i: (0, 0))
    big = pl.BlockSpec((tm, D), lambda i: (i, 0))
    return pl.pallas_call(
        body, name=name, grid=(T // tm,),
        in_specs=[big, big, row, row],
        out_specs=[big, pl.BlockSpec((8, D), lambda i: (0, 0))],
        out_shape=(jax.ShapeDtypeStruct((T, D), BF16), jax.ShapeDtypeStruct((8, D), F32)),
        compiler_params=_params(("arbitrary",)),
    )(dx, y, gate, gpost)


def _ffn_in_fwd(h, w_t, name):
    T, D = h.shape
    F = w_t.shape[0] // 2
    tm, tn = _tile(T, MM_ROWS), _tile(F, FFN_COLS)
    nj = F // tn

    def body(h_ref, wg_ref, wu_ref, g_ref, u_ref, act_ref):
        hv = h_ref[...]
        g = _dot(hv, wg_ref[...], NT)
        u = _dot(hv, wu_ref[...], NT)
        g_ref[...] = g.astype(BF16)
        u_ref[...] = u.astype(BF16)
        act_ref[...] = ((g * _sigmoid(g)) * u).astype(BF16)

    out = pl.BlockSpec((tm, tn), lambda j, i: (i, j))
    return pl.pallas_call(
        body, name=name, grid=(nj, T // tm),
        in_specs=[pl.BlockSpec((tm, D), lambda j, i: (i, 0)),
                  pl.BlockSpec((tn, D), lambda j, i: (j, 0)),
                  pl.BlockSpec((tn, D), lambda j, i: (j + nj, 0))],
        out_specs=[out, out, out],
        out_shape=tuple(jax.ShapeDtypeStruct((T, F), BF16) for _ in range(3)),
        compiler_params=_params(("parallel", "arbitrary")),
    )(h, w_t, w_t)


def _ffn_act_bwd(dy, w_out, g, u, name):
    T, D = dy.shape
    F = w_out.shape[0]
    tm, tn = _tile(T, MM_ROWS), _tile(F, FFN_COLS)

    def body(dy_ref, w_ref, g_ref, u_ref, dg_ref, du_ref):
        dact = _dot(dy_ref[...], w_ref[...], NT)
        gv, uv = g_ref[...].astype(F32), u_ref[...].astype(F32)
        sg = _sigmoid(gv)
        dg_ref[...] = (dact * uv * (sg * (1.0 + gv * (1.0 - sg)))).astype(BF16)
        du_ref[...] = (dact * (gv * sg)).astype(BF16)

    tile = pl.BlockSpec((tm, tn), lambda j, i: (i, j))
    return pl.pallas_call(
        body, name=name, grid=(F // tn, T // tm),
        in_specs=[pl.BlockSpec((tm, D), lambda j, i: (i, 0)), pl.BlockSpec((tn, D), lambda j, i: (j, 0)), tile, tile],
        out_specs=[tile, tile],
        out_shape=(jax.ShapeDtypeStruct((T, F), BF16), jax.ShapeDtypeStruct((T, F), BF16)),
        compiler_params=_params(("parallel", "arbitrary")),
    )(dy, w_out, g, u)


def _lane_scan(v, reverse):
    T = v.shape[-1]
    lane = lax.broadcasted_iota(jnp.int32, v.shape, 1)
    d = 1
    while d < T:
        if reverse:
            v = v + jnp.where(lane < T - d, pltpu.roll(v, T - d, axis=1), 0.0)
        else:
            v = v + jnp.where(lane >= d, pltpu.roll(v, d, axis=1), 0.0)
        d *= 2
    return v


def _fgate_fwd(h, wf_t, bf, name):
    T, D = h.shape
    R = wf_t.shape[0]

    def body(h_ref, w_ref, b_ref, fl_ref, cum_ref):
        fl = _dot(w_ref[...], h_ref[...], NT) + b_ref[...]
        fl_ref[...] = fl
        logf = jnp.minimum(fl, 0.0) - jnp.log(1.0 + jnp.exp(-jnp.abs(fl)))
        cum_ref[...] = _lane_scan(logf, reverse=False)

    return pl.pallas_call(
        body, name=name,
        out_shape=(jax.ShapeDtypeStruct((R, T), F32), jax.ShapeDtypeStruct((R, T), F32)),
        compiler_params=_params(),
    )(h, wf_t, bf)


def _fgate_bwd(dcum, fl, h, name):
    R, T = fl.shape
    D = h.shape[1]

    def body(dc_ref, fl_ref, h_ref, dfl_ref, dw_ref, db_ref):
        dlogf = _lane_scan(dc_ref[...], reverse=True)
        dfl = dlogf * _sigmoid(-fl_ref[...])
        dfl_ref[...] = dfl
        dw_ref[...] = _dot(dfl.astype(BF16), h_ref[...], NN)
        db_ref[...] = jnp.broadcast_to(jnp.sum(dfl, axis=-1, keepdims=True), (R, LANES))

    return pl.pallas_call(
        body, name=name,
        out_shape=(jax.ShapeDtypeStruct((R, T), F32), jax.ShapeDtypeStruct((R, D), F32),
                   jax.ShapeDtypeStruct((R, LANES), F32)),
        compiler_params=_params(),
    )(dcum, fl, h)


def _head_masks(hpb, dh, rows):
    lane = lax.broadcasted_iota(jnp.int32, (rows, LANES), 1)
    return [(lane >= h * dh) & (lane < (h + 1) * dh) for h in range(hpb)]


def _stack_heads(v, masks):
    return jnp.concatenate([jnp.where(mk, v, jnp.zeros_like(v)) for mk in masks], axis=0)


def _heads_to_lanes(col, masks, tq):
    out = jnp.broadcast_to(col[0:tq], (tq, LANES))
    for h in range(1, len(masks)):
        out = jnp.where(masks[h], col[h * tq:(h + 1) * tq], out)
    return out


def _causal_stack(hpb, tq):
    r = lax.broadcasted_iota(jnp.int32, (tq, tq), 0)
    c = lax.broadcasted_iota(jnp.int32, (tq, tq), 1)
    return jnp.concatenate([c] * hpb, axis=0) <= jnp.concatenate([r] * hpb, axis=0)


def _attn_fwd(qkv, cum_rows, n_heads, name, tq):
    T = qkv.shape[0]
    A = qkv.shape[1] // 3
    dh = A // n_heads
    hpb = LANES // dh
    nb = A // LANES
    nq = T // tq
    scale = dh ** -0.5

    def body(q_ref, k_ref, v_ref, c_ref, o_ref, l_ref, vbd):
        hp, i = pl.program_id(0), pl.program_id(1)
        masks = _head_masks(hpb, dh, tq)

        @pl.when(i == 0)
        def _():
            def fill(j, _):
                vbd[j] = _stack_heads(v_ref[pl.ds(pl.multiple_of(j * tq, tq), tq), :], masks)
                return 0

            lax.fori_loop(0, nq, fill, 0)

        qs = _stack_heads(q_ref[...], masks)
        crow0 = hp * hpb * nq

        def tile(j, carry, diag):
            m, l, acc = carry
            kt = k_ref[pl.ds(pl.multiple_of(j * tq, tq), tq), :]
            bias = jnp.concatenate(
                [jnp.broadcast_to(c_ref[pl.ds(crow0 + h * nq + j, 1), :], (tq, tq)) for h in range(hpb)], axis=0)
            s = _dot(qs, kt, NT) * scale - bias
            if diag:
                s = jnp.where(_causal_stack(hpb, tq), s, NEG)
            m_new = jnp.maximum(m, jnp.max(s, axis=-1, keepdims=True))
            p = jnp.exp(s - m_new)
            alpha = jnp.exp(m - m_new)
            l = alpha * l + jnp.sum(p, axis=-1, keepdims=True)
            pcat = jnp.concatenate([p[h * tq:(h + 1) * tq] for h in range(hpb)], axis=1).astype(BF16)
            acc = _heads_to_lanes(alpha, masks, tq) * acc + _dot(pcat, vbd[j], NN)
            return m_new, l, acc

        init = (jnp.full((hpb * tq, 1), NEG, F32), jnp.zeros((hpb * tq, 1), F32), jnp.zeros((tq, LANES), F32))
        carry = lax.fori_loop(0, i, lambda j, c: tile(j, c, False), init)
        m, l, acc = tile(i, carry, True)
        o_ref[...] = (acc / _heads_to_lanes(l, masks, tq)).astype(BF16)
        lse = m + jnp.log(l)
        for h in range(hpb):
            l_ref[:, h:h + 1] = lse[h * tq:(h + 1) * tq]

    return pl.pallas_call(
        body, name=name, grid=(nb, nq),
        in_specs=[pl.BlockSpec((tq, LANES), lambda h, i: (i, h)),
                  pl.BlockSpec((T, LANES), lambda h, i: (0, nb + h)),
                  pl.BlockSpec((T, LANES), lambda h, i: (0, 2 * nb + h)),
                  pl.BlockSpec(cum_rows.shape, lambda h, i: (0, 0))],
        out_specs=[pl.BlockSpec((tq, LANES), lambda h, i: (i, h)),
                   pl.BlockSpec((None, tq, hpb), lambda h, i: (h, i, 0))],
        out_shape=(jax.ShapeDtypeStruct((T, A), BF16), jax.ShapeDtypeStruct((nb, T, hpb), F32)),
        scratch_shapes=[pltpu.VMEM((nq, hpb * tq, LANES), BF16)],
        compiler_params=_params(("arbitrary", "arbitrary")),
    )(qkv, qkv, qkv, cum_rows)


def _attn_bwd(qkv, dcat, o, lse, cum_rows, n_heads, name, tq):
    T = qkv.shape[0]
    A = qkv.shape[1] // 3
    dh = A // n_heads
    hpb = LANES // dh
    nb = A // LANES
    nq = T // tq
    scale = dh ** -0.5

    def body(q_ref, k_ref, v_ref, do_ref, o_ref, l_ref, c_ref, dqkv_ref, dc_ref, dr_ref,
             dq_acc, delta, drow, qs_scr, dos_scr, kbd_scr):
        hp = pl.program_id(0)
        masks = _head_masks(hpb, dh, tq)
        crow0 = hp * hpb * nq

        def prologue(i, _):
            rs = pl.ds(pl.multiple_of(i * tq, tq), tq)
            do = do_ref[rs, :]
            prod = do * o_ref[rs, :].astype(F32)
            for h in range(hpb):
                delta[rs, h:h + 1] = jnp.sum(jnp.where(masks[h], prod, 0.0), axis=-1, keepdims=True)
            qs_scr[i] = _stack_heads(q_ref[rs, :], masks)
            dos_scr[i] = _stack_heads(do, masks).astype(BF16)
            kbd_scr[i] = _stack_heads(k_ref[rs, :], masks)
            dq_acc[rs, :] = jnp.zeros((tq, LANES), F32)
            drow[rs, :] = jnp.zeros((tq, hpb), F32)
            return 0

        lax.fori_loop(0, nq, prologue, 0)

        def kv_step(j, _):
            ks = pl.ds(pl.multiple_of(j * tq, tq), tq)
            kt, vt = k_ref[ks, :], v_ref[ks, :]
            kbd = kbd_scr[j]
            bias = jnp.concatenate(
                [jnp.broadcast_to(c_ref[pl.ds(crow0 + h * nq + j, 1), :], (tq, tq)) for h in range(hpb)], axis=0)

            def q_step(i, carry, diag):
                dk, dv, dcs = carry
                rs = pl.ds(pl.multiple_of(i * tq, tq), tq)
                qs, dos = qs_scr[i], dos_scr[i]
                s = _dot(qs, kt, NT) * scale - bias
                if diag:
                    s = jnp.where(_causal_stack(hpb, tq), s, NEG)
                lse = jnp.concatenate([l_ref[rs, h:h + 1] for h in range(hpb)], axis=0)
                p = jnp.exp(s - lse)
                dv = dv + _dot(p.astype(BF16), dos, TN)
                dp = _dot(dos, vt, NT)
                ds = p * (dp - jnp.concatenate([delta[rs, h:h + 1] for h in range(hpb)], axis=0))
                dcs = tuple(dcs[h] - jnp.sum(ds[h * tq:(h + 1) * tq], axis=0, keepdims=True) for h in range(hpb))
                rsum = jnp.sum(ds, axis=-1, keepdims=True)
                for h in range(hpb):
                    drow[rs, h:h + 1] += rsum[h * tq:(h + 1) * tq]
                dsb = (ds * scale).astype(BF16)
                dk = dk + _dot(dsb, qs, TN)
                dscat = jnp.concatenate([dsb[h * tq:(h + 1) * tq] for h in range(hpb)], axis=1)
                dq_acc[rs, :] += _dot(dscat, kbd, NN)
                return dk, dv, dcs

            init = (jnp.zeros((tq, LANES), F32), jnp.zeros((tq, LANES), F32),
                    tuple(jnp.zeros((1, tq), F32) for _ in range(hpb)))
            carry = q_step(j, init, True)
            dk, dv, dcs = lax.fori_loop(j + 1, nq, lambda i, c: q_step(i, c, False), carry)
            dqkv_ref[1, ks, :] = dk.astype(BF16)
            dqkv_ref[2, ks, :] = dv.astype(BF16)
            for h in range(hpb):
                dc_ref[pl.ds(crow0 + h * nq + j, 1), :] = dcs[h]
            return 0

        lax.fori_loop(0, nq, kv_step, 0)
        dqkv_ref[0] = dq_acc[...].astype(BF16)
        dr_ref[...] = drow[...]

    col = lambda off: pl.BlockSpec((T, LANES), functools.partial(lambda h, off: (0, off + h), off=off))
    return pl.pallas_call(
        body, name=name, grid=(nb,),
        in_specs=[col(0), col(nb), col(2 * nb), col(0), col(0),
                  pl.BlockSpec((None, T, hpb), lambda h: (h, 0, 0)),
                  pl.BlockSpec(cum_rows.shape, lambda h: (0, 0))],
        out_specs=[pl.BlockSpec((3, T, LANES), lambda h: (0, 0, h)),
                   pl.BlockSpec(cum_rows.shape, lambda h: (0, 0)),
                   pl.BlockSpec((None, T, hpb), lambda h: (h, 0, 0))],
        out_shape=(jax.ShapeDtypeStruct((3, T, A), BF16), jax.ShapeDtypeStruct(cum_rows.shape, F32),
                   jax.ShapeDtypeStruct((nb, T, hpb), F32)),
        scratch_shapes=[pltpu.VMEM((T, LANES), F32), pltpu.VMEM((T, hpb), F32), pltpu.VMEM((T, hpb), F32),
                        pltpu.VMEM((nq, hpb * tq, LANES), BF16), pltpu.VMEM((nq, hpb * tq, LANES), BF16),
                        pltpu.VMEM((nq, hpb * tq, LANES), BF16)],
        compiler_params=_params(("arbitrary",)),
    )(qkv, qkv, qkv, dcat, o, lse, cum_rows)


def _glu_into(upad, cv_ref, cg_ref, T):
    upad[0:CONV_PAD, :] = jnp.zeros((CONV_PAD, upad.shape[1]), F32)

    def fill(c, _):
        rs = pl.ds(pl.multiple_of(c * CONV_CHUNK, CONV_CHUNK), CONV_CHUNK)
        upad[pl.ds(pl.multiple_of(CONV_PAD + c * CONV_CHUNK, 8), CONV_CHUNK), :] = cv_ref[rs, :] * _sigmoid(cg_ref[rs, :])
        return 0

    lax.fori_loop(0, T // CONV_CHUNK, fill, 0)


SUBLANES = 8
CONV_SHIFT_ROWS = CONV_CHUNK + CONV_PAD - SUBLANES


def _load_window(win, sh, src, r0):
    win[...] = src[pl.ds(r0, CONV_CHUNK + CONV_PAD), :]
    for b in range(1, SUBLANES):
        sh[b - 1] = win[b:b + CONV_SHIFT_ROWS, :]


def _tap(win, sh, o):
    b = o % SUBLANES
    if b == 0:
        return win[o:o + CONV_CHUNK, :]
    return sh[b - 1, o - b:o - b + CONV_CHUNK, :]


def _conv_taps(win, sh, w_ref, first, step):
    acc = None
    for k in range(CONV_K):
        t = w_ref[k:k + 1, :] * _tap(win, sh, first + step * k)
        acc = t if acc is None else acc + t
    return acc


def _conv_scratch(C):
    return [pltpu.VMEM((CONV_CHUNK + CONV_PAD, C), F32), pltpu.VMEM((SUBLANES - 1, CONV_SHIFT_ROWS, C), F32)]


def _conv_fwd(cproj, w, b, lg, lb, name):
    T = cproj.shape[0]
    C = cproj.shape[1] // 2
    off = CONV_PAD - (CONV_K - 1)

    def body(cv_ref, cg_ref, w_ref, b_ref, lg_ref, lb_ref, out_ref, upad, win, sh):
        _glu_into(upad, cv_ref, cg_ref, T)

        def chunk(c, _):
            r0 = pl.multiple_of(c * CONV_CHUNK, CONV_CHUNK)
            _load_window(win, sh, upad, r0)
            u1 = _conv_taps(win, sh, w_ref, off, 1) + b_ref[...]
            mu = jnp.mean(u1, axis=-1, keepdims=True)
            var = jnp.mean(jnp.square(u1 - mu), axis=-1, keepdims=True)
            u2 = ((u1 - mu) * lax.rsqrt(var + EPS)) * lg_ref[...] + lb_ref[...]
            out_ref[pl.ds(r0, CONV_CHUNK), :] = (u2 * _sigmoid(u2)).astype(BF16)
            return 0

        lax.fori_loop(0, T // CONV_CHUNK, chunk, 0)

    row = pl.BlockSpec((1, C), lambda i: (0, 0))
    return pl.pallas_call(
        body, name=name, grid=(1,),
        in_specs=[pl.BlockSpec((T, C), lambda i: (0, 0)), pl.BlockSpec((T, C), lambda i: (0, 1)),
                  pl.BlockSpec(w.shape, lambda i: (0, 0)), row, row, row],
        out_specs=pl.BlockSpec((T, C), lambda i: (0, 0)),
        out_shape=jax.ShapeDtypeStruct((T, C), BF16),
        scratch_shapes=[pltpu.VMEM((T + CONV_PAD, C), F32)] + _conv_scratch(C),
        compiler_params=_params(("arbitrary",)),
    )(cproj, cproj, w, b, lg, lb)


def _conv_bwd(cproj, dcat, w, b, lg, lb, name):
    T = cproj.shape[0]
    C = cproj.shape[1] // 2
    off = CONV_PAD - (CONV_K - 1)
    n_chunks = T // CONV_CHUNK

    def fold(v):
        return jnp.sum(v.reshape(CONV_CHUNK // 8, 8, C), axis=0)

    def body(cv_ref, cg_ref, du_ref, w_ref, b_ref, lg_ref, lb_ref, dc_ref, dw_ref, sm_ref,
             upad, dpad, dwacc, smacc, win, sh):
        _glu_into(upad, cv_ref, cg_ref, T)
        dpad[pl.ds(T, CONV_PAD), :] = jnp.zeros((CONV_PAD, C), F32)
        dwacc[...] = jnp.zeros_like(dwacc)
        smacc[...] = jnp.zeros_like(smacc)

        def chunk_a(c, _):
            r0 = pl.multiple_of(c * CONV_CHUNK, CONV_CHUNK)
            _load_window(win, sh, upad, r0)
            u1 = _conv_taps(win, sh, w_ref, off, 1) + b_ref[...]
            mu = jnp.mean(u1, axis=-1, keepdims=True)
            var = jnp.mean(jnp.square(u1 - mu), axis=-1, keepdims=True)
            rstd = lax.rsqrt(var + EPS)
            u1h = (u1 - mu) * rstd
            u2 = u1h * lg_ref[...] + lb_ref[...]
            sg = _sigmoid(u2)
            du2 = du_ref[pl.ds(r0, CONV_CHUNK), :] * (sg * (1.0 + u2 * (1.0 - sg)))
            smacc[8:16, :] += fold(du2 * u1h)
            smacc[16:24, :] += fold(du2)
            du1h = du2 * lg_ref[...]
            du1 = rstd * (du1h - jnp.mean(du1h, axis=-1, keepdims=True)
                          - u1h * jnp.mean(du1h * u1h, axis=-1, keepdims=True))
            smacc[0:8, :] += fold(du1)
            dpad[pl.ds(r0, CONV_CHUNK), :] = du1
            for k in range(CONV_K):
                dwacc[8 * k:8 * k + 8, :] += fold(du1 * _tap(win, sh, off + k))
            return 0

        lax.fori_loop(0, n_chunks, chunk_a, 0)

        def chunk_b(c, _):
            r0 = pl.multiple_of(c * CONV_CHUNK, CONV_CHUNK)
            rs = pl.ds(r0, CONV_CHUNK)
            _load_window(win, sh, dpad, r0)
            du0 = _conv_taps(win, sh, w_ref, CONV_K - 1, -1)
            cv, sg = cv_ref[rs, :], _sigmoid(cg_ref[rs, :])
            dc_ref[rs, 0:C] = (du0 * sg).astype(BF16)
            dc_ref[rs, C:2 * C] = (du0 * cv * (sg * (1.0 - sg))).astype(BF16)
            return 0

        lax.fori_loop(0, n_chunks, chunk_b, 0)
        for k in range(CONV_K):
            dw_ref[k:k + 1, :] = jnp.sum(dwacc[8 * k:8 * k + 8, :], axis=0, keepdims=True)
        dw_ref[CONV_K:CONV_PAD, :] = jnp.zeros((CONV_PAD - CONV_K, C), F32)
        for r in range(3):
            sm_ref[r:r + 1, :] = jnp.sum(smacc[8 * r:8 * r + 8, :], axis=0, keepdims=True)
        sm_ref[3:8, :] = jnp.zeros((5, C), F32)

    row = pl.BlockSpec((1, C), lambda i: (0, 0))
    return pl.pallas_call(
        body, name=name, grid=(1,),
        in_specs=[pl.BlockSpec((T, C), lambda i: (0, 0)), pl.BlockSpec((T, C), lambda i: (0, 1)),
                  pl.BlockSpec((T, C), lambda i: (0, 1)),
                  pl.BlockSpec(w.shape, lambda i: (0, 0)), row, row, row],
        out_specs=[pl.BlockSpec((T, 2 * C), lambda i: (0, 0)), pl.BlockSpec((CONV_PAD, C), lambda i: (0, 0)),
                   pl.BlockSpec((8, C), lambda i: (0, 0))],
        out_shape=(jax.ShapeDtypeStruct((T, 2 * C), BF16), jax.ShapeDtypeStruct((CONV_PAD, C), F32),
                   jax.ShapeDtypeStruct((8, C), F32)),
        scratch_shapes=[pltpu.VMEM((T + CONV_PAD, C), F32), pltpu.VMEM((T + CONV_PAD, C), F32),
                        pltpu.VMEM((8 * CONV_PAD, C), F32), pltpu.VMEM((24, C), F32)] + _conv_scratch(C),
        compiler_params=_params(("arbitrary",)),
    )(cproj, cproj, dcat, w, b, lg, lb)


def _loss_head(x, target, name):
    T, D = x.shape
    tm = _tile(T, 512, 8)

    def body(x_ref, t_ref, loss_ref, dx_ref):
        @pl.when(pl.program_id(0) == 0)
        def _():
            loss_ref[...] = jnp.zeros_like(loss_ref)

        err = x_ref[...] - t_ref[...]
        part = jnp.sum(jnp.mean(err * err, axis=-1, keepdims=True), axis=0, keepdims=True)
        loss_ref[...] += jnp.broadcast_to(0.5 * part, loss_ref.shape)
        dx_ref[...] = err * (1.0 / D)

    big = pl.BlockSpec((tm, D), lambda i: (i, 0))
    return pl.pallas_call(
        body, name=name, grid=(T // tm,),
        in_specs=[big, big],
        out_specs=[pl.BlockSpec((8, LANES), lambda i: (0, 0)), big],
        out_shape=(jax.ShapeDtypeStruct((8, LANES), F32), jax.ShapeDtypeStruct((T, D), F32)),
        compiler_params=_params(("arbitrary",)),
    )(x, target)


def _ada_fwd(c_all, ada_w, ada_b_loc, name):
    L, D, S = ada_w.shape
    B = c_all.shape[0]

    def body(c_ref, w_ref, b_ref, o_ref):
        c = c_ref[...]
        ca = (c * _sigmoid(c)).astype(BF16)
        o_ref[...] = _dot(ca, w_ref[...].astype(BF16), NN) + b_ref[...]

    return pl.pallas_call(
        body, name=name, grid=(L,),
        in_specs=[pl.BlockSpec((B, D), lambda l: (0, 0)), pl.BlockSpec((None, D, S), lambda l: (l, 0, 0)),
                  pl.BlockSpec((None, 1, S), lambda l: (l, 0, 0))],
        out_specs=pl.BlockSpec((None, B, S), lambda l: (l, 0, 0)),
        out_shape=jax.ShapeDtypeStruct((L, B, S), F32),
        compiler_params=_params(("parallel",)),
    )(c_all, ada_w, ada_b_loc)


def _ada_bwd(c_all_t, dmod_loc, name):
    D, B = c_all_t.shape
    L, _, S = dmod_loc.shape

    def body(c_ref, dm_ref, o_ref):
        c = c_ref[...]
        ca = c * _sigmoid(c)
        acc = None
        for bb in range(B):
            t = ca[:, bb:bb + 1] * dm_ref[bb:bb + 1, :]
            acc = t if acc is None else acc + t
        o_ref[...] = acc

    return pl.pallas_call(
        body, name=name, grid=(L,),
        in_specs=[pl.BlockSpec((D, B), lambda l: (0, 0)), pl.BlockSpec((None, B, S), lambda l: (l, 0, 0))],
        out_specs=pl.BlockSpec((None, D, S), lambda l: (l, 0, 0)),
        out_shape=jax.ShapeDtypeStruct((L, D, S), F32),
        compiler_params=_params(("parallel",)),
    )(c_all_t, dmod_loc)


def _sum_devices(parts, name):
    _, R, C = parts.shape
    tm = _tile(R, 256, 8)

    def body(p_ref, o_ref):
        acc = p_ref[0].astype(F32)
        for d in range(1, N_DEV):
            acc = acc + p_ref[d].astype(F32)
        o_ref[...] = acc

    return pl.pallas_call(
        body, name=name, grid=(R // tm,),
        in_specs=[pl.BlockSpec((N_DEV, tm, C), lambda i: (0, i, 0))],
        out_specs=pl.BlockSpec((tm, C), lambda i: (i, 0)),
        out_shape=jax.ShapeDtypeStruct((R, C), F32),
        compiler_params=_params(("parallel",)),
    )(parts)


def _adamw_math(w, g, m, v):
    m = ADAM_B1 * m + (1.0 - ADAM_B1) * g
    v = ADAM_B2 * v + (1.0 - ADAM_B2) * (g * g)
    m_hat = m / (1.0 - ADAM_B1 ** ADAM_STEP)
    v_hat = v / (1.0 - ADAM_B2 ** ADAM_STEP)
    delta = -ADAM_LR * (m_hat / (jnp.sqrt(v_hat) + ADAM_EPS) + ADAM_WD * w)
    return delta, m, v


def _adamw(w, g, m, v, name, summed):
    R, C = w.shape
    tm = _tile(R, 256, 16)
    n_parts = g.shape[0] if summed else 0

    def body(w_ref, g_ref, m_ref, v_ref, go_ref, d_ref, mo_ref, vo_ref):
        if summed:
            g = g_ref[0].astype(F32)
            for d in range(1, n_parts):
                g = g + g_ref[d].astype(F32)
        else:
            g = g_ref[...]
        delta, mn, vn = _adamw_math(w_ref[...], g, m_ref[...], v_ref[...])
        go_ref[...] = g
        d_ref[...] = delta
        mo_ref[...] = mn
        vo_ref[...] = vn

    big = pl.BlockSpec((tm, C), lambda i: (i, 0))
    gspec = pl.BlockSpec((n_parts, tm, C), lambda i: (0, i, 0)) if summed else big
    return pl.pallas_call(
        body, name=name, grid=(R // tm,),
        in_specs=[big, gspec, big, big],
        out_specs=[big, big, big, big],
        out_shape=tuple(jax.ShapeDtypeStruct((R, C), F32) for _ in range(4)),
        compiler_params=_params(("parallel",)),
    )(w, g, m, v)


def _adamw_summed(w, parts, m, v, name):
    L = len(parts)
    n_parts, R, C = parts[0].shape
    tm = _tile(R, 128, 16)
    tc = C if tm < R else _tile(C, 256)
    nr = (R // tm) * (C // tc)
    ncb = C // tc

    def body(*refs):
        w_ref, g_refs = refs[0], refs[1:1 + L]
        m_ref, v_ref, go_ref, d_ref, mo_ref, vo_ref = refs[1 + L:]
        for ll in range(L):
            @pl.when(pl.program_id(0) == ll)
            def _(ll=ll):
                g = g_refs[ll][0].astype(F32)
                for d in range(1, n_parts):
                    g = g + g_refs[ll][d].astype(F32)
                delta, mn, vn = _adamw_math(w_ref[...], g, m_ref[...], v_ref[...])
                go_ref[...] = g
                d_ref[...] = delta
                mo_ref[...] = mn
                vo_ref[...] = vn

    big = pl.BlockSpec((None, tm, tc), lambda l, i: (l, i // ncb, i % ncb))

    def part_index(l, i, ll):
        i = jnp.where(l == ll, i, 0)
        return 0, i // ncb, i % ncb

    gspecs = [pl.BlockSpec((n_parts, tm, tc), functools.partial(part_index, ll=ll)) for ll in range(L)]
    return pl.pallas_call(
        body, name=name, grid=(L, nr),
        in_specs=[big, *gspecs, big, big],
        out_specs=[big, big, big, big],
        out_shape=tuple(jax.ShapeDtypeStruct((L, R, C), F32) for _ in range(4)),
        compiler_params=_params(("arbitrary", "arbitrary")),
    )(w, *parts, m, v)


def _pack(arrs, D):
    L = arrs[0].shape[0]
    cols = []
    for a in arrs:
        f = a.reshape(L, -1)
        n = f.shape[1]
        cols.append(jnp.pad(f, ((0, 0), (0, -(-n // D) * D - n))))
    flat = jnp.concatenate(cols, axis=1)
    return flat.reshape(L, flat.shape[1] // D, D)


def _unpack(p, shapes, D):
    L = p.shape[0]
    out, r = [], 0
    for s in shapes:
        n = math.prod(s[1:])
        rows = -(-n // D)
        out.append(p[:, r:r + rows].reshape(L, rows * D)[:, :n].reshape(s))
        r += rows
    return out


def kernel(x, c, w_in, b_f, conv_w, conv_b, conv_ln_g, conv_ln_b, w_o, w_ffn_in, w_ffn_out, mix_pre_g, mix_post_g, ffn_pre_g, ffn_post_g, ada_w, ada_b, loss_target, m_w_in, m_b_f, m_conv_w, m_conv_b, m_conv_ln_g, m_conv_ln_b, m_w_o, m_w_ffn_in, m_w_ffn_out, m_mix_pre_g, m_mix_post_g, m_ffn_pre_g, m_ffn_post_g, m_ada_w, m_ada_b, v_w_in, v_b_f, v_conv_w, v_conv_b, v_conv_ln_g, v_conv_ln_b, v_w_o, v_w_ffn_in, v_w_ffn_out, v_mix_pre_g, v_mix_post_g, v_ffn_pre_g, v_ffn_post_g, v_ada_w, v_ada_b):
    L, D, s_in = w_in.shape
    T = x.shape[1]
    H = b_f.shape[1]
    A = D // 2
    C = D - A
    cs = conv_w.shape[2]
    F = w_ffn_out.shape[1] * N_DEV
    s_ff = w_ffn_in.shape[2]
    w_ffn_in_t = jnp.transpose(w_ffn_in, (0, 2, 1))
    w_in_t = jnp.transpose(w_in, (0, 2, 1))
    s_ada = ada_w.shape[2]
    R = 16
    me = _my_index()
    x0 = x[0]
    target = loss_target[0]
    tq = _tile(T, 512)
    nq = T // tq

    c_all = _exchange(c, gather=True, name="gather_c").reshape(N_DEV, D)
    ada_b_loc = lax.dynamic_slice_in_dim(ada_b, me * s_ada, s_ada, axis=1)[:, None, :]
    mod_loc = _ada_fwd(c_all, ada_w, ada_b_loc, "ada_fwd")
    mod_g = _exchange(mod_loc, gather=True, name="gather_mod")
    mod = lax.dynamic_index_in_dim(mod_g, me, axis=2, keepdims=False)
    mod = jnp.transpose(mod, (1, 0, 2)).reshape(L, N_MOD, 1, D)
    cw_g = _exchange(conv_w, gather=True, name="gather_conv_w")
    conv_w_full = jnp.transpose(cw_g, (1, 2, 0, 3)).reshape(L, CONV_K, C)
    conv_w_pad = jnp.pad(conv_w_full, ((0, 0), (0, CONV_PAD - CONV_K), (0, 0)))
    b_f_col = jnp.pad(b_f, ((0, 0), (0, R - H)))[:, :, None]

    h_gmix, h_gffn, chain = [], [], (mod_g, cw_g)
    for l in range(L):
        h_gmix.append(_gather_start([w_in_t[l].astype(BF16), w_o[l].astype(BF16)], f"gather_mix_{l}", chain))
        h_gffn.append(_gather_start([w_ffn_in_t[l].astype(BF16), w_ffn_out[l].astype(BF16)], f"gather_ffn_{l}",
                                    (h_gmix[l]["token"],)))
        chain = (h_gffn[l]["token"],)
    gather_tokens = chain

    def by_rows(g):
        return g.reshape(N_DEV * g.shape[1], g.shape[2])

    W_in_t, W_f_t, W_c_t, W_o, W_ffn_in_t, W_ffn_out = ([None] * L for _ in range(6))

    saved = []
    xc = x0
    for l in range(L):
        sh1, sc1, g1, sh2, sc2, g2 = (mod[l, k] for k in range(N_MOD))
        gpre1, gpost1, gpre2, gpost2 = (p[l][None, :] for p in (mix_pre_g, mix_post_g, ffn_pre_g, ffn_post_g))
        dep = gather_tokens[0] if l == 0 else xc
        g_in, g_o = _gather_wait(_gather_mid(h_gmix[l], dep, f"gather_mid_mix_{l}"), dep, f"gather_wait_mix_{l}")
        W_in_t[l], W_o[l] = by_rows(g_in), by_rows(g_o)
        W_f_t[l] = jnp.pad(W_in_t[l][3 * A:3 * A + H], ((0, R - H), (0, 0)))
        W_c_t[l] = W_in_t[l][3 * A + H:]
        h1 = _prenorm(xc, gpre1, sc1, sh1, f"prenorm1_{l}", first=gather_tokens if l == 0 else ())
        qkv = _mm([(h1, W_in_t[l])], "nt", BF16, f"proj_qkv_{l}", tm=MM_ROWS, n_cols=3 * A)
        cproj = _mm([(h1, W_c_t[l])], "nt", F32, f"proj_conv_{l}", tm=MM_ROWS, tn=MM_COLS)
        fl, cum = _fgate_fwd(h1, W_f_t[l], b_f_col[l], f"fgate_{l}")
        cum_rows = cum[:H].reshape(H * nq, tq)
        o, lse = _attn_fwd(qkv, cum_rows, H, f"attn_{l}", tq)
        h_ffn_l = _gather_mid(h_gffn[l], o, f"gather_mid_ffn_{l}")
        u3 = _conv_fwd(cproj, conv_w_pad[l], conv_b[l][None, :], conv_ln_g[l][None, :], conv_ln_b[l][None, :], f"conv_{l}")
        cat = jnp.concatenate([o, u3], axis=-1)
        y1, x_mid = _mm_postnorm(cat, W_o[l], xc, g1, gpost1, f"out_proj_{l}")
        g_fi, g_fo = _gather_wait(h_ffn_l, x_mid, f"gather_wait_ffn_{l}")
        W_ffn_in_t[l], W_ffn_out[l] = by_rows(g_fi), by_rows(g_fo)
        h2 = _prenorm(x_mid, gpre2, sc2, sh2, f"prenorm2_{l}")
        g, u, act = _ffn_in_fwd(h2, W_ffn_in_t[l], f"ffn_in_{l}")
        y2, x_out = _mm_postnorm(act, W_ffn_out[l], x_mid, g2, gpost2, f"ffn_out_{l}")
        saved.append((xc, h1, qkv, cproj, fl, cum_rows, o, lse, cat, y1, x_mid, h2, g, u, act, y2))
        xc = x_out

    loss_tile, dx = _loss_head(xc, target, "loss_head")
    loss = lax.psum(loss_tile[0, 0], ("x", "y", "c"))

    def reduce_start(parts, name):
        return _scatter_start(parts, "scatter_" + name)

    small, h_ffn, h_mix = [None] * L, [None] * L, [None] * L
    for l in reversed(range(L)):
        xin, h1, qkv, cproj, fl, cum_rows, o, lse, cat, y1, x_mid, h2, g, u, act, y2 = saved[l]
        sh1, sc1, g1, sh2, sc2, g2 = (mod[l, k] for k in range(N_MOD))
        gpre1, gpost1, gpre2, gpost2 = (p[l][None, :] for p in (mix_pre_g, mix_post_g, ffn_pre_g, ffn_post_g))
        dy2, sm_post2 = _postnorm_bwd(dx, y2, g2, gpost2, f"postnorm2_bwd_{l}")
        dgate, dup = _ffn_act_bwd(dy2, W_ffn_out[l], g, u, f"ffn_act_bwd_{l}")
        dW_ffn_out = _mm([(act, dy2)], "tn", BF16, f"dw_ffn_out_{l}", tm=DW_TILE)
        dh2 = _mm([(dgate, W_ffn_in_t[l], 0), (dup, W_ffn_in_t[l], 1)], "nn", F32, f"dh2_{l}", tn=FFN_COLS)
        dWg_t = _mm([(dgate, h2)], "tn", BF16, f"dw_ffn_gate_{l}", tm=DW_TILE)
        dWu_t = _mm([(dup, h2)], "tn", BF16, f"dw_ffn_up_{l}", tm=DW_TILE)
        half = N_DEV // 2
        dW_ffn_in_t = jnp.concatenate([dWg_t.reshape(half, s_ff, D), dWu_t.reshape(half, s_ff, D)], axis=0)
        h_ffn[l], token = reduce_start([dW_ffn_in_t, dW_ffn_out.reshape(N_DEV, F // N_DEV, D)], f"ffn_{l}")
        dx_mid, sm_pre2 = _prenorm_bwd(dh2, x_mid, dx, gpre2, sc2, f"prenorm2_bwd_{l}", first=(token,))
        dy1, sm_post1 = _postnorm_bwd(dx_mid, y1, g1, gpost1, f"postnorm1_bwd_{l}")
        dcat = _mm([(dy1, W_o[l])], "nt", F32, f"dcat_{l}", tm=MM_ROWS, tn=MM_COLS)
        dW_o = _mm([(cat, dy1)], "tn", BF16, f"dw_o_{l}", tm=DW_TILE)
        dqkv, dcum_rows, dcum_q = _attn_bwd(qkv, dcat, o, lse, cum_rows, H, f"attn_bwd_{l}", tq)
        dcproj, dconv_w, sm_conv = _conv_bwd(cproj, dcat, conv_w_pad[l], conv_b[l][None, :], conv_ln_g[l][None, :],
                                             conv_ln_b[l][None, :], f"conv_bwd_{l}")
        dcum = dcum_rows.reshape(H, T) + jnp.transpose(dcum_q, (0, 2, 1)).reshape(H, T)
        dcum = jnp.pad(dcum, ((0, R - H), (0, 0)))
        dfl_t, dwf_t, dbf = _fgate_bwd(dcum, fl, h1, f"fgate_bwd_{l}")
        dfl = jnp.transpose(dfl_t).astype(BF16)
        dh1 = _mm([(dqkv[0], W_in_t[l], 0), (dqkv[1], W_in_t[l], 1), (dqkv[2], W_in_t[l], 2),
                   (dfl, W_f_t[l]), (dcproj, W_c_t[l])], "nn", F32, f"dh1_{l}", tm=MM_ROWS, tn=FFN_COLS)
        dWq_t = [_mm([(dqkv[k], h1)], "tn", F32, f"dw_qkv{k}_{l}", tm=DW_TILE) for k in range(3)]
        dWc_t = _mm([(dcproj, h1)], "tn", F32, f"dw_conv_{l}", tm=DW_TILE)
        dW_in_t = jnp.concatenate(dWq_t + [dwf_t[:H], dWc_t], axis=0).astype(BF16)
        h_mix[l], token = reduce_start([dW_in_t.reshape(N_DEV, s_in, D), dW_o.reshape(N_DEV, D // N_DEV, D)], f"mix_{l}")
        dx, sm_pre1 = _prenorm_bwd(dh1, xin, dx_mid, gpre1, sc1, f"prenorm1_bwd_{l}", first=(token,))
        dmod = jnp.stack([sm_pre1[0], sm_pre1[1], sm_post1[0], sm_pre2[0], sm_pre2[1], sm_post2[0]])
        small[l] = (dmod, sm_pre1[2], sm_post1[1], sm_pre2[2], sm_post2[1], sm_conv[0], sm_conv[1], sm_conv[2],
                    dbf[:H, 0], dconv_w[:CONV_K])
    grad_x = dx[None]

    small_names = 10
    small_l = [jnp.stack([small[l][k] for l in range(L)]) for k in range(small_names)]
    small_shapes = [a.shape for a in small_l]
    packed = _pack(small_l, D)
    rows = packed.shape[1]
    rows_pad = -(-L * rows // 8) * 8
    packed2 = jnp.pad(packed.reshape(L * rows, D), ((0, rows_pad - L * rows), (0, 0)))
    h_small = _gather_start([packed2], "gather_small")

    got_ffn = [_scatter_wait(h_ffn[l], h_small["token"], f"scatter_wait_ffn_{l}") for l in range(L)]

    def step(w, m, v, parts, name):
        return list(_adamw_summed(w, parts, m, v, "adamw_" + name))

    swap = lambda t: jnp.transpose(t, (0, 2, 1))
    r_w_ffn_in = [swap(t) for t in step(w_ffn_in_t, swap(m_w_ffn_in), swap(v_w_ffn_in),
                                        [got_ffn[l][0] for l in range(L)], "w_ffn_in")]
    r_w_ffn_out = step(w_ffn_out, m_w_ffn_out, v_w_ffn_out, [got_ffn[l][1] for l in range(L)], "w_ffn_out")
    got_mix = [_scatter_wait(h_mix[l], r_w_ffn_out[1], f"scatter_wait_mix_{l}") for l in range(L)]
    r_w_in = [swap(t) for t in step(w_in_t, swap(m_w_in), swap(v_w_in), [got_mix[l][0] for l in range(L)], "w_in")]
    r_w_o = step(w_o, m_w_o, v_w_o, [got_mix[l][1] for l in range(L)], "w_o")

    small_g = _gather_wait(_gather_mid(h_small, r_w_o[1], "gather_small_mid"), r_w_o[1], "gather_small_wait")[0]
    small_sum = _sum_devices(small_g, "sum_small")[:L * rows].reshape(L, rows, D)
    (g_ada_b6, g_mix_pre, g_mix_post, g_ffn_pre, g_ffn_post, g_conv_b, g_ln_g, g_ln_b, g_b_f,
     g_conv_w_full) = _unpack(small_sum, small_shapes, D)
    g_ada_b = g_ada_b6.reshape(L, N_MOD * D)
    g_conv_w = lax.dynamic_slice_in_dim(g_conv_w_full, me * cs, cs, axis=2)
    dmod_all = small_g[:, :L * rows].reshape(N_DEV, L, rows, D)[:, :, :N_MOD].reshape(N_DEV, L, N_MOD * D)
    dmod_loc = jnp.transpose(lax.dynamic_slice_in_dim(dmod_all, me * s_ada, s_ada, axis=2), (1, 0, 2))
    g_ada_w = _ada_bwd(jnp.transpose(c_all), dmod_loc, "ada_bwd")
    r_ada_w = [t.reshape(ada_w.shape) for t in _adamw(
        ada_w.reshape(L * D, s_ada), g_ada_w.reshape(L * D, s_ada), m_ada_w.reshape(L * D, s_ada),
        v_ada_w.reshape(L * D, s_ada), "adamw_ada_w", False)]

    sw = [b_f, conv_w, conv_b, conv_ln_g, conv_ln_b, mix_pre_g, mix_post_g, ffn_pre_g, ffn_post_g, ada_b]
    sg = [g_b_f, g_conv_w, g_conv_b, g_ln_g, g_ln_b, g_mix_pre, g_mix_post, g_ffn_pre, g_ffn_post, g_ada_b]
    sm = [m_b_f, m_conv_w, m_conv_b, m_conv_ln_g, m_conv_ln_b, m_mix_pre_g, m_mix_post_g, m_ffn_pre_g, m_ffn_post_g, m_ada_b]
    sv = [v_b_f, v_conv_w, v_conv_b, v_conv_ln_g, v_conv_ln_b, v_mix_pre_g, v_mix_post_g, v_ffn_pre_g, v_ffn_post_g, v_ada_b]
    shapes = [a.shape for a in sw]

    def flat(arrs):
        p = _pack(arrs, D)
        n = p.shape[0] * p.shape[1]
        return jnp.pad(p.reshape(n, D), ((0, -(-n // 8) * 8 - n), (0, 0))), p.shape

    pw, pshape = flat(sw)
    pg, pm, pv = flat(sg)[0], flat(sm)[0], flat(sv)[0]
    s_outs = _adamw(pw, pg, pm, pv, "adamw_small", False)
    n_small = pshape[0] * pshape[1]
    s_g, s_d, s_m, s_v = (_unpack(t[:n_small].reshape(pshape), shapes, D) for t in s_outs)

    big = {"w_in": r_w_in, "w_o": r_w_o, "w_ffn_in": r_w_ffn_in, "w_ffn_out": r_w_ffn_out, "ada_w": r_ada_w}
    order = ["w_in", "b_f", "conv_w", "conv_b", "conv_ln_g", "conv_ln_b", "w_o", "w_ffn_in", "w_ffn_out",
             "mix_pre_g", "mix_post_g", "ffn_pre_g", "ffn_post_g", "ada_w", "ada_b"]
    small_pos = {n: i for i, n in enumerate(["b_f", "conv_w", "conv_b", "conv_ln_g", "conv_ln_b", "mix_pre_g",
                                             "mix_post_g", "ffn_pre_g", "ffn_post_g", "ada_b"])}

    def pick(n, k):
        if n in big:
            return big[n][k]
        return (s_g, s_d, s_m, s_v)[k][small_pos[n]]

    return (loss, grad_x, *[pick(n, 0) for n in order], *[pick(n, 1) for n in order],
            *[pick(n, 2) for n in order], *[pick(n, 3) for n in order])
```

```python
import functools
import math

import jax
import jax.numpy as jnp
from jax import lax
from jax.experimental import pallas as pl
from jax.experimental.pallas import tpu as pltpu

F32 = jnp.float32
BF16 = jnp.bfloat16
MESH = pl.DeviceIdType.MESH
N_DEV = 8
EPS = 1e-6
CONV_K = 31
CONV_PAD = 32
CONV_CHUNK = 128
N_MOD = 6
NEG = -1e30
LANES = 128
VMEM_LIMIT = 56 * 2**20
MM_TILE = 1024
MM_ROWS = 2048
MM_COLS = 512
MM_ROWS_SMALL = 512
FFN_COLS = 256
DW_TILE = 1408
ADAM_LR, ADAM_B1, ADAM_B2, ADAM_EPS, ADAM_WD, ADAM_STEP = 0.001, 0.9, 0.999, 1e-08, 0.01, 10

NN = (((1,), (0,)), ((), ()))
NT = (((1,), (1,)), ((), ()))
TN = (((0,), (0,)), ((), ()))


def _dot(a, b, dims):
    return lax.dot_general(a, b, dims, preferred_element_type=F32)


def _tile(n, pref, align=LANES):
    if n <= pref:
        return n
    t = (pref // align) * align
    while t >= align:
        if n % t == 0:
            return t
        t -= align
    return n


def _params(sem=None):
    return pltpu.CompilerParams(dimension_semantics=sem, vmem_limit_bytes=VMEM_LIMIT)


def _sigmoid(x):
    return 1.0 / (1.0 + jnp.exp(-x))


def _my_index():
    return 4 * lax.axis_index("x") + 2 * lax.axis_index("y") + lax.axis_index("c")


def _exchange(x, *, gather, name):
    blk = x.shape if gather else x.shape[1:]

    def body(x_ref, y_ref, send_sems, recv_sems, local_sem):
        mx, my, mc = lax.axis_index("x"), lax.axis_index("y"), lax.axis_index("c")
        me = 4 * mx + 2 * my + mc

        def src(p):
            return x_ref if gather else x_ref.at[p]

        mine = pltpu.make_async_copy(src(me), y_ref.at[me], local_sem)
        mine.start()
        copies = []
        for k in range(1, N_DEV):
            px = (1 - mx) if (k >> 2) & 1 else mx
            py = (1 - my) if (k >> 1) & 1 else my
            pc = (1 - mc) if k & 1 else mc
            cp = pltpu.make_async_remote_copy(
                src_ref=src(4 * px + 2 * py + pc), dst_ref=y_ref.at[me],
                send_sem=send_sems.at[k - 1], recv_sem=recv_sems.at[k - 1],
                device_id=(px, py, pc), device_id_type=MESH)
            cp.start()
            copies.append(cp)
        for cp in copies:
            cp.wait()
        mine.wait()

    return pl.pallas_call(
        body, name=name,
        out_shape=jax.ShapeDtypeStruct((N_DEV,) + tuple(blk), x.dtype),
        in_specs=[pl.BlockSpec(memory_space=pl.ANY)],
        out_specs=pl.BlockSpec(memory_space=pl.ANY),
        scratch_shapes=[pltpu.SemaphoreType.DMA((N_DEV - 1,)), pltpu.SemaphoreType.DMA((N_DEV - 1,)),
                        pltpu.SemaphoreType.DMA(())],
    )(x)


_HBM = pl.BlockSpec(memory_space=pl.ANY)


_SEM = pl.BlockSpec(memory_space=pltpu.SEMAPHORE)
_HBM_SPEC = pl.BlockSpec(memory_space=pltpu.HBM)
_EFFECT = pltpu.CompilerParams(has_side_effects=pltpu.SideEffectType.DATAFLOW_SIDE_EFFECTING)


def _in_hbm(a):
    return pltpu.with_memory_space_constraint(a, pltpu.HBM)


def _scatter_copies(x_refs, land_refs, send_sems, recv_sems, loc_sems):
    mx, my, mc = lax.axis_index("x"), lax.axis_index("y"), lax.axis_index("c")
    me = 4 * mx + 2 * my + mc
    local, remote = [], []
    for a, (x_ref, land_ref) in enumerate(zip(x_refs, land_refs)):
        local.append(pltpu.make_async_copy(x_ref.at[me], land_ref.at[me], loc_sems.at[a]))
        for k in range(1, N_DEV):
            px = (1 - mx) if (k >> 2) & 1 else mx
            py = (1 - my) if (k >> 1) & 1 else my
            pc = (1 - mc) if k & 1 else mc
            remote.append(pltpu.make_async_remote_copy(
                src_ref=x_ref.at[4 * px + 2 * py + pc], dst_ref=land_ref.at[me],
                send_sem=send_sems.at[(N_DEV - 1) * a + k - 1], recv_sem=recv_sems.at[(N_DEV - 1) * a + k - 1],
                device_id=(px, py, pc), device_id_type=MESH))
    return local, remote


def _scatter_start(xs, name):
    n = len(xs)

    def body(*refs):
        local, remote = _scatter_copies(refs[:n], refs[n:2 * n], *refs[2 * n:2 * n + 3])
        for cp in local + remote:
            cp.start()
        refs[-1][...] = jnp.zeros_like(refs[-1])

    lands = [lax.empty(x.shape, x.dtype) for x in xs]
    n_sems = (N_DEV - 1) * n
    outs = pl.pallas_call(
        body, name=name,
        out_shape=(pltpu.SemaphoreType.DMA((n_sems,)), pltpu.SemaphoreType.DMA((n_sems,)), pltpu.SemaphoreType.DMA((n,)),
                   *[pltpu.HBM(x.shape, x.dtype) for x in xs], *[pltpu.HBM(x.shape, x.dtype) for x in xs],
                   jax.ShapeDtypeStruct((8, LANES), F32)),
        in_specs=[_HBM_SPEC] * (2 * n),
        out_specs=(_SEM, _SEM, _SEM, *[_HBM_SPEC] * (2 * n), pl.BlockSpec(memory_space=pltpu.VMEM)),
        input_output_aliases={i: 3 + i for i in range(2 * n)},
        compiler_params=_EFFECT,
    )(*[_in_hbm(x) for x in xs], *[_in_hbm(t) for t in lands])
    return (outs[0], outs[1], outs[2], list(outs[3:3 + n]), list(outs[3 + n:3 + 2 * n])), outs[-1]


def _scatter_wait(handle, after, name):
    send_sems, recv_sems, loc_sems, x_thru, land_thru = handle
    n = len(x_thru)

    def body(*refs):
        local, remote = _scatter_copies(refs[:n], refs[n:2 * n], *refs[2 * n:2 * n + 3])
        for cp in local:
            cp.wait()
        for cp in remote:
            cp.wait_send()
            cp.wait_recv()

    outs = pl.pallas_call(
        body, name=name,
        out_shape=[pltpu.HBM(x.shape, x.dtype) for x in x_thru + land_thru],
        in_specs=[_HBM_SPEC] * (2 * n) + [_SEM, _SEM, _SEM, _HBM],
        out_specs=[_HBM_SPEC] * (2 * n),
        input_output_aliases={i: i for i in range(2 * n)},
        compiler_params=_EFFECT,
    )(*x_thru, *land_thru, send_sems, recv_sems, loc_sems, after)
    return list(outs[n:])


def _gather_first_copies(x_refs, y_refs, send_sems, sib_recv, ici_recv, loc_sems):
    mx, my, mc = lax.axis_index("x"), lax.axis_index("y"), lax.axis_index("c")
    me = 4 * mx + 2 * my + mc
    local, remote = [], []
    for a, (x_ref, y_ref) in enumerate(zip(x_refs, y_refs)):
        local.append(pltpu.make_async_copy(x_ref, y_ref.at[me], loc_sems.at[a]))
        remote.append(pltpu.make_async_remote_copy(
            src_ref=x_ref, dst_ref=y_ref.at[me], send_sem=send_sems.at[4 * a], recv_sem=sib_recv.at[a],
            device_id=(mx, my, 1 - mc), device_id_type=MESH))
        for j, (px, py) in enumerate([(1 - mx, my), (mx, 1 - my), (1 - mx, 1 - my)]):
            remote.append(pltpu.make_async_remote_copy(
                src_ref=x_ref, dst_ref=y_ref.at[me], send_sem=send_sems.at[4 * a + 1 + j], recv_sem=ici_recv.at[3 * a + j],
                device_id=(px, py, mc), device_id_type=MESH))
    return local, remote


def _gather_forward_copies(y_refs, ici_recv, fwd_send, fwd_recv):
    mx, my, mc = lax.axis_index("x"), lax.axis_index("y"), lax.axis_index("c")
    pairs = []
    for a, y_ref in enumerate(y_refs):
        for j, (px, py) in enumerate([(1 - mx, my), (mx, 1 - my), (1 - mx, 1 - my)]):
            slot = y_ref.at[4 * px + 2 * py + mc]
            arrival = pltpu.make_async_remote_copy(
                src_ref=slot, dst_ref=slot, send_sem=fwd_send.at[3 * a + j], recv_sem=ici_recv.at[3 * a + j],
                device_id=(px, py, mc), device_id_type=MESH)
            onward = pltpu.make_async_remote_copy(
                src_ref=slot, dst_ref=slot, send_sem=fwd_send.at[3 * a + j], recv_sem=fwd_recv.at[3 * a + j],
                device_id=(mx, my, 1 - mc), device_id_type=MESH)
            pairs.append((arrival, onward))
    return pairs


def _gather_start(xs, name, after=()):
    n = len(xs)
    ni = 2 * n + len(after)

    def body(*refs):
        local, remote = _gather_first_copies(refs[:n], refs[n:2 * n], *refs[ni:ni + 4])
        for cp in local + remote:
            cp.start()
        refs[-1][...] = jnp.zeros_like(refs[-1])

    ys = [lax.empty((N_DEV,) + tuple(x.shape), x.dtype) for x in xs]
    dma = pltpu.SemaphoreType.DMA
    outs = pl.pallas_call(
        body, name=name,
        out_shape=(dma((4 * n,)), dma((n,)), dma((3 * n,)), dma((n,)),
                   *[pltpu.HBM(x.shape, x.dtype) for x in xs], *[pltpu.HBM(y.shape, y.dtype) for y in ys],
                   jax.ShapeDtypeStruct((8, LANES), F32)),
        in_specs=[_HBM_SPEC] * (2 * n) + [_HBM] * len(after),
        out_specs=(_SEM, _SEM, _SEM, _SEM, *[_HBM_SPEC] * (2 * n), pl.BlockSpec(memory_space=pltpu.VMEM)),
        input_output_aliases={i: 4 + i for i in range(2 * n)},
        compiler_params=_EFFECT,
    )(*[_in_hbm(x) for x in xs], *[_in_hbm(y) for y in ys], *after)
    return dict(send=outs[0], sib_recv=outs[1], ici_recv=outs[2], loc=outs[3], x=list(outs[4:4 + n]),
                y=list(outs[4 + n:4 + 2 * n]), token=outs[-1])


def _gather_mid(h, after, name):
    n = len(h["y"])

    def body(*refs):
        for arrival, onward in _gather_forward_copies(refs[:n], refs[n], refs[n + 2 + n], refs[n + 3 + n]):
            arrival.wait_recv()
            onward.start()

    dma = pltpu.SemaphoreType.DMA
    outs = pl.pallas_call(
        body, name=name,
        out_shape=(*[pltpu.HBM(y.shape, y.dtype) for y in h["y"]], dma((3 * n,)), dma((3 * n,))),
        in_specs=[_HBM_SPEC] * n + [_SEM, _HBM],
        out_specs=(*[_HBM_SPEC] * n, _SEM, _SEM),
        input_output_aliases={i: i for i in range(n)},
        compiler_params=_EFFECT,
    )(*h["y"], h["ici_recv"], after)
    return dict(h, y=list(outs[:n]), fwd_send=outs[n], fwd_recv=outs[n + 1])


def _gather_wait(h, after, name):
    n = len(h["y"])

    def body(*refs):
        x_refs, y_refs = refs[:n], refs[n:2 * n]
        send, sib_recv, loc, fwd_send, fwd_recv = refs[2 * n:2 * n + 5]
        local, remote = _gather_first_copies(x_refs, y_refs, send, sib_recv, fwd_recv, loc)
        for cp in local:
            cp.wait()
        for k, cp in enumerate(remote):
            cp.wait_send()
            if k % 4 == 0:
                cp.wait_recv()
        for _, onward in _gather_forward_copies(y_refs, fwd_recv, fwd_send, fwd_recv):
            onward.wait_send()
            onward.wait_recv()

    outs = pl.pallas_call(
        body, name=name,
        out_shape=[pltpu.HBM(t.shape, t.dtype) for t in h["x"] + h["y"]],
        in_specs=[_HBM_SPEC] * (2 * n) + [_SEM] * 5 + [_HBM],
        out_specs=[_HBM_SPEC] * (2 * n),
        input_output_aliases={i: i for i in range(2 * n)},
        compiler_params=_EFFECT,
    )(*h["x"], *h["y"], h["send"], h["sib_recv"], h["loc"], h["fwd_send"], h["fwd_recv"], after)
    return list(outs[n:])


def _mm(pairs, mode, out_dtype, name, tm=MM_TILE, tn=MM_TILE, n_cols=None):
    dims = {"nn": NN, "nt": NT, "tn": TN}[mode]
    a0, b0 = pairs[0][0], pairs[0][1]
    M = a0.shape[1] if mode == "tn" else a0.shape[0]
    N = n_cols or (b0.shape[0] if mode == "nt" else b0.shape[1])
    tm, tn = _tile(M, tm), _tile(N, tn)
    in_specs, args = [], []
    for pr in pairs:
        a, b = pr[0], pr[1]
        if mode == "tn":
            K = a.shape[0]
            in_specs.append(pl.BlockSpec((K, tm), lambda i, j: (0, i)))
            in_specs.append(pl.BlockSpec((K, tn), lambda i, j: (0, j)))
        elif mode == "nn":
            K = a.shape[1]
            rb = pr[2] if len(pr) > 2 else 0
            in_specs.append(pl.BlockSpec((tm, K), lambda i, j: (i, 0)))
            in_specs.append(pl.BlockSpec((K, tn), functools.partial(lambda i, j, rb: (rb, j), rb=rb)))
        else:
            K = a.shape[1]
            cb = pr[2] if len(pr) > 2 else 0
            in_specs.append(pl.BlockSpec((tm, K), lambda i, j: (i, 0)))
            in_specs.append(pl.BlockSpec((tn, K), functools.partial(lambda i, j, cb: (j, cb), cb=cb)))
        args += [a, b]
    n_pairs = len(pairs)

    def body(*refs):
        o_ref = refs[-1]
        acc = None
        for k in range(n_pairs):
            d = _dot(refs[2 * k][...], refs[2 * k + 1][...], dims)
            acc = d if acc is None else acc + d
        o_ref[...] = acc.astype(o_ref.dtype)

    return pl.pallas_call(
        body, name=name, grid=(M // tm, N // tn), in_specs=in_specs,
        out_specs=pl.BlockSpec((tm, tn), lambda i, j: (i, j)),
        out_shape=jax.ShapeDtypeStruct((M, N), out_dtype),
        compiler_params=_params(("parallel", "arbitrary")),
    )(*args)


def _dw_stack(a3, b, name):
    n, T, A = a3.shape
    D = b.shape[1]

    def body(a_ref, b_ref, o_ref):
        o_ref[...] = _dot(a_ref[...], b_ref[...], TN)

    return pl.pallas_call(
        body, name=name, grid=(n,),
        in_specs=[pl.BlockSpec((None, T, A), lambda k: (k, 0, 0)), pl.BlockSpec((T, D), lambda k: (0, 0))],
        out_specs=pl.BlockSpec((A, D), lambda k: (k, 0)),
        out_shape=jax.ShapeDtypeStruct((n * A, D), F32),
        compiler_params=_params(("parallel",)),
    )(a3, b)


def _prenorm(x, g, sc, sh, name, first=()):
    T, D = x.shape
    tm = _tile(T, 512, 8)

    def body(x_ref, g_ref, sc_ref, sh_ref, *rest):
        h_ref = rest[-1]
        xv = x_ref[...]
        r = lax.rsqrt(jnp.mean(xv * xv, axis=-1, keepdims=True) + EPS)
        h_ref[...] = (((xv * r) * g_ref[...]) * (1.0 + sc_ref[...]) + sh_ref[...]).astype(BF16)

    row = pl.BlockSpec((1, D), lambda i: (0, 0))
    return pl.pallas_call(
        body, name=name, grid=(T // tm,),
        in_specs=[pl.BlockSpec((tm, D), lambda i: (i, 0)), row, row, row] + [_HBM] * len(first),
        out_specs=pl.BlockSpec((tm, D), lambda i: (i, 0)),
        out_shape=jax.ShapeDtypeStruct((T, D), BF16),
        compiler_params=_params(("parallel",)),
    )(x, g, sc, sh, *first)


def _prenorm_bwd(dh, x, dres, g, sc, name, first=()):
    T, D = x.shape
    tm = _tile(T, 256, 8)

    def body(dh_ref, x_ref, dres_ref, g_ref, sc_ref, *rest):
        dx_ref, sm_ref = rest[-2:]
        @pl.when(pl.program_id(0) == 0)
        def _():
            sm_ref[...] = jnp.zeros_like(sm_ref)

        xv, dhv = x_ref[...], dh_ref[...]
        r = lax.rsqrt(jnp.mean(xv * xv, axis=-1, keepdims=True) + EPS)
        xh = xv * r
        one_sc = 1.0 + sc_ref[...]
        sm_ref[0:1, :] += jnp.sum(dhv, axis=0, keepdims=True)
        sm_ref[1:2, :] += jnp.sum(dhv * (xh * g_ref[...]), axis=0, keepdims=True)
        sm_ref[2:3, :] += jnp.sum(dhv * one_sc * xh, axis=0, keepdims=True)
        dxh = dhv * one_sc * g_ref[...]
        dx_ref[...] = dres_ref[...] + r * (dxh - xh * jnp.mean(dxh * xh, axis=-1, keepdims=True))

    row = pl.BlockSpec((1, D), lambda i: (0, 0))
    big = pl.BlockSpec((tm, D), lambda i: (i, 0))
    return pl.pallas_call(
        body, name=name, grid=(T // tm,),
        in_specs=[big, big, big, row, row] + [_HBM] * len(first),
        out_specs=[big, pl.BlockSpec((8, D), lambda i: (0, 0))],
        out_shape=(jax.ShapeDtypeStruct((T, D), F32), jax.ShapeDtypeStruct((8, D), F32)),
        compiler_params=_params(("arbitrary",)),
    )(dh, x, dres, g, sc, *first)


def _mm_postnorm(a, w, x, gate, gpost, name, nxt=None):
    T, K = a.shape
    D = w.shape[1]
    tm = _tile(T, MM_ROWS_SMALL, 8)

    def body(a_ref, w_ref, x_ref, gate_ref, gp_ref, *rest):
        y = _dot(a_ref[...], w_ref[...], NN)
        r = lax.rsqrt(jnp.mean(y * y, axis=-1, keepdims=True) + EPS)
        xn = x_ref[...] + gate_ref[...] * ((y * r) * gp_ref[...])
        if nxt is None:
            y_ref, xn_ref = rest
        else:
            g_ref, sc_ref, sh_ref, y_ref, xn_ref, h_ref = rest
            rn = lax.rsqrt(jnp.mean(xn * xn, axis=-1, keepdims=True) + EPS)
            h_ref[...] = (((xn * rn) * g_ref[...]) * (1.0 + sc_ref[...]) + sh_ref[...]).astype(BF16)
        y_ref[...] = y
        xn_ref[...] = xn

    row = pl.BlockSpec((1, D), lambda i: (0, 0))
    big = pl.BlockSpec((tm, D), lambda i: (i, 0))
    extra = () if nxt is None else tuple(nxt)
    return pl.pallas_call(
        body, name=name, grid=(T // tm,),
        in_specs=[pl.BlockSpec((tm, K), lambda i: (i, 0)), pl.BlockSpec((K, D), lambda i: (0, 0)), big, row, row]
        + [row] * len(extra),
        out_specs=[big, big] + [big] * (nxt is not None),
        out_shape=(jax.ShapeDtypeStruct((T, D), F32), jax.ShapeDtypeStruct((T, D), F32))
        + ((jax.ShapeDtypeStruct((T, D), BF16),) if nxt is not None else ()),
        compiler_params=_params(("parallel",)),
    )(a, w, x, gate, gpost, *extra)


def _postnorm_bwd(dx, y, gate, gpost, name):
    T, D = y.shape
    tm = _tile(T, 256, 8)

    def body(dx_ref, y_ref, gate_ref, gp_ref, dy_ref, sm_ref):
        @pl.when(pl.program_id(0) == 0)
        def _():
            sm_ref[...] = jnp.zeros_like(sm_ref)

        yv, dxv = y_ref[...], dx_ref[...]
        r = lax.rsqrt(jnp.mean(yv * yv, axis=-1, keepdims=True) + EPS)
        yh = yv * r
        dn = dxv * gate_ref[...]
        sm_ref[0:1, :] += jnp.sum(dxv * (yh * gp_ref[...]), axis=0, keepdims=True)
        sm_ref[1:2, :] += jnp.sum(dn * yh, axis=0, keepdims=True)
        dyh = dn * gp_ref[...]
        dy_ref[...] = (r * (dyh - yh * jnp.mean(dyh * yh, axis=-1, keepdims=True))).astype(BF16)

    row = pl.BlockSpec((1, D), lambda i: (0, 0))
    big = pl.BlockSpec((tm, D), lambda i: (i, 0))
    return pl.pallas_call(
        body, name=name, grid=(T // tm,),
        in_specs=[big, big, row, row],
        out_specs=[big, pl.BlockSpec((8, D), lambda i: (0, 0))],
        out_shape=(jax.ShapeDtypeStruct((T, D), BF16), jax.ShapeDtypeStruct((8, D), F32)),
        compiler_params=_params(("arbitrary",)),
    )(dx, y, gate, gpost)


def _ffn_in_fwd(h, w_t, name):
    T, D = h.shape
    F = w_t.shape[0] // 2
    tm, tn = _tile(T, MM_ROWS), _tile(F, FFN_COLS)
    nj = F // tn

    def body(h_ref, wg_ref, wu_ref, g_ref, u_ref, act_ref):
        hv = h_ref[...]
        g = _dot(hv, wg_ref[...], NT)
        u = _dot(hv, wu_ref[...], NT)
        g_ref[...] = g.astype(BF16)
        u_ref[...] = u.astype(BF16)
        act_ref[...] = ((g * _sigmoid(g)) * u).astype(BF16)

    out = pl.BlockSpec((tm, tn), lambda j, i: (i, j))
    return pl.pallas_call(
        body, name=name, grid=(nj, T // tm),
        in_specs=[pl.BlockSpec((tm, D), lambda j, i: (i, 0)),
                  pl.BlockSpec((tn, D), lambda j, i: (j, 0)),
                  pl.BlockSpec((tn, D), lambda j, i: (j + nj, 0))],
        out_specs=[out, out, out],
        out_shape=tuple(jax.ShapeDtypeStruct((T, F), BF16) for _ in range(3)),
        compiler_params=_params(("parallel", "arbitrary")),
    )(h, w_t, w_t)


def _ffn_act_bwd(dy, w_out, g, u, name):
    T, D = dy.shape
    F = w_out.shape[0]
    tm, tn = _tile(T, MM_ROWS), _tile(F, FFN_COLS)

    def body(dy_ref, w_ref, g_ref, u_ref, dg_ref, du_ref):
        dact = _dot(dy_ref[...], w_ref[...], NT)
        gv, uv = g_ref[...].astype(F32), u_ref[...].astype(F32)
        sg = _sigmoid(gv)
        dg_ref[...] = (dact * uv * (sg * (1.0 + gv * (1.0 - sg)))).astype(BF16)
        du_ref[...] = (dact * (gv * sg)).astype(BF16)

    tile = pl.BlockSpec((tm, tn), lambda j, i: (i, j))
    return pl.pallas_call(
        body, name=name, grid=(F // tn, T // tm),
        in_specs=[pl.BlockSpec((tm, D), lambda j, i: (i, 0)), pl.BlockSpec((tn, D), lambda j, i: (j, 0)), tile, tile],
        out_specs=[tile, tile],
        out_shape=(jax.ShapeDtypeStruct((T, F), BF16), jax.ShapeDtypeStruct((T, F), BF16)),
        compiler_params=_params(("parallel", "arbitrary")),
    )(dy, w_out, g, u)


def _lane_scan(v, reverse):
    T = v.shape[-1]
    lane = lax.broadcasted_iota(jnp.int32, v.shape, 1)
    d = 1
    while d < T:
        if reverse:
            v = v + jnp.where(lane < T - d, pltpu.roll(v, T - d, axis=1), 0.0)
        else:
            v = v + jnp.where(lane >= d, pltpu.roll(v, d, axis=1), 0.0)
        d *= 2
    return v


def _fgate_fwd(h, wf_t, bf, name):
    T, D = h.shape
    R = wf_t.shape[0]

    def body(h_ref, w_ref, b_ref, fl_ref, cum_ref):
        fl = _dot(w_ref[...], h_ref[...], NT) + b_ref[...]
        fl_ref[...] = fl
        logf = jnp.minimum(fl, 0.0) - jnp.log(1.0 + jnp.exp(-jnp.abs(fl)))
        cum_ref[...] = _lane_scan(logf, reverse=False)

    return pl.pallas_call(
        body, name=name,
        out_shape=(jax.ShapeDtypeStruct((R, T), F32), jax.ShapeDtypeStruct((R, T), F32)),
        compiler_params=_params(),
    )(h, wf_t, bf)


def _fgate_bwd(dcum, fl, h, name):
    R, T = fl.shape
    D = h.shape[1]

    def body(dc_ref, fl_ref, h_ref, dfl_ref, dw_ref, db_ref):
        dlogf = _lane_scan(dc_ref[...], reverse=True)
        dfl = dlogf * _sigmoid(-fl_ref[...])
        dfl_ref[...] = dfl
        dw_ref[...] = _dot(dfl.astype(BF16), h_ref[...], NN)
        db_ref[...] = jnp.broadcast_to(jnp.sum(dfl, axis=-1, keepdims=True), (R, LANES))

    return pl.pallas_call(
        body, name=name,
        out_shape=(jax.ShapeDtypeStruct((R, T), F32), jax.ShapeDtypeStruct((R, D), F32),
                   jax.ShapeDtypeStruct((R, LANES), F32)),
        compiler_params=_params(),
    )(dcum, fl, h)


def _head_masks(hpb, dh, rows):
    lane = lax.broadcasted_iota(jnp.int32, (rows, LANES), 1)
    return [(lane >= h * dh) & (lane < (h + 1) * dh) for h in range(hpb)]


def _stack_heads(v, masks):
    return jnp.concatenate([jnp.where(mk, v, jnp.zeros_like(v)) for mk in masks], axis=0)


def _heads_to_lanes(col, masks, tq):
    out = jnp.broadcast_to(col[0:tq], (tq, LANES))
    for h in range(1, len(masks)):
        out = jnp.where(masks[h], col[h * tq:(h + 1) * tq], out)
    return out


def _causal_stack(hpb, tq):
    r = lax.broadcasted_iota(jnp.int32, (tq, tq), 0)
    c = lax.broadcasted_iota(jnp.int32, (tq, tq), 1)
    return jnp.concatenate([c] * hpb, axis=0) <= jnp.concatenate([r] * hpb, axis=0)


def _attn_fwd(qkv, cum_rows, n_heads, name, tq):
    T = qkv.shape[0]
    A = qkv.shape[1] // 3
    dh = A // n_heads
    hpb = LANES // dh
    nb = A // LANES
    nq = T // tq
    scale = dh ** -0.5

    def body(q_ref, k_ref, v_ref, c_ref, o_ref, l_ref, vbd):
        hp, i = pl.program_id(0), pl.program_id(1)
        masks = _head_masks(hpb, dh, tq)

        @pl.when(i == 0)
        def _():
            def fill(j, _):
                vbd[j] = _stack_heads(v_ref[pl.ds(pl.multiple_of(j * tq, tq), tq), :], masks)
                return 0

            lax.fori_loop(0, nq, fill, 0)

        qs = _stack_heads(q_ref[...], masks)
        crow0 = hp * hpb * nq

        def tile(j, carry, diag):
            m, l, acc = carry
            kt = k_ref[pl.ds(pl.multiple_of(j * tq, tq), tq), :]
            bias = jnp.concatenate(
                [jnp.broadcast_to(c_ref[pl.ds(crow0 + h * nq + j, 1), :], (tq, tq)) for h in range(hpb)], axis=0)
            s = _dot(qs, kt, NT) * scale - bias
            if diag:
                s = jnp.where(_causal_stack(hpb, tq), s, NEG)
            m_new = jnp.maximum(m, jnp.max(s, axis=-1, keepdims=True))
            p = jnp.exp(s - m_new)
            alpha = jnp.exp(m - m_new)
            l = alpha * l + jnp.sum(p, axis=-1, keepdims=True)
            pcat = jnp.concatenate([p[h * tq:(h + 1) * tq] for h in range(hpb)], axis=1).astype(BF16)
            acc = _heads_to_lanes(alpha, masks, tq) * acc + _dot(pcat, vbd[j], NN)
            return m_new, l, acc

        init = (jnp.full((hpb * tq, 1), NEG, F32), jnp.zeros((hpb * tq, 1), F32), jnp.zeros((tq, LANES), F32))
        carry = lax.fori_loop(0, i, lambda j, c: tile(j, c, False), init)
        m, l, acc = tile(i, carry, True)
        o_ref[...] = (acc / _heads_to_lanes(l, masks, tq)).astype(BF16)
        lse = m + jnp.log(l)
        for h in range(hpb):
            l_ref[:, h:h + 1] = lse[h * tq:(h + 1) * tq]

    return pl.pallas_call(
        body, name=name, grid=(nb, nq),
        in_specs=[pl.BlockSpec((tq, LANES), lambda h, i: (i, h)),
                  pl.BlockSpec((T, LANES), lambda h, i: (0, nb + h)),
                  pl.BlockSpec((T, LANES), lambda h, i: (0, 2 * nb + h)),
                  pl.BlockSpec(cum_rows.shape, lambda h, i: (0, 0))],
        out_specs=[pl.BlockSpec((tq, LANES), lambda h, i: (i, h)),
                   pl.BlockSpec((None, tq, hpb), lambda h, i: (h, i, 0))],
        out_shape=(jax.ShapeDtypeStruct((T, A), BF16), jax.ShapeDtypeStruct((nb, T, hpb), F32)),
        scratch_shapes=[pltpu.VMEM((nq, hpb * tq, LANES), BF16)],
        compiler_params=_params(("arbitrary", "arbitrary")),
    )(qkv, qkv, qkv, cum_rows)


def _attn_bwd(qkv, dcat, o, lse, cum_rows, n_heads, name, tq):
    T = qkv.shape[0]
    A = qkv.shape[1] // 3
    dh = A // n_heads
    hpb = LANES // dh
    nb = A // LANES
    nq = T // tq
    scale = dh ** -0.5

    def body(q_ref, k_ref, v_ref, do_ref, o_ref, l_ref, c_ref, dqkv_ref, dc_ref, dr_ref,
             dq_acc, delta, drow, qs_scr, dos_scr, kbd_scr):
        hp = pl.program_id(0)
        masks = _head_masks(hpb, dh, tq)
        crow0 = hp * hpb * nq

        def prologue(i, _):
            rs = pl.ds(pl.multiple_of(i * tq, tq), tq)
            do = do_ref[rs, :]
            prod = do * o_ref[rs, :].astype(F32)
            for h in range(hpb):
                delta[rs, h:h + 1] = jnp.sum(jnp.where(masks[h], prod, 0.0), axis=-1, keepdims=True)
            qs_scr[i] = _stack_heads(q_ref[rs, :], masks)
            dos_scr[i] = _stack_heads(do, masks).astype(BF16)
            kbd_scr[i] = _stack_heads(k_ref[rs, :], masks)
            dq_acc[rs, :] = jnp.zeros((tq, LANES), F32)
            drow[rs, :] = jnp.zeros((tq, hpb), F32)
            return 0

        lax.fori_loop(0, nq, prologue, 0)

        def kv_step(j, _):
            ks = pl.ds(pl.multiple_of(j * tq, tq), tq)
            kt, vt = k_ref[ks, :], v_ref[ks, :]
            kbd = kbd_scr[j]
            bias = jnp.concatenate(
                [jnp.broadcast_to(c_ref[pl.ds(crow0 + h * nq + j, 1), :], (tq, tq)) for h in range(hpb)], axis=0)

            def q_step(i, carry, diag):
                dk, dv, dcs = carry
                rs = pl.ds(pl.multiple_of(i * tq, tq), tq)
                qs, dos = qs_scr[i], dos_scr[i]
                s = _dot(qs, kt, NT) * scale - bias
                if diag:
                    s = jnp.where(_causal_stack(hpb, tq), s, NEG)
                lse = jnp.concatenate([l_ref[rs, h:h + 1] for h in range(hpb)], axis=0)
                p = jnp.exp(s - lse)
                dv = dv + _dot(p.astype(BF16), dos, TN)
                dp = _dot(dos, vt, NT)
                ds = p * (dp - jnp.concatenate([delta[rs, h:h + 1] for h in range(hpb)], axis=0))
                dcs = tuple(dcs[h] - jnp.sum(ds[h * tq:(h + 1) * tq], axis=0, keepdims=True) for h in range(hpb))
                rsum = jnp.sum(ds, axis=-1, keepdims=True)
                for h in range(hpb):
                    drow[rs, h:h + 1] += rsum[h * tq:(h + 1) * tq]
                dsb = (ds * scale).astype(BF16)
                dk = dk + _dot(dsb, qs, TN)
                dscat = jnp.concatenate([dsb[h * tq:(h + 1) * tq] for h in range(hpb)], axis=1)
                dq_acc[rs, :] += _dot(dscat, kbd, NN)
                return dk, dv, dcs

            init = (jnp.zeros((tq, LANES), F32), jnp.zeros((tq, LANES), F32),
                    tuple(jnp.zeros((1, tq), F32) for _ in range(hpb)))
            carry = q_step(j, init, True)
            dk, dv, dcs = lax.fori_loop(j + 1, nq, lambda i, c: q_step(i, c, False), carry)
            dqkv_ref[1, ks, :] = dk.astype(BF16)
            dqkv_ref[2, ks, :] = dv.astype(BF16)
            for h in range(hpb):
                dc_ref[pl.ds(crow0 + h * nq + j, 1), :] = dcs[h]
            return 0

        lax.fori_loop(0, nq, kv_step, 0)
        dqkv_ref[0] = dq_acc[...].astype(BF16)
        dr_ref[...] = drow[...]

    col = lambda off: pl.BlockSpec((T, LANES), functools.partial(lambda h, off: (0, off + h), off=off))
    return pl.pallas_call(
        body, name=name, grid=(nb,),
        in_specs=[col(0), col(nb), col(2 * nb), col(0), col(0),
                  pl.BlockSpec((None, T, hpb), lambda h: (h, 0, 0)),
                  pl.BlockSpec(cum_rows.shape, lambda h: (0, 0))],
        out_specs=[pl.BlockSpec((3, T, LANES), lambda h: (0, 0, h)),
                   pl.BlockSpec(cum_rows.shape, lambda h: (0, 0)),
                   pl.BlockSpec((None, T, hpb), lambda h: (h, 0, 0))],
        out_shape=(jax.ShapeDtypeStruct((3, T, A), BF16), jax.ShapeDtypeStruct(cum_rows.shape, F32),
                   jax.ShapeDtypeStruct((nb, T, hpb), F32)),
        scratch_shapes=[pltpu.VMEM((T, LANES), F32), pltpu.VMEM((T, hpb), F32), pltpu.VMEM((T, hpb), F32),
                        pltpu.VMEM((nq, hpb * tq, LANES), BF16), pltpu.VMEM((nq, hpb * tq, LANES), BF16),
                        pltpu.VMEM((nq, hpb * tq, LANES), BF16)],
        compiler_params=_params(("arbitrary",)),
    )(qkv, qkv, qkv, dcat, o, lse, cum_rows)


def _glu_into(upad, cv_ref, cg_ref, T):
    upad[0:CONV_PAD, :] = jnp.zeros((CONV_PAD, upad.shape[1]), F32)

    def fill(c, _):
        rs = pl.ds(pl.multiple_of(c * CONV_CHUNK, CONV_CHUNK), CONV_CHUNK)
        upad[pl.ds(pl.multiple_of(CONV_PAD + c * CONV_CHUNK, 8), CONV_CHUNK), :] = cv_ref[rs, :] * _sigmoid(cg_ref[rs, :])
        return 0

    lax.fori_loop(0, T // CONV_CHUNK, fill, 0)


SUBLANES = 8
CONV_SHIFT_ROWS = CONV_CHUNK + CONV_PAD - SUBLANES


def _load_window(win, sh, src, r0):
    win[...] = src[pl.ds(r0, CONV_CHUNK + CONV_PAD), :]
    for b in range(1, SUBLANES):
        sh[b - 1] = win[b:b + CONV_SHIFT_ROWS, :]


def _tap(win, sh, o):
    b = o % SUBLANES
    if b == 0:
        return win[o:o + CONV_CHUNK, :]
    return sh[b - 1, o - b:o - b + CONV_CHUNK, :]


def _conv_taps(win, sh, w_ref, first, step):
    acc = None
    for k in range(CONV_K):
        t = w_ref[k:k + 1, :] * _tap(win, sh, first + step * k)
        acc = t if acc is None else acc + t
    return acc


def _conv_scratch(C):
    return [pltpu.VMEM((CONV_CHUNK + CONV_PAD, C), F32), pltpu.VMEM((SUBLANES - 1, CONV_SHIFT_ROWS, C), F32)]


def _conv_fwd(cproj, w, b, lg, lb, name):
    T = cproj.shape[0]
    C = cproj.shape[1] // 2
    off = CONV_PAD - (CONV_K - 1)

    def body(cv_ref, cg_ref, w_ref, b_ref, lg_ref, lb_ref, out_ref, u1_ref, upad, win, sh):
        _glu_into(upad, cv_ref, cg_ref, T)

        def chunk(c, _):
            r0 = pl.multiple_of(c * CONV_CHUNK, CONV_CHUNK)
            _load_window(win, sh, upad, r0)
            u1 = _conv_taps(win, sh, w_ref, off, 1) + b_ref[...]
            u1_ref[pl.ds(r0, CONV_CHUNK), :] = u1
            mu = jnp.mean(u1, axis=-1, keepdims=True)
            var = jnp.mean(jnp.square(u1 - mu), axis=-1, keepdims=True)
            u2 = ((u1 - mu) * lax.rsqrt(var + EPS)) * lg_ref[...] + lb_ref[...]
            out_ref[pl.ds(r0, CONV_CHUNK), :] = (u2 * _sigmoid(u2)).astype(BF16)
            return 0

        lax.fori_loop(0, T // CONV_CHUNK, chunk, 0)

    row = pl.BlockSpec((1, C), lambda i: (0, 0))
    return pl.pallas_call(
        body, name=name, grid=(1,),
        in_specs=[pl.BlockSpec((T, C), lambda i: (0, 0)), pl.BlockSpec((T, C), lambda i: (0, 1)),
                  pl.BlockSpec(w.shape, lambda i: (0, 0)), row, row, row],
        out_specs=[pl.BlockSpec((T, C), lambda i: (0, 0)), pl.BlockSpec((T, C), lambda i: (0, 0))],
        out_shape=(jax.ShapeDtypeStruct((T, C), BF16), jax.ShapeDtypeStruct((T, C), F32)),
        scratch_shapes=[pltpu.VMEM((T + CONV_PAD, C), F32)] + _conv_scratch(C),
        compiler_params=_params(("arbitrary",)),
    )(cproj, cproj, w, b, lg, lb)


def _conv_bwd(cproj, u1_saved, dcat, w, lg, lb, name):
    T = cproj.shape[0]
    C = cproj.shape[1] // 2
    off = CONV_PAD - (CONV_K - 1)
    n_chunks = T // CONV_CHUNK

    def fold(v):
        return jnp.sum(v.reshape(CONV_CHUNK // 8, 8, C), axis=0)

    def body(cv_ref, cg_ref, u1_ref, du_ref, w_ref, lg_ref, lb_ref, dc_ref, dw_ref, sm_ref,
             upad, dpad, dwacc, smacc, win, sh):
        _glu_into(upad, cv_ref, cg_ref, T)
        dpad[pl.ds(T, CONV_PAD), :] = jnp.zeros((CONV_PAD, C), F32)
        dwacc[...] = jnp.zeros_like(dwacc)
        smacc[...] = jnp.zeros_like(smacc)

        def chunk_a(c, _):
            r0 = pl.multiple_of(c * CONV_CHUNK, CONV_CHUNK)
            _load_window(win, sh, upad, r0)
            u1 = u1_ref[pl.ds(r0, CONV_CHUNK), :]
            mu = jnp.mean(u1, axis=-1, keepdims=True)
            var = jnp.mean(jnp.square(u1 - mu), axis=-1, keepdims=True)
            rstd = lax.rsqrt(var + EPS)
            u1h = (u1 - mu) * rstd
            u2 = u1h * lg_ref[...] + lb_ref[...]
            sg = _sigmoid(u2)
            du2 = du_ref[pl.ds(r0, CONV_CHUNK), :] * (sg * (1.0 + u2 * (1.0 - sg)))
            smacc[8:16, :] += fold(du2 * u1h)
            smacc[16:24, :] += fold(du2)
            du1h = du2 * lg_ref[...]
            du1 = rstd * (du1h - jnp.mean(du1h, axis=-1, keepdims=True)
                          - u1h * jnp.mean(du1h * u1h, axis=-1, keepdims=True))
            smacc[0:8, :] += fold(du1)
            dpad[pl.ds(r0, CONV_CHUNK), :] = du1
            for k in range(CONV_K):
                dwacc[8 * k:8 * k + 8, :] += fold(du1 * _tap(win, sh, off + k))
            return 0

        lax.fori_loop(0, n_chunks, chunk_a, 0)

        def chunk_b(c, _):
            r0 = pl.multiple_of(c * CONV_CHUNK, CONV_CHUNK)
            rs = pl.ds(r0, CONV_CHUNK)
            _load_window(win, sh, dpad, r0)
            du0 = _conv_taps(win, sh, w_ref, CONV_K - 1, -1)
            cv, sg = cv_ref[rs, :], _sigmoid(cg_ref[rs, :])
            dc_ref[rs, 0:C] = (du0 * sg).astype(BF16)
            dc_ref[rs, C:2 * C] = (du0 * cv * (sg * (1.0 - sg))).astype(BF16)
            return 0

        lax.fori_loop(0, n_chunks, chunk_b, 0)
        for k in range(CONV_K):
            dw_ref[k:k + 1, :] = jnp.sum(dwacc[8 * k:8 * k + 8, :], axis=0, keepdims=True)
        dw_ref[CONV_K:CONV_PAD, :] = jnp.zeros((CONV_PAD - CONV_K, C), F32)
        for r in range(3):
            sm_ref[r:r + 1, :] = jnp.sum(smacc[8 * r:8 * r + 8, :], axis=0, keepdims=True)
        sm_ref[3:8, :] = jnp.zeros((5, C), F32)

    row = pl.BlockSpec((1, C), lambda i: (0, 0))
    return pl.pallas_call(
        body, name=name, grid=(1,),
        in_specs=[pl.BlockSpec((T, C), lambda i: (0, 0)), pl.BlockSpec((T, C), lambda i: (0, 1)),
                  pl.BlockSpec((T, C), lambda i: (0, 0)), pl.BlockSpec((T, C), lambda i: (0, 1)),
                  pl.BlockSpec(w.shape, lambda i: (0, 0)), row, row],
        out_specs=[pl.BlockSpec((T, 2 * C), lambda i: (0, 0)), pl.BlockSpec((CONV_PAD, C), lambda i: (0, 0)),
                   pl.BlockSpec((8, C), lambda i: (0, 0))],
        out_shape=(jax.ShapeDtypeStruct((T, 2 * C), BF16), jax.ShapeDtypeStruct((CONV_PAD, C), F32),
                   jax.ShapeDtypeStruct((8, C), F32)),
        scratch_shapes=[pltpu.VMEM((T + CONV_PAD, C), F32), pltpu.VMEM((T + CONV_PAD, C), F32),
                        pltpu.VMEM((8 * CONV_PAD, C), F32), pltpu.VMEM((24, C), F32)] + _conv_scratch(C),
        compiler_params=_params(("arbitrary",)),
    )(cproj, cproj, u1_saved, dcat, w, lg, lb)


def _loss_head(x, target, name):
    T, D = x.shape
    tm = _tile(T, 512, 8)

    def body(x_ref, t_ref, loss_ref, dx_ref):
        @pl.when(pl.program_id(0) == 0)
        def _():
            loss_ref[...] = jnp.zeros_like(loss_ref)

        err = x_ref[...] - t_ref[...]
        part = jnp.sum(jnp.mean(err * err, axis=-1, keepdims=True), axis=0, keepdims=True)
        loss_ref[...] += jnp.broadcast_to(0.5 * part, loss_ref.shape)
        dx_ref[...] = err * (1.0 / D)

    big = pl.BlockSpec((tm, D), lambda i: (i, 0))
    return pl.pallas_call(
        body, name=name, grid=(T // tm,),
        in_specs=[big, big],
        out_specs=[pl.BlockSpec((8, LANES), lambda i: (0, 0)), big],
        out_shape=(jax.ShapeDtypeStruct((8, LANES), F32), jax.ShapeDtypeStruct((T, D), F32)),
        compiler_params=_params(("arbitrary",)),
    )(x, target)


def _ada_fwd(c_all, ada_w, ada_b_loc, name):
    L, D, S = ada_w.shape
    B = c_all.shape[0]

    def body(c_ref, w_ref, b_ref, o_ref):
        c = c_ref[...]
        ca = (c * _sigmoid(c)).astype(BF16)
        o_ref[...] = _dot(ca, w_ref[...].astype(BF16), NN) + b_ref[...]

    return pl.pallas_call(
        body, name=name, grid=(L,),
        in_specs=[pl.BlockSpec((B, D), lambda l: (0, 0)), pl.BlockSpec((None, D, S), lambda l: (l, 0, 0)),
                  pl.BlockSpec((None, 1, S), lambda l: (l, 0, 0))],
        out_specs=pl.BlockSpec((None, B, S), lambda l: (l, 0, 0)),
        out_shape=jax.ShapeDtypeStruct((L, B, S), F32),
        compiler_params=_params(("parallel",)),
    )(c_all, ada_w, ada_b_loc)


def _ada_bwd(c_all_t, dmod_loc, name):
    D, B = c_all_t.shape
    L, _, S = dmod_loc.shape

    def body(c_ref, dm_ref, o_ref):
        c = c_ref[...]
        ca = c * _sigmoid(c)
        acc = None
        for bb in range(B):
            t = ca[:, bb:bb + 1] * dm_ref[bb:bb + 1, :]
            acc = t if acc is None else acc + t
        o_ref[...] = acc

    return pl.pallas_call(
        body, name=name, grid=(L,),
        in_specs=[pl.BlockSpec((D, B), lambda l: (0, 0)), pl.BlockSpec((None, B, S), lambda l: (l, 0, 0))],
        out_specs=pl.BlockSpec((None, D, S), lambda l: (l, 0, 0)),
        out_shape=jax.ShapeDtypeStruct((L, D, S), F32),
        compiler_params=_params(("parallel",)),
    )(c_all_t, dmod_loc)


def _sum_devices(parts, name):
    _, R, C = parts.shape
    tm = _tile(R, 256, 8)

    def body(p_ref, o_ref):
        acc = p_ref[0].astype(F32)
        for d in range(1, N_DEV):
            acc = acc + p_ref[d].astype(F32)
        o_ref[...] = acc

    return pl.pallas_call(
        body, name=name, grid=(R // tm,),
        in_specs=[pl.BlockSpec((N_DEV, tm, C), lambda i: (0, i, 0))],
        out_specs=pl.BlockSpec((tm, C), lambda i: (i, 0)),
        out_shape=jax.ShapeDtypeStruct((R, C), F32),
        compiler_params=_params(("parallel",)),
    )(parts)


def _adamw_math(w, g, m, v):
    m = ADAM_B1 * m + (1.0 - ADAM_B1) * g
    v = ADAM_B2 * v + (1.0 - ADAM_B2) * (g * g)
    m_hat = m / (1.0 - ADAM_B1 ** ADAM_STEP)
    v_hat = v / (1.0 - ADAM_B2 ** ADAM_STEP)
    delta = -ADAM_LR * (m_hat / (jnp.sqrt(v_hat) + ADAM_EPS) + ADAM_WD * w)
    return delta, m, v


def _adamw(w, g, m, v, name, summed):
    R, C = w.shape
    tm = _tile(R, 256, 16)
    n_parts = g.shape[0] if summed else 0

    def body(w_ref, g_ref, m_ref, v_ref, go_ref, d_ref, mo_ref, vo_ref):
        if summed:
            g = g_ref[0].astype(F32)
            for d in range(1, n_parts):
                g = g + g_ref[d].astype(F32)
        else:
            g = g_ref[...]
        delta, mn, vn = _adamw_math(w_ref[...], g, m_ref[...], v_ref[...])
        go_ref[...] = g
        d_ref[...] = delta
        mo_ref[...] = mn
        vo_ref[...] = vn

    big = pl.BlockSpec((tm, C), lambda i: (i, 0))
    gspec = pl.BlockSpec((n_parts, tm, C), lambda i: (0, i, 0)) if summed else big
    return pl.pallas_call(
        body, name=name, grid=(R // tm,),
        in_specs=[big, gspec, big, big],
        out_specs=[big, big, big, big],
        out_shape=tuple(jax.ShapeDtypeStruct((R, C), F32) for _ in range(4)),
        compiler_params=_params(("parallel",)),
    )(w, g, m, v)


def _adamw_summed(w, parts, m, v, name):
    L = len(parts)
    n_parts, R, C = parts[0].shape
    tm = _tile(R, 128, 16)
    tc = C if tm < R else _tile(C, 256)
    nr = (R // tm) * (C // tc)
    ncb = C // tc

    def body(*refs):
        w_ref, g_refs = refs[0], refs[1:1 + L]
        m_ref, v_ref, go_ref, d_ref, mo_ref, vo_ref = refs[1 + L:]
        for ll in range(L):
            @pl.when(pl.program_id(0) == ll)
            def _(ll=ll):
                g = g_refs[ll][0].astype(F32)
                for d in range(1, n_parts):
                    g = g + g_refs[ll][d].astype(F32)
                delta, mn, vn = _adamw_math(w_ref[...], g, m_ref[...], v_ref[...])
                go_ref[...] = g
                d_ref[...] = delta
                mo_ref[...] = mn
                vo_ref[...] = vn

    big = pl.BlockSpec((None, tm, tc), lambda l, i: (l, i // ncb, i % ncb))

    def part_index(l, i, ll):
        i = jnp.where(l == ll, i, 0)
        return 0, i // ncb, i % ncb

    gspecs = [pl.BlockSpec((n_parts, tm, tc), functools.partial(part_index, ll=ll)) for ll in range(L)]
    return pl.pallas_call(
        body, name=name, grid=(L, nr),
        in_specs=[big, *gspecs, big, big],
        out_specs=[big, big, big, big],
        out_shape=tuple(jax.ShapeDtypeStruct((L, R, C), F32) for _ in range(4)),
        compiler_params=_params(("arbitrary", "arbitrary")),
    )(w, *parts, m, v)


def _pack(arrs, D):
    L = arrs[0].shape[0]
    cols = []
    for a in arrs:
        f = a.reshape(L, -1)
        n = f.shape[1]
        cols.append(jnp.pad(f, ((0, 0), (0, -(-n // D) * D - n))))
    flat = jnp.concatenate(cols, axis=1)
    return flat.reshape(L, flat.shape[1] // D, D)


def _unpack(p, shapes, D):
    L = p.shape[0]
    out, r = [], 0
    for s in shapes:
        n = math.prod(s[1:])
        rows = -(-n // D)
        out.append(p[:, r:r + rows].reshape(L, rows * D)[:, :n].reshape(s))
        r += rows
    return out


def kernel(x, c, w_in, b_f, conv_w, conv_b, conv_ln_g, conv_ln_b, w_o, w_ffn_in, w_ffn_out, mix_pre_g, mix_post_g, ffn_pre_g, ffn_post_g, ada_w, ada_b, loss_target, m_w_in, m_b_f, m_conv_w, m_conv_b, m_conv_ln_g, m_conv_ln_b, m_w_o, m_w_ffn_in, m_w_ffn_out, m_mix_pre_g, m_mix_post_g, m_ffn_pre_g, m_ffn_post_g, m_ada_w, m_ada_b, v_w_in, v_b_f, v_conv_w, v_conv_b, v_conv_ln_g, v_conv_ln_b, v_w_o, v_w_ffn_in, v_w_ffn_out, v_mix_pre_g, v_mix_post_g, v_ffn_pre_g, v_ffn_post_g, v_ada_w, v_ada_b):
    L, D, s_in = w_in.shape
    T = x.shape[1]
    H = b_f.shape[1]
    A = D // 2
    C = D - A
    cs = conv_w.shape[2]
    F = w_ffn_out.shape[1] * N_DEV
    s_ff = w_ffn_in.shape[2]
    w_ffn_in_t = jnp.transpose(w_ffn_in, (0, 2, 1))
    w_in_t = jnp.transpose(w_in, (0, 2, 1))
    s_ada = ada_w.shape[2]
    R = 16
    me = _my_index()
    x0 = x[0]
    target = loss_target[0]
    tq = _tile(T, 512)
    nq = T // tq

    c_all = _exchange(c, gather=True, name="gather_c").reshape(N_DEV, D)
    ada_b_loc = lax.dynamic_slice_in_dim(ada_b, me * s_ada, s_ada, axis=1)[:, None, :]
    mod_loc = _ada_fwd(c_all, ada_w, ada_b_loc, "ada_fwd")
    mod_g = _exchange(mod_loc, gather=True, name="gather_mod")
    mod = lax.dynamic_index_in_dim(mod_g, me, axis=2, keepdims=False)
    mod = jnp.transpose(mod, (1, 0, 2)).reshape(L, N_MOD, 1, D)
    cw_g = _exchange(conv_w, gather=True, name="gather_conv_w")
    conv_w_full = jnp.transpose(cw_g, (1, 2, 0, 3)).reshape(L, CONV_K, C)
    conv_w_pad = jnp.pad(conv_w_full, ((0, 0), (0, CONV_PAD - CONV_K), (0, 0)))
    b_f_col = jnp.pad(b_f, ((0, 0), (0, R - H)))[:, :, None]

    h_gmix, h_gffn, chain = [], [], (mod_g, cw_g)
    for l in range(L):
        h_gmix.append(_gather_start([w_in_t[l].astype(BF16), w_o[l].astype(BF16)], f"gather_mix_{l}", chain))
        h_gffn.append(_gather_start([w_ffn_in_t[l].astype(BF16), w_ffn_out[l].astype(BF16)], f"gather_ffn_{l}",
                                    (h_gmix[l]["token"],)))
        chain = (h_gffn[l]["token"],)
    gather_tokens = chain

    def by_rows(g):
        return g.reshape(N_DEV * g.shape[1], g.shape[2])

    W_in_t, W_f_t, W_c_t, W_o, W_ffn_in_t, W_ffn_out = ([None] * L for _ in range(6))

    saved = []
    xc = x0
    for l in range(L):
        sh1, sc1, g1, sh2, sc2, g2 = (mod[l, k] for k in range(N_MOD))
        gpre1, gpost1, gpre2, gpost2 = (p[l][None, :] for p in (mix_pre_g, mix_post_g, ffn_pre_g, ffn_post_g))
        dep = gather_tokens[0] if l == 0 else xc
        g_in, g_o = _gather_wait(_gather_mid(h_gmix[l], dep, f"gather_mid_mix_{l}"), dep, f"gather_wait_mix_{l}")
        W_in_t[l], W_o[l] = by_rows(g_in), by_rows(g_o)
        W_f_t[l] = jnp.pad(W_in_t[l][3 * A:3 * A + H], ((0, R - H), (0, 0)))
        W_c_t[l] = W_in_t[l][3 * A + H:]
        if l == 0:
            h1 = _prenorm(xc, gpre1, sc1, sh1, "prenorm1_0", first=gather_tokens)
        qkv = _mm([(h1, W_in_t[l])], "nt", BF16, f"proj_qkv_{l}", tm=MM_ROWS, n_cols=3 * A)
        cproj = _mm([(h1, W_c_t[l])], "nt", F32, f"proj_conv_{l}", tm=MM_ROWS, tn=MM_COLS)
        fl, cum = _fgate_fwd(h1, W_f_t[l], b_f_col[l], f"fgate_{l}")
        cum_rows = cum[:H].reshape(H * nq, tq)
        o, lse = _attn_fwd(qkv, cum_rows, H, f"attn_{l}", tq)
        h_ffn_l = _gather_mid(h_gffn[l], o, f"gather_mid_ffn_{l}")
        u3, u1c = _conv_fwd(cproj, conv_w_pad[l], conv_b[l][None, :], conv_ln_g[l][None, :], conv_ln_b[l][None, :],
                            f"conv_{l}")
        cat = jnp.concatenate([o, u3], axis=-1)
        y1, x_mid, h2 = _mm_postnorm(cat, W_o[l], xc, g1, gpost1, f"out_proj_{l}", nxt=(gpre2, sc2, sh2))
        g_fi, g_fo = _gather_wait(h_ffn_l, x_mid, f"gather_wait_ffn_{l}")
        W_ffn_in_t[l], W_ffn_out[l] = by_rows(g_fi), by_rows(g_fo)
        g, u, act = _ffn_in_fwd(h2, W_ffn_in_t[l], f"ffn_in_{l}")
        saved_l = (xc, h1, qkv, cproj, u1c, fl, cum_rows, o, lse, cat, y1, x_mid, h2, g, u, act)
        if l + 1 < L:
            nxt = (mix_pre_g[l + 1][None, :], mod[l + 1, 1], mod[l + 1, 0])
            y2, xc, h1 = _mm_postnorm(act, W_ffn_out[l], x_mid, g2, gpost2, f"ffn_out_{l}", nxt=nxt)
        else:
            y2, xc = _mm_postnorm(act, W_ffn_out[l], x_mid, g2, gpost2, f"ffn_out_{l}")
        saved.append(saved_l + (y2,))

    loss_tile, dx = _loss_head(xc, target, "loss_head")
    loss = lax.psum(loss_tile[0, 0], ("x", "y", "c"))

    def reduce_start(parts, name):
        return _scatter_start(parts, "scatter_" + name)

    small, h_ffn, h_mix = [None] * L, [None] * L, [None] * L
    for l in reversed(range(L)):
        xin, h1, qkv, cproj, u1c, fl, cum_rows, o, lse, cat, y1, x_mid, h2, g, u, act, y2 = saved[l]
        sh1, sc1, g1, sh2, sc2, g2 = (mod[l, k] for k in range(N_MOD))
        gpre1, gpost1, gpre2, gpost2 = (p[l][None, :] for p in (mix_pre_g, mix_post_g, ffn_pre_g, ffn_post_g))
        dy2, sm_post2 = _postnorm_bwd(dx, y2, g2, gpost2, f"postnorm2_bwd_{l}")
        dgate, dup = _ffn_act_bwd(dy2, W_ffn_out[l], g, u, f"ffn_act_bwd_{l}")
        dW_ffn_out = _mm([(act, dy2)], "tn", BF16, f"dw_ffn_out_{l}", tm=DW_TILE)
        dh2 = _mm([(dgate, W_ffn_in_t[l], 0), (dup, W_ffn_in_t[l], 1)], "nn", F32, f"dh2_{l}", tn=FFN_COLS)
        dWg_t = _mm([(dgate, h2)], "tn", BF16, f"dw_ffn_gate_{l}", tm=DW_TILE)
        dWu_t = _mm([(dup, h2)], "tn", BF16, f"dw_ffn_up_{l}", tm=DW_TILE)
        half = N_DEV // 2
        dW_ffn_in_t = jnp.concatenate([dWg_t.reshape(half, s_ff, D), dWu_t.reshape(half, s_ff, D)], axis=0)
        h_ffn[l], token = reduce_start([dW_ffn_in_t, dW_ffn_out.reshape(N_DEV, F // N_DEV, D)], f"ffn_{l}")
        dx_mid, sm_pre2 = _prenorm_bwd(dh2, x_mid, dx, gpre2, sc2, f"prenorm2_bwd_{l}", first=(token,))
        dy1, sm_post1 = _postnorm_bwd(dx_mid, y1, g1, gpost1, f"postnorm1_bwd_{l}")
        dcat = _mm([(dy1, W_o[l])], "nt", F32, f"dcat_{l}", tm=MM_ROWS, tn=MM_COLS)
        dW_o = _mm([(cat, dy1)], "tn", BF16, f"dw_o_{l}", tm=DW_TILE)
        dqkv, dcum_rows, dcum_q = _attn_bwd(qkv, dcat, o, lse, cum_rows, H, f"attn_bwd_{l}", tq)
        dcproj, dconv_w, sm_conv = _conv_bwd(cproj, u1c, dcat, conv_w_pad[l], conv_ln_g[l][None, :],
                                             conv_ln_b[l][None, :], f"conv_bwd_{l}")
        dcum = dcum_rows.reshape(H, T) + jnp.transpose(dcum_q, (0, 2, 1)).reshape(H, T)
        dcum = jnp.pad(dcum, ((0, R - H), (0, 0)))
        dfl_t, dwf_t, dbf = _fgate_bwd(dcum, fl, h1, f"fgate_bwd_{l}")
        dfl = jnp.transpose(dfl_t).astype(BF16)
        dh1 = _mm([(dqkv[0], W_in_t[l], 0), (dqkv[1], W_in_t[l], 1), (dqkv[2], W_in_t[l], 2),
                   (dfl, W_f_t[l]), (dcproj, W_c_t[l])], "nn", F32, f"dh1_{l}", tm=MM_ROWS, tn=FFN_COLS)
        dWq_t = _dw_stack(dqkv, h1, f"dw_qkv_{l}")
        dWc_t = _mm([(dcproj, h1)], "tn", F32, f"dw_conv_{l}", tm=DW_TILE)
        dW_in_t = jnp.concatenate([dWq_t, dwf_t[:H], dWc_t], axis=0).astype(BF16)
        h_mix[l], token = reduce_start([dW_in_t.reshape(N_DEV, s_in, D), dW_o.reshape(N_DEV, D // N_DEV, D)], f"mix_{l}")
        dx, sm_pre1 = _prenorm_bwd(dh1, xin, dx_mid, gpre1, sc1, f"prenorm1_bwd_{l}", first=(token,))
        dmod = jnp.stack([sm_pre1[0], sm_pre1[1], sm_post1[0], sm_pre2[0], sm_pre2[1], sm_post2[0]])
        small[l] = (dmod, sm_pre1[2], sm_post1[1], sm_pre2[2], sm_post2[1], sm_conv[0], sm_conv[1], sm_conv[2],
                    dbf[:H, 0], dconv_w[:CONV_K])
    grad_x = dx[None]

    small_names = 10
    small_l = [jnp.stack([small[l][k] for l in range(L)]) for k in range(small_names)]
    small_shapes = [a.shape for a in small_l]
    packed = _pack(small_l, D)
    rows = packed.shape[1]
    rows_pad = -(-L * rows // 8) * 8
    packed2 = jnp.pad(packed.reshape(L * rows, D), ((0, rows_pad - L * rows), (0, 0)))
    h_small = _gather_start([packed2], "gather_small")

    got_ffn = [_scatter_wait(h_ffn[l], h_small["token"], f"scatter_wait_ffn_{l}") for l in range(L)]

    def step(w, m, v, parts, name):
        return list(_adamw_summed(w, parts, m, v, "adamw_" + name))

    swap = lambda t: jnp.transpose(t, (0, 2, 1))
    r_w_ffn_in = [swap(t) for t in step(w_ffn_in_t, swap(m_w_ffn_in), swap(v_w_ffn_in),
                                        [got_ffn[l][0] for l in range(L)], "w_ffn_in")]
    r_w_ffn_out = step(w_ffn_out, m_w_ffn_out, v_w_ffn_out, [got_ffn[l][1] for l in range(L)], "w_ffn_out")
    got_mix = [_scatter_wait(h_mix[l], r_w_ffn_out[1], f"scatter_wait_mix_{l}") for l in range(L)]
    r_w_in = [swap(t) for t in step(w_in_t, swap(m_w_in), swap(v_w_in), [got_mix[l][0] for l in range(L)], "w_in")]
    r_w_o = step(w_o, m_w_o, v_w_o, [got_mix[l][1] for l in range(L)], "w_o")

    small_g = _gather_wait(_gather_mid(h_small, r_w_o[1], "gather_small_mid"), r_w_o[1], "gather_small_wait")[0]
    small_sum = _sum_devices(small_g, "sum_small")[:L * rows].reshape(L, rows, D)
    (g_ada_b6, g_mix_pre, g_mix_post, g_ffn_pre, g_ffn_post, g_conv_b, g_ln_g, g_ln_b, g_b_f,
     g_conv_w_full) = _unpack(small_sum, small_shapes, D)
    g_ada_b = g_ada_b6.reshape(L, N_MOD * D)
    g_conv_w = lax.dynamic_slice_in_dim(g_conv_w_full, me * cs, cs, axis=2)
    dmod_all = small_g[:, :L * rows].reshape(N_DEV, L, rows, D)[:, :, :N_MOD].reshape(N_DEV, L, N_MOD * D)
    dmod_loc = jnp.transpose(lax.dynamic_slice_in_dim(dmod_all, me * s_ada, s_ada, axis=2), (1, 0, 2))
    g_ada_w = _ada_bwd(jnp.transpose(c_all), dmod_loc, "ada_bwd")
    r_ada_w = [t.reshape(ada_w.shape) for t in _adamw(
        ada_w.reshape(L * D, s_ada), g_ada_w.reshape(L * D, s_ada), m_ada_w.reshape(L * D, s_ada),
        v_ada_w.reshape(L * D, s_ada), "adamw_ada_w", False)]

    sw = [b_f, conv_w, conv_b, conv_ln_g, conv_ln_b, mix_pre_g, mix_post_g, ffn_pre_g, ffn_post_g, ada_b]
    sg = [g_b_f, g_conv_w, g_conv_b, g_ln_g, g_ln_b, g_mix_pre, g_mix_post, g_ffn_pre, g_ffn_post, g_ada_b]
    sm = [m_b_f, m_conv_w, m_conv_b, m_conv_ln_g, m_conv_ln_b, m_mix_pre_g, m_mix_post_g, m_ffn_pre_g, m_ffn_post_g, m_ada_b]
    sv = [v_b_f, v_conv_w, v_conv_b, v_conv_ln_g, v_conv_ln_b, v_mix_pre_g, v_mix_post_g, v_ffn_pre_g, v_ffn_post_g, v_ada_b]
    shapes = [a.shape for a in sw]

    def flat(arrs):
        p = _pack(arrs, D)
        n = p.shape[0] * p.shape[1]
        return jnp.pad(p.reshape(n, D), ((0, -(-n // 8) * 8 - n), (0, 0))), p.shape

    pw, pshape = flat(sw)
    pg, pm, pv = flat(sg)[0], flat(sm)[0], flat(sv)[0]
    s_outs = _adamw(pw, pg, pm, pv, "adamw_small", False)
    n_small = pshape[0] * pshape[1]
    s_g, s_d, s_m, s_v = (_unpack(t[:n_small].reshape(pshape), shapes, D) for t in s_outs)

    big = {"w_in": r_w_in, "w_o": r_w_o, "w_ffn_in": r_w_ffn_in, "w_ffn_out": r_w_ffn_out, "ada_w": r_ada_w}
    order = ["w_in", "b_f", "conv_w", "conv_b", "conv_ln_g", "conv_ln_b", "w_o", "w_ffn_in", "w_ffn_out",
             "mix_pre_g", "mix_post_g", "ffn_pre_g", "ffn_post_g", "ada_w", "ada_b"]
    small_pos = {n: i for i, n in enumerate(["b_f", "conv_w", "conv_b", "conv_ln_g", "conv_ln_b", "mix_pre_g",
                                             "mix_post_g", "ffn_pre_g", "ffn_post_g", "ada_b"])}

    def pick(n, k):
        if n in big:
            return big[n][k]
        return (s_g, s_d, s_m, s_v)[k][small_pos[n]]

    return (loss, grad_x, *[pick(n, 0) for n in order], *[pick(n, 1) for n in order],
            *[pick(n, 2) for n in order], *[pick(n, 3) for n in order])
```

```python
import functools
import math

import jax
import jax.numpy as jnp
from jax import lax
from jax.experimental import pallas as pl
from jax.experimental.pallas import tpu as pltpu

F32 = jnp.float32
BF16 = jnp.bfloat16
MESH = pl.DeviceIdType.MESH
N_DEV = 8
EPS = 1e-6
CONV_K = 31
CONV_PAD = 32
CONV_CHUNK = 128
N_MOD = 6
NEG = -1e30
LANES = 128
VMEM_LIMIT = 56 * 2**20
MM_TILE = 1024
MM_ROWS = 2048
MM_COLS = 512
MM_ROWS_SMALL = 512
FFN_COLS = 256
DW_TILE = 1408
ADAM_LR, ADAM_B1, ADAM_B2, ADAM_EPS, ADAM_WD, ADAM_STEP = 0.001, 0.9, 0.999, 1e-08, 0.01, 10

NN = (((1,), (0,)), ((), ()))
NT = (((1,), (1,)), ((), ()))
TN = (((0,), (0,)), ((), ()))


def _dot(a, b, dims):
    return lax.dot_general(a, b, dims, preferred_element_type=F32)


def _tile(n, pref, align=LANES):
    if n <= pref:
        return n
    t = (pref // align) * align
    while t >= align:
        if n % t == 0:
            return t
        t -= align
    return n


def _params(sem=None):
    return pltpu.CompilerParams(dimension_semantics=sem, vmem_limit_bytes=VMEM_LIMIT)


def _sigmoid(x):
    return 1.0 / (1.0 + jnp.exp(-x))


def _my_index():
    return 4 * lax.axis_index("x") + 2 * lax.axis_index("y") + lax.axis_index("c")


def _exchange(x, *, gather, name):
    blk = x.shape if gather else x.shape[1:]

    def body(x_ref, y_ref, send_sems, recv_sems, local_sem):
        mx, my, mc = lax.axis_index("x"), lax.axis_index("y"), lax.axis_index("c")
        me = 4 * mx + 2 * my + mc

        def src(p):
            return x_ref if gather else x_ref.at[p]

        mine = pltpu.make_async_copy(src(me), y_ref.at[me], local_sem)
        mine.start()
        copies = []
        for k in range(1, N_DEV):
            px = (1 - mx) if (k >> 2) & 1 else mx
            py = (1 - my) if (k >> 1) & 1 else my
            pc = (1 - mc) if k & 1 else mc
            cp = pltpu.make_async_remote_copy(
                src_ref=src(4 * px + 2 * py + pc), dst_ref=y_ref.at[me],
                send_sem=send_sems.at[k - 1], recv_sem=recv_sems.at[k - 1],
                device_id=(px, py, pc), device_id_type=MESH)
            cp.start()
            copies.append(cp)
        for cp in copies:
            cp.wait()
        mine.wait()

    return pl.pallas_call(
        body, name=name,
        out_shape=jax.ShapeDtypeStruct((N_DEV,) + tuple(blk), x.dtype),
        in_specs=[pl.BlockSpec(memory_space=pl.ANY)],
        out_specs=pl.BlockSpec(memory_space=pl.ANY),
        scratch_shapes=[pltpu.SemaphoreType.DMA((N_DEV - 1,)), pltpu.SemaphoreType.DMA((N_DEV - 1,)),
                        pltpu.SemaphoreType.DMA(())],
    )(x)


_HBM = pl.BlockSpec(memory_space=pl.ANY)


_SEM = pl.BlockSpec(memory_space=pltpu.SEMAPHORE)
_HBM_SPEC = pl.BlockSpec(memory_space=pltpu.HBM)
_EFFECT = pltpu.CompilerParams(has_side_effects=pltpu.SideEffectType.DATAFLOW_SIDE_EFFECTING)


def _in_hbm(a):
    return pltpu.with_memory_space_constraint(a, pltpu.HBM)


def _scatter_copies(x_refs, land_refs, send_sems, recv_sems, loc_sems):
    mx, my, mc = lax.axis_index("x"), lax.axis_index("y"), lax.axis_index("c")
    me = 4 * mx + 2 * my + mc
    local, remote = [], []
    for a, (x_ref, land_ref) in enumerate(zip(x_refs, land_refs)):
        local.append(pltpu.make_async_copy(x_ref.at[me], land_ref.at[me], loc_sems.at[a]))
        for k in range(1, N_DEV):
            px = (1 - mx) if (k >> 2) & 1 else mx
            py = (1 - my) if (k >> 1) & 1 else my
            pc = (1 - mc) if k & 1 else mc
            remote.append(pltpu.make_async_remote_copy(
                src_ref=x_ref.at[4 * px + 2 * py + pc], dst_ref=land_ref.at[me],
                send_sem=send_sems.at[(N_DEV - 1) * a + k - 1], recv_sem=recv_sems.at[(N_DEV - 1) * a + k - 1],
                device_id=(px, py, pc), device_id_type=MESH))
    return local, remote


def _scatter_start(xs, name):
    n = len(xs)

    def body(*refs):
        local, remote = _scatter_copies(refs[:n], refs[n:2 * n], *refs[2 * n:2 * n + 3])
        for cp in local + remote:
            cp.start()
        refs[-1][...] = jnp.zeros_like(refs[-1])

    lands = [lax.empty(x.shape, x.dtype) for x in xs]
    n_sems = (N_DEV - 1) * n
    outs = pl.pallas_call(
        body, name=name,
        out_shape=(pltpu.SemaphoreType.DMA((n_sems,)), pltpu.SemaphoreType.DMA((n_sems,)), pltpu.SemaphoreType.DMA((n,)),
                   *[pltpu.HBM(x.shape, x.dtype) for x in xs], *[pltpu.HBM(x.shape, x.dtype) for x in xs],
                   jax.ShapeDtypeStruct((8, LANES), F32)),
        in_specs=[_HBM_SPEC] * (2 * n),
        out_specs=(_SEM, _SEM, _SEM, *[_HBM_SPEC] * (2 * n), pl.BlockSpec(memory_space=pltpu.VMEM)),
        input_output_aliases={i: 3 + i for i in range(2 * n)},
        compiler_params=_EFFECT,
    )(*[_in_hbm(x) for x in xs], *[_in_hbm(t) for t in lands])
    return (outs[0], outs[1], outs[2], list(outs[3:3 + n]), list(outs[3 + n:3 + 2 * n])), outs[-1]


def _scatter_wait(handle, after, name):
    send_sems, recv_sems, loc_sems, x_thru, land_thru = handle
    n = len(x_thru)

    def body(*refs):
        local, remote = _scatter_copies(refs[:n], refs[n:2 * n], *refs[2 * n:2 * n + 3])
        for cp in local:
            cp.wait()
        for cp in remote:
            cp.wait_send()
            cp.wait_recv()

    outs = pl.pallas_call(
        body, name=name,
        out_shape=[pltpu.HBM(x.shape, x.dtype) for x in x_thru + land_thru],
        in_specs=[_HBM_SPEC] * (2 * n) + [_SEM, _SEM, _SEM, _HBM],
        out_specs=[_HBM_SPEC] * (2 * n),
        input_output_aliases={i: i for i in range(2 * n)},
        compiler_params=_EFFECT,
    )(*x_thru, *land_thru, send_sems, recv_sems, loc_sems, after)
    return list(outs[n:])


def _gather_first_copies(x_refs, y_refs, send_sems, sib_recv, ici_recv, loc_sems):
    mx, my, mc = lax.axis_index("x"), lax.axis_index("y"), lax.axis_index("c")
    me = 4 * mx + 2 * my + mc
    local, remote = [], []
    for a, (x_ref, y_ref) in enumerate(zip(x_refs, y_refs)):
        local.append(pltpu.make_async_copy(x_ref, y_ref.at[me], loc_sems.at[a]))
        remote.append(pltpu.make_async_remote_copy(
            src_ref=x_ref, dst_ref=y_ref.at[me], send_sem=send_sems.at[4 * a], recv_sem=sib_recv.at[a],
            device_id=(mx, my, 1 - mc), device_id_type=MESH))
        for j, (px, py) in enumerate([(1 - mx, my), (mx, 1 - my), (1 - mx, 1 - my)]):
            remote.append(pltpu.make_async_remote_copy(
                src_ref=x_ref, dst_ref=y_ref.at[me], send_sem=send_sems.at[4 * a + 1 + j], recv_sem=ici_recv.at[3 * a + j],
                device_id=(px, py, mc), device_id_type=MESH))
    return local, remote


def _gather_forward_copies(y_refs, ici_recv, fwd_send, fwd_recv):
    mx, my, mc = lax.axis_index("x"), lax.axis_index("y"), lax.axis_index("c")
    pairs = []
    for a, y_ref in enumerate(y_refs):
        for j, (px, py) in enumerate([(1 - mx, my), (mx, 1 - my), (1 - mx, 1 - my)]):
            slot = y_ref.at[4 * px + 2 * py + mc]
            arrival = pltpu.make_async_remote_copy(
                src_ref=slot, dst_ref=slot, send_sem=fwd_send.at[3 * a + j], recv_sem=ici_recv.at[3 * a + j],
                device_id=(px, py, mc), device_id_type=MESH)
            onward = pltpu.make_async_remote_copy(
                src_ref=slot, dst_ref=slot, send_sem=fwd_send.at[3 * a + j], recv_sem=fwd_recv.at[3 * a + j],
                device_id=(mx, my, 1 - mc), device_id_type=MESH)
            pairs.append((arrival, onward))
    return pairs


def _gather_start(xs, name, after=()):
    n = len(xs)
    ni = 2 * n + len(after)

    def body(*refs):
        local, remote = _gather_first_copies(refs[:n], refs[n:2 * n], *refs[ni:ni + 4])
        for cp in local + remote:
            cp.start()
        refs[-1][...] = jnp.zeros_like(refs[-1])

    ys = [lax.empty((N_DEV,) + tuple(x.shape), x.dtype) for x in xs]
    dma = pltpu.SemaphoreType.DMA
    outs = pl.pallas_call(
        body, name=name,
        out_shape=(dma((4 * n,)), dma((n,)), dma((3 * n,)), dma((n,)),
                   *[pltpu.HBM(x.shape, x.dtype) for x in xs], *[pltpu.HBM(y.shape, y.dtype) for y in ys],
                   jax.ShapeDtypeStruct((8, LANES), F32)),
        in_specs=[_HBM_SPEC] * (2 * n) + [_HBM] * len(after),
        out_specs=(_SEM, _SEM, _SEM, _SEM, *[_HBM_SPEC] * (2 * n), pl.BlockSpec(memory_space=pltpu.VMEM)),
        input_output_aliases={i: 4 + i for i in range(2 * n)},
        compiler_params=_EFFECT,
    )(*[_in_hbm(x) for x in xs], *[_in_hbm(y) for y in ys], *after)
    return dict(send=outs[0], sib_recv=outs[1], ici_recv=outs[2], loc=outs[3], x=list(outs[4:4 + n]),
                y=list(outs[4 + n:4 + 2 * n]), token=outs[-1])


def _gather_mid(h, after, name):
    n = len(h["y"])

    def body(*refs):
        for arrival, onward in _gather_forward_copies(refs[:n], refs[n], refs[n + 2 + n], refs[n + 3 + n]):
            arrival.wait_recv()
            onward.start()
        refs[-1][...] = jnp.zeros_like(refs[-1])

    dma = pltpu.SemaphoreType.DMA
    outs = pl.pallas_call(
        body, name=name,
        out_shape=(*[pltpu.HBM(y.shape, y.dtype) for y in h["y"]], dma((3 * n,)), dma((3 * n,)),
                   jax.ShapeDtypeStruct((8, LANES), F32)),
        in_specs=[_HBM_SPEC] * n + [_SEM, _HBM],
        out_specs=(*[_HBM_SPEC] * n, _SEM, _SEM, pl.BlockSpec(memory_space=pltpu.VMEM)),
        input_output_aliases={i: i for i in range(n)},
        compiler_params=_EFFECT,
    )(*h["y"], h["ici_recv"], after)
    return dict(h, y=list(outs[:n]), fwd_send=outs[n], fwd_recv=outs[n + 1], token=outs[n + 2])


def _gather_wait(h, after, name):
    n = len(h["y"])

    def body(*refs):
        x_refs, y_refs = refs[:n], refs[n:2 * n]
        send, sib_recv, loc, fwd_send, fwd_recv = refs[2 * n:2 * n + 5]
        local, remote = _gather_first_copies(x_refs, y_refs, send, sib_recv, fwd_recv, loc)
        for cp in local:
            cp.wait()
        for k, cp in enumerate(remote):
            cp.wait_send()
            if k % 4 == 0:
                cp.wait_recv()
        for _, onward in _gather_forward_copies(y_refs, fwd_recv, fwd_send, fwd_recv):
            onward.wait_send()
            onward.wait_recv()

    outs = pl.pallas_call(
        body, name=name,
        out_shape=[pltpu.HBM(t.shape, t.dtype) for t in h["x"] + h["y"]],
        in_specs=[_HBM_SPEC] * (2 * n) + [_SEM] * 5 + [_HBM],
        out_specs=[_HBM_SPEC] * (2 * n),
        input_output_aliases={i: i for i in range(2 * n)},
        compiler_params=_EFFECT,
    )(*h["x"], *h["y"], h["send"], h["sib_recv"], h["loc"], h["fwd_send"], h["fwd_recv"], after)
    return list(outs[n:])


def _mm(pairs, mode, out_dtype, name, tm=MM_TILE, tn=MM_TILE, n_cols=None):
    dims = {"nn": NN, "nt": NT, "tn": TN}[mode]
    a0, b0 = pairs[0][0], pairs[0][1]
    M = a0.shape[1] if mode == "tn" else a0.shape[0]
    N = n_cols or (b0.shape[0] if mode == "nt" else b0.shape[1])
    tm, tn = _tile(M, tm), _tile(N, tn)
    in_specs, args = [], []
    for pr in pairs:
        a, b = pr[0], pr[1]
        if mode == "tn":
            K = a.shape[0]
            in_specs.append(pl.BlockSpec((K, tm), lambda i, j: (0, i)))
            in_specs.append(pl.BlockSpec((K, tn), lambda i, j: (0, j)))
        elif mode == "nn":
            K = a.shape[1]
            rb = pr[2] if len(pr) > 2 else 0
            in_specs.append(pl.BlockSpec((tm, K), lambda i, j: (i, 0)))
            in_specs.append(pl.BlockSpec((K, tn), functools.partial(lambda i, j, rb: (rb, j), rb=rb)))
        else:
            K = a.shape[1]
            cb = pr[2] if len(pr) > 2 else 0
            in_specs.append(pl.BlockSpec((tm, K), lambda i, j: (i, 0)))
            in_specs.append(pl.BlockSpec((tn, K), functools.partial(lambda i, j, cb: (j, cb), cb=cb)))
        args += [a, b]
    n_pairs = len(pairs)

    def body(*refs):
        o_ref = refs[-1]
        acc = None
        for k in range(n_pairs):
            d = _dot(refs[2 * k][...], refs[2 * k + 1][...], dims)
            acc = d if acc is None else acc + d
        o_ref[...] = acc.astype(o_ref.dtype)

    return pl.pallas_call(
        body, name=name, grid=(M // tm, N // tn), in_specs=in_specs,
        out_specs=pl.BlockSpec((tm, tn), lambda i, j: (i, j)),
        out_shape=jax.ShapeDtypeStruct((M, N), out_dtype),
        compiler_params=_params(("parallel", "arbitrary")),
    )(*args)


def _dw_stack(a3, b, name):
    n, T, A = a3.shape
    D = b.shape[1]

    def body(a_ref, b_ref, o_ref):
        o_ref[...] = _dot(a_ref[...], b_ref[...], TN)

    return pl.pallas_call(
        body, name=name, grid=(n,),
        in_specs=[pl.BlockSpec((None, T, A), lambda k: (k, 0, 0)), pl.BlockSpec((T, D), lambda k: (0, 0))],
        out_specs=pl.BlockSpec((A, D), lambda k: (k, 0)),
        out_shape=jax.ShapeDtypeStruct((n * A, D), F32),
        compiler_params=_params(("parallel",)),
    )(a3, b)


def _prenorm(x, g, sc, sh, name, first=()):
    T, D = x.shape
    tm = _tile(T, 512, 8)

    def body(x_ref, g_ref, sc_ref, sh_ref, *rest):
        h_ref = rest[-1]
        xv = x_ref[...]
        r = lax.rsqrt(jnp.mean(xv * xv, axis=-1, keepdims=True) + EPS)
        h_ref[...] = (((xv * r) * g_ref[...]) * (1.0 + sc_ref[...]) + sh_ref[...]).astype(BF16)

    row = pl.BlockSpec((1, D), lambda i: (0, 0))
    return pl.pallas_call(
        body, name=name, grid=(T // tm,),
        in_specs=[pl.BlockSpec((tm, D), lambda i: (i, 0)), row, row, row] + [_HBM] * len(first),
        out_specs=pl.BlockSpec((tm, D), lambda i: (i, 0)),
        out_shape=jax.ShapeDtypeStruct((T, D), BF16),
        compiler_params=_params(("parallel",)),
    )(x, g, sc, sh, *first)


def _prenorm_bwd(pairs, x, dres, g, sc, name, first=()):
    T, D = x.shape
    tm = _tile(T, 256, 8)
    n_pairs = len(pairs)
    pair_specs, pair_args = [], []
    for pr in pairs:
        a, b = pr[0], pr[1]
        K = a.shape[1]
        rb = pr[2] if len(pr) > 2 else 0
        pair_specs += [pl.BlockSpec((tm, K), lambda i: (i, 0)),
                       pl.BlockSpec((K, D), functools.partial(lambda i, rb: (rb, 0), rb=rb))]
        pair_args += [a, b]

    def body(*refs):
        x_ref, dres_ref, g_ref, sc_ref = refs[2 * n_pairs:2 * n_pairs + 4]
        dx_ref, sm_ref = refs[-2:]

        @pl.when(pl.program_id(0) == 0)
        def _():
            sm_ref[...] = jnp.zeros_like(sm_ref)

        dhv = None
        for k in range(n_pairs):
            d = _dot(refs[2 * k][...], refs[2 * k + 1][...], NN)
            dhv = d if dhv is None else dhv + d
        xv = x_ref[...]
        r = lax.rsqrt(jnp.mean(xv * xv, axis=-1, keepdims=True) + EPS)
        xh = xv * r
        one_sc = 1.0 + sc_ref[...]
        sm_ref[0:1, :] += jnp.sum(dhv, axis=0, keepdims=True)
        sm_ref[1:2, :] += jnp.sum(dhv * (xh * g_ref[...]), axis=0, keepdims=True)
        sm_ref[2:3, :] += jnp.sum(dhv * one_sc * xh, axis=0, keepdims=True)
        dxh = dhv * one_sc * g_ref[...]
        dx_ref[...] = dres_ref[...] + r * (dxh - xh * jnp.mean(dxh * xh, axis=-1, keepdims=True))

    row = pl.BlockSpec((1, D), lambda i: (0, 0))
    big = pl.BlockSpec((tm, D), lambda i: (i, 0))
    return pl.pallas_call(
        body, name=name, grid=(T // tm,),
        in_specs=pair_specs + [big, big, row, row] + [_HBM] * len(first),
        out_specs=[big, pl.BlockSpec((8, D), lambda i: (0, 0))],
        out_shape=(jax.ShapeDtypeStruct((T, D), F32), jax.ShapeDtypeStruct((8, D), F32)),
        compiler_params=_params(("arbitrary",)),
    )(*pair_args, x, dres, g, sc, *first)


def _mm_postnorm(a, w, x, gate, gpost, name, nxt=None, first=()):
    T, K = a.shape
    D = w.shape[1]
    tm = _tile(T, MM_ROWS_SMALL, 8)

    def body(a_ref, w_ref, x_ref, gate_ref, gp_ref, *rest):
        y = _dot(a_ref[...], w_ref[...], NN)
        r = lax.rsqrt(jnp.mean(y * y, axis=-1, keepdims=True) + EPS)
        xn = x_ref[...] + gate_ref[...] * ((y * r) * gp_ref[...])
        if nxt is None:
            y_ref, xn_ref = rest[-2:]
        else:
            g_ref, sc_ref, sh_ref = rest[:3]
            y_ref, xn_ref, h_ref = rest[-3:]
            rn = lax.rsqrt(jnp.mean(xn * xn, axis=-1, keepdims=True) + EPS)
            h_ref[...] = (((xn * rn) * g_ref[...]) * (1.0 + sc_ref[...]) + sh_ref[...]).astype(BF16)
        y_ref[...] = y
        xn_ref[...] = xn

    row = pl.BlockSpec((1, D), lambda i: (0, 0))
    big = pl.BlockSpec((tm, D), lambda i: (i, 0))
    extra = () if nxt is None else tuple(nxt)
    return pl.pallas_call(
        body, name=name, grid=(T // tm,),
        in_specs=[pl.BlockSpec((tm, K), lambda i: (i, 0)), pl.BlockSpec((K, D), lambda i: (0, 0)), big, row, row]
        + [row] * len(extra) + [_HBM] * len(first),
        out_specs=[big, big] + [big] * (nxt is not None),
        out_shape=(jax.ShapeDtypeStruct((T, D), F32), jax.ShapeDtypeStruct((T, D), F32))
        + ((jax.ShapeDtypeStruct((T, D), BF16),) if nxt is not None else ()),
        compiler_params=_params(("parallel",)),
    )(a, w, x, gate, gpost, *extra, *first)


def _postnorm_bwd(dx, y, gate, gpost, name):
    T, D = y.shape
    tm = _tile(T, 256, 8)

    def body(dx_ref, y_ref, gate_ref, gp_ref, dy_ref, sm_ref):
        @pl.when(pl.program_id(0) == 0)
        def _():
            sm_ref[...] = jnp.zeros_like(sm_ref)

        yv, dxv = y_ref[...], dx_ref[...]
        r = lax.rsqrt(jnp.mean(yv * yv, axis=-1, keepdims=True) + EPS)
        yh = yv * r
        dn = dxv * gate_ref[...]
        sm_ref[0:1, :] += jnp.sum(dxv * (yh * gp_ref[...]), axis=0, keepdims=True)
        sm_ref[1:2, :] += jnp.sum(dn * yh, axis=0, keepdims=True)
        dyh = dn * gp_ref[...]
        dy_ref[...] = (r * (dyh - yh * jnp.mean(dyh * yh, axis=-1, keepdims=True))).astype(BF16)

    row = pl.BlockSpec((1, D), lambda i: (0, 0))
    big = pl.BlockSpec((tm, D), lambda i: (i, 0))
    return pl.pallas_call(
        body, name=name, grid=(T // tm,),
        in_specs=[big, big, row, row],
        out_specs=[big, pl.BlockSpec((8, D), lambda i: (0, 0))],
        out_shape=(jax.ShapeDtypeStruct((T, D), BF16), jax.ShapeDtypeStruct((8, D), F32)),
        compiler_params=_params(("arbitrary",)),
    )(dx, y, gate, gpost)


def _ffn_in_fwd(h, w_t, name):
    T, D = h.shape
    F = w_t.shape[0] // 2
    tm, tn = _tile(T, MM_ROWS), _tile(F, FFN_COLS)
    nj = F // tn

    def body(h_ref, wg_ref, wu_ref, g_ref, u_ref, act_ref):
        hv = h_ref[...]
        g = _dot(hv, wg_ref[...], NT)
        u = _dot(hv, wu_ref[...], NT)
        g_ref[...] = g.astype(BF16)
        u_ref[...] = u.astype(BF16)
        act_ref[...] = ((g * _sigmoid(g)) * u).astype(BF16)

    out = pl.BlockSpec((tm, tn), lambda j, i: (i, j))
    return pl.pallas_call(
        body, name=name, grid=(nj, T // tm),
        in_specs=[pl.BlockSpec((tm, D), lambda j, i: (i, 0)),
                  pl.BlockSpec((tn, D), lambda j, i: (j, 0)),
                  pl.BlockSpec((tn, D), lambda j, i: (j + nj, 0))],
        out_specs=[out, out, out],
        out_shape=tuple(jax.ShapeDtypeStruct((T, F), BF16) for _ in range(3)),
        compiler_params=_params(("parallel", "arbitrary")),
    )(h, w_t, w_t)


def _ffn_act_bwd(dy, w_out, g, u, name):
    T, D = dy.shape
    F = w_out.shape[0]
    tm, tn = _tile(T, MM_ROWS), _tile(F, FFN_COLS)

    def body(dy_ref, w_ref, g_ref, u_ref, dg_ref, du_ref):
        dact = _dot(dy_ref[...], w_ref[...], NT)
        gv, uv = g_ref[...].astype(F32), u_ref[...].astype(F32)
        sg = _sigmoid(gv)
        dg_ref[...] = (dact * uv * (sg * (1.0 + gv * (1.0 - sg)))).astype(BF16)
        du_ref[...] = (dact * (gv * sg)).astype(BF16)

    tile = pl.BlockSpec((tm, tn), lambda j, i: (i, j))
    return pl.pallas_call(
        body, name=name, grid=(F // tn, T // tm),
        in_specs=[pl.BlockSpec((tm, D), lambda j, i: (i, 0)), pl.BlockSpec((tn, D), lambda j, i: (j, 0)), tile, tile],
        out_specs=[tile, tile],
        out_shape=(jax.ShapeDtypeStruct((T, F), BF16), jax.ShapeDtypeStruct((T, F), BF16)),
        compiler_params=_params(("parallel", "arbitrary")),
    )(dy, w_out, g, u)


def _lane_scan(v, reverse):
    T = v.shape[-1]
    lane = lax.broadcasted_iota(jnp.int32, v.shape, 1)
    d = 1
    while d < T:
        if reverse:
            v = v + jnp.where(lane < T - d, pltpu.roll(v, T - d, axis=1), 0.0)
        else:
            v = v + jnp.where(lane >= d, pltpu.roll(v, d, axis=1), 0.0)
        d *= 2
    return v


def _fgate_fwd(h, wf_t, bf, name):
    T, D = h.shape
    R = wf_t.shape[0]

    def body(h_ref, w_ref, b_ref, fl_ref, cum_ref):
        fl = _dot(w_ref[...], h_ref[...], NT) + b_ref[...]
        fl_ref[...] = fl
        logf = jnp.minimum(fl, 0.0) - jnp.log(1.0 + jnp.exp(-jnp.abs(fl)))
        cum_ref[...] = _lane_scan(logf, reverse=False)

    return pl.pallas_call(
        body, name=name,
        out_shape=(jax.ShapeDtypeStruct((R, T), F32), jax.ShapeDtypeStruct((R, T), F32)),
        compiler_params=_params(),
    )(h, wf_t, bf)


def _fgate_bwd(dcum, fl, h, name):
    R, T = fl.shape
    D = h.shape[1]

    def body(dc_ref, fl_ref, h_ref, dfl_ref, dw_ref, db_ref):
        dlogf = _lane_scan(dc_ref[...], reverse=True)
        dfl = dlogf * _sigmoid(-fl_ref[...])
        dfl_ref[...] = dfl
        dw_ref[...] = _dot(dfl.astype(BF16), h_ref[...], NN)
        db_ref[...] = jnp.broadcast_to(jnp.sum(dfl, axis=-1, keepdims=True), (R, LANES))

    return pl.pallas_call(
        body, name=name,
        out_shape=(jax.ShapeDtypeStruct((R, T), F32), jax.ShapeDtypeStruct((R, D), F32),
                   jax.ShapeDtypeStruct((R, LANES), F32)),
        compiler_params=_params(),
    )(dcum, fl, h)


def _head_masks(hpb, dh, rows):
    lane = lax.broadcasted_iota(jnp.int32, (rows, LANES), 1)
    return [(lane >= h * dh) & (lane < (h + 1) * dh) for h in range(hpb)]


def _stack_heads(v, masks):
    return jnp.concatenate([jnp.where(mk, v, jnp.zeros_like(v)) for mk in masks], axis=0)


def _heads_to_lanes(col, masks, tq):
    out = jnp.broadcast_to(col[0:tq], (tq, LANES))
    for h in range(1, len(masks)):
        out = jnp.where(masks[h], col[h * tq:(h + 1) * tq], out)
    return out


def _causal_stack(hpb, tq):
    r = lax.broadcasted_iota(jnp.int32, (tq, tq), 0)
    c = lax.broadcasted_iota(jnp.int32, (tq, tq), 1)
    return jnp.concatenate([c] * hpb, axis=0) <= jnp.concatenate([r] * hpb, axis=0)


def _attn_fwd(qkv, cum_rows, n_heads, name, tq, first=()):
    T = qkv.shape[0]
    A = qkv.shape[1] // 3
    dh = A // n_heads
    hpb = LANES // dh
    nb = A // LANES
    nq = T // tq
    scale = dh ** -0.5

    def body(q_ref, k_ref, v_ref, c_ref, *rest):
        o_ref, l_ref, vbd = rest[-3:]
        hp, i = pl.program_id(0), pl.program_id(1)
        masks = _head_masks(hpb, dh, tq)

        @pl.when(i == 0)
        def _():
            def fill(j, _):
                vbd[j] = _stack_heads(v_ref[pl.ds(pl.multiple_of(j * tq, tq), tq), :], masks)
                return 0

            lax.fori_loop(0, nq, fill, 0)

        qs = _stack_heads(q_ref[...], masks)
        crow0 = hp * hpb * nq

        def tile(j, carry, diag):
            m, l, acc = carry
            kt = k_ref[pl.ds(pl.multiple_of(j * tq, tq), tq), :]
            bias = jnp.concatenate(
                [jnp.broadcast_to(c_ref[pl.ds(crow0 + h * nq + j, 1), :], (tq, tq)) for h in range(hpb)], axis=0)
            s = _dot(qs, kt, NT) * scale - bias
            if diag:
                s = jnp.where(_causal_stack(hpb, tq), s, NEG)
            m_new = jnp.maximum(m, jnp.max(s, axis=-1, keepdims=True))
            p = jnp.exp(s - m_new)
            alpha = jnp.exp(m - m_new)
            l = alpha * l + jnp.sum(p, axis=-1, keepdims=True)
            pcat = jnp.concatenate([p[h * tq:(h + 1) * tq] for h in range(hpb)], axis=1).astype(BF16)
            acc = _heads_to_lanes(alpha, masks, tq) * acc + _dot(pcat, vbd[j], NN)
            return m_new, l, acc

        init = (jnp.full((hpb * tq, 1), NEG, F32), jnp.zeros((hpb * tq, 1), F32), jnp.zeros((tq, LANES), F32))
        carry = lax.fori_loop(0, i, lambda j, c: tile(j, c, False), init)
        m, l, acc = tile(i, carry, True)
        o_ref[...] = (acc / _heads_to_lanes(l, masks, tq)).astype(BF16)
        lse = m + jnp.log(l)
        for h in range(hpb):
            l_ref[:, h:h + 1] = lse[h * tq:(h + 1) * tq]

    return pl.pallas_call(
        body, name=name, grid=(nb, nq),
        in_specs=[pl.BlockSpec((tq, LANES), lambda h, i: (i, h)),
                  pl.BlockSpec((T, LANES), lambda h, i: (0, nb + h)),
                  pl.BlockSpec((T, LANES), lambda h, i: (0, 2 * nb + h)),
                  pl.BlockSpec(cum_rows.shape, lambda h, i: (0, 0))] + [_HBM] * len(first),
        out_specs=[pl.BlockSpec((tq, LANES), lambda h, i: (i, h)),
                   pl.BlockSpec((None, tq, hpb), lambda h, i: (h, i, 0))],
        out_shape=(jax.ShapeDtypeStruct((T, A), BF16), jax.ShapeDtypeStruct((nb, T, hpb), F32)),
        scratch_shapes=[pltpu.VMEM((nq, hpb * tq, LANES), BF16)],
        compiler_params=_params(("arbitrary", "arbitrary")),
    )(qkv, qkv, qkv, cum_rows, *first)


def _attn_bwd(qkv, dcat, o, lse, cum_rows, n_heads, name, tq):
    T = qkv.shape[0]
    A = qkv.shape[1] // 3
    dh = A // n_heads
    hpb = LANES // dh
    nb = A // LANES
    nq = T // tq
    scale = dh ** -0.5

    def body(q_ref, k_ref, v_ref, do_ref, o_ref, l_ref, c_ref, dqkv_ref, dc_ref, dr_ref,
             dq_acc, delta, drow, qs_scr, dos_scr, kbd_scr):
        hp = pl.program_id(0)
        masks = _head_masks(hpb, dh, tq)
        crow0 = hp * hpb * nq

        def prologue(i, _):
            rs = pl.ds(pl.multiple_of(i * tq, tq), tq)
            do = do_ref[rs, :]
            prod = do * o_ref[rs, :].astype(F32)
            for h in range(hpb):
                delta[rs, h:h + 1] = jnp.sum(jnp.where(masks[h], prod, 0.0), axis=-1, keepdims=True)
            qs_scr[i] = _stack_heads(q_ref[rs, :], masks)
            dos_scr[i] = _stack_heads(do, masks).astype(BF16)
            kbd_scr[i] = _stack_heads(k_ref[rs, :], masks)
            dq_acc[rs, :] = jnp.zeros((tq, LANES), F32)
            drow[rs, :] = jnp.zeros((tq, hpb), F32)
            return 0

        lax.fori_loop(0, nq, prologue, 0)

        def kv_step(j, _):
            ks = pl.ds(pl.multiple_of(j * tq, tq), tq)
            kt, vt = k_ref[ks, :], v_ref[ks, :]
            kbd = kbd_scr[j]
            bias = jnp.concatenate(
                [jnp.broadcast_to(c_ref[pl.ds(crow0 + h * nq + j, 1), :], (tq, tq)) for h in range(hpb)], axis=0)

            def q_step(i, carry, diag):
                dk, dv, dcs = carry
                rs = pl.ds(pl.multiple_of(i * tq, tq), tq)
                qs, dos = qs_scr[i], dos_scr[i]
                s = _dot(qs, kt, NT) * scale - bias
                if diag:
                    s = jnp.where(_causal_stack(hpb, tq), s, NEG)
                lse = jnp.concatenate([l_ref[rs, h:h + 1] for h in range(hpb)], axis=0)
                p = jnp.exp(s - lse)
                dv = dv + _dot(p.astype(BF16), dos, TN)
                dp = _dot(dos, vt, NT)
                ds = p * (dp - jnp.concatenate([delta[rs, h:h + 1] for h in range(hpb)], axis=0))
                dcs = tuple(dcs[h] - jnp.sum(ds[h * tq:(h + 1) * tq], axis=0, keepdims=True) for h in range(hpb))
                rsum = jnp.sum(ds, axis=-1, keepdims=True)
                for h in range(hpb):
                    drow[rs, h:h + 1] += rsum[h * tq:(h + 1) * tq]
                dsb = (ds * scale).astype(BF16)
                dk = dk + _dot(dsb, qs, TN)
                dscat = jnp.concatenate([dsb[h * tq:(h + 1) * tq] for h in range(hpb)], axis=1)
                dq_acc[rs, :] += _dot(dscat, kbd, NN)
                return dk, dv, dcs

            init = (jnp.zeros((tq, LANES), F32), jnp.zeros((tq, LANES), F32),
                    tuple(jnp.zeros((1, tq), F32) for _ in range(hpb)))
            carry = q_step(j, init, True)
            dk, dv, dcs = lax.fori_loop(j + 1, nq, lambda i, c: q_step(i, c, False), carry)
            dqkv_ref[1, ks, :] = dk.astype(BF16)
            dqkv_ref[2, ks, :] = dv.astype(BF16)
            for h in range(hpb):
                dc_ref[pl.ds(crow0 + h * nq + j, 1), :] = dcs[h]
            return 0

        lax.fori_loop(0, nq, kv_step, 0)
        dqkv_ref[0] = dq_acc[...].astype(BF16)
        dr_ref[...] = drow[...]

    col = lambda off: pl.BlockSpec((T, LANES), functools.partial(lambda h, off: (0, off + h), off=off))
    return pl.pallas_call(
        body, name=name, grid=(nb,),
        in_specs=[col(0), col(nb), col(2 * nb), col(0), col(0),
                  pl.BlockSpec((None, T, hpb), lambda h: (h, 0, 0)),
                  pl.BlockSpec(cum_rows.shape, lambda h: (0, 0))],
        out_specs=[pl.BlockSpec((3, T, LANES), lambda h: (0, 0, h)),
                   pl.BlockSpec(cum_rows.shape, lambda h: (0, 0)),
                   pl.BlockSpec((None, T, hpb), lambda h: (h, 0, 0))],
        out_shape=(jax.ShapeDtypeStruct((3, T, A), BF16), jax.ShapeDtypeStruct(cum_rows.shape, F32),
                   jax.ShapeDtypeStruct((nb, T, hpb), F32)),
        scratch_shapes=[pltpu.VMEM((T, LANES), F32), pltpu.VMEM((T, hpb), F32), pltpu.VMEM((T, hpb), F32),
                        pltpu.VMEM((nq, hpb * tq, LANES), BF16), pltpu.VMEM((nq, hpb * tq, LANES), BF16),
                        pltpu.VMEM((nq, hpb * tq, LANES), BF16)],
        compiler_params=_params(("arbitrary",)),
    )(qkv, qkv, qkv, dcat, o, lse, cum_rows)


def _glu_into(upad, cv_ref, cg_ref, T):
    upad[0:CONV_PAD, :] = jnp.zeros((CONV_PAD, upad.shape[1]), F32)

    def fill(c, _):
        rs = pl.ds(pl.multiple_of(c * CONV_CHUNK, CONV_CHUNK), CONV_CHUNK)
        upad[pl.ds(pl.multiple_of(CONV_PAD + c * CONV_CHUNK, 8), CONV_CHUNK), :] = cv_ref[rs, :] * _sigmoid(cg_ref[rs, :])
        return 0

    lax.fori_loop(0, T // CONV_CHUNK, fill, 0)


SUBLANES = 8
CONV_SHIFT_ROWS = CONV_CHUNK + CONV_PAD - SUBLANES


def _load_window(win, sh, src, r0):
    win[...] = src[pl.ds(r0, CONV_CHUNK + CONV_PAD), :]
    for b in range(1, SUBLANES):
        sh[b - 1] = win[b:b + CONV_SHIFT_ROWS, :]


def _tap(win, sh, o):
    b = o % SUBLANES
    if b == 0:
        return win[o:o + CONV_CHUNK, :]
    return sh[b - 1, o - b:o - b + CONV_CHUNK, :]


def _conv_taps(win, sh, w_ref, first, step):
    acc = None
    for k in range(CONV_K):
        t = w_ref[k:k + 1, :] * _tap(win, sh, first + step * k)
        acc = t if acc is None else acc + t
    return acc


def _conv_scratch(C):
    return [pltpu.VMEM((CONV_CHUNK + CONV_PAD, C), F32), pltpu.VMEM((SUBLANES - 1, CONV_SHIFT_ROWS, C), F32)]


def _conv_fwd(cproj, w, b, lg, lb, name):
    T = cproj.shape[0]
    C = cproj.shape[1] // 2
    off = CONV_PAD - (CONV_K - 1)

    def body(cv_ref, cg_ref, w_ref, b_ref, lg_ref, lb_ref, out_ref, u1_ref, upad, win, sh):
        _glu_into(upad, cv_ref, cg_ref, T)

        def chunk(c, _):
            r0 = pl.multiple_of(c * CONV_CHUNK, CONV_CHUNK)
            _load_window(win, sh, upad, r0)
            u1 = _conv_taps(win, sh, w_ref, off, 1) + b_ref[...]
            u1_ref[pl.ds(r0, CONV_CHUNK), :] = u1
            mu = jnp.mean(u1, axis=-1, keepdims=True)
            var = jnp.mean(jnp.square(u1 - mu), axis=-1, keepdims=True)
            u2 = ((u1 - mu) * lax.rsqrt(var + EPS)) * lg_ref[...] + lb_ref[...]
            out_ref[pl.ds(r0, CONV_CHUNK), :] = (u2 * _sigmoid(u2)).astype(BF16)
            return 0

        lax.fori_loop(0, T // CONV_CHUNK, chunk, 0)

    row = pl.BlockSpec((1, C), lambda i: (0, 0))
    return pl.pallas_call(
        body, name=name, grid=(1,),
        in_specs=[pl.BlockSpec((T, C), lambda i: (0, 0)), pl.BlockSpec((T, C), lambda i: (0, 1)),
                  pl.BlockSpec(w.shape, lambda i: (0, 0)), row, row, row],
        out_specs=[pl.BlockSpec((T, C), lambda i: (0, 0)), pl.BlockSpec((T, C), lambda i: (0, 0))],
        out_shape=(jax.ShapeDtypeStruct((T, C), BF16), jax.ShapeDtypeStruct((T, C), F32)),
        scratch_shapes=[pltpu.VMEM((T + CONV_PAD, C), F32)] + _conv_scratch(C),
        compiler_params=_params(("arbitrary",)),
    )(cproj, cproj, w, b, lg, lb)


def _conv_bwd(cproj, u1_saved, dcat, w, lg, lb, name):
    T = cproj.shape[0]
    C = cproj.shape[1] // 2
    off = CONV_PAD - (CONV_K - 1)
    n_chunks = T // CONV_CHUNK

    def fold(v):
        return jnp.sum(v.reshape(CONV_CHUNK // 8, 8, C), axis=0)

    def body(cv_ref, cg_ref, u1_ref, du_ref, w_ref, lg_ref, lb_ref, dc_ref, dw_ref, sm_ref,
             upad, dpad, dwacc, smacc, win, sh):
        _glu_into(upad, cv_ref, cg_ref, T)
        dpad[pl.ds(T, CONV_PAD), :] = jnp.zeros((CONV_PAD, C), F32)
        dwacc[...] = jnp.zeros_like(dwacc)
        smacc[...] = jnp.zeros_like(smacc)

        def chunk_a(c, _):
            r0 = pl.multiple_of(c * CONV_CHUNK, CONV_CHUNK)
            _load_window(win, sh, upad, r0)
            u1 = u1_ref[pl.ds(r0, CONV_CHUNK), :]
            mu = jnp.mean(u1, axis=-1, keepdims=True)
            var = jnp.mean(jnp.square(u1 - mu), axis=-1, keepdims=True)
            rstd = lax.rsqrt(var + EPS)
            u1h = (u1 - mu) * rstd
            u2 = u1h * lg_ref[...] + lb_ref[...]
            sg = _sigmoid(u2)
            du2 = du_ref[pl.ds(r0, CONV_CHUNK), :] * (sg * (1.0 + u2 * (1.0 - sg)))
            smacc[8:16, :] += fold(du2 * u1h)
            smacc[16:24, :] += fold(du2)
            du1h = du2 * lg_ref[...]
            du1 = rstd * (du1h - jnp.mean(du1h, axis=-1, keepdims=True)
                          - u1h * jnp.mean(du1h * u1h, axis=-1, keepdims=True))
            smacc[0:8, :] += fold(du1)
            dpad[pl.ds(r0, CONV_CHUNK), :] = du1
            for k in range(CONV_K):
                dwacc[8 * k:8 * k + 8, :] += fold(du1 * _tap(win, sh, off + k))
            return 0

        lax.fori_loop(0, n_chunks, chunk_a, 0)

        def chunk_b(c, _):
            r0 = pl.multiple_of(c * CONV_CHUNK, CONV_CHUNK)
            rs = pl.ds(r0, CONV_CHUNK)
            _load_window(win, sh, dpad, r0)
            du0 = _conv_taps(win, sh, w_ref, CONV_K - 1, -1)
            cv, sg = cv_ref[rs, :], _sigmoid(cg_ref[rs, :])
            dc_ref[rs, 0:C] = (du0 * sg).astype(BF16)
            dc_ref[rs, C:2 * C] = (du0 * cv * (sg * (1.0 - sg))).astype(BF16)
            return 0

        lax.fori_loop(0, n_chunks, chunk_b, 0)
        for k in range(CONV_K):
            dw_ref[k:k + 1, :] = jnp.sum(dwacc[8 * k:8 * k + 8, :], axis=0, keepdims=True)
        dw_ref[CONV_K:CONV_PAD, :] = jnp.zeros((CONV_PAD - CONV_K, C), F32)
        for r in range(3):
            sm_ref[r:r + 1, :] = jnp.sum(smacc[8 * r:8 * r + 8, :], axis=0, keepdims=True)
        sm_ref[3:8, :] = jnp.zeros((5, C), F32)

    row = pl.BlockSpec((1, C), lambda i: (0, 0))
    return pl.pallas_call(
        body, name=name, grid=(1,),
        in_specs=[pl.BlockSpec((T, C), lambda i: (0, 0)), pl.BlockSpec((T, C), lambda i: (0, 1)),
                  pl.BlockSpec((T, C), lambda i: (0, 0)), pl.BlockSpec((T, C), lambda i: (0, 1)),
                  pl.BlockSpec(w.shape, lambda i: (0, 0)), row, row],
        out_specs=[pl.BlockSpec((T, 2 * C), lambda i: (0, 0)), pl.BlockSpec((CONV_PAD, C), lambda i: (0, 0)),
                   pl.BlockSpec((8, C), lambda i: (0, 0))],
        out_shape=(jax.ShapeDtypeStruct((T, 2 * C), BF16), jax.ShapeDtypeStruct((CONV_PAD, C), F32),
                   jax.ShapeDtypeStruct((8, C), F32)),
        scratch_shapes=[pltpu.VMEM((T + CONV_PAD, C), F32), pltpu.VMEM((T + CONV_PAD, C), F32),
                        pltpu.VMEM((8 * CONV_PAD, C), F32), pltpu.VMEM((24, C), F32)] + _conv_scratch(C),
        compiler_params=_params(("arbitrary",)),
    )(cproj, cproj, u1_saved, dcat, w, lg, lb)


def _loss_head(x, target, name):
    T, D = x.shape
    tm = _tile(T, 512, 8)

    def body(x_ref, t_ref, loss_ref, dx_ref):
        @pl.when(pl.program_id(0) == 0)
        def _():
            loss_ref[...] = jnp.zeros_like(loss_ref)

        err = x_ref[...] - t_ref[...]
        part = jnp.sum(jnp.mean(err * err, axis=-1, keepdims=True), axis=0, keepdims=True)
        loss_ref[...] += jnp.broadcast_to(0.5 * part, loss_ref.shape)
        dx_ref[...] = err * (1.0 / D)

    big = pl.BlockSpec((tm, D), lambda i: (i, 0))
    return pl.pallas_call(
        body, name=name, grid=(T // tm,),
        in_specs=[big, big],
        out_specs=[pl.BlockSpec((8, LANES), lambda i: (0, 0)), big],
        out_shape=(jax.ShapeDtypeStruct((8, LANES), F32), jax.ShapeDtypeStruct((T, D), F32)),
        compiler_params=_params(("arbitrary",)),
    )(x, target)


def _ada_fwd(c_all, ada_w, ada_b_loc, name):
    L, D, S = ada_w.shape
    B = c_all.shape[0]

    def body(c_ref, w_ref, b_ref, o_ref):
        c = c_ref[...]
        ca = (c * _sigmoid(c)).astype(BF16)
        o_ref[...] = _dot(ca, w_ref[...].astype(BF16), NN) + b_ref[...]

    return pl.pallas_call(
        body, name=name, grid=(L,),
        in_specs=[pl.BlockSpec((B, D), lambda l: (0, 0)), pl.BlockSpec((None, D, S), lambda l: (l, 0, 0)),
                  pl.BlockSpec((None, 1, S), lambda l: (l, 0, 0))],
        out_specs=pl.BlockSpec((None, B, S), lambda l: (l, 0, 0)),
        out_shape=jax.ShapeDtypeStruct((L, B, S), F32),
        compiler_params=_params(("parallel",)),
    )(c_all, ada_w, ada_b_loc)


def _ada_bwd(c_all_t, dmod_loc, name):
    D, B = c_all_t.shape
    L, _, S = dmod_loc.shape

    def body(c_ref, dm_ref, o_ref):
        c = c_ref[...]
        ca = c * _sigmoid(c)
        acc = None
        for bb in range(B):
            t = ca[:, bb:bb + 1] * dm_ref[bb:bb + 1, :]
            acc = t if acc is None else acc + t
        o_ref[...] = acc

    return pl.pallas_call(
        body, name=name, grid=(L,),
        in_specs=[pl.BlockSpec((D, B), lambda l: (0, 0)), pl.BlockSpec((None, B, S), lambda l: (l, 0, 0))],
        out_specs=pl.BlockSpec((None, D, S), lambda l: (l, 0, 0)),
        out_shape=jax.ShapeDtypeStruct((L, D, S), F32),
        compiler_params=_params(("parallel",)),
    )(c_all_t, dmod_loc)


def _sum_devices(parts, name):
    _, R, C = parts.shape
    tm = _tile(R, 256, 8)

    def body(p_ref, o_ref):
        acc = p_ref[0].astype(F32)
        for d in range(1, N_DEV):
            acc = acc + p_ref[d].astype(F32)
        o_ref[...] = acc

    return pl.pallas_call(
        body, name=name, grid=(R // tm,),
        in_specs=[pl.BlockSpec((N_DEV, tm, C), lambda i: (0, i, 0))],
        out_specs=pl.BlockSpec((tm, C), lambda i: (i, 0)),
        out_shape=jax.ShapeDtypeStruct((R, C), F32),
        compiler_params=_params(("parallel",)),
    )(parts)


def _adamw_math(w, g, m, v):
    m = ADAM_B1 * m + (1.0 - ADAM_B1) * g
    v = ADAM_B2 * v + (1.0 - ADAM_B2) * (g * g)
    m_hat = m / (1.0 - ADAM_B1 ** ADAM_STEP)
    v_hat = v / (1.0 - ADAM_B2 ** ADAM_STEP)
    delta = -ADAM_LR * (m_hat / (jnp.sqrt(v_hat) + ADAM_EPS) + ADAM_WD * w)
    return delta, m, v


def _adamw(w, g, m, v, name, summed):
    R, C = w.shape
    tm = _tile(R, 256, 16)
    n_parts = g.shape[0] if summed else 0

    def body(w_ref, g_ref, m_ref, v_ref, go_ref, d_ref, mo_ref, vo_ref):
        if summed:
            g = g_ref[0].astype(F32)
            for d in range(1, n_parts):
                g = g + g_ref[d].astype(F32)
        else:
            g = g_ref[...]
        delta, mn, vn = _adamw_math(w_ref[...], g, m_ref[...], v_ref[...])
        go_ref[...] = g
        d_ref[...] = delta
        mo_ref[...] = mn
        vo_ref[...] = vn

    big = pl.BlockSpec((tm, C), lambda i: (i, 0))
    gspec = pl.BlockSpec((n_parts, tm, C), lambda i: (0, i, 0)) if summed else big
    return pl.pallas_call(
        body, name=name, grid=(R // tm,),
        in_specs=[big, gspec, big, big],
        out_specs=[big, big, big, big],
        out_shape=tuple(jax.ShapeDtypeStruct((R, C), F32) for _ in range(4)),
        compiler_params=_params(("parallel",)),
    )(w, g, m, v)


def _adamw_summed(w, parts, m, v, name):
    L = len(parts)
    n_parts, R, C = parts[0].shape
    tm = _tile(R, 128, 16)
    tc = C if tm < R else _tile(C, 256)
    nr = (R // tm) * (C // tc)
    ncb = C // tc

    def body(*refs):
        w_ref, g_refs = refs[0], refs[1:1 + L]
        m_ref, v_ref, go_ref, d_ref, mo_ref, vo_ref = refs[1 + L:]
        for ll in range(L):
            @pl.when(pl.program_id(0) == ll)
            def _(ll=ll):
                g = g_refs[ll][0].astype(F32)
                for d in range(1, n_parts):
                    g = g + g_refs[ll][d].astype(F32)
                delta, mn, vn = _adamw_math(w_ref[...], g, m_ref[...], v_ref[...])
                go_ref[...] = g
                d_ref[...] = delta
                mo_ref[...] = mn
                vo_ref[...] = vn

    big = pl.BlockSpec((None, tm, tc), lambda l, i: (l, i // ncb, i % ncb))

    def part_index(l, i, ll):
        i = jnp.where(l == ll, i, 0)
        return 0, i // ncb, i % ncb

    gspecs = [pl.BlockSpec((n_parts, tm, tc), functools.partial(part_index, ll=ll)) for ll in range(L)]
    return pl.pallas_call(
        body, name=name, grid=(L, nr),
        in_specs=[big, *gspecs, big, big],
        out_specs=[big, big, big, big],
        out_shape=tuple(jax.ShapeDtypeStruct((L, R, C), F32) for _ in range(4)),
        compiler_params=_params(("arbitrary", "arbitrary")),
    )(w, *parts, m, v)


def _pack(arrs, D):
    L = arrs[0].shape[0]
    cols = []
    for a in arrs:
        f = a.reshape(L, -1)
        n = f.shape[1]
        cols.append(jnp.pad(f, ((0, 0), (0, -(-n // D) * D - n))))
    flat = jnp.concatenate(cols, axis=1)
    return flat.reshape(L, flat.shape[1] // D, D)


def _unpack(p, shapes, D):
    L = p.shape[0]
    out, r = [], 0
    for s in shapes:
        n = math.prod(s[1:])
        rows = -(-n // D)
        out.append(p[:, r:r + rows].reshape(L, rows * D)[:, :n].reshape(s))
        r += rows
    return out


def kernel(x, c, w_in, b_f, conv_w, conv_b, conv_ln_g, conv_ln_b, w_o, w_ffn_in, w_ffn_out, mix_pre_g, mix_post_g, ffn_pre_g, ffn_post_g, ada_w, ada_b, loss_target, m_w_in, m_b_f, m_conv_w, m_conv_b, m_conv_ln_g, m_conv_ln_b, m_w_o, m_w_ffn_in, m_w_ffn_out, m_mix_pre_g, m_mix_post_g, m_ffn_pre_g, m_ffn_post_g, m_ada_w, m_ada_b, v_w_in, v_b_f, v_conv_w, v_conv_b, v_conv_ln_g, v_conv_ln_b, v_w_o, v_w_ffn_in, v_w_ffn_out, v_mix_pre_g, v_mix_post_g, v_ffn_pre_g, v_ffn_post_g, v_ada_w, v_ada_b):
    L, D, s_in = w_in.shape
    T = x.shape[1]
    H = b_f.shape[1]
    A = D // 2
    C = D - A
    cs = conv_w.shape[2]
    F = w_ffn_out.shape[1] * N_DEV
    s_ff = w_ffn_in.shape[2]
    w_ffn_in_t = jnp.transpose(w_ffn_in, (0, 2, 1))
    w_in_t = jnp.transpose(w_in, (0, 2, 1))
    s_ada = ada_w.shape[2]
    R = 16
    me = _my_index()
    x0 = x[0]
    target = loss_target[0]
    tq = _tile(T, 512)
    nq = T // tq

    c_all = _exchange(c, gather=True, name="gather_c").reshape(N_DEV, D)
    ada_b_loc = lax.dynamic_slice_in_dim(ada_b, me * s_ada, s_ada, axis=1)[:, None, :]
    mod_loc = _ada_fwd(c_all, ada_w, ada_b_loc, "ada_fwd")
    mod_g = _exchange(mod_loc, gather=True, name="gather_mod")
    mod = lax.dynamic_index_in_dim(mod_g, me, axis=2, keepdims=False)
    mod = jnp.transpose(mod, (1, 0, 2)).reshape(L, N_MOD, 1, D)
    cw_g = _exchange(conv_w, gather=True, name="gather_conv_w")
    conv_w_full = jnp.transpose(cw_g, (1, 2, 0, 3)).reshape(L, CONV_K, C)
    conv_w_pad = jnp.pad(conv_w_full, ((0, 0), (0, CONV_PAD - CONV_K), (0, 0)))
    b_f_col = jnp.pad(b_f, ((0, 0), (0, R - H)))[:, :, None]

    h_gmix, h_gffn, chain = [], [], (mod_g, cw_g)
    for l in range(L):
        h_gmix.append(_gather_start([w_in_t[l].astype(BF16), w_o[l].astype(BF16)], f"gather_mix_{l}", chain))
        h_gffn.append(_gather_start([w_ffn_in_t[l].astype(BF16), w_ffn_out[l].astype(BF16)], f"gather_ffn_{l}",
                                    (h_gmix[l]["token"],)))
        chain = (h_gffn[l]["token"],)
    gather_tokens = chain

    def by_rows(g):
        return g.reshape(N_DEV * g.shape[1], g.shape[2])

    W_in_t, W_f_t, W_c_t, W_o, W_ffn_in_t, W_ffn_out = ([None] * L for _ in range(6))

    saved = []
    xc = x0
    for l in range(L):
        sh1, sc1, g1, sh2, sc2, g2 = (mod[l, k] for k in range(N_MOD))
        gpre1, gpost1, gpre2, gpost2 = (p[l][None, :] for p in (mix_pre_g, mix_post_g, ffn_pre_g, ffn_post_g))
        if l == 0:
            h_gmix[0] = _gather_mid(h_gmix[0], gather_tokens[0], "gather_mid_mix_0")
        g_in, g_o = _gather_wait(h_gmix[l], gather_tokens[0] if l == 0 else xc, f"gather_wait_mix_{l}")
        W_in_t[l], W_o[l] = by_rows(g_in), by_rows(g_o)
        W_f_t[l] = jnp.pad(W_in_t[l][3 * A:3 * A + H], ((0, R - H), (0, 0)))
        W_c_t[l] = W_in_t[l][3 * A + H:]
        if l == 0:
            h1 = _prenorm(xc, gpre1, sc1, sh1, "prenorm1_0", first=gather_tokens)
        qkv = _mm([(h1, W_in_t[l])], "nt", BF16, f"proj_qkv_{l}", tm=MM_ROWS, n_cols=3 * A)
        cproj = _mm([(h1, W_c_t[l])], "nt", F32, f"proj_conv_{l}", tm=MM_ROWS, tn=MM_COLS)
        fl, cum = _fgate_fwd(h1, W_f_t[l], b_f_col[l], f"fgate_{l}")
        cum_rows = cum[:H].reshape(H * nq, tq)
        if l > 0:
            h_gffn[l] = _gather_mid(h_gffn[l], qkv, f"gather_mid_ffn_{l}")
            o, lse = _attn_fwd(qkv, cum_rows, H, f"attn_{l}", tq, first=(h_gffn[l]["token"],))
        else:
            o, lse = _attn_fwd(qkv, cum_rows, H, "attn_0", tq)
            h_gffn[0] = _gather_mid(h_gffn[0], o, "gather_mid_ffn_0")
        u3, u1c = _conv_fwd(cproj, conv_w_pad[l], conv_b[l][None, :], conv_ln_g[l][None, :], conv_ln_b[l][None, :],
                            f"conv_{l}")
        cat = jnp.concatenate([o, u3], axis=-1)
        y1, x_mid, h2 = _mm_postnorm(cat, W_o[l], xc, g1, gpost1, f"out_proj_{l}", nxt=(gpre2, sc2, sh2))
        g_fi, g_fo = _gather_wait(h_gffn[l], x_mid, f"gather_wait_ffn_{l}")
        W_ffn_in_t[l], W_ffn_out[l] = by_rows(g_fi), by_rows(g_fo)
        g, u, act = _ffn_in_fwd(h2, W_ffn_in_t[l], f"ffn_in_{l}")
        if l + 1 < L:
            h_gmix[l + 1] = _gather_mid(h_gmix[l + 1], act, f"gather_mid_mix_{l + 1}")
        saved_l = (xc, h1, qkv, cproj, u1c, fl, cum_rows, o, lse, cat, y1, x_mid, h2, g, u, act)
        if l + 1 < L:
            nxt = (mix_pre_g[l + 1][None, :], mod[l + 1, 1], mod[l + 1, 0])
            y2, xc, h1 = _mm_postnorm(act, W_ffn_out[l], x_mid, g2, gpost2, f"ffn_out_{l}", nxt=nxt,
                                      first=(h_gmix[l + 1]["token"],))
        else:
            y2, xc = _mm_postnorm(act, W_ffn_out[l], x_mid, g2, gpost2, f"ffn_out_{l}")
        saved.append(saved_l + (y2,))

    loss_tile, dx = _loss_head(xc, target, "loss_head")
    loss = lax.psum(loss_tile[0, 0], ("x", "y", "c"))

    def reduce_start(parts, name):
        return _scatter_start(parts, "scatter_" + name)

    small, h_ffn, h_mix = [None] * L, [None] * L, [None] * L
    for l in reversed(range(L)):
        xin, h1, qkv, cproj, u1c, fl, cum_rows, o, lse, cat, y1, x_mid, h2, g, u, act, y2 = saved[l]
        sh1, sc1, g1, sh2, sc2, g2 = (mod[l, k] for k in range(N_MOD))
        gpre1, gpost1, gpre2, gpost2 = (p[l][None, :] for p in (mix_pre_g, mix_post_g, ffn_pre_g, ffn_post_g))
        dy2, sm_post2 = _postnorm_bwd(dx, y2, g2, gpost2, f"postnorm2_bwd_{l}")
        dgate, dup = _ffn_act_bwd(dy2, W_ffn_out[l], g, u, f"ffn_act_bwd_{l}")
        dW_ffn_out = _mm([(act, dy2)], "tn", BF16, f"dw_ffn_out_{l}", tm=DW_TILE)
        dWg_t = _mm([(dgate, h2)], "tn", BF16, f"dw_ffn_gate_{l}", tm=DW_TILE)
        dWu_t = _mm([(dup, h2)], "tn", BF16, f"dw_ffn_up_{l}", tm=DW_TILE)
        half = N_DEV // 2
        dW_ffn_in_t = jnp.concatenate([dWg_t.reshape(half, s_ff, D), dWu_t.reshape(half, s_ff, D)], axis=0)
        h_ffn[l], token = reduce_start([dW_ffn_in_t, dW_ffn_out.reshape(N_DEV, F // N_DEV, D)], f"ffn_{l}")
        dx_mid, sm_pre2 = _prenorm_bwd([(dgate, W_ffn_in_t[l], 0), (dup, W_ffn_in_t[l], 1)], x_mid, dx, gpre2, sc2,
                                       f"prenorm2_bwd_{l}", first=(token,))
        dy1, sm_post1 = _postnorm_bwd(dx_mid, y1, g1, gpost1, f"postnorm1_bwd_{l}")
        dcat = _mm([(dy1, W_o[l])], "nt", F32, f"dcat_{l}", tm=MM_ROWS, tn=MM_COLS)
        dW_o = _mm([(cat, dy1)], "tn", BF16, f"dw_o_{l}", tm=DW_TILE)
        dqkv, dcum_rows, dcum_q = _attn_bwd(qkv, dcat, o, lse, cum_rows, H, f"attn_bwd_{l}", tq)
        dcproj, dconv_w, sm_conv = _conv_bwd(cproj, u1c, dcat, conv_w_pad[l], conv_ln_g[l][None, :],
                                             conv_ln_b[l][None, :], f"conv_bwd_{l}")
        dcum = dcum_rows.reshape(H, T) + jnp.transpose(dcum_q, (0, 2, 1)).reshape(H, T)
        dcum = jnp.pad(dcum, ((0, R - H), (0, 0)))
        dfl_t, dwf_t, dbf = _fgate_bwd(dcum, fl, h1, f"fgate_bwd_{l}")
        dfl = jnp.transpose(dfl_t).astype(BF16)
        dh1_pairs = [(dqkv[0], W_in_t[l], 0), (dqkv[1], W_in_t[l], 1), (dqkv[2], W_in_t[l], 2),
                     (dfl, W_f_t[l]), (dcproj, W_c_t[l])]
        dWq_t = _dw_stack(dqkv, h1, f"dw_qkv_{l}")
        dWc_t = _mm([(dcproj, h1)], "tn", F32, f"dw_conv_{l}", tm=DW_TILE)
        dW_in_t = jnp.concatenate([dWq_t, dwf_t[:H], dWc_t], axis=0).astype(BF16)
        h_mix[l], token = reduce_start([dW_in_t.reshape(N_DEV, s_in, D), dW_o.reshape(N_DEV, D // N_DEV, D)], f"mix_{l}")
        dx, sm_pre1 = _prenorm_bwd(dh1_pairs, xin, dx_mid, gpre1, sc1, f"prenorm1_bwd_{l}", first=(token,))
        dmod = jnp.stack([sm_pre1[0], sm_pre1[1], sm_post1[0], sm_pre2[0], sm_pre2[1], sm_post2[0]])
        small[l] = (dmod, sm_pre1[2], sm_post1[1], sm_pre2[2], sm_post2[1], sm_conv[0], sm_conv[1], sm_conv[2],
                    dbf[:H, 0], dconv_w[:CONV_K])
    grad_x = dx[None]

    small_names = 10
    small_l = [jnp.stack([small[l][k] for l in range(L)]) for k in range(small_names)]
    small_shapes = [a.shape for a in small_l]
    packed = _pack(small_l, D)
    rows = packed.shape[1]
    rows_pad = -(-L * rows // 8) * 8
    packed2 = jnp.pad(packed.reshape(L * rows, D), ((0, rows_pad - L * rows), (0, 0)))
    h_small = _gather_start([packed2], "gather_small")

    got_ffn = [_scatter_wait(h_ffn[l], h_small["token"], f"scatter_wait_ffn_{l}") for l in range(L)]

    def step(w, m, v, parts, name):
        return list(_adamw_summed(w, parts, m, v, "adamw_" + name))

    swap = lambda t: jnp.transpose(t, (0, 2, 1))
    r_w_ffn_in = [swap(t) for t in step(w_ffn_in_t, swap(m_w_ffn_in), swap(v_w_ffn_in),
                                        [got_ffn[l][0] for l in range(L)], "w_ffn_in")]
    r_w_ffn_out = step(w_ffn_out, m_w_ffn_out, v_w_ffn_out, [got_ffn[l][1] for l in range(L)], "w_ffn_out")
    got_mix = [_scatter_wait(h_mix[l], r_w_ffn_out[1], f"scatter_wait_mix_{l}") for l in range(L)]
    r_w_in = [swap(t) for t in step(w_in_t, swap(m_w_in), swap(v_w_in), [got_mix[l][0] for l in range(L)], "w_in")]
    r_w_o = step(w_o, m_w_o, v_w_o, [got_mix[l][1] for l in range(L)], "w_o")

    small_g = _gather_wait(_gather_mid(h_small, r_w_o[1], "gather_small_mid"), r_w_o[1], "gather_small_wait")[0]
    small_sum = _sum_devices(small_g, "sum_small")[:L * rows].reshape(L, rows, D)
    (g_ada_b6, g_mix_pre, g_mix_post, g_ffn_pre, g_ffn_post, g_conv_b, g_ln_g, g_ln_b, g_b_f,
     g_conv_w_full) = _unpack(small_sum, small_shapes, D)
    g_ada_b = g_ada_b6.reshape(L, N_MOD * D)
    g_conv_w = lax.dynamic_slice_in_dim(g_conv_w_full, me * cs, cs, axis=2)
    dmod_all = small_g[:, :L * rows].reshape(N_DEV, L, rows, D)[:, :, :N_MOD].reshape(N_DEV, L, N_MOD * D)
    dmod_loc = jnp.transpose(lax.dynamic_slice_in_dim(dmod_all, me * s_ada, s_ada, axis=2), (1, 0, 2))
    g_ada_w = _ada_bwd(jnp.transpose(c_all), dmod_loc, "ada_bwd")
    r_ada_w = [t.reshape(ada_w.shape) for t in _adamw(
        ada_w.reshape(L * D, s_ada), g_ada_w.reshape(L * D, s_ada), m_ada_w.reshape(L * D, s_ada),
        v_ada_w.reshape(L * D, s_ada), "adamw_ada_w", False)]

    sw = [b_f, conv_w, conv_b, conv_ln_g, conv_ln_b, mix_pre_g, mix_post_g, ffn_pre_g, ffn_post_g, ada_b]
    sg = [g_b_f, g_conv_w, g_conv_b, g_ln_g, g_ln_b, g_mix_pre, g_mix_post, g_ffn_pre, g_ffn_post, g_ada_b]
    sm = [m_b_f, m_conv_w, m_conv_b, m_conv_ln_g, m_conv_ln_b, m_mix_pre_g, m_mix_post_g, m_ffn_pre_g, m_ffn_post_g, m_ada_b]
    sv = [v_b_f, v_conv_w, v_conv_b, v_conv_ln_g, v_conv_ln_b, v_mix_pre_g, v_mix_post_g, v_ffn_pre_g, v_ffn_post_g, v_ada_b]
    shapes = [a.shape for a in sw]

    def flat(arrs):
        p = _pack(arrs, D)
        n = p.shape[0] * p.shape[1]
        return jnp.pad(p.reshape(n, D), ((0, -(-n // 8) * 8 - n), (0, 0))), p.shape

    pw, pshape = flat(sw)
    pg, pm, pv = flat(sg)[0], flat(sm)[0], flat(sv)[0]
    s_outs = _adamw(pw, pg, pm, pv, "adamw_small", False)
    n_small = pshape[0] * pshape[1]
    s_g, s_d, s_m, s_v = (_unpack(t[:n_small].reshape(pshape), shapes, D) for t in s_outs)

    big = {"w_in": r_w_in, "w_o": r_w_o, "w_ffn_in": r_w_ffn_in, "w_ffn_out": r_w_ffn_out, "ada_w": r_ada_w}
    order = ["w_in", "b_f", "conv_w", "conv_b", "conv_ln_g", "conv_ln_b", "w_o", "w_ffn_in", "w_ffn_out",
             "mix_pre_g", "mix_post_g", "ffn_pre_g", "ffn_post_g", "ada_w", "ada_b"]
    small_pos = {n: i for i, n in enumerate(["b_f", "conv_w", "conv_b", "conv_ln_g", "conv_ln_b", "mix_pre_g",
                                             "mix_post_g", "ffn_pre_g", "ffn_post_g", "ada_b"])}

    def pick(n, k):
        if n in big:
            return big[n][k]
        return (s_g, s_d, s_m, s_v)[k][small_pos[n]]

    return (loss, grad_x, *[pick(n, 0) for n in order], *[pick(n, 1) for n in order],
            *[pick(n, 2) for n in order], *[pick(n, 3) for n in order])
```

```python
import functools
import math

import jax
import jax.numpy as jnp
from jax import lax
from jax.experimental import pallas as pl
from jax.experimental.pallas import tpu as pltpu

F32 = jnp.float32
BF16 = jnp.bfloat16
MESH = pl.DeviceIdType.MESH
N_DEV = 8
EPS = 1e-6
CONV_K = 31
CONV_PAD = 32
CONV_CHUNK = 128
N_MOD = 6
NEG = -1e30
LANES = 128
VMEM_LIMIT = 56 * 2**20
MM_TILE = 1024
MM_ROWS = 2048
MM_COLS = 512
MM_ROWS_SMALL = 512
FFN_COLS = 256
DW_TILE = 1408
ADAM_LR, ADAM_B1, ADAM_B2, ADAM_EPS, ADAM_WD, ADAM_STEP = 0.001, 0.9, 0.999, 1e-08, 0.01, 10

NN = (((1,), (0,)), ((), ()))
NT = (((1,), (1,)), ((), ()))
TN = (((0,), (0,)), ((), ()))


def _dot(a, b, dims):
    return lax.dot_general(a, b, dims, preferred_element_type=F32)


def _tile(n, pref, align=LANES):
    if n <= pref:
        return n
    t = (pref // align) * align
    while t >= align:
        if n % t == 0:
            return t
        t -= align
    return n


def _params(sem=None):
    return pltpu.CompilerParams(dimension_semantics=sem, vmem_limit_bytes=VMEM_LIMIT)


def _sigmoid(x):
    return 1.0 / (1.0 + jnp.exp(-x))


def _my_index():
    return 4 * lax.axis_index("x") + 2 * lax.axis_index("y") + lax.axis_index("c")


def _exchange(x, *, gather, name):
    blk = x.shape if gather else x.shape[1:]

    def body(x_ref, y_ref, send_sems, recv_sems, local_sem):
        mx, my, mc = lax.axis_index("x"), lax.axis_index("y"), lax.axis_index("c")
        me = 4 * mx + 2 * my + mc

        def src(p):
            return x_ref if gather else x_ref.at[p]

        mine = pltpu.make_async_copy(src(me), y_ref.at[me], local_sem)
        mine.start()
        copies = []
        for k in range(1, N_DEV):
            px = (1 - mx) if (k >> 2) & 1 else mx
            py = (1 - my) if (k >> 1) & 1 else my
            pc = (1 - mc) if k & 1 else mc
            cp = pltpu.make_async_remote_copy(
                src_ref=src(4 * px + 2 * py + pc), dst_ref=y_ref.at[me],
                send_sem=send_sems.at[k - 1], recv_sem=recv_sems.at[k - 1],
                device_id=(px, py, pc), device_id_type=MESH)
            cp.start()
            copies.append(cp)
        for cp in copies:
            cp.wait()
        mine.wait()

    return pl.pallas_call(
        body, name=name,
        out_shape=jax.ShapeDtypeStruct((N_DEV,) + tuple(blk), x.dtype),
        in_specs=[pl.BlockSpec(memory_space=pl.ANY)],
        out_specs=pl.BlockSpec(memory_space=pl.ANY),
        scratch_shapes=[pltpu.SemaphoreType.DMA((N_DEV - 1,)), pltpu.SemaphoreType.DMA((N_DEV - 1,)),
                        pltpu.SemaphoreType.DMA(())],
    )(x)


_HBM = pl.BlockSpec(memory_space=pl.ANY)


_SEM = pl.BlockSpec(memory_space=pltpu.SEMAPHORE)
_HBM_SPEC = pl.BlockSpec(memory_space=pltpu.HBM)
_EFFECT = pltpu.CompilerParams(has_side_effects=pltpu.SideEffectType.DATAFLOW_SIDE_EFFECTING)


def _in_hbm(a):
    return pltpu.with_memory_space_constraint(a, pltpu.HBM)


def _scatter_copies(x_refs, land_refs, send_sems, recv_sems, loc_sems):
    mx, my, mc = lax.axis_index("x"), lax.axis_index("y"), lax.axis_index("c")
    me = 4 * mx + 2 * my + mc
    local, remote = [], []
    for a, (x_ref, land_ref) in enumerate(zip(x_refs, land_refs)):
        local.append(pltpu.make_async_copy(x_ref.at[me], land_ref.at[me], loc_sems.at[a]))
        for k in range(1, N_DEV):
            px = (1 - mx) if (k >> 2) & 1 else mx
            py = (1 - my) if (k >> 1) & 1 else my
            pc = (1 - mc) if k & 1 else mc
            remote.append(pltpu.make_async_remote_copy(
                src_ref=x_ref.at[4 * px + 2 * py + pc], dst_ref=land_ref.at[me],
                send_sem=send_sems.at[(N_DEV - 1) * a + k - 1], recv_sem=recv_sems.at[(N_DEV - 1) * a + k - 1],
                device_id=(px, py, pc), device_id_type=MESH))
    return local, remote


def _scatter_start(xs, name):
    n = len(xs)

    def body(*refs):
        local, remote = _scatter_copies(refs[:n], refs[n:2 * n], *refs[2 * n:2 * n + 3])
        for cp in local + remote:
            cp.start()
        refs[-1][...] = jnp.zeros_like(refs[-1])

    lands = [lax.empty(x.shape, x.dtype) for x in xs]
    n_sems = (N_DEV - 1) * n
    outs = pl.pallas_call(
        body, name=name,
        out_shape=(pltpu.SemaphoreType.DMA((n_sems,)), pltpu.SemaphoreType.DMA((n_sems,)), pltpu.SemaphoreType.DMA((n,)),
                   *[pltpu.HBM(x.shape, x.dtype) for x in xs], *[pltpu.HBM(x.shape, x.dtype) for x in xs],
                   jax.ShapeDtypeStruct((8, LANES), F32)),
        in_specs=[_HBM_SPEC] * (2 * n),
        out_specs=(_SEM, _SEM, _SEM, *[_HBM_SPEC] * (2 * n), pl.BlockSpec(memory_space=pltpu.VMEM)),
        input_output_aliases={i: 3 + i for i in range(2 * n)},
        compiler_params=_EFFECT,
    )(*[_in_hbm(x) for x in xs], *[_in_hbm(t) for t in lands])
    return (outs[0], outs[1], outs[2], list(outs[3:3 + n]), list(outs[3 + n:3 + 2 * n])), outs[-1]


def _scatter_wait(handle, after, name):
    send_sems, recv_sems, loc_sems, x_thru, land_thru = handle
    n = len(x_thru)

    def body(*refs):
        local, remote = _scatter_copies(refs[:n], refs[n:2 * n], *refs[2 * n:2 * n + 3])
        for cp in local:
            cp.wait()
        for cp in remote:
            cp.wait_send()
            cp.wait_recv()

    outs = pl.pallas_call(
        body, name=name,
        out_shape=[pltpu.HBM(x.shape, x.dtype) for x in x_thru + land_thru],
        in_specs=[_HBM_SPEC] * (2 * n) + [_SEM, _SEM, _SEM, _HBM],
        out_specs=[_HBM_SPEC] * (2 * n),
        input_output_aliases={i: i for i in range(2 * n)},
        compiler_params=_EFFECT,
    )(*x_thru, *land_thru, send_sems, recv_sems, loc_sems, after)
    return list(outs[n:])


def _gather_first_copies(x_refs, y_refs, send_sems, sib_recv, ici_recv, loc_sems):
    mx, my, mc = lax.axis_index("x"), lax.axis_index("y"), lax.axis_index("c")
    me = 4 * mx + 2 * my + mc
    local, remote = [], []
    for a, (x_ref, y_ref) in enumerate(zip(x_refs, y_refs)):
        local.append(pltpu.make_async_copy(x_ref, y_ref.at[me], loc_sems.at[a]))
        remote.append(pltpu.make_async_remote_copy(
            src_ref=x_ref, dst_ref=y_ref.at[me], send_sem=send_sems.at[4 * a], recv_sem=sib_recv.at[a],
            device_id=(mx, my, 1 - mc), device_id_type=MESH))
        for j, (px, py) in enumerate([(1 - mx, my), (mx, 1 - my), (1 - mx, 1 - my)]):
            remote.append(pltpu.make_async_remote_copy(
                src_ref=x_ref, dst_ref=y_ref.at[me], send_sem=send_sems.at[4 * a + 1 + j], recv_sem=ici_recv.at[3 * a + j],
                device_id=(px, py, mc), device_id_type=MESH))
    return local, remote


def _gather_forward_copies(y_refs, ici_recv, fwd_send, fwd_recv):
    mx, my, mc = lax.axis_index("x"), lax.axis_index("y"), lax.axis_index("c")
    pairs = []
    for a, y_ref in enumerate(y_refs):
        for j, (px, py) in enumerate([(1 - mx, my), (mx, 1 - my), (1 - mx, 1 - my)]):
            slot = y_ref.at[4 * px + 2 * py + mc]
            arrival = pltpu.make_async_remote_copy(
                src_ref=slot, dst_ref=slot, send_sem=fwd_send.at[3 * a + j], recv_sem=ici_recv.at[3 * a + j],
                device_id=(px, py, mc), device_id_type=MESH)
            onward = pltpu.make_async_remote_copy(
                src_ref=slot, dst_ref=slot, send_sem=fwd_send.at[3 * a + j], recv_sem=fwd_recv.at[3 * a + j],
                device_id=(mx, my, 1 - mc), device_id_type=MESH)
            pairs.append((arrival, onward))
    return pairs


def _gather_start(xs, name, after=()):
    n = len(xs)
    ni = 2 * n + len(after)

    def body(*refs):
        local, remote = _gather_first_copies(refs[:n], refs[n:2 * n], *refs[ni:ni + 4])
        for cp in local + remote:
            cp.start()
        refs[-1][...] = jnp.zeros_like(refs[-1])

    ys = [lax.empty((N_DEV,) + tuple(x.shape), x.dtype) for x in xs]
    dma = pltpu.SemaphoreType.DMA
    outs = pl.pallas_call(
        body, name=name,
        out_shape=(dma((4 * n,)), dma((n,)), dma((3 * n,)), dma((n,)),
                   *[pltpu.HBM(x.shape, x.dtype) for x in xs], *[pltpu.HBM(y.shape, y.dtype) for y in ys],
                   jax.ShapeDtypeStruct((8, LANES), F32)),
        in_specs=[_HBM_SPEC] * (2 * n) + [_HBM] * len(after),
        out_specs=(_SEM, _SEM, _SEM, _SEM, *[_HBM_SPEC] * (2 * n), pl.BlockSpec(memory_space=pltpu.VMEM)),
        input_output_aliases={i: 4 + i for i in range(2 * n)},
        compiler_params=_EFFECT,
    )(*[_in_hbm(x) for x in xs], *[_in_hbm(y) for y in ys], *after)
    return dict(send=outs[0], sib_recv=outs[1], ici_recv=outs[2], loc=outs[3], x=list(outs[4:4 + n]),
                y=list(outs[4 + n:4 + 2 * n]), token=outs[-1])


def _gather_mid(h, after, name):
    n = len(h["y"])

    def body(*refs):
        for arrival, onward in _gather_forward_copies(refs[:n], refs[n], refs[n + 2 + n], refs[n + 3 + n]):
            arrival.wait_recv()
            onward.start()
        refs[-1][...] = jnp.zeros_like(refs[-1])

    dma = pltpu.SemaphoreType.DMA
    outs = pl.pallas_call(
        body, name=name,
        out_shape=(*[pltpu.HBM(y.shape, y.dtype) for y in h["y"]], dma((3 * n,)), dma((3 * n,)),
                   jax.ShapeDtypeStruct((8, LANES), F32)),
        in_specs=[_HBM_SPEC] * n + [_SEM, _HBM],
        out_specs=(*[_HBM_SPEC] * n, _SEM, _SEM, pl.BlockSpec(memory_space=pltpu.VMEM)),
        input_output_aliases={i: i for i in range(n)},
        compiler_params=_EFFECT,
    )(*h["y"], h["ici_recv"], after)
    return dict(h, y=list(outs[:n]), fwd_send=outs[n], fwd_recv=outs[n + 1], token=outs[n + 2])


def _gather_wait(h, after, name):
    n = len(h["y"])

    def body(*refs):
        x_refs, y_refs = refs[:n], refs[n:2 * n]
        send, sib_recv, loc, fwd_send, fwd_recv = refs[2 * n:2 * n + 5]
        local, remote = _gather_first_copies(x_refs, y_refs, send, sib_recv, fwd_recv, loc)
        for cp in local:
            cp.wait()
        for k, cp in enumerate(remote):
            cp.wait_send()
            if k % 4 == 0:
                cp.wait_recv()
        for _, onward in _gather_forward_copies(y_refs, fwd_recv, fwd_send, fwd_recv):
            onward.wait_send()
            onward.wait_recv()

    outs = pl.pallas_call(
        body, name=name,
        out_shape=[pltpu.HBM(t.shape, t.dtype) for t in h["x"] + h["y"]],
        in_specs=[_HBM_SPEC] * (2 * n) + [_SEM] * 5 + [_HBM],
        out_specs=[_HBM_SPEC] * (2 * n),
        input_output_aliases={i: i for i in range(2 * n)},
        compiler_params=_EFFECT,
    )(*h["x"], *h["y"], h["send"], h["sib_recv"], h["loc"], h["fwd_send"], h["fwd_recv"], after)
    return list(outs[n:])


def _mm(pairs, mode, out_dtype, name, tm=MM_TILE, tn=MM_TILE, n_cols=None):
    dims = {"nn": NN, "nt": NT, "tn": TN}[mode]
    a0, b0 = pairs[0][0], pairs[0][1]
    M = a0.shape[1] if mode == "tn" else a0.shape[0]
    N = n_cols or (b0.shape[0] if mode == "nt" else b0.shape[1])
    tm, tn = _tile(M, tm), _tile(N, tn)
    in_specs, args = [], []
    for pr in pairs:
        a, b = pr[0], pr[1]
        if mode == "tn":
            K = a.shape[0]
            in_specs.append(pl.BlockSpec((K, tm), lambda i, j: (0, i)))
            in_specs.append(pl.BlockSpec((K, tn), lambda i, j: (0, j)))
        elif mode == "nn":
            K = a.shape[1]
            rb = pr[2] if len(pr) > 2 else 0
            in_specs.append(pl.BlockSpec((tm, K), lambda i, j: (i, 0)))
            in_specs.append(pl.BlockSpec((K, tn), functools.partial(lambda i, j, rb: (rb, j), rb=rb)))
        else:
            K = a.shape[1]
            cb = pr[2] if len(pr) > 2 else 0
            in_specs.append(pl.BlockSpec((tm, K), lambda i, j: (i, 0)))
            in_specs.append(pl.BlockSpec((tn, K), functools.partial(lambda i, j, cb: (j, cb), cb=cb)))
        args += [a, b]
    n_pairs = len(pairs)

    def body(*refs):
        o_ref = refs[-1]
        acc = None
        for k in range(n_pairs):
            d = _dot(refs[2 * k][...], refs[2 * k + 1][...], dims)
            acc = d if acc is None else acc + d
        o_ref[...] = acc.astype(o_ref.dtype)

    return pl.pallas_call(
        body, name=name, grid=(M // tm, N // tn), in_specs=in_specs,
        out_specs=pl.BlockSpec((tm, tn), lambda i, j: (i, j)),
        out_shape=jax.ShapeDtypeStruct((M, N), out_dtype),
        compiler_params=_params(("parallel", "arbitrary")),
    )(*args)


def _dw_stack(a3, b, name):
    n, T, A = a3.shape
    D = b.shape[1]

    def body(a_ref, b_ref, o_ref):
        o_ref[...] = _dot(a_ref[...], b_ref[...], TN)

    return pl.pallas_call(
        body, name=name, grid=(n,),
        in_specs=[pl.BlockSpec((None, T, A), lambda k: (k, 0, 0)), pl.BlockSpec((T, D), lambda k: (0, 0))],
        out_specs=pl.BlockSpec((A, D), lambda k: (k, 0)),
        out_shape=jax.ShapeDtypeStruct((n * A, D), F32),
        compiler_params=_params(("parallel",)),
    )(a3, b)


def _prenorm(x, g, sc, sh, name, first=()):
    T, D = x.shape
    tm = _tile(T, 512, 8)

    def body(x_ref, g_ref, sc_ref, sh_ref, *rest):
        h_ref = rest[-1]
        xv = x_ref[...]
        r = lax.rsqrt(jnp.mean(xv * xv, axis=-1, keepdims=True) + EPS)
        h_ref[...] = (((xv * r) * g_ref[...]) * (1.0 + sc_ref[...]) + sh_ref[...]).astype(BF16)

    row = pl.BlockSpec((1, D), lambda i: (0, 0))
    return pl.pallas_call(
        body, name=name, grid=(T // tm,),
        in_specs=[pl.BlockSpec((tm, D), lambda i: (i, 0)), row, row, row] + [_HBM] * len(first),
        out_specs=pl.BlockSpec((tm, D), lambda i: (i, 0)),
        out_shape=jax.ShapeDtypeStruct((T, D), BF16),
        compiler_params=_params(("parallel",)),
    )(x, g, sc, sh, *first)


def _prenorm_bwd(pairs, x, dres, g, sc, name, first=()):
    T, D = x.shape
    tm = _tile(T, 256, 8)
    n_pairs = len(pairs)
    pair_specs, pair_args = [], []
    for pr in pairs:
        a, b = pr[0], pr[1]
        K = a.shape[1]
        rb = pr[2] if len(pr) > 2 else 0
        pair_specs += [pl.BlockSpec((tm, K), lambda i: (i, 0)),
                       pl.BlockSpec((K, D), functools.partial(lambda i, rb: (rb, 0), rb=rb))]
        pair_args += [a, b]

    def body(*refs):
        x_ref, dres_ref, g_ref, sc_ref = refs[2 * n_pairs:2 * n_pairs + 4]
        dx_ref, sm_ref = refs[-2:]

        @pl.when(pl.program_id(0) == 0)
        def _():
            sm_ref[...] = jnp.zeros_like(sm_ref)

        dhv = None
        for k in range(n_pairs):
            d = _dot(refs[2 * k][...], refs[2 * k + 1][...], NN)
            dhv = d if dhv is None else dhv + d
        xv = x_ref[...]
        r = lax.rsqrt(jnp.mean(xv * xv, axis=-1, keepdims=True) + EPS)
        xh = xv * r
        one_sc = 1.0 + sc_ref[...]
        sm_ref[0:1, :] += jnp.sum(dhv, axis=0, keepdims=True)
        sm_ref[1:2, :] += jnp.sum(dhv * (xh * g_ref[...]), axis=0, keepdims=True)
        sm_ref[2:3, :] += jnp.sum(dhv * one_sc * xh, axis=0, keepdims=True)
        dxh = dhv * one_sc * g_ref[...]
        dx_ref[...] = dres_ref[...] + r * (dxh - xh * jnp.mean(dxh * xh, axis=-1, keepdims=True))

    row = pl.BlockSpec((1, D), lambda i: (0, 0))
    big = pl.BlockSpec((tm, D), lambda i: (i, 0))
    return pl.pallas_call(
        body, name=name, grid=(T // tm,),
        in_specs=pair_specs + [big, big, row, row] + [_HBM] * len(first),
        out_specs=[big, pl.BlockSpec((8, D), lambda i: (0, 0))],
        out_shape=(jax.ShapeDtypeStruct((T, D), F32), jax.ShapeDtypeStruct((8, D), F32)),
        compiler_params=_params(("arbitrary",)),
    )(*pair_args, x, dres, g, sc, *first)


def _mm_postnorm(a, w, x, gate, gpost, name, nxt=None, first=()):
    T, K = a.shape
    D = w.shape[1]
    tm = _tile(T, MM_ROWS_SMALL, 8)

    def body(a_ref, w_ref, x_ref, gate_ref, gp_ref, *rest):
        y = _dot(a_ref[...], w_ref[...], NN)
        r = lax.rsqrt(jnp.mean(y * y, axis=-1, keepdims=True) + EPS)
        xn = x_ref[...] + gate_ref[...] * ((y * r) * gp_ref[...])
        if nxt is None:
            y_ref, xn_ref = rest[-2:]
        else:
            g_ref, sc_ref, sh_ref = rest[:3]
            y_ref, xn_ref, h_ref = rest[-3:]
            rn = lax.rsqrt(jnp.mean(xn * xn, axis=-1, keepdims=True) + EPS)
            h_ref[...] = (((xn * rn) * g_ref[...]) * (1.0 + sc_ref[...]) + sh_ref[...]).astype(BF16)
        y_ref[...] = y
        xn_ref[...] = xn

    row = pl.BlockSpec((1, D), lambda i: (0, 0))
    big = pl.BlockSpec((tm, D), lambda i: (i, 0))
    extra = () if nxt is None else tuple(nxt)
    return pl.pallas_call(
        body, name=name, grid=(T // tm,),
        in_specs=[pl.BlockSpec((tm, K), lambda i: (i, 0)), pl.BlockSpec((K, D), lambda i: (0, 0)), big, row, row]
        + [row] * len(extra) + [_HBM] * len(first),
        out_specs=[big, big] + [big] * (nxt is not None),
        out_shape=(jax.ShapeDtypeStruct((T, D), F32), jax.ShapeDtypeStruct((T, D), F32))
        + ((jax.ShapeDtypeStruct((T, D), BF16),) if nxt is not None else ()),
        compiler_params=_params(("parallel",)),
    )(a, w, x, gate, gpost, *extra, *first)


def _postnorm_bwd(dx, y, gate, gpost, name):
    T, D = y.shape
    tm = _tile(T, 256, 8)

    def body(dx_ref, y_ref, gate_ref, gp_ref, dy_ref, sm_ref):
        @pl.when(pl.program_id(0) == 0)
        def _():
            sm_ref[...] = jnp.zeros_like(sm_ref)

        yv, dxv = y_ref[...], dx_ref[...]
        r = lax.rsqrt(jnp.mean(yv * yv, axis=-1, keepdims=True) + EPS)
        yh = yv * r
        dn = dxv * gate_ref[...]
        sm_ref[0:1, :] += jnp.sum(dxv * (yh * gp_ref[...]), axis=0, keepdims=True)
        sm_ref[1:2, :] += jnp.sum(dn * yh, axis=0, keepdims=True)
        dyh = dn * gp_ref[...]
        dy_ref[...] = (r * (dyh - yh * jnp.mean(dyh * yh, axis=-1, keepdims=True))).astype(BF16)

    row = pl.BlockSpec((1, D), lambda i: (0, 0))
    big = pl.BlockSpec((tm, D), lambda i: (i, 0))
    return pl.pallas_call(
        body, name=name, grid=(T // tm,),
        in_specs=[big, big, row, row],
        out_specs=[big, pl.BlockSpec((8, D), lambda i: (0, 0))],
        out_shape=(jax.ShapeDtypeStruct((T, D), BF16), jax.ShapeDtypeStruct((8, D), F32)),
        compiler_params=_params(("arbitrary",)),
    )(dx, y, gate, gpost)


def _ffn_in_fwd(h, w_t, name):
    T, D = h.shape
    F = w_t.shape[0] // 2
    tm, tn = _tile(T, MM_ROWS), _tile(F, FFN_COLS)
    nj = F // tn

    def body(h_ref, wg_ref, wu_ref, g_ref, u_ref, act_ref):
        hv = h_ref[...]
        g = _dot(hv, wg_ref[...], NT)
        u = _dot(hv, wu_ref[...], NT)
        g_ref[...] = g.astype(BF16)
        u_ref[...] = u.astype(BF16)
        act_ref[...] = ((g * _sigmoid(g)) * u).astype(BF16)

    out = pl.BlockSpec((tm, tn), lambda j, i: (i, j))
    return pl.pallas_call(
        body, name=name, grid=(nj, T // tm),
        in_specs=[pl.BlockSpec((tm, D), lambda j, i: (i, 0)),
                  pl.BlockSpec((tn, D), lambda j, i: (j, 0)),
                  pl.BlockSpec((tn, D), lambda j, i: (j + nj, 0))],
        out_specs=[out, out, out],
        out_shape=tuple(jax.ShapeDtypeStruct((T, F), BF16) for _ in range(3)),
        compiler_params=_params(("parallel", "arbitrary")),
    )(h, w_t, w_t)


def _ffn_act_bwd(dy, w_out, g, u, name):
    T, D = dy.shape
    F = w_out.shape[0]
    tm, tn = _tile(T, MM_ROWS), _tile(F, FFN_COLS)

    def body(dy_ref, w_ref, g_ref, u_ref, dg_ref, du_ref):
        dact = _dot(dy_ref[...], w_ref[...], NT)
        gv, uv = g_ref[...].astype(F32), u_ref[...].astype(F32)
        sg = _sigmoid(gv)
        dg_ref[...] = (dact * uv * (sg * (1.0 + gv * (1.0 - sg)))).astype(BF16)
        du_ref[...] = (dact * (gv * sg)).astype(BF16)

    tile = pl.BlockSpec((tm, tn), lambda j, i: (i, j))
    return pl.pallas_call(
        body, name=name, grid=(F // tn, T // tm),
        in_specs=[pl.BlockSpec((tm, D), lambda j, i: (i, 0)), pl.BlockSpec((tn, D), lambda j, i: (j, 0)), tile, tile],
        out_specs=[tile, tile],
        out_shape=(jax.ShapeDtypeStruct((T, F), BF16), jax.ShapeDtypeStruct((T, F), BF16)),
        compiler_params=_params(("parallel", "arbitrary")),
    )(dy, w_out, g, u)


def _lane_scan(v, reverse):
    T = v.shape[-1]
    lane = lax.broadcasted_iota(jnp.int32, v.shape, 1)
    d = 1
    while d < T:
        if reverse:
            v = v + jnp.where(lane < T - d, pltpu.roll(v, T - d, axis=1), 0.0)
        else:
            v = v + jnp.where(lane >= d, pltpu.roll(v, d, axis=1), 0.0)
        d *= 2
    return v


def _fgate_fwd(h, wf_t, bf, name):
    T, D = h.shape
    R = wf_t.shape[0]

    def body(h_ref, w_ref, b_ref, fl_ref, cum_ref):
        fl = _dot(w_ref[...], h_ref[...], NT) + b_ref[...]
        fl_ref[...] = fl
        logf = jnp.minimum(fl, 0.0) - jnp.log(1.0 + jnp.exp(-jnp.abs(fl)))
        cum_ref[...] = _lane_scan(logf, reverse=False)

    return pl.pallas_call(
        body, name=name,
        out_shape=(jax.ShapeDtypeStruct((R, T), F32), jax.ShapeDtypeStruct((R, T), F32)),
        compiler_params=_params(),
    )(h, wf_t, bf)


def _fgate_bwd(dcum, fl, h, name):
    R, T = fl.shape
    D = h.shape[1]

    def body(dc_ref, fl_ref, h_ref, dfl_ref, dw_ref, db_ref):
        dlogf = _lane_scan(dc_ref[...], reverse=True)
        dfl = dlogf * _sigmoid(-fl_ref[...])
        dfl_ref[...] = dfl
        dw_ref[...] = _dot(dfl.astype(BF16), h_ref[...], NN)
        db_ref[...] = jnp.broadcast_to(jnp.sum(dfl, axis=-1, keepdims=True), (R, LANES))

    return pl.pallas_call(
        body, name=name,
        out_shape=(jax.ShapeDtypeStruct((R, T), F32), jax.ShapeDtypeStruct((R, D), F32),
                   jax.ShapeDtypeStruct((R, LANES), F32)),
        compiler_params=_params(),
    )(dcum, fl, h)


def _head_masks(hpb, dh, rows):
    lane = lax.broadcasted_iota(jnp.int32, (rows, LANES), 1)
    return [(lane >= h * dh) & (lane < (h + 1) * dh) for h in range(hpb)]


def _stack_heads(v, masks):
    return jnp.concatenate([jnp.where(mk, v, jnp.zeros_like(v)) for mk in masks], axis=0)


def _heads_to_lanes(col, masks, tq):
    out = jnp.broadcast_to(col[0:tq], (tq, LANES))
    for h in range(1, len(masks)):
        out = jnp.where(masks[h], col[h * tq:(h + 1) * tq], out)
    return out


def _causal_stack(hpb, tq):
    r = lax.broadcasted_iota(jnp.int32, (tq, tq), 0)
    c = lax.broadcasted_iota(jnp.int32, (tq, tq), 1)
    return jnp.concatenate([c] * hpb, axis=0) <= jnp.concatenate([r] * hpb, axis=0)


def _attn_fwd(qkv, cum_rows, n_heads, name, tq, first=()):
    T = qkv.shape[0]
    A = qkv.shape[1] // 3
    dh = A // n_heads
    hpb = LANES // dh
    nb = A // LANES
    nq = T // tq
    scale = dh ** -0.5

    def body(q_ref, k_ref, v_ref, c_ref, *rest):
        o_ref, l_ref, vbd = rest[-3:]
        hp, i = pl.program_id(0), pl.program_id(1)
        masks = _head_masks(hpb, dh, tq)

        @pl.when(i == 0)
        def _():
            def fill(j, _):
                vbd[j] = _stack_heads(v_ref[pl.ds(pl.multiple_of(j * tq, tq), tq), :], masks)
                return 0

            lax.fori_loop(0, nq, fill, 0)

        qs = _stack_heads(q_ref[...], masks)
        crow0 = hp * hpb * nq

        def tile(j, carry, diag):
            m, l, acc = carry
            kt = k_ref[pl.ds(pl.multiple_of(j * tq, tq), tq), :]
            bias = jnp.concatenate(
                [jnp.broadcast_to(c_ref[pl.ds(crow0 + h * nq + j, 1), :], (tq, tq)) for h in range(hpb)], axis=0)
            s = _dot(qs, kt, NT) * scale - bias
            if diag:
                s = jnp.where(_causal_stack(hpb, tq), s, NEG)
            m_new = jnp.maximum(m, jnp.max(s, axis=-1, keepdims=True))
            p = jnp.exp(s - m_new)
            alpha = jnp.exp(m - m_new)
            l = alpha * l + jnp.sum(p, axis=-1, keepdims=True)
            pcat = jnp.concatenate([p[h * tq:(h + 1) * tq] for h in range(hpb)], axis=1).astype(BF16)
            acc = _heads_to_lanes(alpha, masks, tq) * acc + _dot(pcat, vbd[j], NN)
            return m_new, l, acc

        init = (jnp.full((hpb * tq, 1), NEG, F32), jnp.zeros((hpb * tq, 1), F32), jnp.zeros((tq, LANES), F32))
        carry = lax.fori_loop(0, i, lambda j, c: tile(j, c, False), init)
        m, l, acc = tile(i, carry, True)
        o_ref[...] = (acc / _heads_to_lanes(l, masks, tq)).astype(BF16)
        lse = m + jnp.log(l)
        for h in range(hpb):
            l_ref[:, h:h + 1] = lse[h * tq:(h + 1) * tq]

    return pl.pallas_call(
        body, name=name, grid=(nb, nq),
        in_specs=[pl.BlockSpec((tq, LANES), lambda h, i: (i, h)),
                  pl.BlockSpec((T, LANES), lambda h, i: (0, nb + h)),
                  pl.BlockSpec((T, LANES), lambda h, i: (0, 2 * nb + h)),
                  pl.BlockSpec(cum_rows.shape, lambda h, i: (0, 0))] + [_HBM] * len(first),
        out_specs=[pl.BlockSpec((tq, LANES), lambda h, i: (i, h)),
                   pl.BlockSpec((None, tq, hpb), lambda h, i: (h, i, 0))],
        out_shape=(jax.ShapeDtypeStruct((T, A), BF16), jax.ShapeDtypeStruct((nb, T, hpb), F32)),
        scratch_shapes=[pltpu.VMEM((nq, hpb * tq, LANES), BF16)],
        compiler_params=_params(("arbitrary", "arbitrary")),
    )(qkv, qkv, qkv, cum_rows, *first)


def _attn_bwd(qkv, dcat, o, lse, cum_rows, n_heads, name, tq):
    T = qkv.shape[0]
    A = qkv.shape[1] // 3
    dh = A // n_heads
    hpb = LANES // dh
    nb = A // LANES
    nq = T // tq
    scale = dh ** -0.5

    def body(q_ref, k_ref, v_ref, do_ref, o_ref, l_ref, c_ref, dqkv_ref, dc_ref, dr_ref,
             dq_acc, delta, drow, qs_scr, dos_scr, kbd_scr):
        hp = pl.program_id(0)
        masks = _head_masks(hpb, dh, tq)
        crow0 = hp * hpb * nq

        def prologue(i, _):
            rs = pl.ds(pl.multiple_of(i * tq, tq), tq)
            do = do_ref[rs, :]
            prod = do * o_ref[rs, :].astype(F32)
            for h in range(hpb):
                delta[rs, h:h + 1] = jnp.sum(jnp.where(masks[h], prod, 0.0), axis=-1, keepdims=True)
            qs_scr[i] = _stack_heads(q_ref[rs, :], masks)
            dos_scr[i] = _stack_heads(do, masks).astype(BF16)
            kbd_scr[i] = _stack_heads(k_ref[rs, :], masks)
            dq_acc[rs, :] = jnp.zeros((tq, LANES), F32)
            drow[rs, :] = jnp.zeros((tq, hpb), F32)
            return 0

        lax.fori_loop(0, nq, prologue, 0)

        def kv_step(j, _):
            ks = pl.ds(pl.multiple_of(j * tq, tq), tq)
            kt, vt = k_ref[ks, :], v_ref[ks, :]
            kbd = kbd_scr[j]
            bias = jnp.concatenate(
                [jnp.broadcast_to(c_ref[pl.ds(crow0 + h * nq + j, 1), :], (tq, tq)) for h in range(hpb)], axis=0)

            def q_step(i, carry, diag):
                dk, dv, dcs = carry
                rs = pl.ds(pl.multiple_of(i * tq, tq), tq)
                qs, dos = qs_scr[i], dos_scr[i]
                s = _dot(qs, kt, NT) * scale - bias
                if diag:
                    s = jnp.where(_causal_stack(hpb, tq), s, NEG)
                lse = jnp.concatenate([l_ref[rs, h:h + 1] for h in range(hpb)], axis=0)
                p = jnp.exp(s - lse)
                dv = dv + _dot(p.astype(BF16), dos, TN)
                dp = _dot(dos, vt, NT)
                ds = p * (dp - jnp.concatenate([delta[rs, h:h + 1] for h in range(hpb)], axis=0))
                dcs = tuple(dcs[h] - jnp.sum(ds[h * tq:(h + 1) * tq], axis=0, keepdims=True) for h in range(hpb))
                rsum = jnp.sum(ds, axis=-1, keepdims=True)
                for h in range(hpb):
                    drow[rs, h:h + 1] += rsum[h * tq:(h + 1) * tq]
                dsb = (ds * scale).astype(BF16)
                dk = dk + _dot(dsb, qs, TN)
                dscat = jnp.concatenate([dsb[h * tq:(h + 1) * tq] for h in range(hpb)], axis=1)
                dq_acc[rs, :] += _dot(dscat, kbd, NN)
                return dk, dv, dcs

            init = (jnp.zeros((tq, LANES), F32), jnp.zeros((tq, LANES), F32),
                    tuple(jnp.zeros((1, tq), F32) for _ in range(hpb)))
            carry = q_step(j, init, True)
            dk, dv, dcs = lax.fori_loop(j + 1, nq, lambda i, c: q_step(i, c, False), carry)
            dqkv_ref[1, ks, :] = dk.astype(BF16)
            dqkv_ref[2, ks, :] = dv.astype(BF16)
            for h in range(hpb):
                dc_ref[pl.ds(crow0 + h * nq + j, 1), :] = dcs[h]
            return 0

        lax.fori_loop(0, nq, kv_step, 0)
        dqkv_ref[0] = dq_acc[...].astype(BF16)
        dr_ref[...] = drow[...]

    col = lambda off: pl.BlockSpec((T, LANES), functools.partial(lambda h, off: (0, off + h), off=off))
    return pl.pallas_call(
        body, name=name, grid=(nb,),
        in_specs=[col(0), col(nb), col(2 * nb), col(0), col(0),
                  pl.BlockSpec((None, T, hpb), lambda h: (h, 0, 0)),
                  pl.BlockSpec(cum_rows.shape, lambda h: (0, 0))],
        out_specs=[pl.BlockSpec((3, T, LANES), lambda h: (0, 0, h)),
                   pl.BlockSpec(cum_rows.shape, lambda h: (0, 0)),
                   pl.BlockSpec((None, T, hpb), lambda h: (h, 0, 0))],
        out_shape=(jax.ShapeDtypeStruct((3, T, A), BF16), jax.ShapeDtypeStruct(cum_rows.shape, F32),
                   jax.ShapeDtypeStruct((nb, T, hpb), F32)),
        scratch_shapes=[pltpu.VMEM((T, LANES), F32), pltpu.VMEM((T, hpb), F32), pltpu.VMEM((T, hpb), F32),
                        pltpu.VMEM((nq, hpb * tq, LANES), BF16), pltpu.VMEM((nq, hpb * tq, LANES), BF16),
                        pltpu.VMEM((nq, hpb * tq, LANES), BF16)],
        compiler_params=_params(("arbitrary",)),
    )(qkv, qkv, qkv, dcat, o, lse, cum_rows)


def _glu_into(upad, cv_ref, cg_ref, T):
    upad[0:CONV_PAD, :] = jnp.zeros((CONV_PAD, upad.shape[1]), F32)

    def fill(c, _):
        rs = pl.ds(pl.multiple_of(c * CONV_CHUNK, CONV_CHUNK), CONV_CHUNK)
        upad[pl.ds(pl.multiple_of(CONV_PAD + c * CONV_CHUNK, 8), CONV_CHUNK), :] = cv_ref[rs, :] * _sigmoid(cg_ref[rs, :])
        return 0

    lax.fori_loop(0, T // CONV_CHUNK, fill, 0)


SUBLANES = 8
CONV_SHIFT_ROWS = CONV_CHUNK + CONV_PAD - SUBLANES


def _load_window(win, sh, src, r0):
    win[...] = src[pl.ds(r0, CONV_CHUNK + CONV_PAD), :]
    for b in range(1, SUBLANES):
        sh[b - 1] = win[b:b + CONV_SHIFT_ROWS, :]


def _tap(win, sh, o):
    b = o % SUBLANES
    if b == 0:
        return win[o:o + CONV_CHUNK, :]
    return sh[b - 1, o - b:o - b + CONV_CHUNK, :]


def _conv_taps(win, sh, w_ref, first, step):
    acc = None
    for k in range(CONV_K):
        t = w_ref[k:k + 1, :] * _tap(win, sh, first + step * k)
        acc = t if acc is None else acc + t
    return acc


def _conv_scratch(C):
    return [pltpu.VMEM((CONV_CHUNK + CONV_PAD, C), F32), pltpu.VMEM((SUBLANES - 1, CONV_SHIFT_ROWS, C), F32)]


def _conv_fwd(cproj, w, b, lg, lb, name):
    T = cproj.shape[0]
    C = cproj.shape[1] // 2
    off = CONV_PAD - (CONV_K - 1)

    def body(cv_ref, cg_ref, w_ref, b_ref, lg_ref, lb_ref, out_ref, u1_ref, upad, win, sh):
        _glu_into(upad, cv_ref, cg_ref, T)

        def chunk(c, _):
            r0 = pl.multiple_of(c * CONV_CHUNK, CONV_CHUNK)
            _load_window(win, sh, upad, r0)
            u1 = _conv_taps(win, sh, w_ref, off, 1) + b_ref[...]
            u1_ref[pl.ds(r0, CONV_CHUNK), :] = u1
            mu = jnp.mean(u1, axis=-1, keepdims=True)
            var = jnp.mean(jnp.square(u1 - mu), axis=-1, keepdims=True)
            u2 = ((u1 - mu) * lax.rsqrt(var + EPS)) * lg_ref[...] + lb_ref[...]
            out_ref[pl.ds(r0, CONV_CHUNK), :] = (u2 * _sigmoid(u2)).astype(BF16)
            return 0

        lax.fori_loop(0, T // CONV_CHUNK, chunk, 0)

    row = pl.BlockSpec((1, C), lambda i: (0, 0))
    return pl.pallas_call(
        body, name=name, grid=(1,),
        in_specs=[pl.BlockSpec((T, C), lambda i: (0, 0)), pl.BlockSpec((T, C), lambda i: (0, 1)),
                  pl.BlockSpec(w.shape, lambda i: (0, 0)), row, row, row],
        out_specs=[pl.BlockSpec((T, C), lambda i: (0, 0)), pl.BlockSpec((T, C), lambda i: (0, 0))],
        out_shape=(jax.ShapeDtypeStruct((T, C), BF16), jax.ShapeDtypeStruct((T, C), F32)),
        scratch_shapes=[pltpu.VMEM((T + CONV_PAD, C), F32)] + _conv_scratch(C),
        compiler_params=_params(("arbitrary",)),
    )(cproj, cproj, w, b, lg, lb)


def _conv_bwd(cproj, u1_saved, dcat, w, lg, lb, name):
    T = cproj.shape[0]
    C = cproj.shape[1] // 2
    off = CONV_PAD - (CONV_K - 1)
    n_chunks = T // CONV_CHUNK

    def fold(v):
        return jnp.sum(v.reshape(CONV_CHUNK // 8, 8, C), axis=0)

    def body(cv_ref, cg_ref, u1_ref, du_ref, w_ref, lg_ref, lb_ref, dc_ref, dw_ref, sm_ref,
             upad, dpad, dwacc, smacc, win, sh):
        _glu_into(upad, cv_ref, cg_ref, T)
        dpad[pl.ds(T, CONV_PAD), :] = jnp.zeros((CONV_PAD, C), F32)
        dwacc[...] = jnp.zeros_like(dwacc)
        smacc[...] = jnp.zeros_like(smacc)

        def chunk_a(c, _):
            r0 = pl.multiple_of(c * CONV_CHUNK, CONV_CHUNK)
            _load_window(win, sh, upad, r0)
            u1 = u1_ref[pl.ds(r0, CONV_CHUNK), :]
            mu = jnp.mean(u1, axis=-1, keepdims=True)
            var = jnp.mean(jnp.square(u1 - mu), axis=-1, keepdims=True)
            rstd = lax.rsqrt(var + EPS)
            u1h = (u1 - mu) * rstd
            u2 = u1h * lg_ref[...] + lb_ref[...]
            sg = _sigmoid(u2)
            du2 = du_ref[pl.ds(r0, CONV_CHUNK), :] * (sg * (1.0 + u2 * (1.0 - sg)))
            smacc[8:16, :] += fold(du2 * u1h)
            smacc[16:24, :] += fold(du2)
            du1h = du2 * lg_ref[...]
            du1 = rstd * (du1h - jnp.mean(du1h, axis=-1, keepdims=True)
                          - u1h * jnp.mean(du1h * u1h, axis=-1, keepdims=True))
            smacc[0:8, :] += fold(du1)
            dpad[pl.ds(r0, CONV_CHUNK), :] = du1
            for k in range(CONV_K):
                dwacc[8 * k:8 * k + 8, :] += fold(du1 * _tap(win, sh, off + k))
            return 0

        lax.fori_loop(0, n_chunks, chunk_a, 0)

        def chunk_b(c, _):
            r0 = pl.multiple_of(c * CONV_CHUNK, CONV_CHUNK)
            rs = pl.ds(r0, CONV_CHUNK)
            _load_window(win, sh, dpad, r0)
            du0 = _conv_taps(win, sh, w_ref, CONV_K - 1, -1)
            cv, sg = cv_ref[rs, :], _sigmoid(cg_ref[rs, :])
            dc_ref[rs, 0:C] = (du0 * sg).astype(BF16)
            dc_ref[rs, C:2 * C] = (du0 * cv * (sg * (1.0 - sg))).astype(BF16)
            return 0

        lax.fori_loop(0, n_chunks, chunk_b, 0)
        for k in range(CONV_K):
            dw_ref[k:k + 1, :] = jnp.sum(dwacc[8 * k:8 * k + 8, :], axis=0, keepdims=True)
        dw_ref[CONV_K:CONV_PAD, :] = jnp.zeros((CONV_PAD - CONV_K, C), F32)
        for r in range(3):
            sm_ref[r:r + 1, :] = jnp.sum(smacc[8 * r:8 * r + 8, :], axis=0, keepdims=True)
        sm_ref[3:8, :] = jnp.zeros((5, C), F32)

    row = pl.BlockSpec((1, C), lambda i: (0, 0))
    return pl.pallas_call(
        body, name=name, grid=(1,),
        in_specs=[pl.BlockSpec((T, C), lambda i: (0, 0)), pl.BlockSpec((T, C), lambda i: (0, 1)),
                  pl.BlockSpec((T, C), lambda i: (0, 0)), pl.BlockSpec((T, C), lambda i: (0, 1)),
                  pl.BlockSpec(w.shape, lambda i: (0, 0)), row, row],
        out_specs=[pl.BlockSpec((T, 2 * C), lambda i: (0, 0)), pl.BlockSpec((CONV_PAD, C), lambda i: (0, 0)),
                   pl.BlockSpec((8, C), lambda i: (0, 0))],
        out_shape=(jax.ShapeDtypeStruct((T, 2 * C), BF16), jax.ShapeDtypeStruct((CONV_PAD, C), F32),
                   jax.ShapeDtypeStruct((8, C), F32)),
        scratch_shapes=[pltpu.VMEM((T + CONV_PAD, C), F32), pltpu.VMEM((T + CONV_PAD, C), F32),
                        pltpu.VMEM((8 * CONV_PAD, C), F32), pltpu.VMEM((24, C), F32)] + _conv_scratch(C),
        compiler_params=_params(("arbitrary",)),
    )(cproj, cproj, u1_saved, dcat, w, lg, lb)


def _loss_head(x, target, name):
    T, D = x.shape
    tm = _tile(T, 512, 8)

    def body(x_ref, t_ref, loss_ref, dx_ref):
        @pl.when(pl.program_id(0) == 0)
        def _():
            loss_ref[...] = jnp.zeros_like(loss_ref)

        err = x_ref[...] - t_ref[...]
        part = jnp.sum(jnp.mean(err * err, axis=-1, keepdims=True), axis=0, keepdims=True)
        loss_ref[...] += jnp.broadcast_to(0.5 * part, loss_ref.shape)
        dx_ref[...] = err * (1.0 / D)

    big = pl.BlockSpec((tm, D), lambda i: (i, 0))
    return pl.pallas_call(
        body, name=name, grid=(T // tm,),
        in_specs=[big, big],
        out_specs=[pl.BlockSpec((8, LANES), lambda i: (0, 0)), big],
        out_shape=(jax.ShapeDtypeStruct((8, LANES), F32), jax.ShapeDtypeStruct((T, D), F32)),
        compiler_params=_params(("arbitrary",)),
    )(x, target)


def _ada_fwd(c_all, ada_w, ada_b_loc, name):
    L, D, S = ada_w.shape
    B = c_all.shape[0]

    def body(c_ref, w_ref, b_ref, o_ref):
        c = c_ref[...]
        ca = (c * _sigmoid(c)).astype(BF16)
        o_ref[...] = _dot(ca, w_ref[...].astype(BF16), NN) + b_ref[...]

    return pl.pallas_call(
        body, name=name, grid=(L,),
        in_specs=[pl.BlockSpec((B, D), lambda l: (0, 0)), pl.BlockSpec((None, D, S), lambda l: (l, 0, 0)),
                  pl.BlockSpec((None, 1, S), lambda l: (l, 0, 0))],
        out_specs=pl.BlockSpec((None, B, S), lambda l: (l, 0, 0)),
        out_shape=jax.ShapeDtypeStruct((L, B, S), F32),
        compiler_params=_params(("parallel",)),
    )(c_all, ada_w, ada_b_loc)


def _ada_bwd(c_all_t, dmod_loc, name):
    D, B = c_all_t.shape
    L, _, S = dmod_loc.shape

    def body(c_ref, dm_ref, o_ref):
        c = c_ref[...]
        ca = c * _sigmoid(c)
        acc = None
        for bb in range(B):
            t = ca[:, bb:bb + 1] * dm_ref[bb:bb + 1, :]
            acc = t if acc is None else acc + t
        o_ref[...] = acc

    return pl.pallas_call(
        body, name=name, grid=(L,),
        in_specs=[pl.BlockSpec((D, B), lambda l: (0, 0)), pl.BlockSpec((None, B, S), lambda l: (l, 0, 0))],
        out_specs=pl.BlockSpec((None, D, S), lambda l: (l, 0, 0)),
        out_shape=jax.ShapeDtypeStruct((L, D, S), F32),
        compiler_params=_params(("parallel",)),
    )(c_all_t, dmod_loc)


def _sum_devices(parts, name):
    _, R, C = parts.shape
    tm = _tile(R, 256, 8)

    def body(p_ref, o_ref):
        acc = p_ref[0].astype(F32)
        for d in range(1, N_DEV):
            acc = acc + p_ref[d].astype(F32)
        o_ref[...] = acc

    return pl.pallas_call(
        body, name=name, grid=(R // tm,),
        in_specs=[pl.BlockSpec((N_DEV, tm, C), lambda i: (0, i, 0))],
        out_specs=pl.BlockSpec((tm, C), lambda i: (i, 0)),
        out_shape=jax.ShapeDtypeStruct((R, C), F32),
        compiler_params=_params(("parallel",)),
    )(parts)


def _adamw_math(w, g, m, v):
    m = ADAM_B1 * m + (1.0 - ADAM_B1) * g
    v = ADAM_B2 * v + (1.0 - ADAM_B2) * (g * g)
    m_hat = m / (1.0 - ADAM_B1 ** ADAM_STEP)
    v_hat = v / (1.0 - ADAM_B2 ** ADAM_STEP)
    delta = -ADAM_LR * (m_hat / (jnp.sqrt(v_hat) + ADAM_EPS) + ADAM_WD * w)
    return delta, m, v


def _adamw(w, g, m, v, name, summed):
    R, C = w.shape
    tm = _tile(R, 256, 16)
    n_parts = g.shape[0] if summed else 0

    def body(w_ref, g_ref, m_ref, v_ref, go_ref, d_ref, mo_ref, vo_ref):
        if summed:
            g = g_ref[0].astype(F32)
            for d in range(1, n_parts):
                g = g + g_ref[d].astype(F32)
        else:
            g = g_ref[...]
        delta, mn, vn = _adamw_math(w_ref[...], g, m_ref[...], v_ref[...])
        go_ref[...] = g
        d_ref[...] = delta
        mo_ref[...] = mn
        vo_ref[...] = vn

    big = pl.BlockSpec((tm, C), lambda i: (i, 0))
    gspec = pl.BlockSpec((n_parts, tm, C), lambda i: (0, i, 0)) if summed else big
    return pl.pallas_call(
        body, name=name, grid=(R // tm,),
        in_specs=[big, gspec, big, big],
        out_specs=[big, big, big, big],
        out_shape=tuple(jax.ShapeDtypeStruct((R, C), F32) for _ in range(4)),
        compiler_params=_params(("parallel",)),
    )(w, g, m, v)


def _adamw_summed(w, parts, m, v, name):
    L = len(parts)
    n_parts, R, C = parts[0].shape
    tm = _tile(R, 256, 16)
    tc = C if tm < R else _tile(C, 256)
    nr = (R // tm) * (C // tc)
    ncb = C // tc

    def body(*refs):
        w_ref, g_refs = refs[0], refs[1:1 + L]
        m_ref, v_ref, go_ref, d_ref, mo_ref, vo_ref = refs[1 + L:]
        for ll in range(L):
            @pl.when(pl.program_id(0) == ll)
            def _(ll=ll):
                g = g_refs[ll][0].astype(F32)
                for d in range(1, n_parts):
                    g = g + g_refs[ll][d].astype(F32)
                delta, mn, vn = _adamw_math(w_ref[...], g, m_ref[...], v_ref[...])
                go_ref[...] = g
                d_ref[...] = delta
                mo_ref[...] = mn
                vo_ref[...] = vn

    big = pl.BlockSpec((None, tm, tc), lambda l, i: (l, i // ncb, i % ncb))

    def part_index(l, i, ll):
        i = jnp.where(l == ll, i, 0)
        return 0, i // ncb, i % ncb

    gspecs = [pl.BlockSpec((n_parts, tm, tc), functools.partial(part_index, ll=ll)) for ll in range(L)]
    return pl.pallas_call(
        body, name=name, grid=(L, nr),
        in_specs=[big, *gspecs, big, big],
        out_specs=[big, big, big, big],
        out_shape=tuple(jax.ShapeDtypeStruct((L, R, C), F32) for _ in range(4)),
        compiler_params=_params(("arbitrary", "arbitrary")),
    )(w, *parts, m, v)


def _pack(arrs, D):
    L = arrs[0].shape[0]
    cols = []
    for a in arrs:
        f = a.reshape(L, -1)
        n = f.shape[1]
        cols.append(jnp.pad(f, ((0, 0), (0, -(-n // D) * D - n))))
    flat = jnp.concatenate(cols, axis=1)
    return flat.reshape(L, flat.shape[1] // D, D)


def _unpack(p, shapes, D):
    L = p.shape[0]
    out, r = [], 0
    for s in shapes:
        n = math.prod(s[1:])
        rows = -(-n // D)
        out.append(p[:, r:r + rows].reshape(L, rows * D)[:, :n].reshape(s))
        r += rows
    return out


def kernel(x, c, w_in, b_f, conv_w, conv_b, conv_ln_g, conv_ln_b, w_o, w_ffn_in, w_ffn_out, mix_pre_g, mix_post_g, ffn_pre_g, ffn_post_g, ada_w, ada_b, loss_target, m_w_in, m_b_f, m_conv_w, m_conv_b, m_conv_ln_g, m_conv_ln_b, m_w_o, m_w_ffn_in, m_w_ffn_out, m_mix_pre_g, m_mix_post_g, m_ffn_pre_g, m_ffn_post_g, m_ada_w, m_ada_b, v_w_in, v_b_f, v_conv_w, v_conv_b, v_conv_ln_g, v_conv_ln_b, v_w_o, v_w_ffn_in, v_w_ffn_out, v_mix_pre_g, v_mix_post_g, v_ffn_pre_g, v_ffn_post_g, v_ada_w, v_ada_b):
    L, D, s_in = w_in.shape
    T = x.shape[1]
    H = b_f.shape[1]
    A = D // 2
    C = D - A
    cs = conv_w.shape[2]
    F = w_ffn_out.shape[1] * N_DEV
    s_ff = w_ffn_in.shape[2]
    w_ffn_in_t = jnp.transpose(w_ffn_in, (0, 2, 1))
    w_in_t = jnp.transpose(w_in, (0, 2, 1))
    s_ada = ada_w.shape[2]
    R = 16
    me = _my_index()
    x0 = x[0]
    target = loss_target[0]
    tq = _tile(T, 512)
    nq = T // tq

    c_all = _exchange(c, gather=True, name="gather_c").reshape(N_DEV, D)
    ada_b_loc = lax.dynamic_slice_in_dim(ada_b, me * s_ada, s_ada, axis=1)[:, None, :]
    mod_loc = _ada_fwd(c_all, ada_w, ada_b_loc, "ada_fwd")
    mod_g = _exchange(mod_loc, gather=True, name="gather_mod")
    mod = lax.dynamic_index_in_dim(mod_g, me, axis=2, keepdims=False)
    mod = jnp.transpose(mod, (1, 0, 2)).reshape(L, N_MOD, 1, D)
    cw_g = _exchange(conv_w, gather=True, name="gather_conv_w")
    conv_w_full = jnp.transpose(cw_g, (1, 2, 0, 3)).reshape(L, CONV_K, C)
    conv_w_pad = jnp.pad(conv_w_full, ((0, 0), (0, CONV_PAD - CONV_K), (0, 0)))
    b_f_col = jnp.pad(b_f, ((0, 0), (0, R - H)))[:, :, None]

    h_gmix, h_gffn, chain = [], [], (mod_g, cw_g)
    for l in range(L):
        h_gmix.append(_gather_start([w_in_t[l].astype(BF16), w_o[l].astype(BF16)], f"gather_mix_{l}", chain))
        h_gffn.append(_gather_start([w_ffn_in_t[l].astype(BF16), w_ffn_out[l].astype(BF16)], f"gather_ffn_{l}",
                                    (h_gmix[l]["token"],)))
        chain = (h_gffn[l]["token"],)
    gather_tokens = chain

    def by_rows(g):
        return g.reshape(N_DEV * g.shape[1], g.shape[2])

    W_in_t, W_f_t, W_c_t, W_o, W_ffn_in_t, W_ffn_out = ([None] * L for _ in range(6))

    saved = []
    xc = x0
    for l in range(L):
        sh1, sc1, g1, sh2, sc2, g2 = (mod[l, k] for k in range(N_MOD))
        gpre1, gpost1, gpre2, gpost2 = (p[l][None, :] for p in (mix_pre_g, mix_post_g, ffn_pre_g, ffn_post_g))
        if l == 0:
            h_gmix[0] = _gather_mid(h_gmix[0], gather_tokens[0], "gather_mid_mix_0")
        g_in, g_o = _gather_wait(h_gmix[l], gather_tokens[0] if l == 0 else xc, f"gather_wait_mix_{l}")
        W_in_t[l], W_o[l] = by_rows(g_in), by_rows(g_o)
        W_f_t[l] = jnp.pad(W_in_t[l][3 * A:3 * A + H], ((0, R - H), (0, 0)))
        W_c_t[l] = W_in_t[l][3 * A + H:]
        if l == 0:
            h1 = _prenorm(xc, gpre1, sc1, sh1, "prenorm1_0", first=gather_tokens)
        qkv = _mm([(h1, W_in_t[l])], "nt", BF16, f"proj_qkv_{l}", tm=MM_ROWS, n_cols=3 * A)
        cproj = _mm([(h1, W_c_t[l])], "nt", F32, f"proj_conv_{l}", tm=MM_ROWS, tn=MM_COLS)
        fl, cum = _fgate_fwd(h1, W_f_t[l], b_f_col[l], f"fgate_{l}")
        cum_rows = cum[:H].reshape(H * nq, tq)
        if l > 0:
            h_gffn[l] = _gather_mid(h_gffn[l], qkv, f"gather_mid_ffn_{l}")
            o, lse = _attn_fwd(qkv, cum_rows, H, f"attn_{l}", tq, first=(h_gffn[l]["token"],))
        else:
            o, lse = _attn_fwd(qkv, cum_rows, H, "attn_0", tq)
            h_gffn[0] = _gather_mid(h_gffn[0], o, "gather_mid_ffn_0")
        u3, u1c = _conv_fwd(cproj, conv_w_pad[l], conv_b[l][None, :], conv_ln_g[l][None, :], conv_ln_b[l][None, :],
                            f"conv_{l}")
        cat = jnp.concatenate([o, u3], axis=-1)
        y1, x_mid, h2 = _mm_postnorm(cat, W_o[l], xc, g1, gpost1, f"out_proj_{l}", nxt=(gpre2, sc2, sh2))
        g_fi, g_fo = _gather_wait(h_gffn[l], x_mid, f"gather_wait_ffn_{l}")
        W_ffn_in_t[l], W_ffn_out[l] = by_rows(g_fi), by_rows(g_fo)
        g, u, act = _ffn_in_fwd(h2, W_ffn_in_t[l], f"ffn_in_{l}")
        if l + 1 < L:
            h_gmix[l + 1] = _gather_mid(h_gmix[l + 1], act, f"gather_mid_mix_{l + 1}")
        saved_l = (xc, h1, qkv, cproj, u1c, fl, cum_rows, o, lse, cat, y1, x_mid, h2, g, u, act)
        if l + 1 < L:
            nxt = (mix_pre_g[l + 1][None, :], mod[l + 1, 1], mod[l + 1, 0])
            y2, xc, h1 = _mm_postnorm(act, W_ffn_out[l], x_mid, g2, gpost2, f"ffn_out_{l}", nxt=nxt,
                                      first=(h_gmix[l + 1]["token"],))
        else:
            y2, xc = _mm_postnorm(act, W_ffn_out[l], x_mid, g2, gpost2, f"ffn_out_{l}")
        saved.append(saved_l + (y2,))

    loss_tile, dx = _loss_head(xc, target, "loss_head")
    loss = lax.psum(loss_tile[0, 0], ("x", "y", "c"))

    def reduce_start(parts, name):
        return _scatter_start(parts, "scatter_" + name)

    small, h_ffn, h_mix = [None] * L, [None] * L, [None] * L
    for l in reversed(range(L)):
        xin, h1, qkv, cproj, u1c, fl, cum_rows, o, lse, cat, y1, x_mid, h2, g, u, act, y2 = saved[l]
        sh1, sc1, g1, sh2, sc2, g2 = (mod[l, k] for k in range(N_MOD))
        gpre1, gpost1, gpre2, gpost2 = (p[l][None, :] for p in (mix_pre_g, mix_post_g, ffn_pre_g, ffn_post_g))
        dy2, sm_post2 = _postnorm_bwd(dx, y2, g2, gpost2, f"postnorm2_bwd_{l}")
        dgate, dup = _ffn_act_bwd(dy2, W_ffn_out[l], g, u, f"ffn_act_bwd_{l}")
        dW_ffn_out = _mm([(act, dy2)], "tn", BF16, f"dw_ffn_out_{l}", tm=DW_TILE)
        dWg_t = _mm([(dgate, h2)], "tn", BF16, f"dw_ffn_gate_{l}", tm=DW_TILE)
        dWu_t = _mm([(dup, h2)], "tn", BF16, f"dw_ffn_up_{l}", tm=DW_TILE)
        half = N_DEV // 2
        dW_ffn_in_t = jnp.concatenate([dWg_t.reshape(half, s_ff, D), dWu_t.reshape(half, s_ff, D)], axis=0)
        h_ffn[l], token = reduce_start([dW_ffn_in_t, dW_ffn_out.reshape(N_DEV, F // N_DEV, D)], f"ffn_{l}")
        dx_mid, sm_pre2 = _prenorm_bwd([(dgate, W_ffn_in_t[l], 0), (dup, W_ffn_in_t[l], 1)], x_mid, dx, gpre2, sc2,
                                       f"prenorm2_bwd_{l}", first=(token,))
        dy1, sm_post1 = _postnorm_bwd(dx_mid, y1, g1, gpost1, f"postnorm1_bwd_{l}")
        dcat = _mm([(dy1, W_o[l])], "nt", F32, f"dcat_{l}", tm=MM_ROWS, tn=MM_COLS)
        dW_o = _mm([(cat, dy1)], "tn", BF16, f"dw_o_{l}", tm=DW_TILE)
        dqkv, dcum_rows, dcum_q = _attn_bwd(qkv, dcat, o, lse, cum_rows, H, f"attn_bwd_{l}", tq)
        dcproj, dconv_w, sm_conv = _conv_bwd(cproj, u1c, dcat, conv_w_pad[l], conv_ln_g[l][None, :],
                                             conv_ln_b[l][None, :], f"conv_bwd_{l}")
        dcum = dcum_rows.reshape(H, T) + jnp.transpose(dcum_q, (0, 2, 1)).reshape(H, T)
        dcum = jnp.pad(dcum, ((0, R - H), (0, 0)))
        dfl_t, dwf_t, dbf = _fgate_bwd(dcum, fl, h1, f"fgate_bwd_{l}")
        dfl = jnp.transpose(dfl_t).astype(BF16)
        dh1_pairs = [(dqkv[0], W_in_t[l], 0), (dqkv[1], W_in_t[l], 1), (dqkv[2], W_in_t[l], 2),
                     (dfl, W_f_t[l]), (dcproj, W_c_t[l])]
        dWq_t = _dw_stack(dqkv, h1, f"dw_qkv_{l}")
        dWc_t = _mm([(dcproj, h1)], "tn", F32, f"dw_conv_{l}", tm=DW_TILE)
        dW_in_t = jnp.concatenate([dWq_t, dwf_t[:H], dWc_t], axis=0).astype(BF16)
        h_mix[l], token = reduce_start([dW_in_t.reshape(N_DEV, s_in, D), dW_o.reshape(N_DEV, D // N_DEV, D)], f"mix_{l}")
        dx, sm_pre1 = _prenorm_bwd(dh1_pairs, xin, dx_mid, gpre1, sc1, f"prenorm1_bwd_{l}", first=(token,))
        dmod = jnp.stack([sm_pre1[0], sm_pre1[1], sm_post1[0], sm_pre2[0], sm_pre2[1], sm_post2[0]])
        small[l] = (dmod, sm_pre1[2], sm_post1[1], sm_pre2[2], sm_post2[1], sm_conv[0], sm_conv[1], sm_conv[2],
                    dbf[:H, 0], dconv_w[:CONV_K])
    grad_x = dx[None]

    small_names = 10
    small_l = [jnp.stack([small[l][k] for l in range(L)]) for k in range(small_names)]
    small_shapes = [a.shape for a in small_l]
    packed = _pack(small_l, D)
    rows = packed.shape[1]
    rows_pad = -(-L * rows // 8) * 8
    packed2 = jnp.pad(packed.reshape(L * rows, D), ((0, rows_pad - L * rows), (0, 0)))
    h_small = _gather_start([packed2], "gather_small")

    got_ffn = [_scatter_wait(h_ffn[l], h_small["token"], f"scatter_wait_ffn_{l}") for l in range(L)]

    def step(w, m, v, parts, name):
        return list(_adamw_summed(w, parts, m, v, "adamw_" + name))

    swap = lambda t: jnp.transpose(t, (0, 2, 1))
    r_w_ffn_in = [swap(t) for t in step(w_ffn_in_t, swap(m_w_ffn_in), swap(v_w_ffn_in),
                                        [got_ffn[l][0] for l in range(L)], "w_ffn_in")]
    r_w_ffn_out = step(w_ffn_out, m_w_ffn_out, v_w_ffn_out, [got_ffn[l][1] for l in range(L)], "w_ffn_out")
    got_mix = [_scatter_wait(h_mix[l], r_w_ffn_out[1], f"scatter_wait_mix_{l}") for l in range(L)]
    r_w_in = [swap(t) for t in step(w_in_t, swap(m_w_in), swap(v_w_in), [got_mix[l][0] for l in range(L)], "w_in")]
    r_w_o = step(w_o, m_w_o, v_w_o, [got_mix[l][1] for l in range(L)], "w_o")

    small_g = _gather_wait(_gather_mid(h_small, r_w_o[1], "gather_small_mid"), r_w_o[1], "gather_small_wait")[0]
    small_sum = _sum_devices(small_g, "sum_small")[:L * rows].reshape(L, rows, D)
    (g_ada_b6, g_mix_pre, g_mix_post, g_ffn_pre, g_ffn_post, g_conv_b, g_ln_g, g_ln_b, g_b_f,
     g_conv_w_full) = _unpack(small_sum, small_shapes, D)
    g_ada_b = g_ada_b6.reshape(L, N_MOD * D)
    g_conv_w = lax.dynamic_slice_in_dim(g_conv_w_full, me * cs, cs, axis=2)
    dmod_all = small_g[:, :L * rows].reshape(N_DEV, L, rows, D)[:, :, :N_MOD].reshape(N_DEV, L, N_MOD * D)
    dmod_loc = jnp.transpose(lax.dynamic_slice_in_dim(dmod_all, me * s_ada, s_ada, axis=2), (1, 0, 2))
    g_ada_w = _ada_bwd(jnp.transpose(c_all), dmod_loc, "ada_bwd")
    r_ada_w = [t.reshape(ada_w.shape) for t in _adamw(
        ada_w.reshape(L * D, s_ada), g_ada_w.reshape(L * D, s_ada), m_ada_w.reshape(L * D, s_ada),
        v_ada_w.reshape(L * D, s_ada), "adamw_ada_w", False)]

    sw = [b_f, conv_w, conv_b, conv_ln_g, conv_ln_b, mix_pre_g, mix_post_g, ffn_pre_g, ffn_post_g, ada_b]
    sg = [g_b_f, g_conv_w, g_conv_b, g_ln_g, g_ln_b, g_mix_pre, g_mix_post, g_ffn_pre, g_ffn_post, g_ada_b]
    sm = [m_b_f, m_conv_w, m_conv_b, m_conv_ln_g, m_conv_ln_b, m_mix_pre_g, m_mix_post_g, m_ffn_pre_g, m_ffn_post_g, m_ada_b]
    sv = [v_b_f, v_conv_w, v_conv_b, v_conv_ln_g, v_conv_ln_b, v_mix_pre_g, v_mix_post_g, v_ffn_pre_g, v_ffn_post_g, v_ada_b]
    shapes = [a.shape for a in sw]

    def flat(arrs):
        p = _pack(arrs, D)
        n = p.shape[0] * p.shape[1]
        return jnp.pad(p.reshape(n, D), ((0, -(-n // 8) * 8 - n), (0, 0))), p.shape

    pw, pshape = flat(sw)
    pg, pm, pv = flat(sg)[0], flat(sm)[0], flat(sv)[0]
    s_outs = _adamw(pw, pg, pm, pv, "adamw_small", False)
    n_small = pshape[0] * pshape[1]
    s_g, s_d, s_m, s_v = (_unpack(t[:n_small].reshape(pshape), shapes, D) for t in s_outs)

    big = {"w_in": r_w_in, "w_o": r_w_o, "w_ffn_in": r_w_ffn_in, "w_ffn_out": r_w_ffn_out, "ada_w": r_ada_w}
    order = ["w_in", "b_f", "conv_w", "conv_b", "conv_ln_g", "conv_ln_b", "w_o", "w_ffn_in", "w_ffn_out",
             "mix_pre_g", "mix_post_g", "ffn_pre_g", "ffn_post_g", "ada_w", "ada_b"]
    small_pos = {n: i for i, n in enumerate(["b_f", "conv_w", "conv_b", "conv_ln_g", "conv_ln_b", "mix_pre_g",
                                             "mix_post_g", "ffn_pre_g", "ffn_post_g", "ada_b"])}

    def pick(n, k):
        if n in big:
            return big[n][k]
        return (s_g, s_d, s_m, s_v)[k][small_pos[n]]

    return (loss, grad_x, *[pick(n, 0) for n in order], *[pick(n, 1) for n in order],
            *[pick(n, 2) for n in order], *[pick(n, 3) for n in order])
```

```python
import functools
import math

import jax
import jax.numpy as jnp
from jax import lax
from jax.experimental import pallas as pl
from jax.experimental.pallas import tpu as pltpu

F32 = jnp.float32
BF16 = jnp.bfloat16
MESH = pl.DeviceIdType.MESH
N_DEV = 8
EPS = 1e-6
CONV_K = 31
CONV_PAD = 32
CONV_CHUNK = 128
N_MOD = 6
NEG = -1e30
LANES = 128
VMEM_LIMIT = 56 * 2**20
MM_TILE = 1024
MM_ROWS = 2048
MM_COLS = 512
MM_ROWS_SMALL = 512
FFN_COLS = 256
RESIDENT = pl.Buffered(1)
DW_TILE = 1408
ADAM_LR, ADAM_B1, ADAM_B2, ADAM_EPS, ADAM_WD, ADAM_STEP = 0.001, 0.9, 0.999, 1e-08, 0.01, 10

NN = (((1,), (0,)), ((), ()))
NT = (((1,), (1,)), ((), ()))
TN = (((0,), (0,)), ((), ()))


def _dot(a, b, dims):
    return lax.dot_general(a, b, dims, preferred_element_type=F32)


def _tile(n, pref, align=LANES):
    if n <= pref:
        return n
    t = (pref // align) * align
    while t >= align:
        if n % t == 0:
            return t
        t -= align
    return n


def _params(sem=None):
    return pltpu.CompilerParams(dimension_semantics=sem, vmem_limit_bytes=VMEM_LIMIT)


def _sigmoid(x):
    return 1.0 / (1.0 + jnp.exp(-x))


def _my_index():
    return 4 * lax.axis_index("x") + 2 * lax.axis_index("y") + lax.axis_index("c")


def _exchange(x, *, gather, name):
    blk = x.shape if gather else x.shape[1:]

    def body(x_ref, y_ref, send_sems, recv_sems, local_sem):
        mx, my, mc = lax.axis_index("x"), lax.axis_index("y"), lax.axis_index("c")
        me = 4 * mx + 2 * my + mc

        def src(p):
            return x_ref if gather else x_ref.at[p]

        mine = pltpu.make_async_copy(src(me), y_ref.at[me], local_sem)
        mine.start()
        copies = []
        for k in range(1, N_DEV):
            px = (1 - mx) if (k >> 2) & 1 else mx
            py = (1 - my) if (k >> 1) & 1 else my
            pc = (1 - mc) if k & 1 else mc
            cp = pltpu.make_async_remote_copy(
                src_ref=src(4 * px + 2 * py + pc), dst_ref=y_ref.at[me],
                send_sem=send_sems.at[k - 1], recv_sem=recv_sems.at[k - 1],
                device_id=(px, py, pc), device_id_type=MESH)
            cp.start()
            copies.append(cp)
        for cp in copies:
            cp.wait()
        mine.wait()

    return pl.pallas_call(
        body, name=name,
        out_shape=jax.ShapeDtypeStruct((N_DEV,) + tuple(blk), x.dtype),
        in_specs=[pl.BlockSpec(memory_space=pl.ANY)],
        out_specs=pl.BlockSpec(memory_space=pl.ANY),
        scratch_shapes=[pltpu.SemaphoreType.DMA((N_DEV - 1,)), pltpu.SemaphoreType.DMA((N_DEV - 1,)),
                        pltpu.SemaphoreType.DMA(())],
    )(x)


_HBM = pl.BlockSpec(memory_space=pl.ANY)


_SEM = pl.BlockSpec(memory_space=pltpu.SEMAPHORE)
_HBM_SPEC = pl.BlockSpec(memory_space=pltpu.HBM)
_EFFECT = pltpu.CompilerParams(has_side_effects=pltpu.SideEffectType.DATAFLOW_SIDE_EFFECTING)


def _in_hbm(a):
    return pltpu.with_memory_space_constraint(a, pltpu.HBM)


def _scatter_copies(x_refs, land_refs, send_sems, recv_sems, loc_sems):
    mx, my, mc = lax.axis_index("x"), lax.axis_index("y"), lax.axis_index("c")
    me = 4 * mx + 2 * my + mc
    local, remote = [], []
    for a, (x_ref, land_ref) in enumerate(zip(x_refs, land_refs)):
        local.append(pltpu.make_async_copy(x_ref.at[me], land_ref.at[me], loc_sems.at[a]))
        for k in range(1, N_DEV):
            px = (1 - mx) if (k >> 2) & 1 else mx
            py = (1 - my) if (k >> 1) & 1 else my
            pc = (1 - mc) if k & 1 else mc
            remote.append(pltpu.make_async_remote_copy(
                src_ref=x_ref.at[4 * px + 2 * py + pc], dst_ref=land_ref.at[me],
                send_sem=send_sems.at[(N_DEV - 1) * a + k - 1], recv_sem=recv_sems.at[(N_DEV - 1) * a + k - 1],
                device_id=(px, py, pc), device_id_type=MESH))
    return local, remote


def _scatter_start(xs, name):
    n = len(xs)

    def body(*refs):
        local, remote = _scatter_copies(refs[:n], refs[n:2 * n], *refs[2 * n:2 * n + 3])
        for cp in local + remote:
            cp.start()
        refs[-1][...] = jnp.zeros_like(refs[-1])

    lands = [lax.empty(x.shape, x.dtype) for x in xs]
    n_sems = (N_DEV - 1) * n
    outs = pl.pallas_call(
        body, name=name,
        out_shape=(pltpu.SemaphoreType.DMA((n_sems,)), pltpu.SemaphoreType.DMA((n_sems,)), pltpu.SemaphoreType.DMA((n,)),
                   *[pltpu.HBM(x.shape, x.dtype) for x in xs], *[pltpu.HBM(x.shape, x.dtype) for x in xs],
                   jax.ShapeDtypeStruct((8, LANES), F32)),
        in_specs=[_HBM_SPEC] * (2 * n),
        out_specs=(_SEM, _SEM, _SEM, *[_HBM_SPEC] * (2 * n), pl.BlockSpec(memory_space=pltpu.VMEM)),
        input_output_aliases={i: 3 + i for i in range(2 * n)},
        compiler_params=_EFFECT,
    )(*[_in_hbm(x) for x in xs], *[_in_hbm(t) for t in lands])
    return (outs[0], outs[1], outs[2], list(outs[3:3 + n]), list(outs[3 + n:3 + 2 * n])), outs[-1]


def _scatter_wait(handle, after, name):
    send_sems, recv_sems, loc_sems, x_thru, land_thru = handle
    n = len(x_thru)

    def body(*refs):
        local, remote = _scatter_copies(refs[:n], refs[n:2 * n], *refs[2 * n:2 * n + 3])
        for cp in local:
            cp.wait()
        for cp in remote:
            cp.wait_send()
            cp.wait_recv()

    outs = pl.pallas_call(
        body, name=name,
        out_shape=[pltpu.HBM(x.shape, x.dtype) for x in x_thru + land_thru],
        in_specs=[_HBM_SPEC] * (2 * n) + [_SEM, _SEM, _SEM, _HBM],
        out_specs=[_HBM_SPEC] * (2 * n),
        input_output_aliases={i: i for i in range(2 * n)},
        compiler_params=_EFFECT,
    )(*x_thru, *land_thru, send_sems, recv_sems, loc_sems, after)
    return list(outs[n:])


def _gather_first_copies(x_refs, y_refs, send_sems, sib_recv, ici_recv, loc_sems):
    mx, my, mc = lax.axis_index("x"), lax.axis_index("y"), lax.axis_index("c")
    me = 4 * mx + 2 * my + mc
    local, remote = [], []
    for a, (x_ref, y_ref) in enumerate(zip(x_refs, y_refs)):
        local.append(pltpu.make_async_copy(x_ref, y_ref.at[me], loc_sems.at[a]))
        remote.append(pltpu.make_async_remote_copy(
            src_ref=x_ref, dst_ref=y_ref.at[me], send_sem=send_sems.at[4 * a], recv_sem=sib_recv.at[a],
            device_id=(mx, my, 1 - mc), device_id_type=MESH))
        for j, (px, py) in enumerate([(1 - mx, my), (mx, 1 - my), (1 - mx, 1 - my)]):
            remote.append(pltpu.make_async_remote_copy(
                src_ref=x_ref, dst_ref=y_ref.at[me], send_sem=send_sems.at[4 * a + 1 + j], recv_sem=ici_recv.at[3 * a + j],
                device_id=(px, py, mc), device_id_type=MESH))
    return local, remote


def _gather_forward_copies(y_refs, ici_recv, fwd_send, fwd_recv):
    mx, my, mc = lax.axis_index("x"), lax.axis_index("y"), lax.axis_index("c")
    pairs = []
    for a, y_ref in enumerate(y_refs):
        for j, (px, py) in enumerate([(1 - mx, my), (mx, 1 - my), (1 - mx, 1 - my)]):
            slot = y_ref.at[4 * px + 2 * py + mc]
            arrival = pltpu.make_async_remote_copy(
                src_ref=slot, dst_ref=slot, send_sem=fwd_send.at[3 * a + j], recv_sem=ici_recv.at[3 * a + j],
                device_id=(px, py, mc), device_id_type=MESH)
            onward = pltpu.make_async_remote_copy(
                src_ref=slot, dst_ref=slot, send_sem=fwd_send.at[3 * a + j], recv_sem=fwd_recv.at[3 * a + j],
                device_id=(mx, my, 1 - mc), device_id_type=MESH)
            pairs.append((arrival, onward))
    return pairs


def _gather_start(xs, name, after=()):
    n = len(xs)
    ni = 2 * n + len(after)

    def body(*refs):
        local, remote = _gather_first_copies(refs[:n], refs[n:2 * n], *refs[ni:ni + 4])
        for cp in local + remote:
            cp.start()
        refs[-1][...] = jnp.zeros_like(refs[-1])

    ys = [lax.empty((N_DEV,) + tuple(x.shape), x.dtype) for x in xs]
    dma = pltpu.SemaphoreType.DMA
    outs = pl.pallas_call(
        body, name=name,
        out_shape=(dma((4 * n,)), dma((n,)), dma((3 * n,)), dma((n,)),
                   *[pltpu.HBM(x.shape, x.dtype) for x in xs], *[pltpu.HBM(y.shape, y.dtype) for y in ys],
                   jax.ShapeDtypeStruct((8, LANES), F32)),
        in_specs=[_HBM_SPEC] * (2 * n) + [_HBM] * len(after),
        out_specs=(_SEM, _SEM, _SEM, _SEM, *[_HBM_SPEC] * (2 * n), pl.BlockSpec(memory_space=pltpu.VMEM)),
        input_output_aliases={i: 4 + i for i in range(2 * n)},
        compiler_params=_EFFECT,
    )(*[_in_hbm(x) for x in xs], *[_in_hbm(y) for y in ys], *after)
    return dict(send=outs[0], sib_recv=outs[1], ici_recv=outs[2], loc=outs[3], x=list(outs[4:4 + n]),
                y=list(outs[4 + n:4 + 2 * n]), token=outs[-1])


def _gather_mid(h, after, name):
    n = len(h["y"])

    def body(*refs):
        for arrival, onward in _gather_forward_copies(refs[:n], refs[n], refs[n + 2 + n], refs[n + 3 + n]):
            arrival.wait_recv()
            onward.start()
        refs[-1][...] = jnp.zeros_like(refs[-1])

    dma = pltpu.SemaphoreType.DMA
    outs = pl.pallas_call(
        body, name=name,
        out_shape=(*[pltpu.HBM(y.shape, y.dtype) for y in h["y"]], dma((3 * n,)), dma((3 * n,)),
                   jax.ShapeDtypeStruct((8, LANES), F32)),
        in_specs=[_HBM_SPEC] * n + [_SEM, _HBM],
        out_specs=(*[_HBM_SPEC] * n, _SEM, _SEM, pl.BlockSpec(memory_space=pltpu.VMEM)),
        input_output_aliases={i: i for i in range(n)},
        compiler_params=_EFFECT,
    )(*h["y"], h["ici_recv"], after)
    return dict(h, y=list(outs[:n]), fwd_send=outs[n], fwd_recv=outs[n + 1], token=outs[n + 2])


def _gather_wait(h, after, name):
    n = len(h["y"])

    def body(*refs):
        x_refs, y_refs = refs[:n], refs[n:2 * n]
        send, sib_recv, loc, fwd_send, fwd_recv = refs[2 * n:2 * n + 5]
        local, remote = _gather_first_copies(x_refs, y_refs, send, sib_recv, fwd_recv, loc)
        for cp in local:
            cp.wait()
        for k, cp in enumerate(remote):
            cp.wait_send()
            if k % 4 == 0:
                cp.wait_recv()
        for _, onward in _gather_forward_copies(y_refs, fwd_recv, fwd_send, fwd_recv):
            onward.wait_send()
            onward.wait_recv()

    outs = pl.pallas_call(
        body, name=name,
        out_shape=[pltpu.HBM(t.shape, t.dtype) for t in h["x"] + h["y"]],
        in_specs=[_HBM_SPEC] * (2 * n) + [_SEM] * 5 + [_HBM],
        out_specs=[_HBM_SPEC] * (2 * n),
        input_output_aliases={i: i for i in range(2 * n)},
        compiler_params=_EFFECT,
    )(*h["x"], *h["y"], h["send"], h["sib_recv"], h["loc"], h["fwd_send"], h["fwd_recv"], after)
    return list(outs[n:])


def _mm(pairs, mode, out_dtype, name, tm=MM_TILE, tn=MM_TILE, n_cols=None):
    dims = {"nn": NN, "nt": NT, "tn": TN}[mode]
    a0, b0 = pairs[0][0], pairs[0][1]
    M = a0.shape[1] if mode == "tn" else a0.shape[0]
    N = n_cols or (b0.shape[0] if mode == "nt" else b0.shape[1])
    tm, tn = _tile(M, tm), _tile(N, tn)
    in_specs, args = [], []
    for pr in pairs:
        a, b = pr[0], pr[1]
        if mode == "tn":
            K = a.shape[0]
            in_specs.append(pl.BlockSpec((K, tm), lambda i, j: (0, i)))
            in_specs.append(pl.BlockSpec((K, tn), lambda i, j: (0, j)))
        elif mode == "nn":
            K = a.shape[1]
            rb = pr[2] if len(pr) > 2 else 0
            in_specs.append(pl.BlockSpec((tm, K), lambda i, j: (i, 0)))
            in_specs.append(pl.BlockSpec((K, tn), functools.partial(lambda i, j, rb: (rb, j), rb=rb)))
        else:
            K = a.shape[1]
            cb = pr[2] if len(pr) > 2 else 0
            in_specs.append(pl.BlockSpec((tm, K), lambda i, j: (i, 0)))
            in_specs.append(pl.BlockSpec((tn, K), functools.partial(lambda i, j, cb: (j, cb), cb=cb)))
        args += [a, b]
    n_pairs = len(pairs)

    def body(*refs):
        o_ref = refs[-1]
        acc = None
        for k in range(n_pairs):
            d = _dot(refs[2 * k][...], refs[2 * k + 1][...], dims)
            acc = d if acc is None else acc + d
        o_ref[...] = acc.astype(o_ref.dtype)

    return pl.pallas_call(
        body, name=name, grid=(M // tm, N // tn), in_specs=in_specs,
        out_specs=pl.BlockSpec((tm, tn), lambda i, j: (i, j)),
        out_shape=jax.ShapeDtypeStruct((M, N), out_dtype),
        compiler_params=_params(("parallel", "arbitrary")),
    )(*args)


def _dw_stack(a3, b, name):
    n, T, A = a3.shape
    D = b.shape[1]

    def body(a_ref, b_ref, o_ref):
        o_ref[...] = _dot(a_ref[...], b_ref[...], TN)

    return pl.pallas_call(
        body, name=name, grid=(n,),
        in_specs=[pl.BlockSpec((None, T, A), lambda k: (k, 0, 0)), pl.BlockSpec((T, D), lambda k: (0, 0))],
        out_specs=pl.BlockSpec((A, D), lambda k: (k, 0)),
        out_shape=jax.ShapeDtypeStruct((n * A, D), F32),
        compiler_params=_params(("parallel",)),
    )(a3, b)


def _prenorm(x, g, sc, sh, name, first=()):
    T, D = x.shape
    tm = _tile(T, 512, 8)

    def body(x_ref, g_ref, sc_ref, sh_ref, *rest):
        h_ref = rest[-1]
        xv = x_ref[...]
        r = lax.rsqrt(jnp.mean(xv * xv, axis=-1, keepdims=True) + EPS)
        h_ref[...] = (((xv * r) * g_ref[...]) * (1.0 + sc_ref[...]) + sh_ref[...]).astype(BF16)

    row = pl.BlockSpec((1, D), lambda i: (0, 0))
    return pl.pallas_call(
        body, name=name, grid=(T // tm,),
        in_specs=[pl.BlockSpec((tm, D), lambda i: (i, 0)), row, row, row] + [_HBM] * len(first),
        out_specs=pl.BlockSpec((tm, D), lambda i: (i, 0)),
        out_shape=jax.ShapeDtypeStruct((T, D), BF16),
        compiler_params=_params(("parallel",)),
    )(x, g, sc, sh, *first)


def _prenorm_bwd(pairs, x, dres, g, sc, name, first=()):
    T, D = x.shape
    tm = _tile(T, MM_ROWS_SMALL, 8)
    n_pairs = len(pairs)
    pair_specs, pair_args = [], []
    for pr in pairs:
        a, b = pr[0], pr[1]
        K = a.shape[1]
        rb = pr[2] if len(pr) > 2 else 0
        pair_specs += [pl.BlockSpec((tm, K), lambda i: (i, 0)),
                       pl.BlockSpec((K, D), functools.partial(lambda i, rb: (rb, 0), rb=rb), pipeline_mode=RESIDENT)]
        pair_args += [a, b]

    def body(*refs):
        x_ref, dres_ref, g_ref, sc_ref = refs[2 * n_pairs:2 * n_pairs + 4]
        dx_ref, sm_ref = refs[-2:]

        @pl.when(pl.program_id(0) == 0)
        def _():
            sm_ref[...] = jnp.zeros_like(sm_ref)

        dhv = None
        for k in range(n_pairs):
            d = _dot(refs[2 * k][...], refs[2 * k + 1][...], NN)
            dhv = d if dhv is None else dhv + d
        xv = x_ref[...]
        r = lax.rsqrt(jnp.mean(xv * xv, axis=-1, keepdims=True) + EPS)
        xh = xv * r
        one_sc = 1.0 + sc_ref[...]
        sm_ref[0:1, :] += jnp.sum(dhv, axis=0, keepdims=True)
        sm_ref[1:2, :] += jnp.sum(dhv * (xh * g_ref[...]), axis=0, keepdims=True)
        sm_ref[2:3, :] += jnp.sum(dhv * one_sc * xh, axis=0, keepdims=True)
        dxh = dhv * one_sc * g_ref[...]
        dx_ref[...] = dres_ref[...] + r * (dxh - xh * jnp.mean(dxh * xh, axis=-1, keepdims=True))

    row = pl.BlockSpec((1, D), lambda i: (0, 0))
    big = pl.BlockSpec((tm, D), lambda i: (i, 0))
    return pl.pallas_call(
        body, name=name, grid=(T // tm,),
        in_specs=pair_specs + [big, big, row, row] + [_HBM] * len(first),
        out_specs=[big, pl.BlockSpec((8, D), lambda i: (0, 0))],
        out_shape=(jax.ShapeDtypeStruct((T, D), F32), jax.ShapeDtypeStruct((8, D), F32)),
        compiler_params=_params(("arbitrary",)),
    )(*pair_args, x, dres, g, sc, *first)


def _mm_postnorm(a, w, x, gate, gpost, name, nxt=None, first=()):
    T, K = a.shape
    D = w.shape[1]
    tm = _tile(T, MM_ROWS_SMALL, 8)

    def body(a_ref, w_ref, x_ref, gate_ref, gp_ref, *rest):
        y = _dot(a_ref[...], w_ref[...], NN)
        r = lax.rsqrt(jnp.mean(y * y, axis=-1, keepdims=True) + EPS)
        xn = x_ref[...] + gate_ref[...] * ((y * r) * gp_ref[...])
        if nxt is None:
            y_ref, xn_ref = rest[-2:]
        else:
            g_ref, sc_ref, sh_ref = rest[:3]
            y_ref, xn_ref, h_ref = rest[-3:]
            rn = lax.rsqrt(jnp.mean(xn * xn, axis=-1, keepdims=True) + EPS)
            h_ref[...] = (((xn * rn) * g_ref[...]) * (1.0 + sc_ref[...]) + sh_ref[...]).astype(BF16)
        y_ref[...] = y
        xn_ref[...] = xn

    row = pl.BlockSpec((1, D), lambda i: (0, 0))
    big = pl.BlockSpec((tm, D), lambda i: (i, 0))
    extra = () if nxt is None else tuple(nxt)
    return pl.pallas_call(
        body, name=name, grid=(T // tm,),
        in_specs=[pl.BlockSpec((tm, K), lambda i: (i, 0)), pl.BlockSpec((K, D), lambda i: (0, 0)), big, row, row]
        + [row] * len(extra) + [_HBM] * len(first),
        out_specs=[big, big] + [big] * (nxt is not None),
        out_shape=(jax.ShapeDtypeStruct((T, D), F32), jax.ShapeDtypeStruct((T, D), F32))
        + ((jax.ShapeDtypeStruct((T, D), BF16),) if nxt is not None else ()),
        compiler_params=_params(("parallel",)),
    )(a, w, x, gate, gpost, *extra, *first)


def _postnorm_bwd(dx, y, gate, gpost, name):
    T, D = y.shape
    tm = _tile(T, 256, 8)

    def body(dx_ref, y_ref, gate_ref, gp_ref, dy_ref, sm_ref):
        @pl.when(pl.program_id(0) == 0)
        def _():
            sm_ref[...] = jnp.zeros_like(sm_ref)

        yv, dxv = y_ref[...], dx_ref[...]
        r = lax.rsqrt(jnp.mean(yv * yv, axis=-1, keepdims=True) + EPS)
        yh = yv * r
        dn = dxv * gate_ref[...]
        sm_ref[0:1, :] += jnp.sum(dxv * (yh * gp_ref[...]), axis=0, keepdims=True)
        sm_ref[1:2, :] += jnp.sum(dn * yh, axis=0, keepdims=True)
        dyh = dn * gp_ref[...]
        dy_ref[...] = (r * (dyh - yh * jnp.mean(dyh * yh, axis=-1, keepdims=True))).astype(BF16)

    row = pl.BlockSpec((1, D), lambda i: (0, 0))
    big = pl.BlockSpec((tm, D), lambda i: (i, 0))
    return pl.pallas_call(
        body, name=name, grid=(T // tm,),
        in_specs=[big, big, row, row],
        out_specs=[big, pl.BlockSpec((8, D), lambda i: (0, 0))],
        out_shape=(jax.ShapeDtypeStruct((T, D), BF16), jax.ShapeDtypeStruct((8, D), F32)),
        compiler_params=_params(("arbitrary",)),
    )(dx, y, gate, gpost)


def _ffn_in_fwd(h, w_t, name):
    T, D = h.shape
    F = w_t.shape[0] // 2
    tm, tn = _tile(T, MM_ROWS), _tile(F, FFN_COLS)
    nj = F // tn

    def body(h_ref, wg_ref, wu_ref, g_ref, u_ref, act_ref):
        hv = h_ref[...]
        g = _dot(hv, wg_ref[...], NT)
        u = _dot(hv, wu_ref[...], NT)
        g_ref[...] = g.astype(BF16)
        u_ref[...] = u.astype(BF16)
        act_ref[...] = ((g * _sigmoid(g)) * u).astype(BF16)

    out = pl.BlockSpec((tm, tn), lambda j, i: (i, j))
    return pl.pallas_call(
        body, name=name, grid=(nj, T // tm),
        in_specs=[pl.BlockSpec((tm, D), lambda j, i: (i, 0)),
                  pl.BlockSpec((tn, D), lambda j, i: (j, 0)),
                  pl.BlockSpec((tn, D), lambda j, i: (j + nj, 0))],
        out_specs=[out, out, out],
        out_shape=tuple(jax.ShapeDtypeStruct((T, F), BF16) for _ in range(3)),
        compiler_params=_params(("parallel", "arbitrary")),
    )(h, w_t, w_t)


def _ffn_act_bwd(dy, w_out, g, u, name):
    T, D = dy.shape
    F = w_out.shape[0]
    tm, tn = _tile(T, MM_ROWS), _tile(F, FFN_COLS)

    def body(dy_ref, w_ref, g_ref, u_ref, dg_ref, du_ref):
        dact = _dot(dy_ref[...], w_ref[...], NT)
        gv, uv = g_ref[...].astype(F32), u_ref[...].astype(F32)
        sg = _sigmoid(gv)
        dg_ref[...] = (dact * uv * (sg * (1.0 + gv * (1.0 - sg)))).astype(BF16)
        du_ref[...] = (dact * (gv * sg)).astype(BF16)

    tile = pl.BlockSpec((tm, tn), lambda j, i: (i, j))
    return pl.pallas_call(
        body, name=name, grid=(F // tn, T // tm),
        in_specs=[pl.BlockSpec((tm, D), lambda j, i: (i, 0)), pl.BlockSpec((tn, D), lambda j, i: (j, 0)), tile, tile],
        out_specs=[tile, tile],
        out_shape=(jax.ShapeDtypeStruct((T, F), BF16), jax.ShapeDtypeStruct((T, F), BF16)),
        compiler_params=_params(("parallel", "arbitrary")),
    )(dy, w_out, g, u)


def _lane_scan(v, reverse):
    T = v.shape[-1]
    lane = lax.broadcasted_iota(jnp.int32, v.shape, 1)
    d = 1
    while d < T:
        if reverse:
            v = v + jnp.where(lane < T - d, pltpu.roll(v, T - d, axis=1), 0.0)
        else:
            v = v + jnp.where(lane >= d, pltpu.roll(v, d, axis=1), 0.0)
        d *= 2
    return v


def _fgate_fwd(h, wf_t, bf, name):
    T, D = h.shape
    R = wf_t.shape[0]

    def body(h_ref, w_ref, b_ref, fl_ref, cum_ref):
        fl = _dot(w_ref[...], h_ref[...], NT) + b_ref[...]
        fl_ref[...] = fl
        logf = jnp.minimum(fl, 0.0) - jnp.log(1.0 + jnp.exp(-jnp.abs(fl)))
        cum_ref[...] = _lane_scan(logf, reverse=False)

    return pl.pallas_call(
        body, name=name,
        out_shape=(jax.ShapeDtypeStruct((R, T), F32), jax.ShapeDtypeStruct((R, T), F32)),
        compiler_params=_params(),
    )(h, wf_t, bf)


def _fgate_bwd(dcum, fl, h, name):
    R, T = fl.shape
    D = h.shape[1]

    def body(dc_ref, fl_ref, h_ref, dfl_ref, dw_ref, db_ref):
        dlogf = _lane_scan(dc_ref[...], reverse=True)
        dfl = dlogf * _sigmoid(-fl_ref[...])
        dfl_ref[...] = dfl
        dw_ref[...] = _dot(dfl.astype(BF16), h_ref[...], NN)
        db_ref[...] = jnp.broadcast_to(jnp.sum(dfl, axis=-1, keepdims=True), (R, LANES))

    return pl.pallas_call(
        body, name=name,
        out_shape=(jax.ShapeDtypeStruct((R, T), F32), jax.ShapeDtypeStruct((R, D), F32),
                   jax.ShapeDtypeStruct((R, LANES), F32)),
        compiler_params=_params(),
    )(dcum, fl, h)


def _head_masks(hpb, dh, rows):
    lane = lax.broadcasted_iota(jnp.int32, (rows, LANES), 1)
    return [(lane >= h * dh) & (lane < (h + 1) * dh) for h in range(hpb)]


def _stack_heads(v, masks):
    return jnp.concatenate([jnp.where(mk, v, jnp.zeros_like(v)) for mk in masks], axis=0)


def _heads_to_lanes(col, masks, tq):
    out = jnp.broadcast_to(col[0:tq], (tq, LANES))
    for h in range(1, len(masks)):
        out = jnp.where(masks[h], col[h * tq:(h + 1) * tq], out)
    return out


def _causal_stack(hpb, tq):
    r = lax.broadcasted_iota(jnp.int32, (tq, tq), 0)
    c = lax.broadcasted_iota(jnp.int32, (tq, tq), 1)
    return jnp.concatenate([c] * hpb, axis=0) <= jnp.concatenate([r] * hpb, axis=0)


def _attn_fwd(qkv, cum_rows, n_heads, name, tq, first=()):
    T = qkv.shape[0]
    A = qkv.shape[1] // 3
    dh = A // n_heads
    hpb = LANES // dh
    nb = A // LANES
    nq = T // tq
    scale = dh ** -0.5

    def body(q_ref, k_ref, v_ref, c_ref, *rest):
        o_ref, l_ref, vbd = rest[-3:]
        hp, i = pl.program_id(0), pl.program_id(1)
        masks = _head_masks(hpb, dh, tq)

        @pl.when(i == 0)
        def _():
            def fill(j, _):
                vbd[j] = _stack_heads(v_ref[pl.ds(pl.multiple_of(j * tq, tq), tq), :], masks)
                return 0

            lax.fori_loop(0, nq, fill, 0)

        qs = _stack_heads(q_ref[...], masks)
        crow0 = hp * hpb * nq

        def tile(j, carry, diag):
            m, l, acc = carry
            kt = k_ref[pl.ds(pl.multiple_of(j * tq, tq), tq), :]
            bias = jnp.concatenate(
                [jnp.broadcast_to(c_ref[pl.ds(crow0 + h * nq + j, 1), :], (tq, tq)) for h in range(hpb)], axis=0)
            s = _dot(qs, kt, NT) * scale - bias
            if diag:
                s = jnp.where(_causal_stack(hpb, tq), s, NEG)
            m_new = jnp.maximum(m, jnp.max(s, axis=-1, keepdims=True))
            p = jnp.exp(s - m_new)
            alpha = jnp.exp(m - m_new)
            l = alpha * l + jnp.sum(p, axis=-1, keepdims=True)
            pcat = jnp.concatenate([p[h * tq:(h + 1) * tq] for h in range(hpb)], axis=1).astype(BF16)
            acc = _heads_to_lanes(alpha, masks, tq) * acc + _dot(pcat, vbd[j], NN)
            return m_new, l, acc

        init = (jnp.full((hpb * tq, 1), NEG, F32), jnp.zeros((hpb * tq, 1), F32), jnp.zeros((tq, LANES), F32))
        carry = lax.fori_loop(0, i, lambda j, c: tile(j, c, False), init)
        m, l, acc = tile(i, carry, True)
        o_ref[...] = (acc / _heads_to_lanes(l, masks, tq)).astype(BF16)
        lse = m + jnp.log(l)
        for h in range(hpb):
            l_ref[:, h:h + 1] = lse[h * tq:(h + 1) * tq]

    return pl.pallas_call(
        body, name=name, grid=(nb, nq),
        in_specs=[pl.BlockSpec((tq, LANES), lambda h, i: (i, h)),
                  pl.BlockSpec((T, LANES), lambda h, i: (0, nb + h)),
                  pl.BlockSpec((T, LANES), lambda h, i: (0, 2 * nb + h)),
                  pl.BlockSpec(cum_rows.shape, lambda h, i: (0, 0))] + [_HBM] * len(first),
        out_specs=[pl.BlockSpec((tq, LANES), lambda h, i: (i, h)),
                   pl.BlockSpec((None, tq, hpb), lambda h, i: (h, i, 0))],
        out_shape=(jax.ShapeDtypeStruct((T, A), BF16), jax.ShapeDtypeStruct((nb, T, hpb), F32)),
        scratch_shapes=[pltpu.VMEM((nq, hpb * tq, LANES), BF16)],
        compiler_params=_params(("arbitrary", "arbitrary")),
    )(qkv, qkv, qkv, cum_rows, *first)


def _attn_bwd(qkv, dcat, o, lse, cum_rows, n_heads, name, tq):
    T = qkv.shape[0]
    A = qkv.shape[1] // 3
    dh = A // n_heads
    hpb = LANES // dh
    nb = A // LANES
    nq = T // tq
    scale = dh ** -0.5

    def body(q_ref, k_ref, v_ref, do_ref, o_ref, l_ref, c_ref, dqkv_ref, dc_ref, dr_ref,
             dq_acc, delta, drow, qs_scr, dos_scr, kbd_scr):
        hp = pl.program_id(0)
        masks = _head_masks(hpb, dh, tq)
        crow0 = hp * hpb * nq

        def prologue(i, _):
            rs = pl.ds(pl.multiple_of(i * tq, tq), tq)
            do = do_ref[rs, :]
            prod = do * o_ref[rs, :].astype(F32)
            for h in range(hpb):
                delta[rs, h:h + 1] = jnp.sum(jnp.where(masks[h], prod, 0.0), axis=-1, keepdims=True)
            qs_scr[i] = _stack_heads(q_ref[rs, :], masks)
            dos_scr[i] = _stack_heads(do, masks).astype(BF16)
            kbd_scr[i] = _stack_heads(k_ref[rs, :], masks)
            dq_acc[rs, :] = jnp.zeros((tq, LANES), F32)
            drow[rs, :] = jnp.zeros((tq, hpb), F32)
            return 0

        lax.fori_loop(0, nq, prologue, 0)

        def kv_step(j, _):
            ks = pl.ds(pl.multiple_of(j * tq, tq), tq)
            kt, vt = k_ref[ks, :], v_ref[ks, :]
            kbd = kbd_scr[j]
            bias = jnp.concatenate(
                [jnp.broadcast_to(c_ref[pl.ds(crow0 + h * nq + j, 1), :], (tq, tq)) for h in range(hpb)], axis=0)

            def q_step(i, carry, diag):
                dk, dv, dcs = carry
                rs = pl.ds(pl.multiple_of(i * tq, tq), tq)
                qs, dos = qs_scr[i], dos_scr[i]
                s = _dot(qs, kt, NT) * scale - bias
                if diag:
                    s = jnp.where(_causal_stack(hpb, tq), s, NEG)
                lse = jnp.concatenate([l_ref[rs, h:h + 1] for h in range(hpb)], axis=0)
                p = jnp.exp(s - lse)
                dv = dv + _dot(p.astype(BF16), dos, TN)
                dp = _dot(dos, vt, NT)
                ds = p * (dp - jnp.concatenate([delta[rs, h:h + 1] for h in range(hpb)], axis=0))
                dcs = tuple(dcs[h] - jnp.sum(ds[h * tq:(h + 1) * tq], axis=0, keepdims=True) for h in range(hpb))
                rsum = jnp.sum(ds, axis=-1, keepdims=True)
                for h in range(hpb):
                    drow[rs, h:h + 1] += rsum[h * tq:(h + 1) * tq]
                dsb = (ds * scale).astype(BF16)
                dk = dk + _dot(dsb, qs, TN)
                dscat = jnp.concatenate([dsb[h * tq:(h + 1) * tq] for h in range(hpb)], axis=1)
                dq_acc[rs, :] += _dot(dscat, kbd, NN)
                return dk, dv, dcs

            init = (jnp.zeros((tq, LANES), F32), jnp.zeros((tq, LANES), F32),
                    tuple(jnp.zeros((1, tq), F32) for _ in range(hpb)))
            carry = q_step(j, init, True)
            dk, dv, dcs = lax.fori_loop(j + 1, nq, lambda i, c: q_step(i, c, False), carry)
            dqkv_ref[1, ks, :] = dk.astype(BF16)
            dqkv_ref[2, ks, :] = dv.astype(BF16)
            for h in range(hpb):
                dc_ref[pl.ds(crow0 + h * nq + j, 1), :] = dcs[h]
            return 0

        lax.fori_loop(0, nq, kv_step, 0)
        dqkv_ref[0] = dq_acc[...].astype(BF16)
        dr_ref[...] = drow[...]

    col = lambda off: pl.BlockSpec((T, LANES), functools.partial(lambda h, off: (0, off + h), off=off))
    return pl.pallas_call(
        body, name=name, grid=(nb,),
        in_specs=[col(0), col(nb), col(2 * nb), col(0), col(0),
                  pl.BlockSpec((None, T, hpb), lambda h: (h, 0, 0)),
                  pl.BlockSpec(cum_rows.shape, lambda h: (0, 0))],
        out_specs=[pl.BlockSpec((3, T, LANES), lambda h: (0, 0, h)),
                   pl.BlockSpec(cum_rows.shape, lambda h: (0, 0)),
                   pl.BlockSpec((None, T, hpb), lambda h: (h, 0, 0))],
        out_shape=(jax.ShapeDtypeStruct((3, T, A), BF16), jax.ShapeDtypeStruct(cum_rows.shape, F32),
                   jax.ShapeDtypeStruct((nb, T, hpb), F32)),
        scratch_shapes=[pltpu.VMEM((T, LANES), F32), pltpu.VMEM((T, hpb), F32), pltpu.VMEM((T, hpb), F32),
                        pltpu.VMEM((nq, hpb * tq, LANES), BF16), pltpu.VMEM((nq, hpb * tq, LANES), BF16),
                        pltpu.VMEM((nq, hpb * tq, LANES), BF16)],
        compiler_params=_params(("arbitrary",)),
    )(qkv, qkv, qkv, dcat, o, lse, cum_rows)


def _glu_into(upad, cv_ref, cg_ref, T):
    upad[0:CONV_PAD, :] = jnp.zeros((CONV_PAD, upad.shape[1]), F32)

    def fill(c, _):
        rs = pl.ds(pl.multiple_of(c * CONV_CHUNK, CONV_CHUNK), CONV_CHUNK)
        upad[pl.ds(pl.multiple_of(CONV_PAD + c * CONV_CHUNK, 8), CONV_CHUNK), :] = cv_ref[rs, :] * _sigmoid(cg_ref[rs, :])
        return 0

    lax.fori_loop(0, T // CONV_CHUNK, fill, 0)


SUBLANES = 8
CONV_SHIFT_ROWS = CONV_CHUNK + CONV_PAD - SUBLANES


def _load_window(win, sh, src, r0):
    win[...] = src[pl.ds(r0, CONV_CHUNK + CONV_PAD), :]
    for b in range(1, SUBLANES):
        sh[b - 1] = win[b:b + CONV_SHIFT_ROWS, :]


def _tap(win, sh, o):
    b = o % SUBLANES
    if b == 0:
        return win[o:o + CONV_CHUNK, :]
    return sh[b - 1, o - b:o - b + CONV_CHUNK, :]


def _conv_taps(win, sh, w_ref, first, step):
    acc = None
    for k in range(CONV_K):
        t = w_ref[k:k + 1, :] * _tap(win, sh, first + step * k)
        acc = t if acc is None else acc + t
    return acc


def _conv_scratch(C):
    return [pltpu.VMEM((CONV_CHUNK + CONV_PAD, C), F32), pltpu.VMEM((SUBLANES - 1, CONV_SHIFT_ROWS, C), F32)]


def _conv_fwd(cproj, w, b, lg, lb, name):
    T = cproj.shape[0]
    C = cproj.shape[1] // 2
    off = CONV_PAD - (CONV_K - 1)

    def body(cv_ref, cg_ref, w_ref, b_ref, lg_ref, lb_ref, out_ref, u1_ref, upad, win, sh):
        _glu_into(upad, cv_ref, cg_ref, T)

        def chunk(c, _):
            r0 = pl.multiple_of(c * CONV_CHUNK, CONV_CHUNK)
            _load_window(win, sh, upad, r0)
            u1 = _conv_taps(win, sh, w_ref, off, 1) + b_ref[...]
            u1_ref[pl.ds(r0, CONV_CHUNK), :] = u1
            mu = jnp.mean(u1, axis=-1, keepdims=True)
            var = jnp.mean(jnp.square(u1 - mu), axis=-1, keepdims=True)
            u2 = ((u1 - mu) * lax.rsqrt(var + EPS)) * lg_ref[...] + lb_ref[...]
            out_ref[pl.ds(r0, CONV_CHUNK), :] = (u2 * _sigmoid(u2)).astype(BF16)
            return 0

        lax.fori_loop(0, T // CONV_CHUNK, chunk, 0)

    row = pl.BlockSpec((1, C), lambda i: (0, 0))
    return pl.pallas_call(
        body, name=name, grid=(1,),
        in_specs=[pl.BlockSpec((T, C), lambda i: (0, 0)), pl.BlockSpec((T, C), lambda i: (0, 1)),
                  pl.BlockSpec(w.shape, lambda i: (0, 0)), row, row, row],
        out_specs=[pl.BlockSpec((T, C), lambda i: (0, 0)), pl.BlockSpec((T, C), lambda i: (0, 0))],
        out_shape=(jax.ShapeDtypeStruct((T, C), BF16), jax.ShapeDtypeStruct((T, C), F32)),
        scratch_shapes=[pltpu.VMEM((T + CONV_PAD, C), F32)] + _conv_scratch(C),
        compiler_params=_params(("arbitrary",)),
    )(cproj, cproj, w, b, lg, lb)


def _conv_bwd(cproj, u1_saved, dcat, w, lg, lb, name):
    T = cproj.shape[0]
    C = cproj.shape[1] // 2
    off = CONV_PAD - (CONV_K - 1)
    n_chunks = T // CONV_CHUNK

    def fold(v):
        return jnp.sum(v.reshape(CONV_CHUNK // 8, 8, C), axis=0)

    def body(cv_ref, cg_ref, u1_ref, du_ref, w_ref, lg_ref, lb_ref, dc_ref, dw_ref, sm_ref,
             upad, dpad, dwacc, smacc, win, sh):
        _glu_into(upad, cv_ref, cg_ref, T)
        dpad[pl.ds(T, CONV_PAD), :] = jnp.zeros((CONV_PAD, C), F32)
        dwacc[...] = jnp.zeros_like(dwacc)
        smacc[...] = jnp.zeros_like(smacc)

        def chunk_a(c, _):
            r0 = pl.multiple_of(c * CONV_CHUNK, CONV_CHUNK)
            _load_window(win, sh, upad, r0)
            u1 = u1_ref[pl.ds(r0, CONV_CHUNK), :]
            mu = jnp.mean(u1, axis=-1, keepdims=True)
            var = jnp.mean(jnp.square(u1 - mu), axis=-1, keepdims=True)
            rstd = lax.rsqrt(var + EPS)
            u1h = (u1 - mu) * rstd
            u2 = u1h * lg_ref[...] + lb_ref[...]
            sg = _sigmoid(u2)
            du2 = du_ref[pl.ds(r0, CONV_CHUNK), :] * (sg * (1.0 + u2 * (1.0 - sg)))
            smacc[8:16, :] += fold(du2 * u1h)
            smacc[16:24, :] += fold(du2)
            du1h = du2 * lg_ref[...]
            du1 = rstd * (du1h - jnp.mean(du1h, axis=-1, keepdims=True)
                          - u1h * jnp.mean(du1h * u1h, axis=-1, keepdims=True))
            smacc[0:8, :] += fold(du1)
            dpad[pl.ds(r0, CONV_CHUNK), :] = du1
            for k in range(CONV_K):
                dwacc[8 * k:8 * k + 8, :] += fold(du1 * _tap(win, sh, off + k))
            return 0

        lax.fori_loop(0, n_chunks, chunk_a, 0)

        def chunk_b(c, _):
            r0 = pl.multiple_of(c * CONV_CHUNK, CONV_CHUNK)
            rs = pl.ds(r0, CONV_CHUNK)
            _load_window(win, sh, dpad, r0)
            du0 = _conv_taps(win, sh, w_ref, CONV_K - 1, -1)
            cv, sg = cv_ref[rs, :], _sigmoid(cg_ref[rs, :])
            dc_ref[rs, 0:C] = (du0 * sg).astype(BF16)
            dc_ref[rs, C:2 * C] = (du0 * cv * (sg * (1.0 - sg))).astype(BF16)
            return 0

        lax.fori_loop(0, n_chunks, chunk_b, 0)
        for k in range(CONV_K):
            dw_ref[k:k + 1, :] = jnp.sum(dwacc[8 * k:8 * k + 8, :], axis=0, keepdims=True)
        dw_ref[CONV_K:CONV_PAD, :] = jnp.zeros((CONV_PAD - CONV_K, C), F32)
        for r in range(3):
            sm_ref[r:r + 1, :] = jnp.sum(smacc[8 * r:8 * r + 8, :], axis=0, keepdims=True)
        sm_ref[3:8, :] = jnp.zeros((5, C), F32)

    row = pl.BlockSpec((1, C), lambda i: (0, 0))
    return pl.pallas_call(
        body, name=name, grid=(1,),
        in_specs=[pl.BlockSpec((T, C), lambda i: (0, 0)), pl.BlockSpec((T, C), lambda i: (0, 1)),
                  pl.BlockSpec((T, C), lambda i: (0, 0)), pl.BlockSpec((T, C), lambda i: (0, 1)),
                  pl.BlockSpec(w.shape, lambda i: (0, 0)), row, row],
        out_specs=[pl.BlockSpec((T, 2 * C), lambda i: (0, 0)), pl.BlockSpec((CONV_PAD, C), lambda i: (0, 0)),
                   pl.BlockSpec((8, C), lambda i: (0, 0))],
        out_shape=(jax.ShapeDtypeStruct((T, 2 * C), BF16), jax.ShapeDtypeStruct((CONV_PAD, C), F32),
                   jax.ShapeDtypeStruct((8, C), F32)),
        scratch_shapes=[pltpu.VMEM((T + CONV_PAD, C), F32), pltpu.VMEM((T + CONV_PAD, C), F32),
                        pltpu.VMEM((8 * CONV_PAD, C), F32), pltpu.VMEM((24, C), F32)] + _conv_scratch(C),
        compiler_params=_params(("arbitrary",)),
    )(cproj, cproj, u1_saved, dcat, w, lg, lb)


def _loss_head(x, target, name):
    T, D = x.shape
    tm = _tile(T, 512, 8)

    def body(x_ref, t_ref, loss_ref, dx_ref):
        @pl.when(pl.program_id(0) == 0)
        def _():
            loss_ref[...] = jnp.zeros_like(loss_ref)

        err = x_ref[...] - t_ref[...]
        part = jnp.sum(jnp.mean(err * err, axis=-1, keepdims=True), axis=0, keepdims=True)
        loss_ref[...] += jnp.broadcast_to(0.5 * part, loss_ref.shape)
        dx_ref[...] = err * (1.0 / D)

    big = pl.BlockSpec((tm, D), lambda i: (i, 0))
    return pl.pallas_call(
        body, name=name, grid=(T // tm,),
        in_specs=[big, big],
        out_specs=[pl.BlockSpec((8, LANES), lambda i: (0, 0)), big],
        out_shape=(jax.ShapeDtypeStruct((8, LANES), F32), jax.ShapeDtypeStruct((T, D), F32)),
        compiler_params=_params(("arbitrary",)),
    )(x, target)


def _ada_fwd(c_all, ada_w, ada_b_loc, name):
    L, D, S = ada_w.shape
    B = c_all.shape[0]

    def body(c_ref, w_ref, b_ref, o_ref):
        c = c_ref[...]
        ca = (c * _sigmoid(c)).astype(BF16)
        o_ref[...] = _dot(ca, w_ref[...].astype(BF16), NN) + b_ref[...]

    return pl.pallas_call(
        body, name=name, grid=(L,),
        in_specs=[pl.BlockSpec((B, D), lambda l: (0, 0)), pl.BlockSpec((None, D, S), lambda l: (l, 0, 0)),
                  pl.BlockSpec((None, 1, S), lambda l: (l, 0, 0))],
        out_specs=pl.BlockSpec((None, B, S), lambda l: (l, 0, 0)),
        out_shape=jax.ShapeDtypeStruct((L, B, S), F32),
        compiler_params=_params(("parallel",)),
    )(c_all, ada_w, ada_b_loc)


def _ada_bwd(c_all_t, dmod_loc, name):
    D, B = c_all_t.shape
    L, _, S = dmod_loc.shape

    def body(c_ref, dm_ref, o_ref):
        c = c_ref[...]
        ca = c * _sigmoid(c)
        acc = None
        for bb in range(B):
            t = ca[:, bb:bb + 1] * dm_ref[bb:bb + 1, :]
            acc = t if acc is None else acc + t
        o_ref[...] = acc

    return pl.pallas_call(
        body, name=name, grid=(L,),
        in_specs=[pl.BlockSpec((D, B), lambda l: (0, 0)), pl.BlockSpec((None, B, S), lambda l: (l, 0, 0))],
        out_specs=pl.BlockSpec((None, D, S), lambda l: (l, 0, 0)),
        out_shape=jax.ShapeDtypeStruct((L, D, S), F32),
        compiler_params=_params(("parallel",)),
    )(c_all_t, dmod_loc)


def _sum_devices(parts, name):
    _, R, C = parts.shape
    tm = _tile(R, 256, 8)

    def body(p_ref, o_ref):
        acc = p_ref[0].astype(F32)
        for d in range(1, N_DEV):
            acc = acc + p_ref[d].astype(F32)
        o_ref[...] = acc

    return pl.pallas_call(
        body, name=name, grid=(R // tm,),
        in_specs=[pl.BlockSpec((N_DEV, tm, C), lambda i: (0, i, 0))],
        out_specs=pl.BlockSpec((tm, C), lambda i: (i, 0)),
        out_shape=jax.ShapeDtypeStruct((R, C), F32),
        compiler_params=_params(("parallel",)),
    )(parts)


def _adamw_math(w, g, m, v):
    m = ADAM_B1 * m + (1.0 - ADAM_B1) * g
    v = ADAM_B2 * v + (1.0 - ADAM_B2) * (g * g)
    m_hat = m / (1.0 - ADAM_B1 ** ADAM_STEP)
    v_hat = v / (1.0 - ADAM_B2 ** ADAM_STEP)
    delta = -ADAM_LR * (m_hat / (jnp.sqrt(v_hat) + ADAM_EPS) + ADAM_WD * w)
    return delta, m, v


def _adamw(w, g, m, v, name, summed):
    R, C = w.shape
    tm = _tile(R, 256, 16)
    n_parts = g.shape[0] if summed else 0

    def body(w_ref, g_ref, m_ref, v_ref, go_ref, d_ref, mo_ref, vo_ref):
        if summed:
            g = g_ref[0].astype(F32)
            for d in range(1, n_parts):
                g = g + g_ref[d].astype(F32)
        else:
            g = g_ref[...]
        delta, mn, vn = _adamw_math(w_ref[...], g, m_ref[...], v_ref[...])
        go_ref[...] = g
        d_ref[...] = delta
        mo_ref[...] = mn
        vo_ref[...] = vn

    big = pl.BlockSpec((tm, C), lambda i: (i, 0))
    gspec = pl.BlockSpec((n_parts, tm, C), lambda i: (0, i, 0)) if summed else big
    return pl.pallas_call(
        body, name=name, grid=(R // tm,),
        in_specs=[big, gspec, big, big],
        out_specs=[big, big, big, big],
        out_shape=tuple(jax.ShapeDtypeStruct((R, C), F32) for _ in range(4)),
        compiler_params=_params(("parallel",)),
    )(w, g, m, v)


def _adamw_summed(w, parts, m, v, name):
    L = len(parts)
    n_parts, R, C = parts[0].shape
    tm = _tile(R, 256, 16)
    tc = C if tm < R else _tile(C, 256)
    nr = (R // tm) * (C // tc)
    ncb = C // tc

    def body(*refs):
        w_ref, g_refs = refs[0], refs[1:1 + L]
        m_ref, v_ref, go_ref, d_ref, mo_ref, vo_ref = refs[1 + L:]
        for ll in range(L):
            @pl.when(pl.program_id(0) == ll)
            def _(ll=ll):
                g = g_refs[ll][0].astype(F32)
                for d in range(1, n_parts):
                    g = g + g_refs[ll][d].astype(F32)
                delta, mn, vn = _adamw_math(w_ref[...], g, m_ref[...], v_ref[...])
                go_ref[...] = g
                d_ref[...] = delta
                mo_ref[...] = mn
                vo_ref[...] = vn

    big = pl.BlockSpec((None, tm, tc), lambda l, i: (l, i // ncb, i % ncb))

    def part_index(l, i, ll):
        i = jnp.where(l == ll, i, 0)
        return 0, i // ncb, i % ncb

    gspecs = [pl.BlockSpec((n_parts, tm, tc), functools.partial(part_index, ll=ll)) for ll in range(L)]
    return pl.pallas_call(
        body, name=name, grid=(L, nr),
        in_specs=[big, *gspecs, big, big],
        out_specs=[big, big, big, big],
        out_shape=tuple(jax.ShapeDtypeStruct((L, R, C), F32) for _ in range(4)),
        compiler_params=_params(("arbitrary", "arbitrary")),
    )(w, *parts, m, v)


def _pack(arrs, D):
    L = arrs[0].shape[0]
    cols = []
    for a in arrs:
        f = a.reshape(L, -1)
        n = f.shape[1]
        cols.append(jnp.pad(f, ((0, 0), (0, -(-n // D) * D - n))))
    flat = jnp.concatenate(cols, axis=1)
    return flat.reshape(L, flat.shape[1] // D, D)


def _unpack(p, shapes, D):
    L = p.shape[0]
    out, r = [], 0
    for s in shapes:
        n = math.prod(s[1:])
        rows = -(-n // D)
        out.append(p[:, r:r + rows].reshape(L, rows * D)[:, :n].reshape(s))
        r += rows
    return out


def kernel(x, c, w_in, b_f, conv_w, conv_b, conv_ln_g, conv_ln_b, w_o, w_ffn_in, w_ffn_out, mix_pre_g, mix_post_g, ffn_pre_g, ffn_post_g, ada_w, ada_b, loss_target, m_w_in, m_b_f, m_conv_w, m_conv_b, m_conv_ln_g, m_conv_ln_b, m_w_o, m_w_ffn_in, m_w_ffn_out, m_mix_pre_g, m_mix_post_g, m_ffn_pre_g, m_ffn_post_g, m_ada_w, m_ada_b, v_w_in, v_b_f, v_conv_w, v_conv_b, v_conv_ln_g, v_conv_ln_b, v_w_o, v_w_ffn_in, v_w_ffn_out, v_mix_pre_g, v_mix_post_g, v_ffn_pre_g, v_ffn_post_g, v_ada_w, v_ada_b):
    L, D, s_in = w_in.shape
    T = x.shape[1]
    H = b_f.shape[1]
    A = D // 2
    C = D - A
    cs = conv_w.shape[2]
    F = w_ffn_out.shape[1] * N_DEV
    s_ff = w_ffn_in.shape[2]
    w_ffn_in_t = jnp.transpose(w_ffn_in, (0, 2, 1))
    w_in_t = jnp.transpose(w_in, (0, 2, 1))
    s_ada = ada_w.shape[2]
    R = 16
    me = _my_index()
    x0 = x[0]
    target = loss_target[0]
    tq = _tile(T, 512)
    nq = T // tq

    c_all = _exchange(c, gather=True, name="gather_c").reshape(N_DEV, D)
    ada_b_loc = lax.dynamic_slice_in_dim(ada_b, me * s_ada, s_ada, axis=1)[:, None, :]
    mod_loc = _ada_fwd(c_all, ada_w, ada_b_loc, "ada_fwd")
    mod_g = _exchange(mod_loc, gather=True, name="gather_mod")
    mod = lax.dynamic_index_in_dim(mod_g, me, axis=2, keepdims=False)
    mod = jnp.transpose(mod, (1, 0, 2)).reshape(L, N_MOD, 1, D)
    cw_g = _exchange(conv_w, gather=True, name="gather_conv_w")
    conv_w_full = jnp.transpose(cw_g, (1, 2, 0, 3)).reshape(L, CONV_K, C)
    conv_w_pad = jnp.pad(conv_w_full, ((0, 0), (0, CONV_PAD - CONV_K), (0, 0)))
    b_f_col = jnp.pad(b_f, ((0, 0), (0, R - H)))[:, :, None]

    h_gmix, h_gffn, chain = [], [], (mod_g, cw_g)
    for l in range(L):
        h_gmix.append(_gather_start([w_in_t[l].astype(BF16), w_o[l].astype(BF16)], f"gather_mix_{l}", chain))
        h_gffn.append(_gather_start([w_ffn_in_t[l].astype(BF16), w_ffn_out[l].astype(BF16)], f"gather_ffn_{l}",
                                    (h_gmix[l]["token"],)))
        chain = (h_gffn[l]["token"],)
    gather_tokens = chain

    def by_rows(g):
        return g.reshape(N_DEV * g.shape[1], g.shape[2])

    W_in_t, W_f_t, W_c_t, W_o, W_ffn_in_t, W_ffn_out = ([None] * L for _ in range(6))

    saved = []
    xc = x0
    for l in range(L):
        sh1, sc1, g1, sh2, sc2, g2 = (mod[l, k] for k in range(N_MOD))
        gpre1, gpost1, gpre2, gpost2 = (p[l][None, :] for p in (mix_pre_g, mix_post_g, ffn_pre_g, ffn_post_g))
        if l == 0:
            h_gmix[0] = _gather_mid(h_gmix[0], gather_tokens[0], "gather_mid_mix_0")
        g_in, g_o = _gather_wait(h_gmix[l], gather_tokens[0] if l == 0 else xc, f"gather_wait_mix_{l}")
        W_in_t[l], W_o[l] = by_rows(g_in), by_rows(g_o)
        W_f_t[l] = jnp.pad(W_in_t[l][3 * A:3 * A + H], ((0, R - H), (0, 0)))
        W_c_t[l] = W_in_t[l][3 * A + H:]
        if l == 0:
            h1 = _prenorm(xc, gpre1, sc1, sh1, "prenorm1_0", first=gather_tokens)
        qkv = _mm([(h1, W_in_t[l])], "nt", BF16, f"proj_qkv_{l}", tm=MM_ROWS, n_cols=3 * A)
        cproj = _mm([(h1, W_c_t[l])], "nt", F32, f"proj_conv_{l}", tm=MM_ROWS, tn=MM_COLS)
        fl, cum = _fgate_fwd(h1, W_f_t[l], b_f_col[l], f"fgate_{l}")
        cum_rows = cum[:H].reshape(H * nq, tq)
        if l > 0:
            h_gffn[l] = _gather_mid(h_gffn[l], qkv, f"gather_mid_ffn_{l}")
            o, lse = _attn_fwd(qkv, cum_rows, H, f"attn_{l}", tq, first=(h_gffn[l]["token"],))
        else:
            o, lse = _attn_fwd(qkv, cum_rows, H, "attn_0", tq)
            h_gffn[0] = _gather_mid(h_gffn[0], o, "gather_mid_ffn_0")
        u3, u1c = _conv_fwd(cproj, conv_w_pad[l], conv_b[l][None, :], conv_ln_g[l][None, :], conv_ln_b[l][None, :],
                            f"conv_{l}")
        cat = jnp.concatenate([o, u3], axis=-1)
        y1, x_mid, h2 = _mm_postnorm(cat, W_o[l], xc, g1, gpost1, f"out_proj_{l}", nxt=(gpre2, sc2, sh2))
        g_fi, g_fo = _gather_wait(h_gffn[l], x_mid, f"gather_wait_ffn_{l}")
        W_ffn_in_t[l], W_ffn_out[l] = by_rows(g_fi), by_rows(g_fo)
        g, u, act = _ffn_in_fwd(h2, W_ffn_in_t[l], f"ffn_in_{l}")
        if l + 1 < L:
            h_gmix[l + 1] = _gather_mid(h_gmix[l + 1], act, f"gather_mid_mix_{l + 1}")
        saved_l = (xc, h1, qkv, cproj, u1c, fl, cum_rows, o, lse, cat, y1, x_mid, h2, g, u, act)
        if l + 1 < L:
            nxt = (mix_pre_g[l + 1][None, :], mod[l + 1, 1], mod[l + 1, 0])
            y2, xc, h1 = _mm_postnorm(act, W_ffn_out[l], x_mid, g2, gpost2, f"ffn_out_{l}", nxt=nxt,
                                      first=(h_gmix[l + 1]["token"],))
        else:
            y2, xc = _mm_postnorm(act, W_ffn_out[l], x_mid, g2, gpost2, f"ffn_out_{l}")
        saved.append(saved_l + (y2,))

    loss_tile, dx = _loss_head(xc, target, "loss_head")
    loss = lax.psum(loss_tile[0, 0], ("x", "y", "c"))

    def reduce_start(parts, name):
        return _scatter_start(parts, "scatter_" + name)

    small, h_ffn, h_mix = [None] * L, [None] * L, [None] * L
    for l in reversed(range(L)):
        xin, h1, qkv, cproj, u1c, fl, cum_rows, o, lse, cat, y1, x_mid, h2, g, u, act, y2 = saved[l]
        sh1, sc1, g1, sh2, sc2, g2 = (mod[l, k] for k in range(N_MOD))
        gpre1, gpost1, gpre2, gpost2 = (p[l][None, :] for p in (mix_pre_g, mix_post_g, ffn_pre_g, ffn_post_g))
        dy2, sm_post2 = _postnorm_bwd(dx, y2, g2, gpost2, f"postnorm2_bwd_{l}")
        dgate, dup = _ffn_act_bwd(dy2, W_ffn_out[l], g, u, f"ffn_act_bwd_{l}")
        dW_ffn_out = _mm([(act, dy2)], "tn", BF16, f"dw_ffn_out_{l}", tm=DW_TILE)
        dWg_t = _mm([(dgate, h2)], "tn", BF16, f"dw_ffn_gate_{l}", tm=DW_TILE)
        dWu_t = _mm([(dup, h2)], "tn", BF16, f"dw_ffn_up_{l}", tm=DW_TILE)
        half = N_DEV // 2
        dW_ffn_in_t = jnp.concatenate([dWg_t.reshape(half, s_ff, D), dWu_t.reshape(half, s_ff, D)], axis=0)
        h_ffn[l], token = reduce_start([dW_ffn_in_t, dW_ffn_out.reshape(N_DEV, F // N_DEV, D)], f"ffn_{l}")
        dx_mid, sm_pre2 = _prenorm_bwd([(dgate, W_ffn_in_t[l], 0), (dup, W_ffn_in_t[l], 1)], x_mid, dx, gpre2, sc2,
                                       f"prenorm2_bwd_{l}", first=(token,))
        dy1, sm_post1 = _postnorm_bwd(dx_mid, y1, g1, gpost1, f"postnorm1_bwd_{l}")
        dcat = _mm([(dy1, W_o[l])], "nt", F32, f"dcat_{l}", tm=MM_ROWS, tn=MM_COLS)
        dW_o = _mm([(cat, dy1)], "tn", BF16, f"dw_o_{l}", tm=DW_TILE)
        dqkv, dcum_rows, dcum_q = _attn_bwd(qkv, dcat, o, lse, cum_rows, H, f"attn_bwd_{l}", tq)
        dcproj, dconv_w, sm_conv = _conv_bwd(cproj, u1c, dcat, conv_w_pad[l], conv_ln_g[l][None, :],
                                             conv_ln_b[l][None, :], f"conv_bwd_{l}")
        dcum = dcum_rows.reshape(H, T) + jnp.transpose(dcum_q, (0, 2, 1)).reshape(H, T)
        dcum = jnp.pad(dcum, ((0, R - H), (0, 0)))
        dfl_t, dwf_t, dbf = _fgate_bwd(dcum, fl, h1, f"fgate_bwd_{l}")
        dfl = jnp.transpose(dfl_t).astype(BF16)
        dh1_pairs = [(dqkv[0], W_in_t[l], 0), (dqkv[1], W_in_t[l], 1), (dqkv[2], W_in_t[l], 2),
                     (dfl, W_f_t[l]), (dcproj, W_c_t[l])]
        dWq_t = _dw_stack(dqkv, h1, f"dw_qkv_{l}")
        dWc_t = _mm([(dcproj, h1)], "tn", F32, f"dw_conv_{l}", tm=DW_TILE)
        dW_in_t = jnp.concatenate([dWq_t, dwf_t[:H], dWc_t], axis=0).astype(BF16)
        h_mix[l], token = reduce_start([dW_in_t.reshape(N_DEV, s_in, D), dW_o.reshape(N_DEV, D // N_DEV, D)], f"mix_{l}")
        dx, sm_pre1 = _prenorm_bwd(dh1_pairs, xin, dx_mid, gpre1, sc1, f"prenorm1_bwd_{l}", first=(token,))
        dmod = jnp.stack([sm_pre1[0], sm_pre1[1], sm_post1[0], sm_pre2[0], sm_pre2[1], sm_post2[0]])
        small[l] = (dmod, sm_pre1[2], sm_post1[1], sm_pre2[2], sm_post2[1], sm_conv[0], sm_conv[1], sm_conv[2],
                    dbf[:H, 0], dconv_w[:CONV_K])
    grad_x = dx[None]

    small_names = 10
    small_l = [jnp.stack([small[l][k] for l in range(L)]) for k in range(small_names)]
    small_shapes = [a.shape for a in small_l]
    packed = _pack(small_l, D)
    rows = packed.shape[1]
    rows_pad = -(-L * rows // 8) * 8
    packed2 = jnp.pad(packed.reshape(L * rows, D), ((0, rows_pad - L * rows), (0, 0)))
    h_small = _gather_start([packed2], "gather_small")

    got_ffn = [_scatter_wait(h_ffn[l], h_small["token"], f"scatter_wait_ffn_{l}") for l in range(L)]

    def step(w, m, v, parts, name):
        return list(_adamw_summed(w, parts, m, v, "adamw_" + name))

    swap = lambda t: jnp.transpose(t, (0, 2, 1))
    r_w_ffn_in = [swap(t) for t in step(w_ffn_in_t, swap(m_w_ffn_in), swap(v_w_ffn_in),
                                        [got_ffn[l][0] for l in range(L)], "w_ffn_in")]
    r_w_ffn_out = step(w_ffn_out, m_w_ffn_out, v_w_ffn_out, [got_ffn[l][1] for l in range(L)], "w_ffn_out")
    got_mix = [_scatter_wait(h_mix[l], r_w_ffn_out[1], f"scatter_wait_mix_{l}") for l in range(L)]
    r_w_in = [swap(t) for t in step(w_in_t, swap(m_w_in), swap(v_w_in), [got_mix[l][0] for l in range(L)], "w_in")]
    r_w_o = step(w_o, m_w_o, v_w_o, [got_mix[l][1] for l in range(L)], "w_o")

    small_g = _gather_wait(_gather_mid(h_small, r_w_o[1], "gather_small_mid"), r_w_o[1], "gather_small_wait")[0]
    small_sum = _sum_devices(small_g, "sum_small")[:L * rows].reshape(L, rows, D)
    (g_ada_b6, g_mix_pre, g_mix_post, g_ffn_pre, g_ffn_post, g_conv_b, g_ln_g, g_ln_b, g_b_f,
     g_conv_w_full) = _unpack(small_sum, small_shapes, D)
    g_ada_b = g_ada_b6.reshape(L, N_MOD * D)
    g_conv_w = lax.dynamic_slice_in_dim(g_conv_w_full, me * cs, cs, axis=2)
    dmod_all = small_g[:, :L * rows].reshape(N_DEV, L, rows, D)[:, :, :N_MOD].reshape(N_DEV, L, N_MOD * D)
    dmod_loc = jnp.transpose(lax.dynamic_slice_in_dim(dmod_all, me * s_ada, s_ada, axis=2), (1, 0, 2))
    g_ada_w = _ada_bwd(jnp.transpose(c_all), dmod_loc, "ada_bwd")
    r_ada_w = [t.reshape(ada_w.shape) for t in _adamw(
        ada_w.reshape(L * D, s_ada), g_ada_w.reshape(L * D, s_ada), m_ada_w.reshape(L * D, s_ada),
        v_ada_w.reshape(L * D, s_ada), "adamw_ada_w", False)]

    sw = [b_f, conv_w, conv_b, conv_ln_g, conv_ln_b, mix_pre_g, mix_post_g, ffn_pre_g, ffn_post_g, ada_b]
    sg = [g_b_f, g_conv_w, g_conv_b, g_ln_g, g_ln_b, g_mix_pre, g_mix_post, g_ffn_pre, g_ffn_post, g_ada_b]
    sm = [m_b_f, m_conv_w, m_conv_b, m_conv_ln_g, m_conv_ln_b, m_mix_pre_g, m_mix_post_g, m_ffn_pre_g, m_ffn_post_g, m_ada_b]
    sv = [v_b_f, v_conv_w, v_conv_b, v_conv_ln_g, v_conv_ln_b, v_mix_pre_g, v_mix_post_g, v_ffn_pre_g, v_ffn_post_g, v_ada_b]
    shapes = [a.shape for a in sw]

    def flat(arrs):
        p = _pack(arrs, D)
        n = p.shape[0] * p.shape[1]
        return jnp.pad(p.reshape(n, D), ((0, -(-n // 8) * 8 - n), (0, 0))), p.shape

    pw, pshape = flat(sw)
    pg, pm, pv = flat(sg)[0], flat(sm)[0], flat(sv)[0]
    s_outs = _adamw(pw, pg, pm, pv, "adamw_small", False)
    n_small = pshape[0] * pshape[1]
    s_g, s_d, s_m, s_v = (_unpack(t[:n_small].reshape(pshape), shapes, D) for t in s_outs)

    big = {"w_in": r_w_in, "w_o": r_w_o, "w_ffn_in": r_w_ffn_in, "w_ffn_out": r_w_ffn_out, "ada_w": r_ada_w}
    order = ["w_in", "b_f", "conv_w", "conv_b", "conv_ln_g", "conv_ln_b", "w_o", "w_ffn_in", "w_ffn_out",
             "mix_pre_g", "mix_post_g", "ffn_pre_g", "ffn_post_g", "ada_w", "ada_b"]
    small_pos = {n: i for i, n in enumerate(["b_f", "conv_w", "conv_b", "conv_ln_g", "conv_ln_b", "mix_pre_g",
                                             "mix_post_g", "ffn_pre_g", "ffn_post_g", "ada_b"])}

    def pick(n, k):
        if n in big:
            return big[n][k]
        return (s_g, s_d, s_m, s_v)[k][small_pos[n]]

    return (loss, grad_x, *[pick(n, 0) for n in order], *[pick(n, 1) for n in order],
            *[pick(n, 2) for n in order], *[pick(n, 3) for n in order])
```
